```python
import jax, jax.numpy as jnp
from jax import lax
import numpy as np

D_MODEL = 1024
BATCH = 8
SEQ = 8192
DEPTH = 4

MLA_HEADS = 8
MLA_Q_LORA = 256
MLA_KV_LORA = 128
MLA_NOPE = 64
MLA_ROPE = 32
MLA_V = 64
SWA_HEADS = 8
SWA_KV_HEADS = 2
SWA_HEAD_DIM = 64
SWA_WINDOW = 128
SB_HEADS = 8
SB_HEAD_DIM = 64
D_FF = 4 * D_MODEL
BLOCK = 128
ROPE_THETA = 10000.0
EPS = 1e-6
N_BRANCHES = 3
MIX_A = MLA_HEADS * MLA_V
MIX_B = SWA_HEADS * SWA_HEAD_DIM
MIX_C = SB_HEADS * SB_HEAD_DIM

SPLIT_SIZES = (
    MLA_Q_LORA, MLA_KV_LORA, MLA_ROPE,
    SWA_HEADS * SWA_HEAD_DIM, SWA_KV_HEADS * SWA_HEAD_DIM, SWA_KV_HEADS * SWA_HEAD_DIM,
    SB_HEADS * SB_HEAD_DIM, SB_HEADS * SB_HEAD_DIM, SB_HEADS * SB_HEAD_DIM,
    N_BRANCHES * D_MODEL,
)
IN_WIDTH = sum(SPLIT_SIZES)
SPLIT_POINTS = [int(v) for v in np.cumsum(SPLIT_SIZES)[:-1]]

kernel_name = "hybrid_mla_swa_sinks_stickbreak_gated"


def rms_norm(x, g):
    xf = x.astype(jnp.float32)
    y = xf * lax.rsqrt(jnp.mean(xf * xf, axis=-1, keepdims=True) + EPS)
    return (y * g.astype(jnp.float32)).astype(x.dtype)


def rope(x, positions):
    d = x.shape[-1]
    inv = 1.0 / (ROPE_THETA ** (jnp.arange(0, d, 2, dtype=jnp.float32) / d))
    ang = positions.astype(jnp.float32)[..., None] * inv
    cos = jnp.cos(ang)[:, :, None, :]
    sin = jnp.sin(ang)[:, :, None, :]
    xf = x.astype(jnp.float32)
    x1, x2 = xf[..., : d // 2], xf[..., d // 2:]
    return jnp.concatenate([x1 * cos - x2 * sin, x2 * cos + x1 * sin], axis=-1).astype(x.dtype)


def to_blocks(t):
    b, s = t.shape[:2]
    return jnp.moveaxis(t.reshape(b, s // BLOCK, BLOCK, *t.shape[2:]), 1, 0)


def from_blocks(t):
    t = jnp.moveaxis(t, 0, 1)
    return t.reshape(t.shape[0], t.shape[1] * t.shape[2], *t.shape[3:])


def mla_branch(c_q, c_kv, k_rope, positions, g_q_lat, g_kv_lat, w_uq, w_ukv):
    B, S, _ = c_q.shape
    q = (rms_norm(c_q, g_q_lat) @ w_uq).reshape(B, S, MLA_HEADS, MLA_NOPE + MLA_ROPE)
    q_nope = q[..., :MLA_NOPE]
    q_pe = rope(q[..., MLA_NOPE:], positions)
    kv = (rms_norm(c_kv, g_kv_lat) @ w_ukv).reshape(B, S, MLA_HEADS, MLA_NOPE + MLA_V)
    k_nope, v = kv[..., :MLA_NOPE], kv[..., MLA_NOPE:]
    k_pe = rope(k_rope[:, :, None, :], positions)[:, :, 0]
    scale = (MLA_NOPE + MLA_ROPE) ** -0.5
    key_pos = jnp.arange(S)

    def block(args):
        qn, qp, i = args
        s = (jnp.einsum('bqhd,bkhd->bhqk', qn, k_nope, preferred_element_type=jnp.float32)
             + jnp.einsum('bqhd,bkd->bhqk', qp, k_pe, preferred_element_type=jnp.float32)) * scale
        q_pos = i * BLOCK + jnp.arange(BLOCK)
        s = jnp.where(key_pos[None, :] <= q_pos[:, None], s, -jnp.inf)
        p = jax.nn.softmax(s, axis=-1).astype(v.dtype)
        return jnp.einsum('bhqk,bkhd->bqhd', p, v)

    out = lax.map(block, (to_blocks(q_nope), to_blocks(q_pe), jnp.arange(S // BLOCK)))
    return from_blocks(out).reshape(B, S, MIX_A)


def swa_branch(q, k, v, positions, sinks):
    B, S, _ = q.shape
    G = SWA_HEADS // SWA_KV_HEADS
    n = S // BLOCK
    q = rope(q.reshape(B, S, SWA_HEADS, SWA_HEAD_DIM), positions)
    k = rope(k.reshape(B, S, SWA_KV_HEADS, SWA_HEAD_DIM), positions)
    v = v.reshape(B, S, SWA_KV_HEADS, SWA_HEAD_DIM)
    qb = q.reshape(B, n, BLOCK, SWA_KV_HEADS, G, SWA_HEAD_DIM)

    def band(t):
        tb = t.reshape(B, n, BLOCK, *t.shape[2:])
        prev = jnp.pad(tb, ((0, 0), (1, 0), (0, 0), (0, 0), (0, 0)))[:, :-1]
        return jnp.concatenate([prev, tb], axis=2)

    kb, vb = band(k), band(v)
    s = jnp.einsum('bnqkgd,bnskd->bnkgqs', qb, kb, preferred_element_type=jnp.float32) * SWA_HEAD_DIM ** -0.5
    qi = jnp.arange(BLOCK)[:, None] + BLOCK
    si = jnp.arange(2 * BLOCK)[None, :]
    diff = qi - si
    valid = (diff >= 0) & (diff < SWA_WINDOW)
    blk = jnp.arange(n)
    valid = valid[None] & ((blk[:, None, None] > 0) | (si[None] >= BLOCK))
    s = jnp.where(valid[None, :, None, None], s, -jnp.inf)
    sink = sinks.astype(jnp.float32).reshape(SWA_KV_HEADS, G)[None, None, :, :, None, None]
    m = jnp.maximum(jnp.max(s, axis=-1, keepdims=True), sink)
    p = jnp.exp(s - m)
    p = (p / (jnp.sum(p, axis=-1, keepdims=True) + jnp.exp(sink - m))).astype(v.dtype)
    o = jnp.einsum('bnkgqs,bnskd->bnqkgd', p, vb)
    return o.reshape(B, S, MIX_B)


def stick_breaking_branch(q, k, v):
    B, S, _ = q.shape
    q = q.reshape(B, S, SB_HEADS, SB_HEAD_DIM)
    k = k.reshape(B, S, SB_HEADS, SB_HEAD_DIM)
    v = v.reshape(B, S, SB_HEADS, SB_HEAD_DIM)
    scale = SB_HEAD_DIM ** -0.5
    key_pos = jnp.arange(S)

    def block(args):
        qb, i = args
        z = jnp.einsum('bqhd,bkhd->bhqk', qb, k, preferred_element_type=jnp.float32) * scale
        q_pos = i * BLOCK + jnp.arange(BLOCK)
        before = key_pos[None, :] < q_pos[:, None]
        log_1m_beta = jnp.where(before, jax.nn.log_sigmoid(-z), 0.0)
        tail = lax.cumsum(log_1m_beta, axis=3, reverse=True) - log_1m_beta
        a = jnp.where(before, jnp.exp(jax.nn.log_sigmoid(z) + tail), 0.0).astype(v.dtype)
        return jnp.einsum('bhqk,bkhd->bqhd', a, v)

    out = lax.map(block, (to_blocks(q), jnp.arange(S // BLOCK)))
    return from_blocks(out).reshape(B, S, MIX_C)


def hybrid_layer(x, positions, g_mix_pre, w_in, b_gate, g_q_lat, g_kv_lat, w_uq, w_ukv, swa_sinks,
                 w_o_mla, w_o_swa, w_o_sb, w_out, g_mix_post, g_mlp_pre, w_up, w_down, g_mlp_post):
    B, S, D = x.shape
    h = rms_norm(x, g_mix_pre)
    proj = h @ w_in
    (c_q, c_kv, k_rope, q_swa, k_swa, v_swa, q_sb, k_sb, v_sb, gate_logits) = jnp.split(proj, SPLIT_POINTS, axis=-1)
    o_a = mla_branch(c_q, c_kv, k_rope, positions, g_q_lat, g_kv_lat, w_uq, w_ukv) @ w_o_mla
    o_b = swa_branch(q_swa, k_swa, v_swa, positions, swa_sinks) @ w_o_swa
    o_c = stick_breaking_branch(q_sb, k_sb, v_sb) @ w_o_sb
    gates = jax.nn.sigmoid((gate_logits + b_gate).astype(jnp.float32)).astype(x.dtype)
    gates = gates.reshape(B, S, N_BRANCHES, D)
    mixed = gates[:, :, 0] * o_a + gates[:, :, 1] * o_b + gates[:, :, 2] * o_c
    x = x + rms_norm(mixed @ w_out, g_mix_post)
    h = rms_norm(x, g_mlp_pre)
    u = jnp.square(jax.nn.relu(h @ w_up))
    return x + rms_norm(u @ w_down, g_mlp_post)


def _fwd_setup_inputs(seed: int = 0) -> dict:
    key = jax.random.key(seed)
    ks = jax.random.split(key, 24)
    f32 = jnp.float32

    def dense(k, fan_in, fan_out):
        return jax.random.normal(k, (DEPTH, fan_in, fan_out), f32) * fan_in ** -0.5

    def gain(k, n):
        return 1.0 + 0.05 * jax.random.normal(k, (DEPTH, n), f32)

    x = jax.random.normal(ks[0], (BATCH, SEQ, D_MODEL), f32)
    start = jax.random.randint(ks[1], (BATCH, 1), 0, 1024, dtype=jnp.int32)
    positions = start + jnp.arange(SEQ, dtype=jnp.int32)[None, :]
    return {
        "x": x,
        "positions": positions,
        "g_mix_pre": gain(ks[2], D_MODEL),
        "w_in": dense(ks[3], D_MODEL, IN_WIDTH),
        "b_gate": 0.02 * jax.random.normal(ks[4], (DEPTH, N_BRANCHES * D_MODEL), f32),
        "g_q_lat": gain(ks[5], MLA_Q_LORA),
        "g_kv_lat": gain(ks[6], MLA_KV_LORA),
        "w_uq": dense(ks[7], MLA_Q_LORA, MLA_HEADS * (MLA_NOPE + MLA_ROPE)),
        "w_ukv": dense(ks[8], MLA_KV_LORA, MLA_HEADS * (MLA_NOPE + MLA_V)),
        "swa_sinks": 0.5 * jax.random.normal(ks[9], (DEPTH, SWA_HEADS), f32),
        "w_o_mla": dense(ks[10], MIX_A, D_MODEL),
        "w_o_swa": dense(ks[11], MIX_B, D_MODEL),
        "w_o_sb": dense(ks[12], MIX_C, D_MODEL),
        "w_out": dense(ks[13], D_MODEL, D_MODEL),
        "g_mix_post": gain(ks[14], D_MODEL),
        "g_mlp_pre": gain(ks[15], D_MODEL),
        "w_up": dense(ks[16], D_MODEL, D_FF),
        "w_down": dense(ks[17], D_FF, D_MODEL),
        "g_mlp_post": gain(ks[18], D_MODEL),
    }


def _fwd_reference(x, positions, g_mix_pre, w_in, b_gate, g_q_lat, g_kv_lat, w_uq, w_ukv, swa_sinks,
              w_o_mla, w_o_swa, w_o_sb, w_out, g_mix_post, g_mlp_pre, w_up, w_down, g_mlp_post):
    for l in range(DEPTH):
        x = hybrid_layer(x, positions, g_mix_pre[l], w_in[l], b_gate[l], g_q_lat[l], g_kv_lat[l],
                         w_uq[l], w_ukv[l], swa_sinks[l], w_o_mla[l], w_o_swa[l], w_o_sb[l], w_out[l],
                         g_mix_post[l], g_mlp_pre[l], w_up[l], w_down[l], g_mlp_post[l])
    return x


import jax as _jax
import jax.numpy as _jnp

TWIN_FORMAT = 'train_step'
FWD_PARAMS = ['x', 'positions', 'g_mix_pre', 'w_in', 'b_gate', 'g_q_lat', 'g_kv_lat', 'w_uq', 'w_ukv', 'swa_sinks', 'w_o_mla', 'w_o_swa', 'w_o_sb', 'w_out', 'g_mix_post', 'g_mlp_pre', 'w_up', 'w_down', 'g_mlp_post']
TWIN_WEIGHTS = ['g_mix_pre', 'w_in', 'b_gate', 'g_q_lat', 'g_kv_lat', 'w_uq', 'w_ukv', 'swa_sinks', 'w_o_mla', 'w_o_swa', 'w_o_sb', 'w_out', 'g_mix_post', 'g_mlp_pre', 'w_up', 'w_down', 'g_mlp_post']
TWIN_DIFF_INPUT = 'x'
TWIN_INPUTS = ['x', 'positions', 'g_mix_pre', 'w_in', 'b_gate', 'g_q_lat', 'g_kv_lat', 'w_uq', 'w_ukv', 'swa_sinks', 'w_o_mla', 'w_o_swa', 'w_o_sb', 'w_out', 'g_mix_post', 'g_mlp_pre', 'w_up', 'w_down', 'g_mlp_post', 'loss_target', 'm_g_mix_pre', 'm_w_in', 'm_b_gate', 'm_g_q_lat', 'm_g_kv_lat', 'm_w_uq', 'm_w_ukv', 'm_swa_sinks', 'm_w_o_mla', 'm_w_o_swa', 'm_w_o_sb', 'm_w_out', 'm_g_mix_post', 'm_g_mlp_pre', 'm_w_up', 'm_w_down', 'm_g_mlp_post', 'v_g_mix_pre', 'v_w_in', 'v_b_gate', 'v_g_q_lat', 'v_g_kv_lat', 'v_w_uq', 'v_w_ukv', 'v_swa_sinks', 'v_w_o_mla', 'v_w_o_swa', 'v_w_o_sb', 'v_w_out', 'v_g_mix_post', 'v_g_mlp_pre', 'v_w_up', 'v_w_down', 'v_g_mlp_post']
TWIN_OUTPUTS = ['loss', 'grad_x', 'grad_g_mix_pre', 'grad_w_in', 'grad_b_gate', 'grad_g_q_lat', 'grad_g_kv_lat', 'grad_w_uq', 'grad_w_ukv', 'grad_swa_sinks', 'grad_w_o_mla', 'grad_w_o_swa', 'grad_w_o_sb', 'grad_w_out', 'grad_g_mix_post', 'grad_g_mlp_pre', 'grad_w_up', 'grad_w_down', 'grad_g_mlp_post', 'delta_g_mix_pre', 'delta_w_in', 'delta_b_gate', 'delta_g_q_lat', 'delta_g_kv_lat', 'delta_w_uq', 'delta_w_ukv', 'delta_swa_sinks', 'delta_w_o_mla', 'delta_w_o_swa', 'delta_w_o_sb', 'delta_w_out', 'delta_g_mix_post', 'delta_g_mlp_pre', 'delta_w_up', 'delta_w_down', 'delta_g_mlp_post', 'new_m_g_mix_pre', 'new_m_w_in', 'new_m_b_gate', 'new_m_g_q_lat', 'new_m_g_kv_lat', 'new_m_w_uq', 'new_m_w_ukv', 'new_m_swa_sinks', 'new_m_w_o_mla', 'new_m_w_o_swa', 'new_m_w_o_sb', 'new_m_w_out', 'new_m_g_mix_post', 'new_m_g_mlp_pre', 'new_m_w_up', 'new_m_w_down', 'new_m_g_mlp_post', 'new_v_g_mix_pre', 'new_v_w_in', 'new_v_b_gate', 'new_v_g_q_lat', 'new_v_g_kv_lat', 'new_v_w_uq', 'new_v_w_ukv', 'new_v_swa_sinks', 'new_v_w_o_mla', 'new_v_w_o_swa', 'new_v_w_o_sb', 'new_v_w_out', 'new_v_g_mix_post', 'new_v_g_mlp_pre', 'new_v_w_up', 'new_v_w_down', 'new_v_g_mlp_post']
TWIN_LEAF_KINDS = {'loss': 'loss', 'grad_x': 'grad_x', 'grad_g_mix_pre': 'grad_w', 'grad_w_in': 'grad_w', 'grad_b_gate': 'grad_w', 'grad_g_q_lat': 'grad_w', 'grad_g_kv_lat': 'grad_w', 'grad_w_uq': 'grad_w', 'grad_w_ukv': 'grad_w', 'grad_swa_sinks': 'grad_w', 'grad_w_o_mla': 'grad_w', 'grad_w_o_swa': 'grad_w', 'grad_w_o_sb': 'grad_w', 'grad_w_out': 'grad_w', 'grad_g_mix_post': 'grad_w', 'grad_g_mlp_pre': 'grad_w', 'grad_w_up': 'grad_w', 'grad_w_down': 'grad_w', 'grad_g_mlp_post': 'grad_w', 'delta_g_mix_pre': 'delta_w', 'delta_w_in': 'delta_w', 'delta_b_gate': 'delta_w', 'delta_g_q_lat': 'delta_w', 'delta_g_kv_lat': 'delta_w', 'delta_w_uq': 'delta_w', 'delta_w_ukv': 'delta_w', 'delta_swa_sinks': 'delta_w', 'delta_w_o_mla': 'delta_w', 'delta_w_o_swa': 'delta_w', 'delta_w_o_sb': 'delta_w', 'delta_w_out': 'delta_w', 'delta_g_mix_post': 'delta_w', 'delta_g_mlp_pre': 'delta_w', 'delta_w_up': 'delta_w', 'delta_w_down': 'delta_w', 'delta_g_mlp_post': 'delta_w', 'new_m_g_mix_pre': 'new_m', 'new_m_w_in': 'new_m', 'new_m_b_gate': 'new_m', 'new_m_g_q_lat': 'new_m', 'new_m_g_kv_lat': 'new_m', 'new_m_w_uq': 'new_m', 'new_m_w_ukv': 'new_m', 'new_m_swa_sinks': 'new_m', 'new_m_w_o_mla': 'new_m', 'new_m_w_o_swa': 'new_m', 'new_m_w_o_sb': 'new_m', 'new_m_w_out': 'new_m', 'new_m_g_mix_post': 'new_m', 'new_m_g_mlp_pre': 'new_m', 'new_m_w_up': 'new_m', 'new_m_w_down': 'new_m', 'new_m_g_mlp_post': 'new_m', 'new_v_g_mix_pre': 'new_v', 'new_v_w_in': 'new_v', 'new_v_b_gate': 'new_v', 'new_v_g_q_lat': 'new_v', 'new_v_g_kv_lat': 'new_v', 'new_v_w_uq': 'new_v', 'new_v_w_ukv': 'new_v', 'new_v_swa_sinks': 'new_v', 'new_v_w_o_mla': 'new_v', 'new_v_w_o_swa': 'new_v', 'new_v_w_o_sb': 'new_v', 'new_v_w_out': 'new_v', 'new_v_g_mix_post': 'new_v', 'new_v_g_mlp_pre': 'new_v', 'new_v_w_up': 'new_v', 'new_v_w_down': 'new_v', 'new_v_g_mlp_post': 'new_v'}


def _forward(args):
    return _fwd_reference(*[args[k] for k in FWD_PARAMS])


def _output_shape():
    def fwd():
        inp = _fwd_setup_inputs(0)
        return _fwd_reference(*[inp[k] for k in FWD_PARAMS])
    out = _jax.eval_shape(fwd)
    return out.shape, out.dtype

N_MICROBATCH = 1
ADAM_LR = 0.001
ADAM_B1 = 0.9
ADAM_B2 = 0.999
ADAM_EPS = 1e-08
ADAM_WD = 0.01
ADAM_STEP = 10
PER_EXAMPLE_BATCH_AXIS = {'x': 0, 'positions': 0, 'loss_target': 0}
SHARED_INPUTS = []
_WEIGHT_DTYPES = {'g_mix_pre': _jnp.float32, 'w_in': _jnp.float32, 'b_gate': _jnp.float32, 'g_q_lat': _jnp.float32, 'g_kv_lat': _jnp.float32, 'w_uq': _jnp.float32, 'w_ukv': _jnp.float32, 'swa_sinks': _jnp.float32, 'w_o_mla': _jnp.float32, 'w_o_swa': _jnp.float32, 'w_o_sb': _jnp.float32, 'w_out': _jnp.float32, 'g_mix_post': _jnp.float32, 'g_mlp_pre': _jnp.float32, 'w_up': _jnp.float32, 'w_down': _jnp.float32, 'g_mlp_post': _jnp.float32}
MOMENT_SCALE = {'g_mix_pre': 4.937881e+01, 'w_in': 2.141244e+01, 'b_gate': 1.154014e+01, 'g_q_lat': 3.541081e+00, 'g_kv_lat': 8.346782e+01, 'w_uq': 2.173632e+00, 'w_ukv': 2.909128e+01, 'swa_sinks': 1.496379e+00, 'w_o_mla': 2.932108e+01, 'w_o_swa': 2.621158e+01, 'w_o_sb': 2.965815e+01, 'w_out': 4.886521e+01, 'g_mix_post': 8.614041e+01, 'g_mlp_pre': 2.581310e+01, 'w_up': 1.311184e+01, 'w_down': 5.814612e+01, 'g_mlp_post': 9.002266e+01}


def _to_microbatches(a, axis):
    t = _jnp.moveaxis(a, axis, 0)
    t = t.reshape((N_MICROBATCH, t.shape[0] // N_MICROBATCH) + t.shape[1:])
    return _jnp.moveaxis(t, 1, axis + 1)


def setup_inputs(seed: int = 0) -> dict:
    inp = _fwd_setup_inputs(seed)
    key = _jax.random.fold_in(_jax.random.key(seed), 7919)
    shape, _ = _output_shape()
    out = dict(inp)
    out["loss_target"] = _jax.random.normal(_jax.random.fold_in(key, 0), shape, _jnp.float32)
    for i, name in enumerate(TWIN_WEIGHTS):
        w = inp[name].astype(_jnp.float32)
        if MOMENT_SCALE is None:
            s = _jnp.sqrt(_jnp.mean(_jnp.square(w)) + 1e-30)
        else:
            s = MOMENT_SCALE[name]
        km, kv = _jax.random.split(_jax.random.fold_in(key, i + 1))
        out[name] = w
        out["m_" + name] = s * _jax.random.normal(km, w.shape, _jnp.float32)
        out["v_" + name] = (s * s) * _jax.random.uniform(kv, w.shape, _jnp.float32, 0.5, 1.5)
    if N_MICROBATCH > 1:
        for name, axis in PER_EXAMPLE_BATCH_AXIS.items():
            out[name] = _to_microbatches(out[name], axis)
    return {'x': out['x'], 'positions': out['positions'], 'g_mix_pre': out['g_mix_pre'], 'w_in': out['w_in'], 'b_gate': out['b_gate'], 'g_q_lat': out['g_q_lat'], 'g_kv_lat': out['g_kv_lat'], 'w_uq': out['w_uq'], 'w_ukv': out['w_ukv'], 'swa_sinks': out['swa_sinks'], 'w_o_mla': out['w_o_mla'], 'w_o_swa': out['w_o_swa'], 'w_o_sb': out['w_o_sb'], 'w_out': out['w_out'], 'g_mix_post': out['g_mix_post'], 'g_mlp_pre': out['g_mlp_pre'], 'w_up': out['w_up'], 'w_down': out['w_down'], 'g_mlp_post': out['g_mlp_post'], 'loss_target': out['loss_target'], 'm_g_mix_pre': out['m_g_mix_pre'], 'm_w_in': out['m_w_in'], 'm_b_gate': out['m_b_gate'], 'm_g_q_lat': out['m_g_q_lat'], 'm_g_kv_lat': out['m_g_kv_lat'], 'm_w_uq': out['m_w_uq'], 'm_w_ukv': out['m_w_ukv'], 'm_swa_sinks': out['m_swa_sinks'], 'm_w_o_mla': out['m_w_o_mla'], 'm_w_o_swa': out['m_w_o_swa'], 'm_w_o_sb': out['m_w_o_sb'], 'm_w_out': out['m_w_out'], 'm_g_mix_post': out['m_g_mix_post'], 'm_g_mlp_pre': out['m_g_mlp_pre'], 'm_w_up': out['m_w_up'], 'm_w_down': out['m_w_down'], 'm_g_mlp_post': out['m_g_mlp_post'], 'v_g_mix_pre': out['v_g_mix_pre'], 'v_w_in': out['v_w_in'], 'v_b_gate': out['v_b_gate'], 'v_g_q_lat': out['v_g_q_lat'], 'v_g_kv_lat': out['v_g_kv_lat'], 'v_w_uq': out['v_w_uq'], 'v_w_ukv': out['v_w_ukv'], 'v_swa_sinks': out['v_swa_sinks'], 'v_w_o_mla': out['v_w_o_mla'], 'v_w_o_swa': out['v_w_o_swa'], 'v_w_o_sb': out['v_w_o_sb'], 'v_w_out': out['v_w_out'], 'v_g_mix_post': out['v_g_mix_post'], 'v_g_mlp_pre': out['v_g_mlp_pre'], 'v_w_up': out['v_w_up'], 'v_w_down': out['v_w_down'], 'v_g_mlp_post': out['v_g_mlp_post']}


def _loss(weights, diff, rest, loss_target):
    with _jax.named_scope("forward"):
        args = {**rest, TWIN_DIFF_INPUT: diff, **{k: w.astype(_WEIGHT_DTYPES[k]) for k, w in weights.items()}}
        y = _forward(args)
    with _jax.named_scope("loss_head"):
        err = _jnp.square(y.astype(_jnp.float32) - loss_target)
        return 0.5 * _jnp.sum(_jnp.mean(err, axis=-1)) if err.ndim else 0.5 * err


def _adamw(w, g, m, v):
    m = ADAM_B1 * m + (1.0 - ADAM_B1) * g
    v = ADAM_B2 * v + (1.0 - ADAM_B2) * _jnp.square(g)
    m_hat = m / (1.0 - ADAM_B1 ** ADAM_STEP)
    v_hat = v / (1.0 - ADAM_B2 ** ADAM_STEP)
    delta = -ADAM_LR * (m_hat / (_jnp.sqrt(v_hat) + ADAM_EPS) + ADAM_WD * w)
    return delta, m, v


def reference(x, positions, g_mix_pre, w_in, b_gate, g_q_lat, g_kv_lat, w_uq, w_ukv, swa_sinks, w_o_mla, w_o_swa, w_o_sb, w_out, g_mix_post, g_mlp_pre, w_up, w_down, g_mlp_post, loss_target, m_g_mix_pre, m_w_in, m_b_gate, m_g_q_lat, m_g_kv_lat, m_w_uq, m_w_ukv, m_swa_sinks, m_w_o_mla, m_w_o_swa, m_w_o_sb, m_w_out, m_g_mix_post, m_g_mlp_pre, m_w_up, m_w_down, m_g_mlp_post, v_g_mix_pre, v_w_in, v_b_gate, v_g_q_lat, v_g_kv_lat, v_w_uq, v_w_ukv, v_swa_sinks, v_w_o_mla, v_w_o_swa, v_w_o_sb, v_w_out, v_g_mix_post, v_g_mlp_pre, v_w_up, v_w_down, v_g_mlp_post):
    given = dict(x=x, positions=positions, g_mix_pre=g_mix_pre, w_in=w_in, b_gate=b_gate, g_q_lat=g_q_lat, g_kv_lat=g_kv_lat, w_uq=w_uq, w_ukv=w_ukv, swa_sinks=swa_sinks, w_o_mla=w_o_mla, w_o_swa=w_o_swa, w_o_sb=w_o_sb, w_out=w_out, g_mix_post=g_mix_post, g_mlp_pre=g_mlp_pre, w_up=w_up, w_down=w_down, g_mlp_post=g_mlp_post, loss_target=loss_target, m_g_mix_pre=m_g_mix_pre, m_w_in=m_w_in, m_b_gate=m_b_gate, m_g_q_lat=m_g_q_lat, m_g_kv_lat=m_g_kv_lat, m_w_uq=m_w_uq, m_w_ukv=m_w_ukv, m_swa_sinks=m_swa_sinks, m_w_o_mla=m_w_o_mla, m_w_o_swa=m_w_o_swa, m_w_o_sb=m_w_o_sb, m_w_out=m_w_out, m_g_mix_post=m_g_mix_post, m_g_mlp_pre=m_g_mlp_pre, m_w_up=m_w_up, m_w_down=m_w_down, m_g_mlp_post=m_g_mlp_post, v_g_mix_pre=v_g_mix_pre, v_w_in=v_w_in, v_b_gate=v_b_gate, v_g_q_lat=v_g_q_lat, v_g_kv_lat=v_g_kv_lat, v_w_uq=v_w_uq, v_w_ukv=v_w_ukv, v_swa_sinks=v_swa_sinks, v_w_o_mla=v_w_o_mla, v_w_o_swa=v_w_o_swa, v_w_o_sb=v_w_o_sb, v_w_out=v_w_out, v_g_mix_post=v_g_mix_post, v_g_mlp_pre=v_g_mlp_pre, v_w_up=v_w_up, v_w_down=v_w_down, v_g_mlp_post=v_g_mlp_post)
    weights = {n: given[n] for n in TWIN_WEIGHTS}
    shared = {n: given[n] for n in SHARED_INPUTS}
    per_example = {n: given[n] for n in ['x', 'positions']}
    grad_fn = _jax.value_and_grad(_loss, argnums=(0, 1))

    def one_microbatch(ex, loss_target):
        ex = dict(ex)
        diff = ex.pop(TWIN_DIFF_INPUT)
        return grad_fn(weights, diff, {**shared, **ex}, loss_target)

    if N_MICROBATCH == 1:
        loss, (grad_w, grad_x) = one_microbatch(per_example, given["loss_target"])
    else:
        def body(carry, xs):
            loss_sum, grad_sum = carry
            l_k, (gw_k, gx_k) = one_microbatch(xs[0], xs[1])
            with _jax.named_scope("update"):
                return (loss_sum + l_k, _jax.tree.map(_jnp.add, grad_sum, gw_k)), gx_k

        init = (_jnp.zeros((), _jnp.float32), _jax.tree.map(_jnp.zeros_like, weights))
        (loss, grad_w), grad_x = _jax.lax.scan(body, init, (per_example, given["loss_target"]))
    with _jax.named_scope("update"):
        delta_w, new_m, new_v = {}, {}, {}
        for n in TWIN_WEIGHTS:
            delta_w[n], new_m[n], new_v[n] = _adamw(weights[n], grad_w[n], given["m_" + n], given["v_" + n])
    return (loss, grad_x, *[grad_w[n] for n in TWIN_WEIGHTS], *[delta_w[n] for n in TWIN_WEIGHTS],
            *[new_m[n] for n in TWIN_WEIGHTS], *[new_v[n] for n in TWIN_WEIGHTS])
```

```python
import functools

import numpy as np
import jax
import jax.numpy as jnp
from jax import lax
from jax.experimental import pallas as pl
from jax.experimental.pallas import tpu as pltpu

F32 = jnp.float32
BF16 = jnp.bfloat16

D_MODEL = 1024
DEPTH = 4
MLA_HEADS, MLA_Q_LORA, MLA_KV_LORA, MLA_NOPE, MLA_ROPE, MLA_V = 8, 256, 128, 64, 32, 64
SWA_HEADS, SWA_KV_HEADS, SWA_HEAD_DIM, SWA_WINDOW = 8, 2, 64, 128
SB_HEADS, SB_HEAD_DIM = 8, 64
D_FF = 4 * D_MODEL
ROPE_THETA = 10000.0
EPS = 1e-6
SPLIT_SIZES = (256, 128, 32, 512, 128, 128, 512, 512, 512, 3 * D_MODEL)
SPLIT_POINTS = [int(v) for v in np.cumsum(SPLIT_SIZES)[:-1]]

ADAM_LR, ADAM_B1, ADAM_B2, ADAM_EPS, ADAM_WD, ADAM_STEP = 0.001, 0.9, 0.999, 1e-08, 0.01, 10

LANES = 128
V7X_VMEM_BYTES = 64 * 1024 * 1024
VMEM_LIMIT = V7X_VMEM_BYTES - 8 * 1024 * 1024
N_CHIPS = 4
SLAB_ROW_ALIGN = 512

P1_W = 256 + 128 + 128 + 1024 + 256
P2_W = 256 + 1024 + 1024 + 1024
P3_W = 3 * D_MODEL

SHARDED = ("w_in", "w_uq", "w_ukv", "w_o_mla", "w_o_swa", "w_o_sb", "w_out", "w_up", "w_down")
SHARD_AXIS = {"w_in": 2, "w_uq": 2, "w_ukv": 2, "w_o_mla": 2, "w_o_swa": 2, "w_o_sb": 2, "w_out": 1, "w_up": 2, "w_down": 1}
SMALL = ("g_mix_pre", "b_gate", "g_q_lat", "g_kv_lat", "swa_sinks", "g_mix_post", "g_mlp_pre", "g_mlp_post")
WEIGHTS = ("g_mix_pre", "w_in", "b_gate", "g_q_lat", "g_kv_lat", "w_uq", "w_ukv", "swa_sinks", "w_o_mla", "w_o_swa",
           "w_o_sb", "w_out", "g_mix_post", "g_mlp_pre", "w_up", "w_down", "g_mlp_post")

NN = (((1,), (0,)), ((), ()))
NT = (((1,), (1,)), ((), ()))
TN = (((0,), (0,)), ((), ()))


def _dot(a, b, dims):
    return lax.dot_general(a, b, dims, preferred_element_type=F32)


def _pick(n, prefs=(512, 256, 128)):
    for p in prefs:
        if n % p == 0:
            return p
    return n


def _params(sem):
    return pltpu.CompilerParams(dimension_semantics=sem, vmem_limit_bytes=VMEM_LIMIT)


def _matmul(a, b, mode, out_dtypes, name, epilogue=None, extras=(), row_extras=()):
    if mode == "nn":
        (M, K), (K2, N) = a.shape, b.shape
    elif mode == "nt":
        (M, K), (N, K2) = a.shape, b.shape
    else:
        (K, M), (K2, N) = a.shape, b.shape
    assert K == K2, (name, a.shape, b.shape)
    tm, tn, tk = _pick(M), _pick(N), _pick(K)
    nk = K // tk
    if mode == "tn":
        a_spec = pl.BlockSpec((tk, tm), lambda i, j, k: (k, i))
    else:
        a_spec = pl.BlockSpec((tm, tk), lambda i, j, k: (i, k))
    if mode == "nt":
        b_spec = pl.BlockSpec((tn, tk), lambda i, j, k: (j, k))
    else:
        b_spec = pl.BlockSpec((tk, tn), lambda i, j, k: (k, j))
    dims = {"nn": NN, "nt": NT, "tn": TN}[mode]
    n_ex, n_rex, n_out = len(extras), len(row_extras), len(out_dtypes)

    def body(*refs):
        a_ref, b_ref = refs[:2]
        ex = refs[2:2 + n_ex]
        rex = refs[2 + n_ex:2 + n_ex + n_rex]
        outs = refs[2 + n_ex + n_rex:2 + n_ex + n_rex + n_out]
        acc = refs[-1]
        k = pl.program_id(2)

        @pl.when(k == 0)
        def _():
            acc[...] = jnp.zeros_like(acc)

        acc[...] += _dot(a_ref[...].astype(BF16), b_ref[...].astype(BF16), dims)

        @pl.when(k == nk - 1)
        def _():
            if epilogue is None:
                res = (acc[...],)
            else:
                res = epilogue(acc[...], *[e[...] for e in ex], *[e[...] for e in rex])
            for o, r in zip(outs, res):
                o[...] = r.astype(o.dtype)

    in_specs = [a_spec, b_spec]
    in_specs += [pl.BlockSpec((tm, tn), lambda i, j, k: (i, j)) for _ in extras]
    in_specs += [pl.BlockSpec((1, tn), lambda i, j, k: (0, j)) for _ in row_extras]
    out = pl.pallas_call(
        body,
        name=name,
        grid=(M // tm, N // tn, nk),
        in_specs=in_specs,
        out_specs=[pl.BlockSpec((tm, tn), lambda i, j, k: (i, j)) for _ in out_dtypes],
        out_shape=[jax.ShapeDtypeStruct((M, N), dt) for dt in out_dtypes],
        scratch_shapes=[pltpu.VMEM((tm, tn), F32)],
        compiler_params=_params(("parallel", "parallel", "arbitrary")),
    )(a, b, *extras, *row_extras)
    return out[0] if n_out == 1 else out


def _rowwise(fn, rows, consts, out_defs, sum_widths, name, bm=256):
    R = rows[0].shape[0]
    bm = min(bm, R)
    assert R % bm == 0, (name, R, bm)
    n_r, n_c, n_o, n_s = len(rows), len(consts), len(out_defs), len(sum_widths)

    def body(*refs):
        r_in = refs[:n_r]
        c_in = refs[n_r:n_r + n_c]
        o_refs = refs[n_r + n_c:n_r + n_c + n_o]
        s_refs = refs[n_r + n_c + n_o:]
        outs, sums = fn([r[...] for r in r_in], [c[...] for c in c_in])
        for o, val in zip(o_refs, outs):
            o[...] = val.astype(o.dtype)
        if n_s:
            @pl.when(pl.program_id(0) == 0)
            def _():
                for s in s_refs:
                    s[...] = jnp.zeros_like(s)

            for s, val in zip(s_refs, sums):
                s[...] += jnp.sum(val, axis=0, keepdims=True)

    in_specs = [pl.BlockSpec((bm, r.shape[1]), lambda i: (i, 0)) for r in rows]
    in_specs += [pl.BlockSpec(c.shape, lambda i: (0, 0)) for c in consts]
    out_specs = [pl.BlockSpec((bm, w), lambda i: (i, 0)) for w, _ in out_defs]
    out_specs += [pl.BlockSpec((1, w), lambda i: (0, 0)) for w in sum_widths]
    out_shape = [jax.ShapeDtypeStruct((R, w), dt) for w, dt in out_defs]
    out_shape += [jax.ShapeDtypeStruct((1, w), F32) for w in sum_widths]
    res = pl.pallas_call(
        body,
        name=name,
        grid=(R // bm,),
        in_specs=in_specs,
        out_specs=out_specs,
        out_shape=out_shape,
        compiler_params=_params(("arbitrary",)),
    )(*rows, *consts)
    return res


def _rms(x, g):
    r = lax.rsqrt(jnp.mean(x * x, axis=-1, keepdims=True) + EPS)
    return x * r * g


def _rms_bwd(x, g, dy):
    r = lax.rsqrt(jnp.mean(x * x, axis=-1, keepdims=True) + EPS)
    n = x * r
    dn = dy * g
    dx = r * (dn - n * jnp.mean(dn * n, axis=-1, keepdims=True))
    return dx, dy * n


def _rope(x, c, s_up, s_dn, half):
    return x * c + pltpu.roll(x, half, 1) * s_up + pltpu.roll(x, LANES - half, 1) * s_dn


def _rope_tables(positions, lo, d, nope_pass):
    S = positions.shape[0]
    half = d // 2
    inv = 1.0 / (ROPE_THETA ** (jnp.arange(0, d, 2, dtype=F32) / d))
    ang = positions.astype(F32)[:, None] * inv
    cos, sin = jnp.cos(ang), jnp.sin(ang)
    z = lambda n: jnp.zeros((S, n), F32)
    head = jnp.ones((S, lo), F32) if nope_pass else z(lo)
    tail = LANES - lo - d
    c = jnp.concatenate([head, cos, cos, z(tail)], axis=1)
    s_up = jnp.concatenate([z(lo), z(half), sin, z(tail)], axis=1)
    s_dn = jnp.concatenate([z(lo), -sin, z(half), z(tail)], axis=1)
    return c, s_up, s_dn


ATT_BLOCK = 256


def _causal_mask(i, kb, bq, bk, strict):
    row = i * bq + lax.broadcasted_iota(jnp.int32, (bq, bk), 0)
    col = kb * bk + lax.broadcasted_iota(jnp.int32, (bq, bk), 1)
    return (col < row) if strict else (col <= row)


def _softmax_attn_fwd(q, k, v, heads, scale, name, q_off=0, k_off=0, v_off=0):
    S = q.shape[0]
    bq = bk = min(ATT_BLOCK, S)
    nq = S // bq

    def body(q_ref, k_ref, v_ref, o_ref, lse_ref):
        i = pl.program_id(1)
        qb = q_ref[...]

        def tile(kb, carry, masked):
            m, l, acc = carry
            off = pl.multiple_of(kb * bk, bk)
            ks = k_ref[pl.ds(off, bk), :]
            vs = v_ref[pl.ds(off, bk), :]
            s = _dot(qb, ks, NT) * scale
            if masked:
                s = jnp.where(_causal_mask(i, kb, bq, bk, False), s, -1e30)
            m_new = jnp.maximum(m, jnp.max(s, axis=1, keepdims=True))
            p = jnp.exp(s - m_new)
            alpha = jnp.exp(m - m_new)
            l = alpha * l + jnp.sum(p, axis=1, keepdims=True)
            acc = alpha * acc + _dot(p.astype(BF16), vs, NN)
            return m_new, l, acc

        init = (jnp.full((bq, 1), -1e30, F32), jnp.zeros((bq, 1), F32), jnp.zeros((bq, LANES), F32))
        carry = lax.fori_loop(0, i, lambda kb, c: tile(kb, c, False), init)
        m, l, acc = tile(i, carry, True)
        o_ref[...] = (acc / l).astype(o_ref.dtype)
        lse_ref[...] = m + jnp.log(l)

    return pl.pallas_call(
        body,
        name=name,
        grid=(heads, nq),
        in_specs=[
            pl.BlockSpec((bq, LANES), lambda h, i: (i, q_off + h)),
            pl.BlockSpec((S, LANES), lambda h, i: (0, k_off + h)),
            pl.BlockSpec((S, LANES), lambda h, i: (0, v_off + h)),
        ],
        out_specs=[
            pl.BlockSpec((bq, LANES), lambda h, i: (i, h)),
            pl.BlockSpec((None, bq, 1), lambda h, i: (h, i, 0)),
        ],
        out_shape=[jax.ShapeDtypeStruct((S, heads * LANES), BF16), jax.ShapeDtypeStruct((heads, S, 1), F32)],
        compiler_params=_params(("parallel", "arbitrary")),
    )(q, k, v)


def _softmax_attn_bwd(q, k, v, o, lse, do, heads, scale, name, q_off=0, k_off=0, v_off=0):
    S = q.shape[0]
    bq = bk = min(ATT_BLOCK, S)
    nq = S // bq

    def body(q_ref, k_ref, v_ref, o_ref, lse_ref, do_ref, dq_ref, dk_ref, dv_ref):
        i = pl.program_id(1)

        @pl.when(i == 0)
        def _():
            dk_ref[...] = jnp.zeros_like(dk_ref)
            dv_ref[...] = jnp.zeros_like(dv_ref)

        qb = q_ref[...]
        dob = do_ref[...]
        delta = jnp.sum(dob.astype(F32) * o_ref[...].astype(F32), axis=1, keepdims=True)
        lse_b = lse_ref[...]

        def tile(kb, dq, masked):
            off = pl.multiple_of(kb * bk, bk)
            ks = k_ref[pl.ds(off, bk), :]
            vs = v_ref[pl.ds(off, bk), :]
            s = _dot(qb, ks, NT) * scale
            if masked:
                s = jnp.where(_causal_mask(i, kb, bq, bk, False), s, -1e30)
            p = jnp.exp(s - lse_b)
            dv_ref[pl.ds(off, bk), :] += _dot(p.astype(BF16), dob, TN)
            dp = _dot(dob, vs, NT)
            ds = (p * (dp - delta) * scale).astype(BF16)
            dk_ref[pl.ds(off, bk), :] += _dot(ds, qb, TN)
            return dq + _dot(ds, ks, NN)

        dq = lax.fori_loop(0, i, lambda kb, c: tile(kb, c, False), jnp.zeros((bq, LANES), F32))
        dq_ref[...] = tile(i, dq, True)

    return pl.pallas_call(
        body,
        name=name,
        grid=(heads, nq),
        in_specs=[
            pl.BlockSpec((bq, LANES), lambda h, i: (i, q_off + h)),
            pl.BlockSpec((S, LANES), lambda h, i: (0, k_off + h)),
            pl.BlockSpec((S, LANES), lambda h, i: (0, v_off + h)),
            pl.BlockSpec((bq, LANES), lambda h, i: (i, h)),
            pl.BlockSpec((None, bq, 1), lambda h, i: (h, i, 0)),
            pl.BlockSpec((bq, LANES), lambda h, i: (i, h)),
        ],
        out_specs=[
            pl.BlockSpec((bq, LANES), lambda h, i: (i, h)),
            pl.BlockSpec((S, LANES), lambda h, i: (0, h)),
            pl.BlockSpec((S, LANES), lambda h, i: (0, h)),
        ],
        out_shape=[jax.ShapeDtypeStruct((S, heads * LANES), F32)] * 3,
        compiler_params=_params(("parallel", "arbitrary")),
    )(q, k, v, o, lse, do)


def _tri(n, inclusive):
    r = lax.broadcasted_iota(jnp.int32, (n, n), 0)
    c = lax.broadcasted_iota(jnp.int32, (n, n), 1)
    return jnp.where((r >= c) if inclusive else (r > c), 1.0, 0.0).astype(BF16)


def _suffix_sum(x, tri):
    hi = x.astype(BF16)
    lo = (x - hi.astype(F32)).astype(BF16)
    return _dot(hi, tri, NN) + _dot(lo, tri, NN)


def _sb_logs(z):
    lg = jnp.log(1.0 + jnp.exp(-jnp.abs(z)))
    l1m = -(jnp.maximum(z, 0.0) + lg)
    return l1m, l1m + z


def _sb_attn_fwd(qkv, heads, scale, name, q_off, k_off, v_off):
    S = qkv.shape[0]
    bq = bk = min(ATT_BLOCK, S)
    nq = S // bq

    def body(q_ref, k_ref, v_ref, o_ref):
        i = pl.program_id(1)
        qb = q_ref[...]
        tri = _tri(bk, False)

        def tile(kb, carry, masked):
            run, acc = carry
            off = pl.multiple_of(kb * bk, bk)
            ks = k_ref[pl.ds(off, bk), :]
            vs = v_ref[pl.ds(off, bk), :]
            z = _dot(qb, ks, NT) * scale
            l1m, lb = _sb_logs(z)
            if masked:
                valid = _causal_mask(i, kb, bq, bk, True)
                l1m = jnp.where(valid, l1m, 0.0)
            ex = lb + run + _suffix_sum(l1m, tri)
            if masked:
                ex = jnp.where(valid, ex, -1e30)
            a = jnp.exp(ex)
            acc = acc + _dot(a.astype(BF16), vs, NN)
            return run + jnp.sum(l1m, axis=1, keepdims=True), acc

        carry = tile(i, (jnp.zeros((bq, 1), F32), jnp.zeros((bq, LANES), F32)), True)
        _, acc = lax.fori_loop(0, i, lambda t, c: tile(i - 1 - t, c, False), carry)
        o_ref[...] = acc

    return pl.pallas_call(
        body,
        name=name,
        grid=(heads, nq),
        in_specs=[
            pl.BlockSpec((bq, LANES), lambda h, i: (i, q_off + h)),
            pl.BlockSpec((S, LANES), lambda h, i: (0, k_off + h)),
            pl.BlockSpec((S, LANES), lambda h, i: (0, v_off + h)),
        ],
        out_specs=pl.BlockSpec((bq, LANES), lambda h, i: (i, h)),
        out_shape=jax.ShapeDtypeStruct((S, heads * LANES), F32),
        compiler_params=_params(("parallel", "arbitrary")),
    )(qkv, qkv, qkv)


def _sb_attn_bwd(qkv, o, do, heads, scale, name, q_off, k_off, v_off):
    S = qkv.shape[0]
    bq = bk = min(ATT_BLOCK, S)
    nq = S // bq

    def body(q_ref, k_ref, v_ref, o_ref, do_ref, dq_ref, dk_ref, dv_ref):
        i = pl.program_id(1)

        @pl.when(i == 0)
        def _():
            dk_ref[...] = jnp.zeros_like(dk_ref)
            dv_ref[...] = jnp.zeros_like(dv_ref)

        qb = q_ref[...]
        dob = do_ref[...]
        delta = jnp.sum(dob.astype(F32) * o_ref[...], axis=1, keepdims=True)
        tri = _tri(bk, False)
        tri_inc = _tri(bk, True)

        def tile(kb, carry, masked):
            run, grun, dq = carry
            off = pl.multiple_of(kb * bk, bk)
            ks = k_ref[pl.ds(off, bk), :]
            vs = v_ref[pl.ds(off, bk), :]
            z = _dot(qb, ks, NT) * scale
            l1m, lb = _sb_logs(z)
            if masked:
                valid = _causal_mask(i, kb, bq, bk, True)
                l1m = jnp.where(valid, l1m, 0.0)
            ex = lb + run + _suffix_sum(l1m, tri)
            if masked:
                ex = jnp.where(valid, ex, -1e30)
            ab = jnp.exp(ex).astype(BF16)
            dv_ref[pl.ds(off, bk), :] += _dot(ab, dob, TN)
            g = ab.astype(F32) * _dot(dob, vs, NT)
            g_left = delta - (grun + _suffix_sum(g, tri_inc))
            beta = jnp.exp(lb)
            dz = g * (1.0 - beta) - g_left * beta
            if masked:
                dz = jnp.where(valid, dz, 0.0)
            dzb = (dz * scale).astype(BF16)
            dk_ref[pl.ds(off, bk), :] += _dot(dzb, qb, TN)
            dq = dq + _dot(dzb, ks, NN)
            return run + jnp.sum(l1m, axis=1, keepdims=True), grun + jnp.sum(g, axis=1, keepdims=True), dq

        zcol = jnp.zeros((bq, 1), F32)
        carry = tile(i, (zcol, zcol, jnp.zeros((bq, LANES), F32)), True)
        _, _, dq = lax.fori_loop(0, i, lambda t, c: tile(i - 1 - t, c, False), carry)
        dq_ref[...] = dq

    return pl.pallas_call(
        body,
        name=name,
        grid=(heads, nq),
        in_specs=[
            pl.BlockSpec((bq, LANES), lambda h, i: (i, q_off + h)),
            pl.BlockSpec((S, LANES), lambda h, i: (0, k_off + h)),
            pl.BlockSpec((S, LANES), lambda h, i: (0, v_off + h)),
            pl.BlockSpec((bq, LANES), lambda h, i: (i, h)),
            pl.BlockSpec((bq, LANES), lambda h, i: (i, h)),
        ],
        out_specs=[
            pl.BlockSpec((bq, LANES), lambda h, i: (i, h)),
            pl.BlockSpec((S, LANES), lambda h, i: (0, h)),
            pl.BlockSpec((S, LANES), lambda h, i: (0, h)),
        ],
        out_shape=[jax.ShapeDtypeStruct((S, heads * LANES), F32)] * 3,
        compiler_params=_params(("parallel", "arbitrary")),
    )(qkv, qkv, qkv, o, do)


SWA_BLK = 128
SWA_GROUP = SWA_HEADS // SWA_KV_HEADS


def _swa_band_mask(n):
    row = lax.broadcasted_iota(jnp.int32, (SWA_BLK, 2 * SWA_BLK), 0)
    col = lax.broadcasted_iota(jnp.int32, (SWA_BLK, 2 * SWA_BLK), 1)
    return (col > row) & (col <= row + SWA_WINDOW) & ((n > 0) | (col >= SWA_BLK))


def _swa_fwd(q, k, v, v_off, sink_b, name):
    S = q.shape[0]
    nb = S // SWA_BLK
    scale = SWA_HEAD_DIM ** -0.5
    gw = SWA_GROUP * LANES

    def body(q_ref, kp_ref, kc_ref, vp_ref, vc_ref, sink_ref, o_ref, lse_ref):
        n = pl.program_id(1)
        kband = jnp.concatenate([kp_ref[...], kc_ref[...]], axis=0)
        vband = jnp.concatenate([vp_ref[...], vc_ref[...]], axis=0)
        valid = _swa_band_mask(n)
        for g in range(SWA_GROUP):
            lanes = slice(g * LANES, (g + 1) * LANES)
            s = jnp.where(valid, _dot(q_ref[:, lanes], kband, NT) * scale, -1e30)
            sk = sink_ref[:, g * LANES:g * LANES + 1]
            m = jnp.maximum(jnp.max(s, axis=1, keepdims=True), sk)
            p = jnp.exp(s - m)
            den = jnp.sum(p, axis=1, keepdims=True) + jnp.exp(sk - m)
            o_ref[:, lanes] = _dot((p / den).astype(BF16), vband, NN).astype(o_ref.dtype)
            lse_ref[g] = m + jnp.log(den)

    prev = lambda h, n: (jnp.maximum(n - 1, 0), h)
    return pl.pallas_call(
        body,
        name=name,
        grid=(SWA_KV_HEADS, nb),
        in_specs=[
            pl.BlockSpec((SWA_BLK, gw), lambda h, n: (n, h)),
            pl.BlockSpec((SWA_BLK, LANES), prev),
            pl.BlockSpec((SWA_BLK, LANES), lambda h, n: (n, h)),
            pl.BlockSpec((SWA_BLK, LANES), lambda h, n: (jnp.maximum(n - 1, 0), v_off + h)),
            pl.BlockSpec((SWA_BLK, LANES), lambda h, n: (n, v_off + h)),
            pl.BlockSpec((1, gw), lambda h, n: (0, h)),
        ],
        out_specs=[
            pl.BlockSpec((SWA_BLK, gw), lambda h, n: (n, h)),
            pl.BlockSpec((SWA_GROUP, SWA_BLK, 1), lambda h, n: (h, n, 0)),
        ],
        out_shape=[jax.ShapeDtypeStruct((S, SWA_HEADS * LANES), BF16), jax.ShapeDtypeStruct((SWA_HEADS, S, 1), F32)],
        compiler_params=_params(("parallel", "arbitrary")),
    )(q, k, k, v, v, sink_b)


def _swa_bwd(q, k, v, v_off, sink_b, o, lse, do, name):
    S = q.shape[0]
    nb = S // SWA_BLK
    scale = SWA_HEAD_DIM ** -0.5
    gw = SWA_GROUP * LANES

    def body(q_ref, kp_ref, kc_ref, vp_ref, vc_ref, sink_ref, o_ref, lse_ref, do_ref, dq_ref, dk_ref, dv_ref, dsink_ref):
        n = pl.program_id(1)

        @pl.when(n == 0)
        def _():
            dk_ref[...] = jnp.zeros_like(dk_ref)
            dv_ref[...] = jnp.zeros_like(dv_ref)
            dsink_ref[...] = jnp.zeros_like(dsink_ref)

        kband = jnp.concatenate([kp_ref[...], kc_ref[...]], axis=0)
        vband = jnp.concatenate([vp_ref[...], vc_ref[...]], axis=0)
        valid = _swa_band_mask(n)
        dkb = jnp.zeros((2 * SWA_BLK, LANES), F32)
        dvb = jnp.zeros((2 * SWA_BLK, LANES), F32)
        for g in range(SWA_GROUP):
            lanes = slice(g * LANES, (g + 1) * LANES)
            qg = q_ref[:, lanes]
            dog = do_ref[:, lanes]
            delta = jnp.sum(dog.astype(F32) * o_ref[:, lanes].astype(F32), axis=1, keepdims=True)
            s = jnp.where(valid, _dot(qg, kband, NT) * scale, -1e30)
            lse_g = lse_ref[g]
            p = jnp.exp(s - lse_g)
            p_sink = jnp.exp(sink_ref[:, g * LANES:g * LANES + 1] - lse_g)
            dsink_ref[:, lanes] += jnp.zeros((1, LANES), F32) - jnp.sum(p_sink * delta, axis=0, keepdims=True)
            dvb = dvb + _dot(p.astype(BF16), dog, TN)
            ds = (p * (_dot(dog, vband, NT) - delta) * scale).astype(BF16)
            dq_ref[:, lanes] = _dot(ds, kband, NN)
            dkb = dkb + _dot(ds, qg, TN)

        cur = pl.multiple_of(n * SWA_BLK, SWA_BLK)
        dk_ref[pl.ds(cur, SWA_BLK), :] += dkb[SWA_BLK:]
        dv_ref[pl.ds(cur, SWA_BLK), :] += dvb[SWA_BLK:]

        @pl.when(n > 0)
        def _():
            before = pl.multiple_of((n - 1) * SWA_BLK, SWA_BLK)
            dk_ref[pl.ds(before, SWA_BLK), :] += dkb[:SWA_BLK]
            dv_ref[pl.ds(before, SWA_BLK), :] += dvb[:SWA_BLK]

    return pl.pallas_call(
        body,
        name=name,
        grid=(SWA_KV_HEADS, nb),
        in_specs=[
            pl.BlockSpec((SWA_BLK, gw), lambda h, n: (n, h)),
            pl.BlockSpec((SWA_BLK, LANES), lambda h, n: (jnp.maximum(n - 1, 0), h)),
            pl.BlockSpec((SWA_BLK, LANES), lambda h, n: (n, h)),
            pl.BlockSpec((SWA_BLK, LANES), lambda h, n: (jnp.maximum(n - 1, 0), v_off + h)),
            pl.BlockSpec((SWA_BLK, LANES), lambda h, n: (n, v_off + h)),
            pl.BlockSpec((1, gw), lambda h, n: (0, h)),
            pl.BlockSpec((SWA_BLK, gw), lambda h, n: (n, h)),
            pl.BlockSpec((SWA_GROUP, SWA_BLK, 1), lambda h, n: (h, n, 0)),
            pl.BlockSpec((SWA_BLK, gw), lambda h, n: (n, h)),
        ],
        out_specs=[
            pl.BlockSpec((SWA_BLK, gw), lambda h, n: (n, h)),
            pl.BlockSpec((S, LANES), lambda h, n: (0, h)),
            pl.BlockSpec((S, LANES), lambda h, n: (0, h)),
            pl.BlockSpec((1, gw), lambda h, n: (0, h)),
        ],
        out_shape=[
            jax.ShapeDtypeStruct((S, SWA_HEADS * LANES), F32),
            jax.ShapeDtypeStruct((S, SWA_KV_HEADS * LANES), F32),
            jax.ShapeDtypeStruct((S, SWA_KV_HEADS * LANES), F32),
            jax.ShapeDtypeStruct((1, SWA_HEADS * LANES), F32),
        ],
        compiler_params=_params(("parallel", "arbitrary")),
    )(q, k, k, v, v, sink_b, o, lse, do)


def _pad_cols(w, heads, real):
    k = w.shape[0]
    return jnp.pad(w.reshape(k, heads, real), ((0, 0), (0, 0), (0, LANES - real))).reshape(k, heads * LANES)


def _unpad_cols(g, heads, real):
    k = g.shape[0]
    return g.reshape(k, heads, LANES)[:, :, :real].reshape(k, heads * real)


def _pad_rows(w, heads, real):
    n = w.shape[1]
    return jnp.pad(w.reshape(heads, real, n), ((0, 0), (0, LANES - real), (0, 0))).reshape(heads * LANES, n)


def _unpad_rows(g, heads, real):
    n = g.shape[1]
    return g.reshape(heads, LANES, n)[:, :real, :].reshape(heads * real, n)


def _w_in_internal(w_in):
    c_q, c_kv, k_r, q_swa, k_swa, v_swa, q_sb, k_sb, v_sb, gate = jnp.split(w_in, SPLIT_POINTS, axis=1)
    k_r = jnp.pad(k_r, ((0, 0), (MLA_NOPE, LANES - MLA_NOPE - MLA_ROPE)))
    return jnp.concatenate([
        c_q, c_kv, k_r, _pad_cols(q_swa, 8, 64), _pad_cols(k_swa, 2, 64),
        _pad_cols(v_swa, 2, 64), _pad_cols(q_sb, 8, 64), _pad_cols(k_sb, 8, 64), _pad_cols(v_sb, 8, 64),
        gate], axis=1)


def _w_in_reference(g):
    cuts = np.cumsum([256, 128, 128, 1024, 256, 256, 1024, 1024, 1024])
    c_q, c_kv, k_r, q_swa, k_swa, v_swa, q_sb, k_sb, v_sb, gate = jnp.split(g, [int(c) for c in cuts], axis=1)
    return jnp.concatenate([
        c_q, c_kv, k_r[:, MLA_NOPE:MLA_NOPE + MLA_ROPE], _unpad_cols(q_swa, 8, 64), _unpad_cols(k_swa, 2, 64),
        _unpad_cols(v_swa, 2, 64), _unpad_cols(q_sb, 8, 64), _unpad_cols(k_sb, 8, 64), _unpad_cols(v_sb, 8, 64),
        gate], axis=1)


def _w_ukv_internal(w):
    w3 = w.reshape(MLA_KV_LORA, MLA_HEADS, MLA_NOPE + MLA_V)
    pad = lambda t: jnp.pad(t, ((0, 0), (0, 0), (0, LANES - t.shape[2]))).reshape(MLA_KV_LORA, MLA_HEADS * LANES)
    return pad(w3[:, :, :MLA_NOPE]), pad(w3[:, :, MLA_NOPE:])


def _w_ukv_reference(gk, gv):
    gk = gk.reshape(MLA_KV_LORA, MLA_HEADS, LANES)[:, :, :MLA_NOPE]
    gv = gv.reshape(MLA_KV_LORA, MLA_HEADS, LANES)[:, :, :MLA_V]
    return jnp.concatenate([gk, gv], axis=2).reshape(MLA_KV_LORA, MLA_HEADS * (MLA_NOPE + MLA_V))


def _layer_fwd(x, w, tabs):
    mla_tab, swa_tab = tabs
    sv = {"x": x}

    def f_norm(rows, consts):
        return [_rms(rows[0], consts[0])], []

    (h,) = _rowwise(f_norm, [x], [w["g_mix_pre"]], [(D_MODEL, BF16)], [], "norm_mix_pre")
    p1 = _matmul(h, w["w_in1"], "nn", [F32], "proj_lat")
    p2 = _matmul(h, w["w_in2"], "nn", [BF16], "proj_qkv")
    gates = _matmul(h, w["w_in3"], "nn", [BF16], "proj_gate",
                    epilogue=lambda acc, b: (1.0 / (1.0 + jnp.exp(-(acc + b))),), row_extras=[w["b_gate"]])

    def f_prep(rows, consts):
        t = rows[0]
        gq, gkv = consts[0], consts[1]
        mc, mu, md = rows[1], rows[2], rows[3]
        sc, su, sd = rows[4], rows[5], rows[6]
        cq_n = _rms(t[:, 0:256], gq)
        ckv_n = _rms(t[:, 256:384], gkv)
        kr = _rope(t[:, 384:512], mc, mu, md, MLA_ROPE // 2)
        qs = [_rope(t[:, 512 + j * LANES:512 + (j + 1) * LANES], sc, su, sd, SWA_HEAD_DIM // 2) for j in range(8)]
        ks = [_rope(t[:, 1536 + j * LANES:1536 + (j + 1) * LANES], sc, su, sd, SWA_HEAD_DIM // 2) for j in range(2)]
        return [cq_n, ckv_n, kr, jnp.concatenate(qs, axis=1), jnp.concatenate(ks, axis=1)], []

    cq_n, ckv_n, kr, q_swa, k_swa = _rowwise(
        f_prep, [p1, *mla_tab["k"], *swa_tab["f"]], [w["g_q_lat"], w["g_kv_lat"]],
        [(256, BF16), (128, BF16), (LANES, F32), (1024, BF16), (256, BF16)], [], "lat_prep")

    q_lat = _matmul(cq_n, w["w_uq"], "nn", [F32], "mla_q_up")
    k_lat = _matmul(ckv_n, w["w_ukv_k"], "nn", [F32], "mla_k_up")
    v_mla = _matmul(ckv_n, w["w_ukv_v"], "nn", [BF16], "mla_v_up")

    def f_mla_prep(rows, consts):
        ql, kl, krr, mc, mu, md = rows
        qs = [_rope(ql[:, j * LANES:(j + 1) * LANES], mc, mu, md, MLA_ROPE // 2) for j in range(8)]
        ks = [kl[:, j * LANES:(j + 1) * LANES] + krr for j in range(8)]
        return [jnp.concatenate(qs, axis=1), jnp.concatenate(ks, axis=1)], []

    q_mla, k_mla = _rowwise(f_mla_prep, [q_lat, k_lat, kr, *mla_tab["q"]], [], [(1024, BF16), (1024, BF16)], [], "mla_prep")

    o_mla, lse_mla = _softmax_attn_fwd(q_mla, k_mla, v_mla, MLA_HEADS, (MLA_NOPE + MLA_ROPE) ** -0.5, "mla_fwd")
    o_swa, lse_swa = _swa_fwd(q_swa, k_swa, p2, 0, w["sink_b"], "swa_fwd")
    o_sb = _sb_attn_fwd(p2, SB_HEADS, SB_HEAD_DIM ** -0.5, "sb_fwd", 2, 10, 18)

    oa = _matmul(o_mla, w["w_o_mla"], "nn", [F32], "o_proj_mla")
    ob = _matmul(o_swa, w["w_o_swa"], "nn", [F32], "o_proj_swa")
    oc = _matmul(o_sb, w["w_o_sb"], "nn", [F32], "o_proj_sb")

    def f_mix(rows, consts):
        a, b, c, g = rows
        g = g.astype(F32)
        return [g[:, 0:1024] * a + g[:, 1024:2048] * b + g[:, 2048:3072] * c], []

    (mixed,) = _rowwise(f_mix, [oa, ob, oc, gates], [], [(D_MODEL, BF16)], [], "gate_mix")
    y = _matmul(mixed, w["w_out"], "nn", [F32], "out_proj")

    def f_res_norm(rows, consts):
        return [rows[0] + _rms(rows[1], consts[0])], []

    (x1,) = _rowwise(f_res_norm, [x, y], [w["g_mix_post"]], [(D_MODEL, F32)], [], "res_norm_mix")
    (h2,) = _rowwise(f_norm, [x1], [w["g_mlp_pre"]], [(D_MODEL, BF16)], [], "norm_mlp_pre")

    def relu2(acc):
        r = jnp.maximum(acc, 0.0)
        return acc, r * r

    up, u = _matmul(h2, w["w_up"], "nn", [BF16, BF16], "mlp_up", epilogue=relu2)
    zd = _matmul(u, w["w_down"], "nn", [F32], "mlp_down")
    (x2,) = _rowwise(f_res_norm, [x1, zd], [w["g_mlp_post"]], [(D_MODEL, F32)], [], "res_norm_mlp")

    sv.update(h=h, p1=p1, p2=p2, gates=gates, cq_n=cq_n, ckv_n=ckv_n, q_swa=q_swa, k_swa=k_swa, q_mla=q_mla,
              k_mla=k_mla, v_mla=v_mla, o_mla=o_mla, lse_mla=lse_mla, o_swa=o_swa, lse_swa=lse_swa, o_sb=o_sb,
              oa=oa, ob=ob, oc=oc, mixed=mixed, y=y, x1=x1, h2=h2, up=up, u=u, zd=zd)
    return x2, sv


def _layer_bwd(dx2, w, sv, tabs):
    mla_tab, swa_tab = tabs
    gr = {}

    def f_norm_bwd(rows, consts):
        dx, dg = _rms_bwd(rows[0], consts[0], rows[1])
        return [dx], [dg]

    def f_norm_bwd_res(rows, consts):
        dx, dg = _rms_bwd(rows[0], consts[0], rows[1])
        return [rows[2] + dx], [dg]

    dzd, gr["g_mlp_post"] = _rowwise(f_norm_bwd, [sv["zd"], dx2], [w["g_mlp_post"]], [(D_MODEL, BF16)], [D_MODEL], "b_norm_mlp_post")
    gr["w_down"] = _matmul(sv["u"], dzd, "tn", [F32], "b_w_down")
    dup = _matmul(dzd, w["w_down"], "nt", [BF16], "b_mlp_down",
                  epilogue=lambda acc, up: (acc * 2.0 * jnp.maximum(up.astype(F32), 0.0),), extras=[sv["up"]])
    gr["w_up"] = _matmul(sv["h2"], dup, "tn", [F32], "b_w_up")
    dh2 = _matmul(dup, w["w_up"], "nt", [F32], "b_mlp_up")
    dx1, gr["g_mlp_pre"] = _rowwise(f_norm_bwd_res, [sv["x1"], dh2, dx2], [w["g_mlp_pre"]], [(D_MODEL, F32)], [D_MODEL], "b_norm_mlp_pre")

    dy, gr["g_mix_post"] = _rowwise(f_norm_bwd, [sv["y"], dx1], [w["g_mix_post"]], [(D_MODEL, BF16)], [D_MODEL], "b_norm_mix_post")
    gr["w_out"] = _matmul(sv["mixed"], dy, "tn", [F32], "b_w_out")
    dmixed = _matmul(dy, w["w_out"], "nt", [F32], "b_out_proj")

    def f_mix_bwd(rows, consts):
        dm, a, b, c, g = rows
        g = g.astype(F32)
        outs, dls = [], []
        for j, o in enumerate((a, b, c)):
            gj = g[:, j * D_MODEL:(j + 1) * D_MODEL]
            outs.append(dm * gj)
            dls.append(dm * o * gj * (1.0 - gj))
        dl = jnp.concatenate(dls, axis=1)
        return outs + [dl], [dl]

    doa, dob, doc, dlogit, gr["b_gate"] = _rowwise(
        f_mix_bwd, [dmixed, sv["oa"], sv["ob"], sv["oc"], sv["gates"]], [],
        [(D_MODEL, BF16)] * 3 + [(P3_W, BF16)], [P3_W], "b_gate_mix")

    gr["w_o_mla"] = _matmul(sv["o_mla"], doa, "tn", [F32], "b_w_o_mla")
    gr["w_o_swa"] = _matmul(sv["o_swa"], dob, "tn", [F32], "b_w_o_swa")
    gr["w_o_sb"] = _matmul(sv["o_sb"], doc, "tn", [F32], "b_w_o_sb")
    do_mla = _matmul(doa, w["w_o_mla"], "nt", [BF16], "b_o_proj_mla")
    do_swa = _matmul(dob, w["w_o_swa"], "nt", [BF16], "b_o_proj_swa")
    do_sb = _matmul(doc, w["w_o_sb"], "nt", [BF16], "b_o_proj_sb")

    dq_sb, dk_sb, dv_sb = _sb_attn_bwd(sv["p2"], sv["o_sb"], do_sb, SB_HEADS, SB_HEAD_DIM ** -0.5, "sb_bwd", 2, 10, 18)
    dq_swa, dk_swa, dv_swa, dsink = _swa_bwd(sv["q_swa"], sv["k_swa"], sv["p2"], 0, w["sink_b"], sv["o_swa"],
                                             sv["lse_swa"], do_swa, "swa_bwd")
    gr["swa_sinks"] = dsink.reshape(SWA_HEADS, LANES)[:, 0]
    dq_mla, dk_mla, dv_mla = _softmax_attn_bwd(sv["q_mla"], sv["k_mla"], sv["v_mla"], sv["o_mla"], sv["lse_mla"], do_mla,
                                               MLA_HEADS, (MLA_NOPE + MLA_ROPE) ** -0.5, "mla_bwd")

    def f_mla_post(rows, consts):
        dq, dk, qc, qu, qd, kc, ku, kd = rows
        dqs = [_rope(dq[:, j * LANES:(j + 1) * LANES], qc, qu, qd, MLA_ROPE // 2) for j in range(8)]
        dkr = dk[:, 0:LANES]
        for j in range(1, 8):
            dkr = dkr + dk[:, j * LANES:(j + 1) * LANES]
        return [jnp.concatenate(dqs, axis=1), _rope(dkr, kc, ku, kd, MLA_ROPE // 2)], []

    dq_lat, dkr = _rowwise(f_mla_post, [dq_mla, dk_mla, *mla_tab["q_inv"], *mla_tab["k_inv"]], [],
                           [(1024, BF16), (LANES, F32)], [], "b_mla_post")
    gr["w_uq"] = _matmul(sv["cq_n"], dq_lat, "tn", [F32], "b_w_uq")
    gr["w_ukv_k"] = _matmul(sv["ckv_n"], dk_mla, "tn", [F32], "b_w_ukv_k")
    gr["w_ukv_v"] = _matmul(sv["ckv_n"], dv_mla, "tn", [F32], "b_w_ukv_v")
    dcq_n = _matmul(dq_lat, w["w_uq"], "nt", [F32], "b_mla_q_up")
    dckv_a = _matmul(dk_mla, w["w_ukv_k"], "nt", [F32], "b_mla_k_up")
    dckv_b = _matmul(dv_mla, w["w_ukv_v"], "nt", [F32], "b_mla_v_up")

    def f_prep_bwd(rows, consts):
        t, dcq, dca, dcb, dkr_, dqs, dks, sc, su, sd = rows
        gq, gkv = consts
        dc_q, dgq = _rms_bwd(t[:, 0:256], gq, dcq)
        dc_kv, dgkv = _rms_bwd(t[:, 256:384], gkv, dca + dcb)
        q_parts = [_rope(dqs[:, j * LANES:(j + 1) * LANES], sc, su, sd, SWA_HEAD_DIM // 2) for j in range(8)]
        k_parts = [_rope(dks[:, j * LANES:(j + 1) * LANES], sc, su, sd, SWA_HEAD_DIM // 2) for j in range(2)]
        return [jnp.concatenate([dc_q, dc_kv, dkr_] + q_parts + k_parts, axis=1)], [dgq, dgkv]

    dp1, gr["g_q_lat"], gr["g_kv_lat"] = _rowwise(
        f_prep_bwd, [sv["p1"], dcq_n, dckv_a, dckv_b, dkr, dq_swa, dk_swa, *swa_tab["inv"]], [w["g_q_lat"], w["g_kv_lat"]],
        [(P1_W, BF16)], [256, 128], "b_lat_prep")

    dp = jnp.concatenate([dp1, dv_swa.astype(BF16), dq_sb.astype(BF16), dk_sb.astype(BF16), dv_sb.astype(BF16), dlogit], axis=1)
    gr["w_in"] = _matmul(sv["h"], dp, "tn", [F32], "b_w_in")
    dh = _matmul(dp, w["w_in"], "nt", [F32], "b_proj")
    dx, gr["g_mix_pre"] = _rowwise(f_norm_bwd_res, [sv["x"], dh, dx1], [w["g_mix_pre"]], [(D_MODEL, F32)], [D_MODEL], "b_norm_mix_pre")
    return dx, gr


def _local_step(x, positions, loss_target, full):
    mc, mu, md = _rope_tables(positions, MLA_NOPE, MLA_ROPE, True)
    kc, ku, kd = _rope_tables(positions, MLA_NOPE, MLA_ROPE, False)
    sc, su, sd = _rope_tables(positions, 0, SWA_HEAD_DIM, False)
    mla_tab = {"q": (mc, mu, md), "k": (kc, ku, kd), "q_inv": (mc, -mu, -md), "k_inv": (kc, -ku, -kd)}
    swa_tab = {"f": (sc, su, sd), "inv": (sc, -su, -sd)}
    tabs = (mla_tab, swa_tab)

    layers = []
    for l in range(DEPTH):
        w_in = _w_in_internal(full["w_in"][l].astype(BF16))
        uk, uv = _w_ukv_internal(full["w_ukv"][l].astype(BF16))
        layers.append({
            "w_in": w_in, "w_in1": w_in[:, :P1_W], "w_in2": w_in[:, P1_W:P1_W + P2_W], "w_in3": w_in[:, P1_W + P2_W:],
            "w_uq": _pad_cols(full["w_uq"][l].astype(BF16), MLA_HEADS, MLA_NOPE + MLA_ROPE),
            "w_ukv_k": uk, "w_ukv_v": uv,
            "w_o_mla": _pad_rows(full["w_o_mla"][l].astype(BF16), 8, 64),
            "w_o_swa": _pad_rows(full["w_o_swa"][l].astype(BF16), 8, 64),
            "w_o_sb": _pad_rows(full["w_o_sb"][l].astype(BF16), 8, 64),
            "w_out": full["w_out"][l].astype(BF16), "w_up": full["w_up"][l].astype(BF16),
            "w_down": full["w_down"][l].astype(BF16),
            "g_mix_pre": full["g_mix_pre"][l][None], "b_gate": full["b_gate"][l][None],
            "g_q_lat": full["g_q_lat"][l][None], "g_kv_lat": full["g_kv_lat"][l][None],
            "g_mix_post": full["g_mix_post"][l][None], "g_mlp_pre": full["g_mlp_pre"][l][None],
            "g_mlp_post": full["g_mlp_post"][l][None],
            "sink_b": jnp.repeat(full["swa_sinks"][l], LANES)[None],
        })

    saved = []
    h = x
    for l in range(DEPTH):
        h, sv = _layer_fwd(h, layers[l], tabs)
        saved.append(sv)

    def f_loss(rows, consts):
        err = rows[0] - rows[1]
        return [err * (1.0 / D_MODEL)], [jnp.sum(err * err, axis=1, keepdims=True)]

    dy, sq = _rowwise(f_loss, [h, loss_target], [], [(D_MODEL, F32)], [1], "loss_head")
    loss_part = sq * (0.5 / D_MODEL)

    grads = [None] * DEPTH
    d = dy
    for l in reversed(range(DEPTH)):
        d, gr = _layer_bwd(d, layers[l], saved[l], tabs)
        grads[l] = {
            "g_mix_pre": gr["g_mix_pre"][0], "w_in": _w_in_reference(gr["w_in"]), "b_gate": gr["b_gate"][0],
            "g_q_lat": gr["g_q_lat"][0], "g_kv_lat": gr["g_kv_lat"][0],
            "w_uq": _unpad_cols(gr["w_uq"], MLA_HEADS, MLA_NOPE + MLA_ROPE),
            "w_ukv": _w_ukv_reference(gr["w_ukv_k"], gr["w_ukv_v"]), "swa_sinks": gr["swa_sinks"],
            "w_o_mla": _unpad_rows(gr["w_o_mla"], 8, 64), "w_o_swa": _unpad_rows(gr["w_o_swa"], 8, 64),
            "w_o_sb": _unpad_rows(gr["w_o_sb"], 8, 64), "w_out": gr["w_out"], "g_mix_post": gr["g_mix_post"][0],
            "g_mlp_pre": gr["g_mlp_pre"][0], "w_up": gr["w_up"], "w_down": gr["w_down"], "g_mlp_post": gr["g_mlp_post"][0],
        }
    stacked = {n: jnp.stack([grads[l][n] for l in range(DEPTH)]) for n in WEIGHTS}
    return loss_part, d, stacked


def _rows_of(a):
    return a.reshape(-1, LANES)


def _small_rows(d):
    parts = []
    for n in SMALL:
        a = d[n]
        if a.shape[1] < LANES:
            a = jnp.pad(a, ((0, 0), (0, LANES - a.shape[1])))
        parts.append(_rows_of(a))
    return parts


def _pack(shards, small, dtype):
    parts = [_rows_of(shards[n]) for n in SHARDED]
    if small is not None:
        parts += _small_rows(small)
    slab = jnp.concatenate(parts, axis=0).astype(dtype)
    pad = (-slab.shape[0]) % SLAB_ROW_ALIGN
    return jnp.pad(slab, ((0, pad), (0, 0)))


def _unpack(slab, shard_shapes, small_shapes):
    out, r = {}, 0
    for n in SHARDED:
        rows = int(np.prod(shard_shapes[n])) // LANES
        out[n] = slab[r:r + rows].reshape(shard_shapes[n])
        r += rows
    if small_shapes is not None:
        for n in SMALL:
            depth, width = small_shapes[n]
            rows = depth * max(width, LANES) // LANES
            out[n] = slab[r:r + rows].reshape(depth, max(width, LANES))[:, :width]
            r += rows
    return out


def _chip_exchange(src, per_chip_src, name):
    rows = src.shape[-2]

    def body(src_ref, out_ref, send_sems, recv_sems, local_sem):
        x, y, c = lax.axis_index("x"), lax.axis_index("y"), lax.axis_index("c")
        me = 2 * x + y
        chips = [(1 - x, y), (x, 1 - y), (1 - x, 1 - y)]
        pick = (lambda j: src_ref.at[j]) if per_chip_src else (lambda j: src_ref)
        mine = pltpu.make_async_copy(pick(me), out_ref.at[me], local_sem)
        mine.start()
        sends = []
        for k, (cx, cy) in enumerate(chips):
            cp = pltpu.make_async_remote_copy(
                src_ref=pick(2 * cx + cy), dst_ref=out_ref.at[me], send_sem=send_sems.at[k], recv_sem=recv_sems.at[k],
                device_id=(cx, cy, c), device_id_type=pl.DeviceIdType.MESH)
            cp.start()
            sends.append(cp)
        for k, (cx, cy) in enumerate(chips):
            pltpu.make_async_remote_copy(
                src_ref=pick(me), dst_ref=out_ref.at[2 * cx + cy], send_sem=send_sems.at[k], recv_sem=recv_sems.at[k],
                device_id=(cx, cy, c), device_id_type=pl.DeviceIdType.MESH).wait_recv()
        for cp in sends:
            cp.wait_send()
        mine.wait()

    return pl.pallas_call(
        body,
        name=name,
        in_specs=[pl.BlockSpec(memory_space=pl.ANY)],
        out_specs=pl.BlockSpec(memory_space=pl.ANY),
        out_shape=jax.ShapeDtypeStruct((N_CHIPS, rows, LANES), src.dtype),
        scratch_shapes=[pltpu.SemaphoreType.DMA((3,)), pltpu.SemaphoreType.DMA((3,)), pltpu.SemaphoreType.DMA],
    )(src)


def _sibling_exchange(src, name):
    def body(src_ref, out_ref, send_sem, recv_sem):
        peer = (lax.axis_index("x"), lax.axis_index("y"), 1 - lax.axis_index("c"))
        cp = pltpu.make_async_remote_copy(src_ref=src_ref, dst_ref=out_ref, send_sem=send_sem, recv_sem=recv_sem,
                                          device_id=peer, device_id_type=pl.DeviceIdType.MESH)
        cp.start()
        cp.wait()

    return pl.pallas_call(
        body,
        name=name,
        in_specs=[pl.BlockSpec(memory_space=pl.ANY)],
        out_specs=pl.BlockSpec(memory_space=pl.ANY),
        out_shape=jax.ShapeDtypeStruct(src.shape, src.dtype),
        scratch_shapes=[pltpu.SemaphoreType.DMA, pltpu.SemaphoreType.DMA],
    )(src)


def _sum_chips(buf, name):
    rows = buf.shape[1]
    bm = 2048 if rows % 2048 == 0 else SLAB_ROW_ALIGN

    def body(b_ref, o_ref):
        o_ref[...] = ((b_ref[0] + b_ref[1]) + b_ref[2]) + b_ref[3]

    return pl.pallas_call(
        body,
        name=name,
        grid=(rows // bm,),
        in_specs=[pl.BlockSpec((N_CHIPS, bm, LANES), lambda i: (0, i, 0))],
        out_specs=pl.BlockSpec((bm, LANES), lambda i: (i, 0)),
        out_shape=jax.ShapeDtypeStruct((rows, LANES), F32),
        compiler_params=_params(("arbitrary",)),
    )(buf)


def _adamw(w, m, v, g_mine, g_sibling, name):
    def fn(rows, consts):
        w_, m_, v_, ga, gb = rows
        g = ga + gb
        m_new = ADAM_B1 * m_ + (1.0 - ADAM_B1) * g
        v_new = ADAM_B2 * v_ + (1.0 - ADAM_B2) * (g * g)
        m_hat = m_new / (1.0 - ADAM_B1 ** ADAM_STEP)
        v_hat = v_new / (1.0 - ADAM_B2 ** ADAM_STEP)
        delta = -ADAM_LR * (m_hat / (jnp.sqrt(v_hat) + ADAM_EPS) + ADAM_WD * w_)
        return [g, delta, m_new, v_new], []

    return _rowwise(fn, [w, m, v, g_mine, g_sibling], [], [(LANES, F32)] * 4, [], name, bm=2048)


def kernel(x, positions, g_mix_pre, w_in, b_gate, g_q_lat, g_kv_lat, w_uq, w_ukv, swa_sinks, w_o_mla, w_o_swa, w_o_sb, w_out, g_mix_post, g_mlp_pre, w_up, w_down, g_mlp_post, loss_target, m_g_mix_pre, m_w_in, m_b_gate, m_g_q_lat, m_g_kv_lat, m_w_uq, m_w_ukv, m_swa_sinks, m_w_o_mla, m_w_o_swa, m_w_o_sb, m_w_out, m_g_mix_post, m_g_mlp_pre, m_w_up, m_w_down, m_g_mlp_post, v_g_mix_pre, v_w_in, v_b_gate, v_g_q_lat, v_g_kv_lat, v_w_uq, v_w_ukv, v_swa_sinks, v_w_o_mla, v_w_o_swa, v_w_o_sb, v_w_out, v_g_mix_post, v_g_mlp_pre, v_w_up, v_w_down, v_g_mlp_post):
    given = dict(locals())
    wts = {n: given[n] for n in WEIGHTS}
    mom_m = {n: given["m_" + n] for n in WEIGHTS}
    mom_v = {n: given["v_" + n] for n in WEIGHTS}
    shard_shapes = {n: wts[n].shape for n in SHARDED}
    small_shapes = {n: wts[n].shape for n in SMALL}

    gathered = _chip_exchange(_pack(wts, None, BF16), False, "gather_weights")
    full = {n: wts[n] for n in SMALL}
    per_chip = [_unpack(gathered[j], shard_shapes, None) for j in range(N_CHIPS)]
    for n in SHARDED:
        full[n] = jnp.concatenate([per_chip[j][n] for j in range(N_CHIPS)], axis=SHARD_AXIS[n])

    loss_part, grad_x, grads = _local_step(x[0], positions[0], loss_target[0], full)
    loss = lax.psum(loss_part[0, 0], ("x", "y", "c"))

    small_g = {n: grads[n] for n in SMALL}
    slabs = []
    for j in range(N_CHIPS):
        shard = {n: jnp.split(grads[n], N_CHIPS, axis=SHARD_AXIS[n])[j] for n in SHARDED}
        slabs.append(_pack(shard, small_g, F32))
    landed = _chip_exchange(jnp.stack(slabs), True, "scatter_grads")
    g_mine = _sum_chips(landed, "sum_chips")
    g_sibling = _sibling_exchange(g_mine, "sibling_grads")

    g_slab, d_slab, m_slab, v_slab = _adamw(
        _pack(wts, wts, F32), _pack(mom_m, mom_m, F32), _pack(mom_v, mom_v, F32), g_mine, g_sibling, "adamw")
    outs = [loss, grad_x[None]]
    for slab in (g_slab, d_slab, m_slab, v_slab):
        un = _unpack(slab, shard_shapes, small_shapes)
        outs += [un[n] for n in WEIGHTS]
    return tuple(outs)
```

```python
import numpy as np
import jax
import jax.numpy as jnp
from jax import lax
from jax.experimental import pallas as pl
from jax.experimental.pallas import tpu as pltpu

F32 = jnp.float32
BF16 = jnp.bfloat16

D_MODEL = 1024
DEPTH = 4
MLA_HEADS, MLA_Q_LORA, MLA_KV_LORA, MLA_NOPE, MLA_ROPE, MLA_V = 8, 256, 128, 64, 32, 64
SWA_HEADS, SWA_KV_HEADS, SWA_HEAD_DIM, SWA_WINDOW = 8, 2, 64, 128
SB_HEADS, SB_HEAD_DIM = 8, 64
D_FF = 4 * D_MODEL
ROPE_THETA = 10000.0
EPS = 1e-6
SPLIT_SIZES = (256, 128, 32, 512, 128, 128, 512, 512, 512, 3 * D_MODEL)
SPLIT_POINTS = [int(v) for v in np.cumsum(SPLIT_SIZES)[:-1]]

ADAM_LR, ADAM_B1, ADAM_B2, ADAM_EPS, ADAM_WD, ADAM_STEP = 0.001, 0.9, 0.999, 1e-08, 0.01, 10

LANES = 128
V7X_VMEM_BYTES = 64 * 1024 * 1024
VMEM_LIMIT = V7X_VMEM_BYTES - 8 * 1024 * 1024
MATMUL_VMEM_BUDGET = 36 * 1024 * 1024
N_CHIPS = 4
SLAB_ROW_ALIGN = 512

P1_W = 256 + 128 + 128 + 1024 + 256
P2_W = 256 + 1024 + 1024 + 1024
P3_W = 3 * D_MODEL
P2_CUTS = (0, 256, 1280, 2304, 3328)

SHARDED = ("w_in", "w_uq", "w_ukv", "w_o_mla", "w_o_swa", "w_o_sb", "w_out", "w_up", "w_down")
SHARD_AXIS = {"w_in": 2, "w_uq": 2, "w_ukv": 2, "w_o_mla": 2, "w_o_swa": 2, "w_o_sb": 2, "w_out": 1, "w_up": 2, "w_down": 1}
SMALL = ("g_mix_pre", "b_gate", "g_q_lat", "g_kv_lat", "swa_sinks", "g_mix_post", "g_mlp_pre", "g_mlp_post")
WEIGHTS = ("g_mix_pre", "w_in", "b_gate", "g_q_lat", "g_kv_lat", "w_uq", "w_ukv", "swa_sinks", "w_o_mla", "w_o_swa",
           "w_o_sb", "w_out", "g_mix_post", "g_mlp_pre", "w_up", "w_down", "g_mlp_post")

NN = (((1,), (0,)), ((), ()))
NT = (((1,), (1,)), ((), ()))
TN = (((0,), (0,)), ((), ()))


def _dot(a, b, dims):
    return lax.dot_general(a, b, dims, preferred_element_type=F32)


def _params(sem):
    return pltpu.CompilerParams(dimension_semantics=sem, vmem_limit_bytes=VMEM_LIMIT)


def _largest_tile(n, cap):
    if n <= cap:
        return n
    best = LANES
    for t in range(LANES, cap + 1, LANES):
        if n % t == 0:
            best = t
    return best


def _matmul_tiles(M, N, K, a_bytes, b_bytes, out_bytes, extra_bytes):
    tn = _largest_tile(N, 1792)
    tm = _largest_tile(M, 1024 if tn <= 1024 else 512)
    tk = _largest_tile(K, 2048)

    def need(tm_, tk_):
        acc = 4 * tm_ * tn if tk_ < K else 0
        return 2 * (tm_ * tk_ * a_bytes + tk_ * tn * b_bytes + tm_ * tn * (out_bytes + extra_bytes)) + acc

    while need(tm, tk) > MATMUL_VMEM_BUDGET:
        if tk >= tm and tk % 256 == 0:
            tk //= 2
        elif tm % 256 == 0:
            tm //= 2
        else:
            break
    return tm, tn, tk


def _matmul(a, b, mode, out_dtypes, name, epilogue=None, extras=(), row_extras=()):
    if mode == "nn":
        (M, K), (K2, N) = a.shape, b.shape
    elif mode == "nt":
        (M, K), (N, K2) = a.shape, b.shape
    else:
        (K, M), (K2, N) = a.shape, b.shape
    assert K == K2, (name, a.shape, b.shape)
    tm, tn, tk = _matmul_tiles(
        M, N, K, a.dtype.itemsize, b.dtype.itemsize, sum(jnp.dtype(d).itemsize for d in out_dtypes),
        sum(e.dtype.itemsize for e in extras))
    assert M % tm == 0 and N % tn == 0 and K % tk == 0, (name, M, N, K, tm, tn, tk)
    nk = K // tk
    if mode == "tn":
        a_spec = pl.BlockSpec((tk, tm), lambda i, j, k: (k, i))
    else:
        a_spec = pl.BlockSpec((tm, tk), lambda i, j, k: (i, k))
    if mode == "nt":
        b_spec = pl.BlockSpec((tn, tk), lambda i, j, k: (j, k))
    else:
        b_spec = pl.BlockSpec((tk, tn), lambda i, j, k: (k, j))
    dims = {"nn": NN, "nt": NT, "tn": TN}[mode]
    n_ex, n_rex, n_out = len(extras), len(row_extras), len(out_dtypes)

    def body(*refs):
        a_ref, b_ref = refs[:2]
        ex = refs[2:2 + n_ex]
        rex = refs[2 + n_ex:2 + n_ex + n_rex]
        outs = refs[2 + n_ex + n_rex:2 + n_ex + n_rex + n_out]

        def finish(total):
            res = (total,) if epilogue is None else epilogue(total, *[e[...] for e in ex], *[e[...] for e in rex])
            for o, r in zip(outs, res):
                o[...] = r.astype(o.dtype)

        part = _dot(a_ref[...].astype(BF16), b_ref[...].astype(BF16), dims)
        if nk == 1:
            finish(part)
            return
        acc = refs[-1]
        k = pl.program_id(2)

        @pl.when(k == 0)
        def _():
            acc[...] = part

        @pl.when(k > 0)
        def _():
            acc[...] += part

        @pl.when(k == nk - 1)
        def _():
            finish(acc[...])

    in_specs = [a_spec, b_spec]
    in_specs += [pl.BlockSpec((tm, tn), lambda i, j, k: (i, j)) for _ in extras]
    in_specs += [pl.BlockSpec((1, tn), lambda i, j, k: (0, j)) for _ in row_extras]
    out = pl.pallas_call(
        body,
        name=name,
        grid=(M // tm, N // tn, nk),
        in_specs=in_specs,
        out_specs=[pl.BlockSpec((tm, tn), lambda i, j, k: (i, j)) for _ in out_dtypes],
        out_shape=[jax.ShapeDtypeStruct((M, N), dt) for dt in out_dtypes],
        scratch_shapes=[pltpu.VMEM((tm, tn), F32)] if nk > 1 else [],
        compiler_params=_params(("parallel", "parallel", "arbitrary")),
    )(a, b, *extras, *row_extras)
    return out[0] if n_out == 1 else out


def _rowwise(fn, rows, consts, out_defs, sum_widths, name, bm=256):
    R = rows[0].shape[0]
    bm = min(bm, R)
    assert R % bm == 0, (name, R, bm)
    n_r, n_c, n_o = len(rows), len(consts), len(out_defs)
    n_s = len(sum_widths)

    def body(*refs):
        r_in = refs[:n_r]
        c_in = refs[n_r:n_r + n_c]
        o_refs = refs[n_r + n_c:n_r + n_c + n_o]
        s_refs = refs[n_r + n_c + n_o:]
        outs, sums = fn([r[...] for r in r_in], [c[...] for c in c_in])
        for o, val in zip(o_refs, outs):
            o[...] = val.astype(o.dtype)
        if n_s:
            @pl.when(pl.program_id(0) == 0)
            def _():
                for s in s_refs:
                    s[...] = jnp.zeros_like(s)

            for s, val in zip(s_refs, sums):
                s[...] += jnp.sum(val, axis=0, keepdims=True)

    in_specs = [pl.BlockSpec((bm, r.shape[1]), lambda i: (i, 0)) for r in rows]
    in_specs += [pl.BlockSpec(c.shape, lambda i: (0, 0)) for c in consts]
    out_specs = [pl.BlockSpec((bm, w), lambda i: (i, 0)) for w, _ in out_defs]
    out_specs += [pl.BlockSpec((1, w), lambda i: (0, 0)) for w in sum_widths]
    out_shape = [jax.ShapeDtypeStruct((R, w), dt) for w, dt in out_defs]
    out_shape += [jax.ShapeDtypeStruct((1, w), F32) for w in sum_widths]
    return pl.pallas_call(
        body,
        name=name,
        grid=(R // bm,),
        in_specs=in_specs,
        out_specs=out_specs,
        out_shape=out_shape,
        compiler_params=_params(("arbitrary",)),
    )(*rows, *consts)


def _rms(x, g):
    r = lax.rsqrt(jnp.mean(x * x, axis=-1, keepdims=True) + EPS)
    return x * r * g


def _rms_bwd(x, g, dy):
    r = lax.rsqrt(jnp.mean(x * x, axis=-1, keepdims=True) + EPS)
    n = x * r
    dn = dy * g
    dx = r * (dn - n * jnp.mean(dn * n, axis=-1, keepdims=True))
    return dx, dy * n


def _rope(x, c, s_up, s_dn, half):
    return x * c + pltpu.roll(x, half, 1) * s_up + pltpu.roll(x, LANES - half, 1) * s_dn


def _rope_tables(positions, lo, d, nope_pass):
    S = positions.shape[0]
    half = d // 2
    inv = 1.0 / (ROPE_THETA ** (jnp.arange(0, d, 2, dtype=F32) / d))
    ang = positions.astype(F32)[:, None] * inv
    cos, sin = jnp.cos(ang), jnp.sin(ang)
    z = lambda n: jnp.zeros((S, n), F32)
    head = jnp.ones((S, lo), F32) if nope_pass else z(lo)
    tail = LANES - lo - d
    c = jnp.concatenate([head, cos, cos, z(tail)], axis=1)
    s_up = jnp.concatenate([z(lo), z(half), sin, z(tail)], axis=1)
    s_dn = jnp.concatenate([z(lo), -sin, z(half), z(tail)], axis=1)
    return c, s_up, s_dn


ATT_BK = 256
ATT_BQ = 2 * ATT_BK


def _diag_mask(strict):
    row = lax.broadcasted_iota(jnp.int32, (ATT_BK, ATT_BK), 0)
    col = lax.broadcasted_iota(jnp.int32, (ATT_BK, ATT_BK), 1)
    return (col < row) if strict else (col <= row)


def _att_specs(S, q_off, k_off, v_off):
    return [
        pl.BlockSpec((ATT_BQ, LANES), lambda h, i: (i, q_off + h)),
        pl.BlockSpec((S, LANES), lambda h, i: (0, k_off + h)),
        pl.BlockSpec((S, LANES), lambda h, i: (0, v_off + h)),
    ]


def _q_block_spec():
    return pl.BlockSpec((ATT_BQ, LANES), lambda h, i: (i, h))


def _kv_block_spec(S):
    return pl.BlockSpec((S, LANES), lambda h, i: (0, h))


def _chain_rows(r):
    return slice(r * ATT_BK, (r + 1) * ATT_BK)


def _softmax_attn_fwd(q, k, v, heads, scale, name, q_off=0, k_off=0, v_off=0):
    S = q.shape[0]
    assert S % ATT_BQ == 0
    bk = ATT_BK

    def body(q_ref, k_ref, v_ref, o_ref, lse_ref):
        i = pl.program_id(1)
        qs = [q_ref[_chain_rows(r), :] for r in range(2)]

        def kv(kb):
            off = pl.multiple_of(kb * bk, bk)
            return k_ref[pl.ds(off, bk), :], v_ref[pl.ds(off, bk), :]

        def tile(qb, ks, vs, carry, masked):
            m, l, acc = carry
            s = _dot(qb, ks, NT) * scale
            if masked:
                s = jnp.where(_diag_mask(False), s, -1e30)
            m_new = jnp.maximum(m, jnp.max(s, axis=1, keepdims=True))
            p = jnp.exp(s - m_new)
            alpha = jnp.exp(m - m_new)
            l = alpha * l + jnp.sum(p, axis=1, keepdims=True)
            acc = alpha * acc + _dot(p.astype(BF16), vs, NN)
            return m_new, l, acc

        def joint(kb, c):
            ks, vs = kv(kb)
            return tile(qs[0], ks, vs, c[0], False), tile(qs[1], ks, vs, c[1], False)

        init = (jnp.full((bk, 1), -1e30, F32), jnp.zeros((bk, 1), F32), jnp.zeros((bk, LANES), F32))
        ca, cb = lax.fori_loop(0, 2 * i, joint, (init, init))
        ks, vs = kv(2 * i)
        ca = tile(qs[0], ks, vs, ca, True)
        cb = tile(qs[1], ks, vs, cb, False)
        ks, vs = kv(2 * i + 1)
        cb = tile(qs[1], ks, vs, cb, True)
        for r, (m, l, acc) in enumerate((ca, cb)):
            o_ref[_chain_rows(r), :] = (acc / l).astype(o_ref.dtype)
            lse_ref[_chain_rows(r), :] = m + jnp.log(l)

    return pl.pallas_call(
        body,
        name=name,
        grid=(heads, S // ATT_BQ),
        in_specs=_att_specs(S, q_off, k_off, v_off),
        out_specs=[_q_block_spec(), pl.BlockSpec((None, ATT_BQ, 1), lambda h, i: (h, i, 0))],
        out_shape=[jax.ShapeDtypeStruct((S, heads * LANES), BF16), jax.ShapeDtypeStruct((heads, S, 1), F32)],
        compiler_params=_params(("parallel", "arbitrary")),
    )(q, k, v)


def _softmax_attn_bwd(q, k, v, o, lse, do, heads, scale, name, q_off=0, k_off=0, v_off=0):
    S = q.shape[0]
    assert S % ATT_BQ == 0
    bk = ATT_BK

    def body(q_ref, k_ref, v_ref, o_ref, lse_ref, do_ref, dq_ref, dk_ref, dv_ref):
        i = pl.program_id(1)

        @pl.when(i == 0)
        def _():
            dk_ref[...] = jnp.zeros_like(dk_ref)
            dv_ref[...] = jnp.zeros_like(dv_ref)

        chains = []
        for r in range(2):
            rows = _chain_rows(r)
            dob = do_ref[rows, :]
            delta = jnp.sum(dob.astype(F32) * o_ref[rows, :].astype(F32), axis=1, keepdims=True)
            chains.append((q_ref[rows, :], dob, lse_ref[rows, :], delta))

        def tile(chain, ks, vs, dq, masked):
            qb, dob, lse_b, delta = chain
            s = _dot(qb, ks, NT) * scale
            if masked:
                s = jnp.where(_diag_mask(False), s, -1e30)
            p = jnp.exp(s - lse_b)
            dv_c = _dot(p.astype(BF16), dob, TN)
            ds = (p * (_dot(dob, vs, NT) - delta) * scale).astype(BF16)
            return dq + _dot(ds, ks, NN), _dot(ds, qb, TN), dv_c

        def step(kb, dqs, masked_a, with_a):
            off = pl.multiple_of(kb * bk, bk)
            ks, vs = k_ref[pl.ds(off, bk), :], v_ref[pl.ds(off, bk), :]
            dq_b, dk_c, dv_c = tile(chains[1], ks, vs, dqs[1], not with_a)
            dq_a = dqs[0]
            if with_a:
                dq_a, dk_a, dv_a = tile(chains[0], ks, vs, dqs[0], masked_a)
                dk_c, dv_c = dk_c + dk_a, dv_c + dv_a
            dk_ref[pl.ds(off, bk), :] += dk_c
            dv_ref[pl.ds(off, bk), :] += dv_c
            return dq_a, dq_b

        zero = jnp.zeros((bk, LANES), F32)
        dqs = lax.fori_loop(0, 2 * i, lambda kb, c: step(kb, c, False, True), (zero, zero))
        dqs = step(2 * i, dqs, True, True)
        dqs = step(2 * i + 1, dqs, False, False)
        for r in range(2):
            dq_ref[_chain_rows(r), :] = dqs[r]

    return pl.pallas_call(
        body,
        name=name,
        grid=(heads, S // ATT_BQ),
        in_specs=_att_specs(S, q_off, k_off, v_off) + [
            _q_block_spec(), pl.BlockSpec((None, ATT_BQ, 1), lambda h, i: (h, i, 0)), _q_block_spec()],
        out_specs=[_q_block_spec(), _kv_block_spec(S), _kv_block_spec(S)],
        out_shape=[jax.ShapeDtypeStruct((S, heads * LANES), F32)] * 3,
        compiler_params=_params(("parallel", "arbitrary")),
    )(q, k, v, o, lse, do)


def _tri(n, inclusive):
    r = lax.broadcasted_iota(jnp.int32, (n, n), 0)
    c = lax.broadcasted_iota(jnp.int32, (n, n), 1)
    return jnp.where((r >= c) if inclusive else (r > c), 1.0, 0.0).astype(BF16)


def _suffix_sum(x, tri):
    hi = x.astype(BF16)
    lo = (x - hi.astype(F32)).astype(BF16)
    return _dot(hi, tri, NN) + _dot(lo, tri, NN)


def _sb_logs(z):
    lg = jnp.log(1.0 + jnp.exp(-jnp.abs(z)))
    l1m = -(jnp.maximum(z, 0.0) + lg)
    return l1m, l1m + z


SB_SCALE = SB_HEAD_DIM ** -0.5
assert SB_SCALE == 0.125


def _sb_attn_fwd(qkv, heads, name, q_off, k_off, v_off):
    S = qkv.shape[0]
    assert S % ATT_BQ == 0
    bk = ATT_BK

    def body(q_ref, k_ref, v_ref, o_ref):
        i = pl.program_id(1)
        qs = [q_ref[_chain_rows(r), :] * SB_SCALE for r in range(2)]
        tri = _tri(bk, False)

        def kv(kb):
            off = pl.multiple_of(kb * bk, bk)
            return k_ref[pl.ds(off, bk), :], v_ref[pl.ds(off, bk), :]

        def tile(qb, ks, vs, carry, masked):
            run, acc = carry
            l1m, lb = _sb_logs(_dot(qb, ks, NT))
            if masked:
                valid = _diag_mask(True)
                l1m = jnp.where(valid, l1m, 0.0)
            ex = lb + run + _suffix_sum(l1m, tri)
            if masked:
                ex = jnp.where(valid, ex, -1e30)
            acc = acc + _dot(jnp.exp(ex).astype(BF16), vs, NN)
            return run + jnp.sum(l1m, axis=1, keepdims=True), acc

        def joint(t, c):
            ks, vs = kv(2 * i - 1 - t)
            return tile(qs[0], ks, vs, c[0], False), tile(qs[1], ks, vs, c[1], False)

        init = (jnp.zeros((bk, 1), F32), jnp.zeros((bk, LANES), F32))
        ks, vs = kv(2 * i + 1)
        cb = tile(qs[1], ks, vs, init, True)
        ks, vs = kv(2 * i)
        ca = tile(qs[0], ks, vs, init, True)
        cb = tile(qs[1], ks, vs, cb, False)
        ca, cb = lax.fori_loop(0, 2 * i, joint, (ca, cb))
        o_ref[_chain_rows(0), :] = ca[1]
        o_ref[_chain_rows(1), :] = cb[1]

    return pl.pallas_call(
        body,
        name=name,
        grid=(heads, S // ATT_BQ),
        in_specs=_att_specs(S, q_off, k_off, v_off),
        out_specs=_q_block_spec(),
        out_shape=jax.ShapeDtypeStruct((S, heads * LANES), F32),
        compiler_params=_params(("parallel", "arbitrary")),
    )(qkv, qkv, qkv)


def _sb_attn_bwd(qkv, o, do, heads, name, q_off, k_off, v_off):
    S = qkv.shape[0]
    assert S % ATT_BQ == 0
    bk = ATT_BK

    def body(q_ref, k_ref, v_ref, o_ref, do_ref, dq_ref, dk_ref, dv_ref):
        i = pl.program_id(1)

        @pl.when(i == 0)
        def _():
            dk_ref[...] = jnp.zeros_like(dk_ref)
            dv_ref[...] = jnp.zeros_like(dv_ref)

        tri = _tri(bk, False)
        tri_inc = _tri(bk, True)
        chains = []
        for r in range(2):
            rows = _chain_rows(r)
            dob = do_ref[rows, :]
            delta = jnp.sum(dob.astype(F32) * o_ref[rows, :], axis=1, keepdims=True)
            chains.append((q_ref[rows, :] * SB_SCALE, dob, delta))

        def tile(chain, ks, vs, carry, masked):
            qb, dob, delta = chain
            run, grun, dq = carry
            l1m, lb = _sb_logs(_dot(qb, ks, NT))
            if masked:
                valid = _diag_mask(True)
                l1m = jnp.where(valid, l1m, 0.0)
            ex = lb + run + _suffix_sum(l1m, tri)
            if masked:
                ex = jnp.where(valid, ex, -1e30)
            ab = jnp.exp(ex).astype(BF16)
            dv_c = _dot(ab, dob, TN)
            g = ab.astype(F32) * _dot(dob, vs, NT)
            g_left = delta - (grun + _suffix_sum(g, tri_inc))
            beta = jnp.exp(lb)
            dz = g * (1.0 - beta) - g_left * beta
            if masked:
                dz = jnp.where(valid, dz, 0.0)
            dzb = dz.astype(BF16)
            carry = (run + jnp.sum(l1m, axis=1, keepdims=True), grun + jnp.sum(g, axis=1, keepdims=True),
                     dq + _dot(dzb, ks, NN))
            return carry, _dot(dzb, qb, TN), dv_c

        def step(kb, cs, masked_a, with_a):
            off = pl.multiple_of(kb * bk, bk)
            ks, vs = k_ref[pl.ds(off, bk), :], v_ref[pl.ds(off, bk), :]
            cb, dk_c, dv_c = tile(chains[1], ks, vs, cs[1], not with_a)
            ca = cs[0]
            if with_a:
                ca, dk_a, dv_a = tile(chains[0], ks, vs, cs[0], masked_a)
                dk_c, dv_c = dk_c + dk_a, dv_c + dv_a
            dk_ref[pl.ds(off, bk), :] += dk_c
            dv_ref[pl.ds(off, bk), :] += dv_c
            return ca, cb

        zcol = jnp.zeros((bk, 1), F32)
        init = (zcol, zcol, jnp.zeros((bk, LANES), F32))
        cs = step(2 * i + 1, (init, init), False, False)
        cs = step(2 * i, cs, True, True)
        cs = lax.fori_loop(0, 2 * i, lambda t, c: step(2 * i - 1 - t, c, False, True), cs)
        for r in range(2):
            dq_ref[_chain_rows(r), :] = cs[r][2] * SB_SCALE

    return pl.pallas_call(
        body,
        name=name,
        grid=(heads, S // ATT_BQ),
        in_specs=_att_specs(S, q_off, k_off, v_off) + [_q_block_spec(), _q_block_spec()],
        out_specs=[_q_block_spec(), _kv_block_spec(S), _kv_block_spec(S)],
        out_shape=[jax.ShapeDtypeStruct((S, heads * LANES), F32)] * 3,
        compiler_params=_params(("parallel", "arbitrary")),
    )(qkv, qkv, qkv, o, do)


SWA_BLK = 128
SWA_GROUP = SWA_HEADS // SWA_KV_HEADS


def _swa_band_mask(n):
    row = lax.broadcasted_iota(jnp.int32, (SWA_BLK, 2 * SWA_BLK), 0)
    col = lax.broadcasted_iota(jnp.int32, (SWA_BLK, 2 * SWA_BLK), 1)
    return (col > row) & (col <= row + SWA_WINDOW) & ((n > 0) | (col >= SWA_BLK))


def _swa_fwd(q, k, v, v_off, sink_b, name):
    S = q.shape[0]
    nb = S // SWA_BLK
    scale = SWA_HEAD_DIM ** -0.5
    gw = SWA_GROUP * LANES

    def body(q_ref, kp_ref, kc_ref, vp_ref, vc_ref, sink_ref, o_ref, lse_ref):
        n = pl.program_id(1)
        kband = jnp.concatenate([kp_ref[...], kc_ref[...]], axis=0)
        vband = jnp.concatenate([vp_ref[...], vc_ref[...]], axis=0)
        valid = _swa_band_mask(n)
        for g in range(SWA_GROUP):
            lanes = slice(g * LANES, (g + 1) * LANES)
            s = jnp.where(valid, _dot(q_ref[:, lanes], kband, NT) * scale, -1e30)
            sk = sink_ref[:, g * LANES:g * LANES + 1]
            m = jnp.maximum(jnp.max(s, axis=1, keepdims=True), sk)
            p = jnp.exp(s - m)
            den = jnp.sum(p, axis=1, keepdims=True) + jnp.exp(sk - m)
            o_ref[:, lanes] = _dot((p / den).astype(BF16), vband, NN).astype(o_ref.dtype)
            lse_ref[g] = m + jnp.log(den)

    return pl.pallas_call(
        body,
        name=name,
        grid=(SWA_KV_HEADS, nb),
        in_specs=[
            pl.BlockSpec((SWA_BLK, gw), lambda h, n: (n, h)),
            pl.BlockSpec((SWA_BLK, LANES), lambda h, n: (jnp.maximum(n - 1, 0), h)),
            pl.BlockSpec((SWA_BLK, LANES), lambda h, n: (n, h)),
            pl.BlockSpec((SWA_BLK, LANES), lambda h, n: (jnp.maximum(n - 1, 0), v_off + h)),
            pl.BlockSpec((SWA_BLK, LANES), lambda h, n: (n, v_off + h)),
            pl.BlockSpec((1, gw), lambda h, n: (0, h)),
        ],
        out_specs=[
            pl.BlockSpec((SWA_BLK, gw), lambda h, n: (n, h)),
            pl.BlockSpec((SWA_GROUP, SWA_BLK, 1), lambda h, n: (h, n, 0)),
        ],
        out_shape=[jax.ShapeDtypeStruct((S, SWA_HEADS * LANES), BF16), jax.ShapeDtypeStruct((SWA_HEADS, S, 1), F32)],
        compiler_params=_params(("parallel", "arbitrary")),
    )(q, k, k, v, v, sink_b)


def _swa_bwd(q, k, v, v_off, sink_b, o, lse, do, name):
    S = q.shape[0]
    nb = S // SWA_BLK
    scale = SWA_HEAD_DIM ** -0.5
    gw = SWA_GROUP * LANES

    def body(q_ref, kp_ref, kc_ref, vp_ref, vc_ref, sink_ref, o_ref, lse_ref, do_ref, dq_ref, dk_ref, dv_ref, dsink_ref):
        n = pl.program_id(1)

        @pl.when(n == 0)
        def _():
            dk_ref[...] = jnp.zeros_like(dk_ref)
            dv_ref[...] = jnp.zeros_like(dv_ref)
            dsink_ref[...] = jnp.zeros_like(dsink_ref)

        kband = jnp.concatenate([kp_ref[...], kc_ref[...]], axis=0)
        vband = jnp.concatenate([vp_ref[...], vc_ref[...]], axis=0)
        valid = _swa_band_mask(n)
        dkb = jnp.zeros((2 * SWA_BLK, LANES), F32)
        dvb = jnp.zeros((2 * SWA_BLK, LANES), F32)
        for g in range(SWA_GROUP):
            lanes = slice(g * LANES, (g + 1) * LANES)
            qg = q_ref[:, lanes]
            dog = do_ref[:, lanes]
            delta = jnp.sum(dog.astype(F32) * o_ref[:, lanes].astype(F32), axis=1, keepdims=True)
            s = jnp.where(valid, _dot(qg, kband, NT) * scale, -1e30)
            lse_g = lse_ref[g]
            p = jnp.exp(s - lse_g)
            p_sink = jnp.exp(sink_ref[:, g * LANES:g * LANES + 1] - lse_g)
            dsink_ref[:, lanes] += jnp.zeros((1, LANES), F32) - jnp.sum(p_sink * delta, axis=0, keepdims=True)
            dvb = dvb + _dot(p.astype(BF16), dog, TN)
            ds = (p * (_dot(dog, vband, NT) - delta) * scale).astype(BF16)
            dq_ref[:, lanes] = _dot(ds, kband, NN)
            dkb = dkb + _dot(ds, qg, TN)

        cur = pl.multiple_of(n * SWA_BLK, SWA_BLK)
        dk_ref[pl.ds(cur, SWA_BLK), :] += dkb[SWA_BLK:]
        dv_ref[pl.ds(cur, SWA_BLK), :] += dvb[SWA_BLK:]

        @pl.when(n > 0)
        def _():
            before = pl.multiple_of((n - 1) * SWA_BLK, SWA_BLK)
            dk_ref[pl.ds(before, SWA_BLK), :] += dkb[:SWA_BLK]
            dv_ref[pl.ds(before, SWA_BLK), :] += dvb[:SWA_BLK]

    return pl.pallas_call(
        body,
        name=name,
        grid=(SWA_KV_HEADS, nb),
        in_specs=[
            pl.BlockSpec((SWA_BLK, gw), lambda h, n: (n, h)),
            pl.BlockSpec((SWA_BLK, LANES), lambda h, n: (jnp.maximum(n - 1, 0), h)),
            pl.BlockSpec((SWA_BLK, LANES), lambda h, n: (n, h)),
            pl.BlockSpec((SWA_BLK, LANES), lambda h, n: (jnp.maximum(n - 1, 0), v_off + h)),
            pl.BlockSpec((SWA_BLK, LANES), lambda h, n: (n, v_off + h)),
            pl.BlockSpec((1, gw), lambda h, n: (0, h)),
            pl.BlockSpec((SWA_BLK, gw), lambda h, n: (n, h)),
            pl.BlockSpec((SWA_GROUP, SWA_BLK, 1), lambda h, n: (h, n, 0)),
            pl.BlockSpec((SWA_BLK, gw), lambda h, n: (n, h)),
        ],
        out_specs=[
            pl.BlockSpec((SWA_BLK, gw), lambda h, n: (n, h)),
            pl.BlockSpec((S, LANES), lambda h, n: (0, h)),
            pl.BlockSpec((S, LANES), lambda h, n: (0, h)),
            pl.BlockSpec((1, gw), lambda h, n: (0, h)),
        ],
        out_shape=[
            jax.ShapeDtypeStruct((S, SWA_HEADS * LANES), F32),
            jax.ShapeDtypeStruct((S, SWA_KV_HEADS * LANES), F32),
            jax.ShapeDtypeStruct((S, SWA_KV_HEADS * LANES), F32),
            jax.ShapeDtypeStruct((1, SWA_HEADS * LANES), F32),
        ],
        compiler_params=_params(("parallel", "arbitrary")),
    )(q, k, k, v, v, sink_b, o, lse, do)


def _pad_cols(w, heads, real):
    k = w.shape[0]
    return jnp.pad(w.reshape(k, heads, real), ((0, 0), (0, 0), (0, LANES - real))).reshape(k, heads * LANES)


def _unpad_cols(g, heads, real):
    k = g.shape[0]
    return g.reshape(k, heads, LANES)[:, :, :real].reshape(k, heads * real)


def _pad_rows(w, heads, real):
    n = w.shape[1]
    return jnp.pad(w.reshape(heads, real, n), ((0, 0), (0, LANES - real), (0, 0))).reshape(heads * LANES, n)


def _unpad_rows(g, heads, real):
    n = g.shape[1]
    return g.reshape(heads, LANES, n)[:, :real, :].reshape(heads * real, n)


def _w_in_internal(w_in):
    c_q, c_kv, k_r, q_swa, k_swa, v_swa, q_sb, k_sb, v_sb, gate = jnp.split(w_in, SPLIT_POINTS, axis=1)
    k_r = jnp.pad(k_r, ((0, 0), (MLA_NOPE, LANES - MLA_NOPE - MLA_ROPE)))
    w1 = jnp.concatenate([c_q, c_kv, k_r, _pad_cols(q_swa, 8, 64), _pad_cols(k_swa, 2, 64)], axis=1)
    w2 = [_pad_cols(v_swa, 2, 64), _pad_cols(q_sb, 8, 64), _pad_cols(k_sb, 8, 64), _pad_cols(v_sb, 8, 64)]
    return w1, w2, gate


def _w_in_reference(g1, g2, g3):
    c_q, c_kv, k_r, q_swa, k_swa = jnp.split(g1, [256, 384, 512, 1536], axis=1)
    v_swa, q_sb, k_sb, v_sb = g2
    return jnp.concatenate([
        c_q, c_kv, k_r[:, MLA_NOPE:MLA_NOPE + MLA_ROPE], _unpad_cols(q_swa, 8, 64), _unpad_cols(k_swa, 2, 64),
        _unpad_cols(v_swa, 2, 64), _unpad_cols(q_sb, 8, 64), _unpad_cols(k_sb, 8, 64), _unpad_cols(v_sb, 8, 64),
        g3], axis=1)


def _w_ukv_internal(w):
    w3 = w.reshape(MLA_KV_LORA, MLA_HEADS, MLA_NOPE + MLA_V)
    pad = lambda t: jnp.pad(t, ((0, 0), (0, 0), (0, LANES - t.shape[2]))).reshape(MLA_KV_LORA, MLA_HEADS * LANES)
    return pad(w3[:, :, :MLA_NOPE]), pad(w3[:, :, MLA_NOPE:])


def _w_ukv_reference(gk, gv):
    gk = gk.reshape(MLA_KV_LORA, MLA_HEADS, LANES)[:, :, :MLA_NOPE]
    gv = gv.reshape(MLA_KV_LORA, MLA_HEADS, LANES)[:, :, :MLA_V]
    return jnp.concatenate([gk, gv], axis=2).reshape(MLA_KV_LORA, MLA_HEADS * (MLA_NOPE + MLA_V))


def _layer_fwd(x, w, tabs):
    mla_tab, swa_tab = tabs
    sv = {"x": x}

    def f_norm(rows, consts):
        return [_rms(rows[0], consts[0])], []

    (h,) = _rowwise(f_norm, [x], [w["g_mix_pre"]], [(D_MODEL, BF16)], [], "norm_mix_pre")
    p1 = _matmul(h, w["w_in1"], "nn", [F32], "proj_lat")
    p2 = _matmul(h, w["w_in2"], "nn", [BF16], "proj_qkv")
    gates = _matmul(h, w["w_in3"], "nn", [BF16], "proj_gate",
                    epilogue=lambda acc, b: (1.0 / (1.0 + jnp.exp(-(acc + b))),), row_extras=[w["b_gate"]])

    def f_prep(rows, consts):
        t = rows[0]
        gq, gkv = consts[0], consts[1]
        mc, mu, md = rows[1], rows[2], rows[3]
        sc, su, sd = rows[4], rows[5], rows[6]
        cq_n = _rms(t[:, 0:256], gq)
        ckv_n = _rms(t[:, 256:384], gkv)
        kr = _rope(t[:, 384:512], mc, mu, md, MLA_ROPE // 2)
        qs = [_rope(t[:, 512 + j * LANES:512 + (j + 1) * LANES], sc, su, sd, SWA_HEAD_DIM // 2) for j in range(8)]
        ks = [_rope(t[:, 1536 + j * LANES:1536 + (j + 1) * LANES], sc, su, sd, SWA_HEAD_DIM // 2) for j in range(2)]
        return [cq_n, ckv_n, kr, jnp.concatenate(qs, axis=1), jnp.concatenate(ks, axis=1)], []

    cq_n, ckv_n, kr, q_swa, k_swa = _rowwise(
        f_prep, [p1, *mla_tab["k"], *swa_tab["f"]], [w["g_q_lat"], w["g_kv_lat"]],
        [(256, BF16), (128, BF16), (LANES, F32), (1024, BF16), (256, BF16)], [], "lat_prep")

    q_lat = _matmul(cq_n, w["w_uq"], "nn", [F32], "mla_q_up")
    k_lat = _matmul(ckv_n, w["w_ukv_k"], "nn", [F32], "mla_k_up")
    v_mla = _matmul(ckv_n, w["w_ukv_v"], "nn", [BF16], "mla_v_up")

    def f_mla_prep(rows, consts):
        ql, kl, krr, mc, mu, md = rows
        qs = [_rope(ql[:, j * LANES:(j + 1) * LANES], mc, mu, md, MLA_ROPE // 2) for j in range(8)]
        ks = [kl[:, j * LANES:(j + 1) * LANES] + krr for j in range(8)]
        return [jnp.concatenate(qs, axis=1), jnp.concatenate(ks, axis=1)], []

    q_mla, k_mla = _rowwise(f_mla_prep, [q_lat, k_lat, kr, *mla_tab["q"]], [], [(1024, BF16), (1024, BF16)], [], "mla_prep")

    o_mla, lse_mla = _softmax_attn_fwd(q_mla, k_mla, v_mla, MLA_HEADS, (MLA_NOPE + MLA_ROPE) ** -0.5, "mla_fwd")
    o_swa, lse_swa = _swa_fwd(q_swa, k_swa, p2, 0, w["sink_b"], "swa_fwd")
    o_sb = _sb_attn_fwd(p2, SB_HEADS, "sb_fwd", 2, 10, 18)

    oa = _matmul(o_mla, w["w_o_mla"], "nn", [F32], "o_proj_mla")
    ob = _matmul(o_swa, w["w_o_swa"], "nn", [F32], "o_proj_swa")
    oc = _matmul(o_sb, w["w_o_sb"], "nn", [F32], "o_proj_sb")

    def f_mix(rows, consts):
        a, b, c, g = rows
        g = g.astype(F32)
        return [g[:, 0:1024] * a + g[:, 1024:2048] * b + g[:, 2048:3072] * c], []

    (mixed,) = _rowwise(f_mix, [oa, ob, oc, gates], [], [(D_MODEL, BF16)], [], "gate_mix")
    y = _matmul(mixed, w["w_out"], "nn", [F32], "out_proj")

    def f_res_norm(rows, consts):
        return [rows[0] + _rms(rows[1], consts[0])], []

    (x1,) = _rowwise(f_res_norm, [x, y], [w["g_mix_post"]], [(D_MODEL, F32)], [], "res_norm_mix")
    (h2,) = _rowwise(f_norm, [x1], [w["g_mlp_pre"]], [(D_MODEL, BF16)], [], "norm_mlp_pre")

    def relu2(acc):
        r = jnp.maximum(acc, 0.0)
        return acc, r * r

    up, u = _matmul(h2, w["w_up"], "nn", [BF16, BF16], "mlp_up", epilogue=relu2)
    zd = _matmul(u, w["w_down"], "nn", [F32], "mlp_down")
    (x2,) = _rowwise(f_res_norm, [x1, zd], [w["g_mlp_post"]], [(D_MODEL, F32)], [], "res_norm_mlp")

    sv.update(h=h, p1=p1, p2=p2, gates=gates, cq_n=cq_n, ckv_n=ckv_n, q_swa=q_swa, k_swa=k_swa, q_mla=q_mla,
              k_mla=k_mla, v_mla=v_mla, o_mla=o_mla, lse_mla=lse_mla, o_swa=o_swa, lse_swa=lse_swa, o_sb=o_sb,
              oa=oa, ob=ob, oc=oc, mixed=mixed, y=y, x1=x1, h2=h2, up=up, u=u, zd=zd)
    return x2, sv


def _layer_bwd(dx2, w, sv, tabs):
    mla_tab, swa_tab = tabs
    gr = {}

    def f_norm_bwd(rows, consts):
        dx, dg = _rms_bwd(rows[0], consts[0], rows[1])
        return [dx], [dg]

    def f_norm_bwd_res(rows, consts):
        dx, dg = _rms_bwd(rows[0], consts[0], rows[1])
        return [rows[2] + dx], [dg]

    dzd, gr["g_mlp_post"] = _rowwise(f_norm_bwd, [sv["zd"], dx2], [w["g_mlp_post"]], [(D_MODEL, BF16)], [D_MODEL], "b_norm_mlp_post")
    gr["w_down"] = _matmul(sv["u"], dzd, "tn", [F32], "b_w_down")
    dup = _matmul(dzd, w["w_down"], "nt", [BF16], "b_mlp_down",
                  epilogue=lambda acc, up: (acc * 2.0 * jnp.maximum(up.astype(F32), 0.0),), extras=[sv["up"]])
    gr["w_up"] = _matmul(sv["h2"], dup, "tn", [F32], "b_w_up")
    dh2 = _matmul(dup, w["w_up"], "nt", [F32], "b_mlp_up")
    dx1, gr["g_mlp_pre"] = _rowwise(f_norm_bwd_res, [sv["x1"], dh2, dx2], [w["g_mlp_pre"]], [(D_MODEL, F32)], [D_MODEL], "b_norm_mlp_pre")

    dy, gr["g_mix_post"] = _rowwise(f_norm_bwd, [sv["y"], dx1], [w["g_mix_post"]], [(D_MODEL, BF16)], [D_MODEL], "b_norm_mix_post")
    gr["w_out"] = _matmul(sv["mixed"], dy, "tn", [F32], "b_w_out")
    dmixed = _matmul(dy, w["w_out"], "nt", [F32], "b_out_proj")

    def f_mix_bwd(rows, consts):
        dm, a, b, c, g = rows
        g = g.astype(F32)
        outs, dls = [], []
        for j, o in enumerate((a, b, c)):
            gj = g[:, j * D_MODEL:(j + 1) * D_MODEL]
            outs.append(dm * gj)
            dls.append(dm * o * gj * (1.0 - gj))
        dl = jnp.concatenate(dls, axis=1)
        return outs + [dl], [dl]

    doa, dob, doc, dlogit, gr["b_gate"] = _rowwise(
        f_mix_bwd, [dmixed, sv["oa"], sv["ob"], sv["oc"], sv["gates"]], [],
        [(D_MODEL, BF16)] * 3 + [(P3_W, BF16)], [P3_W], "b_gate_mix")

    gr["w_o_mla"] = _matmul(sv["o_mla"], doa, "tn", [F32], "b_w_o_mla")
    gr["w_o_swa"] = _matmul(sv["o_swa"], dob, "tn", [F32], "b_w_o_swa")
    gr["w_o_sb"] = _matmul(sv["o_sb"], doc, "tn", [F32], "b_w_o_sb")
    do_mla = _matmul(doa, w["w_o_mla"], "nt", [BF16], "b_o_proj_mla")
    do_swa = _matmul(dob, w["w_o_swa"], "nt", [BF16], "b_o_proj_swa")
    do_sb = _matmul(doc, w["w_o_sb"], "nt", [BF16], "b_o_proj_sb")

    dq_sb, dk_sb, dv_sb = _sb_attn_bwd(sv["p2"], sv["o_sb"], do_sb, SB_HEADS, "sb_bwd", 2, 10, 18)
    dq_swa, dk_swa, dv_swa, dsink = _swa_bwd(sv["q_swa"], sv["k_swa"], sv["p2"], 0, w["sink_b"], sv["o_swa"],
                                             sv["lse_swa"], do_swa, "swa_bwd")
    gr["swa_sinks"] = dsink.reshape(SWA_HEADS, LANES)[:, 0]
    dq_mla, dk_mla, dv_mla = _softmax_attn_bwd(sv["q_mla"], sv["k_mla"], sv["v_mla"], sv["o_mla"], sv["lse_mla"], do_mla,
                                               MLA_HEADS, (MLA_NOPE + MLA_ROPE) ** -0.5, "mla_bwd")

    def f_mla_post(rows, consts):
        dq, dk, qc, qu, qd, kc, ku, kd = rows
        dqs = [_rope(dq[:, j * LANES:(j + 1) * LANES], qc, qu, qd, MLA_ROPE // 2) for j in range(8)]
        dkr = dk[:, 0:LANES]
        for j in range(1, 8):
            dkr = dkr + dk[:, j * LANES:(j + 1) * LANES]
        return [jnp.concatenate(dqs, axis=1), _rope(dkr, kc, ku, kd, MLA_ROPE // 2)], []

    dq_lat, dkr = _rowwise(f_mla_post, [dq_mla, dk_mla, *mla_tab["q_inv"], *mla_tab["k_inv"]], [],
                           [(1024, BF16), (LANES, F32)], [], "b_mla_post")
    gr["w_uq"] = _matmul(sv["cq_n"], dq_lat, "tn", [F32], "b_w_uq")
    gr["w_ukv_k"] = _matmul(sv["ckv_n"], dk_mla, "tn", [F32], "b_w_ukv_k")
    gr["w_ukv_v"] = _matmul(sv["ckv_n"], dv_mla, "tn", [F32], "b_w_ukv_v")
    dcq_n = _matmul(dq_lat, w["w_uq"], "nt", [F32], "b_mla_q_up")
    dckv_a = _matmul(dk_mla, w["w_ukv_k"], "nt", [F32], "b_mla_k_up")
    dckv_b = _matmul(dv_mla, w["w_ukv_v"], "nt", [F32], "b_mla_v_up")

    def f_prep_bwd(rows, consts):
        t, dcq, dca, dcb, dkr_, dqs, dks, sc, su, sd = rows
        gq, gkv = consts
        dc_q, dgq = _rms_bwd(t[:, 0:256], gq, dcq)
        dc_kv, dgkv = _rms_bwd(t[:, 256:384], gkv, dca + dcb)
        q_parts = [_rope(dqs[:, j * LANES:(j + 1) * LANES], sc, su, sd, SWA_HEAD_DIM // 2) for j in range(8)]
        k_parts = [_rope(dks[:, j * LANES:(j + 1) * LANES], sc, su, sd, SWA_HEAD_DIM // 2) for j in range(2)]
        return [jnp.concatenate([dc_q, dc_kv, dkr_] + q_parts + k_parts, axis=1)], [dgq, dgkv]

    dp1, gr["g_q_lat"], gr["g_kv_lat"] = _rowwise(
        f_prep_bwd, [sv["p1"], dcq_n, dckv_a, dckv_b, dkr, dq_swa, dk_swa, *swa_tab["inv"]], [w["g_q_lat"], w["g_kv_lat"]],
        [(P1_W, BF16)], [256, 128], "b_lat_prep")

    gr["w_in1"] = _matmul(sv["h"], dp1, "tn", [F32], "b_w_in_lat")
    dh = _matmul(dp1, w["w_in1"], "nt", [F32], "b_proj_lat")
    gr["w_in2"] = []
    add_prev = lambda acc, prev: (acc + prev,)
    for piece, wp, tag in zip((dv_swa, dq_sb, dk_sb, dv_sb), w["w_in2_parts"], ("vswa", "qsb", "ksb", "vsb")):
        gr["w_in2"].append(_matmul(sv["h"], piece, "tn", [F32], "b_w_in_" + tag))
        dh = _matmul(piece, wp, "nt", [F32], "b_proj_" + tag, epilogue=add_prev, extras=[dh])
    gr["w_in3"] = _matmul(sv["h"], dlogit, "tn", [F32], "b_w_in_gate")
    dh = _matmul(dlogit, w["w_in3"], "nt", [F32], "b_proj_gate", epilogue=add_prev, extras=[dh])
    dx, gr["g_mix_pre"] = _rowwise(f_norm_bwd_res, [sv["x"], dh, dx1], [w["g_mix_pre"]], [(D_MODEL, F32)], [D_MODEL], "b_norm_mix_pre")
    return dx, gr


def _local_step(x, positions, loss_target, full):
    mc, mu, md = _rope_tables(positions, MLA_NOPE, MLA_ROPE, True)
    kc, ku, kd = _rope_tables(positions, MLA_NOPE, MLA_ROPE, False)
    sc, su, sd = _rope_tables(positions, 0, SWA_HEAD_DIM, False)
    mla_tab = {"q": (mc, mu, md), "k": (kc, ku, kd), "q_inv": (mc, -mu, -md), "k_inv": (kc, -ku, -kd)}
    swa_tab = {"f": (sc, su, sd), "inv": (sc, -su, -sd)}
    tabs = (mla_tab, swa_tab)

    layers = []
    for l in range(DEPTH):
        w1, w2, w3 = _w_in_internal(full["w_in"][l].astype(BF16))
        uk, uv = _w_ukv_internal(full["w_ukv"][l].astype(BF16))
        layers.append({
            "w_in1": w1, "w_in2": jnp.concatenate(w2, axis=1), "w_in2_parts": w2, "w_in3": w3,
            "w_uq": _pad_cols(full["w_uq"][l].astype(BF16), MLA_HEADS, MLA_NOPE + MLA_ROPE),
            "w_ukv_k": uk, "w_ukv_v": uv,
            "w_o_mla": _pad_rows(full["w_o_mla"][l].astype(BF16), 8, 64),
            "w_o_swa": _pad_rows(full["w_o_swa"][l].astype(BF16), 8, 64),
            "w_o_sb": _pad_rows(full["w_o_sb"][l].astype(BF16), 8, 64),
            "w_out": full["w_out"][l].astype(BF16), "w_up": full["w_up"][l].astype(BF16),
            "w_down": full["w_down"][l].astype(BF16),
            "g_mix_pre": full["g_mix_pre"][l][None], "b_gate": full["b_gate"][l][None],
            "g_q_lat": full["g_q_lat"][l][None], "g_kv_lat": full["g_kv_lat"][l][None],
            "g_mix_post": full["g_mix_post"][l][None], "g_mlp_pre": full["g_mlp_pre"][l][None],
            "g_mlp_post": full["g_mlp_post"][l][None],
            "sink_b": jnp.repeat(full["swa_sinks"][l], LANES)[None],
        })

    saved = []
    h = x
    for l in range(DEPTH):
        h, sv = _layer_fwd(h, layers[l], tabs)
        saved.append(sv)

    def f_loss(rows, consts):
        err = rows[0] - rows[1]
        return [err * (1.0 / D_MODEL)], [jnp.sum(err * err, axis=1, keepdims=True)]

    dy, sq = _rowwise(f_loss, [h, loss_target], [], [(D_MODEL, F32)], [1], "loss_head")
    loss_part = sq * (0.5 / D_MODEL)

    grads = [None] * DEPTH
    d = dy
    for l in reversed(range(DEPTH)):
        d, gr = _layer_bwd(d, layers[l], saved[l], tabs)
        grads[l] = {
            "g_mix_pre": gr["g_mix_pre"][0], "w_in": _w_in_reference(gr["w_in1"], gr["w_in2"], gr["w_in3"]),
            "b_gate": gr["b_gate"][0], "g_q_lat": gr["g_q_lat"][0], "g_kv_lat": gr["g_kv_lat"][0],
            "w_uq": _unpad_cols(gr["w_uq"], MLA_HEADS, MLA_NOPE + MLA_ROPE),
            "w_ukv": _w_ukv_reference(gr["w_ukv_k"], gr["w_ukv_v"]), "swa_sinks": gr["swa_sinks"],
            "w_o_mla": _unpad_rows(gr["w_o_mla"], 8, 64), "w_o_swa": _unpad_rows(gr["w_o_swa"], 8, 64),
            "w_o_sb": _unpad_rows(gr["w_o_sb"], 8, 64), "w_out": gr["w_out"], "g_mix_post": gr["g_mix_post"][0],
            "g_mlp_pre": gr["g_mlp_pre"][0], "w_up": gr["w_up"], "w_down": gr["w_down"], "g_mlp_post": gr["g_mlp_post"][0],
        }
    stacked = {n: jnp.stack([grads[l][n] for l in range(DEPTH)]) for n in WEIGHTS}
    return loss_part, d, stacked


def _rows_of(a):
    return a.reshape(-1, LANES)


def _small_rows(d):
    parts = []
    for n in SMALL:
        a = d[n]
        if a.shape[1] < LANES:
            a = jnp.pad(a, ((0, 0), (0, LANES - a.shape[1])))
        parts.append(_rows_of(a))
    return parts


def _pack(shards, small, dtype):
    parts = [_rows_of(shards[n]) for n in SHARDED]
    if small is not None:
        parts += _small_rows(small)
    slab = jnp.concatenate(parts, axis=0).astype(dtype)
    pad = (-slab.shape[0]) % SLAB_ROW_ALIGN
    return jnp.pad(slab, ((0, pad), (0, 0)))


def _unpack(slab, shard_shapes, small_shapes):
    out, r = {}, 0
    for n in SHARDED:
        rows = int(np.prod(shard_shapes[n])) // LANES
        out[n] = slab[r:r + rows].reshape(shard_shapes[n])
        r += rows
    if small_shapes is not None:
        for n in SMALL:
            depth, width = small_shapes[n]
            rows = depth * max(width, LANES) // LANES
            out[n] = slab[r:r + rows].reshape(depth, max(width, LANES))[:, :width]
            r += rows
    return out


def _chip_exchange(src, per_chip_src, name):
    rows = src.shape[-2]

    def body(src_ref, out_ref, send_sems, recv_sems, local_sem):
        x, y, c = lax.axis_index("x"), lax.axis_index("y"), lax.axis_index("c")
        me = 2 * x + y
        chips = [(1 - x, y), (x, 1 - y), (1 - x, 1 - y)]
        pick = (lambda j: src_ref.at[j]) if per_chip_src else (lambda j: src_ref)
        mine = pltpu.make_async_copy(pick(me), out_ref.at[me], local_sem)
        mine.start()
        sends = []
        for k, (cx, cy) in enumerate(chips):
            cp = pltpu.make_async_remote_copy(
                src_ref=pick(2 * cx + cy), dst_ref=out_ref.at[me], send_sem=send_sems.at[k], recv_sem=recv_sems.at[k],
                device_id=(cx, cy, c), device_id_type=pl.DeviceIdType.MESH)
            cp.start()
            sends.append(cp)
        for k, (cx, cy) in enumerate(chips):
            pltpu.make_async_remote_copy(
                src_ref=pick(me), dst_ref=out_ref.at[2 * cx + cy], send_sem=send_sems.at[k], recv_sem=recv_sems.at[k],
                device_id=(cx, cy, c), device_id_type=pl.DeviceIdType.MESH).wait_recv()
        for cp in sends:
            cp.wait_send()
        mine.wait()

    return pl.pallas_call(
        body,
        name=name,
        in_specs=[pl.BlockSpec(memory_space=pl.ANY)],
        out_specs=pl.BlockSpec(memory_space=pl.ANY),
        out_shape=jax.ShapeDtypeStruct((N_CHIPS, rows, LANES), src.dtype),
        scratch_shapes=[pltpu.SemaphoreType.DMA((3,)), pltpu.SemaphoreType.DMA((3,)), pltpu.SemaphoreType.DMA],
    )(src)


def _sibling_exchange(src, name):
    def body(src_ref, out_ref, send_sem, recv_sem):
        peer = (lax.axis_index("x"), lax.axis_index("y"), 1 - lax.axis_index("c"))
        cp = pltpu.make_async_remote_copy(src_ref=src_ref, dst_ref=out_ref, send_sem=send_sem, recv_sem=recv_sem,
                                          device_id=peer, device_id_type=pl.DeviceIdType.MESH)
        cp.start()
        cp.wait()

    return pl.pallas_call(
        body,
        name=name,
        in_specs=[pl.BlockSpec(memory_space=pl.ANY)],
        out_specs=pl.BlockSpec(memory_space=pl.ANY),
        out_shape=jax.ShapeDtypeStruct(src.shape, src.dtype),
        scratch_shapes=[pltpu.SemaphoreType.DMA, pltpu.SemaphoreType.DMA],
    )(src)


def _sum_chips(buf, name):
    rows = buf.shape[1]
    bm = 2048 if rows % 2048 == 0 else SLAB_ROW_ALIGN

    def body(b_ref, o_ref):
        o_ref[...] = ((b_ref[0] + b_ref[1]) + b_ref[2]) + b_ref[3]

    return pl.pallas_call(
        body,
        name=name,
        grid=(rows // bm,),
        in_specs=[pl.BlockSpec((N_CHIPS, bm, LANES), lambda i: (0, i, 0))],
        out_specs=pl.BlockSpec((bm, LANES), lambda i: (i, 0)),
        out_shape=jax.ShapeDtypeStruct((rows, LANES), F32),
        compiler_params=_params(("arbitrary",)),
    )(buf)


def _adamw(w, m, v, g_mine, g_sibling, name):
    def fn(rows, consts):
        w_, m_, v_, ga, gb = rows
        g = ga + gb
        m_new = ADAM_B1 * m_ + (1.0 - ADAM_B1) * g
        v_new = ADAM_B2 * v_ + (1.0 - ADAM_B2) * (g * g)
        m_hat = m_new / (1.0 - ADAM_B1 ** ADAM_STEP)
        v_hat = v_new / (1.0 - ADAM_B2 ** ADAM_STEP)
        delta = -ADAM_LR * (m_hat / (jnp.sqrt(v_hat) + ADAM_EPS) + ADAM_WD * w_)
        return [g, delta, m_new, v_new], []

    return _rowwise(fn, [w, m, v, g_mine, g_sibling], [], [(LANES, F32)] * 4, [], name, bm=2048)


def kernel(x, positions, g_mix_pre, w_in, b_gate, g_q_lat, g_kv_lat, w_uq, w_ukv, swa_sinks, w_o_mla, w_o_swa, w_o_sb, w_out, g_mix_post, g_mlp_pre, w_up, w_down, g_mlp_post, loss_target, m_g_mix_pre, m_w_in, m_b_gate, m_g_q_lat, m_g_kv_lat, m_w_uq, m_w_ukv, m_swa_sinks, m_w_o_mla, m_w_o_swa, m_w_o_sb, m_w_out, m_g_mix_post, m_g_mlp_pre, m_w_up, m_w_down, m_g_mlp_post, v_g_mix_pre, v_w_in, v_b_gate, v_g_q_lat, v_g_kv_lat, v_w_uq, v_w_ukv, v_swa_sinks, v_w_o_mla, v_w_o_swa, v_w_o_sb, v_w_out, v_g_mix_post, v_g_mlp_pre, v_w_up, v_w_down, v_g_mlp_post):
    given = dict(locals())
    wts = {n: given[n] for n in WEIGHTS}
    mom_m = {n: given["m_" + n] for n in WEIGHTS}
    mom_v = {n: given["v_" + n] for n in WEIGHTS}
    shard_shapes = {n: wts[n].shape for n in SHARDED}
    small_shapes = {n: wts[n].shape for n in SMALL}

    gathered = _chip_exchange(_pack(wts, None, BF16), False, "gather_weights")
    full = {n: wts[n] for n in SMALL}
    per_chip = [_unpack(gathered[j], shard_shapes, None) for j in range(N_CHIPS)]
    for n in SHARDED:
        full[n] = jnp.concatenate([per_chip[j][n] for j in range(N_CHIPS)], axis=SHARD_AXIS[n])

    loss_part, grad_x, grads = _local_step(x[0], positions[0], loss_target[0], full)
    loss = lax.psum(loss_part[0, 0], ("x", "y", "c"))

    small_g = {n: grads[n] for n in SMALL}
    slabs = []
    for j in range(N_CHIPS):
        shard = {n: jnp.split(grads[n], N_CHIPS, axis=SHARD_AXIS[n])[j] for n in SHARDED}
        slabs.append(_pack(shard, small_g, F32))
    landed = _chip_exchange(jnp.stack(slabs), True, "scatter_grads")
    g_mine = _sum_chips(landed, "sum_chips")
    g_sibling = _sibling_exchange(g_mine, "sibling_grads")

    g_slab, d_slab, m_slab, v_slab = _adamw(
        _pack(wts, wts, F32), _pack(mom_m, mom_m, F32), _pack(mom_v, mom_v, F32), g_mine, g_sibling, "adamw")
    outs = [loss, grad_x[None]]
    for slab in (g_slab, d_slab, m_slab, v_slab):
        un = _unpack(slab, shard_shapes, small_shapes)
        outs += [un[n] for n in WEIGHTS]
    return tuple(outs)
```

```python
import numpy as np
import jax
import jax.numpy as jnp
from jax import lax
from jax.experimental import pallas as pl
from jax.experimental.pallas import tpu as pltpu

F32 = jnp.float32
BF16 = jnp.bfloat16

D_MODEL = 1024
DEPTH = 4
MLA_HEADS, MLA_Q_LORA, MLA_KV_LORA, MLA_NOPE, MLA_ROPE, MLA_V = 8, 256, 128, 64, 32, 64
SWA_HEADS, SWA_KV_HEADS, SWA_HEAD_DIM, SWA_WINDOW = 8, 2, 64, 128
SB_HEADS, SB_HEAD_DIM = 8, 64
D_FF = 4 * D_MODEL
ROPE_THETA = 10000.0
EPS = 1e-6
SPLIT_SIZES = (256, 128, 32, 512, 128, 128, 512, 512, 512, 3 * D_MODEL)
SPLIT_POINTS = [int(v) for v in np.cumsum(SPLIT_SIZES)[:-1]]

ADAM_LR, ADAM_B1, ADAM_B2, ADAM_EPS, ADAM_WD, ADAM_STEP = 0.001, 0.9, 0.999, 1e-08, 0.01, 10

LANES = 128
V7X_VMEM_BYTES = 64 * 1024 * 1024
VMEM_LIMIT = V7X_VMEM_BYTES - 8 * 1024 * 1024
MATMUL_VMEM_BUDGET = 36 * 1024 * 1024
N_CHIPS = 4
SLAB_ROW_ALIGN = 512

P1_W = 256 + 128 + 128 + 1024 + 256
P2_W = 256 + 1024 + 1024 + 1024
P3_W = 3 * D_MODEL

SHARDED = ("w_in", "w_uq", "w_ukv", "w_o_mla", "w_o_swa", "w_o_sb", "w_out", "w_up", "w_down")
SHARD_AXIS = {"w_in": 2, "w_uq": 2, "w_ukv": 2, "w_o_mla": 2, "w_o_swa": 2, "w_o_sb": 2, "w_out": 1, "w_up": 2, "w_down": 1}
SMALL = ("g_mix_pre", "b_gate", "g_q_lat", "g_kv_lat", "swa_sinks", "g_mix_post", "g_mlp_pre", "g_mlp_post")
WEIGHTS = ("g_mix_pre", "w_in", "b_gate", "g_q_lat", "g_kv_lat", "w_uq", "w_ukv", "swa_sinks", "w_o_mla", "w_o_swa",
           "w_o_sb", "w_out", "g_mix_post", "g_mlp_pre", "w_up", "w_down", "g_mlp_post")

NN = (((1,), (0,)), ((), ()))
NT = (((1,), (1,)), ((), ()))
TN = (((0,), (0,)), ((), ()))


def _dot(a, b, dims):
    return lax.dot_general(a, b, dims, preferred_element_type=F32)


def _params(sem):
    return pltpu.CompilerParams(dimension_semantics=sem, vmem_limit_bytes=VMEM_LIMIT)


def _largest_tile(n, cap):
    if n <= cap:
        return n
    best = LANES
    for t in range(LANES, cap + 1, LANES):
        if n % t == 0:
            best = t
    return best


def _matmul_tiles(M, N, K, a_bytes, b_bytes, out_bytes, extra_bytes):
    tn = _largest_tile(N, 1792)
    tm = _largest_tile(M, 1024 if tn <= 1024 else 512)
    tk = _largest_tile(K, 2048)

    def need(tm_, tk_):
        acc = 4 * tm_ * tn if tk_ < K else 0
        return 2 * (tm_ * tk_ * a_bytes + tk_ * tn * b_bytes + tm_ * tn * (out_bytes + extra_bytes)) + acc

    while need(tm, tk) > MATMUL_VMEM_BUDGET:
        if tk >= tm and tk % 256 == 0:
            tk //= 2
        elif tm % 256 == 0:
            tm //= 2
        else:
            break
    return tm, tn, tk


def _matmul(a, b, mode, out_dtypes, name, epilogue=None, extras=(), row_extras=()):
    if mode == "nn":
        (M, K), (K2, N) = a.shape, b.shape
    elif mode == "nt":
        (M, K), (N, K2) = a.shape, b.shape
    else:
        (K, M), (K2, N) = a.shape, b.shape
    assert K == K2, (name, a.shape, b.shape)
    tm, tn, tk = _matmul_tiles(
        M, N, K, a.dtype.itemsize, b.dtype.itemsize, sum(jnp.dtype(d).itemsize for d in out_dtypes),
        sum(e.dtype.itemsize for e in extras))
    assert M % tm == 0 and N % tn == 0 and K % tk == 0, (name, M, N, K, tm, tn, tk)
    nk = K // tk
    if mode == "tn":
        a_spec = pl.BlockSpec((tk, tm), lambda i, j, k: (k, i))
    else:
        a_spec = pl.BlockSpec((tm, tk), lambda i, j, k: (i, k))
    if mode == "nt":
        b_spec = pl.BlockSpec((tn, tk), lambda i, j, k: (j, k))
    else:
        b_spec = pl.BlockSpec((tk, tn), lambda i, j, k: (k, j))
    dims = {"nn": NN, "nt": NT, "tn": TN}[mode]
    n_ex, n_rex, n_out = len(extras), len(row_extras), len(out_dtypes)

    def body(*refs):
        a_ref, b_ref = refs[:2]
        ex = refs[2:2 + n_ex]
        rex = refs[2 + n_ex:2 + n_ex + n_rex]
        outs = refs[2 + n_ex + n_rex:2 + n_ex + n_rex + n_out]

        def finish(total):
            res = (total,) if epilogue is None else epilogue(total, *[e[...] for e in ex], *[e[...] for e in rex])
            for o, r in zip(outs, res):
                o[...] = r.astype(o.dtype)

        part = _dot(a_ref[...].astype(BF16), b_ref[...].astype(BF16), dims)
        if nk == 1:
            finish(part)
            return
        acc = refs[-1]
        k = pl.program_id(2)

        @pl.when(k == 0)
        def _():
            acc[...] = part

        @pl.when(k > 0)
        def _():
            acc[...] += part

        @pl.when(k == nk - 1)
        def _():
            finish(acc[...])

    in_specs = [a_spec, b_spec]
    in_specs += [pl.BlockSpec((tm, tn), lambda i, j, k: (i, j)) for _ in extras]
    in_specs += [pl.BlockSpec((1, tn), lambda i, j, k: (0, j)) for _ in row_extras]
    out = pl.pallas_call(
        body,
        name=name,
        grid=(M // tm, N // tn, nk),
        in_specs=in_specs,
        out_specs=[pl.BlockSpec((tm, tn), lambda i, j, k: (i, j)) for _ in out_dtypes],
        out_shape=[jax.ShapeDtypeStruct((M, N), dt) for dt in out_dtypes],
        scratch_shapes=[pltpu.VMEM((tm, tn), F32)] if nk > 1 else [],
        compiler_params=_params(("parallel", "parallel", "arbitrary")),
    )(a, b, *extras, *row_extras)
    return out[0] if n_out == 1 else out


def _rowwise(fn, rows, consts, out_defs, sum_widths, name, bm=256):
    R = rows[0].shape[0]
    bm = min(bm, R)
    assert R % bm == 0, (name, R, bm)
    n_r, n_c, n_o = len(rows), len(consts), len(out_defs)
    n_s = len(sum_widths)

    def body(*refs):
        r_in = refs[:n_r]
        c_in = refs[n_r:n_r + n_c]
        o_refs = refs[n_r + n_c:n_r + n_c + n_o]
        s_refs = refs[n_r + n_c + n_o:]
        outs, sums = fn([r[...] for r in r_in], [c[...] for c in c_in])
        for o, val in zip(o_refs, outs):
            o[...] = val.astype(o.dtype)
        if n_s:
            @pl.when(pl.program_id(0) == 0)
            def _():
                for s in s_refs:
                    s[...] = jnp.zeros_like(s)

            for s, val in zip(s_refs, sums):
                s[...] += jnp.sum(val, axis=0, keepdims=True)

    in_specs = [pl.BlockSpec((bm, r.shape[1]), lambda i: (i, 0)) for r in rows]
    in_specs += [pl.BlockSpec(c.shape, lambda i: (0, 0)) for c in consts]
    out_specs = [pl.BlockSpec((bm, w), lambda i: (i, 0)) for w, _ in out_defs]
    out_specs += [pl.BlockSpec((1, w), lambda i: (0, 0)) for w in sum_widths]
    out_shape = [jax.ShapeDtypeStruct((R, w), dt) for w, dt in out_defs]
    out_shape += [jax.ShapeDtypeStruct((1, w), F32) for w in sum_widths]
    return pl.pallas_call(
        body,
        name=name,
        grid=(R // bm,),
        in_specs=in_specs,
        out_specs=out_specs,
        out_shape=out_shape,
        compiler_params=_params(("arbitrary",)),
    )(*rows, *consts)


def _rms(x, g):
    r = lax.rsqrt(jnp.mean(x * x, axis=-1, keepdims=True) + EPS)
    return x * r * g


def _rms_bwd(x, g, dy):
    r = lax.rsqrt(jnp.mean(x * x, axis=-1, keepdims=True) + EPS)
    n = x * r
    dn = dy * g
    dx = r * (dn - n * jnp.mean(dn * n, axis=-1, keepdims=True))
    return dx, dy * n


def _rope(x, c, s_up, s_dn, half):
    return x * c + pltpu.roll(x, half, 1) * s_up + pltpu.roll(x, LANES - half, 1) * s_dn


def _rope_tables(positions, lo, d, nope_pass):
    S = positions.shape[0]
    half = d // 2
    inv = 1.0 / (ROPE_THETA ** (jnp.arange(0, d, 2, dtype=F32) / d))
    ang = positions.astype(F32)[:, None] * inv
    cos, sin = jnp.cos(ang), jnp.sin(ang)
    z = lambda n: jnp.zeros((S, n), F32)
    head = jnp.ones((S, lo), F32) if nope_pass else z(lo)
    tail = LANES - lo - d
    c = jnp.concatenate([head, cos, cos, z(tail)], axis=1)
    s_up = jnp.concatenate([z(lo), z(half), sin, z(tail)], axis=1)
    s_dn = jnp.concatenate([z(lo), -sin, z(half), z(tail)], axis=1)
    return c, s_up, s_dn


MLA_FWD_CFG = (1, 1024)
MLA_BWD_CFG = (2, 512)
SB_FWD_CFG = (8, 256)
SB_BWD_CFG = (4, 256)


def _tile_mask(bk, strict):
    row = lax.broadcasted_iota(jnp.int32, (bk, bk), 0)
    col = lax.broadcasted_iota(jnp.int32, (bk, bk), 1)
    return (col < row) if strict else (col <= row)


def _att_layout(S, cfg):
    nch, bk = cfg
    bq = nch * bk
    assert S % bq == 0, (S, cfg)
    rows = [slice(r * bk, (r + 1) * bk) for r in range(nch)]
    q_spec = lambda off=0: pl.BlockSpec((bq, LANES), lambda h, i: (i, off + h))
    kv_spec = lambda off=0: pl.BlockSpec((S, LANES), lambda h, i: (0, off + h))
    return bq, rows, q_spec, kv_spec


def _total(terms):
    terms = list(terms)
    out = terms[0]
    for t in terms[1:]:
        out = out + t
    return out


def _walk(nch, i, step, carry, leftward):
    everyone = range(nch)
    if leftward:
        for d in reversed(everyone):
            carry = step(nch * i + d, carry, range(d, nch), {d})
        return lax.fori_loop(0, nch * i, lambda t, c: step(nch * i - 1 - t, c, everyone, set()), carry)
    carry = lax.fori_loop(0, nch * i, lambda kb, c: step(kb, c, everyone, set()), carry)
    for d in everyone:
        carry = step(nch * i + d, carry, range(d, nch), {d})
    return carry


def _softmax_attn_fwd(q, k, v, heads, scale, name, q_off=0, k_off=0, v_off=0):
    S = q.shape[0]
    nch, bk = MLA_FWD_CFG
    bq, rows, q_spec, kv_spec = _att_layout(S, MLA_FWD_CFG)

    def body(q_ref, k_ref, v_ref, o_ref, lse_ref):
        i = pl.program_id(1)
        qs = [q_ref[rw, :] for rw in rows]

        def step(kb, cs, active, masked):
            off = pl.multiple_of(kb * bk, bk)
            ks, vs = k_ref[pl.ds(off, bk), :], v_ref[pl.ds(off, bk), :]
            A = list(active)
            s = {r: _dot(qs[r], ks, NT) * scale for r in A}
            s = {r: (jnp.where(_tile_mask(bk, False), s[r], -1e30) if r in masked else s[r]) for r in A}
            m_new = {r: jnp.maximum(cs[r][0], jnp.max(s[r], axis=1, keepdims=True)) for r in A}
            p = {r: jnp.exp(s[r] - m_new[r]) for r in A}
            alpha = {r: jnp.exp(cs[r][0] - m_new[r]) for r in A}
            new = list(cs)
            for r in A:
                new[r] = (m_new[r], alpha[r] * cs[r][1] + jnp.sum(p[r], axis=1, keepdims=True),
                          alpha[r] * cs[r][2] + _dot(p[r].astype(BF16), vs, NN))
            return tuple(new)

        init = (jnp.full((bk, 1), -1e30, F32), jnp.zeros((bk, 1), F32), jnp.zeros((bk, LANES), F32))
        cs = _walk(nch, i, step, tuple(init for _ in rows), False)
        for r, (m, l, acc) in enumerate(cs):
            o_ref[rows[r], :] = (acc / l).astype(o_ref.dtype)
            lse_ref[rows[r], :] = m + jnp.log(l)

    return pl.pallas_call(
        body,
        name=name,
        grid=(heads, S // bq),
        in_specs=[q_spec(q_off), kv_spec(k_off), kv_spec(v_off)],
        out_specs=[q_spec(), pl.BlockSpec((None, bq, 1), lambda h, i: (h, i, 0))],
        out_shape=[jax.ShapeDtypeStruct((S, heads * LANES), BF16), jax.ShapeDtypeStruct((heads, S, 1), F32)],
        compiler_params=_params(("parallel", "arbitrary")),
    )(q, k, v)


def _softmax_attn_bwd(q, k, v, o, lse, do, heads, scale, name, q_off=0, k_off=0, v_off=0):
    S = q.shape[0]
    nch, bk = MLA_BWD_CFG
    bq, rows, q_spec, kv_spec = _att_layout(S, MLA_BWD_CFG)

    def body(q_ref, k_ref, v_ref, o_ref, lse_ref, do_ref, dq_ref, dk_ref, dv_ref):
        i = pl.program_id(1)

        @pl.when(i == 0)
        def _():
            dk_ref[...] = jnp.zeros_like(dk_ref)
            dv_ref[...] = jnp.zeros_like(dv_ref)

        qs = [q_ref[rw, :] for rw in rows]
        dos = [do_ref[rw, :] for rw in rows]
        lses = [lse_ref[rw, :] for rw in rows]
        deltas = [jnp.sum(dos[r].astype(F32) * o_ref[rows[r], :].astype(F32), axis=1, keepdims=True) for r in range(nch)]

        def step(kb, dqs, active, masked):
            off = pl.multiple_of(kb * bk, bk)
            ks, vs = k_ref[pl.ds(off, bk), :], v_ref[pl.ds(off, bk), :]
            A = list(active)
            s = {r: _dot(qs[r], ks, NT) * scale for r in A}
            s = {r: (jnp.where(_tile_mask(bk, False), s[r], -1e30) if r in masked else s[r]) for r in A}
            p = {r: jnp.exp(s[r] - lses[r]) for r in A}
            dp = {r: _dot(dos[r], vs, NT) for r in A}
            ds = {r: (p[r] * (dp[r] - deltas[r]) * scale).astype(BF16) for r in A}
            dv_c = _total(_dot(p[r].astype(BF16), dos[r], TN) for r in A)
            dk_c = _total(_dot(ds[r], qs[r], TN) for r in A)
            dk_ref[pl.ds(off, bk), :] += dk_c
            dv_ref[pl.ds(off, bk), :] += dv_c
            new = list(dqs)
            for r in A:
                new[r] = dqs[r] + _dot(ds[r], ks, NN)
            return tuple(new)

        dqs = _walk(nch, i, step, tuple(jnp.zeros((bk, LANES), F32) for _ in rows), False)
        for r in range(nch):
            dq_ref[rows[r], :] = dqs[r]

    return pl.pallas_call(
        body,
        name=name,
        grid=(heads, S // bq),
        in_specs=[q_spec(q_off), kv_spec(k_off), kv_spec(v_off), q_spec(),
                  pl.BlockSpec((None, bq, 1), lambda h, i: (h, i, 0)), q_spec()],
        out_specs=[q_spec(), kv_spec(), kv_spec()],
        out_shape=[jax.ShapeDtypeStruct((S, heads * LANES), F32)] * 3,
        compiler_params=_params(("parallel", "arbitrary")),
    )(q, k, v, o, lse, do)


def _tri(n, inclusive):
    r = lax.broadcasted_iota(jnp.int32, (n, n), 0)
    c = lax.broadcasted_iota(jnp.int32, (n, n), 1)
    return jnp.where((r >= c) if inclusive else (r > c), 1.0, 0.0).astype(BF16)


def _suffix_sum(x, tri):
    hi = x.astype(BF16)
    lo = (x - hi.astype(F32)).astype(BF16)
    return _dot(hi, tri, NN) + _dot(lo, tri, NN)


def _sb_logs(z):
    lg = jnp.log(1.0 + jnp.exp(-jnp.abs(z)))
    l1m = -(jnp.maximum(z, 0.0) + lg)
    return l1m, l1m + z


SB_SCALE = SB_HEAD_DIM ** -0.5
assert SB_SCALE == 0.125


def _sb_attn_fwd(qkv, heads, name, q_off, k_off, v_off):
    S = qkv.shape[0]
    nch, bk = SB_FWD_CFG
    bq, rows, q_spec, kv_spec = _att_layout(S, SB_FWD_CFG)

    def body(q_ref, k_ref, v_ref, o_ref):
        i = pl.program_id(1)
        qs = [q_ref[rw, :] * SB_SCALE for rw in rows]
        tri = _tri(bk, False)

        def step(kb, cs, active, masked):
            off = pl.multiple_of(kb * bk, bk)
            ks, vs = k_ref[pl.ds(off, bk), :], v_ref[pl.ds(off, bk), :]
            A = list(active)
            lg = {r: _sb_logs(_dot(qs[r], ks, NT)) for r in A}
            l1m = {r: (jnp.where(_tile_mask(bk, True), lg[r][0], 0.0) if r in masked else lg[r][0]) for r in A}
            suf = {r: _suffix_sum(l1m[r], tri) for r in A}
            ex = {r: lg[r][1] + cs[r][0] + suf[r] for r in A}
            ex = {r: (jnp.where(_tile_mask(bk, True), ex[r], -1e30) if r in masked else ex[r]) for r in A}
            ab = {r: jnp.exp(ex[r]).astype(BF16) for r in A}
            new = list(cs)
            for r in A:
                new[r] = (cs[r][0] + jnp.sum(l1m[r], axis=1, keepdims=True), cs[r][1] + _dot(ab[r], vs, NN))
            return tuple(new)

        init = (jnp.zeros((bk, 1), F32), jnp.zeros((bk, LANES), F32))
        cs = _walk(nch, i, step, tuple(init for _ in rows), True)
        for r in range(nch):
            o_ref[rows[r], :] = cs[r][1]

    return pl.pallas_call(
        body,
        name=name,
        grid=(heads, S // bq),
        in_specs=[q_spec(q_off), kv_spec(k_off), kv_spec(v_off)],
        out_specs=q_spec(),
        out_shape=jax.ShapeDtypeStruct((S, heads * LANES), F32),
        compiler_params=_params(("parallel", "arbitrary")),
    )(qkv, qkv, qkv)


def _sb_attn_bwd(qkv, o, do, heads, name, q_off, k_off, v_off):
    S = qkv.shape[0]
    nch, bk = SB_BWD_CFG
    bq, rows, q_spec, kv_spec = _att_layout(S, SB_BWD_CFG)

    def body(q_ref, k_ref, v_ref, o_ref, do_ref, dq_ref, dk_ref, dv_ref):
        i = pl.program_id(1)

        @pl.when(i == 0)
        def _():
            dk_ref[...] = jnp.zeros_like(dk_ref)
            dv_ref[...] = jnp.zeros_like(dv_ref)

        tri = _tri(bk, False)
        qs = [q_ref[rw, :] * SB_SCALE for rw in rows]
        dos = [do_ref[rw, :] for rw in rows]
        deltas = [jnp.sum(dos[r].astype(F32) * o_ref[rows[r], :], axis=1, keepdims=True) for r in range(nch)]

        def step(kb, cs, active, masked):
            off = pl.multiple_of(kb * bk, bk)
            ks, vs = k_ref[pl.ds(off, bk), :], v_ref[pl.ds(off, bk), :]
            A = list(active)
            lg = {r: _sb_logs(_dot(qs[r], ks, NT)) for r in A}
            l1m = {r: (jnp.where(_tile_mask(bk, True), lg[r][0], 0.0) if r in masked else lg[r][0]) for r in A}
            suf = {r: _suffix_sum(l1m[r], tri) for r in A}
            ex = {r: lg[r][1] + cs[r][0] + suf[r] for r in A}
            ex = {r: (jnp.where(_tile_mask(bk, True), ex[r], -1e30) if r in masked else ex[r]) for r in A}
            ab = {r: jnp.exp(ex[r]).astype(BF16) for r in A}
            da = {r: _dot(dos[r], vs, NT) for r in A}
            g = {r: ab[r].astype(F32) * da[r] for r in A}
            gs = {r: _suffix_sum(g[r], tri) for r in A}
            beta = {r: jnp.exp(lg[r][1]) for r in A}
            dz = {r: g[r] - beta[r] * (deltas[r] - cs[r][1] - gs[r]) for r in A}
            dz = {r: (jnp.where(_tile_mask(bk, True), dz[r], 0.0) if r in masked else dz[r]) for r in A}
            dzb = {r: dz[r].astype(BF16) for r in A}
            dv_c = _total(_dot(ab[r], dos[r], TN) for r in A)
            dk_c = _total(_dot(dzb[r], qs[r], TN) for r in A)
            dk_ref[pl.ds(off, bk), :] += dk_c
            dv_ref[pl.ds(off, bk), :] += dv_c
            new = list(cs)
            for r in A:
                new[r] = (cs[r][0] + jnp.sum(l1m[r], axis=1, keepdims=True),
                          cs[r][1] + jnp.sum(g[r], axis=1, keepdims=True), cs[r][2] + _dot(dzb[r], ks, NN))
            return tuple(new)

        zcol = jnp.zeros((bk, 1), F32)
        init = (zcol, zcol, jnp.zeros((bk, LANES), F32))
        cs = _walk(nch, i, step, tuple(init for _ in rows), True)
        for r in range(nch):
            dq_ref[rows[r], :] = cs[r][2] * SB_SCALE

    return pl.pallas_call(
        body,
        name=name,
        grid=(heads, S // bq),
        in_specs=[q_spec(q_off), kv_spec(k_off), kv_spec(v_off), q_spec(), q_spec()],
        out_specs=[q_spec(), kv_spec(), kv_spec()],
        out_shape=[jax.ShapeDtypeStruct((S, heads * LANES), F32)] * 3,
        compiler_params=_params(("parallel", "arbitrary")),
    )(qkv, qkv, qkv, o, do)


SWA_BLK = 128
SWA_GROUP = SWA_HEADS // SWA_KV_HEADS


def _swa_band_mask(n):
    row = lax.broadcasted_iota(jnp.int32, (SWA_BLK, 2 * SWA_BLK), 0)
    col = lax.broadcasted_iota(jnp.int32, (SWA_BLK, 2 * SWA_BLK), 1)
    return (col > row) & (col <= row + SWA_WINDOW) & ((n > 0) | (col >= SWA_BLK))


def _swa_fwd(q, k, v, v_off, sink_b, name):
    S = q.shape[0]
    nb = S // SWA_BLK
    scale = SWA_HEAD_DIM ** -0.5
    gw = SWA_GROUP * LANES

    def body(q_ref, kp_ref, kc_ref, vp_ref, vc_ref, sink_ref, o_ref, lse_ref):
        n = pl.program_id(1)
        kband = jnp.concatenate([kp_ref[...], kc_ref[...]], axis=0)
        vband = jnp.concatenate([vp_ref[...], vc_ref[...]], axis=0)
        valid = _swa_band_mask(n)
        for g in range(SWA_GROUP):
            lanes = slice(g * LANES, (g + 1) * LANES)
            s = jnp.where(valid, _dot(q_ref[:, lanes], kband, NT) * scale, -1e30)
            sk = sink_ref[:, g * LANES:g * LANES + 1]
            m = jnp.maximum(jnp.max(s, axis=1, keepdims=True), sk)
            p = jnp.exp(s - m)
            den = jnp.sum(p, axis=1, keepdims=True) + jnp.exp(sk - m)
            o_ref[:, lanes] = _dot((p / den).astype(BF16), vband, NN).astype(o_ref.dtype)
            lse_ref[g] = m + jnp.log(den)

    return pl.pallas_call(
        body,
        name=name,
        grid=(SWA_KV_HEADS, nb),
        in_specs=[
            pl.BlockSpec((SWA_BLK, gw), lambda h, n: (n, h)),
            pl.BlockSpec((SWA_BLK, LANES), lambda h, n: (jnp.maximum(n - 1, 0), h)),
            pl.BlockSpec((SWA_BLK, LANES), lambda h, n: (n, h)),
            pl.BlockSpec((SWA_BLK, LANES), lambda h, n: (jnp.maximum(n - 1, 0), v_off + h)),
            pl.BlockSpec((SWA_BLK, LANES), lambda h, n: (n, v_off + h)),
            pl.BlockSpec((1, gw), lambda h, n: (0, h)),
        ],
        out_specs=[
            pl.BlockSpec((SWA_BLK, gw), lambda h, n: (n, h)),
            pl.BlockSpec((SWA_GROUP, SWA_BLK, 1), lambda h, n: (h, n, 0)),
        ],
        out_shape=[jax.ShapeDtypeStruct((S, SWA_HEADS * LANES), BF16), jax.ShapeDtypeStruct((SWA_HEADS, S, 1), F32)],
        compiler_params=_params(("parallel", "arbitrary")),
    )(q, k, k, v, v, sink_b)


def _swa_bwd(q, k, v, v_off, sink_b, o, lse, do, name):
    S = q.shape[0]
    nb = S // SWA_BLK
    scale = SWA_HEAD_DIM ** -0.5
    gw = SWA_GROUP * LANES

    def body(q_ref, kp_ref, kc_ref, vp_ref, vc_ref, sink_ref, o_ref, lse_ref, do_ref, dq_ref, dk_ref, dv_ref, dsink_ref):
        n = pl.program_id(1)

        @pl.when(n == 0)
        def _():
            dk_ref[...] = jnp.zeros_like(dk_ref)
            dv_ref[...] = jnp.zeros_like(dv_ref)
            dsink_ref[...] = jnp.zeros_like(dsink_ref)

        kband = jnp.concatenate([kp_ref[...], kc_ref[...]], axis=0)
        vband = jnp.concatenate([vp_ref[...], vc_ref[...]], axis=0)
        valid = _swa_band_mask(n)
        dkb = jnp.zeros((2 * SWA_BLK, LANES), F32)
        dvb = jnp.zeros((2 * SWA_BLK, LANES), F32)
        for g in range(SWA_GROUP):
            lanes = slice(g * LANES, (g + 1) * LANES)
            qg = q_ref[:, lanes]
            dog = do_ref[:, lanes]
            delta = jnp.sum(dog.astype(F32) * o_ref[:, lanes].astype(F32), axis=1, keepdims=True)
            s = jnp.where(valid, _dot(qg, kband, NT) * scale, -1e30)
            lse_g = lse_ref[g]
            p = jnp.exp(s - lse_g)
            p_sink = jnp.exp(sink_ref[:, g * LANES:g * LANES + 1] - lse_g)
            dsink_ref[:, lanes] += jnp.zeros((1, LANES), F32) - jnp.sum(p_sink * delta, axis=0, keepdims=True)
            dvb = dvb + _dot(p.astype(BF16), dog, TN)
            ds = (p * (_dot(dog, vband, NT) - delta) * scale).astype(BF16)
            dq_ref[:, lanes] = _dot(ds, kband, NN)
            dkb = dkb + _dot(ds, qg, TN)

        cur = pl.multiple_of(n * SWA_BLK, SWA_BLK)
        dk_ref[pl.ds(cur, SWA_BLK), :] += dkb[SWA_BLK:]
        dv_ref[pl.ds(cur, SWA_BLK), :] += dvb[SWA_BLK:]

        @pl.when(n > 0)
        def _():
            before = pl.multiple_of((n - 1) * SWA_BLK, SWA_BLK)
            dk_ref[pl.ds(before, SWA_BLK), :] += dkb[:SWA_BLK]
            dv_ref[pl.ds(before, SWA_BLK), :] += dvb[:SWA_BLK]

    return pl.pallas_call(
        body,
        name=name,
        grid=(SWA_KV_HEADS, nb),
        in_specs=[
            pl.BlockSpec((SWA_BLK, gw), lambda h, n: (n, h)),
            pl.BlockSpec((SWA_BLK, LANES), lambda h, n: (jnp.maximum(n - 1, 0), h)),
            pl.BlockSpec((SWA_BLK, LANES), lambda h, n: (n, h)),
            pl.BlockSpec((SWA_BLK, LANES), lambda h, n: (jnp.maximum(n - 1, 0), v_off + h)),
            pl.BlockSpec((SWA_BLK, LANES), lambda h, n: (n, v_off + h)),
            pl.BlockSpec((1, gw), lambda h, n: (0, h)),
            pl.BlockSpec((SWA_BLK, gw), lambda h, n: (n, h)),
            pl.BlockSpec((SWA_GROUP, SWA_BLK, 1), lambda h, n: (h, n, 0)),
            pl.BlockSpec((SWA_BLK, gw), lambda h, n: (n, h)),
        ],
        out_specs=[
            pl.BlockSpec((SWA_BLK, gw), lambda h, n: (n, h)),
            pl.BlockSpec((S, LANES), lambda h, n: (0, h)),
            pl.BlockSpec((S, LANES), lambda h, n: (0, h)),
            pl.BlockSpec((1, gw), lambda h, n: (0, h)),
        ],
        out_shape=[
            jax.ShapeDtypeStruct((S, SWA_HEADS * LANES), F32),
            jax.ShapeDtypeStruct((S, SWA_KV_HEADS * LANES), F32),
            jax.ShapeDtypeStruct((S, SWA_KV_HEADS * LANES), F32),
            jax.ShapeDtypeStruct((1, SWA_HEADS * LANES), F32),
        ],
        compiler_params=_params(("parallel", "arbitrary")),
    )(q, k, k, v, v, sink_b, o, lse, do)


def _pad_cols(w, heads, real):
    k = w.shape[0]
    return jnp.pad(w.reshape(k, heads, real), ((0, 0), (0, 0), (0, LANES - real))).reshape(k, heads * LANES)


def _unpad_cols(g, heads, real):
    k = g.shape[0]
    return g.reshape(k, heads, LANES)[:, :, :real].reshape(k, heads * real)


def _pad_rows(w, heads, real):
    n = w.shape[1]
    return jnp.pad(w.reshape(heads, real, n), ((0, 0), (0, LANES - real), (0, 0))).reshape(heads * LANES, n)


def _unpad_rows(g, heads, real):
    n = g.shape[1]
    return g.reshape(heads, LANES, n)[:, :real, :].reshape(heads * real, n)


def _w_in_internal(w_in):
    c_q, c_kv, k_r, q_swa, k_swa, v_swa, q_sb, k_sb, v_sb, gate = jnp.split(w_in, SPLIT_POINTS, axis=1)
    k_r = jnp.pad(k_r, ((0, 0), (MLA_NOPE, LANES - MLA_NOPE - MLA_ROPE)))
    w1 = jnp.concatenate([c_q, c_kv, k_r, _pad_cols(q_swa, 8, 64), _pad_cols(k_swa, 2, 64)], axis=1)
    w2 = [_pad_cols(v_swa, 2, 64), _pad_cols(q_sb, 8, 64), _pad_cols(k_sb, 8, 64), _pad_cols(v_sb, 8, 64)]
    return w1, w2, gate


def _w_in_reference(g1, g2, g3):
    c_q, c_kv, k_r, q_swa, k_swa = jnp.split(g1, [256, 384, 512, 1536], axis=1)
    v_swa, q_sb, k_sb, v_sb = g2
    return jnp.concatenate([
        c_q, c_kv, k_r[:, MLA_NOPE:MLA_NOPE + MLA_ROPE], _unpad_cols(q_swa, 8, 64), _unpad_cols(k_swa, 2, 64),
        _unpad_cols(v_swa, 2, 64), _unpad_cols(q_sb, 8, 64), _unpad_cols(k_sb, 8, 64), _unpad_cols(v_sb, 8, 64),
        g3], axis=1)


def _w_ukv_internal(w):
    w3 = w.reshape(MLA_KV_LORA, MLA_HEADS, MLA_NOPE + MLA_V)
    pad = lambda t: jnp.pad(t, ((0, 0), (0, 0), (0, LANES - t.shape[2]))).reshape(MLA_KV_LORA, MLA_HEADS * LANES)
    return pad(w3[:, :, :MLA_NOPE]), pad(w3[:, :, MLA_NOPE:])


def _w_ukv_reference(gk, gv):
    gk = gk.reshape(MLA_KV_LORA, MLA_HEADS, LANES)[:, :, :MLA_NOPE]
    gv = gv.reshape(MLA_KV_LORA, MLA_HEADS, LANES)[:, :, :MLA_V]
    return jnp.concatenate([gk, gv], axis=2).reshape(MLA_KV_LORA, MLA_HEADS * (MLA_NOPE + MLA_V))


def _layer_fwd(x, w, tabs):
    mla_tab, swa_tab = tabs
    sv = {"x": x}

    def f_norm(rows, consts):
        return [_rms(rows[0], consts[0])], []

    (h,) = _rowwise(f_norm, [x], [w["g_mix_pre"]], [(D_MODEL, BF16)], [], "norm_mix_pre")
    p1 = _matmul(h, w["w_in1"], "nn", [F32], "proj_lat")
    p2 = _matmul(h, w["w_in2"], "nn", [BF16], "proj_qkv")
    gates = _matmul(h, w["w_in3"], "nn", [BF16], "proj_gate",
                    epilogue=lambda acc, b: (1.0 / (1.0 + jnp.exp(-(acc + b))),), row_extras=[w["b_gate"]])

    def f_prep(rows, consts):
        t = rows[0]
        gq, gkv = consts[0], consts[1]
        mc, mu, md = rows[1], rows[2], rows[3]
        sc, su, sd = rows[4], rows[5], rows[6]
        cq_n = _rms(t[:, 0:256], gq)
        ckv_n = _rms(t[:, 256:384], gkv)
        kr = _rope(t[:, 384:512], mc, mu, md, MLA_ROPE // 2)
        qs = [_rope(t[:, 512 + j * LANES:512 + (j + 1) * LANES], sc, su, sd, SWA_HEAD_DIM // 2) for j in range(8)]
        ks = [_rope(t[:, 1536 + j * LANES:1536 + (j + 1) * LANES], sc, su, sd, SWA_HEAD_DIM // 2) for j in range(2)]
        return [cq_n, ckv_n, kr, jnp.concatenate(qs, axis=1), jnp.concatenate(ks, axis=1)], []

    cq_n, ckv_n, kr, q_swa, k_swa = _rowwise(
        f_prep, [p1, *mla_tab["k"], *swa_tab["f"]], [w["g_q_lat"], w["g_kv_lat"]],
        [(256, BF16), (128, BF16), (LANES, F32), (1024, BF16), (256, BF16)], [], "lat_prep")

    q_lat = _matmul(cq_n, w["w_uq"], "nn", [F32], "mla_q_up")
    k_lat = _matmul(ckv_n, w["w_ukv_k"], "nn", [F32], "mla_k_up")
    v_mla = _matmul(ckv_n, w["w_ukv_v"], "nn", [BF16], "mla_v_up")

    def f_mla_prep(rows, consts):
        ql, kl, krr, mc, mu, md = rows
        qs = [_rope(ql[:, j * LANES:(j + 1) * LANES], mc, mu, md, MLA_ROPE // 2) for j in range(8)]
        ks = [kl[:, j * LANES:(j + 1) * LANES] + krr for j in range(8)]
        return [jnp.concatenate(qs, axis=1), jnp.concatenate(ks, axis=1)], []

    q_mla, k_mla = _rowwise(f_mla_prep, [q_lat, k_lat, kr, *mla_tab["q"]], [], [(1024, BF16), (1024, BF16)], [], "mla_prep")

    o_mla, lse_mla = _softmax_attn_fwd(q_mla, k_mla, v_mla, MLA_HEADS, (MLA_NOPE + MLA_ROPE) ** -0.5, "mla_fwd")
    o_swa, lse_swa = _swa_fwd(q_swa, k_swa, p2, 0, w["sink_b"], "swa_fwd")
    o_sb = _sb_attn_fwd(p2, SB_HEADS, "sb_fwd", 2, 10, 18)

    oa = _matmul(o_mla, w["w_o_mla"], "nn", [F32], "o_proj_mla")
    ob = _matmul(o_swa, w["w_o_swa"], "nn", [F32], "o_proj_swa")
    oc = _matmul(o_sb, w["w_o_sb"], "nn", [F32], "o_proj_sb")

    def f_mix(rows, consts):
        a, b, c, g = rows
        g = g.astype(F32)
        return [g[:, 0:1024] * a + g[:, 1024:2048] * b + g[:, 2048:3072] * c], []

    (mixed,) = _rowwise(f_mix, [oa, ob, oc, gates], [], [(D_MODEL, BF16)], [], "gate_mix")
    y = _matmul(mixed, w["w_out"], "nn", [F32], "out_proj")

    def f_res_norm(rows, consts):
        return [rows[0] + _rms(rows[1], consts[0])], []

    (x1,) = _rowwise(f_res_norm, [x, y], [w["g_mix_post"]], [(D_MODEL, F32)], [], "res_norm_mix")
    (h2,) = _rowwise(f_norm, [x1], [w["g_mlp_pre"]], [(D_MODEL, BF16)], [], "norm_mlp_pre")

    def relu2(acc):
        r = jnp.maximum(acc, 0.0)
        return acc, r * r

    up, u = _matmul(h2, w["w_up"], "nn", [BF16, BF16], "mlp_up", epilogue=relu2)
    zd = _matmul(u, w["w_down"], "nn", [F32], "mlp_down")
    (x2,) = _rowwise(f_res_norm, [x1, zd], [w["g_mlp_post"]], [(D_MODEL, F32)], [], "res_norm_mlp")

    sv.update(h=h, p1=p1, p2=p2, gates=gates, cq_n=cq_n, ckv_n=ckv_n, q_swa=q_swa, k_swa=k_swa, q_mla=q_mla,
              k_mla=k_mla, v_mla=v_mla, o_mla=o_mla, lse_mla=lse_mla, o_swa=o_swa, lse_swa=lse_swa, o_sb=o_sb,
              oa=oa, ob=ob, oc=oc, mixed=mixed, y=y, x1=x1, h2=h2, up=up, u=u, zd=zd)
    return x2, sv


def _layer_bwd(dx2, w, sv, tabs):
    mla_tab, swa_tab = tabs
    gr = {}

    def f_norm_bwd(rows, consts):
        dx, dg = _rms_bwd(rows[0], consts[0], rows[1])
        return [dx], [dg]

    def f_norm_bwd_res(rows, consts):
        dx, dg = _rms_bwd(rows[0], consts[0], rows[1])
        return [rows[2] + dx], [dg]

    dzd, gr["g_mlp_post"] = _rowwise(f_norm_bwd, [sv["zd"], dx2], [w["g_mlp_post"]], [(D_MODEL, BF16)], [D_MODEL], "b_norm_mlp_post")
    gr["w_down"] = _matmul(sv["u"], dzd, "tn", [F32], "b_w_down")
    dup = _matmul(dzd, w["w_down"], "nt", [BF16], "b_mlp_down",
                  epilogue=lambda acc, up: (acc * 2.0 * jnp.maximum(up.astype(F32), 0.0),), extras=[sv["up"]])
    gr["w_up"] = _matmul(sv["h2"], dup, "tn", [F32], "b_w_up")
    dh2 = _matmul(dup, w["w_up"], "nt", [F32], "b_mlp_up")
    dx1, gr["g_mlp_pre"] = _rowwise(f_norm_bwd_res, [sv["x1"], dh2, dx2], [w["g_mlp_pre"]], [(D_MODEL, F32)], [D_MODEL], "b_norm_mlp_pre")

    dy, gr["g_mix_post"] = _rowwise(f_norm_bwd, [sv["y"], dx1], [w["g_mix_post"]], [(D_MODEL, BF16)], [D_MODEL], "b_norm_mix_post")
    gr["w_out"] = _matmul(sv["mixed"], dy, "tn", [F32], "b_w_out")
    dmixed = _matmul(dy, w["w_out"], "nt", [F32], "b_out_proj")

    def f_mix_bwd(rows, consts):
        dm, a, b, c, g = rows
        g = g.astype(F32)
        outs, dls = [], []
        for j, o in enumerate((a, b, c)):
            gj = g[:, j * D_MODEL:(j + 1) * D_MODEL]
            outs.append(dm * gj)
            dls.append(dm * o * gj * (1.0 - gj))
        dl = jnp.concatenate(dls, axis=1)
        return outs + [dl], [dl]

    doa, dob, doc, dlogit, gr["b_gate"] = _rowwise(
        f_mix_bwd, [dmixed, sv["oa"], sv["ob"], sv["oc"], sv["gates"]], [],
        [(D_MODEL, BF16)] * 3 + [(P3_W, BF16)], [P3_W], "b_gate_mix")

    gr["w_o_mla"] = _matmul(sv["o_mla"], doa, "tn", [F32], "b_w_o_mla")
    gr["w_o_swa"] = _matmul(sv["o_swa"], dob, "tn", [F32], "b_w_o_swa")
    gr["w_o_sb"] = _matmul(sv["o_sb"], doc, "tn", [F32], "b_w_o_sb")
    do_mla = _matmul(doa, w["w_o_mla"], "nt", [BF16], "b_o_proj_mla")
    do_swa = _matmul(dob, w["w_o_swa"], "nt", [BF16], "b_o_proj_swa")
    do_sb = _matmul(doc, w["w_o_sb"], "nt", [BF16], "b_o_proj_sb")

    dq_sb, dk_sb, dv_sb = _sb_attn_bwd(sv["p2"], sv["o_sb"], do_sb, SB_HEADS, "sb_bwd", 2, 10, 18)
    dq_swa, dk_swa, dv_swa, dsink = _swa_bwd(sv["q_swa"], sv["k_swa"], sv["p2"], 0, w["sink_b"], sv["o_swa"],
                                             sv["lse_swa"], do_swa, "swa_bwd")
    gr["swa_sinks"] = dsink.reshape(SWA_HEADS, LANES)[:, 0]
    dq_mla, dk_mla, dv_mla = _softmax_attn_bwd(sv["q_mla"], sv["k_mla"], sv["v_mla"], sv["o_mla"], sv["lse_mla"], do_mla,
                                               MLA_HEADS, (MLA_NOPE + MLA_ROPE) ** -0.5, "mla_bwd")

    def f_mla_post(rows, consts):
        dq, dk, qc, qu, qd, kc, ku, kd = rows
        dqs = [_rope(dq[:, j * LANES:(j + 1) * LANES], qc, qu, qd, MLA_ROPE // 2) for j in range(8)]
        dkr = dk[:, 0:LANES]
        for j in range(1, 8):
            dkr = dkr + dk[:, j * LANES:(j + 1) * LANES]
        return [jnp.concatenate(dqs, axis=1), _rope(dkr, kc, ku, kd, MLA_ROPE // 2)], []

    dq_lat, dkr = _rowwise(f_mla_post, [dq_mla, dk_mla, *mla_tab["q_inv"], *mla_tab["k_inv"]], [],
                           [(1024, BF16), (LANES, F32)], [], "b_mla_post")
    gr["w_uq"] = _matmul(sv["cq_n"], dq_lat, "tn", [F32], "b_w_uq")
    gr["w_ukv_k"] = _matmul(sv["ckv_n"], dk_mla, "tn", [F32], "b_w_ukv_k")
    gr["w_ukv_v"] = _matmul(sv["ckv_n"], dv_mla, "tn", [F32], "b_w_ukv_v")
    dcq_n = _matmul(dq_lat, w["w_uq"], "nt", [F32], "b_mla_q_up")
    dckv_a = _matmul(dk_mla, w["w_ukv_k"], "nt", [F32], "b_mla_k_up")
    dckv_b = _matmul(dv_mla, w["w_ukv_v"], "nt", [F32], "b_mla_v_up")

    def f_prep_bwd(rows, consts):
        t, dcq, dca, dcb, dkr_, dqs, dks, sc, su, sd = rows
        gq, gkv = consts
        dc_q, dgq = _rms_bwd(t[:, 0:256], gq, dcq)
        dc_kv, dgkv = _rms_bwd(t[:, 256:384], gkv, dca + dcb)
        q_parts = [_rope(dqs[:, j * LANES:(j + 1) * LANES], sc, su, sd, SWA_HEAD_DIM // 2) for j in range(8)]
        k_parts = [_rope(dks[:, j * LANES:(j + 1) * LANES], sc, su, sd, SWA_HEAD_DIM // 2) for j in range(2)]
        return [jnp.concatenate([dc_q, dc_kv, dkr_] + q_parts + k_parts, axis=1)], [dgq, dgkv]

    dp1, gr["g_q_lat"], gr["g_kv_lat"] = _rowwise(
        f_prep_bwd, [sv["p1"], dcq_n, dckv_a, dckv_b, dkr, dq_swa, dk_swa, *swa_tab["inv"]], [w["g_q_lat"], w["g_kv_lat"]],
        [(P1_W, BF16)], [256, 128], "b_lat_prep")

    gr["w_in1"] = _matmul(sv["h"], dp1, "tn", [F32], "b_w_in_lat")
    dh = _matmul(dp1, w["w_in1"], "nt", [F32], "b_proj_lat")
    gr["w_in2"] = []
    add_prev = lambda acc, prev: (acc + prev,)
    for piece, wp, tag in zip((dv_swa, dq_sb, dk_sb, dv_sb), w["w_in2_parts"], ("vswa", "qsb", "ksb", "vsb")):
        gr["w_in2"].append(_matmul(sv["h"], piece, "tn", [F32], "b_w_in_" + tag))
        dh = _matmul(piece, wp, "nt", [F32], "b_proj_" + tag, epilogue=add_prev, extras=[dh])
    gr["w_in3"] = _matmul(sv["h"], dlogit, "tn", [F32], "b_w_in_gate")
    dh = _matmul(dlogit, w["w_in3"], "nt", [F32], "b_proj_gate", epilogue=add_prev, extras=[dh])
    dx, gr["g_mix_pre"] = _rowwise(f_norm_bwd_res, [sv["x"], dh, dx1], [w["g_mix_pre"]], [(D_MODEL, F32)], [D_MODEL], "b_norm_mix_pre")
    return dx, gr


def _local_step(x, positions, loss_target, full):
    mc, mu, md = _rope_tables(positions, MLA_NOPE, MLA_ROPE, True)
    kc, ku, kd = _rope_tables(positions, MLA_NOPE, MLA_ROPE, False)
    sc, su, sd = _rope_tables(positions, 0, SWA_HEAD_DIM, False)
    mla_tab = {"q": (mc, mu, md), "k": (kc, ku, kd), "q_inv": (mc, -mu, -md), "k_inv": (kc, -ku, -kd)}
    swa_tab = {"f": (sc, su, sd), "inv": (sc, -su, -sd)}
    tabs = (mla_tab, swa_tab)

    layers = []
    for l in range(DEPTH):
        w1, w2, w3 = _w_in_internal(full["w_in"][l].astype(BF16))
        uk, uv = _w_ukv_internal(full["w_ukv"][l].astype(BF16))
        layers.append({
            "w_in1": w1, "w_in2": jnp.concatenate(w2, axis=1), "w_in2_parts": w2, "w_in3": w3,
            "w_uq": _pad_cols(full["w_uq"][l].astype(BF16), MLA_HEADS, MLA_NOPE + MLA_ROPE),
            "w_ukv_k": uk, "w_ukv_v": uv,
            "w_o_mla": _pad_rows(full["w_o_mla"][l].astype(BF16), 8, 64),
            "w_o_swa": _pad_rows(full["w_o_swa"][l].astype(BF16), 8, 64),
            "w_o_sb": _pad_rows(full["w_o_sb"][l].astype(BF16), 8, 64),
            "w_out": full["w_out"][l].astype(BF16), "w_up": full["w_up"][l].astype(BF16),
            "w_down": full["w_down"][l].astype(BF16),
            "g_mix_pre": full["g_mix_pre"][l][None], "b_gate": full["b_gate"][l][None],
            "g_q_lat": full["g_q_lat"][l][None], "g_kv_lat": full["g_kv_lat"][l][None],
            "g_mix_post": full["g_mix_post"][l][None], "g_mlp_pre": full["g_mlp_pre"][l][None],
            "g_mlp_post": full["g_mlp_post"][l][None],
            "sink_b": jnp.repeat(full["swa_sinks"][l], LANES)[None],
        })

    saved = []
    h = x
    for l in range(DEPTH):
        h, sv = _layer_fwd(h, layers[l], tabs)
        saved.append(sv)

    def f_loss(rows, consts):
        err = rows[0] - rows[1]
        return [err * (1.0 / D_MODEL)], [jnp.sum(err * err, axis=1, keepdims=True)]

    dy, sq = _rowwise(f_loss, [h, loss_target], [], [(D_MODEL, F32)], [1], "loss_head")
    loss_part = sq * (0.5 / D_MODEL)

    grads = [None] * DEPTH
    d = dy
    for l in reversed(range(DEPTH)):
        d, gr = _layer_bwd(d, layers[l], saved[l], tabs)
        grads[l] = {
            "g_mix_pre": gr["g_mix_pre"][0], "w_in": _w_in_reference(gr["w_in1"], gr["w_in2"], gr["w_in3"]),
            "b_gate": gr["b_gate"][0], "g_q_lat": gr["g_q_lat"][0], "g_kv_lat": gr["g_kv_lat"][0],
            "w_uq": _unpad_cols(gr["w_uq"], MLA_HEADS, MLA_NOPE + MLA_ROPE),
            "w_ukv": _w_ukv_reference(gr["w_ukv_k"], gr["w_ukv_v"]), "swa_sinks": gr["swa_sinks"],
            "w_o_mla": _unpad_rows(gr["w_o_mla"], 8, 64), "w_o_swa": _unpad_rows(gr["w_o_swa"], 8, 64),
            "w_o_sb": _unpad_rows(gr["w_o_sb"], 8, 64), "w_out": gr["w_out"], "g_mix_post": gr["g_mix_post"][0],
            "g_mlp_pre": gr["g_mlp_pre"][0], "w_up": gr["w_up"], "w_down": gr["w_down"], "g_mlp_post": gr["g_mlp_post"][0],
        }
    stacked = {n: jnp.stack([grads[l][n] for l in range(DEPTH)]) for n in WEIGHTS}
    return loss_part, d, stacked


def _rows_of(a):
    return a.reshape(-1, LANES)


def _small_rows(d):
    parts = []
    for n in SMALL:
        a = d[n]
        if a.shape[1] < LANES:
            a = jnp.pad(a, ((0, 0), (0, LANES - a.shape[1])))
        parts.append(_rows_of(a))
    return parts


def _pack(shards, small, dtype):
    parts = [_rows_of(shards[n]) for n in SHARDED]
    if small is not None:
        parts += _small_rows(small)
    slab = jnp.concatenate(parts, axis=0).astype(dtype)
    pad = (-slab.shape[0]) % SLAB_ROW_ALIGN
    return jnp.pad(slab, ((0, pad), (0, 0)))


def _unpack(slab, shard_shapes, small_shapes):
    out, r = {}, 0
    for n in SHARDED:
        rows = int(np.prod(shard_shapes[n])) // LANES
        out[n] = slab[r:r + rows].reshape(shard_shapes[n])
        r += rows
    if small_shapes is not None:
        for n in SMALL:
            depth, width = small_shapes[n]
            rows = depth * max(width, LANES) // LANES
            out[n] = slab[r:r + rows].reshape(depth, max(width, LANES))[:, :width]
            r += rows
    return out


def _chip_exchange(src, per_chip_src, name):
    rows = src.shape[-2]

    def body(src_ref, out_ref, send_sems, recv_sems, local_sem):
        x, y, c = lax.axis_index("x"), lax.axis_index("y"), lax.axis_index("c")
        me = 2 * x + y
        chips = [(1 - x, y), (x, 1 - y), (1 - x, 1 - y)]
        pick = (lambda j: src_ref.at[j]) if per_chip_src else (lambda j: src_ref)
        mine = pltpu.make_async_copy(pick(me), out_ref.at[me], local_sem)
        mine.start()
        sends = []
        for k, (cx, cy) in enumerate(chips):
            cp = pltpu.make_async_remote_copy(
                src_ref=pick(2 * cx + cy), dst_ref=out_ref.at[me], send_sem=send_sems.at[k], recv_sem=recv_sems.at[k],
                device_id=(cx, cy, c), device_id_type=pl.DeviceIdType.MESH)
            cp.start()
            sends.append(cp)
        for k, (cx, cy) in enumerate(chips):
            pltpu.make_async_remote_copy(
                src_ref=pick(me), dst_ref=out_ref.at[2 * cx + cy], send_sem=send_sems.at[k], recv_sem=recv_sems.at[k],
                device_id=(cx, cy, c), device_id_type=pl.DeviceIdType.MESH).wait_recv()
        for cp in sends:
            cp.wait_send()
        mine.wait()

    return pl.pallas_call(
        body,
        name=name,
        in_specs=[pl.BlockSpec(memory_space=pl.ANY)],
        out_specs=pl.BlockSpec(memory_space=pl.ANY),
        out_shape=jax.ShapeDtypeStruct((N_CHIPS, rows, LANES), src.dtype),
        scratch_shapes=[pltpu.SemaphoreType.DMA((3,)), pltpu.SemaphoreType.DMA((3,)), pltpu.SemaphoreType.DMA],
    )(src)


def _sibling_exchange(src, name):
    def body(src_ref, out_ref, send_sem, recv_sem):
        peer = (lax.axis_index("x"), lax.axis_index("y"), 1 - lax.axis_index("c"))
        cp = pltpu.make_async_remote_copy(src_ref=src_ref, dst_ref=out_ref, send_sem=send_sem, recv_sem=recv_sem,
                                          device_id=peer, device_id_type=pl.DeviceIdType.MESH)
        cp.start()
        cp.wait()

    return pl.pallas_call(
        body,
        name=name,
        in_specs=[pl.BlockSpec(memory_space=pl.ANY)],
        out_specs=pl.BlockSpec(memory_space=pl.ANY),
        out_shape=jax.ShapeDtypeStruct(src.shape, src.dtype),
        scratch_shapes=[pltpu.SemaphoreType.DMA, pltpu.SemaphoreType.DMA],
    )(src)


def _sum_chips(buf, name):
    rows = buf.shape[1]
    bm = 2048 if rows % 2048 == 0 else SLAB_ROW_ALIGN

    def body(b_ref, o_ref):
        o_ref[...] = ((b_ref[0] + b_ref[1]) + b_ref[2]) + b_ref[3]

    return pl.pallas_call(
        body,
        name=name,
        grid=(rows // bm,),
        in_specs=[pl.BlockSpec((N_CHIPS, bm, LANES), lambda i: (0, i, 0))],
        out_specs=pl.BlockSpec((bm, LANES), lambda i: (i, 0)),
        out_shape=jax.ShapeDtypeStruct((rows, LANES), F32),
        compiler_params=_params(("arbitrary",)),
    )(buf)


def _adamw(w, m, v, g_mine, g_sibling, name):
    def fn(rows, consts):
        w_, m_, v_, ga, gb = rows
        g = ga + gb
        m_new = ADAM_B1 * m_ + (1.0 - ADAM_B1) * g
        v_new = ADAM_B2 * v_ + (1.0 - ADAM_B2) * (g * g)
        m_hat = m_new / (1.0 - ADAM_B1 ** ADAM_STEP)
        v_hat = v_new / (1.0 - ADAM_B2 ** ADAM_STEP)
        delta = -ADAM_LR * (m_hat / (jnp.sqrt(v_hat) + ADAM_EPS) + ADAM_WD * w_)
        return [g, delta, m_new, v_new], []

    return _rowwise(fn, [w, m, v, g_mine, g_sibling], [], [(LANES, F32)] * 4, [], name, bm=2048)


def kernel(x, positions, g_mix_pre, w_in, b_gate, g_q_lat, g_kv_lat, w_uq, w_ukv, swa_sinks, w_o_mla, w_o_swa, w_o_sb, w_out, g_mix_post, g_mlp_pre, w_up, w_down, g_mlp_post, loss_target, m_g_mix_pre, m_w_in, m_b_gate, m_g_q_lat, m_g_kv_lat, m_w_uq, m_w_ukv, m_swa_sinks, m_w_o_mla, m_w_o_swa, m_w_o_sb, m_w_out, m_g_mix_post, m_g_mlp_pre, m_w_up, m_w_down, m_g_mlp_post, v_g_mix_pre, v_w_in, v_b_gate, v_g_q_lat, v_g_kv_lat, v_w_uq, v_w_ukv, v_swa_sinks, v_w_o_mla, v_w_o_swa, v_w_o_sb, v_w_out, v_g_mix_post, v_g_mlp_pre, v_w_up, v_w_down, v_g_mlp_post):
    given = dict(locals())
    wts = {n: given[n] for n in WEIGHTS}
    mom_m = {n: given["m_" + n] for n in WEIGHTS}
    mom_v = {n: given["v_" + n] for n in WEIGHTS}
    shard_shapes = {n: wts[n].shape for n in SHARDED}
    small_shapes = {n: wts[n].shape for n in SMALL}

    gathered = _chip_exchange(_pack(wts, None, BF16), False, "gather_weights")
    full = {n: wts[n] for n in SMALL}
    per_chip = [_unpack(gathered[j], shard_shapes, None) for j in range(N_CHIPS)]
    for n in SHARDED:
        full[n] = jnp.concatenate([per_chip[j][n] for j in range(N_CHIPS)], axis=SHARD_AXIS[n])

    loss_part, grad_x, grads = _local_step(x[0], positions[0], loss_target[0], full)
    loss = lax.psum(loss_part[0, 0], ("x", "y", "c"))

    small_g = {n: grads[n] for n in SMALL}
    slabs = []
    for j in range(N_CHIPS):
        shard = {n: jnp.split(grads[n], N_CHIPS, axis=SHARD_AXIS[n])[j] for n in SHARDED}
        slabs.append(_pack(shard, small_g, F32))
    landed = _chip_exchange(jnp.stack(slabs), True, "scatter_grads")
    g_mine = _sum_chips(landed, "sum_chips")
    g_sibling = _sibling_exchange(g_mine, "sibling_grads")

    g_slab, d_slab, m_slab, v_slab = _adamw(
        _pack(wts, wts, F32), _pack(mom_m, mom_m, F32), _pack(mom_v, mom_v, F32), g_mine, g_sibling, "adamw")
    outs = [loss, grad_x[None]]
    for slab in (g_slab, d_slab, m_slab, v_slab):
        un = _unpack(slab, shard_shapes, small_shapes)
        outs += [un[n] for n in WEIGHTS]
    return tuple(outs)
```

```python
import numpy as np
import jax
import jax.numpy as jnp
from jax import lax
from jax.experimental import pallas as pl
from jax.experimental.pallas import tpu as pltpu

F32 = jnp.float32
BF16 = jnp.bfloat16

D_MODEL = 1024
DEPTH = 4
MLA_HEADS, MLA_Q_LORA, MLA_KV_LORA, MLA_NOPE, MLA_ROPE, MLA_V = 8, 256, 128, 64, 32, 64
SWA_HEADS, SWA_KV_HEADS, SWA_HEAD_DIM, SWA_WINDOW = 8, 2, 64, 128
SB_HEADS, SB_HEAD_DIM = 8, 64
D_FF = 4 * D_MODEL
ROPE_THETA = 10000.0
EPS = 1e-6
SPLIT_SIZES = (256, 128, 32, 512, 128, 128, 512, 512, 512, 3 * D_MODEL)
SPLIT_POINTS = [int(v) for v in np.cumsum(SPLIT_SIZES)[:-1]]

ADAM_LR, ADAM_B1, ADAM_B2, ADAM_EPS, ADAM_WD, ADAM_STEP = 0.001, 0.9, 0.999, 1e-08, 0.01, 10

LANES = 128
V7X_VMEM_BYTES = 64 * 1024 * 1024
VMEM_LIMIT = V7X_VMEM_BYTES - 8 * 1024 * 1024
MATMUL_VMEM_BUDGET = 36 * 1024 * 1024
N_CHIPS = 4
SLAB_ROW_ALIGN = 512

P1_W = 256 + 128 + 128 + 1024 + 256
P2_W = 256 + 1024 + 1024 + 1024
P3_W = 3 * D_MODEL

SHARDED = ("w_in", "w_uq", "w_ukv", "w_o_mla", "w_o_swa", "w_o_sb", "w_out", "w_up", "w_down")
SHARD_AXIS = {"w_in": 2, "w_uq": 2, "w_ukv": 2, "w_o_mla": 2, "w_o_swa": 2, "w_o_sb": 2, "w_out": 1, "w_up": 2, "w_down": 1}
SMALL = ("g_mix_pre", "b_gate", "g_q_lat", "g_kv_lat", "swa_sinks", "g_mix_post", "g_mlp_pre", "g_mlp_post")
WEIGHTS = ("g_mix_pre", "w_in", "b_gate", "g_q_lat", "g_kv_lat", "w_uq", "w_ukv", "swa_sinks", "w_o_mla", "w_o_swa",
           "w_o_sb", "w_out", "g_mix_post", "g_mlp_pre", "w_up", "w_down", "g_mlp_post")

NN = (((1,), (0,)), ((), ()))
NT = (((1,), (1,)), ((), ()))
TN = (((0,), (0,)), ((), ()))


def _dot(a, b, dims):
    return lax.dot_general(a, b, dims, preferred_element_type=F32)


def _params(sem):
    return pltpu.CompilerParams(dimension_semantics=sem, vmem_limit_bytes=VMEM_LIMIT)


def _largest_tile(n, cap):
    if n <= cap:
        return n
    best = LANES
    for t in range(LANES, cap + 1, LANES):
        if n % t == 0:
            best = t
    return best


def _matmul_tiles(M, N, K, a_bytes, b_bytes, out_bytes, extra_bytes):
    tn = _largest_tile(N, 1792)
    tm = _largest_tile(M, 1024 if tn <= 1024 else 512)
    tk = _largest_tile(K, 2048)

    def need(tm_, tk_):
        acc = 4 * tm_ * tn if tk_ < K else 0
        return 2 * (tm_ * tk_ * a_bytes + tk_ * tn * b_bytes + tm_ * tn * (out_bytes + extra_bytes)) + acc

    while need(tm, tk) > MATMUL_VMEM_BUDGET:
        if tk >= tm and tk % 256 == 0:
            tk //= 2
        elif tm % 256 == 0:
            tm //= 2
        else:
            break
    return tm, tn, tk


def _matmul(a, b, mode, out_dtypes, name, epilogue=None, extras=(), row_extras=()):
    if mode == "nn":
        (M, K), (K2, N) = a.shape, b.shape
    elif mode == "nt":
        (M, K), (N, K2) = a.shape, b.shape
    else:
        (K, M), (K2, N) = a.shape, b.shape
    assert K == K2, (name, a.shape, b.shape)
    tm, tn, tk = _matmul_tiles(
        M, N, K, a.dtype.itemsize, b.dtype.itemsize, sum(jnp.dtype(d).itemsize for d in out_dtypes),
        sum(e.dtype.itemsize for e in extras))
    assert M % tm == 0 and N % tn == 0 and K % tk == 0, (name, M, N, K, tm, tn, tk)
    nk = K // tk
    if mode == "tn":
        a_spec = pl.BlockSpec((tk, tm), lambda i, j, k: (k, i))
    else:
        a_spec = pl.BlockSpec((tm, tk), lambda i, j, k: (i, k))
    if mode == "nt":
        b_spec = pl.BlockSpec((tn, tk), lambda i, j, k: (j, k))
    else:
        b_spec = pl.BlockSpec((tk, tn), lambda i, j, k: (k, j))
    dims = {"nn": NN, "nt": NT, "tn": TN}[mode]
    n_ex, n_rex, n_out = len(extras), len(row_extras), len(out_dtypes)

    def body(*refs):
        a_ref, b_ref = refs[:2]
        ex = refs[2:2 + n_ex]
        rex = refs[2 + n_ex:2 + n_ex + n_rex]
        outs = refs[2 + n_ex + n_rex:2 + n_ex + n_rex + n_out]

        def finish(total):
            res = (total,) if epilogue is None else epilogue(total, *[e[...] for e in ex], *[e[...] for e in rex])
            for o, r in zip(outs, res):
                o[...] = r.astype(o.dtype)

        part = _dot(a_ref[...].astype(BF16), b_ref[...].astype(BF16), dims)
        if nk == 1:
            finish(part)
            return
        acc = refs[-1]
        k = pl.program_id(2)

        @pl.when(k == 0)
        def _():
            acc[...] = part

        @pl.when(k > 0)
        def _():
            acc[...] += part

        @pl.when(k == nk - 1)
        def _():
            finish(acc[...])

    in_specs = [a_spec, b_spec]
    in_specs += [pl.BlockSpec((tm, tn), lambda i, j, k: (i, j)) for _ in extras]
    in_specs += [pl.BlockSpec((1, tn), lambda i, j, k: (0, j)) for _ in row_extras]
    out = pl.pallas_call(
        body,
        name=name,
        grid=(M // tm, N // tn, nk),
        in_specs=in_specs,
        out_specs=[pl.BlockSpec((tm, tn), lambda i, j, k: (i, j)) for _ in out_dtypes],
        out_shape=[jax.ShapeDtypeStruct((M, N), dt) for dt in out_dtypes],
        scratch_shapes=[pltpu.VMEM((tm, tn), F32)] if nk > 1 else [],
        compiler_params=_params(("parallel", "parallel", "arbitrary")),
    )(a, b, *extras, *row_extras)
    return out[0] if n_out == 1 else out


def _rowwise(fn, rows, consts, out_defs, sum_widths, name, bm=256):
    R = rows[0].shape[0]
    bm = min(bm, R)
    assert R % bm == 0, (name, R, bm)
    n_r, n_c, n_o = len(rows), len(consts), len(out_defs)
    n_s = len(sum_widths)

    def body(*refs):
        r_in = refs[:n_r]
        c_in = refs[n_r:n_r + n_c]
        o_refs = refs[n_r + n_c:n_r + n_c + n_o]
        s_refs = refs[n_r + n_c + n_o:]
        outs, sums = fn([r[...] for r in r_in], [c[...] for c in c_in])
        for o, val in zip(o_refs, outs):
            o[...] = val.astype(o.dtype)
        if n_s:
            @pl.when(pl.program_id(0) == 0)
            def _():
                for s in s_refs:
                    s[...] = jnp.zeros_like(s)

            for s, val in zip(s_refs, sums):
                s[...] += jnp.sum(val, axis=0, keepdims=True)

    in_specs = [pl.BlockSpec((bm, r.shape[1]), lambda i: (i, 0)) for r in rows]
    in_specs += [pl.BlockSpec(c.shape, lambda i: (0, 0)) for c in consts]
    out_specs = [pl.BlockSpec((bm, w), lambda i: (i, 0)) for w, _ in out_defs]
    out_specs += [pl.BlockSpec((1, w), lambda i: (0, 0)) for w in sum_widths]
    out_shape = [jax.ShapeDtypeStruct((R, w), dt) for w, dt in out_defs]
    out_shape += [jax.ShapeDtypeStruct((1, w), F32) for w in sum_widths]
    return pl.pallas_call(
        body,
        name=name,
        grid=(R // bm,),
        in_specs=in_specs,
        out_specs=out_specs,
        out_shape=out_shape,
        compiler_params=_params(("arbitrary",)),
    )(*rows, *consts)


def _rms(x, g):
    r = lax.rsqrt(jnp.mean(x * x, axis=-1, keepdims=True) + EPS)
    return x * r * g


def _rms_bwd(x, g, dy):
    r = lax.rsqrt(jnp.mean(x * x, axis=-1, keepdims=True) + EPS)
    n = x * r
    dn = dy * g
    dx = r * (dn - n * jnp.mean(dn * n, axis=-1, keepdims=True))
    return dx, dy * n


def _rope(x, c, s_up, s_dn, half):
    return x * c + pltpu.roll(x, half, 1) * s_up + pltpu.roll(x, LANES - half, 1) * s_dn


def _rope_tables(positions, lo, d, nope_pass):
    S = positions.shape[0]
    half = d // 2
    inv = 1.0 / (ROPE_THETA ** (jnp.arange(0, d, 2, dtype=F32) / d))
    ang = positions.astype(F32)[:, None] * inv
    cos, sin = jnp.cos(ang), jnp.sin(ang)
    z = lambda n: jnp.zeros((S, n), F32)
    head = jnp.ones((S, lo), F32) if nope_pass else z(lo)
    tail = LANES - lo - d
    c = jnp.concatenate([head, cos, cos, z(tail)], axis=1)
    s_up = jnp.concatenate([z(lo), z(half), sin, z(tail)], axis=1)
    s_dn = jnp.concatenate([z(lo), -sin, z(half), z(tail)], axis=1)
    return c, s_up, s_dn


MLA_FWD_CFG = (1, 1024)
MLA_BWD_CFG = (2, 512)
SB_FWD_CFG = (8, 256)
SB_BWD_CFG = (4, 256)


def _tile_mask(bk, strict):
    row = lax.broadcasted_iota(jnp.int32, (bk, bk), 0)
    col = lax.broadcasted_iota(jnp.int32, (bk, bk), 1)
    return (col < row) if strict else (col <= row)


def _att_layout(S, cfg):
    nch, bk = cfg
    bq = nch * bk
    assert S % bq == 0, (S, cfg)
    rows = [slice(r * bk, (r + 1) * bk) for r in range(nch)]
    q_spec = lambda off=0: pl.BlockSpec((bq, LANES), lambda h, i: (i, off + h))
    kv_spec = lambda off=0: pl.BlockSpec((S, LANES), lambda h, i: (0, off + h))
    return bq, rows, q_spec, kv_spec


def _total(terms):
    terms = list(terms)
    out = terms[0]
    for t in terms[1:]:
        out = out + t
    return out


def _walk(nch, i, step, carry, leftward):
    everyone = range(nch)
    if leftward:
        for d in reversed(everyone):
            carry = step(nch * i + d, carry, range(d, nch), {d})
        return lax.fori_loop(0, nch * i, lambda t, c: step(nch * i - 1 - t, c, everyone, set()), carry)
    carry = lax.fori_loop(0, nch * i, lambda kb, c: step(kb, c, everyone, set()), carry)
    for d in everyone:
        carry = step(nch * i + d, carry, range(d, nch), {d})
    return carry


def _softmax_attn_fwd(q, k, v, heads, scale, name, q_off=0, k_off=0, v_off=0):
    S = q.shape[0]
    nch, bk = MLA_FWD_CFG
    bq, rows, q_spec, kv_spec = _att_layout(S, MLA_FWD_CFG)

    def body(q_ref, k_ref, v_ref, o_ref, lse_ref):
        i = pl.program_id(1)
        qs = [q_ref[rw, :] for rw in rows]

        def step(kb, cs, active, masked):
            off = pl.multiple_of(kb * bk, bk)
            ks, vs = k_ref[pl.ds(off, bk), :], v_ref[pl.ds(off, bk), :]
            A = list(active)
            s = {r: _dot(qs[r], ks, NT) * scale for r in A}
            s = {r: (jnp.where(_tile_mask(bk, False), s[r], -1e30) if r in masked else s[r]) for r in A}
            m_new = {r: jnp.maximum(cs[r][0], jnp.max(s[r], axis=1, keepdims=True)) for r in A}
            p = {r: jnp.exp(s[r] - m_new[r]) for r in A}
            alpha = {r: jnp.exp(cs[r][0] - m_new[r]) for r in A}
            new = list(cs)
            for r in A:
                new[r] = (m_new[r], alpha[r] * cs[r][1] + jnp.sum(p[r], axis=1, keepdims=True),
                          alpha[r] * cs[r][2] + _dot(p[r].astype(BF16), vs, NN))
            return tuple(new)

        init = (jnp.full((bk, 1), -1e30, F32), jnp.zeros((bk, 1), F32), jnp.zeros((bk, LANES), F32))
        cs = _walk(nch, i, step, tuple(init for _ in rows), False)
        for r, (m, l, acc) in enumerate(cs):
            o_ref[rows[r], :] = (acc / l).astype(o_ref.dtype)
            lse_ref[rows[r], :] = m + jnp.log(l)

    return pl.pallas_call(
        body,
        name=name,
        grid=(heads, S // bq),
        in_specs=[q_spec(q_off), kv_spec(k_off), kv_spec(v_off)],
        out_specs=[q_spec(), pl.BlockSpec((None, bq, 1), lambda h, i: (h, i, 0))],
        out_shape=[jax.ShapeDtypeStruct((S, heads * LANES), BF16), jax.ShapeDtypeStruct((heads, S, 1), F32)],
        compiler_params=_params(("parallel", "arbitrary")),
    )(q, k, v)


def _softmax_attn_bwd(q, k, v, o, lse, do, heads, scale, name, q_off=0, k_off=0, v_off=0):
    S = q.shape[0]
    nch, bk = MLA_BWD_CFG
    bq, rows, q_spec, kv_spec = _att_layout(S, MLA_BWD_CFG)

    def body(q_ref, k_ref, v_ref, o_ref, lse_ref, do_ref, dq_ref, dk_ref, dv_ref):
        i = pl.program_id(1)

        @pl.when(i == 0)
        def _():
            dk_ref[...] = jnp.zeros_like(dk_ref)
            dv_ref[...] = jnp.zeros_like(dv_ref)

        qs = [q_ref[rw, :] for rw in rows]
        dos = [do_ref[rw, :] for rw in rows]
        lses = [lse_ref[rw, :] for rw in rows]
        deltas = [jnp.sum(dos[r].astype(F32) * o_ref[rows[r], :].astype(F32), axis=1, keepdims=True) for r in range(nch)]

        def step(kb, dqs, active, masked):
            off = pl.multiple_of(kb * bk, bk)
            ks, vs = k_ref[pl.ds(off, bk), :], v_ref[pl.ds(off, bk), :]
            A = list(active)
            s = {r: _dot(qs[r], ks, NT) * scale for r in A}
            s = {r: (jnp.where(_tile_mask(bk, False), s[r], -1e30) if r in masked else s[r]) for r in A}
            p = {r: jnp.exp(s[r] - lses[r]) for r in A}
            dp = {r: _dot(dos[r], vs, NT) for r in A}
            ds = {r: (p[r] * (dp[r] - deltas[r]) * scale).astype(BF16) for r in A}
            dv_c = _total(_dot(p[r].astype(BF16), dos[r], TN) for r in A)
            dk_c = _total(_dot(ds[r], qs[r], TN) for r in A)
            dk_ref[pl.ds(off, bk), :] += dk_c
            dv_ref[pl.ds(off, bk), :] += dv_c
            new = list(dqs)
            for r in A:
                new[r] = dqs[r] + _dot(ds[r], ks, NN)
            return tuple(new)

        dqs = _walk(nch, i, step, tuple(jnp.zeros((bk, LANES), F32) for _ in rows), False)
        for r in range(nch):
            dq_ref[rows[r], :] = dqs[r]

    return pl.pallas_call(
        body,
        name=name,
        grid=(heads, S // bq),
        in_specs=[q_spec(q_off), kv_spec(k_off), kv_spec(v_off), q_spec(),
                  pl.BlockSpec((None, bq, 1), lambda h, i: (h, i, 0)), q_spec()],
        out_specs=[q_spec(), kv_spec(), kv_spec()],
        out_shape=[jax.ShapeDtypeStruct((S, heads * LANES), F32)] * 3,
        compiler_params=_params(("parallel", "arbitrary")),
    )(q, k, v, o, lse, do)


def _tri(n, inclusive):
    r = lax.broadcasted_iota(jnp.int32, (n, n), 0)
    c = lax.broadcasted_iota(jnp.int32, (n, n), 1)
    return jnp.where((r >= c) if inclusive else (r > c), 1.0, 0.0).astype(BF16)


def _suffix_sum(x, tri):
    hi = x.astype(BF16)
    lo = (x - hi.astype(F32)).astype(BF16)
    return _dot(hi, tri, NN) + _dot(lo, tri, NN)


def _sb_logs(z):
    lg = jnp.log(1.0 + jnp.exp(-jnp.abs(z)))
    l1m = -(jnp.maximum(z, 0.0) + lg)
    return l1m, l1m + z


SB_SCALE = SB_HEAD_DIM ** -0.5
assert SB_SCALE == 0.125


def _sb_attn_fwd(qkv, heads, name, q_off, k_off, v_off):
    S = qkv.shape[0]
    nch, bk = SB_FWD_CFG
    bq, rows, q_spec, kv_spec = _att_layout(S, SB_FWD_CFG)

    def body(q_ref, k_ref, v_ref, o_ref):
        i = pl.program_id(1)
        qs = [q_ref[rw, :] * SB_SCALE for rw in rows]
        tri = _tri(bk, False)

        def step(kb, cs, active, masked):
            off = pl.multiple_of(kb * bk, bk)
            ks, vs = k_ref[pl.ds(off, bk), :], v_ref[pl.ds(off, bk), :]
            A = list(active)
            lg = {r: _sb_logs(_dot(qs[r], ks, NT)) for r in A}
            l1m = {r: (jnp.where(_tile_mask(bk, True), lg[r][0], 0.0) if r in masked else lg[r][0]) for r in A}
            suf = {r: _suffix_sum(l1m[r], tri) for r in A}
            ex = {r: lg[r][1] + cs[r][0] + suf[r] for r in A}
            ex = {r: (jnp.where(_tile_mask(bk, True), ex[r], -1e30) if r in masked else ex[r]) for r in A}
            ab = {r: jnp.exp(ex[r]).astype(BF16) for r in A}
            new = list(cs)
            for r in A:
                new[r] = (cs[r][0] + jnp.sum(l1m[r], axis=1, keepdims=True), cs[r][1] + _dot(ab[r], vs, NN))
            return tuple(new)

        init = (jnp.zeros((bk, 1), F32), jnp.zeros((bk, LANES), F32))
        cs = _walk(nch, i, step, tuple(init for _ in rows), True)
        for r in range(nch):
            o_ref[rows[r], :] = cs[r][1]

    return pl.pallas_call(
        body,
        name=name,
        grid=(heads, S // bq),
        in_specs=[q_spec(q_off), kv_spec(k_off), kv_spec(v_off)],
        out_specs=q_spec(),
        out_shape=jax.ShapeDtypeStruct((S, heads * LANES), F32),
        compiler_params=_params(("parallel", "arbitrary")),
    )(qkv, qkv, qkv)


def _sb_attn_bwd(qkv, o, do, heads, name, q_off, k_off, v_off):
    S = qkv.shape[0]
    nch, bk = SB_BWD_CFG
    bq, rows, q_spec, kv_spec = _att_layout(S, SB_BWD_CFG)

    def body(q_ref, k_ref, v_ref, o_ref, do_ref, dq_ref, dk_ref, dv_ref):
        i = pl.program_id(1)

        @pl.when(i == 0)
        def _():
            dk_ref[...] = jnp.zeros_like(dk_ref)
            dv_ref[...] = jnp.zeros_like(dv_ref)

        tri = _tri(bk, False)
        qs = [q_ref[rw, :] * SB_SCALE for rw in rows]
        dos = [do_ref[rw, :] for rw in rows]
        deltas = [jnp.sum(dos[r].astype(F32) * o_ref[rows[r], :], axis=1, keepdims=True) for r in range(nch)]

        def step(kb, cs, active, masked):
            off = pl.multiple_of(kb * bk, bk)
            ks, vs = k_ref[pl.ds(off, bk), :], v_ref[pl.ds(off, bk), :]
            A = list(active)
            lg = {r: _sb_logs(_dot(qs[r], ks, NT)) for r in A}
            l1m = {r: (jnp.where(_tile_mask(bk, True), lg[r][0], 0.0) if r in masked else lg[r][0]) for r in A}
            suf = {r: _suffix_sum(l1m[r], tri) for r in A}
            ex = {r: lg[r][1] + cs[r][0] + suf[r] for r in A}
            ex = {r: (jnp.where(_tile_mask(bk, True), ex[r], -1e30) if r in masked else ex[r]) for r in A}
            ab = {r: jnp.exp(ex[r]).astype(BF16) for r in A}
            da = {r: _dot(dos[r], vs, NT) for r in A}
            g = {r: ab[r].astype(F32) * da[r] for r in A}
            gs = {r: _suffix_sum(g[r], tri) for r in A}
            beta = {r: jnp.exp(lg[r][1]) for r in A}
            dz = {r: g[r] - beta[r] * (deltas[r] - cs[r][1] - gs[r]) for r in A}
            dz = {r: (jnp.where(_tile_mask(bk, True), dz[r], 0.0) if r in masked else dz[r]) for r in A}
            dzb = {r: dz[r].astype(BF16) for r in A}
            dv_c = _total(_dot(ab[r], dos[r], TN) for r in A)
            dk_c = _total(_dot(dzb[r], qs[r], TN) for r in A)
            dk_ref[pl.ds(off, bk), :] += dk_c
            dv_ref[pl.ds(off, bk), :] += dv_c
            new = list(cs)
            for r in A:
                new[r] = (cs[r][0] + jnp.sum(l1m[r], axis=1, keepdims=True),
                          cs[r][1] + jnp.sum(g[r], axis=1, keepdims=True), cs[r][2] + _dot(dzb[r], ks, NN))
            return tuple(new)

        zcol = jnp.zeros((bk, 1), F32)
        init = (zcol, zcol, jnp.zeros((bk, LANES), F32))
        cs = _walk(nch, i, step, tuple(init for _ in rows), True)
        for r in range(nch):
            dq_ref[rows[r], :] = cs[r][2] * SB_SCALE

    return pl.pallas_call(
        body,
        name=name,
        grid=(heads, S // bq),
        in_specs=[q_spec(q_off), kv_spec(k_off), kv_spec(v_off), q_spec(), q_spec()],
        out_specs=[q_spec(), kv_spec(), kv_spec()],
        out_shape=[jax.ShapeDtypeStruct((S, heads * LANES), F32)] * 3,
        compiler_params=_params(("parallel", "arbitrary")),
    )(qkv, qkv, qkv, o, do)


SWA_BLK = 128
SWA_GROUP = SWA_HEADS // SWA_KV_HEADS


def _swa_band_mask(n):
    row = lax.broadcasted_iota(jnp.int32, (SWA_BLK, 2 * SWA_BLK), 0)
    col = lax.broadcasted_iota(jnp.int32, (SWA_BLK, 2 * SWA_BLK), 1)
    return (col > row) & (col <= row + SWA_WINDOW) & ((n > 0) | (col >= SWA_BLK))


def _swa_fwd(q, k, v, v_off, sink_b, name):
    S = q.shape[0]
    nb = S // SWA_BLK
    scale = SWA_HEAD_DIM ** -0.5
    gw = SWA_GROUP * LANES

    def body(q_ref, kp_ref, kc_ref, vp_ref, vc_ref, sink_ref, o_ref, lse_ref):
        n = pl.program_id(1)
        kband = jnp.concatenate([kp_ref[...], kc_ref[...]], axis=0)
        vband = jnp.concatenate([vp_ref[...], vc_ref[...]], axis=0)
        valid = _swa_band_mask(n)
        for g in range(SWA_GROUP):
            lanes = slice(g * LANES, (g + 1) * LANES)
            s = jnp.where(valid, _dot(q_ref[:, lanes], kband, NT) * scale, -1e30)
            sk = sink_ref[:, g * LANES:g * LANES + 1]
            m = jnp.maximum(jnp.max(s, axis=1, keepdims=True), sk)
            p = jnp.exp(s - m)
            den = jnp.sum(p, axis=1, keepdims=True) + jnp.exp(sk - m)
            o_ref[:, lanes] = _dot((p / den).astype(BF16), vband, NN).astype(o_ref.dtype)
            lse_ref[g] = m + jnp.log(den)

    return pl.pallas_call(
        body,
        name=name,
        grid=(SWA_KV_HEADS, nb),
        in_specs=[
            pl.BlockSpec((SWA_BLK, gw), lambda h, n: (n, h)),
            pl.BlockSpec((SWA_BLK, LANES), lambda h, n: (jnp.maximum(n - 1, 0), h)),
            pl.BlockSpec((SWA_BLK, LANES), lambda h, n: (n, h)),
            pl.BlockSpec((SWA_BLK, LANES), lambda h, n: (jnp.maximum(n - 1, 0), v_off + h)),
            pl.BlockSpec((SWA_BLK, LANES), lambda h, n: (n, v_off + h)),
            pl.BlockSpec((1, gw), lambda h, n: (0, h)),
        ],
        out_specs=[
            pl.BlockSpec((SWA_BLK, gw), lambda h, n: (n, h)),
            pl.BlockSpec((SWA_GROUP, SWA_BLK, 1), lambda h, n: (h, n, 0)),
        ],
        out_shape=[jax.ShapeDtypeStruct((S, SWA_HEADS * LANES), BF16), jax.ShapeDtypeStruct((SWA_HEADS, S, 1), F32)],
        compiler_params=_params(("parallel", "arbitrary")),
    )(q, k, k, v, v, sink_b)


def _swa_bwd(q, k, v, v_off, sink_b, o, lse, do, name):
    S = q.shape[0]
    nb = S // SWA_BLK
    scale = SWA_HEAD_DIM ** -0.5
    gw = SWA_GROUP * LANES

    def body(q_ref, kp_ref, kc_ref, vp_ref, vc_ref, sink_ref, o_ref, lse_ref, do_ref, dq_ref, dk_ref, dv_ref, dsink_ref):
        n = pl.program_id(1)

        @pl.when(n == 0)
        def _():
            dk_ref[...] = jnp.zeros_like(dk_ref)
            dv_ref[...] = jnp.zeros_like(dv_ref)
            dsink_ref[...] = jnp.zeros_like(dsink_ref)

        kband = jnp.concatenate([kp_ref[...], kc_ref[...]], axis=0)
        vband = jnp.concatenate([vp_ref[...], vc_ref[...]], axis=0)
        valid = _swa_band_mask(n)
        dkb = jnp.zeros((2 * SWA_BLK, LANES), F32)
        dvb = jnp.zeros((2 * SWA_BLK, LANES), F32)
        for g in range(SWA_GROUP):
            lanes = slice(g * LANES, (g + 1) * LANES)
            qg = q_ref[:, lanes]
            dog = do_ref[:, lanes]
            delta = jnp.sum(dog.astype(F32) * o_ref[:, lanes].astype(F32), axis=1, keepdims=True)
            s = jnp.where(valid, _dot(qg, kband, NT) * scale, -1e30)
            lse_g = lse_ref[g]
            p = jnp.exp(s - lse_g)
            p_sink = jnp.exp(sink_ref[:, g * LANES:g * LANES + 1] - lse_g)
            dsink_ref[:, lanes] += jnp.zeros((1, LANES), F32) - jnp.sum(p_sink * delta, axis=0, keepdims=True)
            dvb = dvb + _dot(p.astype(BF16), dog, TN)
            ds = (p * (_dot(dog, vband, NT) - delta) * scale).astype(BF16)
            dq_ref[:, lanes] = _dot(ds, kband, NN)
            dkb = dkb + _dot(ds, qg, TN)

        cur = pl.multiple_of(n * SWA_BLK, SWA_BLK)
        dk_ref[pl.ds(cur, SWA_BLK), :] += dkb[SWA_BLK:]
        dv_ref[pl.ds(cur, SWA_BLK), :] += dvb[SWA_BLK:]

        @pl.when(n > 0)
        def _():
            before = pl.multiple_of((n - 1) * SWA_BLK, SWA_BLK)
            dk_ref[pl.ds(before, SWA_BLK), :] += dkb[:SWA_BLK]
            dv_ref[pl.ds(before, SWA_BLK), :] += dvb[:SWA_BLK]

    return pl.pallas_call(
        body,
        name=name,
        grid=(SWA_KV_HEADS, nb),
        in_specs=[
            pl.BlockSpec((SWA_BLK, gw), lambda h, n: (n, h)),
            pl.BlockSpec((SWA_BLK, LANES), lambda h, n: (jnp.maximum(n - 1, 0), h)),
            pl.BlockSpec((SWA_BLK, LANES), lambda h, n: (n, h)),
            pl.BlockSpec((SWA_BLK, LANES), lambda h, n: (jnp.maximum(n - 1, 0), v_off + h)),
            pl.BlockSpec((SWA_BLK, LANES), lambda h, n: (n, v_off + h)),
            pl.BlockSpec((1, gw), lambda h, n: (0, h)),
            pl.BlockSpec((SWA_BLK, gw), lambda h, n: (n, h)),
            pl.BlockSpec((SWA_GROUP, SWA_BLK, 1), lambda h, n: (h, n, 0)),
            pl.BlockSpec((SWA_BLK, gw), lambda h, n: (n, h)),
        ],
        out_specs=[
            pl.BlockSpec((SWA_BLK, gw), lambda h, n: (n, h)),
            pl.BlockSpec((S, LANES), lambda h, n: (0, h)),
            pl.BlockSpec((S, LANES), lambda h, n: (0, h)),
            pl.BlockSpec((1, gw), lambda h, n: (0, h)),
        ],
        out_shape=[
            jax.ShapeDtypeStruct((S, SWA_HEADS * LANES), F32),
            jax.ShapeDtypeStruct((S, SWA_KV_HEADS * LANES), F32),
            jax.ShapeDtypeStruct((S, SWA_KV_HEADS * LANES), F32),
            jax.ShapeDtypeStruct((1, SWA_HEADS * LANES), F32),
        ],
        compiler_params=_params(("parallel", "arbitrary")),
    )(q, k, k, v, v, sink_b, o, lse, do)


def _pad_cols(w, heads, real):
    k = w.shape[0]
    return jnp.pad(w.reshape(k, heads, real), ((0, 0), (0, 0), (0, LANES - real))).reshape(k, heads * LANES)


def _unpad_cols(g, heads, real):
    k = g.shape[0]
    return g.reshape(k, heads, LANES)[:, :, :real].reshape(k, heads * real)


def _pad_rows(w, heads, real):
    n = w.shape[1]
    return jnp.pad(w.reshape(heads, real, n), ((0, 0), (0, LANES - real), (0, 0))).reshape(heads * LANES, n)


def _unpad_rows(g, heads, real):
    n = g.shape[1]
    return g.reshape(heads, LANES, n)[:, :real, :].reshape(heads * real, n)


def _w_in_internal(w_in):
    c_q, c_kv, k_r, q_swa, k_swa, v_swa, q_sb, k_sb, v_sb, gate = jnp.split(w_in, SPLIT_POINTS, axis=1)
    k_r = jnp.pad(k_r, ((0, 0), (MLA_NOPE, LANES - MLA_NOPE - MLA_ROPE)))
    w1 = jnp.concatenate([c_q, c_kv, k_r, _pad_cols(q_swa, 8, 64), _pad_cols(k_swa, 2, 64)], axis=1)
    w2 = [_pad_cols(v_swa, 2, 64), _pad_cols(q_sb, 8, 64), _pad_cols(k_sb, 8, 64), _pad_cols(v_sb, 8, 64)]
    return w1, w2, gate


def _w_in_reference(g1, g2, g3):
    c_q, c_kv, k_r, q_swa, k_swa = jnp.split(g1, [256, 384, 512, 1536], axis=1)
    v_swa, q_sb, k_sb, v_sb = g2
    return jnp.concatenate([
        c_q, c_kv, k_r[:, MLA_NOPE:MLA_NOPE + MLA_ROPE], _unpad_cols(q_swa, 8, 64), _unpad_cols(k_swa, 2, 64),
        _unpad_cols(v_swa, 2, 64), _unpad_cols(q_sb, 8, 64), _unpad_cols(k_sb, 8, 64), _unpad_cols(v_sb, 8, 64),
        g3], axis=1)


def _w_ukv_internal(w):
    w3 = w.reshape(MLA_KV_LORA, MLA_HEADS, MLA_NOPE + MLA_V)
    pad = lambda t: jnp.pad(t, ((0, 0), (0, 0), (0, LANES - t.shape[2]))).reshape(MLA_KV_LORA, MLA_HEADS * LANES)
    return pad(w3[:, :, :MLA_NOPE]), pad(w3[:, :, MLA_NOPE:])


def _w_ukv_reference(gk, gv):
    gk = gk.reshape(MLA_KV_LORA, MLA_HEADS, LANES)[:, :, :MLA_NOPE]
    gv = gv.reshape(MLA_KV_LORA, MLA_HEADS, LANES)[:, :, :MLA_V]
    return jnp.concatenate([gk, gv], axis=2).reshape(MLA_KV_LORA, MLA_HEADS * (MLA_NOPE + MLA_V))


def _layer_fwd(x, w, tabs):
    mla_tab, swa_tab = tabs
    sv = {"x": x}

    def f_norm(rows, consts):
        return [_rms(rows[0], consts[0])], []

    (h,) = _rowwise(f_norm, [x], [w["g_mix_pre"]], [(D_MODEL, BF16)], [], "norm_mix_pre")
    p1 = _matmul(h, w["w_in1"], "nn", [F32], "proj_lat")
    p2 = _matmul(h, w["w_in2"], "nn", [BF16], "proj_qkv")
    gates = _matmul(h, w["w_in3"], "nn", [BF16], "proj_gate",
                    epilogue=lambda acc, b: (1.0 / (1.0 + jnp.exp(-(acc + b))),), row_extras=[w["b_gate"]])

    def f_prep(rows, consts):
        t = rows[0]
        gq, gkv = consts[0], consts[1]
        mc, mu, md = rows[1], rows[2], rows[3]
        sc, su, sd = rows[4], rows[5], rows[6]
        cq_n = _rms(t[:, 0:256], gq)
        ckv_n = _rms(t[:, 256:384], gkv)
        kr = _rope(t[:, 384:512], mc, mu, md, MLA_ROPE // 2)
        qs = [_rope(t[:, 512 + j * LANES:512 + (j + 1) * LANES], sc, su, sd, SWA_HEAD_DIM // 2) for j in range(8)]
        ks = [_rope(t[:, 1536 + j * LANES:1536 + (j + 1) * LANES], sc, su, sd, SWA_HEAD_DIM // 2) for j in range(2)]
        return [cq_n, ckv_n, kr, jnp.concatenate(qs, axis=1), jnp.concatenate(ks, axis=1)], []

    cq_n, ckv_n, kr, q_swa, k_swa = _rowwise(
        f_prep, [p1, *mla_tab["k"], *swa_tab["f"]], [w["g_q_lat"], w["g_kv_lat"]],
        [(256, BF16), (128, BF16), (LANES, F32), (1024, BF16), (256, BF16)], [], "lat_prep")

    q_lat = _matmul(cq_n, w["w_uq"], "nn", [F32], "mla_q_up")
    k_lat = _matmul(ckv_n, w["w_ukv_k"], "nn", [F32], "mla_k_up")
    v_mla = _matmul(ckv_n, w["w_ukv_v"], "nn", [BF16], "mla_v_up")

    def f_mla_prep(rows, consts):
        ql, kl, krr, mc, mu, md = rows
        qs = [_rope(ql[:, j * LANES:(j + 1) * LANES], mc, mu, md, MLA_ROPE // 2) for j in range(8)]
        ks = [kl[:, j * LANES:(j + 1) * LANES] + krr for j in range(8)]
        return [jnp.concatenate(qs, axis=1), jnp.concatenate(ks, axis=1)], []

    q_mla, k_mla = _rowwise(f_mla_prep, [q_lat, k_lat, kr, *mla_tab["q"]], [], [(1024, BF16), (1024, BF16)], [], "mla_prep")

    o_mla, lse_mla = _softmax_attn_fwd(q_mla, k_mla, v_mla, MLA_HEADS, (MLA_NOPE + MLA_ROPE) ** -0.5, "mla_fwd")
    o_swa, lse_swa = _swa_fwd(q_swa, k_swa, p2, 0, w["sink_b"], "swa_fwd")
    o_sb = _sb_attn_fwd(p2, SB_HEADS, "sb_fwd", 2, 10, 18)

    oa = _matmul(o_mla, w["w_o_mla"], "nn", [F32], "o_proj_mla")
    ob = _matmul(o_swa, w["w_o_swa"], "nn", [F32], "o_proj_swa")
    oc = _matmul(o_sb, w["w_o_sb"], "nn", [F32], "o_proj_sb")

    def f_mix(rows, consts):
        a, b, c, g = rows
        g = g.astype(F32)
        return [g[:, 0:1024] * a + g[:, 1024:2048] * b + g[:, 2048:3072] * c], []

    (mixed,) = _rowwise(f_mix, [oa, ob, oc, gates], [], [(D_MODEL, BF16)], [], "gate_mix")
    y = _matmul(mixed, w["w_out"], "nn", [F32], "out_proj")

    def f_res_norm(rows, consts):
        return [rows[0] + _rms(rows[1], consts[0])], []

    (x1,) = _rowwise(f_res_norm, [x, y], [w["g_mix_post"]], [(D_MODEL, F32)], [], "res_norm_mix")
    (h2,) = _rowwise(f_norm, [x1], [w["g_mlp_pre"]], [(D_MODEL, BF16)], [], "norm_mlp_pre")

    def relu2(acc):
        r = jnp.maximum(acc, 0.0)
        return acc, r * r

    up, u = _matmul(h2, w["w_up"], "nn", [BF16, BF16], "mlp_up", epilogue=relu2)
    zd = _matmul(u, w["w_down"], "nn", [F32], "mlp_down")
    (x2,) = _rowwise(f_res_norm, [x1, zd], [w["g_mlp_post"]], [(D_MODEL, F32)], [], "res_norm_mlp")

    sv.update(h=h, p1=p1, p2=p2, gates=gates, cq_n=cq_n, ckv_n=ckv_n, q_swa=q_swa, k_swa=k_swa, q_mla=q_mla,
              k_mla=k_mla, v_mla=v_mla, o_mla=o_mla, lse_mla=lse_mla, o_swa=o_swa, lse_swa=lse_swa, o_sb=o_sb,
              oa=oa, ob=ob, oc=oc, mixed=mixed, y=y, x1=x1, h2=h2, up=up, u=u, zd=zd)
    return x2, sv


def _layer_bwd(dx2, w, sv, tabs):
    mla_tab, swa_tab = tabs
    gr = {}

    def f_norm_bwd(rows, consts):
        dx, dg = _rms_bwd(rows[0], consts[0], rows[1])
        return [dx], [dg]

    def f_norm_bwd_res(rows, consts):
        dx, dg = _rms_bwd(rows[0], consts[0], rows[1])
        return [rows[2] + dx], [dg]

    dzd, gr["g_mlp_post"] = _rowwise(f_norm_bwd, [sv["zd"], dx2], [w["g_mlp_post"]], [(D_MODEL, BF16)], [D_MODEL], "b_norm_mlp_post")
    gr["w_down"] = _matmul(sv["u"], dzd, "tn", [F32], "b_w_down")
    dup = _matmul(dzd, w["w_down"], "nt", [BF16], "b_mlp_down",
                  epilogue=lambda acc, up: (acc * 2.0 * jnp.maximum(up.astype(F32), 0.0),), extras=[sv["up"]])
    gr["w_up"] = _matmul(sv["h2"], dup, "tn", [F32], "b_w_up")
    dh2 = _matmul(dup, w["w_up"], "nt", [F32], "b_mlp_up")
    dx1, gr["g_mlp_pre"] = _rowwise(f_norm_bwd_res, [sv["x1"], dh2, dx2], [w["g_mlp_pre"]], [(D_MODEL, F32)], [D_MODEL], "b_norm_mlp_pre")

    dy, gr["g_mix_post"] = _rowwise(f_norm_bwd, [sv["y"], dx1], [w["g_mix_post"]], [(D_MODEL, BF16)], [D_MODEL], "b_norm_mix_post")
    gr["w_out"] = _matmul(sv["mixed"], dy, "tn", [F32], "b_w_out")
    dmixed = _matmul(dy, w["w_out"], "nt", [F32], "b_out_proj")

    def f_mix_bwd(rows, consts):
        dm, a, b, c, g = rows
        g = g.astype(F32)
        outs, dls = [], []
        for j, o in enumerate((a, b, c)):
            gj = g[:, j * D_MODEL:(j + 1) * D_MODEL]
            outs.append(dm * gj)
            dls.append(dm * o * gj * (1.0 - gj))
        dl = jnp.concatenate(dls, axis=1)
        return outs + [dl], [dl]

    doa, dob, doc, dlogit, gr["b_gate"] = _rowwise(
        f_mix_bwd, [dmixed, sv["oa"], sv["ob"], sv["oc"], sv["gates"]], [],
        [(D_MODEL, BF16)] * 3 + [(P3_W, BF16)], [P3_W], "b_gate_mix")

    gr["w_o_mla"] = _matmul(sv["o_mla"], doa, "tn", [F32], "b_w_o_mla")
    gr["w_o_swa"] = _matmul(sv["o_swa"], dob, "tn", [F32], "b_w_o_swa")
    gr["w_o_sb"] = _matmul(sv["o_sb"], doc, "tn", [F32], "b_w_o_sb")
    do_mla = _matmul(doa, w["w_o_mla"], "nt", [BF16], "b_o_proj_mla")
    do_swa = _matmul(dob, w["w_o_swa"], "nt", [BF16], "b_o_proj_swa")
    do_sb = _matmul(doc, w["w_o_sb"], "nt", [BF16], "b_o_proj_sb")

    dq_sb, dk_sb, dv_sb = _sb_attn_bwd(sv["p2"], sv["o_sb"], do_sb, SB_HEADS, "sb_bwd", 2, 10, 18)
    dq_swa, dk_swa, dv_swa, dsink = _swa_bwd(sv["q_swa"], sv["k_swa"], sv["p2"], 0, w["sink_b"], sv["o_swa"],
                                             sv["lse_swa"], do_swa, "swa_bwd")
    gr["swa_sinks"] = dsink.reshape(SWA_HEADS, LANES)[:, 0]
    dq_mla, dk_mla, dv_mla = _softmax_attn_bwd(sv["q_mla"], sv["k_mla"], sv["v_mla"], sv["o_mla"], sv["lse_mla"], do_mla,
                                               MLA_HEADS, (MLA_NOPE + MLA_ROPE) ** -0.5, "mla_bwd")

    def f_mla_post(rows, consts):
        dq, dk, qc, qu, qd, kc, ku, kd = rows
        dqs = [_rope(dq[:, j * LANES:(j + 1) * LANES], qc, qu, qd, MLA_ROPE // 2) for j in range(8)]
        dkr = dk[:, 0:LANES]
        for j in range(1, 8):
            dkr = dkr + dk[:, j * LANES:(j + 1) * LANES]
        return [jnp.concatenate(dqs, axis=1), _rope(dkr, kc, ku, kd, MLA_ROPE // 2)], []

    dq_lat, dkr = _rowwise(f_mla_post, [dq_mla, dk_mla, *mla_tab["q_inv"], *mla_tab["k_inv"]], [],
                           [(1024, BF16), (LANES, F32)], [], "b_mla_post")
    gr["w_uq"] = _matmul(sv["cq_n"], dq_lat, "tn", [F32], "b_w_uq")
    gr["w_ukv_k"] = _matmul(sv["ckv_n"], dk_mla, "tn", [F32], "b_w_ukv_k")
    gr["w_ukv_v"] = _matmul(sv["ckv_n"], dv_mla, "tn", [F32], "b_w_ukv_v")
    dcq_n = _matmul(dq_lat, w["w_uq"], "nt", [F32], "b_mla_q_up")
    dckv_a = _matmul(dk_mla, w["w_ukv_k"], "nt", [F32], "b_mla_k_up")
    dckv_b = _matmul(dv_mla, w["w_ukv_v"], "nt", [F32], "b_mla_v_up")

    def f_prep_bwd(rows, consts):
        t, dcq, dca, dcb, dkr_, dqs, dks, sc, su, sd = rows
        gq, gkv = consts
        dc_q, dgq = _rms_bwd(t[:, 0:256], gq, dcq)
        dc_kv, dgkv = _rms_bwd(t[:, 256:384], gkv, dca + dcb)
        q_parts = [_rope(dqs[:, j * LANES:(j + 1) * LANES], sc, su, sd, SWA_HEAD_DIM // 2) for j in range(8)]
        k_parts = [_rope(dks[:, j * LANES:(j + 1) * LANES], sc, su, sd, SWA_HEAD_DIM // 2) for j in range(2)]
        return [jnp.concatenate([dc_q, dc_kv, dkr_] + q_parts + k_parts, axis=1)], [dgq, dgkv]

    dp1, gr["g_q_lat"], gr["g_kv_lat"] = _rowwise(
        f_prep_bwd, [sv["p1"], dcq_n, dckv_a, dckv_b, dkr, dq_swa, dk_swa, *swa_tab["inv"]], [w["g_q_lat"], w["g_kv_lat"]],
        [(P1_W, BF16)], [256, 128], "b_lat_prep")

    gr["w_in1"] = _matmul(sv["h"], dp1, "tn", [F32], "b_w_in_lat")
    dh = _matmul(dp1, w["w_in1"], "nt", [F32], "b_proj_lat")
    gr["w_in2"] = []
    add_prev = lambda acc, prev: (acc + prev,)
    for piece, wp, tag in zip((dv_swa, dq_sb, dk_sb, dv_sb), w["w_in2_parts"], ("vswa", "qsb", "ksb", "vsb")):
        gr["w_in2"].append(_matmul(sv["h"], piece, "tn", [F32], "b_w_in_" + tag))
        dh = _matmul(piece, wp, "nt", [F32], "b_proj_" + tag, epilogue=add_prev, extras=[dh])
    gr["w_in3"] = _matmul(sv["h"], dlogit, "tn", [F32], "b_w_in_gate")
    dh = _matmul(dlogit, w["w_in3"], "nt", [F32], "b_proj_gate", epilogue=add_prev, extras=[dh])
    dx, gr["g_mix_pre"] = _rowwise(f_norm_bwd_res, [sv["x"], dh, dx1], [w["g_mix_pre"]], [(D_MODEL, F32)], [D_MODEL], "b_norm_mix_pre")
    return dx, gr


def _local_step(x, positions, loss_target, full):
    mc, mu, md = _rope_tables(positions, MLA_NOPE, MLA_ROPE, True)
    kc, ku, kd = _rope_tables(positions, MLA_NOPE, MLA_ROPE, False)
    sc, su, sd = _rope_tables(positions, 0, SWA_HEAD_DIM, False)
    mla_tab = {"q": (mc, mu, md), "k": (kc, ku, kd), "q_inv": (mc, -mu, -md), "k_inv": (kc, -ku, -kd)}
    swa_tab = {"f": (sc, su, sd), "inv": (sc, -su, -sd)}
    tabs = (mla_tab, swa_tab)

    layers = []
    for l in range(DEPTH):
        w1, w2, w3 = _w_in_internal(full["w_in"][l].astype(BF16))
        uk, uv = _w_ukv_internal(full["w_ukv"][l].astype(BF16))
        layers.append({
            "w_in1": w1, "w_in2": jnp.concatenate(w2, axis=1), "w_in2_parts": w2, "w_in3": w3,
            "w_uq": _pad_cols(full["w_uq"][l].astype(BF16), MLA_HEADS, MLA_NOPE + MLA_ROPE),
            "w_ukv_k": uk, "w_ukv_v": uv,
            "w_o_mla": _pad_rows(full["w_o_mla"][l].astype(BF16), 8, 64),
            "w_o_swa": _pad_rows(full["w_o_swa"][l].astype(BF16), 8, 64),
            "w_o_sb": _pad_rows(full["w_o_sb"][l].astype(BF16), 8, 64),
            "w_out": full["w_out"][l].astype(BF16), "w_up": full["w_up"][l].astype(BF16),
            "w_down": full["w_down"][l].astype(BF16),
            "g_mix_pre": full["g_mix_pre"][l][None], "b_gate": full["b_gate"][l][None],
            "g_q_lat": full["g_q_lat"][l][None], "g_kv_lat": full["g_kv_lat"][l][None],
            "g_mix_post": full["g_mix_post"][l][None], "g_mlp_pre": full["g_mlp_pre"][l][None],
            "g_mlp_post": full["g_mlp_post"][l][None],
            "sink_b": jnp.repeat(full["swa_sinks"][l], LANES)[None],
        })

    saved = []
    h = x
    for l in range(DEPTH):
        h, sv = _layer_fwd(h, layers[l], tabs)
        saved.append(sv)

    def f_loss(rows, consts):
        err = rows[0] - rows[1]
        return [err * (1.0 / D_MODEL)], [jnp.sum(err * err, axis=1, keepdims=True)]

    dy, sq = _rowwise(f_loss, [h, loss_target], [], [(D_MODEL, F32)], [1], "loss_head")
    loss_part = sq * (0.5 / D_MODEL)

    grads = [None] * DEPTH
    d = dy
    for l in reversed(range(DEPTH)):
        d, gr = _layer_bwd(d, layers[l], saved[l], tabs)
        grads[l] = {
            "g_mix_pre": gr["g_mix_pre"][0], "w_in": _w_in_reference(gr["w_in1"], gr["w_in2"], gr["w_in3"]),
            "b_gate": gr["b_gate"][0], "g_q_lat": gr["g_q_lat"][0], "g_kv_lat": gr["g_kv_lat"][0],
            "w_uq": _unpad_cols(gr["w_uq"], MLA_HEADS, MLA_NOPE + MLA_ROPE),
            "w_ukv": _w_ukv_reference(gr["w_ukv_k"], gr["w_ukv_v"]), "swa_sinks": gr["swa_sinks"],
            "w_o_mla": _unpad_rows(gr["w_o_mla"], 8, 64), "w_o_swa": _unpad_rows(gr["w_o_swa"], 8, 64),
            "w_o_sb": _unpad_rows(gr["w_o_sb"], 8, 64), "w_out": gr["w_out"], "g_mix_post": gr["g_mix_post"][0],
            "g_mlp_pre": gr["g_mlp_pre"][0], "w_up": gr["w_up"], "w_down": gr["w_down"], "g_mlp_post": gr["g_mlp_post"][0],
        }
    stacked = {n: jnp.stack([grads[l][n] for l in range(DEPTH)]) for n in WEIGHTS}
    return loss_part, d, stacked


def _rows_of(a):
    return a.reshape(-1, LANES)


def _small_rows(d):
    parts = []
    for n in SMALL:
        a = d[n]
        if a.shape[1] < LANES:
            a = jnp.pad(a, ((0, 0), (0, LANES - a.shape[1])))
        parts.append(_rows_of(a))
    return parts


def _pack(shards, small, dtype):
    parts = [_rows_of(shards[n]) for n in SHARDED]
    if small is not None:
        parts += _small_rows(small)
    slab = jnp.concatenate(parts, axis=0).astype(dtype)
    pad = (-slab.shape[0]) % SLAB_ROW_ALIGN
    return jnp.pad(slab, ((0, pad), (0, 0)))


def _unpack(slab, shard_shapes, small_shapes):
    out, r = {}, 0
    for n in SHARDED:
        rows = int(np.prod(shard_shapes[n])) // LANES
        out[n] = slab[r:r + rows].reshape(shard_shapes[n])
        r += rows
    if small_shapes is not None:
        for n in SMALL:
            depth, width = small_shapes[n]
            rows = depth * max(width, LANES) // LANES
            out[n] = slab[r:r + rows].reshape(depth, max(width, LANES))[:, :width]
            r += rows
    return out


def _chip_exchange(src, name):
    rows = src.shape[-2]

    def body(src_ref, out_ref, send_sems, recv_sems, local_sem):
        x, y, c = lax.axis_index("x"), lax.axis_index("y"), lax.axis_index("c")
        me = 2 * x + y
        chips = [(1 - x, y), (x, 1 - y), (1 - x, 1 - y)]
        mine = pltpu.make_async_copy(src_ref.at[me], out_ref.at[me], local_sem)
        mine.start()
        sends = []
        for k, (cx, cy) in enumerate(chips):
            cp = pltpu.make_async_remote_copy(
                src_ref=src_ref.at[2 * cx + cy], dst_ref=out_ref.at[me], send_sem=send_sems.at[k],
                recv_sem=recv_sems.at[k], device_id=(cx, cy, c), device_id_type=pl.DeviceIdType.MESH)
            cp.start()
            sends.append(cp)
        for k, (cx, cy) in enumerate(chips):
            pltpu.make_async_remote_copy(
                src_ref=src_ref.at[me], dst_ref=out_ref.at[2 * cx + cy], send_sem=send_sems.at[k],
                recv_sem=recv_sems.at[k], device_id=(cx, cy, c), device_id_type=pl.DeviceIdType.MESH).wait_recv()
        for cp in sends:
            cp.wait_send()
        mine.wait()

    return pl.pallas_call(
        body,
        name=name,
        in_specs=[pl.BlockSpec(memory_space=pl.ANY)],
        out_specs=pl.BlockSpec(memory_space=pl.ANY),
        out_shape=jax.ShapeDtypeStruct((N_CHIPS, rows, LANES), src.dtype),
        scratch_shapes=[pltpu.SemaphoreType.DMA((3,)), pltpu.SemaphoreType.DMA((3,)), pltpu.SemaphoreType.DMA],
    )(src)


def _half_rows(c, half):
    return pl.ds(pl.multiple_of(c * half, SLAB_ROW_ALIGN // 2), half)


def _gather_weights(src, name):
    rows = src.shape[0]
    half = rows // 2

    def body(src_ref, out_ref, send_sems, recv_sems, local_sem):
        x, y, c = lax.axis_index("x"), lax.axis_index("y"), lax.axis_index("c")
        me = 2 * x + y
        chips = [(1 - x, y), (x, 1 - y), (1 - x, 1 - y)]

        def copy(k, src_view, slab, part, to):
            return pltpu.make_async_remote_copy(
                src_ref=src_view, dst_ref=out_ref.at[slab, _half_rows(part, half), :], send_sem=send_sems.at[k],
                recv_sem=recv_sems.at[k], device_id=to, device_id_type=pl.DeviceIdType.MESH)

        mine = pltpu.make_async_copy(src_ref, out_ref.at[me], local_sem)
        mine.start()
        sends = [copy(k, src_ref.at[_half_rows(c, half), :], me, c, (cx, cy, c)) for k, (cx, cy) in enumerate(chips)]
        for cp in sends:
            cp.start()
        for k, (cx, cy) in enumerate(chips):
            j = 2 * cx + cy
            landed = out_ref.at[j, _half_rows(c, half), :]
            copy(k, landed, j, c, (cx, cy, c)).wait_recv()
            fwd = copy(3 + k, landed, j, c, (x, y, 1 - c))
            fwd.start()
            sends.append(fwd)
        for k, (cx, cy) in enumerate(chips):
            j = 2 * cx + cy
            copy(3 + k, out_ref.at[j, _half_rows(1 - c, half), :], j, 1 - c, (x, y, 1 - c)).wait_recv()
        for cp in sends:
            cp.wait_send()
        mine.wait()

    return pl.pallas_call(
        body,
        name=name,
        in_specs=[pl.BlockSpec(memory_space=pl.ANY)],
        out_specs=pl.BlockSpec(memory_space=pl.ANY),
        out_shape=jax.ShapeDtypeStruct((N_CHIPS, rows, LANES), src.dtype),
        scratch_shapes=[pltpu.SemaphoreType.DMA((6,)), pltpu.SemaphoreType.DMA((6,)), pltpu.SemaphoreType.DMA],
    )(src)


def _sibling_halves(src, name):
    n, rows, _ = src.shape
    half = rows // 2

    def body(src_ref, out_ref, send_sem, recv_sem):
        x, y, c = lax.axis_index("x"), lax.axis_index("y"), lax.axis_index("c")
        cp = pltpu.make_async_remote_copy(
            src_ref=src_ref.at[:, _half_rows(1 - c, half), :], dst_ref=out_ref, send_sem=send_sem, recv_sem=recv_sem,
            device_id=(x, y, 1 - c), device_id_type=pl.DeviceIdType.MESH)
        cp.start()
        cp.wait()

    return pl.pallas_call(
        body,
        name=name,
        in_specs=[pl.BlockSpec(memory_space=pl.ANY)],
        out_specs=pl.BlockSpec(memory_space=pl.ANY),
        out_shape=jax.ShapeDtypeStruct((n, half, LANES), src.dtype),
        scratch_shapes=[pltpu.SemaphoreType.DMA, pltpu.SemaphoreType.DMA],
    )(src)


def _sibling_join(src, name):
    half = src.shape[0]

    def body(src_ref, out_ref, send_sem, recv_sem, local_sem):
        x, y, c = lax.axis_index("x"), lax.axis_index("y"), lax.axis_index("c")
        mine = pltpu.make_async_copy(src_ref, out_ref.at[_half_rows(c, half), :], local_sem)
        mine.start()
        cp = pltpu.make_async_remote_copy(
            src_ref=src_ref, dst_ref=out_ref.at[_half_rows(c, half), :], send_sem=send_sem, recv_sem=recv_sem,
            device_id=(x, y, 1 - c), device_id_type=pl.DeviceIdType.MESH)
        cp.start()
        pltpu.make_async_remote_copy(
            src_ref=src_ref, dst_ref=out_ref.at[_half_rows(1 - c, half), :], send_sem=send_sem, recv_sem=recv_sem,
            device_id=(x, y, 1 - c), device_id_type=pl.DeviceIdType.MESH).wait_recv()
        cp.wait_send()
        mine.wait()

    return pl.pallas_call(
        body,
        name=name,
        in_specs=[pl.BlockSpec(memory_space=pl.ANY)],
        out_specs=pl.BlockSpec(memory_space=pl.ANY),
        out_shape=jax.ShapeDtypeStruct((2 * half, LANES), src.dtype),
        scratch_shapes=[pltpu.SemaphoreType.DMA, pltpu.SemaphoreType.DMA, pltpu.SemaphoreType.DMA],
    )(src)


SUM_ROWS = 1024


def _pair_sum(mine, theirs, c, name):
    n, half, _ = theirs.shape
    blocks = half // SUM_ROWS

    def body(c_ref, a_ref, b_ref, o_ref):
        o_ref[...] = (a_ref[...] + b_ref[...]).astype(o_ref.dtype)

    return pl.pallas_call(
        body,
        name=name,
        grid_spec=pltpu.PrefetchScalarGridSpec(
            num_scalar_prefetch=1,
            grid=(blocks,),
            in_specs=[pl.BlockSpec((n, SUM_ROWS, LANES), lambda i, c_ref: (0, c_ref[0] * blocks + i, 0)),
                      pl.BlockSpec((n, SUM_ROWS, LANES), lambda i, c_ref: (0, i, 0))],
            out_specs=pl.BlockSpec((n, SUM_ROWS, LANES), lambda i, c_ref: (0, i, 0)),
        ),
        out_shape=jax.ShapeDtypeStruct((n, half, LANES), BF16),
        compiler_params=_params(("arbitrary",)),
    )(jnp.reshape(c, (1,)).astype(jnp.int32), mine, theirs)


def _sum_chips(buf, name):
    rows = buf.shape[1]

    def body(b_ref, o_ref):
        t = [b_ref[j].astype(F32) for j in range(N_CHIPS)]
        o_ref[...] = ((t[0] + t[1]) + t[2]) + t[3]

    return pl.pallas_call(
        body,
        name=name,
        grid=(rows // SUM_ROWS,),
        in_specs=[pl.BlockSpec((N_CHIPS, SUM_ROWS, LANES), lambda i: (0, i, 0))],
        out_specs=pl.BlockSpec((SUM_ROWS, LANES), lambda i: (i, 0)),
        out_shape=jax.ShapeDtypeStruct((rows, LANES), F32),
        compiler_params=_params(("arbitrary",)),
    )(buf)


def _adamw(w, m, v, g, name):
    shape = w.shape
    flat = lambda a: a.reshape(-1, shape[-1])

    def fn(rows, consts):
        w_, m_, v_, g_ = rows
        m_new = ADAM_B1 * m_ + (1.0 - ADAM_B1) * g_
        v_new = ADAM_B2 * v_ + (1.0 - ADAM_B2) * (g_ * g_)
        m_hat = m_new / (1.0 - ADAM_B1 ** ADAM_STEP)
        v_hat = v_new / (1.0 - ADAM_B2 ** ADAM_STEP)
        delta = -ADAM_LR * (m_hat / (jnp.sqrt(v_hat) + ADAM_EPS) + ADAM_WD * w_)
        return [delta, m_new, v_new], []

    outs = _rowwise(fn, [flat(w), flat(m), flat(v), flat(g)], [], [(shape[-1], F32)] * 3, [], name, bm=256)
    return [o.reshape(shape) for o in outs]


def kernel(x, positions, g_mix_pre, w_in, b_gate, g_q_lat, g_kv_lat, w_uq, w_ukv, swa_sinks, w_o_mla, w_o_swa, w_o_sb, w_out, g_mix_post, g_mlp_pre, w_up, w_down, g_mlp_post, loss_target, m_g_mix_pre, m_w_in, m_b_gate, m_g_q_lat, m_g_kv_lat, m_w_uq, m_w_ukv, m_swa_sinks, m_w_o_mla, m_w_o_swa, m_w_o_sb, m_w_out, m_g_mix_post, m_g_mlp_pre, m_w_up, m_w_down, m_g_mlp_post, v_g_mix_pre, v_w_in, v_b_gate, v_g_q_lat, v_g_kv_lat, v_w_uq, v_w_ukv, v_swa_sinks, v_w_o_mla, v_w_o_swa, v_w_o_sb, v_w_out, v_g_mix_post, v_g_mlp_pre, v_w_up, v_w_down, v_g_mlp_post):
    given = dict(locals())
    wts = {n: given[n] for n in WEIGHTS}
    mom_m = {n: given["m_" + n] for n in WEIGHTS}
    mom_v = {n: given["v_" + n] for n in WEIGHTS}
    shard_shapes = {n: wts[n].shape for n in SHARDED}
    small_shapes = {n: wts[n].shape for n in SMALL}

    gathered = _gather_weights(_pack(wts, None, BF16), "gather_weights")
    full = {n: wts[n] for n in SMALL}
    per_chip = [_unpack(gathered[j], shard_shapes, None) for j in range(N_CHIPS)]
    for n in SHARDED:
        full[n] = jnp.concatenate([per_chip[j][n] for j in range(N_CHIPS)], axis=SHARD_AXIS[n])

    loss_part, grad_x, grads = _local_step(x[0], positions[0], loss_target[0], full)
    loss = lax.psum(loss_part[0, 0], ("x", "y", "c"))

    small_g = {n: grads[n] for n in SMALL}
    slabs = []
    for j in range(N_CHIPS):
        shard = {n: jnp.split(grads[n], N_CHIPS, axis=SHARD_AXIS[n])[j] for n in SHARDED}
        slabs.append(_pack(shard, small_g, F32))
    per_chip_g = jnp.stack(slabs)
    theirs = _sibling_halves(per_chip_g, "pair_grads")
    pair = _pair_sum(per_chip_g, theirs, lax.axis_index("c"), "sum_pair")
    landed = _chip_exchange(pair, "scatter_grads")
    g_slab = _sibling_join(_sum_chips(landed, "sum_chips"), "join_grads")

    g = _unpack(g_slab, shard_shapes, small_shapes)
    stepped = {n: _adamw(wts[n], mom_m[n], mom_v[n], g[n], "adamw_" + n) for n in WEIGHTS}
    outs = [loss, grad_x[None]] + [g[n] for n in WEIGHTS]
    for part in range(3):
        outs += [stepped[n][part] for n in WEIGHTS]
    return tuple(outs)
```

```python
import numpy as np
import jax
import jax.numpy as jnp
from jax import lax
from jax.experimental import pallas as pl
from jax.experimental.pallas import tpu as pltpu

F32 = jnp.float32
BF16 = jnp.bfloat16

D_MODEL = 1024
DEPTH = 4
MLA_HEADS, MLA_Q_LORA, MLA_KV_LORA, MLA_NOPE, MLA_ROPE, MLA_V = 8, 256, 128, 64, 32, 64
SWA_HEADS, SWA_KV_HEADS, SWA_HEAD_DIM, SWA_WINDOW = 8, 2, 64, 128
SB_HEADS, SB_HEAD_DIM = 8, 64
D_FF = 4 * D_MODEL
ROPE_THETA = 10000.0
EPS = 1e-6
SPLIT_SIZES = (256, 128, 32, 512, 128, 128, 512, 512, 512, 3 * D_MODEL)
SPLIT_POINTS = [int(v) for v in np.cumsum(SPLIT_SIZES)[:-1]]

ADAM_LR, ADAM_B1, ADAM_B2, ADAM_EPS, ADAM_WD, ADAM_STEP = 0.001, 0.9, 0.999, 1e-08, 0.01, 10

LANES = 128
V7X_VMEM_BYTES = 64 * 1024 * 1024
VMEM_LIMIT = V7X_VMEM_BYTES - 8 * 1024 * 1024
MATMUL_VMEM_BUDGET = 36 * 1024 * 1024
N_CHIPS = 4
SLAB_ROW_ALIGN = 512

P1_W = 256 + 128 + 128 + 1024 + 256
P2_W = 256 + 1024 + 1024 + 1024
P3_W = 3 * D_MODEL

SHARDED = ("w_in", "w_uq", "w_ukv", "w_o_mla", "w_o_swa", "w_o_sb", "w_out", "w_up", "w_down")
SHARD_AXIS = {"w_in": 2, "w_uq": 2, "w_ukv": 2, "w_o_mla": 2, "w_o_swa": 2, "w_o_sb": 2, "w_out": 1, "w_up": 2, "w_down": 1}
SMALL = ("g_mix_pre", "b_gate", "g_q_lat", "g_kv_lat", "swa_sinks", "g_mix_post", "g_mlp_pre", "g_mlp_post")
WEIGHTS = ("g_mix_pre", "w_in", "b_gate", "g_q_lat", "g_kv_lat", "w_uq", "w_ukv", "swa_sinks", "w_o_mla", "w_o_swa",
           "w_o_sb", "w_out", "g_mix_post", "g_mlp_pre", "w_up", "w_down", "g_mlp_post")

NN = (((1,), (0,)), ((), ()))
NT = (((1,), (1,)), ((), ()))
TN = (((0,), (0,)), ((), ()))


def _dot(a, b, dims):
    return lax.dot_general(a, b, dims, preferred_element_type=F32)


def _params(sem):
    return pltpu.CompilerParams(dimension_semantics=sem, vmem_limit_bytes=VMEM_LIMIT)


def _largest_tile(n, cap):
    if n <= cap:
        return n
    best = LANES
    for t in range(LANES, cap + 1, LANES):
        if n % t == 0:
            best = t
    return best


def _matmul_tiles(M, N, K, a_bytes, b_bytes, out_bytes, extra_bytes):
    tn = _largest_tile(N, 1792)
    tm = _largest_tile(M, 1024 if tn <= 1024 else 512)
    tk = _largest_tile(K, 2048)

    def need(tm_, tk_):
        acc = 4 * tm_ * tn if tk_ < K else 0
        return 2 * (tm_ * tk_ * a_bytes + tk_ * tn * b_bytes + tm_ * tn * (out_bytes + extra_bytes)) + acc

    while need(tm, tk) > MATMUL_VMEM_BUDGET:
        if tk >= tm and tk % 256 == 0:
            tk //= 2
        elif tm % 256 == 0:
            tm //= 2
        else:
            break
    return tm, tn, tk


def _matmul(a, b, mode, out_dtypes, name, epilogue=None, extras=(), row_extras=()):
    if mode == "nn":
        (M, K), (K2, N) = a.shape, b.shape
    elif mode == "nt":
        (M, K), (N, K2) = a.shape, b.shape
    else:
        (K, M), (K2, N) = a.shape, b.shape
    assert K == K2, (name, a.shape, b.shape)
    tm, tn, tk = _matmul_tiles(
        M, N, K, a.dtype.itemsize, b.dtype.itemsize, sum(jnp.dtype(d).itemsize for d in out_dtypes),
        sum(e.dtype.itemsize for e in extras))
    assert M % tm == 0 and N % tn == 0 and K % tk == 0, (name, M, N, K, tm, tn, tk)
    nk = K // tk
    if mode == "tn":
        a_spec = pl.BlockSpec((tk, tm), lambda i, j, k: (k, i))
    else:
        a_spec = pl.BlockSpec((tm, tk), lambda i, j, k: (i, k))
    if mode == "nt":
        b_spec = pl.BlockSpec((tn, tk), lambda i, j, k: (j, k))
    else:
        b_spec = pl.BlockSpec((tk, tn), lambda i, j, k: (k, j))
    dims = {"nn": NN, "nt": NT, "tn": TN}[mode]
    n_ex, n_rex, n_out = len(extras), len(row_extras), len(out_dtypes)

    def body(*refs):
        a_ref, b_ref = refs[:2]
        ex = refs[2:2 + n_ex]
        rex = refs[2 + n_ex:2 + n_ex + n_rex]
        outs = refs[2 + n_ex + n_rex:2 + n_ex + n_rex + n_out]

        def finish(total):
            res = (total,) if epilogue is None else epilogue(total, *[e[...] for e in ex], *[e[...] for e in rex])
            for o, r in zip(outs, res):
                o[...] = r.astype(o.dtype)

        part = _dot(a_ref[...].astype(BF16), b_ref[...].astype(BF16), dims)
        if nk == 1:
            finish(part)
            return
        acc = refs[-1]
        k = pl.program_id(2)

        @pl.when(k == 0)
        def _():
            acc[...] = part

        @pl.when(k > 0)
        def _():
            acc[...] += part

        @pl.when(k == nk - 1)
        def _():
            finish(acc[...])

    in_specs = [a_spec, b_spec]
    in_specs += [pl.BlockSpec((tm, tn), lambda i, j, k: (i, j)) for _ in extras]
    in_specs += [pl.BlockSpec((1, tn), lambda i, j, k: (0, j)) for _ in row_extras]
    out = pl.pallas_call(
        body,
        name=name,
        grid=(M // tm, N // tn, nk),
        in_specs=in_specs,
        out_specs=[pl.BlockSpec((tm, tn), lambda i, j, k: (i, j)) for _ in out_dtypes],
        out_shape=[jax.ShapeDtypeStruct((M, N), dt) for dt in out_dtypes],
        scratch_shapes=[pltpu.VMEM((tm, tn), F32)] if nk > 1 else [],
        compiler_params=_params(("parallel", "parallel", "arbitrary")),
    )(a, b, *extras, *row_extras)
    return out[0] if n_out == 1 else out


def _rowwise(fn, rows, consts, out_defs, sum_widths, name, bm=256):
    R = rows[0].shape[0]
    bm = min(bm, R)
    assert R % bm == 0, (name, R, bm)
    n_r, n_c, n_o = len(rows), len(consts), len(out_defs)
    n_s = len(sum_widths)

    def body(*refs):
        r_in = refs[:n_r]
        c_in = refs[n_r:n_r + n_c]
        o_refs = refs[n_r + n_c:n_r + n_c + n_o]
        s_refs = refs[n_r + n_c + n_o:]
        outs, sums = fn([r[...] for r in r_in], [c[...] for c in c_in])
        for o, val in zip(o_refs, outs):
            o[...] = val.astype(o.dtype)
        if n_s:
            @pl.when(pl.program_id(0) == 0)
            def _():
                for s in s_refs:
                    s[...] = jnp.zeros_like(s)

            for s, val in zip(s_refs, sums):
                s[...] += jnp.sum(val, axis=0, keepdims=True)

    in_specs = [pl.BlockSpec((bm, r.shape[1]), lambda i: (i, 0)) for r in rows]
    in_specs += [pl.BlockSpec(c.shape, lambda i: (0, 0)) for c in consts]
    out_specs = [pl.BlockSpec((bm, w), lambda i: (i, 0)) for w, _ in out_defs]
    out_specs += [pl.BlockSpec((1, w), lambda i: (0, 0)) for w in sum_widths]
    out_shape = [jax.ShapeDtypeStruct((R, w), dt) for w, dt in out_defs]
    out_shape += [jax.ShapeDtypeStruct((1, w), F32) for w in sum_widths]
    return pl.pallas_call(
        body,
        name=name,
        grid=(R // bm,),
        in_specs=in_specs,
        out_specs=out_specs,
        out_shape=out_shape,
        compiler_params=_params(("arbitrary",)),
    )(*rows, *consts)


def _rms(x, g):
    r = lax.rsqrt(jnp.mean(x * x, axis=-1, keepdims=True) + EPS)
    return x * r * g


def _rms_bwd(x, g, dy):
    r = lax.rsqrt(jnp.mean(x * x, axis=-1, keepdims=True) + EPS)
    n = x * r
    dn = dy * g
    dx = r * (dn - n * jnp.mean(dn * n, axis=-1, keepdims=True))
    return dx, dy * n


def _rope(x, c, s_up, s_dn, half):
    return x * c + pltpu.roll(x, half, 1) * s_up + pltpu.roll(x, LANES - half, 1) * s_dn


def _rope_tables(positions, lo, d, nope_pass):
    S = positions.shape[0]
    half = d // 2
    inv = 1.0 / (ROPE_THETA ** (jnp.arange(0, d, 2, dtype=F32) / d))
    ang = positions.astype(F32)[:, None] * inv
    cos, sin = jnp.cos(ang), jnp.sin(ang)
    z = lambda n: jnp.zeros((S, n), F32)
    head = jnp.ones((S, lo), F32) if nope_pass else z(lo)
    tail = LANES - lo - d
    c = jnp.concatenate([head, cos, cos, z(tail)], axis=1)
    s_up = jnp.concatenate([z(lo), z(half), sin, z(tail)], axis=1)
    s_dn = jnp.concatenate([z(lo), -sin, z(half), z(tail)], axis=1)
    return c, s_up, s_dn


MLA_FWD_CFG = (1, 1024)
MLA_BWD_CFG = (2, 512)
SB_FWD_CFG = (8, 256)
SB_BWD_CFG = (4, 256)


def _tile_mask(bk, strict):
    row = lax.broadcasted_iota(jnp.int32, (bk, bk), 0)
    col = lax.broadcasted_iota(jnp.int32, (bk, bk), 1)
    return (col < row) if strict else (col <= row)


def _att_layout(S, cfg):
    nch, bk = cfg
    bq = nch * bk
    assert S % bq == 0, (S, cfg)
    rows = [slice(r * bk, (r + 1) * bk) for r in range(nch)]
    q_spec = lambda off=0: pl.BlockSpec((bq, LANES), lambda h, i: (i, off + h))
    kv_spec = lambda off=0: pl.BlockSpec((S, LANES), lambda h, i: (0, off + h))
    return bq, rows, q_spec, kv_spec


def _total(terms):
    terms = list(terms)
    out = terms[0]
    for t in terms[1:]:
        out = out + t
    return out


def _walk(nch, i, step, carry, leftward, alive=None):
    everyone = range(nch)
    if leftward:
        for d in reversed(everyone):
            carry = step(nch * i + d, carry, range(d, nch), {d})
        if alive is None:
            return lax.fori_loop(0, nch * i, lambda t, c: step(nch * i - 1 - t, c, everyone, set()), carry)
        more = lambda tc: jnp.logical_and(tc[0] < nch * i, alive(tc[1]))
        left = lambda tc: (tc[0] + 1, step(nch * i - 1 - tc[0], tc[1], everyone, set()))
        return lax.while_loop(more, left, (jnp.int32(0), carry))[1]
    carry = lax.fori_loop(0, nch * i, lambda kb, c: step(kb, c, everyone, set()), carry)
    for d in everyone:
        carry = step(nch * i + d, carry, range(d, nch), {d})
    return carry


def _softmax_attn_fwd(q, k, v, heads, scale, name, q_off=0, k_off=0, v_off=0):
    S = q.shape[0]
    nch, bk = MLA_FWD_CFG
    bq, rows, q_spec, kv_spec = _att_layout(S, MLA_FWD_CFG)

    def body(q_ref, k_ref, v_ref, o_ref, lse_ref):
        i = pl.program_id(1)
        qs = [q_ref[rw, :] for rw in rows]

        def step(kb, cs, active, masked):
            off = pl.multiple_of(kb * bk, bk)
            ks, vs = k_ref[pl.ds(off, bk), :], v_ref[pl.ds(off, bk), :]
            A = list(active)
            s = {r: _dot(qs[r], ks, NT) * scale for r in A}
            s = {r: (jnp.where(_tile_mask(bk, False), s[r], -1e30) if r in masked else s[r]) for r in A}
            m_new = {r: jnp.maximum(cs[r][0], jnp.max(s[r], axis=1, keepdims=True)) for r in A}
            p = {r: jnp.exp(s[r] - m_new[r]) for r in A}
            alpha = {r: jnp.exp(cs[r][0] - m_new[r]) for r in A}
            new = list(cs)
            for r in A:
                new[r] = (m_new[r], alpha[r] * cs[r][1] + jnp.sum(p[r], axis=1, keepdims=True),
                          alpha[r] * cs[r][2] + _dot(p[r].astype(BF16), vs, NN))
            return tuple(new)

        init = (jnp.full((bk, 1), -1e30, F32), jnp.zeros((bk, 1), F32), jnp.zeros((bk, LANES), F32))
        cs = _walk(nch, i, step, tuple(init for _ in rows), False)
        for r, (m, l, acc) in enumerate(cs):
            o_ref[rows[r], :] = (acc / l).astype(o_ref.dtype)
            lse_ref[rows[r], :] = m + jnp.log(l)

    return pl.pallas_call(
        body,
        name=name,
        grid=(heads, S // bq),
        in_specs=[q_spec(q_off), kv_spec(k_off), kv_spec(v_off)],
        out_specs=[q_spec(), pl.BlockSpec((None, bq, 1), lambda h, i: (h, i, 0))],
        out_shape=[jax.ShapeDtypeStruct((S, heads * LANES), BF16), jax.ShapeDtypeStruct((heads, S, 1), F32)],
        compiler_params=_params(("parallel", "arbitrary")),
    )(q, k, v)


def _softmax_attn_bwd(q, k, v, o, lse, do, heads, scale, name, q_off=0, k_off=0, v_off=0):
    S = q.shape[0]
    nch, bk = MLA_BWD_CFG
    bq, rows, q_spec, kv_spec = _att_layout(S, MLA_BWD_CFG)

    def body(q_ref, k_ref, v_ref, o_ref, lse_ref, do_ref, dq_ref, dk_ref, dv_ref):
        i = pl.program_id(1)

        @pl.when(i == 0)
        def _():
            dk_ref[...] = jnp.zeros_like(dk_ref)
            dv_ref[...] = jnp.zeros_like(dv_ref)

        qs = [q_ref[rw, :] for rw in rows]
        dos = [do_ref[rw, :] for rw in rows]
        lses = [lse_ref[rw, :] for rw in rows]
        deltas = [jnp.sum(dos[r].astype(F32) * o_ref[rows[r], :].astype(F32), axis=1, keepdims=True) for r in range(nch)]

        def step(kb, dqs, active, masked):
            off = pl.multiple_of(kb * bk, bk)
            ks, vs = k_ref[pl.ds(off, bk), :], v_ref[pl.ds(off, bk), :]
            A = list(active)
            s = {r: _dot(qs[r], ks, NT) * scale for r in A}
            s = {r: (jnp.where(_tile_mask(bk, False), s[r], -1e30) if r in masked else s[r]) for r in A}
            p = {r: jnp.exp(s[r] - lses[r]) for r in A}
            dp = {r: _dot(dos[r], vs, NT) for r in A}
            ds = {r: (p[r] * (dp[r] - deltas[r]) * scale).astype(BF16) for r in A}
            dv_c = _total(_dot(p[r].astype(BF16), dos[r], TN) for r in A)
            dk_c = _total(_dot(ds[r], qs[r], TN) for r in A)
            dk_ref[pl.ds(off, bk), :] += dk_c
            dv_ref[pl.ds(off, bk), :] += dv_c
            new = list(dqs)
            for r in A:
                new[r] = dqs[r] + _dot(ds[r], ks, NN)
            return tuple(new)

        dqs = _walk(nch, i, step, tuple(jnp.zeros((bk, LANES), F32) for _ in rows), False)
        for r in range(nch):
            dq_ref[rows[r], :] = dqs[r]

    return pl.pallas_call(
        body,
        name=name,
        grid=(heads, S // bq),
        in_specs=[q_spec(q_off), kv_spec(k_off), kv_spec(v_off), q_spec(),
                  pl.BlockSpec((None, bq, 1), lambda h, i: (h, i, 0)), q_spec()],
        out_specs=[q_spec(), kv_spec(), kv_spec()],
        out_shape=[jax.ShapeDtypeStruct((S, heads * LANES), F32)] * 3,
        compiler_params=_params(("parallel", "arbitrary")),
    )(q, k, v, o, lse, do)


def _tri(n, inclusive):
    r = lax.broadcasted_iota(jnp.int32, (n, n), 0)
    c = lax.broadcasted_iota(jnp.int32, (n, n), 1)
    return jnp.where((r >= c) if inclusive else (r > c), 1.0, 0.0).astype(BF16)


def _suffix_sum(x, tri):
    hi = x.astype(BF16)
    lo = (x - hi.astype(F32)).astype(BF16)
    return _dot(hi, tri, NN) + _dot(lo, tri, NN)


def _sb_logs(z):
    lg = jnp.log(1.0 + jnp.exp(-jnp.abs(z)))
    l1m = -(jnp.maximum(z, 0.0) + lg)
    return l1m, l1m + z


SB_SCALE = SB_HEAD_DIM ** -0.5
assert SB_SCALE == 0.125
SB_DEAD = -110.0


def _sb_alive(cs):
    top = cs[0][0]
    for c in cs[1:]:
        top = jnp.maximum(top, c[0])
    return jnp.max(top) > SB_DEAD


def _sb_attn_fwd(qkv, heads, name, q_off, k_off, v_off):
    S = qkv.shape[0]
    nch, bk = SB_FWD_CFG
    bq, rows, q_spec, kv_spec = _att_layout(S, SB_FWD_CFG)

    def body(q_ref, k_ref, v_ref, o_ref):
        i = pl.program_id(1)
        qs = [q_ref[rw, :] * SB_SCALE for rw in rows]
        tri = _tri(bk, False)

        def step(kb, cs, active, masked):
            off = pl.multiple_of(kb * bk, bk)
            ks, vs = k_ref[pl.ds(off, bk), :], v_ref[pl.ds(off, bk), :]
            A = list(active)
            lg = {r: _sb_logs(_dot(qs[r], ks, NT)) for r in A}
            l1m = {r: (jnp.where(_tile_mask(bk, True), lg[r][0], 0.0) if r in masked else lg[r][0]) for r in A}
            suf = {r: _suffix_sum(l1m[r], tri) for r in A}
            ex = {r: lg[r][1] + cs[r][0] + suf[r] for r in A}
            ex = {r: (jnp.where(_tile_mask(bk, True), ex[r], -1e30) if r in masked else ex[r]) for r in A}
            ab = {r: jnp.exp(ex[r]).astype(BF16) for r in A}
            new = list(cs)
            for r in A:
                new[r] = (cs[r][0] + jnp.sum(l1m[r], axis=1, keepdims=True), cs[r][1] + _dot(ab[r], vs, NN))
            return tuple(new)

        init = (jnp.zeros((bk, 1), F32), jnp.zeros((bk, LANES), F32))
        cs = _walk(nch, i, step, tuple(init for _ in rows), True, _sb_alive)
        for r in range(nch):
            o_ref[rows[r], :] = cs[r][1]

    return pl.pallas_call(
        body,
        name=name,
        grid=(heads, S // bq),
        in_specs=[q_spec(q_off), kv_spec(k_off), kv_spec(v_off)],
        out_specs=q_spec(),
        out_shape=jax.ShapeDtypeStruct((S, heads * LANES), F32),
        compiler_params=_params(("parallel", "arbitrary")),
    )(qkv, qkv, qkv)


def _sb_attn_bwd(qkv, o, do, heads, name, q_off, k_off, v_off):
    S = qkv.shape[0]
    nch, bk = SB_BWD_CFG
    bq, rows, q_spec, kv_spec = _att_layout(S, SB_BWD_CFG)

    def body(q_ref, k_ref, v_ref, o_ref, do_ref, dq_ref, dk_ref, dv_ref):
        i = pl.program_id(1)

        @pl.when(i == 0)
        def _():
            dk_ref[...] = jnp.zeros_like(dk_ref)
            dv_ref[...] = jnp.zeros_like(dv_ref)

        tri = _tri(bk, False)
        qs = [q_ref[rw, :] * SB_SCALE for rw in rows]
        dos = [do_ref[rw, :] for rw in rows]
        deltas = [jnp.sum(dos[r].astype(F32) * o_ref[rows[r], :], axis=1, keepdims=True) for r in range(nch)]

        def step(kb, cs, active, masked):
            off = pl.multiple_of(kb * bk, bk)
            ks, vs = k_ref[pl.ds(off, bk), :], v_ref[pl.ds(off, bk), :]
            A = list(active)
            lg = {r: _sb_logs(_dot(qs[r], ks, NT)) for r in A}
            l1m = {r: (jnp.where(_tile_mask(bk, True), lg[r][0], 0.0) if r in masked else lg[r][0]) for r in A}
            suf = {r: _suffix_sum(l1m[r], tri) for r in A}
            ex = {r: lg[r][1] + cs[r][0] + suf[r] for r in A}
            ex = {r: (jnp.where(_tile_mask(bk, True), ex[r], -1e30) if r in masked else ex[r]) for r in A}
            ab = {r: jnp.exp(ex[r]).astype(BF16) for r in A}
            da = {r: _dot(dos[r], vs, NT) for r in A}
            g = {r: ab[r].astype(F32) * da[r] for r in A}
            gs = {r: _suffix_sum(g[r], tri) for r in A}
            beta = {r: jnp.exp(lg[r][1]) for r in A}
            dz = {r: g[r] - beta[r] * (deltas[r] - cs[r][1] - gs[r]) for r in A}
            dz = {r: (jnp.where(_tile_mask(bk, True), dz[r], 0.0) if r in masked else dz[r]) for r in A}
            dzb = {r: dz[r].astype(BF16) for r in A}
            dv_c = _total(_dot(ab[r], dos[r], TN) for r in A)
            dk_c = _total(_dot(dzb[r], qs[r], TN) for r in A)
            dk_ref[pl.ds(off, bk), :] += dk_c
            dv_ref[pl.ds(off, bk), :] += dv_c
            new = list(cs)
            for r in A:
                new[r] = (cs[r][0] + jnp.sum(l1m[r], axis=1, keepdims=True),
                          cs[r][1] + jnp.sum(g[r], axis=1, keepdims=True), cs[r][2] + _dot(dzb[r], ks, NN))
            return tuple(new)

        zcol = jnp.zeros((bk, 1), F32)
        init = (zcol, zcol, jnp.zeros((bk, LANES), F32))
        cs = _walk(nch, i, step, tuple(init for _ in rows), True, _sb_alive)
        for r in range(nch):
            dq_ref[rows[r], :] = cs[r][2] * SB_SCALE

    return pl.pallas_call(
        body,
        name=name,
        grid=(heads, S // bq),
        in_specs=[q_spec(q_off), kv_spec(k_off), kv_spec(v_off), q_spec(), q_spec()],
        out_specs=[q_spec(), kv_spec(), kv_spec()],
        out_shape=[jax.ShapeDtypeStruct((S, heads * LANES), F32)] * 3,
        compiler_params=_params(("parallel", "arbitrary")),
    )(qkv, qkv, qkv, o, do)


SWA_BLK = 128
SWA_GROUP = SWA_HEADS // SWA_KV_HEADS


def _swa_band_mask(n):
    row = lax.broadcasted_iota(jnp.int32, (SWA_BLK, 2 * SWA_BLK), 0)
    col = lax.broadcasted_iota(jnp.int32, (SWA_BLK, 2 * SWA_BLK), 1)
    return (col > row) & (col <= row + SWA_WINDOW) & ((n > 0) | (col >= SWA_BLK))


def _swa_fwd(q, k, v, v_off, sink_b, name):
    S = q.shape[0]
    nb = S // SWA_BLK
    scale = SWA_HEAD_DIM ** -0.5
    gw = SWA_GROUP * LANES

    def body(q_ref, kp_ref, kc_ref, vp_ref, vc_ref, sink_ref, o_ref, lse_ref):
        n = pl.program_id(1)
        kband = jnp.concatenate([kp_ref[...], kc_ref[...]], axis=0)
        vband = jnp.concatenate([vp_ref[...], vc_ref[...]], axis=0)
        valid = _swa_band_mask(n)
        for g in range(SWA_GROUP):
            lanes = slice(g * LANES, (g + 1) * LANES)
            s = jnp.where(valid, _dot(q_ref[:, lanes], kband, NT) * scale, -1e30)
            sk = sink_ref[:, g * LANES:g * LANES + 1]
            m = jnp.maximum(jnp.max(s, axis=1, keepdims=True), sk)
            p = jnp.exp(s - m)
            den = jnp.sum(p, axis=1, keepdims=True) + jnp.exp(sk - m)
            o_ref[:, lanes] = _dot((p / den).astype(BF16), vband, NN).astype(o_ref.dtype)
            lse_ref[g] = m + jnp.log(den)

    return pl.pallas_call(
        body,
        name=name,
        grid=(SWA_KV_HEADS, nb),
        in_specs=[
            pl.BlockSpec((SWA_BLK, gw), lambda h, n: (n, h)),
            pl.BlockSpec((SWA_BLK, LANES), lambda h, n: (jnp.maximum(n - 1, 0), h)),
            pl.BlockSpec((SWA_BLK, LANES), lambda h, n: (n, h)),
            pl.BlockSpec((SWA_BLK, LANES), lambda h, n: (jnp.maximum(n - 1, 0), v_off + h)),
            pl.BlockSpec((SWA_BLK, LANES), lambda h, n: (n, v_off + h)),
            pl.BlockSpec((1, gw), lambda h, n: (0, h)),
        ],
        out_specs=[
            pl.BlockSpec((SWA_BLK, gw), lambda h, n: (n, h)),
            pl.BlockSpec((SWA_GROUP, SWA_BLK, 1), lambda h, n: (h, n, 0)),
        ],
        out_shape=[jax.ShapeDtypeStruct((S, SWA_HEADS * LANES), BF16), jax.ShapeDtypeStruct((SWA_HEADS, S, 1), F32)],
        compiler_params=_params(("parallel", "arbitrary")),
    )(q, k, k, v, v, sink_b)


def _swa_bwd(q, k, v, v_off, sink_b, o, lse, do, name):
    S = q.shape[0]
    nb = S // SWA_BLK
    scale = SWA_HEAD_DIM ** -0.5
    gw = SWA_GROUP * LANES

    def body(q_ref, kp_ref, kc_ref, vp_ref, vc_ref, sink_ref, o_ref, lse_ref, do_ref, dq_ref, dk_ref, dv_ref, dsink_ref):
        n = pl.program_id(1)

        @pl.when(n == 0)
        def _():
            dk_ref[...] = jnp.zeros_like(dk_ref)
            dv_ref[...] = jnp.zeros_like(dv_ref)
            dsink_ref[...] = jnp.zeros_like(dsink_ref)

        kband = jnp.concatenate([kp_ref[...], kc_ref[...]], axis=0)
        vband = jnp.concatenate([vp_ref[...], vc_ref[...]], axis=0)
        valid = _swa_band_mask(n)
        dkb = jnp.zeros((2 * SWA_BLK, LANES), F32)
        dvb = jnp.zeros((2 * SWA_BLK, LANES), F32)
        for g in range(SWA_GROUP):
            lanes = slice(g * LANES, (g + 1) * LANES)
            qg = q_ref[:, lanes]
            dog = do_ref[:, lanes]
            delta = jnp.sum(dog.astype(F32) * o_ref[:, lanes].astype(F32), axis=1, keepdims=True)
            s = jnp.where(valid, _dot(qg, kband, NT) * scale, -1e30)
            lse_g = lse_ref[g]
            p = jnp.exp(s - lse_g)
            p_sink = jnp.exp(sink_ref[:, g * LANES:g * LANES + 1] - lse_g)
            dsink_ref[:, lanes] += jnp.zeros((1, LANES), F32) - jnp.sum(p_sink * delta, axis=0, keepdims=True)
            dvb = dvb + _dot(p.astype(BF16), dog, TN)
            ds = (p * (_dot(dog, vband, NT) - delta) * scale).astype(BF16)
            dq_ref[:, lanes] = _dot(ds, kband, NN)
            dkb = dkb + _dot(ds, qg, TN)

        cur = pl.multiple_of(n * SWA_BLK, SWA_BLK)
        dk_ref[pl.ds(cur, SWA_BLK), :] += dkb[SWA_BLK:]
        dv_ref[pl.ds(cur, SWA_BLK), :] += dvb[SWA_BLK:]

        @pl.when(n > 0)
        def _():
            before = pl.multiple_of((n - 1) * SWA_BLK, SWA_BLK)
            dk_ref[pl.ds(before, SWA_BLK), :] += dkb[:SWA_BLK]
            dv_ref[pl.ds(before, SWA_BLK), :] += dvb[:SWA_BLK]

    return pl.pallas_call(
        body,
        name=name,
        grid=(SWA_KV_HEADS, nb),
        in_specs=[
            pl.BlockSpec((SWA_BLK, gw), lambda h, n: (n, h)),
            pl.BlockSpec((SWA_BLK, LANES), lambda h, n: (jnp.maximum(n - 1, 0), h)),
            pl.BlockSpec((SWA_BLK, LANES), lambda h, n: (n, h)),
            pl.BlockSpec((SWA_BLK, LANES), lambda h, n: (jnp.maximum(n - 1, 0), v_off + h)),
            pl.BlockSpec((SWA_BLK, LANES), lambda h, n: (n, v_off + h)),
            pl.BlockSpec((1, gw), lambda h, n: (0, h)),
            pl.BlockSpec((SWA_BLK, gw), lambda h, n: (n, h)),
            pl.BlockSpec((SWA_GROUP, SWA_BLK, 1), lambda h, n: (h, n, 0)),
            pl.BlockSpec((SWA_BLK, gw), lambda h, n: (n, h)),
        ],
        out_specs=[
            pl.BlockSpec((SWA_BLK, gw), lambda h, n: (n, h)),
            pl.BlockSpec((S, LANES), lambda h, n: (0, h)),
            pl.BlockSpec((S, LANES), lambda h, n: (0, h)),
            pl.BlockSpec((1, gw), lambda h, n: (0, h)),
        ],
        out_shape=[
            jax.ShapeDtypeStruct((S, SWA_HEADS * LANES), F32),
            jax.ShapeDtypeStruct((S, SWA_KV_HEADS * LANES), F32),
            jax.ShapeDtypeStruct((S, SWA_KV_HEADS * LANES), F32),
            jax.ShapeDtypeStruct((1, SWA_HEADS * LANES), F32),
        ],
        compiler_params=_params(("parallel", "arbitrary")),
    )(q, k, k, v, v, sink_b, o, lse, do)


def _pad_cols(w, heads, real):
    k = w.shape[0]
    return jnp.pad(w.reshape(k, heads, real), ((0, 0), (0, 0), (0, LANES - real))).reshape(k, heads * LANES)


def _unpad_cols(g, heads, real):
    k = g.shape[0]
    return g.reshape(k, heads, LANES)[:, :, :real].reshape(k, heads * real)


def _pad_rows(w, heads, real):
    n = w.shape[1]
    return jnp.pad(w.reshape(heads, real, n), ((0, 0), (0, LANES - real), (0, 0))).reshape(heads * LANES, n)


def _unpad_rows(g, heads, real):
    n = g.shape[1]
    return g.reshape(heads, LANES, n)[:, :real, :].reshape(heads * real, n)


def _w_in_internal(w_in):
    c_q, c_kv, k_r, q_swa, k_swa, v_swa, q_sb, k_sb, v_sb, gate = jnp.split(w_in, SPLIT_POINTS, axis=1)
    k_r = jnp.pad(k_r, ((0, 0), (MLA_NOPE, LANES - MLA_NOPE - MLA_ROPE)))
    w1 = jnp.concatenate([c_q, c_kv, k_r, _pad_cols(q_swa, 8, 64), _pad_cols(k_swa, 2, 64)], axis=1)
    w2 = [_pad_cols(v_swa, 2, 64), _pad_cols(q_sb, 8, 64), _pad_cols(k_sb, 8, 64), _pad_cols(v_sb, 8, 64)]
    return w1, w2, gate


def _w_in_reference(g1, g2, g3):
    c_q, c_kv, k_r, q_swa, k_swa = jnp.split(g1, [256, 384, 512, 1536], axis=1)
    v_swa, q_sb, k_sb, v_sb = g2
    return jnp.concatenate([
        c_q, c_kv, k_r[:, MLA_NOPE:MLA_NOPE + MLA_ROPE], _unpad_cols(q_swa, 8, 64), _unpad_cols(k_swa, 2, 64),
        _unpad_cols(v_swa, 2, 64), _unpad_cols(q_sb, 8, 64), _unpad_cols(k_sb, 8, 64), _unpad_cols(v_sb, 8, 64),
        g3], axis=1)


def _w_ukv_internal(w):
    w3 = w.reshape(MLA_KV_LORA, MLA_HEADS, MLA_NOPE + MLA_V)
    pad = lambda t: jnp.pad(t, ((0, 0), (0, 0), (0, LANES - t.shape[2]))).reshape(MLA_KV_LORA, MLA_HEADS * LANES)
    return pad(w3[:, :, :MLA_NOPE]), pad(w3[:, :, MLA_NOPE:])


def _w_ukv_reference(gk, gv):
    gk = gk.reshape(MLA_KV_LORA, MLA_HEADS, LANES)[:, :, :MLA_NOPE]
    gv = gv.reshape(MLA_KV_LORA, MLA_HEADS, LANES)[:, :, :MLA_V]
    return jnp.concatenate([gk, gv], axis=2).reshape(MLA_KV_LORA, MLA_HEADS * (MLA_NOPE + MLA_V))


def _layer_fwd(x, w, tabs):
    mla_tab, swa_tab = tabs
    sv = {"x": x}

    def f_norm(rows, consts):
        return [_rms(rows[0], consts[0])], []

    (h,) = _rowwise(f_norm, [x], [w["g_mix_pre"]], [(D_MODEL, BF16)], [], "norm_mix_pre")
    p1 = _matmul(h, w["w_in1"], "nn", [F32], "proj_lat")
    p2 = _matmul(h, w["w_in2"], "nn", [BF16], "proj_qkv")
    gates = _matmul(h, w["w_in3"], "nn", [BF16], "proj_gate",
                    epilogue=lambda acc, b: (1.0 / (1.0 + jnp.exp(-(acc + b))),), row_extras=[w["b_gate"]])

    def f_prep(rows, consts):
        t = rows[0]
        gq, gkv = consts[0], consts[1]
        mc, mu, md = rows[1], rows[2], rows[3]
        sc, su, sd = rows[4], rows[5], rows[6]
        cq_n = _rms(t[:, 0:256], gq)
        ckv_n = _rms(t[:, 256:384], gkv)
        kr = _rope(t[:, 384:512], mc, mu, md, MLA_ROPE // 2)
        qs = [_rope(t[:, 512 + j * LANES:512 + (j + 1) * LANES], sc, su, sd, SWA_HEAD_DIM // 2) for j in range(8)]
        ks = [_rope(t[:, 1536 + j * LANES:1536 + (j + 1) * LANES], sc, su, sd, SWA_HEAD_DIM // 2) for j in range(2)]
        return [cq_n, ckv_n, kr, jnp.concatenate(qs, axis=1), jnp.concatenate(ks, axis=1)], []

    cq_n, ckv_n, kr, q_swa, k_swa = _rowwise(
        f_prep, [p1, *mla_tab["k"], *swa_tab["f"]], [w["g_q_lat"], w["g_kv_lat"]],
        [(256, BF16), (128, BF16), (LANES, F32), (1024, BF16), (256, BF16)], [], "lat_prep")

    q_lat = _matmul(cq_n, w["w_uq"], "nn", [F32], "mla_q_up")
    k_lat = _matmul(ckv_n, w["w_ukv_k"], "nn", [F32], "mla_k_up")
    v_mla = _matmul(ckv_n, w["w_ukv_v"], "nn", [BF16], "mla_v_up")

    def f_mla_prep(rows, consts):
        ql, kl, krr, mc, mu, md = rows
        qs = [_rope(ql[:, j * LANES:(j + 1) * LANES], mc, mu, md, MLA_ROPE // 2) for j in range(8)]
        ks = [kl[:, j * LANES:(j + 1) * LANES] + krr for j in range(8)]
        return [jnp.concatenate(qs, axis=1), jnp.concatenate(ks, axis=1)], []

    q_mla, k_mla = _rowwise(f_mla_prep, [q_lat, k_lat, kr, *mla_tab["q"]], [], [(1024, BF16), (1024, BF16)], [], "mla_prep")

    o_mla, lse_mla = _softmax_attn_fwd(q_mla, k_mla, v_mla, MLA_HEADS, (MLA_NOPE + MLA_ROPE) ** -0.5, "mla_fwd")
    o_swa, lse_swa = _swa_fwd(q_swa, k_swa, p2, 0, w["sink_b"], "swa_fwd")
    o_sb = _sb_attn_fwd(p2, SB_HEADS, "sb_fwd", 2, 10, 18)

    oa = _matmul(o_mla, w["w_o_mla"], "nn", [F32], "o_proj_mla")
    ob = _matmul(o_swa, w["w_o_swa"], "nn", [F32], "o_proj_swa")
    oc = _matmul(o_sb, w["w_o_sb"], "nn", [F32], "o_proj_sb")

    def f_mix(rows, consts):
        a, b, c, g = rows
        g = g.astype(F32)
        return [g[:, 0:1024] * a + g[:, 1024:2048] * b + g[:, 2048:3072] * c], []

    (mixed,) = _rowwise(f_mix, [oa, ob, oc, gates], [], [(D_MODEL, BF16)], [], "gate_mix")
    y = _matmul(mixed, w["w_out"], "nn", [F32], "out_proj")

    def f_res_norm(rows, consts):
        return [rows[0] + _rms(rows[1], consts[0])], []

    (x1,) = _rowwise(f_res_norm, [x, y], [w["g_mix_post"]], [(D_MODEL, F32)], [], "res_norm_mix")
    (h2,) = _rowwise(f_norm, [x1], [w["g_mlp_pre"]], [(D_MODEL, BF16)], [], "norm_mlp_pre")

    def relu2(acc):
        r = jnp.maximum(acc, 0.0)
        return acc, r * r

    up, u = _matmul(h2, w["w_up"], "nn", [BF16, BF16], "mlp_up", epilogue=relu2)
    zd = _matmul(u, w["w_down"], "nn", [F32], "mlp_down")
    (x2,) = _rowwise(f_res_norm, [x1, zd], [w["g_mlp_post"]], [(D_MODEL, F32)], [], "res_norm_mlp")

    sv.update(h=h, p1=p1, p2=p2, gates=gates, cq_n=cq_n, ckv_n=ckv_n, q_swa=q_swa, k_swa=k_swa, q_mla=q_mla,
              k_mla=k_mla, v_mla=v_mla, o_mla=o_mla, lse_mla=lse_mla, o_swa=o_swa, lse_swa=lse_swa, o_sb=o_sb,
              oa=oa, ob=ob, oc=oc, mixed=mixed, y=y, x1=x1, h2=h2, up=up, u=u, zd=zd)
    return x2, sv


def _layer_bwd(dx2, w, sv, tabs):
    mla_tab, swa_tab = tabs
    gr = {}

    def f_norm_bwd(rows, consts):
        dx, dg = _rms_bwd(rows[0], consts[0], rows[1])
        return [dx], [dg]

    def f_norm_bwd_res(rows, consts):
        dx, dg = _rms_bwd(rows[0], consts[0], rows[1])
        return [rows[2] + dx], [dg]

    dzd, gr["g_mlp_post"] = _rowwise(f_norm_bwd, [sv["zd"], dx2], [w["g_mlp_post"]], [(D_MODEL, BF16)], [D_MODEL], "b_norm_mlp_post")
    gr["w_down"] = _matmul(sv["u"], dzd, "tn", [F32], "b_w_down")
    dup = _matmul(dzd, w["w_down"], "nt", [BF16], "b_mlp_down",
                  epilogue=lambda acc, up: (acc * 2.0 * jnp.maximum(up.astype(F32), 0.0),), extras=[sv["up"]])
    gr["w_up"] = _matmul(sv["h2"], dup, "tn", [F32], "b_w_up")
    dh2 = _matmul(dup, w["w_up"], "nt", [F32], "b_mlp_up")
    dx1, gr["g_mlp_pre"] = _rowwise(f_norm_bwd_res, [sv["x1"], dh2, dx2], [w["g_mlp_pre"]], [(D_MODEL, F32)], [D_MODEL], "b_norm_mlp_pre")

    dy, gr["g_mix_post"] = _rowwise(f_norm_bwd, [sv["y"], dx1], [w["g_mix_post"]], [(D_MODEL, BF16)], [D_MODEL], "b_norm_mix_post")
    gr["w_out"] = _matmul(sv["mixed"], dy, "tn", [F32], "b_w_out")
    dmixed = _matmul(dy, w["w_out"], "nt", [F32], "b_out_proj")

    def f_mix_bwd(rows, consts):
        dm, a, b, c, g = rows
        g = g.astype(F32)
        outs, dls = [], []
        for j, o in enumerate((a, b, c)):
            gj = g[:, j * D_MODEL:(j + 1) * D_MODEL]
            outs.append(dm * gj)
            dls.append(dm * o * gj * (1.0 - gj))
        dl = jnp.concatenate(dls, axis=1)
        return outs + [dl], [dl]

    doa, dob, doc, dlogit, gr["b_gate"] = _rowwise(
        f_mix_bwd, [dmixed, sv["oa"], sv["ob"], sv["oc"], sv["gates"]], [],
        [(D_MODEL, BF16)] * 3 + [(P3_W, BF16)], [P3_W], "b_gate_mix")

    gr["w_o_mla"] = _matmul(sv["o_mla"], doa, "tn", [F32], "b_w_o_mla")
    gr["w_o_swa"] = _matmul(sv["o_swa"], dob, "tn", [F32], "b_w_o_swa")
    gr["w_o_sb"] = _matmul(sv["o_sb"], doc, "tn", [F32], "b_w_o_sb")
    do_mla = _matmul(doa, w["w_o_mla"], "nt", [BF16], "b_o_proj_mla")
    do_swa = _matmul(dob, w["w_o_swa"], "nt", [BF16], "b_o_proj_swa")
    do_sb = _matmul(doc, w["w_o_sb"], "nt", [BF16], "b_o_proj_sb")

    dq_sb, dk_sb, dv_sb = _sb_attn_bwd(sv["p2"], sv["o_sb"], do_sb, SB_HEADS, "sb_bwd", 2, 10, 18)
    dq_swa, dk_swa, dv_swa, dsink = _swa_bwd(sv["q_swa"], sv["k_swa"], sv["p2"], 0, w["sink_b"], sv["o_swa"],
                                             sv["lse_swa"], do_swa, "swa_bwd")
    gr["swa_sinks"] = dsink.reshape(SWA_HEADS, LANES)[:, 0]
    dq_mla, dk_mla, dv_mla = _softmax_attn_bwd(sv["q_mla"], sv["k_mla"], sv["v_mla"], sv["o_mla"], sv["lse_mla"], do_mla,
                                               MLA_HEADS, (MLA_NOPE + MLA_ROPE) ** -0.5, "mla_bwd")

    def f_mla_post(rows, consts):
        dq, dk, qc, qu, qd, kc, ku, kd = rows
        dqs = [_rope(dq[:, j * LANES:(j + 1) * LANES], qc, qu, qd, MLA_ROPE // 2) for j in range(8)]
        dkr = dk[:, 0:LANES]
        for j in range(1, 8):
            dkr = dkr + dk[:, j * LANES:(j + 1) * LANES]
        return [jnp.concatenate(dqs, axis=1), _rope(dkr, kc, ku, kd, MLA_ROPE // 2)], []

    dq_lat, dkr = _rowwise(f_mla_post, [dq_mla, dk_mla, *mla_tab["q_inv"], *mla_tab["k_inv"]], [],
                           [(1024, BF16), (LANES, F32)], [], "b_mla_post")
    gr["w_uq"] = _matmul(sv["cq_n"], dq_lat, "tn", [F32], "b_w_uq")
    gr["w_ukv_k"] = _matmul(sv["ckv_n"], dk_mla, "tn", [F32], "b_w_ukv_k")
    gr["w_ukv_v"] = _matmul(sv["ckv_n"], dv_mla, "tn", [F32], "b_w_ukv_v")
    dcq_n = _matmul(dq_lat, w["w_uq"], "nt", [F32], "b_mla_q_up")
    dckv_a = _matmul(dk_mla, w["w_ukv_k"], "nt", [F32], "b_mla_k_up")
    dckv_b = _matmul(dv_mla, w["w_ukv_v"], "nt", [F32], "b_mla_v_up")

    def f_prep_bwd(rows, consts):
        t, dcq, dca, dcb, dkr_, dqs, dks, sc, su, sd = rows
        gq, gkv = consts
        dc_q, dgq = _rms_bwd(t[:, 0:256], gq, dcq)
        dc_kv, dgkv = _rms_bwd(t[:, 256:384], gkv, dca + dcb)
        q_parts = [_rope(dqs[:, j * LANES:(j + 1) * LANES], sc, su, sd, SWA_HEAD_DIM // 2) for j in range(8)]
        k_parts = [_rope(dks[:, j * LANES:(j + 1) * LANES], sc, su, sd, SWA_HEAD_DIM // 2) for j in range(2)]
        return [jnp.concatenate([dc_q, dc_kv, dkr_] + q_parts + k_parts, axis=1)], [dgq, dgkv]

    dp1, gr["g_q_lat"], gr["g_kv_lat"] = _rowwise(
        f_prep_bwd, [sv["p1"], dcq_n, dckv_a, dckv_b, dkr, dq_swa, dk_swa, *swa_tab["inv"]], [w["g_q_lat"], w["g_kv_lat"]],
        [(P1_W, BF16)], [256, 128], "b_lat_prep")

    gr["w_in1"] = _matmul(sv["h"], dp1, "tn", [F32], "b_w_in_lat")
    dh = _matmul(dp1, w["w_in1"], "nt", [F32], "b_proj_lat")
    gr["w_in2"] = []
    add_prev = lambda acc, prev: (acc + prev,)
    for piece, wp, tag in zip((dv_swa, dq_sb, dk_sb, dv_sb), w["w_in2_parts"], ("vswa", "qsb", "ksb", "vsb")):
        gr["w_in2"].append(_matmul(sv["h"], piece, "tn", [F32], "b_w_in_" + tag))
        dh = _matmul(piece, wp, "nt", [F32], "b_proj_" + tag, epilogue=add_prev, extras=[dh])
    gr["w_in3"] = _matmul(sv["h"], dlogit, "tn", [F32], "b_w_in_gate")
    dh = _matmul(dlogit, w["w_in3"], "nt", [F32], "b_proj_gate", epilogue=add_prev, extras=[dh])
    dx, gr["g_mix_pre"] = _rowwise(f_norm_bwd_res, [sv["x"], dh, dx1], [w["g_mix_pre"]], [(D_MODEL, F32)], [D_MODEL], "b_norm_mix_pre")
    return dx, gr


def _local_step(x, positions, loss_target, full):
    mc, mu, md = _rope_tables(positions, MLA_NOPE, MLA_ROPE, True)
    kc, ku, kd = _rope_tables(positions, MLA_NOPE, MLA_ROPE, False)
    sc, su, sd = _rope_tables(positions, 0, SWA_HEAD_DIM, False)
    mla_tab = {"q": (mc, mu, md), "k": (kc, ku, kd), "q_inv": (mc, -mu, -md), "k_inv": (kc, -ku, -kd)}
    swa_tab = {"f": (sc, su, sd), "inv": (sc, -su, -sd)}
    tabs = (mla_tab, swa_tab)

    layers = []
    for l in range(DEPTH):
        w1, w2, w3 = _w_in_internal(full["w_in"][l].astype(BF16))
        uk, uv = _w_ukv_internal(full["w_ukv"][l].astype(BF16))
        layers.append({
            "w_in1": w1, "w_in2": jnp.concatenate(w2, axis=1), "w_in2_parts": w2, "w_in3": w3,
            "w_uq": _pad_cols(full["w_uq"][l].astype(BF16), MLA_HEADS, MLA_NOPE + MLA_ROPE),
            "w_ukv_k": uk, "w_ukv_v": uv,
            "w_o_mla": _pad_rows(full["w_o_mla"][l].astype(BF16), 8, 64),
            "w_o_swa": _pad_rows(full["w_o_swa"][l].astype(BF16), 8, 64),
            "w_o_sb": _pad_rows(full["w_o_sb"][l].astype(BF16), 8, 64),
            "w_out": full["w_out"][l].astype(BF16), "w_up": full["w_up"][l].astype(BF16),
            "w_down": full["w_down"][l].astype(BF16),
            "g_mix_pre": full["g_mix_pre"][l][None], "b_gate": full["b_gate"][l][None],
            "g_q_lat": full["g_q_lat"][l][None], "g_kv_lat": full["g_kv_lat"][l][None],
            "g_mix_post": full["g_mix_post"][l][None], "g_mlp_pre": full["g_mlp_pre"][l][None],
            "g_mlp_post": full["g_mlp_post"][l][None],
            "sink_b": jnp.repeat(full["swa_sinks"][l], LANES)[None],
        })

    saved = []
    h = x
    for l in range(DEPTH):
        h, sv = _layer_fwd(h, layers[l], tabs)
        saved.append(sv)

    def f_loss(rows, consts):
        err = rows[0] - rows[1]
        return [err * (1.0 / D_MODEL)], [jnp.sum(err * err, axis=1, keepdims=True)]

    dy, sq = _rowwise(f_loss, [h, loss_target], [], [(D_MODEL, F32)], [1], "loss_head")
    loss_part = sq * (0.5 / D_MODEL)

    grads = [None] * DEPTH
    d = dy
    for l in reversed(range(DEPTH)):
        d, gr = _layer_bwd(d, layers[l], saved[l], tabs)
        grads[l] = {
            "g_mix_pre": gr["g_mix_pre"][0], "w_in": _w_in_reference(gr["w_in1"], gr["w_in2"], gr["w_in3"]),
            "b_gate": gr["b_gate"][0], "g_q_lat": gr["g_q_lat"][0], "g_kv_lat": gr["g_kv_lat"][0],
            "w_uq": _unpad_cols(gr["w_uq"], MLA_HEADS, MLA_NOPE + MLA_ROPE),
            "w_ukv": _w_ukv_reference(gr["w_ukv_k"], gr["w_ukv_v"]), "swa_sinks": gr["swa_sinks"],
            "w_o_mla": _unpad_rows(gr["w_o_mla"], 8, 64), "w_o_swa": _unpad_rows(gr["w_o_swa"], 8, 64),
            "w_o_sb": _unpad_rows(gr["w_o_sb"], 8, 64), "w_out": gr["w_out"], "g_mix_post": gr["g_mix_post"][0],
            "g_mlp_pre": gr["g_mlp_pre"][0], "w_up": gr["w_up"], "w_down": gr["w_down"], "g_mlp_post": gr["g_mlp_post"][0],
        }
    stacked = {n: jnp.stack([grads[l][n] for l in range(DEPTH)]) for n in WEIGHTS}
    return loss_part, d, stacked


def _rows_of(a):
    return a.reshape(-1, LANES)


def _small_rows(d):
    parts = []
    for n in SMALL:
        a = d[n]
        if a.shape[1] < LANES:
            a = jnp.pad(a, ((0, 0), (0, LANES - a.shape[1])))
        parts.append(_rows_of(a))
    return parts


def _pack(shards, small, dtype):
    parts = [_rows_of(shards[n]) for n in SHARDED]
    if small is not None:
        parts += _small_rows(small)
    slab = jnp.concatenate(parts, axis=0).astype(dtype)
    pad = (-slab.shape[0]) % SLAB_ROW_ALIGN
    return jnp.pad(slab, ((0, pad), (0, 0)))


def _unpack(slab, shard_shapes, small_shapes):
    out, r = {}, 0
    for n in SHARDED:
        rows = int(np.prod(shard_shapes[n])) // LANES
        out[n] = slab[r:r + rows].reshape(shard_shapes[n])
        r += rows
    if small_shapes is not None:
        for n in SMALL:
            depth, width = small_shapes[n]
            rows = depth * max(width, LANES) // LANES
            out[n] = slab[r:r + rows].reshape(depth, max(width, LANES))[:, :width]
            r += rows
    return out


def _chip_exchange(src, name):
    rows = src.shape[-2]

    def body(src_ref, out_ref, send_sems, recv_sems, local_sem):
        x, y, c = lax.axis_index("x"), lax.axis_index("y"), lax.axis_index("c")
        me = 2 * x + y
        chips = [(1 - x, y), (x, 1 - y), (1 - x, 1 - y)]
        mine = pltpu.make_async_copy(src_ref.at[me], out_ref.at[me], local_sem)
        mine.start()
        sends = []
        for k, (cx, cy) in enumerate(chips):
            cp = pltpu.make_async_remote_copy(
                src_ref=src_ref.at[2 * cx + cy], dst_ref=out_ref.at[me], send_sem=send_sems.at[k],
                recv_sem=recv_sems.at[k], device_id=(cx, cy, c), device_id_type=pl.DeviceIdType.MESH)
            cp.start()
            sends.append(cp)
        for k, (cx, cy) in enumerate(chips):
            pltpu.make_async_remote_copy(
                src_ref=src_ref.at[me], dst_ref=out_ref.at[2 * cx + cy], send_sem=send_sems.at[k],
                recv_sem=recv_sems.at[k], device_id=(cx, cy, c), device_id_type=pl.DeviceIdType.MESH).wait_recv()
        for cp in sends:
            cp.wait_send()
        mine.wait()

    return pl.pallas_call(
        body,
        name=name,
        in_specs=[pl.BlockSpec(memory_space=pl.ANY)],
        out_specs=pl.BlockSpec(memory_space=pl.ANY),
        out_shape=jax.ShapeDtypeStruct((N_CHIPS, rows, LANES), src.dtype),
        scratch_shapes=[pltpu.SemaphoreType.DMA((3,)), pltpu.SemaphoreType.DMA((3,)), pltpu.SemaphoreType.DMA],
    )(src)


def _half_rows(c, half):
    return pl.ds(pl.multiple_of(c * half, SLAB_ROW_ALIGN // 2), half)


def _gather_weights(src, name):
    rows = src.shape[0]
    half = rows // 2

    def body(src_ref, out_ref, send_sems, recv_sems, local_sem):
        x, y, c = lax.axis_index("x"), lax.axis_index("y"), lax.axis_index("c")
        me = 2 * x + y
        chips = [(1 - x, y), (x, 1 - y), (1 - x, 1 - y)]

        def copy(k, src_view, slab, part, to):
            return pltpu.make_async_remote_copy(
                src_ref=src_view, dst_ref=out_ref.at[slab, _half_rows(part, half), :], send_sem=send_sems.at[k],
                recv_sem=recv_sems.at[k], device_id=to, device_id_type=pl.DeviceIdType.MESH)

        mine = pltpu.make_async_copy(src_ref, out_ref.at[me], local_sem)
        mine.start()
        sends = [copy(k, src_ref.at[_half_rows(c, half), :], me, c, (cx, cy, c)) for k, (cx, cy) in enumerate(chips)]
        for cp in sends:
            cp.start()
        for k, (cx, cy) in enumerate(chips):
            j = 2 * cx + cy
            landed = out_ref.at[j, _half_rows(c, half), :]
            copy(k, landed, j, c, (cx, cy, c)).wait_recv()
            fwd = copy(3 + k, landed, j, c, (x, y, 1 - c))
            fwd.start()
            sends.append(fwd)
        for k, (cx, cy) in enumerate(chips):
            j = 2 * cx + cy
            copy(3 + k, out_ref.at[j, _half_rows(1 - c, half), :], j, 1 - c, (x, y, 1 - c)).wait_recv()
        for cp in sends:
            cp.wait_send()
        mine.wait()

    return pl.pallas_call(
        body,
        name=name,
        in_specs=[pl.BlockSpec(memory_space=pl.ANY)],
        out_specs=pl.BlockSpec(memory_space=pl.ANY),
        out_shape=jax.ShapeDtypeStruct((N_CHIPS, rows, LANES), src.dtype),
        scratch_shapes=[pltpu.SemaphoreType.DMA((6,)), pltpu.SemaphoreType.DMA((6,)), pltpu.SemaphoreType.DMA],
    )(src)


def _sibling_halves(src, name):
    n, rows, _ = src.shape
    half = rows // 2

    def body(src_ref, out_ref, send_sem, recv_sem):
        x, y, c = lax.axis_index("x"), lax.axis_index("y"), lax.axis_index("c")
        cp = pltpu.make_async_remote_copy(
            src_ref=src_ref.at[:, _half_rows(1 - c, half), :], dst_ref=out_ref, send_sem=send_sem, recv_sem=recv_sem,
            device_id=(x, y, 1 - c), device_id_type=pl.DeviceIdType.MESH)
        cp.start()
        cp.wait()

    return pl.pallas_call(
        body,
        name=name,
        in_specs=[pl.BlockSpec(memory_space=pl.ANY)],
        out_specs=pl.BlockSpec(memory_space=pl.ANY),
        out_shape=jax.ShapeDtypeStruct((n, half, LANES), src.dtype),
        scratch_shapes=[pltpu.SemaphoreType.DMA, pltpu.SemaphoreType.DMA],
    )(src)


def _sibling_join(src, name):
    half = src.shape[0]

    def body(src_ref, out_ref, send_sem, recv_sem, local_sem):
        x, y, c = lax.axis_index("x"), lax.axis_index("y"), lax.axis_index("c")
        mine = pltpu.make_async_copy(src_ref, out_ref.at[_half_rows(c, half), :], local_sem)
        mine.start()
        cp = pltpu.make_async_remote_copy(
            src_ref=src_ref, dst_ref=out_ref.at[_half_rows(c, half), :], send_sem=send_sem, recv_sem=recv_sem,
            device_id=(x, y, 1 - c), device_id_type=pl.DeviceIdType.MESH)
        cp.start()
        pltpu.make_async_remote_copy(
            src_ref=src_ref, dst_ref=out_ref.at[_half_rows(1 - c, half), :], send_sem=send_sem, recv_sem=recv_sem,
            device_id=(x, y, 1 - c), device_id_type=pl.DeviceIdType.MESH).wait_recv()
        cp.wait_send()
        mine.wait()

    return pl.pallas_call(
        body,
        name=name,
        in_specs=[pl.BlockSpec(memory_space=pl.ANY)],
        out_specs=pl.BlockSpec(memory_space=pl.ANY),
        out_shape=jax.ShapeDtypeStruct((2 * half, LANES), src.dtype),
        scratch_shapes=[pltpu.SemaphoreType.DMA, pltpu.SemaphoreType.DMA, pltpu.SemaphoreType.DMA],
    )(src)


SUM_ROWS = 1024


def _pair_sum(mine, theirs, c, name):
    n, half, _ = theirs.shape
    blocks = half // SUM_ROWS

    def body(c_ref, a_ref, b_ref, o_ref):
        o_ref[...] = (a_ref[...] + b_ref[...]).astype(o_ref.dtype)

    return pl.pallas_call(
        body,
        name=name,
        grid_spec=pltpu.PrefetchScalarGridSpec(
            num_scalar_prefetch=1,
            grid=(blocks,),
            in_specs=[pl.BlockSpec((n, SUM_ROWS, LANES), lambda i, c_ref: (0, c_ref[0] * blocks + i, 0)),
                      pl.BlockSpec((n, SUM_ROWS, LANES), lambda i, c_ref: (0, i, 0))],
            out_specs=pl.BlockSpec((n, SUM_ROWS, LANES), lambda i, c_ref: (0, i, 0)),
        ),
        out_shape=jax.ShapeDtypeStruct((n, half, LANES), BF16),
        compiler_params=_params(("arbitrary",)),
    )(jnp.reshape(c, (1,)).astype(jnp.int32), mine, theirs)


def _sum_chips(buf, name):
    rows = buf.shape[1]

    def body(b_ref, o_ref):
        t = [b_ref[j].astype(F32) for j in range(N_CHIPS)]
        o_ref[...] = ((t[0] + t[1]) + t[2]) + t[3]

    return pl.pallas_call(
        body,
        name=name,
        grid=(rows // SUM_ROWS,),
        in_specs=[pl.BlockSpec((N_CHIPS, SUM_ROWS, LANES), lambda i: (0, i, 0))],
        out_specs=pl.BlockSpec((SUM_ROWS, LANES), lambda i: (i, 0)),
        out_shape=jax.ShapeDtypeStruct((rows, LANES), F32),
        compiler_params=_params(("arbitrary",)),
    )(buf)


def _adamw(w, m, v, g, name):
    shape = w.shape
    flat = lambda a: a.reshape(-1, shape[-1])

    def fn(rows, consts):
        w_, m_, v_, g_ = rows
        m_new = ADAM_B1 * m_ + (1.0 - ADAM_B1) * g_
        v_new = ADAM_B2 * v_ + (1.0 - ADAM_B2) * (g_ * g_)
        m_hat = m_new / (1.0 - ADAM_B1 ** ADAM_STEP)
        v_hat = v_new / (1.0 - ADAM_B2 ** ADAM_STEP)
        delta = -ADAM_LR * (m_hat / (jnp.sqrt(v_hat) + ADAM_EPS) + ADAM_WD * w_)
        return [delta, m_new, v_new], []

    outs = _rowwise(fn, [flat(w), flat(m), flat(v), flat(g)], [], [(shape[-1], F32)] * 3, [], name, bm=256)
    return [o.reshape(shape) for o in outs]


def kernel(x, positions, g_mix_pre, w_in, b_gate, g_q_lat, g_kv_lat, w_uq, w_ukv, swa_sinks, w_o_mla, w_o_swa, w_o_sb, w_out, g_mix_post, g_mlp_pre, w_up, w_down, g_mlp_post, loss_target, m_g_mix_pre, m_w_in, m_b_gate, m_g_q_lat, m_g_kv_lat, m_w_uq, m_w_ukv, m_swa_sinks, m_w_o_mla, m_w_o_swa, m_w_o_sb, m_w_out, m_g_mix_post, m_g_mlp_pre, m_w_up, m_w_down, m_g_mlp_post, v_g_mix_pre, v_w_in, v_b_gate, v_g_q_lat, v_g_kv_lat, v_w_uq, v_w_ukv, v_swa_sinks, v_w_o_mla, v_w_o_swa, v_w_o_sb, v_w_out, v_g_mix_post, v_g_mlp_pre, v_w_up, v_w_down, v_g_mlp_post):
    given = dict(locals())
    wts = {n: given[n] for n in WEIGHTS}
    mom_m = {n: given["m_" + n] for n in WEIGHTS}
    mom_v = {n: given["v_" + n] for n in WEIGHTS}
    shard_shapes = {n: wts[n].shape for n in SHARDED}
    small_shapes = {n: wts[n].shape for n in SMALL}

    gathered = _gather_weights(_pack(wts, None, BF16), "gather_weights")
    full = {n: wts[n] for n in SMALL}
    per_chip = [_unpack(gathered[j], shard_shapes, None) for j in range(N_CHIPS)]
    for n in SHARDED:
        full[n] = jnp.concatenate([per_chip[j][n] for j in range(N_CHIPS)], axis=SHARD_AXIS[n])

    loss_part, grad_x, grads = _local_step(x[0], positions[0], loss_target[0], full)
    loss = lax.psum(loss_part[0, 0], ("x", "y", "c"))

    small_g = {n: grads[n] for n in SMALL}
    slabs = []
    for j in range(N_CHIPS):
        shard = {n: jnp.split(grads[n], N_CHIPS, axis=SHARD_AXIS[n])[j] for n in SHARDED}
        slabs.append(_pack(shard, small_g, F32))
    per_chip_g = jnp.stack(slabs)
    theirs = _sibling_halves(per_chip_g, "pair_grads")
    pair = _pair_sum(per_chip_g, theirs, lax.axis_index("c"), "sum_pair")
    landed = _chip_exchange(pair, "scatter_grads")
    g_slab = _sibling_join(_sum_chips(landed, "sum_chips"), "join_grads")

    g = _unpack(g_slab, shard_shapes, small_shapes)
    stepped = {n: _adamw(wts[n], mom_m[n], mom_v[n], g[n], "adamw_" + n) for n in WEIGHTS}
    outs = [loss, grad_x[None]] + [g[n] for n in WEIGHTS]
    for part in range(3):
        outs += [stepped[n][part] for n in WEIGHTS]
    return tuple(outs)
```

```python
import numpy as np
import jax
import jax.numpy as jnp
from jax import lax
from jax.experimental import pallas as pl
from jax.experimental.pallas import tpu as pltpu

F32 = jnp.float32
BF16 = jnp.bfloat16

D_MODEL = 1024
DEPTH = 4
MLA_HEADS, MLA_Q_LORA, MLA_KV_LORA, MLA_NOPE, MLA_ROPE, MLA_V = 8, 256, 128, 64, 32, 64
SWA_HEADS, SWA_KV_HEADS, SWA_HEAD_DIM, SWA_WINDOW = 8, 2, 64, 128
SB_HEADS, SB_HEAD_DIM = 8, 64
D_FF = 4 * D_MODEL
ROPE_THETA = 10000.0
EPS = 1e-6
SPLIT_SIZES = (256, 128, 32, 512, 128, 128, 512, 512, 512, 3 * D_MODEL)
SPLIT_POINTS = [int(v) for v in np.cumsum(SPLIT_SIZES)[:-1]]

ADAM_LR, ADAM_B1, ADAM_B2, ADAM_EPS, ADAM_WD, ADAM_STEP = 0.001, 0.9, 0.999, 1e-08, 0.01, 10

LANES = 128
V7X_VMEM_BYTES = 64 * 1024 * 1024
VMEM_LIMIT = V7X_VMEM_BYTES - 8 * 1024 * 1024
MATMUL_VMEM_BUDGET = 36 * 1024 * 1024
N_CHIPS = 4
SLAB_ROW_ALIGN = 512

P1_W = 256 + 128 + 128 + 1024 + 256
P2_W = 256 + 1024 + 1024 + 1024
P3_W = 3 * D_MODEL

SHARDED = ("w_in", "w_uq", "w_ukv", "w_o_mla", "w_o_swa", "w_o_sb", "w_out", "w_up", "w_down")
SHARD_AXIS = {"w_in": 2, "w_uq": 2, "w_ukv": 2, "w_o_mla": 2, "w_o_swa": 2, "w_o_sb": 2, "w_out": 1, "w_up": 2, "w_down": 1}
SMALL = ("g_mix_pre", "b_gate", "g_q_lat", "g_kv_lat", "swa_sinks", "g_mix_post", "g_mlp_pre", "g_mlp_post")
WEIGHTS = ("g_mix_pre", "w_in", "b_gate", "g_q_lat", "g_kv_lat", "w_uq", "w_ukv", "swa_sinks", "w_o_mla", "w_o_swa",
           "w_o_sb", "w_out", "g_mix_post", "g_mlp_pre", "w_up", "w_down", "g_mlp_post")

NN = (((1,), (0,)), ((), ()))
NT = (((1,), (1,)), ((), ()))
TN = (((0,), (0,)), ((), ()))


def _dot(a, b, dims):
    return lax.dot_general(a, b, dims, preferred_element_type=F32)


def _params(sem):
    return pltpu.CompilerParams(dimension_semantics=sem, vmem_limit_bytes=VMEM_LIMIT)


def _largest_tile(n, cap):
    if n <= cap:
        return n
    best = LANES
    for t in range(LANES, cap + 1, LANES):
        if n % t == 0:
            best = t
    return best


def _matmul_tiles(M, N, K, a_bytes, b_bytes, out_bytes, extra_bytes):
    tn = _largest_tile(N, 1792)
    tm = _largest_tile(M, 1024 if tn <= 1024 else 512)
    tk = _largest_tile(K, 2048)

    def need(tm_, tk_):
        acc = 4 * tm_ * tn if tk_ < K else 0
        return 2 * (tm_ * tk_ * a_bytes + tk_ * tn * b_bytes + tm_ * tn * (out_bytes + extra_bytes)) + acc

    while need(tm, tk) > MATMUL_VMEM_BUDGET:
        if tk >= tm and tk % 256 == 0:
            tk //= 2
        elif tm % 256 == 0:
            tm //= 2
        else:
            break
    return tm, tn, tk


def _matmul(a, b, mode, out_dtypes, name, epilogue=None, extras=(), row_extras=()):
    if mode == "nn":
        (M, K), (K2, N) = a.shape, b.shape
    elif mode == "nt":
        (M, K), (N, K2) = a.shape, b.shape
    else:
        (K, M), (K2, N) = a.shape, b.shape
    assert K == K2, (name, a.shape, b.shape)
    tm, tn, tk = _matmul_tiles(
        M, N, K, a.dtype.itemsize, b.dtype.itemsize, sum(jnp.dtype(d).itemsize for d in out_dtypes),
        sum(e.dtype.itemsize for e in extras))
    assert M % tm == 0 and N % tn == 0 and K % tk == 0, (name, M, N, K, tm, tn, tk)
    nk = K // tk
    if mode == "tn":
        a_spec = pl.BlockSpec((tk, tm), lambda i, j, k: (k, i))
    else:
        a_spec = pl.BlockSpec((tm, tk), lambda i, j, k: (i, k))
    if mode == "nt":
        b_spec = pl.BlockSpec((tn, tk), lambda i, j, k: (j, k))
    else:
        b_spec = pl.BlockSpec((tk, tn), lambda i, j, k: (k, j))
    dims = {"nn": NN, "nt": NT, "tn": TN}[mode]
    n_ex, n_rex, n_out = len(extras), len(row_extras), len(out_dtypes)

    def body(*refs):
        a_ref, b_ref = refs[:2]
        ex = refs[2:2 + n_ex]
        rex = refs[2 + n_ex:2 + n_ex + n_rex]
        outs = refs[2 + n_ex + n_rex:2 + n_ex + n_rex + n_out]

        def finish(total):
            res = (total,) if epilogue is None else epilogue(total, *[e[...] for e in ex], *[e[...] for e in rex])
            for o, r in zip(outs, res):
                o[...] = r.astype(o.dtype)

        part = _dot(a_ref[...].astype(BF16), b_ref[...].astype(BF16), dims)
        if nk == 1:
            finish(part)
            return
        acc = refs[-1]
        k = pl.program_id(2)

        @pl.when(k == 0)
        def _():
            acc[...] = part

        @pl.when(k > 0)
        def _():
            acc[...] += part

        @pl.when(k == nk - 1)
        def _():
            finish(acc[...])

    in_specs = [a_spec, b_spec]
    in_specs += [pl.BlockSpec((tm, tn), lambda i, j, k: (i, j)) for _ in extras]
    in_specs += [pl.BlockSpec((1, tn), lambda i, j, k: (0, j)) for _ in row_extras]
    out = pl.pallas_call(
        body,
        name=name,
        grid=(M // tm, N // tn, nk),
        in_specs=in_specs,
        out_specs=[pl.BlockSpec((tm, tn), lambda i, j, k: (i, j)) for _ in out_dtypes],
        out_shape=[jax.ShapeDtypeStruct((M, N), dt) for dt in out_dtypes],
        scratch_shapes=[pltpu.VMEM((tm, tn), F32)] if nk > 1 else [],
        compiler_params=_params(("parallel", "parallel", "arbitrary")),
    )(a, b, *extras, *row_extras)
    return out[0] if n_out == 1 else out


def _rowwise(fn, rows, consts, out_defs, sum_widths, name, bm=256):
    R = rows[0].shape[0]
    bm = min(bm, R)
    assert R % bm == 0, (name, R, bm)
    n_r, n_c, n_o = len(rows), len(consts), len(out_defs)
    n_s = len(sum_widths)

    def body(*refs):
        r_in = refs[:n_r]
        c_in = refs[n_r:n_r + n_c]
        o_refs = refs[n_r + n_c:n_r + n_c + n_o]
        s_refs = refs[n_r + n_c + n_o:]
        outs, sums = fn([r[...] for r in r_in], [c[...] for c in c_in])
        for o, val in zip(o_refs, outs):
            o[...] = val.astype(o.dtype)
        if n_s:
            @pl.when(pl.program_id(0) == 0)
            def _():
                for s in s_refs:
                    s[...] = jnp.zeros_like(s)

            for s, val in zip(s_refs, sums):
                s[...] += jnp.sum(val, axis=0, keepdims=True)

    in_specs = [pl.BlockSpec((bm, r.shape[1]), lambda i: (i, 0)) for r in rows]
    in_specs += [pl.BlockSpec(c.shape, lambda i: (0, 0)) for c in consts]
    out_specs = [pl.BlockSpec((bm, w), lambda i: (i, 0)) for w, _ in out_defs]
    out_specs += [pl.BlockSpec((1, w), lambda i: (0, 0)) for w in sum_widths]
    out_shape = [jax.ShapeDtypeStruct((R, w), dt) for w, dt in out_defs]
    out_shape += [jax.ShapeDtypeStruct((1, w), F32) for w in sum_widths]
    return pl.pallas_call(
        body,
        name=name,
        grid=(R // bm,),
        in_specs=in_specs,
        out_specs=out_specs,
        out_shape=out_shape,
        compiler_params=_params(("arbitrary",)),
    )(*rows, *consts)


def _rms(x, g):
    r = lax.rsqrt(jnp.mean(x * x, axis=-1, keepdims=True) + EPS)
    return x * r * g


def _rms_bwd(x, g, dy):
    r = lax.rsqrt(jnp.mean(x * x, axis=-1, keepdims=True) + EPS)
    n = x * r
    dn = dy * g
    dx = r * (dn - n * jnp.mean(dn * n, axis=-1, keepdims=True))
    return dx, dy * n


def _rope(x, c, s_up, s_dn, half):
    return x * c + pltpu.roll(x, half, 1) * s_up + pltpu.roll(x, LANES - half, 1) * s_dn


def _rope_tables(positions, lo, d, nope_pass):
    S = positions.shape[0]
    half = d // 2
    inv = 1.0 / (ROPE_THETA ** (jnp.arange(0, d, 2, dtype=F32) / d))
    ang = positions.astype(F32)[:, None] * inv
    cos, sin = jnp.cos(ang), jnp.sin(ang)
    z = lambda n: jnp.zeros((S, n), F32)
    head = jnp.ones((S, lo), F32) if nope_pass else z(lo)
    tail = LANES - lo - d
    c = jnp.concatenate([head, cos, cos, z(tail)], axis=1)
    s_up = jnp.concatenate([z(lo), z(half), sin, z(tail)], axis=1)
    s_dn = jnp.concatenate([z(lo), -sin, z(half), z(tail)], axis=1)
    return c, s_up, s_dn


MLA_FWD_CFG = (1, 1024)
MLA_BWD_CFG = (2, 512)
SB_FWD_CFG = (2, 256)
SB_BWD_CFG = (2, 256)


def _tile_mask(bk, strict):
    row = lax.broadcasted_iota(jnp.int32, (bk, bk), 0)
    col = lax.broadcasted_iota(jnp.int32, (bk, bk), 1)
    return (col < row) if strict else (col <= row)


def _att_layout(S, cfg):
    nch, bk = cfg
    bq = nch * bk
    assert S % bq == 0, (S, cfg)
    rows = [slice(r * bk, (r + 1) * bk) for r in range(nch)]
    q_spec = lambda off=0: pl.BlockSpec((bq, LANES), lambda h, i: (i, off + h))
    kv_spec = lambda off=0: pl.BlockSpec((S, LANES), lambda h, i: (0, off + h))
    return bq, rows, q_spec, kv_spec


def _total(terms):
    terms = list(terms)
    out = terms[0]
    for t in terms[1:]:
        out = out + t
    return out


def _walk(nch, i, step, carry, leftward, alive=None):
    everyone = range(nch)
    if leftward:
        for d in reversed(everyone):
            carry = step(nch * i + d, carry, range(d, nch), {d})
        if alive is None:
            return lax.fori_loop(0, nch * i, lambda t, c: step(nch * i - 1 - t, c, everyone, set()), carry)
        more = lambda tc: jnp.logical_and(tc[0] < nch * i, alive(tc[1]))
        left = lambda tc: (tc[0] + 1, step(nch * i - 1 - tc[0], tc[1], everyone, set()))
        return lax.while_loop(more, left, (jnp.int32(0), carry))[1]
    carry = lax.fori_loop(0, nch * i, lambda kb, c: step(kb, c, everyone, set()), carry)
    for d in everyone:
        carry = step(nch * i + d, carry, range(d, nch), {d})
    return carry


def _softmax_attn_fwd(q, k, v, heads, scale, name, q_off=0, k_off=0, v_off=0):
    S = q.shape[0]
    nch, bk = MLA_FWD_CFG
    bq, rows, q_spec, kv_spec = _att_layout(S, MLA_FWD_CFG)

    def body(q_ref, k_ref, v_ref, o_ref, lse_ref):
        i = pl.program_id(1)
        qs = [q_ref[rw, :] for rw in rows]

        def step(kb, cs, active, masked):
            off = pl.multiple_of(kb * bk, bk)
            ks, vs = k_ref[pl.ds(off, bk), :], v_ref[pl.ds(off, bk), :]
            A = list(active)
            s = {r: _dot(qs[r], ks, NT) * scale for r in A}
            s = {r: (jnp.where(_tile_mask(bk, False), s[r], -1e30) if r in masked else s[r]) for r in A}
            m_new = {r: jnp.maximum(cs[r][0], jnp.max(s[r], axis=1, keepdims=True)) for r in A}
            p = {r: jnp.exp(s[r] - m_new[r]) for r in A}
            alpha = {r: jnp.exp(cs[r][0] - m_new[r]) for r in A}
            new = list(cs)
            for r in A:
                new[r] = (m_new[r], alpha[r] * cs[r][1] + jnp.sum(p[r], axis=1, keepdims=True),
                          alpha[r] * cs[r][2] + _dot(p[r].astype(BF16), vs, NN))
            return tuple(new)

        init = (jnp.full((bk, 1), -1e30, F32), jnp.zeros((bk, 1), F32), jnp.zeros((bk, LANES), F32))
        cs = _walk(nch, i, step, tuple(init for _ in rows), False)
        for r, (m, l, acc) in enumerate(cs):
            o_ref[rows[r], :] = (acc / l).astype(o_ref.dtype)
            lse_ref[rows[r], :] = m + jnp.log(l)

    return pl.pallas_call(
        body,
        name=name,
        grid=(heads, S // bq),
        in_specs=[q_spec(q_off), kv_spec(k_off), kv_spec(v_off)],
        out_specs=[q_spec(), pl.BlockSpec((None, bq, 1), lambda h, i: (h, i, 0))],
        out_shape=[jax.ShapeDtypeStruct((S, heads * LANES), BF16), jax.ShapeDtypeStruct((heads, S, 1), F32)],
        compiler_params=_params(("parallel", "arbitrary")),
    )(q, k, v)


def _softmax_attn_bwd(q, k, v, o, lse, do, heads, scale, name, q_off=0, k_off=0, v_off=0):
    S = q.shape[0]
    nch, bk = MLA_BWD_CFG
    bq, rows, q_spec, kv_spec = _att_layout(S, MLA_BWD_CFG)

    def body(q_ref, k_ref, v_ref, o_ref, lse_ref, do_ref, dq_ref, dk_ref, dv_ref):
        i = pl.program_id(1)

        @pl.when(i == 0)
        def _():
            dk_ref[...] = jnp.zeros_like(dk_ref)
            dv_ref[...] = jnp.zeros_like(dv_ref)

        qs = [q_ref[rw, :] for rw in rows]
        dos = [do_ref[rw, :] for rw in rows]
        lses = [lse_ref[rw, :] for rw in rows]
        deltas = [jnp.sum(dos[r].astype(F32) * o_ref[rows[r], :].astype(F32), axis=1, keepdims=True) for r in range(nch)]

        def step(kb, dqs, active, masked):
            off = pl.multiple_of(kb * bk, bk)
            ks, vs = k_ref[pl.ds(off, bk), :], v_ref[pl.ds(off, bk), :]
            A = list(active)
            s = {r: _dot(qs[r], ks, NT) * scale for r in A}
            s = {r: (jnp.where(_tile_mask(bk, False), s[r], -1e30) if r in masked else s[r]) for r in A}
            p = {r: jnp.exp(s[r] - lses[r]) for r in A}
            dp = {r: _dot(dos[r], vs, NT) for r in A}
            ds = {r: (p[r] * (dp[r] - deltas[r]) * scale).astype(BF16) for r in A}
            dv_c = _total(_dot(p[r].astype(BF16), dos[r], TN) for r in A)
            dk_c = _total(_dot(ds[r], qs[r], TN) for r in A)
            dk_ref[pl.ds(off, bk), :] += dk_c
            dv_ref[pl.ds(off, bk), :] += dv_c
            new = list(dqs)
            for r in A:
                new[r] = dqs[r] + _dot(ds[r], ks, NN)
            return tuple(new)

        dqs = _walk(nch, i, step, tuple(jnp.zeros((bk, LANES), F32) for _ in rows), False)
        for r in range(nch):
            dq_ref[rows[r], :] = dqs[r]

    return pl.pallas_call(
        body,
        name=name,
        grid=(heads, S // bq),
        in_specs=[q_spec(q_off), kv_spec(k_off), kv_spec(v_off), q_spec(),
                  pl.BlockSpec((None, bq, 1), lambda h, i: (h, i, 0)), q_spec()],
        out_specs=[q_spec(), kv_spec(), kv_spec()],
        out_shape=[jax.ShapeDtypeStruct((S, heads * LANES), F32)] * 3,
        compiler_params=_params(("parallel", "arbitrary")),
    )(q, k, v, o, lse, do)


def _tri(n, inclusive):
    r = lax.broadcasted_iota(jnp.int32, (n, n), 0)
    c = lax.broadcasted_iota(jnp.int32, (n, n), 1)
    return jnp.where((r >= c) if inclusive else (r > c), 1.0, 0.0).astype(BF16)


def _suffix_sum(x, tri):
    hi = x.astype(BF16)
    lo = (x - hi.astype(F32)).astype(BF16)
    return _dot(hi, tri, NN) + _dot(lo, tri, NN)


def _sb_logs(z):
    lg = jnp.log(1.0 + jnp.exp(-jnp.abs(z)))
    l1m = -(jnp.maximum(z, 0.0) + lg)
    return l1m, l1m + z


SB_SCALE = SB_HEAD_DIM ** -0.5
assert SB_SCALE == 0.125
SB_DEAD = -110.0


def _sb_alive(cs):
    top = cs[0][0]
    for c in cs[1:]:
        top = jnp.maximum(top, c[0])
    return jnp.max(top) > SB_DEAD


def _sb_attn_fwd(qkv, heads, name, q_off, k_off, v_off):
    S = qkv.shape[0]
    nch, bk = SB_FWD_CFG
    bq, rows, q_spec, kv_spec = _att_layout(S, SB_FWD_CFG)

    def body(q_ref, k_ref, v_ref, o_ref):
        i = pl.program_id(1)
        qs = [q_ref[rw, :] * SB_SCALE for rw in rows]
        tri = _tri(bk, False)

        def step(kb, cs, active, masked):
            off = pl.multiple_of(kb * bk, bk)
            ks, vs = k_ref[pl.ds(off, bk), :], v_ref[pl.ds(off, bk), :]
            A = list(active)
            lg = {r: _sb_logs(_dot(qs[r], ks, NT)) for r in A}
            l1m = {r: (jnp.where(_tile_mask(bk, True), lg[r][0], 0.0) if r in masked else lg[r][0]) for r in A}
            suf = {r: _suffix_sum(l1m[r], tri) for r in A}
            ex = {r: lg[r][1] + cs[r][0] + suf[r] for r in A}
            ex = {r: (jnp.where(_tile_mask(bk, True), ex[r], -1e30) if r in masked else ex[r]) for r in A}
            ab = {r: jnp.exp(ex[r]).astype(BF16) for r in A}
            new = list(cs)
            for r in A:
                new[r] = (cs[r][0] + jnp.sum(l1m[r], axis=1, keepdims=True), cs[r][1] + _dot(ab[r], vs, NN))
            return tuple(new)

        init = (jnp.zeros((bk, 1), F32), jnp.zeros((bk, LANES), F32))
        cs = _walk(nch, i, step, tuple(init for _ in rows), True, _sb_alive)
        for r in range(nch):
            o_ref[rows[r], :] = cs[r][1]

    return pl.pallas_call(
        body,
        name=name,
        grid=(heads, S // bq),
        in_specs=[q_spec(q_off), kv_spec(k_off), kv_spec(v_off)],
        out_specs=q_spec(),
        out_shape=jax.ShapeDtypeStruct((S, heads * LANES), F32),
        compiler_params=_params(("parallel", "arbitrary")),
    )(qkv, qkv, qkv)


def _sb_attn_bwd(qkv, o, do, heads, name, q_off, k_off, v_off):
    S = qkv.shape[0]
    nch, bk = SB_BWD_CFG
    bq, rows, q_spec, kv_spec = _att_layout(S, SB_BWD_CFG)

    def body(q_ref, k_ref, v_ref, o_ref, do_ref, dq_ref, dk_ref, dv_ref):
        i = pl.program_id(1)

        @pl.when(i == 0)
        def _():
            dk_ref[...] = jnp.zeros_like(dk_ref)
            dv_ref[...] = jnp.zeros_like(dv_ref)

        tri = _tri(bk, False)
        qs = [q_ref[rw, :] * SB_SCALE for rw in rows]
        dos = [do_ref[rw, :] for rw in rows]
        deltas = [jnp.sum(dos[r].astype(F32) * o_ref[rows[r], :], axis=1, keepdims=True) for r in range(nch)]

        def step(kb, cs, active, masked):
            off = pl.multiple_of(kb * bk, bk)
            ks, vs = k_ref[pl.ds(off, bk), :], v_ref[pl.ds(off, bk), :]
            A = list(active)
            lg = {r: _sb_logs(_dot(qs[r], ks, NT)) for r in A}
            l1m = {r: (jnp.where(_tile_mask(bk, True), lg[r][0], 0.0) if r in masked else lg[r][0]) for r in A}
            suf = {r: _suffix_sum(l1m[r], tri) for r in A}
            ex = {r: lg[r][1] + cs[r][0] + suf[r] for r in A}
            ex = {r: (jnp.where(_tile_mask(bk, True), ex[r], -1e30) if r in masked else ex[r]) for r in A}
            ab = {r: jnp.exp(ex[r]).astype(BF16) for r in A}
            da = {r: _dot(dos[r], vs, NT) for r in A}
            g = {r: ab[r].astype(F32) * da[r] for r in A}
            gs = {r: _suffix_sum(g[r], tri) for r in A}
            beta = {r: jnp.exp(lg[r][1]) for r in A}
            dz = {r: g[r] - beta[r] * (deltas[r] - cs[r][1] - gs[r]) for r in A}
            dz = {r: (jnp.where(_tile_mask(bk, True), dz[r], 0.0) if r in masked else dz[r]) for r in A}
            dzb = {r: dz[r].astype(BF16) for r in A}
            dv_c = _total(_dot(ab[r], dos[r], TN) for r in A)
            dk_c = _total(_dot(dzb[r], qs[r], TN) for r in A)
            dk_ref[pl.ds(off, bk), :] += dk_c
            dv_ref[pl.ds(off, bk), :] += dv_c
            new = list(cs)
            for r in A:
                new[r] = (cs[r][0] + jnp.sum(l1m[r], axis=1, keepdims=True),
                          cs[r][1] + jnp.sum(g[r], axis=1, keepdims=True), cs[r][2] + _dot(dzb[r], ks, NN))
            return tuple(new)

        zcol = jnp.zeros((bk, 1), F32)
        init = (zcol, zcol, jnp.zeros((bk, LANES), F32))
        cs = _walk(nch, i, step, tuple(init for _ in rows), True, _sb_alive)
        for r in range(nch):
            dq_ref[rows[r], :] = cs[r][2] * SB_SCALE

    return pl.pallas_call(
        body,
        name=name,
        grid=(heads, S // bq),
        in_specs=[q_spec(q_off), kv_spec(k_off), kv_spec(v_off), q_spec(), q_spec()],
        out_specs=[q_spec(), kv_spec(), kv_spec()],
        out_shape=[jax.ShapeDtypeStruct((S, heads * LANES), F32)] * 3,
        compiler_params=_params(("parallel", "arbitrary")),
    )(qkv, qkv, qkv, o, do)


SWA_BLK = 128
SWA_GROUP = SWA_HEADS // SWA_KV_HEADS


SWA_NB = 4
SWA_ROWS = SWA_NB * SWA_BLK


def _swa_band_mask(first):
    row = lax.broadcasted_iota(jnp.int32, (SWA_BLK, 2 * SWA_BLK), 0)
    col = lax.broadcasted_iota(jnp.int32, (SWA_BLK, 2 * SWA_BLK), 1)
    return (col > row) & (col <= row + SWA_WINDOW) & (jnp.logical_not(first) | (col >= SWA_BLK))


def _swa_in_specs(v_off):
    gw = SWA_GROUP * LANES
    before = lambda h, n: (jnp.maximum(SWA_NB * n - 1, 0), h)
    return [
        pl.BlockSpec((SWA_ROWS, gw), lambda h, n: (n, h)),
        pl.BlockSpec((SWA_BLK, LANES), before),
        pl.BlockSpec((SWA_ROWS, LANES), lambda h, n: (n, h)),
        pl.BlockSpec((SWA_BLK, LANES), lambda h, n: (jnp.maximum(SWA_NB * n - 1, 0), v_off + h)),
        pl.BlockSpec((SWA_ROWS, LANES), lambda h, n: (n, v_off + h)),
        pl.BlockSpec((1, gw), lambda h, n: (0, h)),
    ]


def _swa_bands(n, kp_ref, kc_ref, vp_ref, vc_ref):
    k_all = jnp.concatenate([kp_ref[...], kc_ref[...]], axis=0)
    v_all = jnp.concatenate([vp_ref[...], vc_ref[...]], axis=0)
    bands = []
    for j in range(SWA_NB):
        rows = slice(j * SWA_BLK, (j + 2) * SWA_BLK)
        bands.append((k_all[rows], v_all[rows], _swa_band_mask((n == 0) if j == 0 else False)))
    return bands


def _swa_fwd(q, k, v, v_off, sink_b, name):
    S = q.shape[0]
    assert S % SWA_ROWS == 0
    scale = SWA_HEAD_DIM ** -0.5
    gw = SWA_GROUP * LANES

    def body(q_ref, kp_ref, kc_ref, vp_ref, vc_ref, sink_ref, o_ref, lse_ref):
        n = pl.program_id(1)
        bands = _swa_bands(n, kp_ref, kc_ref, vp_ref, vc_ref)
        P = [(j, g) for j in range(SWA_NB) for g in range(SWA_GROUP)]
        rows = lambda j: slice(j * SWA_BLK, (j + 1) * SWA_BLK)
        lanes = lambda g: slice(g * LANES, (g + 1) * LANES)
        sk = {g: sink_ref[:, g * LANES:g * LANES + 1] for g in range(SWA_GROUP)}
        s = {(j, g): jnp.where(bands[j][2], _dot(q_ref[rows(j), lanes(g)], bands[j][0], NT) * scale, -1e30) for j, g in P}
        m = {(j, g): jnp.maximum(jnp.max(s[j, g], axis=1, keepdims=True), sk[g]) for j, g in P}
        p = {(j, g): jnp.exp(s[j, g] - m[j, g]) for j, g in P}
        den = {(j, g): jnp.sum(p[j, g], axis=1, keepdims=True) + jnp.exp(sk[g] - m[j, g]) for j, g in P}
        for j, g in P:
            o_ref[rows(j), lanes(g)] = _dot((p[j, g] / den[j, g]).astype(BF16), bands[j][1], NN).astype(o_ref.dtype)
            lse_ref[g, rows(j), :] = m[j, g] + jnp.log(den[j, g])

    return pl.pallas_call(
        body,
        name=name,
        grid=(SWA_KV_HEADS, S // SWA_ROWS),
        in_specs=_swa_in_specs(v_off),
        out_specs=[
            pl.BlockSpec((SWA_ROWS, gw), lambda h, n: (n, h)),
            pl.BlockSpec((SWA_GROUP, SWA_ROWS, 1), lambda h, n: (h, n, 0)),
        ],
        out_shape=[jax.ShapeDtypeStruct((S, SWA_HEADS * LANES), BF16), jax.ShapeDtypeStruct((SWA_HEADS, S, 1), F32)],
        compiler_params=_params(("parallel", "arbitrary")),
    )(q, k, k, v, v, sink_b)


def _swa_bwd(q, k, v, v_off, sink_b, o, lse, do, name):
    S = q.shape[0]
    assert S % SWA_ROWS == 0
    scale = SWA_HEAD_DIM ** -0.5
    gw = SWA_GROUP * LANES

    def body(q_ref, kp_ref, kc_ref, vp_ref, vc_ref, sink_ref, o_ref, lse_ref, do_ref, dq_ref, dk_ref, dv_ref, dsink_ref):
        n = pl.program_id(1)

        @pl.when(n == 0)
        def _():
            dk_ref[...] = jnp.zeros_like(dk_ref)
            dv_ref[...] = jnp.zeros_like(dv_ref)
            dsink_ref[...] = jnp.zeros_like(dsink_ref)

        bands = _swa_bands(n, kp_ref, kc_ref, vp_ref, vc_ref)
        P = [(j, g) for j in range(SWA_NB) for g in range(SWA_GROUP)]
        rows = lambda j: slice(j * SWA_BLK, (j + 1) * SWA_BLK)
        lanes = lambda g: slice(g * LANES, (g + 1) * LANES)
        qs = {(j, g): q_ref[rows(j), lanes(g)] for j, g in P}
        dos = {(j, g): do_ref[rows(j), lanes(g)] for j, g in P}
        lses = {(j, g): lse_ref[g, rows(j), :] for j, g in P}
        delta = {(j, g): jnp.sum(dos[j, g].astype(F32) * o_ref[rows(j), lanes(g)].astype(F32), axis=1, keepdims=True)
                 for j, g in P}
        s = {(j, g): jnp.where(bands[j][2], _dot(qs[j, g], bands[j][0], NT) * scale, -1e30) for j, g in P}
        p = {(j, g): jnp.exp(s[j, g] - lses[j, g]) for j, g in P}
        dp = {(j, g): _dot(dos[j, g], bands[j][1], NT) for j, g in P}
        ds = {(j, g): (p[j, g] * (dp[j, g] - delta[j, g]) * scale).astype(BF16) for j, g in P}
        for j, g in P:
            dq_ref[rows(j), lanes(g)] = _dot(ds[j, g], bands[j][0], NN)
        for g in range(SWA_GROUP):
            p_sink = [jnp.exp(sink_ref[:, g * LANES:g * LANES + 1] - lses[j, g]) * delta[j, g] for j in range(SWA_NB)]
            dsink_ref[:, lanes(g)] += jnp.zeros((1, LANES), F32) - jnp.sum(_total(p_sink), axis=0, keepdims=True)
        dkb = [_total(_dot(ds[j, g], qs[j, g], TN) for g in range(SWA_GROUP)) for j in range(SWA_NB)]
        dvb = [_total(_dot(p[j, g].astype(BF16), dos[j, g], TN) for g in range(SWA_GROUP)) for j in range(SWA_NB)]
        base = pl.multiple_of(n * SWA_ROWS, SWA_ROWS)
        for j in range(SWA_NB):
            own = pl.ds(base + j * SWA_BLK, SWA_BLK)
            after = j + 1 < SWA_NB
            dk_ref[own, :] += dkb[j][SWA_BLK:] + dkb[j + 1][:SWA_BLK] if after else dkb[j][SWA_BLK:]
            dv_ref[own, :] += dvb[j][SWA_BLK:] + dvb[j + 1][:SWA_BLK] if after else dvb[j][SWA_BLK:]

        @pl.when(n > 0)
        def _():
            before = pl.ds(pl.multiple_of(n * SWA_ROWS - SWA_BLK, SWA_BLK), SWA_BLK)
            dk_ref[before, :] += dkb[0][:SWA_BLK]
            dv_ref[before, :] += dvb[0][:SWA_BLK]

    return pl.pallas_call(
        body,
        name=name,
        grid=(SWA_KV_HEADS, S // SWA_ROWS),
        in_specs=_swa_in_specs(v_off) + [
            pl.BlockSpec((SWA_ROWS, gw), lambda h, n: (n, h)),
            pl.BlockSpec((SWA_GROUP, SWA_ROWS, 1), lambda h, n: (h, n, 0)),
            pl.BlockSpec((SWA_ROWS, gw), lambda h, n: (n, h)),
        ],
        out_specs=[
            pl.BlockSpec((SWA_ROWS, gw), lambda h, n: (n, h)),
            pl.BlockSpec((S, LANES), lambda h, n: (0, h)),
            pl.BlockSpec((S, LANES), lambda h, n: (0, h)),
            pl.BlockSpec((1, gw), lambda h, n: (0, h)),
        ],
        out_shape=[
            jax.ShapeDtypeStruct((S, SWA_HEADS * LANES), F32),
            jax.ShapeDtypeStruct((S, SWA_KV_HEADS * LANES), F32),
            jax.ShapeDtypeStruct((S, SWA_KV_HEADS * LANES), F32),
            jax.ShapeDtypeStruct((1, SWA_HEADS * LANES), F32),
        ],
        compiler_params=_params(("parallel", "arbitrary")),
    )(q, k, k, v, v, sink_b, o, lse, do)


def _pad_cols(w, heads, real):
    k = w.shape[0]
    return jnp.pad(w.reshape(k, heads, real), ((0, 0), (0, 0), (0, LANES - real))).reshape(k, heads * LANES)


def _unpad_cols(g, heads, real):
    k = g.shape[0]
    return g.reshape(k, heads, LANES)[:, :, :real].reshape(k, heads * real)


def _pad_rows(w, heads, real):
    n = w.shape[1]
    return jnp.pad(w.reshape(heads, real, n), ((0, 0), (0, LANES - real), (0, 0))).reshape(heads * LANES, n)


def _unpad_rows(g, heads, real):
    n = g.shape[1]
    return g.reshape(heads, LANES, n)[:, :real, :].reshape(heads * real, n)


def _w_in_internal(w_in):
    c_q, c_kv, k_r, q_swa, k_swa, v_swa, q_sb, k_sb, v_sb, gate = jnp.split(w_in, SPLIT_POINTS, axis=1)
    k_r = jnp.pad(k_r, ((0, 0), (MLA_NOPE, LANES - MLA_NOPE - MLA_ROPE)))
    w1 = jnp.concatenate([c_q, c_kv, k_r, _pad_cols(q_swa, 8, 64), _pad_cols(k_swa, 2, 64)], axis=1)
    w2 = [_pad_cols(v_swa, 2, 64), _pad_cols(q_sb, 8, 64), _pad_cols(k_sb, 8, 64), _pad_cols(v_sb, 8, 64)]
    return w1, w2, gate


def _w_in_reference(g1, g2, g3):
    c_q, c_kv, k_r, q_swa, k_swa = jnp.split(g1, [256, 384, 512, 1536], axis=1)
    v_swa, q_sb, k_sb, v_sb = g2
    return jnp.concatenate([
        c_q, c_kv, k_r[:, MLA_NOPE:MLA_NOPE + MLA_ROPE], _unpad_cols(q_swa, 8, 64), _unpad_cols(k_swa, 2, 64),
        _unpad_cols(v_swa, 2, 64), _unpad_cols(q_sb, 8, 64), _unpad_cols(k_sb, 8, 64), _unpad_cols(v_sb, 8, 64),
        g3], axis=1)


def _w_ukv_internal(w):
    w3 = w.reshape(MLA_KV_LORA, MLA_HEADS, MLA_NOPE + MLA_V)
    pad = lambda t: jnp.pad(t, ((0, 0), (0, 0), (0, LANES - t.shape[2]))).reshape(MLA_KV_LORA, MLA_HEADS * LANES)
    return pad(w3[:, :, :MLA_NOPE]), pad(w3[:, :, MLA_NOPE:])


def _w_ukv_reference(gk, gv):
    gk = gk.reshape(MLA_KV_LORA, MLA_HEADS, LANES)[:, :, :MLA_NOPE]
    gv = gv.reshape(MLA_KV_LORA, MLA_HEADS, LANES)[:, :, :MLA_V]
    return jnp.concatenate([gk, gv], axis=2).reshape(MLA_KV_LORA, MLA_HEADS * (MLA_NOPE + MLA_V))


def _layer_fwd(x, w, tabs):
    mla_tab, swa_tab = tabs
    sv = {"x": x}

    def f_norm(rows, consts):
        return [_rms(rows[0], consts[0])], []

    (h,) = _rowwise(f_norm, [x], [w["g_mix_pre"]], [(D_MODEL, BF16)], [], "norm_mix_pre")
    p1 = _matmul(h, w["w_in1"], "nn", [F32], "proj_lat")
    p2 = _matmul(h, w["w_in2"], "nn", [BF16], "proj_qkv")
    gates = _matmul(h, w["w_in3"], "nn", [BF16], "proj_gate",
                    epilogue=lambda acc, b: (1.0 / (1.0 + jnp.exp(-(acc + b))),), row_extras=[w["b_gate"]])

    def f_prep(rows, consts):
        t = rows[0]
        gq, gkv = consts[0], consts[1]
        mc, mu, md = rows[1], rows[2], rows[3]
        sc, su, sd = rows[4], rows[5], rows[6]
        cq_n = _rms(t[:, 0:256], gq)
        ckv_n = _rms(t[:, 256:384], gkv)
        kr = _rope(t[:, 384:512], mc, mu, md, MLA_ROPE // 2)
        qs = [_rope(t[:, 512 + j * LANES:512 + (j + 1) * LANES], sc, su, sd, SWA_HEAD_DIM // 2) for j in range(8)]
        ks = [_rope(t[:, 1536 + j * LANES:1536 + (j + 1) * LANES], sc, su, sd, SWA_HEAD_DIM // 2) for j in range(2)]
        return [cq_n, ckv_n, kr, jnp.concatenate(qs, axis=1), jnp.concatenate(ks, axis=1)], []

    cq_n, ckv_n, kr, q_swa, k_swa = _rowwise(
        f_prep, [p1, *mla_tab["k"], *swa_tab["f"]], [w["g_q_lat"], w["g_kv_lat"]],
        [(256, BF16), (128, BF16), (LANES, F32), (1024, BF16), (256, BF16)], [], "lat_prep")

    q_lat = _matmul(cq_n, w["w_uq"], "nn", [F32], "mla_q_up")
    k_lat = _matmul(ckv_n, w["w_ukv_k"], "nn", [F32], "mla_k_up")
    v_mla = _matmul(ckv_n, w["w_ukv_v"], "nn", [BF16], "mla_v_up")

    def f_mla_prep(rows, consts):
        ql, kl, krr, mc, mu, md = rows
        qs = [_rope(ql[:, j * LANES:(j + 1) * LANES], mc, mu, md, MLA_ROPE // 2) for j in range(8)]
        ks = [kl[:, j * LANES:(j + 1) * LANES] + krr for j in range(8)]
        return [jnp.concatenate(qs, axis=1), jnp.concatenate(ks, axis=1)], []

    q_mla, k_mla = _rowwise(f_mla_prep, [q_lat, k_lat, kr, *mla_tab["q"]], [], [(1024, BF16), (1024, BF16)], [], "mla_prep")

    o_mla, lse_mla = _softmax_attn_fwd(q_mla, k_mla, v_mla, MLA_HEADS, (MLA_NOPE + MLA_ROPE) ** -0.5, "mla_fwd")
    o_swa, lse_swa = _swa_fwd(q_swa, k_swa, p2, 0, w["sink_b"], "swa_fwd")
    o_sb = _sb_attn_fwd(p2, SB_HEADS, "sb_fwd", 2, 10, 18)

    oa = _matmul(o_mla, w["w_o_mla"], "nn", [F32], "o_proj_mla")
    ob = _matmul(o_swa, w["w_o_swa"], "nn", [F32], "o_proj_swa")
    oc = _matmul(o_sb, w["w_o_sb"], "nn", [F32], "o_proj_sb")

    def f_mix(rows, consts):
        a, b, c, g = rows
        g = g.astype(F32)
        return [g[:, 0:1024] * a + g[:, 1024:2048] * b + g[:, 2048:3072] * c], []

    (mixed,) = _rowwise(f_mix, [oa, ob, oc, gates], [], [(D_MODEL, BF16)], [], "gate_mix")
    y = _matmul(mixed, w["w_out"], "nn", [F32], "out_proj")

    def f_res_norm(rows, consts):
        return [rows[0] + _rms(rows[1], consts[0])], []

    (x1,) = _rowwise(f_res_norm, [x, y], [w["g_mix_post"]], [(D_MODEL, F32)], [], "res_norm_mix")
    (h2,) = _rowwise(f_norm, [x1], [w["g_mlp_pre"]], [(D_MODEL, BF16)], [], "norm_mlp_pre")

    def relu2(acc):
        r = jnp.maximum(acc, 0.0)
        return acc, r * r

    up, u = _matmul(h2, w["w_up"], "nn", [BF16, BF16], "mlp_up", epilogue=relu2)
    zd = _matmul(u, w["w_down"], "nn", [F32], "mlp_down")
    (x2,) = _rowwise(f_res_norm, [x1, zd], [w["g_mlp_post"]], [(D_MODEL, F32)], [], "res_norm_mlp")

    sv.update(h=h, p1=p1, p2=p2, gates=gates, cq_n=cq_n, ckv_n=ckv_n, q_swa=q_swa, k_swa=k_swa, q_mla=q_mla,
              k_mla=k_mla, v_mla=v_mla, o_mla=o_mla, lse_mla=lse_mla, o_swa=o_swa, lse_swa=lse_swa, o_sb=o_sb,
              oa=oa, ob=ob, oc=oc, mixed=mixed, y=y, x1=x1, h2=h2, up=up, u=u, zd=zd)
    return x2, sv


def _layer_bwd(dx2, w, sv, tabs):
    mla_tab, swa_tab = tabs
    gr = {}

    def f_norm_bwd(rows, consts):
        dx, dg = _rms_bwd(rows[0], consts[0], rows[1])
        return [dx], [dg]

    def f_norm_bwd_res(rows, consts):
        dx, dg = _rms_bwd(rows[0], consts[0], rows[1])
        return [rows[2] + dx], [dg]

    dzd, gr["g_mlp_post"] = _rowwise(f_norm_bwd, [sv["zd"], dx2], [w["g_mlp_post"]], [(D_MODEL, BF16)], [D_MODEL], "b_norm_mlp_post")
    gr["w_down"] = _matmul(sv["u"], dzd, "tn", [F32], "b_w_down")
    dup = _matmul(dzd, w["w_down"], "nt", [BF16], "b_mlp_down",
                  epilogue=lambda acc, up: (acc * 2.0 * jnp.maximum(up.astype(F32), 0.0),), extras=[sv["up"]])
    gr["w_up"] = _matmul(sv["h2"], dup, "tn", [F32], "b_w_up")
    dh2 = _matmul(dup, w["w_up"], "nt", [F32], "b_mlp_up")
    dx1, gr["g_mlp_pre"] = _rowwise(f_norm_bwd_res, [sv["x1"], dh2, dx2], [w["g_mlp_pre"]], [(D_MODEL, F32)], [D_MODEL], "b_norm_mlp_pre")

    dy, gr["g_mix_post"] = _rowwise(f_norm_bwd, [sv["y"], dx1], [w["g_mix_post"]], [(D_MODEL, BF16)], [D_MODEL], "b_norm_mix_post")
    gr["w_out"] = _matmul(sv["mixed"], dy, "tn", [F32], "b_w_out")
    dmixed = _matmul(dy, w["w_out"], "nt", [F32], "b_out_proj")

    def f_mix_bwd(rows, consts):
        dm, a, b, c, g = rows
        g = g.astype(F32)
        outs, dls = [], []
        for j, o in enumerate((a, b, c)):
            gj = g[:, j * D_MODEL:(j + 1) * D_MODEL]
            outs.append(dm * gj)
            dls.append(dm * o * gj * (1.0 - gj))
        dl = jnp.concatenate(dls, axis=1)
        return outs + [dl], [dl]

    doa, dob, doc, dlogit, gr["b_gate"] = _rowwise(
        f_mix_bwd, [dmixed, sv["oa"], sv["ob"], sv["oc"], sv["gates"]], [],
        [(D_MODEL, BF16)] * 3 + [(P3_W, BF16)], [P3_W], "b_gate_mix")

    gr["w_o_mla"] = _matmul(sv["o_mla"], doa, "tn", [F32], "b_w_o_mla")
    gr["w_o_swa"] = _matmul(sv["o_swa"], dob, "tn", [F32], "b_w_o_swa")
    gr["w_o_sb"] = _matmul(sv["o_sb"], doc, "tn", [F32], "b_w_o_sb")
    do_mla = _matmul(doa, w["w_o_mla"], "nt", [BF16], "b_o_proj_mla")
    do_swa = _matmul(dob, w["w_o_swa"], "nt", [BF16], "b_o_proj_swa")
    do_sb = _matmul(doc, w["w_o_sb"], "nt", [BF16], "b_o_proj_sb")

    dq_sb, dk_sb, dv_sb = _sb_attn_bwd(sv["p2"], sv["o_sb"], do_sb, SB_HEADS, "sb_bwd", 2, 10, 18)
    dq_swa, dk_swa, dv_swa, dsink = _swa_bwd(sv["q_swa"], sv["k_swa"], sv["p2"], 0, w["sink_b"], sv["o_swa"],
                                             sv["lse_swa"], do_swa, "swa_bwd")
    gr["swa_sinks"] = dsink.reshape(SWA_HEADS, LANES)[:, 0]
    dq_mla, dk_mla, dv_mla = _softmax_attn_bwd(sv["q_mla"], sv["k_mla"], sv["v_mla"], sv["o_mla"], sv["lse_mla"], do_mla,
                                               MLA_HEADS, (MLA_NOPE + MLA_ROPE) ** -0.5, "mla_bwd")

    def f_mla_post(rows, consts):
        dq, dk, qc, qu, qd, kc, ku, kd = rows
        dqs = [_rope(dq[:, j * LANES:(j + 1) * LANES], qc, qu, qd, MLA_ROPE // 2) for j in range(8)]
        dkr = dk[:, 0:LANES]
        for j in range(1, 8):
            dkr = dkr + dk[:, j * LANES:(j + 1) * LANES]
        return [jnp.concatenate(dqs, axis=1), _rope(dkr, kc, ku, kd, MLA_ROPE // 2)], []

    dq_lat, dkr = _rowwise(f_mla_post, [dq_mla, dk_mla, *mla_tab["q_inv"], *mla_tab["k_inv"]], [],
                           [(1024, BF16), (LANES, F32)], [], "b_mla_post")
    gr["w_uq"] = _matmul(sv["cq_n"], dq_lat, "tn", [F32], "b_w_uq")
    gr["w_ukv_k"] = _matmul(sv["ckv_n"], dk_mla, "tn", [F32], "b_w_ukv_k")
    gr["w_ukv_v"] = _matmul(sv["ckv_n"], dv_mla, "tn", [F32], "b_w_ukv_v")
    dcq_n = _matmul(dq_lat, w["w_uq"], "nt", [F32], "b_mla_q_up")
    dckv_a = _matmul(dk_mla, w["w_ukv_k"], "nt", [F32], "b_mla_k_up")
    dckv_b = _matmul(dv_mla, w["w_ukv_v"], "nt", [F32], "b_mla_v_up")

    def f_prep_bwd(rows, consts):
        t, dcq, dca, dcb, dkr_, dqs, dks, sc, su, sd = rows
        gq, gkv = consts
        dc_q, dgq = _rms_bwd(t[:, 0:256], gq, dcq)
        dc_kv, dgkv = _rms_bwd(t[:, 256:384], gkv, dca + dcb)
        q_parts = [_rope(dqs[:, j * LANES:(j + 1) * LANES], sc, su, sd, SWA_HEAD_DIM // 2) for j in range(8)]
        k_parts = [_rope(dks[:, j * LANES:(j + 1) * LANES], sc, su, sd, SWA_HEAD_DIM // 2) for j in range(2)]
        return [jnp.concatenate([dc_q, dc_kv, dkr_] + q_parts + k_parts, axis=1)], [dgq, dgkv]

    dp1, gr["g_q_lat"], gr["g_kv_lat"] = _rowwise(
        f_prep_bwd, [sv["p1"], dcq_n, dckv_a, dckv_b, dkr, dq_swa, dk_swa, *swa_tab["inv"]], [w["g_q_lat"], w["g_kv_lat"]],
        [(P1_W, BF16)], [256, 128], "b_lat_prep")

    gr["w_in1"] = _matmul(sv["h"], dp1, "tn", [F32], "b_w_in_lat")
    dh = _matmul(dp1, w["w_in1"], "nt", [F32], "b_proj_lat")
    gr["w_in2"] = []
    add_prev = lambda acc, prev: (acc + prev,)
    for piece, wp, tag in zip((dv_swa, dq_sb, dk_sb, dv_sb), w["w_in2_parts"], ("vswa", "qsb", "ksb", "vsb")):
        gr["w_in2"].append(_matmul(sv["h"], piece, "tn", [F32], "b_w_in_" + tag))
        dh = _matmul(piece, wp, "nt", [F32], "b_proj_" + tag, epilogue=add_prev, extras=[dh])
    gr["w_in3"] = _matmul(sv["h"], dlogit, "tn", [F32], "b_w_in_gate")
    dh = _matmul(dlogit, w["w_in3"], "nt", [F32], "b_proj_gate", epilogue=add_prev, extras=[dh])
    dx, gr["g_mix_pre"] = _rowwise(f_norm_bwd_res, [sv["x"], dh, dx1], [w["g_mix_pre"]], [(D_MODEL, F32)], [D_MODEL], "b_norm_mix_pre")
    return dx, gr


def _local_step(x, positions, loss_target, full):
    mc, mu, md = _rope_tables(positions, MLA_NOPE, MLA_ROPE, True)
    kc, ku, kd = _rope_tables(positions, MLA_NOPE, MLA_ROPE, False)
    sc, su, sd = _rope_tables(positions, 0, SWA_HEAD_DIM, False)
    mla_tab = {"q": (mc, mu, md), "k": (kc, ku, kd), "q_inv": (mc, -mu, -md), "k_inv": (kc, -ku, -kd)}
    swa_tab = {"f": (sc, su, sd), "inv": (sc, -su, -sd)}
    tabs = (mla_tab, swa_tab)

    layers = []
    for l in range(DEPTH):
        w1, w2, w3 = _w_in_internal(full["w_in"][l].astype(BF16))
        uk, uv = _w_ukv_internal(full["w_ukv"][l].astype(BF16))
        layers.append({
            "w_in1": w1, "w_in2": jnp.concatenate(w2, axis=1), "w_in2_parts": w2, "w_in3": w3,
            "w_uq": _pad_cols(full["w_uq"][l].astype(BF16), MLA_HEADS, MLA_NOPE + MLA_ROPE),
            "w_ukv_k": uk, "w_ukv_v": uv,
            "w_o_mla": _pad_rows(full["w_o_mla"][l].astype(BF16), 8, 64),
            "w_o_swa": _pad_rows(full["w_o_swa"][l].astype(BF16), 8, 64),
            "w_o_sb": _pad_rows(full["w_o_sb"][l].astype(BF16), 8, 64),
            "w_out": full["w_out"][l].astype(BF16), "w_up": full["w_up"][l].astype(BF16),
            "w_down": full["w_down"][l].astype(BF16),
            "g_mix_pre": full["g_mix_pre"][l][None], "b_gate": full["b_gate"][l][None],
            "g_q_lat": full["g_q_lat"][l][None], "g_kv_lat": full["g_kv_lat"][l][None],
            "g_mix_post": full["g_mix_post"][l][None], "g_mlp_pre": full["g_mlp_pre"][l][None],
            "g_mlp_post": full["g_mlp_post"][l][None],
            "sink_b": jnp.repeat(full["swa_sinks"][l], LANES)[None],
        })

    saved = []
    h = x
    for l in range(DEPTH):
        h, sv = _layer_fwd(h, layers[l], tabs)
        saved.append(sv)

    def f_loss(rows, consts):
        err = rows[0] - rows[1]
        return [err * (1.0 / D_MODEL)], [jnp.sum(err * err, axis=1, keepdims=True)]

    dy, sq = _rowwise(f_loss, [h, loss_target], [], [(D_MODEL, F32)], [1], "loss_head")
    loss_part = sq * (0.5 / D_MODEL)

    grads = [None] * DEPTH
    d = dy
    for l in reversed(range(DEPTH)):
        d, gr = _layer_bwd(d, layers[l], saved[l], tabs)
        grads[l] = {
            "g_mix_pre": gr["g_mix_pre"][0], "w_in": _w_in_reference(gr["w_in1"], gr["w_in2"], gr["w_in3"]),
            "b_gate": gr["b_gate"][0], "g_q_lat": gr["g_q_lat"][0], "g_kv_lat": gr["g_kv_lat"][0],
            "w_uq": _unpad_cols(gr["w_uq"], MLA_HEADS, MLA_NOPE + MLA_ROPE),
            "w_ukv": _w_ukv_reference(gr["w_ukv_k"], gr["w_ukv_v"]), "swa_sinks": gr["swa_sinks"],
            "w_o_mla": _unpad_rows(gr["w_o_mla"], 8, 64), "w_o_swa": _unpad_rows(gr["w_o_swa"], 8, 64),
            "w_o_sb": _unpad_rows(gr["w_o_sb"], 8, 64), "w_out": gr["w_out"], "g_mix_post": gr["g_mix_post"][0],
            "g_mlp_pre": gr["g_mlp_pre"][0], "w_up": gr["w_up"], "w_down": gr["w_down"], "g_mlp_post": gr["g_mlp_post"][0],
        }
    stacked = {n: jnp.stack([grads[l][n] for l in range(DEPTH)]) for n in WEIGHTS}
    return loss_part, d, stacked


def _rows_of(a):
    return a.reshape(-1, LANES)


def _small_rows(d):
    parts = []
    for n in SMALL:
        a = d[n]
        if a.shape[1] < LANES:
            a = jnp.pad(a, ((0, 0), (0, LANES - a.shape[1])))
        parts.append(_rows_of(a))
    return parts


def _pack(shards, small, dtype):
    parts = [_rows_of(shards[n]) for n in SHARDED]
    if small is not None:
        parts += _small_rows(small)
    slab = jnp.concatenate(parts, axis=0).astype(dtype)
    pad = (-slab.shape[0]) % SLAB_ROW_ALIGN
    return jnp.pad(slab, ((0, pad), (0, 0)))


def _unpack(slab, shard_shapes, small_shapes):
    out, r = {}, 0
    for n in SHARDED:
        rows = int(np.prod(shard_shapes[n])) // LANES
        out[n] = slab[r:r + rows].reshape(shard_shapes[n])
        r += rows
    if small_shapes is not None:
        for n in SMALL:
            depth, width = small_shapes[n]
            rows = depth * max(width, LANES) // LANES
            out[n] = slab[r:r + rows].reshape(depth, max(width, LANES))[:, :width]
            r += rows
    return out


def _chip_exchange(src, name):
    rows = src.shape[-2]

    def body(src_ref, out_ref, send_sems, recv_sems, local_sem):
        x, y, c = lax.axis_index("x"), lax.axis_index("y"), lax.axis_index("c")
        me = 2 * x + y
        chips = [(1 - x, y), (x, 1 - y), (1 - x, 1 - y)]
        mine = pltpu.make_async_copy(src_ref.at[me], out_ref.at[me], local_sem)
        mine.start()
        sends = []
        for k, (cx, cy) in enumerate(chips):
            cp = pltpu.make_async_remote_copy(
                src_ref=src_ref.at[2 * cx + cy], dst_ref=out_ref.at[me], send_sem=send_sems.at[k],
                recv_sem=recv_sems.at[k], device_id=(cx, cy, c), device_id_type=pl.DeviceIdType.MESH)
            cp.start()
            sends.append(cp)
        for k, (cx, cy) in enumerate(chips):
            pltpu.make_async_remote_copy(
                src_ref=src_ref.at[me], dst_ref=out_ref.at[2 * cx + cy], send_sem=send_sems.at[k],
                recv_sem=recv_sems.at[k], device_id=(cx, cy, c), device_id_type=pl.DeviceIdType.MESH).wait_recv()
        for cp in sends:
            cp.wait_send()
        mine.wait()

    return pl.pallas_call(
        body,
        name=name,
        in_specs=[pl.BlockSpec(memory_space=pl.ANY)],
        out_specs=pl.BlockSpec(memory_space=pl.ANY),
        out_shape=jax.ShapeDtypeStruct((N_CHIPS, rows, LANES), src.dtype),
        scratch_shapes=[pltpu.SemaphoreType.DMA((3,)), pltpu.SemaphoreType.DMA((3,)), pltpu.SemaphoreType.DMA],
    )(src)


def _half_rows(c, half):
    return pl.ds(pl.multiple_of(c * half, SLAB_ROW_ALIGN // 2), half)


def _gather_weights(src, name):
    rows = src.shape[0]
    half = rows // 2

    def body(src_ref, out_ref, send_sems, recv_sems, local_sem):
        x, y, c = lax.axis_index("x"), lax.axis_index("y"), lax.axis_index("c")
        me = 2 * x + y
        chips = [(1 - x, y), (x, 1 - y), (1 - x, 1 - y)]

        def copy(k, src_view, slab, part, to):
            return pltpu.make_async_remote_copy(
                src_ref=src_view, dst_ref=out_ref.at[slab, _half_rows(part, half), :], send_sem=send_sems.at[k],
                recv_sem=recv_sems.at[k], device_id=to, device_id_type=pl.DeviceIdType.MESH)

        mine = pltpu.make_async_copy(src_ref, out_ref.at[me], local_sem)
        mine.start()
        sends = [copy(k, src_ref.at[_half_rows(c, half), :], me, c, (cx, cy, c)) for k, (cx, cy) in enumerate(chips)]
        for cp in sends:
            cp.start()
        for k, (cx, cy) in enumerate(chips):
            j = 2 * cx + cy
            landed = out_ref.at[j, _half_rows(c, half), :]
            copy(k, landed, j, c, (cx, cy, c)).wait_recv()
            fwd = copy(3 + k, landed, j, c, (x, y, 1 - c))
            fwd.start()
            sends.append(fwd)
        for k, (cx, cy) in enumerate(chips):
            j = 2 * cx + cy
            copy(3 + k, out_ref.at[j, _half_rows(1 - c, half), :], j, 1 - c, (x, y, 1 - c)).wait_recv()
        for cp in sends:
            cp.wait_send()
        mine.wait()

    return pl.pallas_call(
        body,
        name=name,
        in_specs=[pl.BlockSpec(memory_space=pl.ANY)],
        out_specs=pl.BlockSpec(memory_space=pl.ANY),
        out_shape=jax.ShapeDtypeStruct((N_CHIPS, rows, LANES), src.dtype),
        scratch_shapes=[pltpu.SemaphoreType.DMA((6,)), pltpu.SemaphoreType.DMA((6,)), pltpu.SemaphoreType.DMA],
    )(src)


def _sibling_halves(src, name):
    n, rows, _ = src.shape
    half = rows // 2

    def body(src_ref, out_ref, send_sem, recv_sem):
        x, y, c = lax.axis_index("x"), lax.axis_index("y"), lax.axis_index("c")
        cp = pltpu.make_async_remote_copy(
            src_ref=src_ref.at[:, _half_rows(1 - c, half), :], dst_ref=out_ref, send_sem=send_sem, recv_sem=recv_sem,
            device_id=(x, y, 1 - c), device_id_type=pl.DeviceIdType.MESH)
        cp.start()
        cp.wait()

    return pl.pallas_call(
        body,
        name=name,
        in_specs=[pl.BlockSpec(memory_space=pl.ANY)],
        out_specs=pl.BlockSpec(memory_space=pl.ANY),
        out_shape=jax.ShapeDtypeStruct((n, half, LANES), src.dtype),
        scratch_shapes=[pltpu.SemaphoreType.DMA, pltpu.SemaphoreType.DMA],
    )(src)


def _sibling_join(src, name):
    half = src.shape[0]

    def body(src_ref, out_ref, send_sem, recv_sem, local_sem):
        x, y, c = lax.axis_index("x"), lax.axis_index("y"), lax.axis_index("c")
        mine = pltpu.make_async_copy(src_ref, out_ref.at[_half_rows(c, half), :], local_sem)
        mine.start()
        cp = pltpu.make_async_remote_copy(
            src_ref=src_ref, dst_ref=out_ref.at[_half_rows(c, half), :], send_sem=send_sem, recv_sem=recv_sem,
            device_id=(x, y, 1 - c), device_id_type=pl.DeviceIdType.MESH)
        cp.start()
        pltpu.make_async_remote_copy(
            src_ref=src_ref, dst_ref=out_ref.at[_half_rows(1 - c, half), :], send_sem=send_sem, recv_sem=recv_sem,
            device_id=(x, y, 1 - c), device_id_type=pl.DeviceIdType.MESH).wait_recv()
        cp.wait_send()
        mine.wait()

    return pl.pallas_call(
        body,
        name=name,
        in_specs=[pl.BlockSpec(memory_space=pl.ANY)],
        out_specs=pl.BlockSpec(memory_space=pl.ANY),
        out_shape=jax.ShapeDtypeStruct((2 * half, LANES), src.dtype),
        scratch_shapes=[pltpu.SemaphoreType.DMA, pltpu.SemaphoreType.DMA, pltpu.SemaphoreType.DMA],
    )(src)


SUM_ROWS = 1024


def _pair_sum(mine, theirs, c, name):
    n, half, _ = theirs.shape
    blocks = half // SUM_ROWS

    def body(c_ref, a_ref, b_ref, o_ref):
        o_ref[...] = (a_ref[...] + b_ref[...]).astype(o_ref.dtype)

    return pl.pallas_call(
        body,
        name=name,
        grid_spec=pltpu.PrefetchScalarGridSpec(
            num_scalar_prefetch=1,
            grid=(blocks,),
            in_specs=[pl.BlockSpec((n, SUM_ROWS, LANES), lambda i, c_ref: (0, c_ref[0] * blocks + i, 0)),
                      pl.BlockSpec((n, SUM_ROWS, LANES), lambda i, c_ref: (0, i, 0))],
            out_specs=pl.BlockSpec((n, SUM_ROWS, LANES), lambda i, c_ref: (0, i, 0)),
        ),
        out_shape=jax.ShapeDtypeStruct((n, half, LANES), BF16),
        compiler_params=_params(("arbitrary",)),
    )(jnp.reshape(c, (1,)).astype(jnp.int32), mine, theirs)


def _sum_chips(buf, name):
    rows = buf.shape[1]

    def body(b_ref, o_ref):
        t = [b_ref[j].astype(F32) for j in range(N_CHIPS)]
        o_ref[...] = ((t[0] + t[1]) + t[2]) + t[3]

    return pl.pallas_call(
        body,
        name=name,
        grid=(rows // SUM_ROWS,),
        in_specs=[pl.BlockSpec((N_CHIPS, SUM_ROWS, LANES), lambda i: (0, i, 0))],
        out_specs=pl.BlockSpec((SUM_ROWS, LANES), lambda i: (i, 0)),
        out_shape=jax.ShapeDtypeStruct((rows, LANES), F32),
        compiler_params=_params(("arbitrary",)),
    )(buf)


def _adamw(w, m, v, g, name):
    shape = w.shape
    flat = lambda a: a.reshape(-1, shape[-1])

    def fn(rows, consts):
        w_, m_, v_, g_ = rows
        m_new = ADAM_B1 * m_ + (1.0 - ADAM_B1) * g_
        v_new = ADAM_B2 * v_ + (1.0 - ADAM_B2) * (g_ * g_)
        m_hat = m_new / (1.0 - ADAM_B1 ** ADAM_STEP)
        v_hat = v_new / (1.0 - ADAM_B2 ** ADAM_STEP)
        delta = -ADAM_LR * (m_hat / (jnp.sqrt(v_hat) + ADAM_EPS) + ADAM_WD * w_)
        return [delta, m_new, v_new], []

    outs = _rowwise(fn, [flat(w), flat(m), flat(v), flat(g)], [], [(shape[-1], F32)] * 3, [], name, bm=256)
    return [o.reshape(shape) for o in outs]


def kernel(x, positions, g_mix_pre, w_in, b_gate, g_q_lat, g_kv_lat, w_uq, w_ukv, swa_sinks, w_o_mla, w_o_swa, w_o_sb, w_out, g_mix_post, g_mlp_pre, w_up, w_down, g_mlp_post, loss_target, m_g_mix_pre, m_w_in, m_b_gate, m_g_q_lat, m_g_kv_lat, m_w_uq, m_w_ukv, m_swa_sinks, m_w_o_mla, m_w_o_swa, m_w_o_sb, m_w_out, m_g_mix_post, m_g_mlp_pre, m_w_up, m_w_down, m_g_mlp_post, v_g_mix_pre, v_w_in, v_b_gate, v_g_q_lat, v_g_kv_lat, v_w_uq, v_w_ukv, v_swa_sinks, v_w_o_mla, v_w_o_swa, v_w_o_sb, v_w_out, v_g_mix_post, v_g_mlp_pre, v_w_up, v_w_down, v_g_mlp_post):
    given = dict(locals())
    wts = {n: given[n] for n in WEIGHTS}
    mom_m = {n: given["m_" + n] for n in WEIGHTS}
    mom_v = {n: given["v_" + n] for n in WEIGHTS}
    shard_shapes = {n: wts[n].shape for n in SHARDED}
    small_shapes = {n: wts[n].shape for n in SMALL}

    gathered = _gather_weights(_pack(wts, None, BF16), "gather_weights")
    full = {n: wts[n] for n in SMALL}
    per_chip = [_unpack(gathered[j], shard_shapes, None) for j in range(N_CHIPS)]
    for n in SHARDED:
        full[n] = jnp.concatenate([per_chip[j][n] for j in range(N_CHIPS)], axis=SHARD_AXIS[n])

    loss_part, grad_x, grads = _local_step(x[0], positions[0], loss_target[0], full)
    loss = lax.psum(loss_part[0, 0], ("x", "y", "c"))

    small_g = {n: grads[n] for n in SMALL}
    slabs = []
    for j in range(N_CHIPS):
        shard = {n: jnp.split(grads[n], N_CHIPS, axis=SHARD_AXIS[n])[j] for n in SHARDED}
        slabs.append(_pack(shard, small_g, F32))
    per_chip_g = jnp.stack(slabs)
    theirs = _sibling_halves(per_chip_g, "pair_grads")
    pair = _pair_sum(per_chip_g, theirs, lax.axis_index("c"), "sum_pair")
    landed = _chip_exchange(pair, "scatter_grads")
    g_slab = _sibling_join(_sum_chips(landed, "sum_chips"), "join_grads")

    g = _unpack(g_slab, shard_shapes, small_shapes)
    stepped = {n: _adamw(wts[n], mom_m[n], mom_v[n], g[n], "adamw_" + n) for n in WEIGHTS}
    outs = [loss, grad_x[None]] + [g[n] for n in WEIGHTS]
    for part in range(3):
        outs += [stepped[n][part] for n in WEIGHTS]
    return tuple(outs)
```

```python
import numpy as np
import jax
import jax.numpy as jnp
from jax import lax
from jax.experimental import pallas as pl
from jax.experimental.pallas import tpu as pltpu

F32 = jnp.float32
BF16 = jnp.bfloat16

D_MODEL = 1024
DEPTH = 4
MLA_HEADS, MLA_Q_LORA, MLA_KV_LORA, MLA_NOPE, MLA_ROPE, MLA_V = 8, 256, 128, 64, 32, 64
SWA_HEADS, SWA_KV_HEADS, SWA_HEAD_DIM, SWA_WINDOW = 8, 2, 64, 128
SB_HEADS, SB_HEAD_DIM = 8, 64
D_FF = 4 * D_MODEL
ROPE_THETA = 10000.0
EPS = 1e-6
SPLIT_SIZES = (256, 128, 32, 512, 128, 128, 512, 512, 512, 3 * D_MODEL)
SPLIT_POINTS = [int(v) for v in np.cumsum(SPLIT_SIZES)[:-1]]

ADAM_LR, ADAM_B1, ADAM_B2, ADAM_EPS, ADAM_WD, ADAM_STEP = 0.001, 0.9, 0.999, 1e-08, 0.01, 10

LANES = 128
V7X_VMEM_BYTES = 64 * 1024 * 1024
VMEM_LIMIT = V7X_VMEM_BYTES - 8 * 1024 * 1024
MATMUL_VMEM_BUDGET = 36 * 1024 * 1024
N_CHIPS = 4
SLAB_ROW_ALIGN = 512

P1_W = 256 + 128 + 128 + 1024 + 256
P2_W = 256 + 1024 + 1024 + 1024
P3_W = 3 * D_MODEL

SHARDED = ("w_in", "w_uq", "w_ukv", "w_o_mla", "w_o_swa", "w_o_sb", "w_out", "w_up", "w_down")
SHARD_AXIS = {"w_in": 2, "w_uq": 2, "w_ukv": 2, "w_o_mla": 2, "w_o_swa": 2, "w_o_sb": 2, "w_out": 1, "w_up": 2, "w_down": 1}
SMALL = ("g_mix_pre", "b_gate", "g_q_lat", "g_kv_lat", "swa_sinks", "g_mix_post", "g_mlp_pre", "g_mlp_post")
WEIGHTS = ("g_mix_pre", "w_in", "b_gate", "g_q_lat", "g_kv_lat", "w_uq", "w_ukv", "swa_sinks", "w_o_mla", "w_o_swa",
           "w_o_sb", "w_out", "g_mix_post", "g_mlp_pre", "w_up", "w_down", "g_mlp_post")

NN = (((1,), (0,)), ((), ()))
NT = (((1,), (1,)), ((), ()))
TN = (((0,), (0,)), ((), ()))


def _dot(a, b, dims):
    return lax.dot_general(a, b, dims, preferred_element_type=F32)


def _params(sem):
    return pltpu.CompilerParams(dimension_semantics=sem, vmem_limit_bytes=VMEM_LIMIT)


def _largest_tile(n, cap):
    if n <= cap:
        return n
    best = LANES
    for t in range(LANES, cap + 1, LANES):
        if n % t == 0:
            best = t
    return best


def _matmul_tiles(M, N, K, a_bytes, b_bytes, out_bytes, extra_bytes):
    tn = _largest_tile(N, 1792)
    tm = _largest_tile(M, 1024 if tn <= 1024 else 512)
    tk = _largest_tile(K, 2048)

    def need(tm_, tk_):
        acc = 4 * tm_ * tn if tk_ < K else 0
        return 2 * (tm_ * tk_ * a_bytes + tk_ * tn * b_bytes + tm_ * tn * (out_bytes + extra_bytes)) + acc

    while need(tm, tk) > MATMUL_VMEM_BUDGET:
        if tk >= tm and tk % 256 == 0:
            tk //= 2
        elif tm % 256 == 0:
            tm //= 2
        else:
            break
    return tm, tn, tk


def _matmul(a, b, mode, out_dtypes, name, epilogue=None, extras=(), row_extras=()):
    if mode == "nn":
        (M, K), (K2, N) = a.shape, b.shape
    elif mode == "nt":
        (M, K), (N, K2) = a.shape, b.shape
    else:
        (K, M), (K2, N) = a.shape, b.shape
    assert K == K2, (name, a.shape, b.shape)
    tm, tn, tk = _matmul_tiles(
        M, N, K, a.dtype.itemsize, b.dtype.itemsize, sum(jnp.dtype(d).itemsize for d in out_dtypes),
        sum(e.dtype.itemsize for e in extras))
    assert M % tm == 0 and N % tn == 0 and K % tk == 0, (name, M, N, K, tm, tn, tk)
    nk = K // tk
    if mode == "tn":
        a_spec = pl.BlockSpec((tk, tm), lambda i, j, k: (k, i))
    else:
        a_spec = pl.BlockSpec((tm, tk), lambda i, j, k: (i, k))
    if mode == "nt":
        b_spec = pl.BlockSpec((tn, tk), lambda i, j, k: (j, k))
    else:
        b_spec = pl.BlockSpec((tk, tn), lambda i, j, k: (k, j))
    dims = {"nn": NN, "nt": NT, "tn": TN}[mode]
    n_ex, n_rex, n_out = len(extras), len(row_extras), len(out_dtypes)

    def body(*refs):
        a_ref, b_ref = refs[:2]
        ex = refs[2:2 + n_ex]
        rex = refs[2 + n_ex:2 + n_ex + n_rex]
        outs = refs[2 + n_ex + n_rex:2 + n_ex + n_rex + n_out]

        def finish(total):
            res = (total,) if epilogue is None else epilogue(total, *[e[...] for e in ex], *[e[...] for e in rex])
            for o, r in zip(outs, res):
                o[...] = r.astype(o.dtype)

        part = _dot(a_ref[...].astype(BF16), b_ref[...].astype(BF16), dims)
        if nk == 1:
            finish(part)
            return
        acc = refs[-1]
        k = pl.program_id(2)

        @pl.when(k == 0)
        def _():
            acc[...] = part

        @pl.when(k > 0)
        def _():
            acc[...] += part

        @pl.when(k == nk - 1)
        def _():
            finish(acc[...])

    in_specs = [a_spec, b_spec]
    in_specs += [pl.BlockSpec((tm, tn), lambda i, j, k: (i, j)) for _ in extras]
    in_specs += [pl.BlockSpec((1, tn), lambda i, j, k: (0, j)) for _ in row_extras]
    out = pl.pallas_call(
        body,
        name=name,
        grid=(M // tm, N // tn, nk),
        in_specs=in_specs,
        out_specs=[pl.BlockSpec((tm, tn), lambda i, j, k: (i, j)) for _ in out_dtypes],
        out_shape=[jax.ShapeDtypeStruct((M, N), dt) for dt in out_dtypes],
        scratch_shapes=[pltpu.VMEM((tm, tn), F32)] if nk > 1 else [],
        compiler_params=_params(("parallel", "parallel", "arbitrary")),
    )(a, b, *extras, *row_extras)
    return out[0] if n_out == 1 else out


def _rowwise(fn, rows, consts, out_defs, sum_widths, name, bm=256):
    R = rows[0].shape[0]
    bm = min(bm, R)
    assert R % bm == 0, (name, R, bm)
    n_r, n_c, n_o = len(rows), len(consts), len(out_defs)
    n_s = len(sum_widths)

    def body(*refs):
        r_in = refs[:n_r]
        c_in = refs[n_r:n_r + n_c]
        o_refs = refs[n_r + n_c:n_r + n_c + n_o]
        s_refs = refs[n_r + n_c + n_o:]
        outs, sums = fn([r[...] for r in r_in], [c[...] for c in c_in])
        for o, val in zip(o_refs, outs):
            o[...] = val.astype(o.dtype)
        if n_s:
            @pl.when(pl.program_id(0) == 0)
            def _():
                for s in s_refs:
                    s[...] = jnp.zeros_like(s)

            for s, val in zip(s_refs, sums):
                s[...] += jnp.sum(val, axis=0, keepdims=True)

    in_specs = [pl.BlockSpec((bm, r.shape[1]), lambda i: (i, 0)) for r in rows]
    in_specs += [pl.BlockSpec(c.shape, lambda i: (0, 0)) for c in consts]
    out_specs = [pl.BlockSpec((bm, w), lambda i: (i, 0)) for w, _ in out_defs]
    out_specs += [pl.BlockSpec((1, w), lambda i: (0, 0)) for w in sum_widths]
    out_shape = [jax.ShapeDtypeStruct((R, w), dt) for w, dt in out_defs]
    out_shape += [jax.ShapeDtypeStruct((1, w), F32) for w in sum_widths]
    return pl.pallas_call(
        body,
        name=name,
        grid=(R // bm,),
        in_specs=in_specs,
        out_specs=out_specs,
        out_shape=out_shape,
        compiler_params=_params(("arbitrary",)),
    )(*rows, *consts)


def _rms(x, g):
    r = lax.rsqrt(jnp.mean(x * x, axis=-1, keepdims=True) + EPS)
    return x * r * g


def _rms_bwd(x, g, dy):
    r = lax.rsqrt(jnp.mean(x * x, axis=-1, keepdims=True) + EPS)
    n = x * r
    dn = dy * g
    dx = r * (dn - n * jnp.mean(dn * n, axis=-1, keepdims=True))
    return dx, dy * n


def _rope(x, c, s_up, s_dn, half):
    return x * c + pltpu.roll(x, half, 1) * s_up + pltpu.roll(x, LANES - half, 1) * s_dn


def _rope_tables(positions, lo, d, nope_pass):
    S = positions.shape[0]
    half = d // 2
    inv = 1.0 / (ROPE_THETA ** (jnp.arange(0, d, 2, dtype=F32) / d))
    ang = positions.astype(F32)[:, None] * inv
    cos, sin = jnp.cos(ang), jnp.sin(ang)
    z = lambda n: jnp.zeros((S, n), F32)
    head = jnp.ones((S, lo), F32) if nope_pass else z(lo)
    tail = LANES - lo - d
    c = jnp.concatenate([head, cos, cos, z(tail)], axis=1)
    s_up = jnp.concatenate([z(lo), z(half), sin, z(tail)], axis=1)
    s_dn = jnp.concatenate([z(lo), -sin, z(half), z(tail)], axis=1)
    return c, s_up, s_dn


MLA_FWD_CFG = (1, 1024)
MLA_BWD_CFG = (2, 512)
SB_FWD_CFG = (2, 256)
SB_BWD_CFG = (2, 256)


def _tile_mask(bk, strict):
    row = lax.broadcasted_iota(jnp.int32, (bk, bk), 0)
    col = lax.broadcasted_iota(jnp.int32, (bk, bk), 1)
    return (col < row) if strict else (col <= row)


def _att_layout(S, cfg):
    nch, bk = cfg
    bq = nch * bk
    assert S % bq == 0, (S, cfg)
    rows = [slice(r * bk, (r + 1) * bk) for r in range(nch)]
    q_spec = lambda off=0: pl.BlockSpec((bq, LANES), lambda h, i: (i, off + h))
    kv_spec = lambda off=0: pl.BlockSpec((S, LANES), lambda h, i: (0, off + h))
    return bq, rows, q_spec, kv_spec


def _total(terms):
    terms = list(terms)
    out = terms[0]
    for t in terms[1:]:
        out = out + t
    return out


def _walk(nch, i, step, carry, leftward, alive=None):
    everyone = range(nch)
    if leftward:
        for d in reversed(everyone):
            carry = step(nch * i + d, carry, range(d, nch), {d})
        if alive is None:
            return lax.fori_loop(0, nch * i, lambda t, c: step(nch * i - 1 - t, c, everyone, set()), carry)
        more = lambda tc: jnp.logical_and(tc[0] < nch * i, alive(tc[1]))
        left = lambda tc: (tc[0] + 1, step(nch * i - 1 - tc[0], tc[1], everyone, set()))
        return lax.while_loop(more, left, (jnp.int32(0), carry))[1]
    carry = lax.fori_loop(0, nch * i, lambda kb, c: step(kb, c, everyone, set()), carry)
    for d in everyone:
        carry = step(nch * i + d, carry, range(d, nch), {d})
    return carry


ONES_LANE = MLA_V


def _softmax_attn_fwd(q, k, v, heads, name, q_off=0, k_off=0, v_off=0):
    S = q.shape[0]
    nch, bk = MLA_FWD_CFG
    bq, rows, q_spec, kv_spec = _att_layout(S, MLA_FWD_CFG)

    def body(q_ref, k_ref, v_ref, o_ref, lse_ref):
        i = pl.program_id(1)
        qs = [q_ref[rw, :] for rw in rows]

        def step(kb, cs, active, masked):
            off = pl.multiple_of(kb * bk, bk)
            ks, vs = k_ref[pl.ds(off, bk), :], v_ref[pl.ds(off, bk), :]
            A = list(active)
            s = {r: _dot(qs[r], ks, NT) for r in A}
            s = {r: (jnp.where(_tile_mask(bk, False), s[r], -1e30) if r in masked else s[r]) for r in A}
            m_new = {r: jnp.maximum(cs[r][0], jnp.max(s[r], axis=1, keepdims=True)) for r in A}
            p = {r: jnp.exp(s[r] - m_new[r]) for r in A}
            alpha = {r: jnp.exp(cs[r][0] - m_new[r]) for r in A}
            new = list(cs)
            for r in A:
                new[r] = (m_new[r], alpha[r] * cs[r][1] + _dot(p[r].astype(BF16), vs, NN))
            return tuple(new)

        init = (jnp.full((bk, 1), -1e30, F32), jnp.zeros((bk, LANES), F32))
        cs = _walk(nch, i, step, tuple(init for _ in rows), False)
        for r, (m, acc) in enumerate(cs):
            l = acc[:, ONES_LANE:ONES_LANE + 1]
            o_ref[rows[r], :] = (acc / l).astype(o_ref.dtype)
            lse_ref[rows[r], :] = m + jnp.log(l)

    return pl.pallas_call(
        body,
        name=name,
        grid=(heads, S // bq),
        in_specs=[q_spec(q_off), kv_spec(k_off), kv_spec(v_off)],
        out_specs=[q_spec(), pl.BlockSpec((None, bq, 1), lambda h, i: (h, i, 0))],
        out_shape=[jax.ShapeDtypeStruct((S, heads * LANES), BF16), jax.ShapeDtypeStruct((heads, S, 1), F32)],
        compiler_params=_params(("parallel", "arbitrary")),
    )(q, k, v)


def _softmax_attn_bwd(q, k, v, o, lse, do, heads, scale, name, q_off=0, k_off=0, v_off=0):
    S = q.shape[0]
    nch, bk = MLA_BWD_CFG
    bq, rows, q_spec, kv_spec = _att_layout(S, MLA_BWD_CFG)

    def body(q_ref, k_ref, v_ref, o_ref, lse_ref, do_ref, dq_ref, dk_ref, dv_ref):
        i = pl.program_id(1)

        @pl.when(i == 0)
        def _():
            dk_ref[...] = jnp.zeros_like(dk_ref)
            dv_ref[...] = jnp.zeros_like(dv_ref)

        qs = [q_ref[rw, :] for rw in rows]
        dos = [do_ref[rw, :] for rw in rows]
        lses = [lse_ref[rw, :] for rw in rows]
        deltas = [jnp.sum(dos[r].astype(F32) * o_ref[rows[r], :].astype(F32), axis=1, keepdims=True) for r in range(nch)]

        def step(kb, dqs, active, masked):
            off = pl.multiple_of(kb * bk, bk)
            ks, vs = k_ref[pl.ds(off, bk), :], v_ref[pl.ds(off, bk), :]
            A = list(active)
            s = {r: _dot(qs[r], ks, NT) for r in A}
            s = {r: (jnp.where(_tile_mask(bk, False), s[r], -1e30) if r in masked else s[r]) for r in A}
            p = {r: jnp.exp(s[r] - lses[r]) for r in A}
            dp = {r: _dot(dos[r], vs, NT) for r in A}
            ds = {r: (p[r] * (dp[r] - deltas[r])).astype(BF16) for r in A}
            dv_c = _total(_dot(p[r].astype(BF16), dos[r], TN) for r in A)
            dk_c = _total(_dot(ds[r], qs[r], TN) for r in A)
            dk_ref[pl.ds(off, bk), :] += dk_c
            dv_ref[pl.ds(off, bk), :] += dv_c
            new = list(dqs)
            for r in A:
                new[r] = dqs[r] + _dot(ds[r], ks, NN)
            return tuple(new)

        dqs = _walk(nch, i, step, tuple(jnp.zeros((bk, LANES), F32) for _ in rows), False)
        for r in range(nch):
            dq_ref[rows[r], :] = dqs[r] * scale

    return pl.pallas_call(
        body,
        name=name,
        grid=(heads, S // bq),
        in_specs=[q_spec(q_off), kv_spec(k_off), kv_spec(v_off), q_spec(),
                  pl.BlockSpec((None, bq, 1), lambda h, i: (h, i, 0)), q_spec()],
        out_specs=[q_spec(), kv_spec(), kv_spec()],
        out_shape=[jax.ShapeDtypeStruct((S, heads * LANES), F32)] * 3,
        compiler_params=_params(("parallel", "arbitrary")),
    )(q, k, v, o, lse, do)


def _tri(n, inclusive):
    r = lax.broadcasted_iota(jnp.int32, (n, n), 0)
    c = lax.broadcasted_iota(jnp.int32, (n, n), 1)
    return jnp.where((r >= c) if inclusive else (r > c), 1.0, 0.0).astype(BF16)


def _suffix_sum(x, tri):
    hi = x.astype(BF16)
    lo = (x - hi.astype(F32)).astype(BF16)
    return _dot(hi, tri, NN) + _dot(lo, tri, NN)


def _sb_logs(z):
    lg = jnp.log(1.0 + jnp.exp(-jnp.abs(z)))
    l1m = -(jnp.maximum(z, 0.0) + lg)
    return l1m, l1m + z


SB_SCALE = SB_HEAD_DIM ** -0.5
assert SB_SCALE == 0.125
SB_DEAD = -110.0


def _sb_alive(cs):
    top = cs[0][0]
    for c in cs[1:]:
        top = jnp.maximum(top, c[0])
    return jnp.max(top) > SB_DEAD


def _sb_attn_fwd(qkv, heads, name, q_off, k_off, v_off):
    S = qkv.shape[0]
    nch, bk = SB_FWD_CFG
    bq, rows, q_spec, kv_spec = _att_layout(S, SB_FWD_CFG)

    def body(q_ref, k_ref, v_ref, o_ref):
        i = pl.program_id(1)
        qs = [q_ref[rw, :] * SB_SCALE for rw in rows]
        tri = _tri(bk, False)

        def step(kb, cs, active, masked):
            off = pl.multiple_of(kb * bk, bk)
            ks, vs = k_ref[pl.ds(off, bk), :], v_ref[pl.ds(off, bk), :]
            A = list(active)
            lg = {r: _sb_logs(_dot(qs[r], ks, NT)) for r in A}
            l1m = {r: (jnp.where(_tile_mask(bk, True), lg[r][0], 0.0) if r in masked else lg[r][0]) for r in A}
            suf = {r: _suffix_sum(l1m[r], tri) for r in A}
            ex = {r: lg[r][1] + cs[r][0] + suf[r] for r in A}
            ex = {r: (jnp.where(_tile_mask(bk, True), ex[r], -1e30) if r in masked else ex[r]) for r in A}
            ab = {r: jnp.exp(ex[r]).astype(BF16) for r in A}
            new = list(cs)
            for r in A:
                new[r] = (cs[r][0] + jnp.sum(l1m[r], axis=1, keepdims=True), cs[r][1] + _dot(ab[r], vs, NN))
            return tuple(new)

        init = (jnp.zeros((bk, 1), F32), jnp.zeros((bk, LANES), F32))
        cs = _walk(nch, i, step, tuple(init for _ in rows), True, _sb_alive)
        for r in range(nch):
            o_ref[rows[r], :] = cs[r][1]

    return pl.pallas_call(
        body,
        name=name,
        grid=(heads, S // bq),
        in_specs=[q_spec(q_off), kv_spec(k_off), kv_spec(v_off)],
        out_specs=q_spec(),
        out_shape=jax.ShapeDtypeStruct((S, heads * LANES), F32),
        compiler_params=_params(("parallel", "arbitrary")),
    )(qkv, qkv, qkv)


def _sb_attn_bwd(qkv, o, do, heads, name, q_off, k_off, v_off):
    S = qkv.shape[0]
    nch, bk = SB_BWD_CFG
    bq, rows, q_spec, kv_spec = _att_layout(S, SB_BWD_CFG)

    def body(q_ref, k_ref, v_ref, o_ref, do_ref, dq_ref, dk_ref, dv_ref):
        i = pl.program_id(1)

        @pl.when(i == 0)
        def _():
            dk_ref[...] = jnp.zeros_like(dk_ref)
            dv_ref[...] = jnp.zeros_like(dv_ref)

        tri = _tri(bk, False)
        qs = [q_ref[rw, :] * SB_SCALE for rw in rows]
        dos = [do_ref[rw, :] for rw in rows]
        deltas = [jnp.sum(dos[r].astype(F32) * o_ref[rows[r], :], axis=1, keepdims=True) for r in range(nch)]

        def step(kb, cs, active, masked):
            off = pl.multiple_of(kb * bk, bk)
            ks, vs = k_ref[pl.ds(off, bk), :], v_ref[pl.ds(off, bk), :]
            A = list(active)
            lg = {r: _sb_logs(_dot(qs[r], ks, NT)) for r in A}
            l1m = {r: (jnp.where(_tile_mask(bk, True), lg[r][0], 0.0) if r in masked else lg[r][0]) for r in A}
            suf = {r: _suffix_sum(l1m[r], tri) for r in A}
            ex = {r: lg[r][1] + cs[r][0] + suf[r] for r in A}
            ex = {r: (jnp.where(_tile_mask(bk, True), ex[r], -1e30) if r in masked else ex[r]) for r in A}
            ab = {r: jnp.exp(ex[r]).astype(BF16) for r in A}
            da = {r: _dot(dos[r], vs, NT) for r in A}
            g = {r: ab[r].astype(F32) * da[r] for r in A}
            gs = {r: _suffix_sum(g[r], tri) for r in A}
            beta = {r: jnp.exp(lg[r][1]) for r in A}
            dz = {r: g[r] - beta[r] * (deltas[r] - cs[r][1] - gs[r]) for r in A}
            dz = {r: (jnp.where(_tile_mask(bk, True), dz[r], 0.0) if r in masked else dz[r]) for r in A}
            dzb = {r: dz[r].astype(BF16) for r in A}
            dv_c = _total(_dot(ab[r], dos[r], TN) for r in A)
            dk_c = _total(_dot(dzb[r], qs[r], TN) for r in A)
            dk_ref[pl.ds(off, bk), :] += dk_c
            dv_ref[pl.ds(off, bk), :] += dv_c
            new = list(cs)
            for r in A:
                new[r] = (cs[r][0] + jnp.sum(l1m[r], axis=1, keepdims=True),
                          cs[r][1] + jnp.sum(g[r], axis=1, keepdims=True), cs[r][2] + _dot(dzb[r], ks, NN))
            return tuple(new)

        zcol = jnp.zeros((bk, 1), F32)
        init = (zcol, zcol, jnp.zeros((bk, LANES), F32))
        cs = _walk(nch, i, step, tuple(init for _ in rows), True, _sb_alive)
        for r in range(nch):
            dq_ref[rows[r], :] = cs[r][2] * SB_SCALE

    return pl.pallas_call(
        body,
        name=name,
        grid=(heads, S // bq),
        in_specs=[q_spec(q_off), kv_spec(k_off), kv_spec(v_off), q_spec(), q_spec()],
        out_specs=[q_spec(), kv_spec(), kv_spec()],
        out_shape=[jax.ShapeDtypeStruct((S, heads * LANES), F32)] * 3,
        compiler_params=_params(("parallel", "arbitrary")),
    )(qkv, qkv, qkv, o, do)


SWA_BLK = 128
SWA_GROUP = SWA_HEADS // SWA_KV_HEADS


SWA_NB = 4
SWA_ROWS = SWA_NB * SWA_BLK


def _swa_band_mask(first):
    row = lax.broadcasted_iota(jnp.int32, (SWA_BLK, 2 * SWA_BLK), 0)
    col = lax.broadcasted_iota(jnp.int32, (SWA_BLK, 2 * SWA_BLK), 1)
    return (col > row) & (col <= row + SWA_WINDOW) & (jnp.logical_not(first) | (col >= SWA_BLK))


def _swa_in_specs(v_off):
    gw = SWA_GROUP * LANES
    before = lambda h, n: (jnp.maximum(SWA_NB * n - 1, 0), h)
    return [
        pl.BlockSpec((SWA_ROWS, gw), lambda h, n: (n, h)),
        pl.BlockSpec((SWA_BLK, LANES), before),
        pl.BlockSpec((SWA_ROWS, LANES), lambda h, n: (n, h)),
        pl.BlockSpec((SWA_BLK, LANES), lambda h, n: (jnp.maximum(SWA_NB * n - 1, 0), v_off + h)),
        pl.BlockSpec((SWA_ROWS, LANES), lambda h, n: (n, v_off + h)),
        pl.BlockSpec((1, gw), lambda h, n: (0, h)),
    ]


def _swa_bands(n, kp_ref, kc_ref, vp_ref, vc_ref):
    k_all = jnp.concatenate([kp_ref[...], kc_ref[...]], axis=0)
    v_all = jnp.concatenate([vp_ref[...], vc_ref[...]], axis=0)
    bands = []
    for j in range(SWA_NB):
        rows = slice(j * SWA_BLK, (j + 2) * SWA_BLK)
        bands.append((k_all[rows], v_all[rows], _swa_band_mask((n == 0) if j == 0 else False)))
    return bands


def _swa_fwd(q, k, v, v_off, sink_b, name):
    S = q.shape[0]
    assert S % SWA_ROWS == 0
    scale = SWA_HEAD_DIM ** -0.5
    gw = SWA_GROUP * LANES

    def body(q_ref, kp_ref, kc_ref, vp_ref, vc_ref, sink_ref, o_ref, lse_ref):
        n = pl.program_id(1)
        bands = _swa_bands(n, kp_ref, kc_ref, vp_ref, vc_ref)
        P = [(j, g) for j in range(SWA_NB) for g in range(SWA_GROUP)]
        rows = lambda j: slice(j * SWA_BLK, (j + 1) * SWA_BLK)
        lanes = lambda g: slice(g * LANES, (g + 1) * LANES)
        sk = {g: sink_ref[:, g * LANES:g * LANES + 1] for g in range(SWA_GROUP)}
        s = {(j, g): jnp.where(bands[j][2], _dot(q_ref[rows(j), lanes(g)], bands[j][0], NT) * scale, -1e30) for j, g in P}
        m = {(j, g): jnp.maximum(jnp.max(s[j, g], axis=1, keepdims=True), sk[g]) for j, g in P}
        p = {(j, g): jnp.exp(s[j, g] - m[j, g]) for j, g in P}
        den = {(j, g): jnp.sum(p[j, g], axis=1, keepdims=True) + jnp.exp(sk[g] - m[j, g]) for j, g in P}
        for j, g in P:
            o_ref[rows(j), lanes(g)] = _dot((p[j, g] / den[j, g]).astype(BF16), bands[j][1], NN).astype(o_ref.dtype)
            lse_ref[g, rows(j), :] = m[j, g] + jnp.log(den[j, g])

    return pl.pallas_call(
        body,
        name=name,
        grid=(SWA_KV_HEADS, S // SWA_ROWS),
        in_specs=_swa_in_specs(v_off),
        out_specs=[
            pl.BlockSpec((SWA_ROWS, gw), lambda h, n: (n, h)),
            pl.BlockSpec((SWA_GROUP, SWA_ROWS, 1), lambda h, n: (h, n, 0)),
        ],
        out_shape=[jax.ShapeDtypeStruct((S, SWA_HEADS * LANES), BF16), jax.ShapeDtypeStruct((SWA_HEADS, S, 1), F32)],
        compiler_params=_params(("parallel", "arbitrary")),
    )(q, k, k, v, v, sink_b)


def _swa_bwd(q, k, v, v_off, sink_b, o, lse, do, name):
    S = q.shape[0]
    assert S % SWA_ROWS == 0
    scale = SWA_HEAD_DIM ** -0.5
    gw = SWA_GROUP * LANES

    def body(q_ref, kp_ref, kc_ref, vp_ref, vc_ref, sink_ref, o_ref, lse_ref, do_ref, dq_ref, dk_ref, dv_ref, dsink_ref):
        n = pl.program_id(1)

        @pl.when(n == 0)
        def _():
            dk_ref[...] = jnp.zeros_like(dk_ref)
            dv_ref[...] = jnp.zeros_like(dv_ref)
            dsink_ref[...] = jnp.zeros_like(dsink_ref)

        bands = _swa_bands(n, kp_ref, kc_ref, vp_ref, vc_ref)
        P = [(j, g) for j in range(SWA_NB) for g in range(SWA_GROUP)]
        rows = lambda j: slice(j * SWA_BLK, (j + 1) * SWA_BLK)
        lanes = lambda g: slice(g * LANES, (g + 1) * LANES)
        qs = {(j, g): q_ref[rows(j), lanes(g)] for j, g in P}
        dos = {(j, g): do_ref[rows(j), lanes(g)] for j, g in P}
        lses = {(j, g): lse_ref[g, rows(j), :] for j, g in P}
        delta = {(j, g): jnp.sum(dos[j, g].astype(F32) * o_ref[rows(j), lanes(g)].astype(F32), axis=1, keepdims=True)
                 for j, g in P}
        s = {(j, g): jnp.where(bands[j][2], _dot(qs[j, g], bands[j][0], NT) * scale, -1e30) for j, g in P}
        p = {(j, g): jnp.exp(s[j, g] - lses[j, g]) for j, g in P}
        dp = {(j, g): _dot(dos[j, g], bands[j][1], NT) for j, g in P}
        ds = {(j, g): (p[j, g] * (dp[j, g] - delta[j, g]) * scale).astype(BF16) for j, g in P}
        for j, g in P:
            dq_ref[rows(j), lanes(g)] = _dot(ds[j, g], bands[j][0], NN)
        for g in range(SWA_GROUP):
            p_sink = [jnp.exp(sink_ref[:, g * LANES:g * LANES + 1] - lses[j, g]) * delta[j, g] for j in range(SWA_NB)]
            dsink_ref[:, lanes(g)] += jnp.zeros((1, LANES), F32) - jnp.sum(_total(p_sink), axis=0, keepdims=True)
        dkb = [_total(_dot(ds[j, g], qs[j, g], TN) for g in range(SWA_GROUP)) for j in range(SWA_NB)]
        dvb = [_total(_dot(p[j, g].astype(BF16), dos[j, g], TN) for g in range(SWA_GROUP)) for j in range(SWA_NB)]
        base = pl.multiple_of(n * SWA_ROWS, SWA_ROWS)
        for j in range(SWA_NB):
            own = pl.ds(base + j * SWA_BLK, SWA_BLK)
            after = j + 1 < SWA_NB
            dk_ref[own, :] += dkb[j][SWA_BLK:] + dkb[j + 1][:SWA_BLK] if after else dkb[j][SWA_BLK:]
            dv_ref[own, :] += dvb[j][SWA_BLK:] + dvb[j + 1][:SWA_BLK] if after else dvb[j][SWA_BLK:]

        @pl.when(n > 0)
        def _():
            before = pl.ds(pl.multiple_of(n * SWA_ROWS - SWA_BLK, SWA_BLK), SWA_BLK)
            dk_ref[before, :] += dkb[0][:SWA_BLK]
            dv_ref[before, :] += dvb[0][:SWA_BLK]

    return pl.pallas_call(
        body,
        name=name,
        grid=(SWA_KV_HEADS, S // SWA_ROWS),
        in_specs=_swa_in_specs(v_off) + [
            pl.BlockSpec((SWA_ROWS, gw), lambda h, n: (n, h)),
            pl.BlockSpec((SWA_GROUP, SWA_ROWS, 1), lambda h, n: (h, n, 0)),
            pl.BlockSpec((SWA_ROWS, gw), lambda h, n: (n, h)),
        ],
        out_specs=[
            pl.BlockSpec((SWA_ROWS, gw), lambda h, n: (n, h)),
            pl.BlockSpec((S, LANES), lambda h, n: (0, h)),
            pl.BlockSpec((S, LANES), lambda h, n: (0, h)),
            pl.BlockSpec((1, gw), lambda h, n: (0, h)),
        ],
        out_shape=[
            jax.ShapeDtypeStruct((S, SWA_HEADS * LANES), F32),
            jax.ShapeDtypeStruct((S, SWA_KV_HEADS * LANES), F32),
            jax.ShapeDtypeStruct((S, SWA_KV_HEADS * LANES), F32),
            jax.ShapeDtypeStruct((1, SWA_HEADS * LANES), F32),
        ],
        compiler_params=_params(("parallel", "arbitrary")),
    )(q, k, k, v, v, sink_b, o, lse, do)


def _pad_cols(w, heads, real):
    k = w.shape[0]
    return jnp.pad(w.reshape(k, heads, real), ((0, 0), (0, 0), (0, LANES - real))).reshape(k, heads * LANES)


def _unpad_cols(g, heads, real):
    k = g.shape[0]
    return g.reshape(k, heads, LANES)[:, :, :real].reshape(k, heads * real)


def _pad_rows(w, heads, real):
    n = w.shape[1]
    return jnp.pad(w.reshape(heads, real, n), ((0, 0), (0, LANES - real), (0, 0))).reshape(heads * LANES, n)


def _unpad_rows(g, heads, real):
    n = g.shape[1]
    return g.reshape(heads, LANES, n)[:, :real, :].reshape(heads * real, n)


def _w_in_internal(w_in):
    c_q, c_kv, k_r, q_swa, k_swa, v_swa, q_sb, k_sb, v_sb, gate = jnp.split(w_in, SPLIT_POINTS, axis=1)
    k_r = jnp.pad(k_r, ((0, 0), (MLA_NOPE, LANES - MLA_NOPE - MLA_ROPE)))
    w1 = jnp.concatenate([c_q, c_kv, k_r, _pad_cols(q_swa, 8, 64), _pad_cols(k_swa, 2, 64)], axis=1)
    w2 = [_pad_cols(v_swa, 2, 64), _pad_cols(q_sb, 8, 64), _pad_cols(k_sb, 8, 64), _pad_cols(v_sb, 8, 64)]
    return w1, w2, gate


def _w_in_reference(g1, g2, g3):
    c_q, c_kv, k_r, q_swa, k_swa = jnp.split(g1, [256, 384, 512, 1536], axis=1)
    v_swa, q_sb, k_sb, v_sb = g2
    return jnp.concatenate([
        c_q, c_kv, k_r[:, MLA_NOPE:MLA_NOPE + MLA_ROPE], _unpad_cols(q_swa, 8, 64), _unpad_cols(k_swa, 2, 64),
        _unpad_cols(v_swa, 2, 64), _unpad_cols(q_sb, 8, 64), _unpad_cols(k_sb, 8, 64), _unpad_cols(v_sb, 8, 64),
        g3], axis=1)


def _w_ukv_internal(w):
    w3 = w.reshape(MLA_KV_LORA, MLA_HEADS, MLA_NOPE + MLA_V)
    pad = lambda t: jnp.pad(t, ((0, 0), (0, 0), (0, LANES - t.shape[2]))).reshape(MLA_KV_LORA, MLA_HEADS * LANES)
    return pad(w3[:, :, :MLA_NOPE]), pad(w3[:, :, MLA_NOPE:])


def _w_ukv_reference(gk, gv):
    gk = gk.reshape(MLA_KV_LORA, MLA_HEADS, LANES)[:, :, :MLA_NOPE]
    gv = gv.reshape(MLA_KV_LORA, MLA_HEADS, LANES)[:, :, :MLA_V]
    return jnp.concatenate([gk, gv], axis=2).reshape(MLA_KV_LORA, MLA_HEADS * (MLA_NOPE + MLA_V))


def _layer_fwd(x, w, tabs):
    mla_tab, swa_tab = tabs
    sv = {"x": x}

    def f_norm(rows, consts):
        return [_rms(rows[0], consts[0])], []

    (h,) = _rowwise(f_norm, [x], [w["g_mix_pre"]], [(D_MODEL, BF16)], [], "norm_mix_pre")
    p1 = _matmul(h, w["w_in1"], "nn", [F32], "proj_lat")
    p2 = _matmul(h, w["w_in2"], "nn", [BF16], "proj_qkv")
    gates = _matmul(h, w["w_in3"], "nn", [BF16], "proj_gate",
                    epilogue=lambda acc, b: (1.0 / (1.0 + jnp.exp(-(acc + b))),), row_extras=[w["b_gate"]])

    def f_prep(rows, consts):
        t = rows[0]
        gq, gkv = consts[0], consts[1]
        mc, mu, md = rows[1], rows[2], rows[3]
        sc, su, sd = rows[4], rows[5], rows[6]
        cq_n = _rms(t[:, 0:256], gq)
        ckv_n = _rms(t[:, 256:384], gkv)
        kr = _rope(t[:, 384:512], mc, mu, md, MLA_ROPE // 2)
        qs = [_rope(t[:, 512 + j * LANES:512 + (j + 1) * LANES], sc, su, sd, SWA_HEAD_DIM // 2) for j in range(8)]
        ks = [_rope(t[:, 1536 + j * LANES:1536 + (j + 1) * LANES], sc, su, sd, SWA_HEAD_DIM // 2) for j in range(2)]
        return [cq_n, ckv_n, kr, jnp.concatenate(qs, axis=1), jnp.concatenate(ks, axis=1)], []

    cq_n, ckv_n, kr, q_swa, k_swa = _rowwise(
        f_prep, [p1, *mla_tab["k"], *swa_tab["f"]], [w["g_q_lat"], w["g_kv_lat"]],
        [(256, BF16), (128, BF16), (LANES, F32), (1024, BF16), (256, BF16)], [], "lat_prep")

    q_lat = _matmul(cq_n, w["w_uq"], "nn", [F32], "mla_q_up")
    k_lat = _matmul(ckv_n, w["w_ukv_k"], "nn", [F32], "mla_k_up")
    def ones_lane(acc):
        lane = lax.broadcasted_iota(jnp.int32, acc.shape, 1) % LANES
        return (jnp.where(lane == ONES_LANE, 1.0, acc),)

    v_mla = _matmul(ckv_n, w["w_ukv_v"], "nn", [BF16], "mla_v_up", epilogue=ones_lane)
    mla_scale = (MLA_NOPE + MLA_ROPE) ** -0.5

    def f_mla_prep(rows, consts):
        ql, kl, krr, mc, mu, md = rows
        qs = [_rope(ql[:, j * LANES:(j + 1) * LANES], mc, mu, md, MLA_ROPE // 2) * mla_scale for j in range(8)]
        ks = [kl[:, j * LANES:(j + 1) * LANES] + krr for j in range(8)]
        return [jnp.concatenate(qs, axis=1), jnp.concatenate(ks, axis=1)], []

    q_mla, k_mla = _rowwise(f_mla_prep, [q_lat, k_lat, kr, *mla_tab["q"]], [], [(1024, BF16), (1024, BF16)], [], "mla_prep")

    o_mla, lse_mla = _softmax_attn_fwd(q_mla, k_mla, v_mla, MLA_HEADS, "mla_fwd")
    o_swa, lse_swa = _swa_fwd(q_swa, k_swa, p2, 0, w["sink_b"], "swa_fwd")
    o_sb = _sb_attn_fwd(p2, SB_HEADS, "sb_fwd", 2, 10, 18)

    oa = _matmul(o_mla, w["w_o_mla"], "nn", [F32], "o_proj_mla")
    ob = _matmul(o_swa, w["w_o_swa"], "nn", [F32], "o_proj_swa")
    oc = _matmul(o_sb, w["w_o_sb"], "nn", [F32], "o_proj_sb")

    def f_mix(rows, consts):
        a, b, c, g = rows
        g = g.astype(F32)
        return [g[:, 0:1024] * a + g[:, 1024:2048] * b + g[:, 2048:3072] * c], []

    (mixed,) = _rowwise(f_mix, [oa, ob, oc, gates], [], [(D_MODEL, BF16)], [], "gate_mix")
    y = _matmul(mixed, w["w_out"], "nn", [F32], "out_proj")

    def f_res_norm(rows, consts):
        return [rows[0] + _rms(rows[1], consts[0])], []

    (x1,) = _rowwise(f_res_norm, [x, y], [w["g_mix_post"]], [(D_MODEL, F32)], [], "res_norm_mix")
    (h2,) = _rowwise(f_norm, [x1], [w["g_mlp_pre"]], [(D_MODEL, BF16)], [], "norm_mlp_pre")

    def relu2(acc):
        r = jnp.maximum(acc, 0.0)
        return acc, r * r

    up, u = _matmul(h2, w["w_up"], "nn", [BF16, BF16], "mlp_up", epilogue=relu2)
    zd = _matmul(u, w["w_down"], "nn", [F32], "mlp_down")
    (x2,) = _rowwise(f_res_norm, [x1, zd], [w["g_mlp_post"]], [(D_MODEL, F32)], [], "res_norm_mlp")

    sv.update(h=h, p1=p1, p2=p2, gates=gates, cq_n=cq_n, ckv_n=ckv_n, q_swa=q_swa, k_swa=k_swa, q_mla=q_mla,
              k_mla=k_mla, v_mla=v_mla, o_mla=o_mla, lse_mla=lse_mla, o_swa=o_swa, lse_swa=lse_swa, o_sb=o_sb,
              oa=oa, ob=ob, oc=oc, mixed=mixed, y=y, x1=x1, h2=h2, up=up, u=u, zd=zd)
    return x2, sv


def _layer_bwd(dx2, w, sv, tabs):
    mla_tab, swa_tab = tabs
    gr = {}

    def f_norm_bwd(rows, consts):
        dx, dg = _rms_bwd(rows[0], consts[0], rows[1])
        return [dx], [dg]

    def f_norm_bwd_res(rows, consts):
        dx, dg = _rms_bwd(rows[0], consts[0], rows[1])
        return [rows[2] + dx], [dg]

    dzd, gr["g_mlp_post"] = _rowwise(f_norm_bwd, [sv["zd"], dx2], [w["g_mlp_post"]], [(D_MODEL, BF16)], [D_MODEL], "b_norm_mlp_post")
    gr["w_down"] = _matmul(sv["u"], dzd, "tn", [F32], "b_w_down")
    dup = _matmul(dzd, w["w_down"], "nt", [BF16], "b_mlp_down",
                  epilogue=lambda acc, up: (acc * 2.0 * jnp.maximum(up.astype(F32), 0.0),), extras=[sv["up"]])
    gr["w_up"] = _matmul(sv["h2"], dup, "tn", [F32], "b_w_up")
    dh2 = _matmul(dup, w["w_up"], "nt", [F32], "b_mlp_up")
    dx1, gr["g_mlp_pre"] = _rowwise(f_norm_bwd_res, [sv["x1"], dh2, dx2], [w["g_mlp_pre"]], [(D_MODEL, F32)], [D_MODEL], "b_norm_mlp_pre")

    dy, gr["g_mix_post"] = _rowwise(f_norm_bwd, [sv["y"], dx1], [w["g_mix_post"]], [(D_MODEL, BF16)], [D_MODEL], "b_norm_mix_post")
    gr["w_out"] = _matmul(sv["mixed"], dy, "tn", [F32], "b_w_out")
    dmixed = _matmul(dy, w["w_out"], "nt", [F32], "b_out_proj")

    def f_mix_bwd(rows, consts):
        dm, a, b, c, g = rows
        g = g.astype(F32)
        outs, dls = [], []
        for j, o in enumerate((a, b, c)):
            gj = g[:, j * D_MODEL:(j + 1) * D_MODEL]
            outs.append(dm * gj)
            dls.append(dm * o * gj * (1.0 - gj))
        dl = jnp.concatenate(dls, axis=1)
        return outs + [dl], [dl]

    doa, dob, doc, dlogit, gr["b_gate"] = _rowwise(
        f_mix_bwd, [dmixed, sv["oa"], sv["ob"], sv["oc"], sv["gates"]], [],
        [(D_MODEL, BF16)] * 3 + [(P3_W, BF16)], [P3_W], "b_gate_mix")

    gr["w_o_mla"] = _matmul(sv["o_mla"], doa, "tn", [F32], "b_w_o_mla")
    gr["w_o_swa"] = _matmul(sv["o_swa"], dob, "tn", [F32], "b_w_o_swa")
    gr["w_o_sb"] = _matmul(sv["o_sb"], doc, "tn", [F32], "b_w_o_sb")
    do_mla = _matmul(doa, w["w_o_mla"], "nt", [BF16], "b_o_proj_mla")
    do_swa = _matmul(dob, w["w_o_swa"], "nt", [BF16], "b_o_proj_swa")
    do_sb = _matmul(doc, w["w_o_sb"], "nt", [BF16], "b_o_proj_sb")

    dq_sb, dk_sb, dv_sb = _sb_attn_bwd(sv["p2"], sv["o_sb"], do_sb, SB_HEADS, "sb_bwd", 2, 10, 18)
    dq_swa, dk_swa, dv_swa, dsink = _swa_bwd(sv["q_swa"], sv["k_swa"], sv["p2"], 0, w["sink_b"], sv["o_swa"],
                                             sv["lse_swa"], do_swa, "swa_bwd")
    gr["swa_sinks"] = dsink.reshape(SWA_HEADS, LANES)[:, 0]
    dq_mla, dk_mla, dv_mla = _softmax_attn_bwd(sv["q_mla"], sv["k_mla"], sv["v_mla"], sv["o_mla"], sv["lse_mla"], do_mla,
                                               MLA_HEADS, (MLA_NOPE + MLA_ROPE) ** -0.5, "mla_bwd")

    def f_mla_post(rows, consts):
        dq, dk, qc, qu, qd, kc, ku, kd = rows
        dqs = [_rope(dq[:, j * LANES:(j + 1) * LANES], qc, qu, qd, MLA_ROPE // 2) for j in range(8)]
        dkr = dk[:, 0:LANES]
        for j in range(1, 8):
            dkr = dkr + dk[:, j * LANES:(j + 1) * LANES]
        return [jnp.concatenate(dqs, axis=1), _rope(dkr, kc, ku, kd, MLA_ROPE // 2)], []

    dq_lat, dkr = _rowwise(f_mla_post, [dq_mla, dk_mla, *mla_tab["q_inv"], *mla_tab["k_inv"]], [],
                           [(1024, BF16), (LANES, F32)], [], "b_mla_post")
    gr["w_uq"] = _matmul(sv["cq_n"], dq_lat, "tn", [F32], "b_w_uq")
    gr["w_ukv_k"] = _matmul(sv["ckv_n"], dk_mla, "tn", [F32], "b_w_ukv_k")
    gr["w_ukv_v"] = _matmul(sv["ckv_n"], dv_mla, "tn", [F32], "b_w_ukv_v")
    dcq_n = _matmul(dq_lat, w["w_uq"], "nt", [F32], "b_mla_q_up")
    dckv_a = _matmul(dk_mla, w["w_ukv_k"], "nt", [F32], "b_mla_k_up")
    dckv_b = _matmul(dv_mla, w["w_ukv_v"], "nt", [F32], "b_mla_v_up")

    def f_prep_bwd(rows, consts):
        t, dcq, dca, dcb, dkr_, dqs, dks, sc, su, sd = rows
        gq, gkv = consts
        dc_q, dgq = _rms_bwd(t[:, 0:256], gq, dcq)
        dc_kv, dgkv = _rms_bwd(t[:, 256:384], gkv, dca + dcb)
        q_parts = [_rope(dqs[:, j * LANES:(j + 1) * LANES], sc, su, sd, SWA_HEAD_DIM // 2) for j in range(8)]
        k_parts = [_rope(dks[:, j * LANES:(j + 1) * LANES], sc, su, sd, SWA_HEAD_DIM // 2) for j in range(2)]
        return [jnp.concatenate([dc_q, dc_kv, dkr_] + q_parts + k_parts, axis=1)], [dgq, dgkv]

    dp1, gr["g_q_lat"], gr["g_kv_lat"] = _rowwise(
        f_prep_bwd, [sv["p1"], dcq_n, dckv_a, dckv_b, dkr, dq_swa, dk_swa, *swa_tab["inv"]], [w["g_q_lat"], w["g_kv_lat"]],
        [(P1_W, BF16)], [256, 128], "b_lat_prep")

    gr["w_in1"] = _matmul(sv["h"], dp1, "tn", [F32], "b_w_in_lat")
    dh = _matmul(dp1, w["w_in1"], "nt", [F32], "b_proj_lat")
    gr["w_in2"] = []
    add_prev = lambda acc, prev: (acc + prev,)
    for piece, wp, tag in zip((dv_swa, dq_sb, dk_sb, dv_sb), w["w_in2_parts"], ("vswa", "qsb", "ksb", "vsb")):
        gr["w_in2"].append(_matmul(sv["h"], piece, "tn", [F32], "b_w_in_" + tag))
        dh = _matmul(piece, wp, "nt", [F32], "b_proj_" + tag, epilogue=add_prev, extras=[dh])
    gr["w_in3"] = _matmul(sv["h"], dlogit, "tn", [F32], "b_w_in_gate")
    dh = _matmul(dlogit, w["w_in3"], "nt", [F32], "b_proj_gate", epilogue=add_prev, extras=[dh])
    dx, gr["g_mix_pre"] = _rowwise(f_norm_bwd_res, [sv["x"], dh, dx1], [w["g_mix_pre"]], [(D_MODEL, F32)], [D_MODEL], "b_norm_mix_pre")
    return dx, gr


def _local_step(x, positions, loss_target, full):
    mc, mu, md = _rope_tables(positions, MLA_NOPE, MLA_ROPE, True)
    kc, ku, kd = _rope_tables(positions, MLA_NOPE, MLA_ROPE, False)
    sc, su, sd = _rope_tables(positions, 0, SWA_HEAD_DIM, False)
    mla_tab = {"q": (mc, mu, md), "k": (kc, ku, kd), "q_inv": (mc, -mu, -md), "k_inv": (kc, -ku, -kd)}
    swa_tab = {"f": (sc, su, sd), "inv": (sc, -su, -sd)}
    tabs = (mla_tab, swa_tab)

    layers = []
    for l in range(DEPTH):
        w1, w2, w3 = _w_in_internal(full["w_in"][l].astype(BF16))
        uk, uv = _w_ukv_internal(full["w_ukv"][l].astype(BF16))
        layers.append({
            "w_in1": w1, "w_in2": jnp.concatenate(w2, axis=1), "w_in2_parts": w2, "w_in3": w3,
            "w_uq": _pad_cols(full["w_uq"][l].astype(BF16), MLA_HEADS, MLA_NOPE + MLA_ROPE),
            "w_ukv_k": uk, "w_ukv_v": uv,
            "w_o_mla": _pad_rows(full["w_o_mla"][l].astype(BF16), 8, 64),
            "w_o_swa": _pad_rows(full["w_o_swa"][l].astype(BF16), 8, 64),
            "w_o_sb": _pad_rows(full["w_o_sb"][l].astype(BF16), 8, 64),
            "w_out": full["w_out"][l].astype(BF16), "w_up": full["w_up"][l].astype(BF16),
            "w_down": full["w_down"][l].astype(BF16),
            "g_mix_pre": full["g_mix_pre"][l][None], "b_gate": full["b_gate"][l][None],
            "g_q_lat": full["g_q_lat"][l][None], "g_kv_lat": full["g_kv_lat"][l][None],
            "g_mix_post": full["g_mix_post"][l][None], "g_mlp_pre": full["g_mlp_pre"][l][None],
            "g_mlp_post": full["g_mlp_post"][l][None],
            "sink_b": jnp.repeat(full["swa_sinks"][l], LANES)[None],
        })

    saved = []
    h = x
    for l in range(DEPTH):
        h, sv = _layer_fwd(h, layers[l], tabs)
        saved.append(sv)

    def f_loss(rows, consts):
        err = rows[0] - rows[1]
        return [err * (1.0 / D_MODEL)], [jnp.sum(err * err, axis=1, keepdims=True)]

    dy, sq = _rowwise(f_loss, [h, loss_target], [], [(D_MODEL, F32)], [1], "loss_head")
    loss_part = sq * (0.5 / D_MODEL)

    grads = [None] * DEPTH
    d = dy
    for l in reversed(range(DEPTH)):
        d, gr = _layer_bwd(d, layers[l], saved[l], tabs)
        grads[l] = {
            "g_mix_pre": gr["g_mix_pre"][0], "w_in": _w_in_reference(gr["w_in1"], gr["w_in2"], gr["w_in3"]),
            "b_gate": gr["b_gate"][0], "g_q_lat": gr["g_q_lat"][0], "g_kv_lat": gr["g_kv_lat"][0],
            "w_uq": _unpad_cols(gr["w_uq"], MLA_HEADS, MLA_NOPE + MLA_ROPE),
            "w_ukv": _w_ukv_reference(gr["w_ukv_k"], gr["w_ukv_v"]), "swa_sinks": gr["swa_sinks"],
            "w_o_mla": _unpad_rows(gr["w_o_mla"], 8, 64), "w_o_swa": _unpad_rows(gr["w_o_swa"], 8, 64),
            "w_o_sb": _unpad_rows(gr["w_o_sb"], 8, 64), "w_out": gr["w_out"], "g_mix_post": gr["g_mix_post"][0],
            "g_mlp_pre": gr["g_mlp_pre"][0], "w_up": gr["w_up"], "w_down": gr["w_down"], "g_mlp_post": gr["g_mlp_post"][0],
        }
    stacked = {n: jnp.stack([grads[l][n] for l in range(DEPTH)]) for n in WEIGHTS}
    return loss_part, d, stacked


def _rows_of(a):
    return a.reshape(-1, LANES)


def _small_rows(d):
    parts = []
    for n in SMALL:
        a = d[n]
        if a.shape[1] < LANES:
            a = jnp.pad(a, ((0, 0), (0, LANES - a.shape[1])))
        parts.append(_rows_of(a))
    return parts


def _pack(shards, small, dtype):
    parts = [_rows_of(shards[n]) for n in SHARDED]
    if small is not None:
        parts += _small_rows(small)
    slab = jnp.concatenate(parts, axis=0).astype(dtype)
    pad = (-slab.shape[0]) % SLAB_ROW_ALIGN
    return jnp.pad(slab, ((0, pad), (0, 0)))


def _unpack(slab, shard_shapes, small_shapes):
    out, r = {}, 0
    for n in SHARDED:
        rows = int(np.prod(shard_shapes[n])) // LANES
        out[n] = slab[r:r + rows].reshape(shard_shapes[n])
        r += rows
    if small_shapes is not None:
        for n in SMALL:
            depth, width = small_shapes[n]
            rows = depth * max(width, LANES) // LANES
            out[n] = slab[r:r + rows].reshape(depth, max(width, LANES))[:, :width]
            r += rows
    return out


def _chip_exchange(src, name):
    rows = src.shape[-2]

    def body(src_ref, out_ref, send_sems, recv_sems, local_sem):
        x, y, c = lax.axis_index("x"), lax.axis_index("y"), lax.axis_index("c")
        me = 2 * x + y
        chips = [(1 - x, y), (x, 1 - y), (1 - x, 1 - y)]
        mine = pltpu.make_async_copy(src_ref.at[me], out_ref.at[me], local_sem)
        mine.start()
        sends = []
        for k, (cx, cy) in enumerate(chips):
            cp = pltpu.make_async_remote_copy(
                src_ref=src_ref.at[2 * cx + cy], dst_ref=out_ref.at[me], send_sem=send_sems.at[k],
                recv_sem=recv_sems.at[k], device_id=(cx, cy, c), device_id_type=pl.DeviceIdType.MESH)
            cp.start()
            sends.append(cp)
        for k, (cx, cy) in enumerate(chips):
            pltpu.make_async_remote_copy(
                src_ref=src_ref.at[me], dst_ref=out_ref.at[2 * cx + cy], send_sem=send_sems.at[k],
                recv_sem=recv_sems.at[k], device_id=(cx, cy, c), device_id_type=pl.DeviceIdType.MESH).wait_recv()
        for cp in sends:
            cp.wait_send()
        mine.wait()

    return pl.pallas_call(
        body,
        name=name,
        in_specs=[pl.BlockSpec(memory_space=pl.ANY)],
        out_specs=pl.BlockSpec(memory_space=pl.ANY),
        out_shape=jax.ShapeDtypeStruct((N_CHIPS, rows, LANES), src.dtype),
        scratch_shapes=[pltpu.SemaphoreType.DMA((3,)), pltpu.SemaphoreType.DMA((3,)), pltpu.SemaphoreType.DMA],
    )(src)


def _half_rows(c, half):
    return pl.ds(pl.multiple_of(c * half, SLAB_ROW_ALIGN // 2), half)


def _gather_weights(src, name):
    rows = src.shape[0]
    half = rows // 2

    def body(src_ref, out_ref, send_sems, recv_sems, local_sem):
        x, y, c = lax.axis_index("x"), lax.axis_index("y"), lax.axis_index("c")
        me = 2 * x + y
        chips = [(1 - x, y), (x, 1 - y), (1 - x, 1 - y)]

        def copy(k, src_view, slab, part, to):
            return pltpu.make_async_remote_copy(
                src_ref=src_view, dst_ref=out_ref.at[slab, _half_rows(part, half), :], send_sem=send_sems.at[k],
                recv_sem=recv_sems.at[k], device_id=to, device_id_type=pl.DeviceIdType.MESH)

        mine = pltpu.make_async_copy(src_ref, out_ref.at[me], local_sem)
        mine.start()
        sends = [copy(k, src_ref.at[_half_rows(c, half), :], me, c, (cx, cy, c)) for k, (cx, cy) in enumerate(chips)]
        for cp in sends:
            cp.start()
        for k, (cx, cy) in enumerate(chips):
            j = 2 * cx + cy
            landed = out_ref.at[j, _half_rows(c, half), :]
            copy(k, landed, j, c, (cx, cy, c)).wait_recv()
            fwd = copy(3 + k, landed, j, c, (x, y, 1 - c))
            fwd.start()
            sends.append(fwd)
        for k, (cx, cy) in enumerate(chips):
            j = 2 * cx + cy
            copy(3 + k, out_ref.at[j, _half_rows(1 - c, half), :], j, 1 - c, (x, y, 1 - c)).wait_recv()
        for cp in sends:
            cp.wait_send()
        mine.wait()

    return pl.pallas_call(
        body,
        name=name,
        in_specs=[pl.BlockSpec(memory_space=pl.ANY)],
        out_specs=pl.BlockSpec(memory_space=pl.ANY),
        out_shape=jax.ShapeDtypeStruct((N_CHIPS, rows, LANES), src.dtype),
        scratch_shapes=[pltpu.SemaphoreType.DMA((6,)), pltpu.SemaphoreType.DMA((6,)), pltpu.SemaphoreType.DMA],
    )(src)


def _sibling_halves(src, name):
    n, rows, _ = src.shape
    half = rows // 2

    def body(src_ref, out_ref, send_sem, recv_sem):
        x, y, c = lax.axis_index("x"), lax.axis_index("y"), lax.axis_index("c")
        cp = pltpu.make_async_remote_copy(
            src_ref=src_ref.at[:, _half_rows(1 - c, half), :], dst_ref=out_ref, send_sem=send_sem, recv_sem=recv_sem,
            device_id=(x, y, 1 - c), device_id_type=pl.DeviceIdType.MESH)
        cp.start()
        cp.wait()

    return pl.pallas_call(
        body,
        name=name,
        in_specs=[pl.BlockSpec(memory_space=pl.ANY)],
        out_specs=pl.BlockSpec(memory_space=pl.ANY),
        out_shape=jax.ShapeDtypeStruct((n, half, LANES), src.dtype),
        scratch_shapes=[pltpu.SemaphoreType.DMA, pltpu.SemaphoreType.DMA],
    )(src)


def _sibling_join(src, name):
    half = src.shape[0]

    def body(src_ref, out_ref, send_sem, recv_sem, local_sem):
        x, y, c = lax.axis_index("x"), lax.axis_index("y"), lax.axis_index("c")
        mine = pltpu.make_async_copy(src_ref, out_ref.at[_half_rows(c, half), :], local_sem)
        mine.start()
        cp = pltpu.make_async_remote_copy(
            src_ref=src_ref, dst_ref=out_ref.at[_half_rows(c, half), :], send_sem=send_sem, recv_sem=recv_sem,
            device_id=(x, y, 1 - c), device_id_type=pl.DeviceIdType.MESH)
        cp.start()
        pltpu.make_async_remote_copy(
            src_ref=src_ref, dst_ref=out_ref.at[_half_rows(1 - c, half), :], send_sem=send_sem, recv_sem=recv_sem,
            device_id=(x, y, 1 - c), device_id_type=pl.DeviceIdType.MESH).wait_recv()
        cp.wait_send()
        mine.wait()

    return pl.pallas_call(
        body,
        name=name,
        in_specs=[pl.BlockSpec(memory_space=pl.ANY)],
        out_specs=pl.BlockSpec(memory_space=pl.ANY),
        out_shape=jax.ShapeDtypeStruct((2 * half, LANES), src.dtype),
        scratch_shapes=[pltpu.SemaphoreType.DMA, pltpu.SemaphoreType.DMA, pltpu.SemaphoreType.DMA],
    )(src)


SUM_ROWS = 1024


def _pair_sum(mine, theirs, c, name):
    n, half, _ = theirs.shape
    blocks = half // SUM_ROWS

    def body(c_ref, a_ref, b_ref, o_ref):
        o_ref[...] = (a_ref[...].astype(F32) + b_ref[...].astype(F32)).astype(o_ref.dtype)

    return pl.pallas_call(
        body,
        name=name,
        grid_spec=pltpu.PrefetchScalarGridSpec(
            num_scalar_prefetch=1,
            grid=(blocks,),
            in_specs=[pl.BlockSpec((n, SUM_ROWS, LANES), lambda i, c_ref: (0, c_ref[0] * blocks + i, 0)),
                      pl.BlockSpec((n, SUM_ROWS, LANES), lambda i, c_ref: (0, i, 0))],
            out_specs=pl.BlockSpec((n, SUM_ROWS, LANES), lambda i, c_ref: (0, i, 0)),
        ),
        out_shape=jax.ShapeDtypeStruct((n, half, LANES), BF16),
        compiler_params=_params(("arbitrary",)),
    )(jnp.reshape(c, (1,)).astype(jnp.int32), mine, theirs)


def _sum_chips(buf, name):
    rows = buf.shape[1]

    def body(b_ref, o_ref):
        t = [b_ref[j].astype(F32) for j in range(N_CHIPS)]
        o_ref[...] = ((t[0] + t[1]) + t[2]) + t[3]

    return pl.pallas_call(
        body,
        name=name,
        grid=(rows // SUM_ROWS,),
        in_specs=[pl.BlockSpec((N_CHIPS, SUM_ROWS, LANES), lambda i: (0, i, 0))],
        out_specs=pl.BlockSpec((SUM_ROWS, LANES), lambda i: (i, 0)),
        out_shape=jax.ShapeDtypeStruct((rows, LANES), F32),
        compiler_params=_params(("arbitrary",)),
    )(buf)


def _adamw(w, m, v, g, name):
    shape = w.shape
    flat = lambda a: a.reshape(-1, shape[-1])

    def fn(rows, consts):
        w_, m_, v_, g_ = rows
        m_new = ADAM_B1 * m_ + (1.0 - ADAM_B1) * g_
        v_new = ADAM_B2 * v_ + (1.0 - ADAM_B2) * (g_ * g_)
        m_hat = m_new / (1.0 - ADAM_B1 ** ADAM_STEP)
        v_hat = v_new / (1.0 - ADAM_B2 ** ADAM_STEP)
        delta = -ADAM_LR * (m_hat / (jnp.sqrt(v_hat) + ADAM_EPS) + ADAM_WD * w_)
        return [delta, m_new, v_new], []

    outs = _rowwise(fn, [flat(w), flat(m), flat(v), flat(g)], [], [(shape[-1], F32)] * 3, [], name, bm=256)
    return [o.reshape(shape) for o in outs]


def kernel(x, positions, g_mix_pre, w_in, b_gate, g_q_lat, g_kv_lat, w_uq, w_ukv, swa_sinks, w_o_mla, w_o_swa, w_o_sb, w_out, g_mix_post, g_mlp_pre, w_up, w_down, g_mlp_post, loss_target, m_g_mix_pre, m_w_in, m_b_gate, m_g_q_lat, m_g_kv_lat, m_w_uq, m_w_ukv, m_swa_sinks, m_w_o_mla, m_w_o_swa, m_w_o_sb, m_w_out, m_g_mix_post, m_g_mlp_pre, m_w_up, m_w_down, m_g_mlp_post, v_g_mix_pre, v_w_in, v_b_gate, v_g_q_lat, v_g_kv_lat, v_w_uq, v_w_ukv, v_swa_sinks, v_w_o_mla, v_w_o_swa, v_w_o_sb, v_w_out, v_g_mix_post, v_g_mlp_pre, v_w_up, v_w_down, v_g_mlp_post):
    given = dict(locals())
    wts = {n: given[n] for n in WEIGHTS}
    mom_m = {n: given["m_" + n] for n in WEIGHTS}
    mom_v = {n: given["v_" + n] for n in WEIGHTS}
    shard_shapes = {n: wts[n].shape for n in SHARDED}
    small_shapes = {n: wts[n].shape for n in SMALL}

    gathered = _gather_weights(_pack(wts, None, BF16), "gather_weights")
    full = {n: wts[n] for n in SMALL}
    per_chip = [_unpack(gathered[j], shard_shapes, None) for j in range(N_CHIPS)]
    for n in SHARDED:
        full[n] = jnp.concatenate([per_chip[j][n] for j in range(N_CHIPS)], axis=SHARD_AXIS[n])

    loss_part, grad_x, grads = _local_step(x[0], positions[0], loss_target[0], full)
    loss = lax.psum(loss_part[0, 0], ("x", "y", "c"))

    small_g = {n: grads[n] for n in SMALL}
    slabs = []
    for j in range(N_CHIPS):
        shard = {n: jnp.split(grads[n], N_CHIPS, axis=SHARD_AXIS[n])[j] for n in SHARDED}
        slabs.append(_pack(shard, small_g, BF16))
    per_chip_g = jnp.stack(slabs)
    theirs = _sibling_halves(per_chip_g, "pair_grads")
    pair = _pair_sum(per_chip_g, theirs, lax.axis_index("c"), "sum_pair")
    landed = _chip_exchange(pair, "scatter_grads")
    g_slab = _sibling_join(_sum_chips(landed, "sum_chips"), "join_grads")

    g = _unpack(g_slab, shard_shapes, small_shapes)
    stepped = {n: _adamw(wts[n], mom_m[n], mom_v[n], g[n], "adamw_" + n) for n in WEIGHTS}
    outs = [loss, grad_x[None]] + [g[n] for n in WEIGHTS]
    for part in range(3):
        outs += [stepped[n][part] for n in WEIGHTS]
    return tuple(outs)
```

```python
import numpy as np
import jax
import jax.numpy as jnp
from jax import lax
from jax.experimental import pallas as pl
from jax.experimental.pallas import tpu as pltpu

F32 = jnp.float32
BF16 = jnp.bfloat16

D_MODEL = 1024
DEPTH = 4
MLA_HEADS, MLA_Q_LORA, MLA_KV_LORA, MLA_NOPE, MLA_ROPE, MLA_V = 8, 256, 128, 64, 32, 64
SWA_HEADS, SWA_KV_HEADS, SWA_HEAD_DIM, SWA_WINDOW = 8, 2, 64, 128
SB_HEADS, SB_HEAD_DIM = 8, 64
D_FF = 4 * D_MODEL
ROPE_THETA = 10000.0
EPS = 1e-6
SPLIT_SIZES = (256, 128, 32, 512, 128, 128, 512, 512, 512, 3 * D_MODEL)
SPLIT_POINTS = [int(v) for v in np.cumsum(SPLIT_SIZES)[:-1]]

ADAM_LR, ADAM_B1, ADAM_B2, ADAM_EPS, ADAM_WD, ADAM_STEP = 0.001, 0.9, 0.999, 1e-08, 0.01, 10

LANES = 128
V7X_VMEM_BYTES = 64 * 1024 * 1024
VMEM_LIMIT = V7X_VMEM_BYTES - 8 * 1024 * 1024
MATMUL_VMEM_BUDGET = 36 * 1024 * 1024
N_CHIPS = 4
SLAB_ROW_ALIGN = 512

P1_W = 256 + 128 + 128 + 1024 + 256
P2_W = 256 + 1024 + 1024 + 1024
P3_W = 3 * D_MODEL

SHARDED = ("w_in", "w_uq", "w_ukv", "w_o_mla", "w_o_swa", "w_o_sb", "w_out", "w_up", "w_down")
SHARD_AXIS = {"w_in": 2, "w_uq": 2, "w_ukv": 2, "w_o_mla": 2, "w_o_swa": 2, "w_o_sb": 2, "w_out": 1, "w_up": 2, "w_down": 1}
SMALL = ("g_mix_pre", "b_gate", "g_q_lat", "g_kv_lat", "swa_sinks", "g_mix_post", "g_mlp_pre", "g_mlp_post")
WEIGHTS = ("g_mix_pre", "w_in", "b_gate", "g_q_lat", "g_kv_lat", "w_uq", "w_ukv", "swa_sinks", "w_o_mla", "w_o_swa",
           "w_o_sb", "w_out", "g_mix_post", "g_mlp_pre", "w_up", "w_down", "g_mlp_post")

NN = (((1,), (0,)), ((), ()))
NT = (((1,), (1,)), ((), ()))
TN = (((0,), (0,)), ((), ()))


def _dot(a, b, dims):
    return lax.dot_general(a, b, dims, preferred_element_type=F32)


def _params(sem):
    return pltpu.CompilerParams(dimension_semantics=sem, vmem_limit_bytes=VMEM_LIMIT)


def _largest_tile(n, cap):
    if n <= cap:
        return n
    best = LANES
    for t in range(LANES, cap + 1, LANES):
        if n % t == 0:
            best = t
    return best


def _matmul_tiles(M, N, K, a_bytes, b_bytes, out_bytes, extra_bytes):
    tn = _largest_tile(N, 1792)
    tm = _largest_tile(M, 1024 if tn <= 1024 else 512)
    tk = _largest_tile(K, 2048)

    def need(tm_, tk_):
        acc = 4 * tm_ * tn if tk_ < K else 0
        return 2 * (tm_ * tk_ * a_bytes + tk_ * tn * b_bytes + tm_ * tn * (out_bytes + extra_bytes)) + acc

    while need(tm, tk) > MATMUL_VMEM_BUDGET:
        if tk >= tm and tk % 256 == 0:
            tk //= 2
        elif tm % 256 == 0:
            tm //= 2
        else:
            break
    return tm, tn, tk


def _matmul(a, b, mode, out_dtypes, name, epilogue=None, extras=(), row_extras=()):
    if mode == "nn":
        (M, K), (K2, N) = a.shape, b.shape
    elif mode == "nt":
        (M, K), (N, K2) = a.shape, b.shape
    else:
        (K, M), (K2, N) = a.shape, b.shape
    assert K == K2, (name, a.shape, b.shape)
    tm, tn, tk = _matmul_tiles(
        M, N, K, a.dtype.itemsize, b.dtype.itemsize, sum(jnp.dtype(d).itemsize for d in out_dtypes),
        sum(e.dtype.itemsize for e in extras))
    assert M % tm == 0 and N % tn == 0 and K % tk == 0, (name, M, N, K, tm, tn, tk)
    nk = K // tk
    if mode == "tn":
        a_spec = pl.BlockSpec((tk, tm), lambda i, j, k: (k, i))
    else:
        a_spec = pl.BlockSpec((tm, tk), lambda i, j, k: (i, k))
    if mode == "nt":
        b_spec = pl.BlockSpec((tn, tk), lambda i, j, k: (j, k))
    else:
        b_spec = pl.BlockSpec((tk, tn), lambda i, j, k: (k, j))
    dims = {"nn": NN, "nt": NT, "tn": TN}[mode]
    n_ex, n_rex, n_out = len(extras), len(row_extras), len(out_dtypes)

    def body(*refs):
        a_ref, b_ref = refs[:2]
        ex = refs[2:2 + n_ex]
        rex = refs[2 + n_ex:2 + n_ex + n_rex]
        outs = refs[2 + n_ex + n_rex:2 + n_ex + n_rex + n_out]

        def finish(total):
            res = (total,) if epilogue is None else epilogue(total, *[e[...] for e in ex], *[e[...] for e in rex])
            for o, r in zip(outs, res):
                o[...] = r.astype(o.dtype)

        part = _dot(a_ref[...].astype(BF16), b_ref[...].astype(BF16), dims)
        if nk == 1:
            finish(part)
            return
        acc = refs[-1]
        k = pl.program_id(2)

        @pl.when(k == 0)
        def _():
            acc[...] = part

        @pl.when(k > 0)
        def _():
            acc[...] += part

        @pl.when(k == nk - 1)
        def _():
            finish(acc[...])

    in_specs = [a_spec, b_spec]
    in_specs += [pl.BlockSpec((tm, tn), lambda i, j, k: (i, j)) for _ in extras]
    in_specs += [pl.BlockSpec((1, tn), lambda i, j, k: (0, j)) for _ in row_extras]
    out = pl.pallas_call(
        body,
        name=name,
        grid=(M // tm, N // tn, nk),
        in_specs=in_specs,
        out_specs=[pl.BlockSpec((tm, tn), lambda i, j, k: (i, j)) for _ in out_dtypes],
        out_shape=[jax.ShapeDtypeStruct((M, N), dt) for dt in out_dtypes],
        scratch_shapes=[pltpu.VMEM((tm, tn), F32)] if nk > 1 else [],
        compiler_params=_params(("parallel", "parallel", "arbitrary")),
    )(a, b, *extras, *row_extras)
    return out[0] if n_out == 1 else out


def _rowwise(fn, rows, consts, out_defs, sum_widths, name, bm=256):
    R = rows[0].shape[0]
    bm = min(bm, R)
    assert R % bm == 0, (name, R, bm)
    n_r, n_c, n_o = len(rows), len(consts), len(out_defs)
    n_s = len(sum_widths)

    def body(*refs):
        r_in = refs[:n_r]
        c_in = refs[n_r:n_r + n_c]
        o_refs = refs[n_r + n_c:n_r + n_c + n_o]
        s_refs = refs[n_r + n_c + n_o:]
        outs, sums = fn([r[...] for r in r_in], [c[...] for c in c_in])
        for o, val in zip(o_refs, outs):
            o[...] = val.astype(o.dtype)
        if n_s:
            @pl.when(pl.program_id(0) == 0)
            def _():
                for s in s_refs:
                    s[...] = jnp.zeros_like(s)

            for s, val in zip(s_refs, sums):
                s[...] += jnp.sum(val, axis=0, keepdims=True)

    in_specs = [pl.BlockSpec((bm, r.shape[1]), lambda i: (i, 0)) for r in rows]
    in_specs += [pl.BlockSpec(c.shape, lambda i: (0, 0)) for c in consts]
    out_specs = [pl.BlockSpec((bm, w), lambda i: (i, 0)) for w, _ in out_defs]
    out_specs += [pl.BlockSpec((1, w), lambda i: (0, 0)) for w in sum_widths]
    out_shape = [jax.ShapeDtypeStruct((R, w), dt) for w, dt in out_defs]
    out_shape += [jax.ShapeDtypeStruct((1, w), F32) for w in sum_widths]
    return pl.pallas_call(
        body,
        name=name,
        grid=(R // bm,),
        in_specs=in_specs,
        out_specs=out_specs,
        out_shape=out_shape,
        compiler_params=_params(("arbitrary",)),
    )(*rows, *consts)


def _rms(x, g):
    r = lax.rsqrt(jnp.mean(x * x, axis=-1, keepdims=True) + EPS)
    return x * r * g


def _rms_bwd(x, g, dy):
    r = lax.rsqrt(jnp.mean(x * x, axis=-1, keepdims=True) + EPS)
    n = x * r
    dn = dy * g
    dx = r * (dn - n * jnp.mean(dn * n, axis=-1, keepdims=True))
    return dx, dy * n


def _rope(x, c, s_up, s_dn, half):
    return x * c + pltpu.roll(x, half, 1) * s_up + pltpu.roll(x, LANES - half, 1) * s_dn


def _rope_tables(positions, lo, d, nope_pass):
    S = positions.shape[0]
    half = d // 2
    inv = 1.0 / (ROPE_THETA ** (jnp.arange(0, d, 2, dtype=F32) / d))
    ang = positions.astype(F32)[:, None] * inv
    cos, sin = jnp.cos(ang), jnp.sin(ang)
    z = lambda n: jnp.zeros((S, n), F32)
    head = jnp.ones((S, lo), F32) if nope_pass else z(lo)
    tail = LANES - lo - d
    c = jnp.concatenate([head, cos, cos, z(tail)], axis=1)
    s_up = jnp.concatenate([z(lo), z(half), sin, z(tail)], axis=1)
    s_dn = jnp.concatenate([z(lo), -sin, z(half), z(tail)], axis=1)
    return c, s_up, s_dn


MLA_FWD_CFG = (1, 1024)
MLA_BWD_CFG = (2, 512)
SB_FWD_CFG = (2, 256)
SB_BWD_CFG = (2, 256)


def _tile_mask(bk, strict):
    row = lax.broadcasted_iota(jnp.int32, (bk, bk), 0)
    col = lax.broadcasted_iota(jnp.int32, (bk, bk), 1)
    return (col < row) if strict else (col <= row)


def _att_layout(S, cfg):
    nch, bk = cfg
    bq = nch * bk
    assert S % bq == 0, (S, cfg)
    rows = [slice(r * bk, (r + 1) * bk) for r in range(nch)]
    q_spec = lambda off=0: pl.BlockSpec((bq, LANES), lambda h, i: (i, off + h))
    kv_spec = lambda off=0: pl.BlockSpec((S, LANES), lambda h, i: (0, off + h))
    return bq, rows, q_spec, kv_spec


def _total(terms):
    terms = list(terms)
    out = terms[0]
    for t in terms[1:]:
        out = out + t
    return out


def _walk(nch, i, step, carry, leftward, alive=None):
    everyone = range(nch)
    if leftward:
        for d in reversed(everyone):
            carry = step(nch * i + d, carry, range(d, nch), {d})
        if alive is None:
            return lax.fori_loop(0, nch * i, lambda t, c: step(nch * i - 1 - t, c, everyone, set()), carry)
        more = lambda tc: jnp.logical_and(tc[0] < nch * i, alive(tc[1]))
        left = lambda tc: (tc[0] + 1, step(nch * i - 1 - tc[0], tc[1], everyone, set()))
        return lax.while_loop(more, left, (jnp.int32(0), carry))[1]
    carry = lax.fori_loop(0, nch * i, lambda kb, c: step(kb, c, everyone, set()), carry)
    for d in everyone:
        carry = step(nch * i + d, carry, range(d, nch), {d})
    return carry


ONES_LANE = MLA_V


def _softmax_attn_fwd(q, k, v, heads, name, q_off=0, k_off=0, v_off=0):
    S = q.shape[0]
    nch, bk = MLA_FWD_CFG
    bq, rows, q_spec, kv_spec = _att_layout(S, MLA_FWD_CFG)

    def body(q_ref, k_ref, v_ref, o_ref, lse_ref):
        i = pl.program_id(1)
        qs = [q_ref[rw, :] for rw in rows]

        def step(kb, cs, active, masked):
            off = pl.multiple_of(kb * bk, bk)
            ks, vs = k_ref[pl.ds(off, bk), :], v_ref[pl.ds(off, bk), :]
            A = list(active)
            s = {r: _dot(qs[r], ks, NT) for r in A}
            s = {r: (jnp.where(_tile_mask(bk, False), s[r], -1e30) if r in masked else s[r]) for r in A}
            m_new = {r: jnp.maximum(cs[r][0], jnp.max(s[r], axis=1, keepdims=True)) for r in A}
            p = {r: jnp.exp(s[r] - m_new[r]) for r in A}
            alpha = {r: jnp.exp(cs[r][0] - m_new[r]) for r in A}
            new = list(cs)
            for r in A:
                new[r] = (m_new[r], alpha[r] * cs[r][1] + _dot(p[r].astype(BF16), vs, NN))
            return tuple(new)

        init = (jnp.full((bk, 1), -1e30, F32), jnp.zeros((bk, LANES), F32))
        cs = _walk(nch, i, step, tuple(init for _ in rows), False)
        for r, (m, acc) in enumerate(cs):
            l = acc[:, ONES_LANE:ONES_LANE + 1]
            o_ref[rows[r], :] = (acc / l).astype(o_ref.dtype)
            lse_ref[rows[r], :] = m + jnp.log(l)

    return pl.pallas_call(
        body,
        name=name,
        grid=(heads, S // bq),
        in_specs=[q_spec(q_off), kv_spec(k_off), kv_spec(v_off)],
        out_specs=[q_spec(), pl.BlockSpec((None, bq, 1), lambda h, i: (h, i, 0))],
        out_shape=[jax.ShapeDtypeStruct((S, heads * LANES), BF16), jax.ShapeDtypeStruct((heads, S, 1), F32)],
        compiler_params=_params(("parallel", "arbitrary")),
    )(q, k, v)


def _softmax_attn_bwd(q, k, v, o, lse, do, heads, scale, name, q_off=0, k_off=0, v_off=0):
    S = q.shape[0]
    nch, bk = MLA_BWD_CFG
    bq, rows, q_spec, kv_spec = _att_layout(S, MLA_BWD_CFG)

    def body(q_ref, k_ref, v_ref, o_ref, lse_ref, do_ref, dq_ref, dk_ref, dv_ref):
        i = pl.program_id(1)

        @pl.when(i == 0)
        def _():
            dk_ref[...] = jnp.zeros_like(dk_ref)
            dv_ref[...] = jnp.zeros_like(dv_ref)

        qs = [q_ref[rw, :] for rw in rows]
        dos = [do_ref[rw, :] for rw in rows]
        lses = [lse_ref[rw, :] for rw in rows]
        deltas = [jnp.sum(dos[r].astype(F32) * o_ref[rows[r], :].astype(F32), axis=1, keepdims=True) for r in range(nch)]

        def step(kb, dqs, active, masked):
            off = pl.multiple_of(kb * bk, bk)
            ks, vs = k_ref[pl.ds(off, bk), :], v_ref[pl.ds(off, bk), :]
            A = list(active)
            s = {r: _dot(qs[r], ks, NT) for r in A}
            s = {r: (jnp.where(_tile_mask(bk, False), s[r], -1e30) if r in masked else s[r]) for r in A}
            p = {r: jnp.exp(s[r] - lses[r]) for r in A}
            dp = {r: _dot(dos[r], vs, NT) for r in A}
            ds = {r: (p[r] * (dp[r] - deltas[r])).astype(BF16) for r in A}
            dv_c = _total(_dot(p[r].astype(BF16), dos[r], TN) for r in A)
            dk_c = _total(_dot(ds[r], qs[r], TN) for r in A)
            dk_ref[pl.ds(off, bk), :] += dk_c
            dv_ref[pl.ds(off, bk), :] += dv_c
            new = list(dqs)
            for r in A:
                new[r] = dqs[r] + _dot(ds[r], ks, NN)
            return tuple(new)

        dqs = _walk(nch, i, step, tuple(jnp.zeros((bk, LANES), F32) for _ in rows), False)
        for r in range(nch):
            dq_ref[rows[r], :] = dqs[r] * scale

    return pl.pallas_call(
        body,
        name=name,
        grid=(heads, S // bq),
        in_specs=[q_spec(q_off), kv_spec(k_off), kv_spec(v_off), q_spec(),
                  pl.BlockSpec((None, bq, 1), lambda h, i: (h, i, 0)), q_spec()],
        out_specs=[q_spec(), kv_spec(), kv_spec()],
        out_shape=[jax.ShapeDtypeStruct((S, heads * LANES), F32)] * 3,
        compiler_params=_params(("parallel", "arbitrary")),
    )(q, k, v, o, lse, do)


def _tri(n, inclusive):
    r = lax.broadcasted_iota(jnp.int32, (n, n), 0)
    c = lax.broadcasted_iota(jnp.int32, (n, n), 1)
    return jnp.where((r >= c) if inclusive else (r > c), 1.0, 0.0).astype(BF16)


def _suffix_sum(x, tri):
    hi = x.astype(BF16)
    lo = (x - hi.astype(F32)).astype(BF16)
    return _dot(hi, tri, NN) + _dot(lo, tri, NN)


def _sb_logs(z):
    lg = jnp.log(1.0 + jnp.exp(-jnp.abs(z)))
    l1m = -(jnp.maximum(z, 0.0) + lg)
    return l1m, l1m + z


SB_SCALE = SB_HEAD_DIM ** -0.5
assert SB_SCALE == 0.125
SB_DEAD = -110.0


def _sb_alive(cs):
    top = cs[0][0]
    for c in cs[1:]:
        top = jnp.maximum(top, c[0])
    return jnp.max(top) > SB_DEAD


def _sb_attn_fwd(qkv, heads, name, q_off, k_off, v_off):
    S = qkv.shape[0]
    nch, bk = SB_FWD_CFG
    bq, rows, q_spec, kv_spec = _att_layout(S, SB_FWD_CFG)

    def body(q_ref, k_ref, v_ref, o_ref):
        i = pl.program_id(1)
        qs = [q_ref[rw, :] * SB_SCALE for rw in rows]
        tri = _tri(bk, False)

        def step(kb, cs, active, masked):
            off = pl.multiple_of(kb * bk, bk)
            ks, vs = k_ref[pl.ds(off, bk), :], v_ref[pl.ds(off, bk), :]
            A = list(active)
            lg = {r: _sb_logs(_dot(qs[r], ks, NT)) for r in A}
            l1m = {r: (jnp.where(_tile_mask(bk, True), lg[r][0], 0.0) if r in masked else lg[r][0]) for r in A}
            suf = {r: _suffix_sum(l1m[r], tri) for r in A}
            ex = {r: lg[r][1] + cs[r][0] + suf[r] for r in A}
            ex = {r: (jnp.where(_tile_mask(bk, True), ex[r], -1e30) if r in masked else ex[r]) for r in A}
            ab = {r: jnp.exp(ex[r]).astype(BF16) for r in A}
            new = list(cs)
            for r in A:
                new[r] = (cs[r][0] + jnp.sum(l1m[r], axis=1, keepdims=True), cs[r][1] + _dot(ab[r], vs, NN))
            return tuple(new)

        init = (jnp.zeros((bk, 1), F32), jnp.zeros((bk, LANES), F32))
        cs = _walk(nch, i, step, tuple(init for _ in rows), True, _sb_alive)
        for r in range(nch):
            o_ref[rows[r], :] = cs[r][1]

    return pl.pallas_call(
        body,
        name=name,
        grid=(heads, S // bq),
        in_specs=[q_spec(q_off), kv_spec(k_off), kv_spec(v_off)],
        out_specs=q_spec(),
        out_shape=jax.ShapeDtypeStruct((S, heads * LANES), F32),
        compiler_params=_params(("parallel", "arbitrary")),
    )(qkv, qkv, qkv)


def _sb_attn_bwd(qkv, o, do, heads, name, q_off, k_off, v_off):
    S = qkv.shape[0]
    nch, bk = SB_BWD_CFG
    bq, rows, q_spec, kv_spec = _att_layout(S, SB_BWD_CFG)

    def body(q_ref, k_ref, v_ref, o_ref, do_ref, dq_ref, dk_ref, dv_ref):
        i = pl.program_id(1)

        @pl.when(i == 0)
        def _():
            dk_ref[...] = jnp.zeros_like(dk_ref)
            dv_ref[...] = jnp.zeros_like(dv_ref)

        tri = _tri(bk, False)
        qs = [q_ref[rw, :] * SB_SCALE for rw in rows]
        dos = [do_ref[rw, :] for rw in rows]
        deltas = [jnp.sum(dos[r].astype(F32) * o_ref[rows[r], :], axis=1, keepdims=True) for r in range(nch)]

        def step(kb, cs, active, masked):
            off = pl.multiple_of(kb * bk, bk)
            ks, vs = k_ref[pl.ds(off, bk), :], v_ref[pl.ds(off, bk), :]
            A = list(active)
            lg = {r: _sb_logs(_dot(qs[r], ks, NT)) for r in A}
            l1m = {r: (jnp.where(_tile_mask(bk, True), lg[r][0], 0.0) if r in masked else lg[r][0]) for r in A}
            suf = {r: _suffix_sum(l1m[r], tri) for r in A}
            ex = {r: lg[r][1] + cs[r][0] + suf[r] for r in A}
            ex = {r: (jnp.where(_tile_mask(bk, True), ex[r], -1e30) if r in masked else ex[r]) for r in A}
            ab = {r: jnp.exp(ex[r]).astype(BF16) for r in A}
            da = {r: _dot(dos[r], vs, NT) for r in A}
            g = {r: ab[r].astype(F32) * da[r] for r in A}
            gs = {r: _suffix_sum(g[r], tri) for r in A}
            beta = {r: jnp.exp(lg[r][1]) for r in A}
            dz = {r: g[r] - beta[r] * (deltas[r] - cs[r][1] - gs[r]) for r in A}
            dz = {r: (jnp.where(_tile_mask(bk, True), dz[r], 0.0) if r in masked else dz[r]) for r in A}
            dzb = {r: dz[r].astype(BF16) for r in A}
            dv_c = _total(_dot(ab[r], dos[r], TN) for r in A)
            dk_c = _total(_dot(dzb[r], qs[r], TN) for r in A)
            dk_ref[pl.ds(off, bk), :] += dk_c
            dv_ref[pl.ds(off, bk), :] += dv_c
            new = list(cs)
            for r in A:
                new[r] = (cs[r][0] + jnp.sum(l1m[r], axis=1, keepdims=True),
                          cs[r][1] + jnp.sum(g[r], axis=1, keepdims=True), cs[r][2] + _dot(dzb[r], ks, NN))
            return tuple(new)

        zcol = jnp.zeros((bk, 1), F32)
        init = (zcol, zcol, jnp.zeros((bk, LANES), F32))
        cs = _walk(nch, i, step, tuple(init for _ in rows), True, _sb_alive)
        for r in range(nch):
            dq_ref[rows[r], :] = cs[r][2] * SB_SCALE

    return pl.pallas_call(
        body,
        name=name,
        grid=(heads, S // bq),
        in_specs=[q_spec(q_off), kv_spec(k_off), kv_spec(v_off), q_spec(), q_spec()],
        out_specs=[q_spec(), kv_spec(), kv_spec()],
        out_shape=[jax.ShapeDtypeStruct((S, heads * LANES), F32)] * 3,
        compiler_params=_params(("parallel", "arbitrary")),
    )(qkv, qkv, qkv, o, do)


SWA_BLK = 128
SWA_GROUP = SWA_HEADS // SWA_KV_HEADS


SWA_NB = 4
SWA_ROWS = SWA_NB * SWA_BLK


def _swa_band_mask(first):
    row = lax.broadcasted_iota(jnp.int32, (SWA_BLK, 2 * SWA_BLK), 0)
    col = lax.broadcasted_iota(jnp.int32, (SWA_BLK, 2 * SWA_BLK), 1)
    return (col > row) & (col <= row + SWA_WINDOW) & (jnp.logical_not(first) | (col >= SWA_BLK))


def _swa_in_specs(v_off):
    gw = SWA_GROUP * LANES
    before = lambda h, n: (jnp.maximum(SWA_NB * n - 1, 0), h)
    return [
        pl.BlockSpec((SWA_ROWS, gw), lambda h, n: (n, h)),
        pl.BlockSpec((SWA_BLK, LANES), before),
        pl.BlockSpec((SWA_ROWS, LANES), lambda h, n: (n, h)),
        pl.BlockSpec((SWA_BLK, LANES), lambda h, n: (jnp.maximum(SWA_NB * n - 1, 0), v_off + h)),
        pl.BlockSpec((SWA_ROWS, LANES), lambda h, n: (n, v_off + h)),
        pl.BlockSpec((1, gw), lambda h, n: (0, h)),
    ]


def _swa_bands(n, kp_ref, kc_ref, vp_ref, vc_ref):
    k_all = jnp.concatenate([kp_ref[...], kc_ref[...]], axis=0)
    v_all = jnp.concatenate([vp_ref[...], vc_ref[...]], axis=0)
    bands = []
    for j in range(SWA_NB):
        rows = slice(j * SWA_BLK, (j + 2) * SWA_BLK)
        bands.append((k_all[rows], v_all[rows], _swa_band_mask((n == 0) if j == 0 else False)))
    return bands


def _swa_fwd(q, k, v, v_off, sink_b, name):
    S = q.shape[0]
    assert S % SWA_ROWS == 0
    scale = SWA_HEAD_DIM ** -0.5
    gw = SWA_GROUP * LANES

    def body(q_ref, kp_ref, kc_ref, vp_ref, vc_ref, sink_ref, o_ref, lse_ref):
        n = pl.program_id(1)
        bands = _swa_bands(n, kp_ref, kc_ref, vp_ref, vc_ref)
        P = [(j, g) for j in range(SWA_NB) for g in range(SWA_GROUP)]
        rows = lambda j: slice(j * SWA_BLK, (j + 1) * SWA_BLK)
        lanes = lambda g: slice(g * LANES, (g + 1) * LANES)
        sk = {g: sink_ref[:, g * LANES:g * LANES + 1] for g in range(SWA_GROUP)}
        s = {(j, g): jnp.where(bands[j][2], _dot(q_ref[rows(j), lanes(g)], bands[j][0], NT) * scale, -1e30) for j, g in P}
        m = {(j, g): jnp.maximum(jnp.max(s[j, g], axis=1, keepdims=True), sk[g]) for j, g in P}
        p = {(j, g): jnp.exp(s[j, g] - m[j, g]) for j, g in P}
        den = {(j, g): jnp.sum(p[j, g], axis=1, keepdims=True) + jnp.exp(sk[g] - m[j, g]) for j, g in P}
        for j, g in P:
            o_ref[rows(j), lanes(g)] = _dot((p[j, g] / den[j, g]).astype(BF16), bands[j][1], NN).astype(o_ref.dtype)
            lse_ref[g, rows(j), :] = m[j, g] + jnp.log(den[j, g])

    return pl.pallas_call(
        body,
        name=name,
        grid=(SWA_KV_HEADS, S // SWA_ROWS),
        in_specs=_swa_in_specs(v_off),
        out_specs=[
            pl.BlockSpec((SWA_ROWS, gw), lambda h, n: (n, h)),
            pl.BlockSpec((SWA_GROUP, SWA_ROWS, 1), lambda h, n: (h, n, 0)),
        ],
        out_shape=[jax.ShapeDtypeStruct((S, SWA_HEADS * LANES), BF16), jax.ShapeDtypeStruct((SWA_HEADS, S, 1), F32)],
        compiler_params=_params(("parallel", "arbitrary")),
    )(q, k, k, v, v, sink_b)


def _swa_bwd(q, k, v, v_off, sink_b, o, lse, do, name):
    S = q.shape[0]
    assert S % SWA_ROWS == 0
    scale = SWA_HEAD_DIM ** -0.5
    gw = SWA_GROUP * LANES

    def body(q_ref, kp_ref, kc_ref, vp_ref, vc_ref, sink_ref, o_ref, lse_ref, do_ref, dq_ref, dk_ref, dv_ref, dsink_ref):
        n = pl.program_id(1)

        @pl.when(n == 0)
        def _():
            dk_ref[...] = jnp.zeros_like(dk_ref)
            dv_ref[...] = jnp.zeros_like(dv_ref)
            dsink_ref[...] = jnp.zeros_like(dsink_ref)

        bands = _swa_bands(n, kp_ref, kc_ref, vp_ref, vc_ref)
        P = [(j, g) for j in range(SWA_NB) for g in range(SWA_GROUP)]
        rows = lambda j: slice(j * SWA_BLK, (j + 1) * SWA_BLK)
        lanes = lambda g: slice(g * LANES, (g + 1) * LANES)
        qs = {(j, g): q_ref[rows(j), lanes(g)] for j, g in P}
        dos = {(j, g): do_ref[rows(j), lanes(g)] for j, g in P}
        lses = {(j, g): lse_ref[g, rows(j), :] for j, g in P}
        delta = {(j, g): jnp.sum(dos[j, g].astype(F32) * o_ref[rows(j), lanes(g)].astype(F32), axis=1, keepdims=True)
                 for j, g in P}
        s = {(j, g): jnp.where(bands[j][2], _dot(qs[j, g], bands[j][0], NT) * scale, -1e30) for j, g in P}
        p = {(j, g): jnp.exp(s[j, g] - lses[j, g]) for j, g in P}
        dp = {(j, g): _dot(dos[j, g], bands[j][1], NT) for j, g in P}
        ds = {(j, g): (p[j, g] * (dp[j, g] - delta[j, g]) * scale).astype(BF16) for j, g in P}
        for j, g in P:
            dq_ref[rows(j), lanes(g)] = _dot(ds[j, g], bands[j][0], NN)
        for g in range(SWA_GROUP):
            p_sink = [jnp.exp(sink_ref[:, g * LANES:g * LANES + 1] - lses[j, g]) * delta[j, g] for j in range(SWA_NB)]
            dsink_ref[:, lanes(g)] += jnp.zeros((1, LANES), F32) - jnp.sum(_total(p_sink), axis=0, keepdims=True)
        dkb = [_total(_dot(ds[j, g], qs[j, g], TN) for g in range(SWA_GROUP)) for j in range(SWA_NB)]
        dvb = [_total(_dot(p[j, g].astype(BF16), dos[j, g], TN) for g in range(SWA_GROUP)) for j in range(SWA_NB)]
        base = pl.multiple_of(n * SWA_ROWS, SWA_ROWS)
        for j in range(SWA_NB):
            own = pl.ds(base + j * SWA_BLK, SWA_BLK)
            after = j + 1 < SWA_NB
            dk_ref[own, :] += dkb[j][SWA_BLK:] + dkb[j + 1][:SWA_BLK] if after else dkb[j][SWA_BLK:]
            dv_ref[own, :] += dvb[j][SWA_BLK:] + dvb[j + 1][:SWA_BLK] if after else dvb[j][SWA_BLK:]

        @pl.when(n > 0)
        def _():
            before = pl.ds(pl.multiple_of(n * SWA_ROWS - SWA_BLK, SWA_BLK), SWA_BLK)
            dk_ref[before, :] += dkb[0][:SWA_BLK]
            dv_ref[before, :] += dvb[0][:SWA_BLK]

    return pl.pallas_call(
        body,
        name=name,
        grid=(SWA_KV_HEADS, S // SWA_ROWS),
        in_specs=_swa_in_specs(v_off) + [
            pl.BlockSpec((SWA_ROWS, gw), lambda h, n: (n, h)),
            pl.BlockSpec((SWA_GROUP, SWA_ROWS, 1), lambda h, n: (h, n, 0)),
            pl.BlockSpec((SWA_ROWS, gw), lambda h, n: (n, h)),
        ],
        out_specs=[
            pl.BlockSpec((SWA_ROWS, gw), lambda h, n: (n, h)),
            pl.BlockSpec((S, LANES), lambda h, n: (0, h)),
            pl.BlockSpec((S, LANES), lambda h, n: (0, h)),
            pl.BlockSpec((1, gw), lambda h, n: (0, h)),
        ],
        out_shape=[
            jax.ShapeDtypeStruct((S, SWA_HEADS * LANES), F32),
            jax.ShapeDtypeStruct((S, SWA_KV_HEADS * LANES), F32),
            jax.ShapeDtypeStruct((S, SWA_KV_HEADS * LANES), F32),
            jax.ShapeDtypeStruct((1, SWA_HEADS * LANES), F32),
        ],
        compiler_params=_params(("parallel", "arbitrary")),
    )(q, k, k, v, v, sink_b, o, lse, do)


def _pad_cols(w, heads, real):
    k = w.shape[0]
    return jnp.pad(w.reshape(k, heads, real), ((0, 0), (0, 0), (0, LANES - real))).reshape(k, heads * LANES)


def _unpad_cols(g, heads, real):
    k = g.shape[0]
    return g.reshape(k, heads, LANES)[:, :, :real].reshape(k, heads * real)


def _pad_rows(w, heads, real):
    n = w.shape[1]
    return jnp.pad(w.reshape(heads, real, n), ((0, 0), (0, LANES - real), (0, 0))).reshape(heads * LANES, n)


def _unpad_rows(g, heads, real):
    n = g.shape[1]
    return g.reshape(heads, LANES, n)[:, :real, :].reshape(heads * real, n)


def _w_in_internal(w_in):
    c_q, c_kv, k_r, q_swa, k_swa, v_swa, q_sb, k_sb, v_sb, gate = jnp.split(w_in, SPLIT_POINTS, axis=1)
    k_r = jnp.pad(k_r, ((0, 0), (MLA_NOPE, LANES - MLA_NOPE - MLA_ROPE)))
    w1 = jnp.concatenate([c_q, c_kv, k_r, _pad_cols(q_swa, 8, 64), _pad_cols(k_swa, 2, 64)], axis=1)
    w2 = [_pad_cols(v_swa, 2, 64), _pad_cols(q_sb, 8, 64), _pad_cols(k_sb, 8, 64), _pad_cols(v_sb, 8, 64)]
    return w1, w2, gate


def _w_in_reference(g1, g2, g3):
    c_q, c_kv, k_r, q_swa, k_swa = jnp.split(g1, [256, 384, 512, 1536], axis=1)
    v_swa, q_sb, k_sb, v_sb = g2
    return jnp.concatenate([
        c_q, c_kv, k_r[:, MLA_NOPE:MLA_NOPE + MLA_ROPE], _unpad_cols(q_swa, 8, 64), _unpad_cols(k_swa, 2, 64),
        _unpad_cols(v_swa, 2, 64), _unpad_cols(q_sb, 8, 64), _unpad_cols(k_sb, 8, 64), _unpad_cols(v_sb, 8, 64),
        g3], axis=1)


def _w_ukv_internal(w):
    w3 = w.reshape(MLA_KV_LORA, MLA_HEADS, MLA_NOPE + MLA_V)
    pad = lambda t: jnp.pad(t, ((0, 0), (0, 0), (0, LANES - t.shape[2]))).reshape(MLA_KV_LORA, MLA_HEADS * LANES)
    return pad(w3[:, :, :MLA_NOPE]), pad(w3[:, :, MLA_NOPE:])


def _w_ukv_reference(gk, gv):
    gk = gk.reshape(MLA_KV_LORA, MLA_HEADS, LANES)[:, :, :MLA_NOPE]
    gv = gv.reshape(MLA_KV_LORA, MLA_HEADS, LANES)[:, :, :MLA_V]
    return jnp.concatenate([gk, gv], axis=2).reshape(MLA_KV_LORA, MLA_HEADS * (MLA_NOPE + MLA_V))


def _layer_fwd(x, w, tabs):
    mla_tab, swa_tab = tabs
    sv = {"x": x}

    def f_norm(rows, consts):
        return [_rms(rows[0], consts[0])], []

    (h,) = _rowwise(f_norm, [x], [w["g_mix_pre"]], [(D_MODEL, BF16)], [], "norm_mix_pre")
    p1 = _matmul(h, w["w_in1"], "nn", [F32], "proj_lat")
    p2 = _matmul(h, w["w_in2"], "nn", [BF16], "proj_qkv")
    gates = _matmul(h, w["w_in3"], "nn", [BF16], "proj_gate",
                    epilogue=lambda acc, b: (1.0 / (1.0 + jnp.exp(-(acc + b))),), row_extras=[w["b_gate"]])

    def f_prep(rows, consts):
        t = rows[0]
        gq, gkv = consts[0], consts[1]
        mc, mu, md = rows[1], rows[2], rows[3]
        sc, su, sd = rows[4], rows[5], rows[6]
        cq_n = _rms(t[:, 0:256], gq)
        ckv_n = _rms(t[:, 256:384], gkv)
        kr = _rope(t[:, 384:512], mc, mu, md, MLA_ROPE // 2)
        qs = [_rope(t[:, 512 + j * LANES:512 + (j + 1) * LANES], sc, su, sd, SWA_HEAD_DIM // 2) for j in range(8)]
        ks = [_rope(t[:, 1536 + j * LANES:1536 + (j + 1) * LANES], sc, su, sd, SWA_HEAD_DIM // 2) for j in range(2)]
        return [cq_n, ckv_n, kr, jnp.concatenate(qs, axis=1), jnp.concatenate(ks, axis=1)], []

    cq_n, ckv_n, kr, q_swa, k_swa = _rowwise(
        f_prep, [p1, *mla_tab["k"], *swa_tab["f"]], [w["g_q_lat"], w["g_kv_lat"]],
        [(256, BF16), (128, BF16), (LANES, F32), (1024, BF16), (256, BF16)], [], "lat_prep")

    q_lat = _matmul(cq_n, w["w_uq"], "nn", [F32], "mla_q_up")
    k_lat = _matmul(ckv_n, w["w_ukv_k"], "nn", [F32], "mla_k_up")
    def ones_lane(acc):
        lane = lax.broadcasted_iota(jnp.int32, acc.shape, 1) % LANES
        return (jnp.where(lane == ONES_LANE, 1.0, acc),)

    v_mla = _matmul(ckv_n, w["w_ukv_v"], "nn", [BF16], "mla_v_up", epilogue=ones_lane)
    mla_scale = (MLA_NOPE + MLA_ROPE) ** -0.5

    def f_mla_prep(rows, consts):
        ql, kl, krr, mc, mu, md = rows
        qs = [_rope(ql[:, j * LANES:(j + 1) * LANES], mc, mu, md, MLA_ROPE // 2) * mla_scale for j in range(8)]
        ks = [kl[:, j * LANES:(j + 1) * LANES] + krr for j in range(8)]
        return [jnp.concatenate(qs, axis=1), jnp.concatenate(ks, axis=1)], []

    q_mla, k_mla = _rowwise(f_mla_prep, [q_lat, k_lat, kr, *mla_tab["q"]], [], [(1024, BF16), (1024, BF16)], [], "mla_prep")

    o_mla, lse_mla = _softmax_attn_fwd(q_mla, k_mla, v_mla, MLA_HEADS, "mla_fwd")
    o_swa, lse_swa = _swa_fwd(q_swa, k_swa, p2, 0, w["sink_b"], "swa_fwd")
    o_sb = _sb_attn_fwd(p2, SB_HEADS, "sb_fwd", 2, 10, 18)

    oa = _matmul(o_mla, w["w_o_mla"], "nn", [F32], "o_proj_mla")
    ob = _matmul(o_swa, w["w_o_swa"], "nn", [F32], "o_proj_swa")
    oc = _matmul(o_sb, w["w_o_sb"], "nn", [F32], "o_proj_sb")

    def f_mix(rows, consts):
        a, b, c, g = rows
        g = g.astype(F32)
        return [g[:, 0:1024] * a + g[:, 1024:2048] * b + g[:, 2048:3072] * c], []

    (mixed,) = _rowwise(f_mix, [oa, ob, oc, gates], [], [(D_MODEL, BF16)], [], "gate_mix")
    y = _matmul(mixed, w["w_out"], "nn", [F32], "out_proj")

    def f_res_norm(rows, consts):
        return [rows[0] + _rms(rows[1], consts[0])], []

    (x1,) = _rowwise(f_res_norm, [x, y], [w["g_mix_post"]], [(D_MODEL, F32)], [], "res_norm_mix")
    (h2,) = _rowwise(f_norm, [x1], [w["g_mlp_pre"]], [(D_MODEL, BF16)], [], "norm_mlp_pre")

    def relu2(acc):
        r = jnp.maximum(acc, 0.0)
        return acc, r * r

    up, u = _matmul(h2, w["w_up"], "nn", [BF16, BF16], "mlp_up", epilogue=relu2)
    zd = _matmul(u, w["w_down"], "nn", [F32], "mlp_down")
    (x2,) = _rowwise(f_res_norm, [x1, zd], [w["g_mlp_post"]], [(D_MODEL, F32)], [], "res_norm_mlp")

    sv.update(h=h, p1=p1, p2=p2, gates=gates, cq_n=cq_n, ckv_n=ckv_n, q_swa=q_swa, k_swa=k_swa, q_mla=q_mla,
              k_mla=k_mla, v_mla=v_mla, o_mla=o_mla, lse_mla=lse_mla, o_swa=o_swa, lse_swa=lse_swa, o_sb=o_sb,
              oa=oa, ob=ob, oc=oc, mixed=mixed, y=y, x1=x1, h2=h2, up=up, u=u, zd=zd)
    return x2, sv


def _layer_bwd(dx2, w, sv, tabs):
    mla_tab, swa_tab = tabs
    gr = {}

    def f_norm_bwd(rows, consts):
        dx, dg = _rms_bwd(rows[0], consts[0], rows[1])
        return [dx], [dg]

    def f_norm_bwd_res(rows, consts):
        dx, dg = _rms_bwd(rows[0], consts[0], rows[1])
        return [rows[2] + dx], [dg]

    dzd, gr["g_mlp_post"] = _rowwise(f_norm_bwd, [sv["zd"], dx2], [w["g_mlp_post"]], [(D_MODEL, BF16)], [D_MODEL], "b_norm_mlp_post")
    gr["w_down"] = _matmul(sv["u"], dzd, "tn", [F32], "b_w_down")
    dup = _matmul(dzd, w["w_down"], "nt", [BF16], "b_mlp_down",
                  epilogue=lambda acc, up: (acc * 2.0 * jnp.maximum(up.astype(F32), 0.0),), extras=[sv["up"]])
    gr["w_up"] = _matmul(sv["h2"], dup, "tn", [F32], "b_w_up")
    dh2 = _matmul(dup, w["w_up"], "nt", [F32], "b_mlp_up")
    dx1, gr["g_mlp_pre"] = _rowwise(f_norm_bwd_res, [sv["x1"], dh2, dx2], [w["g_mlp_pre"]], [(D_MODEL, F32)], [D_MODEL], "b_norm_mlp_pre")

    dy, gr["g_mix_post"] = _rowwise(f_norm_bwd, [sv["y"], dx1], [w["g_mix_post"]], [(D_MODEL, BF16)], [D_MODEL], "b_norm_mix_post")
    gr["w_out"] = _matmul(sv["mixed"], dy, "tn", [F32], "b_w_out")
    dmixed = _matmul(dy, w["w_out"], "nt", [F32], "b_out_proj")

    def f_mix_bwd(rows, consts):
        dm, a, b, c, g = rows
        g = g.astype(F32)
        outs, dls = [], []
        for j, o in enumerate((a, b, c)):
            gj = g[:, j * D_MODEL:(j + 1) * D_MODEL]
            outs.append(dm * gj)
            dls.append(dm * o * gj * (1.0 - gj))
        dl = jnp.concatenate(dls, axis=1)
        return outs + [dl], [dl]

    doa, dob, doc, dlogit, gr["b_gate"] = _rowwise(
        f_mix_bwd, [dmixed, sv["oa"], sv["ob"], sv["oc"], sv["gates"]], [],
        [(D_MODEL, BF16)] * 3 + [(P3_W, BF16)], [P3_W], "b_gate_mix")

    gr["w_o_mla"] = _matmul(sv["o_mla"], doa, "tn", [F32], "b_w_o_mla")
    gr["w_o_swa"] = _matmul(sv["o_swa"], dob, "tn", [F32], "b_w_o_swa")
    gr["w_o_sb"] = _matmul(sv["o_sb"], doc, "tn", [F32], "b_w_o_sb")
    do_mla = _matmul(doa, w["w_o_mla"], "nt", [BF16], "b_o_proj_mla")
    do_swa = _matmul(dob, w["w_o_swa"], "nt", [BF16], "b_o_proj_swa")
    do_sb = _matmul(doc, w["w_o_sb"], "nt", [BF16], "b_o_proj_sb")

    dq_sb, dk_sb, dv_sb = _sb_attn_bwd(sv["p2"], sv["o_sb"], do_sb, SB_HEADS, "sb_bwd", 2, 10, 18)
    dq_swa, dk_swa, dv_swa, dsink = _swa_bwd(sv["q_swa"], sv["k_swa"], sv["p2"], 0, w["sink_b"], sv["o_swa"],
                                             sv["lse_swa"], do_swa, "swa_bwd")
    gr["swa_sinks"] = dsink.reshape(SWA_HEADS, LANES)[:, 0]
    dq_mla, dk_mla, dv_mla = _softmax_attn_bwd(sv["q_mla"], sv["k_mla"], sv["v_mla"], sv["o_mla"], sv["lse_mla"], do_mla,
                                               MLA_HEADS, (MLA_NOPE + MLA_ROPE) ** -0.5, "mla_bwd")

    def f_mla_post(rows, consts):
        dq, dk, qc, qu, qd, kc, ku, kd = rows
        dqs = [_rope(dq[:, j * LANES:(j + 1) * LANES], qc, qu, qd, MLA_ROPE // 2) for j in range(8)]
        dkr = dk[:, 0:LANES]
        for j in range(1, 8):
            dkr = dkr + dk[:, j * LANES:(j + 1) * LANES]
        return [jnp.concatenate(dqs, axis=1), _rope(dkr, kc, ku, kd, MLA_ROPE // 2)], []

    dq_lat, dkr = _rowwise(f_mla_post, [dq_mla, dk_mla, *mla_tab["q_inv"], *mla_tab["k_inv"]], [],
                           [(1024, BF16), (LANES, F32)], [], "b_mla_post")
    gr["w_uq"] = _matmul(sv["cq_n"], dq_lat, "tn", [F32], "b_w_uq")
    gr["w_ukv_k"] = _matmul(sv["ckv_n"], dk_mla, "tn", [F32], "b_w_ukv_k")
    gr["w_ukv_v"] = _matmul(sv["ckv_n"], dv_mla, "tn", [F32], "b_w_ukv_v")
    dcq_n = _matmul(dq_lat, w["w_uq"], "nt", [F32], "b_mla_q_up")
    dckv_a = _matmul(dk_mla, w["w_ukv_k"], "nt", [F32], "b_mla_k_up")
    dckv_b = _matmul(dv_mla, w["w_ukv_v"], "nt", [F32], "b_mla_v_up")

    def f_prep_bwd(rows, consts):
        t, dcq, dca, dcb, dkr_, dqs, dks, sc, su, sd = rows
        gq, gkv = consts
        dc_q, dgq = _rms_bwd(t[:, 0:256], gq, dcq)
        dc_kv, dgkv = _rms_bwd(t[:, 256:384], gkv, dca + dcb)
        q_parts = [_rope(dqs[:, j * LANES:(j + 1) * LANES], sc, su, sd, SWA_HEAD_DIM // 2) for j in range(8)]
        k_parts = [_rope(dks[:, j * LANES:(j + 1) * LANES], sc, su, sd, SWA_HEAD_DIM // 2) for j in range(2)]
        return [jnp.concatenate([dc_q, dc_kv, dkr_] + q_parts + k_parts, axis=1)], [dgq, dgkv]

    dp1, gr["g_q_lat"], gr["g_kv_lat"] = _rowwise(
        f_prep_bwd, [sv["p1"], dcq_n, dckv_a, dckv_b, dkr, dq_swa, dk_swa, *swa_tab["inv"]], [w["g_q_lat"], w["g_kv_lat"]],
        [(P1_W, BF16)], [256, 128], "b_lat_prep")

    gr["w_in1"] = _matmul(sv["h"], dp1, "tn", [F32], "b_w_in_lat")
    dh = _matmul(dp1, w["w_in1"], "nt", [F32], "b_proj_lat")
    gr["w_in2"] = []
    add_prev = lambda acc, prev: (acc + prev,)
    for piece, wp, tag in zip((dv_swa, dq_sb, dk_sb, dv_sb), w["w_in2_parts"], ("vswa", "qsb", "ksb", "vsb")):
        gr["w_in2"].append(_matmul(sv["h"], piece, "tn", [F32], "b_w_in_" + tag))
        dh = _matmul(piece, wp, "nt", [F32], "b_proj_" + tag, epilogue=add_prev, extras=[dh])
    gr["w_in3"] = _matmul(sv["h"], dlogit, "tn", [F32], "b_w_in_gate")
    dh = _matmul(dlogit, w["w_in3"], "nt", [F32], "b_proj_gate", epilogue=add_prev, extras=[dh])
    dx, gr["g_mix_pre"] = _rowwise(f_norm_bwd_res, [sv["x"], dh, dx1], [w["g_mix_pre"]], [(D_MODEL, F32)], [D_MODEL], "b_norm_mix_pre")
    return dx, gr


def _local_step(x, positions, loss_target, full):
    mc, mu, md = _rope_tables(positions, MLA_NOPE, MLA_ROPE, True)
    kc, ku, kd = _rope_tables(positions, MLA_NOPE, MLA_ROPE, False)
    sc, su, sd = _rope_tables(positions, 0, SWA_HEAD_DIM, False)
    mla_tab = {"q": (mc, mu, md), "k": (kc, ku, kd), "q_inv": (mc, -mu, -md), "k_inv": (kc, -ku, -kd)}
    swa_tab = {"f": (sc, su, sd), "inv": (sc, -su, -sd)}
    tabs = (mla_tab, swa_tab)

    layers = []
    for l in range(DEPTH):
        w1, w2, w3 = _w_in_internal(full["w_in"][l].astype(BF16))
        uk, uv = _w_ukv_internal(full["w_ukv"][l].astype(BF16))
        layers.append({
            "w_in1": w1, "w_in2": jnp.concatenate(w2, axis=1), "w_in2_parts": w2, "w_in3": w3,
            "w_uq": _pad_cols(full["w_uq"][l].astype(BF16), MLA_HEADS, MLA_NOPE + MLA_ROPE),
            "w_ukv_k": uk, "w_ukv_v": uv,
            "w_o_mla": _pad_rows(full["w_o_mla"][l].astype(BF16), 8, 64),
            "w_o_swa": _pad_rows(full["w_o_swa"][l].astype(BF16), 8, 64),
            "w_o_sb": _pad_rows(full["w_o_sb"][l].astype(BF16), 8, 64),
            "w_out": full["w_out"][l].astype(BF16), "w_up": full["w_up"][l].astype(BF16),
            "w_down": full["w_down"][l].astype(BF16),
            "g_mix_pre": full["g_mix_pre"][l][None], "b_gate": full["b_gate"][l][None],
            "g_q_lat": full["g_q_lat"][l][None], "g_kv_lat": full["g_kv_lat"][l][None],
            "g_mix_post": full["g_mix_post"][l][None], "g_mlp_pre": full["g_mlp_pre"][l][None],
            "g_mlp_post": full["g_mlp_post"][l][None],
            "sink_b": jnp.repeat(full["swa_sinks"][l], LANES)[None],
        })

    saved = []
    h = x
    for l in range(DEPTH):
        h, sv = _layer_fwd(h, layers[l], tabs)
        saved.append(sv)

    def f_loss(rows, consts):
        err = rows[0] - rows[1]
        return [err * (1.0 / D_MODEL)], [jnp.sum(err * err, axis=1, keepdims=True)]

    dy, sq = _rowwise(f_loss, [h, loss_target], [], [(D_MODEL, F32)], [1], "loss_head")
    loss_part = sq * (0.5 / D_MODEL)

    grads = [None] * DEPTH
    d = dy
    for l in reversed(range(DEPTH)):
        d, gr = _layer_bwd(d, layers[l], saved[l], tabs)
        grads[l] = {
            "g_mix_pre": gr["g_mix_pre"][0], "w_in": _w_in_reference(gr["w_in1"], gr["w_in2"], gr["w_in3"]),
            "b_gate": gr["b_gate"][0], "g_q_lat": gr["g_q_lat"][0], "g_kv_lat": gr["g_kv_lat"][0],
            "w_uq": _unpad_cols(gr["w_uq"], MLA_HEADS, MLA_NOPE + MLA_ROPE),
            "w_ukv": _w_ukv_reference(gr["w_ukv_k"], gr["w_ukv_v"]), "swa_sinks": gr["swa_sinks"],
            "w_o_mla": _unpad_rows(gr["w_o_mla"], 8, 64), "w_o_swa": _unpad_rows(gr["w_o_swa"], 8, 64),
            "w_o_sb": _unpad_rows(gr["w_o_sb"], 8, 64), "w_out": gr["w_out"], "g_mix_post": gr["g_mix_post"][0],
            "g_mlp_pre": gr["g_mlp_pre"][0], "w_up": gr["w_up"], "w_down": gr["w_down"], "g_mlp_post": gr["g_mlp_post"][0],
        }
    stacked = {n: jnp.stack([grads[l][n] for l in range(DEPTH)]) for n in WEIGHTS}
    return loss_part, d, stacked


def _rows_of(a):
    return a.reshape(-1, LANES)


def _small_rows(d):
    parts = []
    for n in SMALL:
        a = d[n]
        if a.shape[1] < LANES:
            a = jnp.pad(a, ((0, 0), (0, LANES - a.shape[1])))
        parts.append(_rows_of(a))
    return parts


def _pack(shards, small, dtype):
    parts = [_rows_of(shards[n]) for n in SHARDED]
    if small is not None:
        parts += _small_rows(small)
    slab = jnp.concatenate(parts, axis=0).astype(dtype)
    pad = (-slab.shape[0]) % SLAB_ROW_ALIGN
    return jnp.pad(slab, ((0, pad), (0, 0)))


def _unpack(slab, shard_shapes, small_shapes):
    out, r = {}, 0
    for n in SHARDED:
        rows = int(np.prod(shard_shapes[n])) // LANES
        out[n] = slab[r:r + rows].reshape(shard_shapes[n])
        r += rows
    if small_shapes is not None:
        for n in SMALL:
            depth, width = small_shapes[n]
            rows = depth * max(width, LANES) // LANES
            out[n] = slab[r:r + rows].reshape(depth, max(width, LANES))[:, :width]
            r += rows
    return out


def _chip_exchange(src, name):
    rows = src.shape[-2]

    def body(src_ref, out_ref, send_sems, recv_sems):
        x, y, c = lax.axis_index("x"), lax.axis_index("y"), lax.axis_index("c")
        me = 2 * x + y
        chips = [(1 - x, y), (x, 1 - y), (1 - x, 1 - y)]
        sends = []
        for k, (cx, cy) in enumerate(chips):
            cp = pltpu.make_async_remote_copy(
                src_ref=src_ref.at[2 * cx + cy], dst_ref=out_ref.at[me], send_sem=send_sems.at[k],
                recv_sem=recv_sems.at[k], device_id=(cx, cy, c), device_id_type=pl.DeviceIdType.MESH)
            cp.start()
            sends.append(cp)
        for k, (cx, cy) in enumerate(chips):
            pltpu.make_async_remote_copy(
                src_ref=src_ref.at[me], dst_ref=out_ref.at[2 * cx + cy], send_sem=send_sems.at[k],
                recv_sem=recv_sems.at[k], device_id=(cx, cy, c), device_id_type=pl.DeviceIdType.MESH).wait_recv()
        for cp in sends:
            cp.wait_send()

    return pl.pallas_call(
        body,
        name=name,
        in_specs=[pl.BlockSpec(memory_space=pl.ANY)],
        out_specs=pl.BlockSpec(memory_space=pl.ANY),
        out_shape=jax.ShapeDtypeStruct((N_CHIPS, rows, LANES), src.dtype),
        scratch_shapes=[pltpu.SemaphoreType.DMA((3,)), pltpu.SemaphoreType.DMA((3,))],
    )(src)


def _half_rows(c, half):
    return pl.ds(pl.multiple_of(c * half, SLAB_ROW_ALIGN // 2), half)


def _gather_weights(src, name):
    rows = src.shape[0]
    half = rows // 2

    def body(src_ref, out_ref, send_sems, recv_sems):
        x, y, c = lax.axis_index("x"), lax.axis_index("y"), lax.axis_index("c")
        me = 2 * x + y
        chips = [(1 - x, y), (x, 1 - y), (1 - x, 1 - y)]

        def copy(k, src_view, slab, part, to):
            return pltpu.make_async_remote_copy(
                src_ref=src_view, dst_ref=out_ref.at[slab, _half_rows(part, half), :], send_sem=send_sems.at[k],
                recv_sem=recv_sems.at[k], device_id=to, device_id_type=pl.DeviceIdType.MESH)

        sends = [copy(k, src_ref.at[_half_rows(c, half), :], me, c, (cx, cy, c)) for k, (cx, cy) in enumerate(chips)]
        for cp in sends:
            cp.start()
        for k, (cx, cy) in enumerate(chips):
            j = 2 * cx + cy
            landed = out_ref.at[j, _half_rows(c, half), :]
            copy(k, landed, j, c, (cx, cy, c)).wait_recv()
            fwd = copy(3 + k, landed, j, c, (x, y, 1 - c))
            fwd.start()
            sends.append(fwd)
        for k, (cx, cy) in enumerate(chips):
            j = 2 * cx + cy
            copy(3 + k, out_ref.at[j, _half_rows(1 - c, half), :], j, 1 - c, (x, y, 1 - c)).wait_recv()
        for cp in sends:
            cp.wait_send()

    return pl.pallas_call(
        body,
        name=name,
        in_specs=[pl.BlockSpec(memory_space=pl.ANY)],
        out_specs=pl.BlockSpec(memory_space=pl.ANY),
        out_shape=jax.ShapeDtypeStruct((N_CHIPS, rows, LANES), src.dtype),
        scratch_shapes=[pltpu.SemaphoreType.DMA((6,)), pltpu.SemaphoreType.DMA((6,))],
    )(src)


def _sibling_halves(src, name):
    n, rows, _ = src.shape
    half = rows // 2

    def body(src_ref, out_ref, send_sem, recv_sem):
        x, y, c = lax.axis_index("x"), lax.axis_index("y"), lax.axis_index("c")
        cp = pltpu.make_async_remote_copy(
            src_ref=src_ref.at[:, _half_rows(1 - c, half), :], dst_ref=out_ref, send_sem=send_sem, recv_sem=recv_sem,
            device_id=(x, y, 1 - c), device_id_type=pl.DeviceIdType.MESH)
        cp.start()
        cp.wait()

    return pl.pallas_call(
        body,
        name=name,
        in_specs=[pl.BlockSpec(memory_space=pl.ANY)],
        out_specs=pl.BlockSpec(memory_space=pl.ANY),
        out_shape=jax.ShapeDtypeStruct((n, half, LANES), src.dtype),
        scratch_shapes=[pltpu.SemaphoreType.DMA, pltpu.SemaphoreType.DMA],
    )(src)


def _sibling_join(src, name):
    half = src.shape[0]

    def body(src_ref, out_ref, send_sem, recv_sem):
        x, y, c = lax.axis_index("x"), lax.axis_index("y"), lax.axis_index("c")
        cp = pltpu.make_async_remote_copy(
            src_ref=src_ref, dst_ref=out_ref.at[_half_rows(c, half), :], send_sem=send_sem, recv_sem=recv_sem,
            device_id=(x, y, 1 - c), device_id_type=pl.DeviceIdType.MESH)
        cp.start()
        pltpu.make_async_remote_copy(
            src_ref=src_ref, dst_ref=out_ref.at[_half_rows(1 - c, half), :], send_sem=send_sem, recv_sem=recv_sem,
            device_id=(x, y, 1 - c), device_id_type=pl.DeviceIdType.MESH).wait_recv()
        cp.wait_send()

    return pl.pallas_call(
        body,
        name=name,
        in_specs=[pl.BlockSpec(memory_space=pl.ANY)],
        out_specs=pl.BlockSpec(memory_space=pl.ANY),
        out_shape=jax.ShapeDtypeStruct((2 * half, LANES), src.dtype),
        scratch_shapes=[pltpu.SemaphoreType.DMA, pltpu.SemaphoreType.DMA],
    )(src)


SUM_ROWS = 1024


def _pair_sum(mine, theirs, c, name):
    n, half, _ = theirs.shape
    blocks = half // SUM_ROWS

    def body(c_ref, a_ref, b_ref, o_ref):
        o_ref[...] = (a_ref[...].astype(F32) + b_ref[...].astype(F32)).astype(o_ref.dtype)

    return pl.pallas_call(
        body,
        name=name,
        grid_spec=pltpu.PrefetchScalarGridSpec(
            num_scalar_prefetch=1,
            grid=(blocks,),
            in_specs=[pl.BlockSpec((n, SUM_ROWS, LANES), lambda i, c_ref: (0, c_ref[0] * blocks + i, 0)),
                      pl.BlockSpec((n, SUM_ROWS, LANES), lambda i, c_ref: (0, i, 0))],
            out_specs=pl.BlockSpec((n, SUM_ROWS, LANES), lambda i, c_ref: (0, i, 0)),
        ),
        out_shape=jax.ShapeDtypeStruct((n, half, LANES), BF16),
        compiler_params=_params(("arbitrary",)),
    )(jnp.reshape(c, (1,)).astype(jnp.int32), mine, theirs)


def _sum_chips(own, landed, me, name):
    rows = landed.shape[1]

    def body(me_ref, a_ref, b_ref, o_ref):
        t = [jnp.where(me_ref[0] == j, a_ref[j], b_ref[j]).astype(F32) for j in range(N_CHIPS)]
        o_ref[...] = ((t[0] + t[1]) + t[2]) + t[3]

    slabs = pl.BlockSpec((N_CHIPS, SUM_ROWS, LANES), lambda i, me_ref: (0, i, 0))
    return pl.pallas_call(
        body,
        name=name,
        grid_spec=pltpu.PrefetchScalarGridSpec(
            num_scalar_prefetch=1,
            grid=(rows // SUM_ROWS,),
            in_specs=[slabs, slabs],
            out_specs=pl.BlockSpec((SUM_ROWS, LANES), lambda i, me_ref: (i, 0)),
        ),
        out_shape=jax.ShapeDtypeStruct((rows, LANES), F32),
        compiler_params=_params(("arbitrary",)),
    )(jnp.reshape(me, (1,)).astype(jnp.int32), own, landed)


def _adamw(w, m, v, g, name):
    shape = w.shape
    flat = lambda a: a.reshape(-1, shape[-1])

    def fn(rows, consts):
        w_, m_, v_, g_ = rows
        m_new = ADAM_B1 * m_ + (1.0 - ADAM_B1) * g_
        v_new = ADAM_B2 * v_ + (1.0 - ADAM_B2) * (g_ * g_)
        m_hat = m_new / (1.0 - ADAM_B1 ** ADAM_STEP)
        v_hat = v_new / (1.0 - ADAM_B2 ** ADAM_STEP)
        delta = -ADAM_LR * (m_hat / (jnp.sqrt(v_hat) + ADAM_EPS) + ADAM_WD * w_)
        return [delta, m_new, v_new], []

    outs = _rowwise(fn, [flat(w), flat(m), flat(v), flat(g)], [], [(shape[-1], F32)] * 3, [], name, bm=256)
    return [o.reshape(shape) for o in outs]


def kernel(x, positions, g_mix_pre, w_in, b_gate, g_q_lat, g_kv_lat, w_uq, w_ukv, swa_sinks, w_o_mla, w_o_swa, w_o_sb, w_out, g_mix_post, g_mlp_pre, w_up, w_down, g_mlp_post, loss_target, m_g_mix_pre, m_w_in, m_b_gate, m_g_q_lat, m_g_kv_lat, m_w_uq, m_w_ukv, m_swa_sinks, m_w_o_mla, m_w_o_swa, m_w_o_sb, m_w_out, m_g_mix_post, m_g_mlp_pre, m_w_up, m_w_down, m_g_mlp_post, v_g_mix_pre, v_w_in, v_b_gate, v_g_q_lat, v_g_kv_lat, v_w_uq, v_w_ukv, v_swa_sinks, v_w_o_mla, v_w_o_swa, v_w_o_sb, v_w_out, v_g_mix_post, v_g_mlp_pre, v_w_up, v_w_down, v_g_mlp_post):
    given = dict(locals())
    wts = {n: given[n] for n in WEIGHTS}
    mom_m = {n: given["m_" + n] for n in WEIGHTS}
    mom_v = {n: given["v_" + n] for n in WEIGHTS}
    shard_shapes = {n: wts[n].shape for n in SHARDED}
    small_shapes = {n: wts[n].shape for n in SMALL}

    me = 2 * lax.axis_index("x") + lax.axis_index("y")
    core = lax.axis_index("c")
    gathered = _gather_weights(_pack(wts, None, BF16), "gather_weights")
    full = {n: wts[n] for n in SMALL}
    per_chip = [_unpack(gathered[j], shard_shapes, None) for j in range(N_CHIPS)]
    for n in SHARDED:
        own = wts[n].astype(BF16)
        full[n] = jnp.concatenate([jnp.where(me == j, own, per_chip[j][n]) for j in range(N_CHIPS)], axis=SHARD_AXIS[n])

    loss_part, grad_x, grads = _local_step(x[0], positions[0], loss_target[0], full)
    loss = lax.psum(loss_part[0, 0], ("x", "y", "c"))

    small_g = {n: grads[n] for n in SMALL}
    slabs = []
    for j in range(N_CHIPS):
        shard = {n: jnp.split(grads[n], N_CHIPS, axis=SHARD_AXIS[n])[j] for n in SHARDED}
        slabs.append(_pack(shard, small_g, BF16))
    per_chip_g = jnp.stack(slabs)
    theirs = _sibling_halves(per_chip_g, "pair_grads")
    pair = _pair_sum(per_chip_g, theirs, core, "sum_pair")
    landed = _chip_exchange(pair, "scatter_grads")
    my_half = _sum_chips(pair, landed, me, "sum_chips")
    g_slab = lax.dynamic_update_slice(_sibling_join(my_half, "join_grads"), my_half, (core * my_half.shape[0], 0))

    g = _unpack(g_slab, shard_shapes, small_shapes)
    stepped = {n: _adamw(wts[n], mom_m[n], mom_v[n], g[n], "adamw_" + n) for n in WEIGHTS}
    outs = [loss, grad_x[None]] + [g[n] for n in WEIGHTS]
    for part in range(3):
        outs += [stepped[n][part] for n in WEIGHTS]
    return tuple(outs)
```

```python
import numpy as np
import jax
import jax.numpy as jnp
from jax import lax
from jax.experimental import pallas as pl
from jax.experimental.pallas import tpu as pltpu

F32 = jnp.float32
BF16 = jnp.bfloat16

D_MODEL = 1024
DEPTH = 4
MLA_HEADS, MLA_Q_LORA, MLA_KV_LORA, MLA_NOPE, MLA_ROPE, MLA_V = 8, 256, 128, 64, 32, 64
SWA_HEADS, SWA_KV_HEADS, SWA_HEAD_DIM, SWA_WINDOW = 8, 2, 64, 128
SB_HEADS, SB_HEAD_DIM = 8, 64
D_FF = 4 * D_MODEL
ROPE_THETA = 10000.0
EPS = 1e-6
SPLIT_SIZES = (256, 128, 32, 512, 128, 128, 512, 512, 512, 3 * D_MODEL)
SPLIT_POINTS = [int(v) for v in np.cumsum(SPLIT_SIZES)[:-1]]

ADAM_LR, ADAM_B1, ADAM_B2, ADAM_EPS, ADAM_WD, ADAM_STEP = 0.001, 0.9, 0.999, 1e-08, 0.01, 10

LANES = 128
V7X_VMEM_BYTES = 64 * 1024 * 1024
VMEM_LIMIT = V7X_VMEM_BYTES - 8 * 1024 * 1024
MATMUL_VMEM_BUDGET = 36 * 1024 * 1024
N_CHIPS = 4
SLAB_ROW_ALIGN = 512

P1_W = 256 + 128 + 128 + 1024 + 256
P2_W = 256 + 1024 + 1024 + 1024
P3_W = 3 * D_MODEL

SHARDED = ("w_in", "w_uq", "w_ukv", "w_o_mla", "w_o_swa", "w_o_sb", "w_out", "w_up", "w_down")
SHARD_AXIS = {"w_in": 2, "w_uq": 2, "w_ukv": 2, "w_o_mla": 2, "w_o_swa": 2, "w_o_sb": 2, "w_out": 1, "w_up": 2, "w_down": 1}
SMALL = ("g_mix_pre", "b_gate", "g_q_lat", "g_kv_lat", "swa_sinks", "g_mix_post", "g_mlp_pre", "g_mlp_post")
WEIGHTS = ("g_mix_pre", "w_in", "b_gate", "g_q_lat", "g_kv_lat", "w_uq", "w_ukv", "swa_sinks", "w_o_mla", "w_o_swa",
           "w_o_sb", "w_out", "g_mix_post", "g_mlp_pre", "w_up", "w_down", "g_mlp_post")

NN = (((1,), (0,)), ((), ()))
NT = (((1,), (1,)), ((), ()))
TN = (((0,), (0,)), ((), ()))


def _dot(a, b, dims):
    return lax.dot_general(a, b, dims, preferred_element_type=F32)


def _params(sem):
    return pltpu.CompilerParams(dimension_semantics=sem, vmem_limit_bytes=VMEM_LIMIT)


def _largest_tile(n, cap):
    if n <= cap:
        return n
    best = LANES
    for t in range(LANES, cap + 1, LANES):
        if n % t == 0:
            best = t
    return best


def _matmul_tiles(M, N, K, a_bytes, b_bytes, out_bytes, extra_bytes):
    tn = _largest_tile(N, 1792)
    tm = _largest_tile(M, 1024 if tn <= 1024 else 512)
    tk = _largest_tile(K, 2048)

    def need(tm_, tk_):
        acc = 4 * tm_ * tn if tk_ < K else 0
        return 2 * (tm_ * tk_ * a_bytes + tk_ * tn * b_bytes + tm_ * tn * (out_bytes + extra_bytes)) + acc

    while need(tm, tk) > MATMUL_VMEM_BUDGET:
        if tk >= tm and tk % 256 == 0:
            tk //= 2
        elif tm % 256 == 0:
            tm //= 2
        else:
            break
    return tm, tn, tk


def _matmul(a, b, mode, out_dtypes, name, epilogue=None, extras=(), row_extras=()):
    if mode == "nn":
        (M, K), (K2, N) = a.shape, b.shape
    elif mode == "nt":
        (M, K), (N, K2) = a.shape, b.shape
    else:
        (K, M), (K2, N) = a.shape, b.shape
    assert K == K2, (name, a.shape, b.shape)
    tm, tn, tk = _matmul_tiles(
        M, N, K, a.dtype.itemsize, b.dtype.itemsize, sum(jnp.dtype(d).itemsize for d in out_dtypes),
        sum(e.dtype.itemsize for e in extras))
    assert M % tm == 0 and N % tn == 0 and K % tk == 0, (name, M, N, K, tm, tn, tk)
    nk = K // tk
    if mode == "tn":
        a_spec = pl.BlockSpec((tk, tm), lambda i, j, k: (k, i))
    else:
        a_spec = pl.BlockSpec((tm, tk), lambda i, j, k: (i, k))
    if mode == "nt":
        b_spec = pl.BlockSpec((tn, tk), lambda i, j, k: (j, k))
    else:
        b_spec = pl.BlockSpec((tk, tn), lambda i, j, k: (k, j))
    dims = {"nn": NN, "nt": NT, "tn": TN}[mode]
    n_ex, n_rex, n_out = len(extras), len(row_extras), len(out_dtypes)

    def body(*refs):
        a_ref, b_ref = refs[:2]
        ex = refs[2:2 + n_ex]
        rex = refs[2 + n_ex:2 + n_ex + n_rex]
        outs = refs[2 + n_ex + n_rex:2 + n_ex + n_rex + n_out]

        def finish(total):
            res = (total,) if epilogue is None else epilogue(total, *[e[...] for e in ex], *[e[...] for e in rex])
            for o, r in zip(outs, res):
                o[...] = r.astype(o.dtype)

        part = _dot(a_ref[...].astype(BF16), b_ref[...].astype(BF16), dims)
        if nk == 1:
            finish(part)
            return
        acc = refs[-1]
        k = pl.program_id(2)

        @pl.when(k == 0)
        def _():
            acc[...] = part

        @pl.when(k > 0)
        def _():
            acc[...] += part

        @pl.when(k == nk - 1)
        def _():
            finish(acc[...])

    in_specs = [a_spec, b_spec]
    in_specs += [pl.BlockSpec((tm, tn), lambda i, j, k: (i, j)) for _ in extras]
    in_specs += [pl.BlockSpec((1, tn), lambda i, j, k: (0, j)) for _ in row_extras]
    out = pl.pallas_call(
        body,
        name=name,
        grid=(M // tm, N // tn, nk),
        in_specs=in_specs,
        out_specs=[pl.BlockSpec((tm, tn), lambda i, j, k: (i, j)) for _ in out_dtypes],
        out_shape=[jax.ShapeDtypeStruct((M, N), dt) for dt in out_dtypes],
        scratch_shapes=[pltpu.VMEM((tm, tn), F32)] if nk > 1 else [],
        compiler_params=_params(("parallel", "parallel", "arbitrary")),
    )(a, b, *extras, *row_extras)
    return out[0] if n_out == 1 else out


ROWWISE_ROW_BYTES = 16 * 1024


def _rowwise(fn, rows, consts, out_defs, sum_widths, name):
    R = rows[0].shape[0]
    per_row = sum(r.shape[1] * r.dtype.itemsize for r in rows) + sum(w * jnp.dtype(dt).itemsize for w, dt in out_defs)
    bm = 512 if per_row <= ROWWISE_ROW_BYTES else 256
    while R % bm:
        bm //= 2
    bm = max(bm, 1)
    n_r, n_c, n_o = len(rows), len(consts), len(out_defs)
    n_s = len(sum_widths)

    def body(*refs):
        r_in = refs[:n_r]
        c_in = refs[n_r:n_r + n_c]
        o_refs = refs[n_r + n_c:n_r + n_c + n_o]
        s_refs = refs[n_r + n_c + n_o:]
        outs, sums = fn([r[...] for r in r_in], [c[...] for c in c_in])
        for o, val in zip(o_refs, outs):
            o[...] = val.astype(o.dtype)
        if n_s:
            @pl.when(pl.program_id(0) == 0)
            def _():
                for s in s_refs:
                    s[...] = jnp.zeros_like(s)

            for s, val in zip(s_refs, sums):
                s[...] += jnp.sum(val, axis=0, keepdims=True)

    in_specs = [pl.BlockSpec((bm, r.shape[1]), lambda i: (i, 0)) for r in rows]
    in_specs += [pl.BlockSpec(c.shape, lambda i: (0, 0)) for c in consts]
    out_specs = [pl.BlockSpec((bm, w), lambda i: (i, 0)) for w, _ in out_defs]
    out_specs += [pl.BlockSpec((1, w), lambda i: (0, 0)) for w in sum_widths]
    out_shape = [jax.ShapeDtypeStruct((R, w), dt) for w, dt in out_defs]
    out_shape += [jax.ShapeDtypeStruct((1, w), F32) for w in sum_widths]
    return pl.pallas_call(
        body,
        name=name,
        grid=(R // bm,),
        in_specs=in_specs,
        out_specs=out_specs,
        out_shape=out_shape,
        compiler_params=_params(("arbitrary",)),
    )(*rows, *consts)


def _rms(x, g):
    r = lax.rsqrt(jnp.mean(x * x, axis=-1, keepdims=True) + EPS)
    return x * r * g


def _rms_bwd(x, g, dy):
    r = lax.rsqrt(jnp.mean(x * x, axis=-1, keepdims=True) + EPS)
    n = x * r
    dn = dy * g
    dx = r * (dn - n * jnp.mean(dn * n, axis=-1, keepdims=True))
    return dx, dy * n


def _rope(x, c, s_up, s_dn, half):
    return x * c + pltpu.roll(x, half, 1) * s_up + pltpu.roll(x, LANES - half, 1) * s_dn


def _rope_tables(positions, lo, d, nope_pass):
    S = positions.shape[0]
    half = d // 2
    inv = 1.0 / (ROPE_THETA ** (jnp.arange(0, d, 2, dtype=F32) / d))
    ang = positions.astype(F32)[:, None] * inv
    cos, sin = jnp.cos(ang), jnp.sin(ang)
    z = lambda n: jnp.zeros((S, n), F32)
    head = jnp.ones((S, lo), F32) if nope_pass else z(lo)
    tail = LANES - lo - d
    c = jnp.concatenate([head, cos, cos, z(tail)], axis=1)
    s_up = jnp.concatenate([z(lo), z(half), sin, z(tail)], axis=1)
    s_dn = jnp.concatenate([z(lo), -sin, z(half), z(tail)], axis=1)
    return c, s_up, s_dn


MLA_FWD_CFG = (2, 1024)
MLA_BWD_CFG = (2, 512)
SB_FWD_CFG = (2, 256)
SB_BWD_CFG = (4, 256)


def _tile_mask(bk, strict):
    row = lax.broadcasted_iota(jnp.int32, (bk, bk), 0)
    col = lax.broadcasted_iota(jnp.int32, (bk, bk), 1)
    return (col < row) if strict else (col <= row)


def _att_layout(S, cfg):
    nch, bk = cfg
    bq = nch * bk
    assert S % bq == 0, (S, cfg)
    rows = [slice(r * bk, (r + 1) * bk) for r in range(nch)]
    q_spec = lambda off=0: pl.BlockSpec((bq, LANES), lambda h, i: (i, off + h))
    kv_spec = lambda off=0: pl.BlockSpec((S, LANES), lambda h, i: (0, off + h))
    return bq, rows, q_spec, kv_spec


def _total(terms):
    terms = list(terms)
    out = terms[0]
    for t in terms[1:]:
        out = out + t
    return out


def _walk(nch, i, step, carry, leftward, alive=None):
    everyone = range(nch)
    if leftward:
        for d in reversed(everyone):
            carry = step(nch * i + d, carry, range(d, nch), {d})
        if alive is None:
            return lax.fori_loop(0, nch * i, lambda t, c: step(nch * i - 1 - t, c, everyone, set()), carry)
        more = lambda tc: jnp.logical_and(tc[0] < nch * i, alive(tc[1]))
        left = lambda tc: (tc[0] + 1, step(nch * i - 1 - tc[0], tc[1], everyone, set()))
        return lax.while_loop(more, left, (jnp.int32(0), carry))[1]
    carry = lax.fori_loop(0, nch * i, lambda kb, c: step(kb, c, everyone, set()), carry)
    for d in everyone:
        carry = step(nch * i + d, carry, range(d, nch), {d})
    return carry


ONES_LANE = MLA_V


def _softmax_attn_fwd(q, k, v, heads, name, q_off=0, k_off=0, v_off=0):
    S = q.shape[0]
    nch, bk = MLA_FWD_CFG
    bq, rows, q_spec, kv_spec = _att_layout(S, MLA_FWD_CFG)

    def body(q_ref, k_ref, v_ref, o_ref, lse_ref):
        i = pl.program_id(1)
        qs = [q_ref[rw, :] for rw in rows]

        def step(kb, cs, active, masked):
            off = pl.multiple_of(kb * bk, bk)
            ks, vs = k_ref[pl.ds(off, bk), :], v_ref[pl.ds(off, bk), :]
            A = list(active)
            s = {r: _dot(qs[r], ks, NT) for r in A}
            s = {r: (jnp.where(_tile_mask(bk, False), s[r], -1e30) if r in masked else s[r]) for r in A}
            m_new = {r: jnp.maximum(cs[r][0], jnp.max(s[r], axis=1, keepdims=True)) for r in A}
            p = {r: jnp.exp(s[r] - m_new[r]) for r in A}
            alpha = {r: jnp.exp(cs[r][0] - m_new[r]) for r in A}
            new = list(cs)
            for r in A:
                new[r] = (m_new[r], alpha[r] * cs[r][1] + _dot(p[r].astype(BF16), vs, NN))
            return tuple(new)

        init = (jnp.full((bk, 1), -1e30, F32), jnp.zeros((bk, LANES), F32))
        cs = _walk(nch, i, step, tuple(init for _ in rows), False)
        for r, (m, acc) in enumerate(cs):
            l = acc[:, ONES_LANE:ONES_LANE + 1]
            o_ref[rows[r], :] = (acc / l).astype(o_ref.dtype)
            lse_ref[rows[r], :] = m + jnp.log(l)

    return pl.pallas_call(
        body,
        name=name,
        grid=(heads, S // bq),
        in_specs=[q_spec(q_off), kv_spec(k_off), kv_spec(v_off)],
        out_specs=[q_spec(), pl.BlockSpec((None, bq, 1), lambda h, i: (h, i, 0))],
        out_shape=[jax.ShapeDtypeStruct((S, heads * LANES), BF16), jax.ShapeDtypeStruct((heads, S, 1), F32)],
        compiler_params=_params(("parallel", "arbitrary")),
    )(q, k, v)


def _softmax_attn_bwd(q, k, v, o, lse, do, heads, scale, name, q_off=0, k_off=0, v_off=0):
    S = q.shape[0]
    nch, bk = MLA_BWD_CFG
    bq, rows, q_spec, kv_spec = _att_layout(S, MLA_BWD_CFG)

    def body(q_ref, k_ref, v_ref, o_ref, lse_ref, do_ref, dq_ref, dk_ref, dv_ref):
        i = pl.program_id(1)

        @pl.when(i == 0)
        def _():
            dk_ref[...] = jnp.zeros_like(dk_ref)
            dv_ref[...] = jnp.zeros_like(dv_ref)

        qs = [q_ref[rw, :] for rw in rows]
        dos = [do_ref[rw, :] for rw in rows]
        lses = [lse_ref[rw, :] for rw in rows]
        deltas = [jnp.sum(dos[r].astype(F32) * o_ref[rows[r], :].astype(F32), axis=1, keepdims=True) for r in range(nch)]

        def step(kb, dqs, active, masked):
            off = pl.multiple_of(kb * bk, bk)
            ks, vs = k_ref[pl.ds(off, bk), :], v_ref[pl.ds(off, bk), :]
            A = list(active)
            s = {r: _dot(qs[r], ks, NT) for r in A}
            s = {r: (jnp.where(_tile_mask(bk, False), s[r], -1e30) if r in masked else s[r]) for r in A}
            p = {r: jnp.exp(s[r] - lses[r]) for r in A}
            dp = {r: _dot(dos[r], vs, NT) for r in A}
            ds = {r: (p[r] * (dp[r] - deltas[r])).astype(BF16) for r in A}
            dv_c = _total(_dot(p[r].astype(BF16), dos[r], TN) for r in A)
            dk_c = _total(_dot(ds[r], qs[r], TN) for r in A)
            dk_ref[pl.ds(off, bk), :] += dk_c
            dv_ref[pl.ds(off, bk), :] += dv_c
            new = list(dqs)
            for r in A:
                new[r] = dqs[r] + _dot(ds[r], ks, NN)
            return tuple(new)

        dqs = _walk(nch, i, step, tuple(jnp.zeros((bk, LANES), F32) for _ in rows), False)
        for r in range(nch):
            dq_ref[rows[r], :] = dqs[r] * scale

    return pl.pallas_call(
        body,
        name=name,
        grid=(heads, S // bq),
        in_specs=[q_spec(q_off), kv_spec(k_off), kv_spec(v_off), q_spec(),
                  pl.BlockSpec((None, bq, 1), lambda h, i: (h, i, 0)), q_spec()],
        out_specs=[q_spec(), kv_spec(), kv_spec()],
        out_shape=[jax.ShapeDtypeStruct((S, heads * LANES), F32)] * 3,
        compiler_params=_params(("parallel", "arbitrary")),
    )(q, k, v, o, lse, do)


def _tri(n, inclusive):
    r = lax.broadcasted_iota(jnp.int32, (n, n), 0)
    c = lax.broadcasted_iota(jnp.int32, (n, n), 1)
    return jnp.where((r >= c) if inclusive else (r > c), 1.0, 0.0).astype(BF16)


def _suffix_sum(x, tri):
    hi = x.astype(BF16)
    lo = (x - hi.astype(F32)).astype(BF16)
    return _dot(hi, tri, NN) + _dot(lo, tri, NN)


def _sb_logs(z):
    lg = jnp.log(1.0 + jnp.exp(-jnp.abs(z)))
    l1m = -(jnp.maximum(z, 0.0) + lg)
    return l1m, l1m + z


SB_SCALE = SB_HEAD_DIM ** -0.5
assert SB_SCALE == 0.125
SB_DEAD = -110.0


def _sb_alive(cs):
    top = cs[0][0]
    for c in cs[1:]:
        top = jnp.maximum(top, c[0])
    return jnp.max(top) > SB_DEAD


def _sb_attn_fwd(qkv, heads, name, q_off, k_off, v_off):
    S = qkv.shape[0]
    nch, bk = SB_FWD_CFG
    bq, rows, q_spec, kv_spec = _att_layout(S, SB_FWD_CFG)

    def body(q_ref, k_ref, v_ref, o_ref):
        i = pl.program_id(1)
        qs = [q_ref[rw, :] * SB_SCALE for rw in rows]
        tri = _tri(bk, False)

        def step(kb, cs, active, masked):
            off = pl.multiple_of(kb * bk, bk)
            ks, vs = k_ref[pl.ds(off, bk), :], v_ref[pl.ds(off, bk), :]
            A = list(active)
            lg = {r: _sb_logs(_dot(qs[r], ks, NT)) for r in A}
            l1m = {r: (jnp.where(_tile_mask(bk, True), lg[r][0], 0.0) if r in masked else lg[r][0]) for r in A}
            suf = {r: _suffix_sum(l1m[r], tri) for r in A}
            ex = {r: lg[r][1] + cs[r][0] + suf[r] for r in A}
            ex = {r: (jnp.where(_tile_mask(bk, True), ex[r], -1e30) if r in masked else ex[r]) for r in A}
            ab = {r: jnp.exp(ex[r]).astype(BF16) for r in A}
            new = list(cs)
            for r in A:
                new[r] = (cs[r][0] + jnp.sum(l1m[r], axis=1, keepdims=True), cs[r][1] + _dot(ab[r], vs, NN))
            return tuple(new)

        init = (jnp.zeros((bk, 1), F32), jnp.zeros((bk, LANES), F32))
        cs = _walk(nch, i, step, tuple(init for _ in rows), True, _sb_alive)
        for r in range(nch):
            o_ref[rows[r], :] = cs[r][1]

    return pl.pallas_call(
        body,
        name=name,
        grid=(heads, S // bq),
        in_specs=[q_spec(q_off), kv_spec(k_off), kv_spec(v_off)],
        out_specs=q_spec(),
        out_shape=jax.ShapeDtypeStruct((S, heads * LANES), F32),
        compiler_params=_params(("parallel", "arbitrary")),
    )(qkv, qkv, qkv)


def _sb_attn_bwd(qkv, o, do, heads, name, q_off, k_off, v_off):
    S = qkv.shape[0]
    nch, bk = SB_BWD_CFG
    bq, rows, q_spec, kv_spec = _att_layout(S, SB_BWD_CFG)

    def body(q_ref, k_ref, v_ref, o_ref, do_ref, dq_ref, dk_ref, dv_ref):
        i = pl.program_id(1)

        @pl.when(i == 0)
        def _():
            dk_ref[...] = jnp.zeros_like(dk_ref)
            dv_ref[...] = jnp.zeros_like(dv_ref)

        tri = _tri(bk, False)
        qs = [q_ref[rw, :] * SB_SCALE for rw in rows]
        dos = [do_ref[rw, :] for rw in rows]
        deltas = [jnp.sum(dos[r].astype(F32) * o_ref[rows[r], :], axis=1, keepdims=True) for r in range(nch)]

        def step(kb, cs, active, masked):
            off = pl.multiple_of(kb * bk, bk)
            ks, vs = k_ref[pl.ds(off, bk), :], v_ref[pl.ds(off, bk), :]
            A = list(active)
            lg = {r: _sb_logs(_dot(qs[r], ks, NT)) for r in A}
            l1m = {r: (jnp.where(_tile_mask(bk, True), lg[r][0], 0.0) if r in masked else lg[r][0]) for r in A}
            suf = {r: _suffix_sum(l1m[r], tri) for r in A}
            ex = {r: lg[r][1] + cs[r][0] + suf[r] for r in A}
            ex = {r: (jnp.where(_tile_mask(bk, True), ex[r], -1e30) if r in masked else ex[r]) for r in A}
            ab = {r: jnp.exp(ex[r]).astype(BF16) for r in A}
            da = {r: _dot(dos[r], vs, NT) for r in A}
            g = {r: ab[r].astype(F32) * da[r] for r in A}
            gs = {r: _suffix_sum(g[r], tri) for r in A}
            beta = {r: jnp.exp(lg[r][1]) for r in A}
            dz = {r: g[r] - beta[r] * (deltas[r] - cs[r][1] - gs[r]) for r in A}
            dz = {r: (jnp.where(_tile_mask(bk, True), dz[r], 0.0) if r in masked else dz[r]) for r in A}
            dzb = {r: dz[r].astype(BF16) for r in A}
            dv_c = _total(_dot(ab[r], dos[r], TN) for r in A)
            dk_c = _total(_dot(dzb[r], qs[r], TN) for r in A)
            dk_ref[pl.ds(off, bk), :] += dk_c
            dv_ref[pl.ds(off, bk), :] += dv_c
            new = list(cs)
            for r in A:
                new[r] = (cs[r][0] + jnp.sum(l1m[r], axis=1, keepdims=True),
                          cs[r][1] + jnp.sum(g[r], axis=1, keepdims=True), cs[r][2] + _dot(dzb[r], ks, NN))
            return tuple(new)

        zcol = jnp.zeros((bk, 1), F32)
        init = (zcol, zcol, jnp.zeros((bk, LANES), F32))
        cs = _walk(nch, i, step, tuple(init for _ in rows), True, _sb_alive)
        for r in range(nch):
            dq_ref[rows[r], :] = cs[r][2] * SB_SCALE

    return pl.pallas_call(
        body,
        name=name,
        grid=(heads, S // bq),
        in_specs=[q_spec(q_off), kv_spec(k_off), kv_spec(v_off), q_spec(), q_spec()],
        out_specs=[q_spec(), kv_spec(), kv_spec()],
        out_shape=[jax.ShapeDtypeStruct((S, heads * LANES), F32)] * 3,
        compiler_params=_params(("parallel", "arbitrary")),
    )(qkv, qkv, qkv, o, do)


SWA_BLK = 128
SWA_GROUP = SWA_HEADS // SWA_KV_HEADS


SWA_NB = 4
SWA_ROWS = SWA_NB * SWA_BLK


def _swa_band_mask(first):
    row = lax.broadcasted_iota(jnp.int32, (SWA_BLK, 2 * SWA_BLK), 0)
    col = lax.broadcasted_iota(jnp.int32, (SWA_BLK, 2 * SWA_BLK), 1)
    return (col > row) & (col <= row + SWA_WINDOW) & (jnp.logical_not(first) | (col >= SWA_BLK))


def _swa_in_specs(v_off):
    gw = SWA_GROUP * LANES
    before = lambda h, n: (jnp.maximum(SWA_NB * n - 1, 0), h)
    return [
        pl.BlockSpec((SWA_ROWS, gw), lambda h, n: (n, h)),
        pl.BlockSpec((SWA_BLK, LANES), before),
        pl.BlockSpec((SWA_ROWS, LANES), lambda h, n: (n, h)),
        pl.BlockSpec((SWA_BLK, LANES), lambda h, n: (jnp.maximum(SWA_NB * n - 1, 0), v_off + h)),
        pl.BlockSpec((SWA_ROWS, LANES), lambda h, n: (n, v_off + h)),
        pl.BlockSpec((1, gw), lambda h, n: (0, h)),
    ]


def _swa_bands(n, kp_ref, kc_ref, vp_ref, vc_ref):
    k_all = jnp.concatenate([kp_ref[...], kc_ref[...]], axis=0)
    v_all = jnp.concatenate([vp_ref[...], vc_ref[...]], axis=0)
    bands = []
    for j in range(SWA_NB):
        rows = slice(j * SWA_BLK, (j + 2) * SWA_BLK)
        bands.append((k_all[rows], v_all[rows], _swa_band_mask((n == 0) if j == 0 else False)))
    return bands


def _swa_fwd(q, k, v, v_off, sink_b, name):
    S = q.shape[0]
    assert S % SWA_ROWS == 0
    scale = SWA_HEAD_DIM ** -0.5
    gw = SWA_GROUP * LANES

    def body(q_ref, kp_ref, kc_ref, vp_ref, vc_ref, sink_ref, o_ref, lse_ref):
        n = pl.program_id(1)
        bands = _swa_bands(n, kp_ref, kc_ref, vp_ref, vc_ref)
        P = [(j, g) for j in range(SWA_NB) for g in range(SWA_GROUP)]
        rows = lambda j: slice(j * SWA_BLK, (j + 1) * SWA_BLK)
        lanes = lambda g: slice(g * LANES, (g + 1) * LANES)
        sk = {g: sink_ref[:, g * LANES:g * LANES + 1] for g in range(SWA_GROUP)}
        s = {(j, g): jnp.where(bands[j][2], _dot(q_ref[rows(j), lanes(g)], bands[j][0], NT) * scale, -1e30) for j, g in P}
        m = {(j, g): jnp.maximum(jnp.max(s[j, g], axis=1, keepdims=True), sk[g]) for j, g in P}
        p = {(j, g): jnp.exp(s[j, g] - m[j, g]) for j, g in P}
        den = {(j, g): jnp.sum(p[j, g], axis=1, keepdims=True) + jnp.exp(sk[g] - m[j, g]) for j, g in P}
        for j, g in P:
            o_ref[rows(j), lanes(g)] = _dot((p[j, g] / den[j, g]).astype(BF16), bands[j][1], NN).astype(o_ref.dtype)
            lse_ref[g, rows(j), :] = m[j, g] + jnp.log(den[j, g])

    return pl.pallas_call(
        body,
        name=name,
        grid=(SWA_KV_HEADS, S // SWA_ROWS),
        in_specs=_swa_in_specs(v_off),
        out_specs=[
            pl.BlockSpec((SWA_ROWS, gw), lambda h, n: (n, h)),
            pl.BlockSpec((SWA_GROUP, SWA_ROWS, 1), lambda h, n: (h, n, 0)),
        ],
        out_shape=[jax.ShapeDtypeStruct((S, SWA_HEADS * LANES), BF16), jax.ShapeDtypeStruct((SWA_HEADS, S, 1), F32)],
        compiler_params=_params(("parallel", "arbitrary")),
    )(q, k, k, v, v, sink_b)


def _swa_bwd(q, k, v, v_off, sink_b, o, lse, do, name):
    S = q.shape[0]
    assert S % SWA_ROWS == 0
    scale = SWA_HEAD_DIM ** -0.5
    gw = SWA_GROUP * LANES

    def body(q_ref, kp_ref, kc_ref, vp_ref, vc_ref, sink_ref, o_ref, lse_ref, do_ref, dq_ref, dk_ref, dv_ref, dsink_ref):
        n = pl.program_id(1)

        @pl.when(n == 0)
        def _():
            dk_ref[...] = jnp.zeros_like(dk_ref)
            dv_ref[...] = jnp.zeros_like(dv_ref)
            dsink_ref[...] = jnp.zeros_like(dsink_ref)

        bands = _swa_bands(n, kp_ref, kc_ref, vp_ref, vc_ref)
        P = [(j, g) for j in range(SWA_NB) for g in range(SWA_GROUP)]
        rows = lambda j: slice(j * SWA_BLK, (j + 1) * SWA_BLK)
        lanes = lambda g: slice(g * LANES, (g + 1) * LANES)
        qs = {(j, g): q_ref[rows(j), lanes(g)] for j, g in P}
        dos = {(j, g): do_ref[rows(j), lanes(g)] for j, g in P}
        lses = {(j, g): lse_ref[g, rows(j), :] for j, g in P}
        delta = {(j, g): jnp.sum(dos[j, g].astype(F32) * o_ref[rows(j), lanes(g)].astype(F32), axis=1, keepdims=True)
                 for j, g in P}
        s = {(j, g): jnp.where(bands[j][2], _dot(qs[j, g], bands[j][0], NT) * scale, -1e30) for j, g in P}
        p = {(j, g): jnp.exp(s[j, g] - lses[j, g]) for j, g in P}
        dp = {(j, g): _dot(dos[j, g], bands[j][1], NT) for j, g in P}
        ds = {(j, g): (p[j, g] * (dp[j, g] - delta[j, g]) * scale).astype(BF16) for j, g in P}
        for j, g in P:
            dq_ref[rows(j), lanes(g)] = _dot(ds[j, g], bands[j][0], NN)
        for g in range(SWA_GROUP):
            p_sink = [jnp.exp(sink_ref[:, g * LANES:g * LANES + 1] - lses[j, g]) * delta[j, g] for j in range(SWA_NB)]
            dsink_ref[:, lanes(g)] += jnp.zeros((1, LANES), F32) - jnp.sum(_total(p_sink), axis=0, keepdims=True)
        dkb = [_total(_dot(ds[j, g], qs[j, g], TN) for g in range(SWA_GROUP)) for j in range(SWA_NB)]
        dvb = [_total(_dot(p[j, g].astype(BF16), dos[j, g], TN) for g in range(SWA_GROUP)) for j in range(SWA_NB)]
        base = pl.multiple_of(n * SWA_ROWS, SWA_ROWS)
        for j in range(SWA_NB):
            own = pl.ds(base + j * SWA_BLK, SWA_BLK)
            after = j + 1 < SWA_NB
            dk_ref[own, :] += dkb[j][SWA_BLK:] + dkb[j + 1][:SWA_BLK] if after else dkb[j][SWA_BLK:]
            dv_ref[own, :] += dvb[j][SWA_BLK:] + dvb[j + 1][:SWA_BLK] if after else dvb[j][SWA_BLK:]

        @pl.when(n > 0)
        def _():
            before = pl.ds(pl.multiple_of(n * SWA_ROWS - SWA_BLK, SWA_BLK), SWA_BLK)
            dk_ref[before, :] += dkb[0][:SWA_BLK]
            dv_ref[before, :] += dvb[0][:SWA_BLK]

    return pl.pallas_call(
        body,
        name=name,
        grid=(SWA_KV_HEADS, S // SWA_ROWS),
        in_specs=_swa_in_specs(v_off) + [
            pl.BlockSpec((SWA_ROWS, gw), lambda h, n: (n, h)),
            pl.BlockSpec((SWA_GROUP, SWA_ROWS, 1), lambda h, n: (h, n, 0)),
            pl.BlockSpec((SWA_ROWS, gw), lambda h, n: (n, h)),
        ],
        out_specs=[
            pl.BlockSpec((SWA_ROWS, gw), lambda h, n: (n, h)),
            pl.BlockSpec((S, LANES), lambda h, n: (0, h)),
            pl.BlockSpec((S, LANES), lambda h, n: (0, h)),
            pl.BlockSpec((1, gw), lambda h, n: (0, h)),
        ],
        out_shape=[
            jax.ShapeDtypeStruct((S, SWA_HEADS * LANES), F32),
            jax.ShapeDtypeStruct((S, SWA_KV_HEADS * LANES), F32),
            jax.ShapeDtypeStruct((S, SWA_KV_HEADS * LANES), F32),
            jax.ShapeDtypeStruct((1, SWA_HEADS * LANES), F32),
        ],
        compiler_params=_params(("parallel", "arbitrary")),
    )(q, k, k, v, v, sink_b, o, lse, do)


def _pad_cols(w, heads, real):
    k = w.shape[0]
    return jnp.pad(w.reshape(k, heads, real), ((0, 0), (0, 0), (0, LANES - real))).reshape(k, heads * LANES)


def _unpad_cols(g, heads, real):
    k = g.shape[0]
    return g.reshape(k, heads, LANES)[:, :, :real].reshape(k, heads * real)


def _pad_rows(w, heads, real):
    n = w.shape[1]
    return jnp.pad(w.reshape(heads, real, n), ((0, 0), (0, LANES - real), (0, 0))).reshape(heads * LANES, n)


def _unpad_rows(g, heads, real):
    n = g.shape[1]
    return g.reshape(heads, LANES, n)[:, :real, :].reshape(heads * real, n)


def _w_in_internal(w_in):
    c_q, c_kv, k_r, q_swa, k_swa, v_swa, q_sb, k_sb, v_sb, gate = jnp.split(w_in, SPLIT_POINTS, axis=1)
    k_r = jnp.pad(k_r, ((0, 0), (MLA_NOPE, LANES - MLA_NOPE - MLA_ROPE)))
    w1 = jnp.concatenate([c_q, c_kv, k_r, _pad_cols(q_swa, 8, 64), _pad_cols(k_swa, 2, 64)], axis=1)
    w2 = [_pad_cols(v_swa, 2, 64), _pad_cols(q_sb, 8, 64), _pad_cols(k_sb, 8, 64), _pad_cols(v_sb, 8, 64)]
    return w1, w2, gate


def _w_in_reference(g1, g2, g3):
    c_q, c_kv, k_r, q_swa, k_swa = jnp.split(g1, [256, 384, 512, 1536], axis=1)
    v_swa, q_sb, k_sb, v_sb = g2
    return jnp.concatenate([
        c_q, c_kv, k_r[:, MLA_NOPE:MLA_NOPE + MLA_ROPE], _unpad_cols(q_swa, 8, 64), _unpad_cols(k_swa, 2, 64),
        _unpad_cols(v_swa, 2, 64), _unpad_cols(q_sb, 8, 64), _unpad_cols(k_sb, 8, 64), _unpad_cols(v_sb, 8, 64),
        g3], axis=1)


def _w_ukv_internal(w):
    w3 = w.reshape(MLA_KV_LORA, MLA_HEADS, MLA_NOPE + MLA_V)
    pad = lambda t: jnp.pad(t, ((0, 0), (0, 0), (0, LANES - t.shape[2]))).reshape(MLA_KV_LORA, MLA_HEADS * LANES)
    return pad(w3[:, :, :MLA_NOPE]), pad(w3[:, :, MLA_NOPE:])


def _w_ukv_reference(gk, gv):
    gk = gk.reshape(MLA_KV_LORA, MLA_HEADS, LANES)[:, :, :MLA_NOPE]
    gv = gv.reshape(MLA_KV_LORA, MLA_HEADS, LANES)[:, :, :MLA_V]
    return jnp.concatenate([gk, gv], axis=2).reshape(MLA_KV_LORA, MLA_HEADS * (MLA_NOPE + MLA_V))


def _layer_fwd(x, w, tabs):
    mla_tab, swa_tab = tabs
    sv = {"x": x}

    def f_norm(rows, consts):
        return [_rms(rows[0], consts[0])], []

    (h,) = _rowwise(f_norm, [x], [w["g_mix_pre"]], [(D_MODEL, BF16)], [], "norm_mix_pre")
    p1 = _matmul(h, w["w_in1"], "nn", [F32], "proj_lat")
    p2 = _matmul(h, w["w_in2"], "nn", [BF16], "proj_qkv")
    gates = _matmul(h, w["w_in3"], "nn", [BF16], "proj_gate",
                    epilogue=lambda acc, b: (1.0 / (1.0 + jnp.exp(-(acc + b))),), row_extras=[w["b_gate"]])

    def f_prep(rows, consts):
        t = rows[0]
        gq, gkv = consts[0], consts[1]
        mc, mu, md = rows[1], rows[2], rows[3]
        sc, su, sd = rows[4], rows[5], rows[6]
        cq_n = _rms(t[:, 0:256], gq)
        ckv_n = _rms(t[:, 256:384], gkv)
        kr = _rope(t[:, 384:512], mc, mu, md, MLA_ROPE // 2)
        qs = [_rope(t[:, 512 + j * LANES:512 + (j + 1) * LANES], sc, su, sd, SWA_HEAD_DIM // 2) for j in range(8)]
        ks = [_rope(t[:, 1536 + j * LANES:1536 + (j + 1) * LANES], sc, su, sd, SWA_HEAD_DIM // 2) for j in range(2)]
        return [cq_n, ckv_n, kr, jnp.concatenate(qs, axis=1), jnp.concatenate(ks, axis=1)], []

    cq_n, ckv_n, kr, q_swa, k_swa = _rowwise(
        f_prep, [p1, *mla_tab["k"], *swa_tab["f"]], [w["g_q_lat"], w["g_kv_lat"]],
        [(256, BF16), (128, BF16), (LANES, F32), (1024, BF16), (256, BF16)], [], "lat_prep")

    q_lat = _matmul(cq_n, w["w_uq"], "nn", [F32], "mla_q_up")
    k_lat = _matmul(ckv_n, w["w_ukv_k"], "nn", [F32], "mla_k_up")
    def ones_lane(acc):
        lane = lax.broadcasted_iota(jnp.int32, acc.shape, 1) % LANES
        return (jnp.where(lane == ONES_LANE, 1.0, acc),)

    v_mla = _matmul(ckv_n, w["w_ukv_v"], "nn", [BF16], "mla_v_up", epilogue=ones_lane)
    mla_scale = (MLA_NOPE + MLA_ROPE) ** -0.5

    def f_mla_prep(rows, consts):
        ql, kl, krr, mc, mu, md = rows
        qs = [_rope(ql[:, j * LANES:(j + 1) * LANES], mc, mu, md, MLA_ROPE // 2) * mla_scale for j in range(8)]
        ks = [kl[:, j * LANES:(j + 1) * LANES] + krr for j in range(8)]
        return [jnp.concatenate(qs, axis=1), jnp.concatenate(ks, axis=1)], []

    q_mla, k_mla = _rowwise(f_mla_prep, [q_lat, k_lat, kr, *mla_tab["q"]], [], [(1024, BF16), (1024, BF16)], [], "mla_prep")

    o_mla, lse_mla = _softmax_attn_fwd(q_mla, k_mla, v_mla, MLA_HEADS, "mla_fwd")
    o_swa, lse_swa = _swa_fwd(q_swa, k_swa, p2, 0, w["sink_b"], "swa_fwd")
    o_sb = _sb_attn_fwd(p2, SB_HEADS, "sb_fwd", 2, 10, 18)

    oa = _matmul(o_mla, w["w_o_mla"], "nn", [F32], "o_proj_mla")
    ob = _matmul(o_swa, w["w_o_swa"], "nn", [F32], "o_proj_swa")
    oc = _matmul(o_sb, w["w_o_sb"], "nn", [F32], "o_proj_sb")

    def f_mix(rows, consts):
        a, b, c, g = rows
        g = g.astype(F32)
        return [g[:, 0:1024] * a + g[:, 1024:2048] * b + g[:, 2048:3072] * c], []

    (mixed,) = _rowwise(f_mix, [oa, ob, oc, gates], [], [(D_MODEL, BF16)], [], "gate_mix")
    y = _matmul(mixed, w["w_out"], "nn", [F32], "out_proj")

    def f_res_norm(rows, consts):
        return [rows[0] + _rms(rows[1], consts[0])], []

    (x1,) = _rowwise(f_res_norm, [x, y], [w["g_mix_post"]], [(D_MODEL, F32)], [], "res_norm_mix")
    (h2,) = _rowwise(f_norm, [x1], [w["g_mlp_pre"]], [(D_MODEL, BF16)], [], "norm_mlp_pre")

    def relu2(acc):
        r = jnp.maximum(acc, 0.0)
        return acc, r * r

    up, u = _matmul(h2, w["w_up"], "nn", [BF16, BF16], "mlp_up", epilogue=relu2)
    zd = _matmul(u, w["w_down"], "nn", [F32], "mlp_down")
    (x2,) = _rowwise(f_res_norm, [x1, zd], [w["g_mlp_post"]], [(D_MODEL, F32)], [], "res_norm_mlp")

    sv.update(h=h, p1=p1, p2=p2, gates=gates, cq_n=cq_n, ckv_n=ckv_n, q_swa=q_swa, k_swa=k_swa, q_mla=q_mla,
              k_mla=k_mla, v_mla=v_mla, o_mla=o_mla, lse_mla=lse_mla, o_swa=o_swa, lse_swa=lse_swa, o_sb=o_sb,
              oa=oa, ob=ob, oc=oc, mixed=mixed, y=y, x1=x1, h2=h2, up=up, u=u, zd=zd)
    return x2, sv


def _layer_bwd(dx2, w, sv, tabs):
    mla_tab, swa_tab = tabs
    gr = {}

    def f_norm_bwd(rows, consts):
        dx, dg = _rms_bwd(rows[0], consts[0], rows[1])
        return [dx], [dg]

    def f_norm_bwd_res(rows, consts):
        dx, dg = _rms_bwd(rows[0], consts[0], rows[1])
        return [rows[2] + dx], [dg]

    dzd, gr["g_mlp_post"] = _rowwise(f_norm_bwd, [sv["zd"], dx2], [w["g_mlp_post"]], [(D_MODEL, BF16)], [D_MODEL], "b_norm_mlp_post")
    gr["w_down"] = _matmul(sv["u"], dzd, "tn", [F32], "b_w_down")
    dup = _matmul(dzd, w["w_down"], "nt", [BF16], "b_mlp_down",
                  epilogue=lambda acc, up: (acc * 2.0 * jnp.maximum(up.astype(F32), 0.0),), extras=[sv["up"]])
    gr["w_up"] = _matmul(sv["h2"], dup, "tn", [F32], "b_w_up")
    dh2 = _matmul(dup, w["w_up"], "nt", [F32], "b_mlp_up")
    dx1, gr["g_mlp_pre"] = _rowwise(f_norm_bwd_res, [sv["x1"], dh2, dx2], [w["g_mlp_pre"]], [(D_MODEL, F32)], [D_MODEL], "b_norm_mlp_pre")

    dy, gr["g_mix_post"] = _rowwise(f_norm_bwd, [sv["y"], dx1], [w["g_mix_post"]], [(D_MODEL, BF16)], [D_MODEL], "b_norm_mix_post")
    gr["w_out"] = _matmul(sv["mixed"], dy, "tn", [F32], "b_w_out")
    dmixed = _matmul(dy, w["w_out"], "nt", [F32], "b_out_proj")

    def f_mix_bwd(rows, consts):
        dm, a, b, c, g = rows
        g = g.astype(F32)
        outs, dls = [], []
        for j, o in enumerate((a, b, c)):
            gj = g[:, j * D_MODEL:(j + 1) * D_MODEL]
            outs.append(dm * gj)
            dls.append(dm * o * gj * (1.0 - gj))
        dl = jnp.concatenate(dls, axis=1)
        return outs + [dl], [dl]

    doa, dob, doc, dlogit, gr["b_gate"] = _rowwise(
        f_mix_bwd, [dmixed, sv["oa"], sv["ob"], sv["oc"], sv["gates"]], [],
        [(D_MODEL, BF16)] * 3 + [(P3_W, BF16)], [P3_W], "b_gate_mix")

    gr["w_o_mla"] = _matmul(sv["o_mla"], doa, "tn", [F32], "b_w_o_mla")
    gr["w_o_swa"] = _matmul(sv["o_swa"], dob, "tn", [F32], "b_w_o_swa")
    gr["w_o_sb"] = _matmul(sv["o_sb"], doc, "tn", [F32], "b_w_o_sb")
    do_mla = _matmul(doa, w["w_o_mla"], "nt", [BF16], "b_o_proj_mla")
    do_swa = _matmul(dob, w["w_o_swa"], "nt", [BF16], "b_o_proj_swa")
    do_sb = _matmul(doc, w["w_o_sb"], "nt", [BF16], "b_o_proj_sb")

    dq_sb, dk_sb, dv_sb = _sb_attn_bwd(sv["p2"], sv["o_sb"], do_sb, SB_HEADS, "sb_bwd", 2, 10, 18)
    dq_swa, dk_swa, dv_swa, dsink = _swa_bwd(sv["q_swa"], sv["k_swa"], sv["p2"], 0, w["sink_b"], sv["o_swa"],
                                             sv["lse_swa"], do_swa, "swa_bwd")
    gr["swa_sinks"] = dsink.reshape(SWA_HEADS, LANES)[:, 0]
    dq_mla, dk_mla, dv_mla = _softmax_attn_bwd(sv["q_mla"], sv["k_mla"], sv["v_mla"], sv["o_mla"], sv["lse_mla"], do_mla,
                                               MLA_HEADS, (MLA_NOPE + MLA_ROPE) ** -0.5, "mla_bwd")

    def f_mla_post(rows, consts):
        dq, dk, qc, qu, qd, kc, ku, kd = rows
        dqs = [_rope(dq[:, j * LANES:(j + 1) * LANES], qc, qu, qd, MLA_ROPE // 2) for j in range(8)]
        dkr = dk[:, 0:LANES]
        for j in range(1, 8):
            dkr = dkr + dk[:, j * LANES:(j + 1) * LANES]
        return [jnp.concatenate(dqs, axis=1), _rope(dkr, kc, ku, kd, MLA_ROPE // 2)], []

    dq_lat, dkr = _rowwise(f_mla_post, [dq_mla, dk_mla, *mla_tab["q_inv"], *mla_tab["k_inv"]], [],
                           [(1024, BF16), (LANES, F32)], [], "b_mla_post")
    gr["w_uq"] = _matmul(sv["cq_n"], dq_lat, "tn", [F32], "b_w_uq")
    gr["w_ukv_k"] = _matmul(sv["ckv_n"], dk_mla, "tn", [F32], "b_w_ukv_k")
    gr["w_ukv_v"] = _matmul(sv["ckv_n"], dv_mla, "tn", [F32], "b_w_ukv_v")
    dcq_n = _matmul(dq_lat, w["w_uq"], "nt", [F32], "b_mla_q_up")
    dckv_a = _matmul(dk_mla, w["w_ukv_k"], "nt", [F32], "b_mla_k_up")
    dckv_b = _matmul(dv_mla, w["w_ukv_v"], "nt", [F32], "b_mla_v_up")

    def f_prep_bwd(rows, consts):
        t, dcq, dca, dcb, dkr_, dqs, dks, sc, su, sd = rows
        gq, gkv = consts
        dc_q, dgq = _rms_bwd(t[:, 0:256], gq, dcq)
        dc_kv, dgkv = _rms_bwd(t[:, 256:384], gkv, dca + dcb)
        q_parts = [_rope(dqs[:, j * LANES:(j + 1) * LANES], sc, su, sd, SWA_HEAD_DIM // 2) for j in range(8)]
        k_parts = [_rope(dks[:, j * LANES:(j + 1) * LANES], sc, su, sd, SWA_HEAD_DIM // 2) for j in range(2)]
        return [jnp.concatenate([dc_q, dc_kv, dkr_] + q_parts + k_parts, axis=1)], [dgq, dgkv]

    dp1, gr["g_q_lat"], gr["g_kv_lat"] = _rowwise(
        f_prep_bwd, [sv["p1"], dcq_n, dckv_a, dckv_b, dkr, dq_swa, dk_swa, *swa_tab["inv"]], [w["g_q_lat"], w["g_kv_lat"]],
        [(P1_W, BF16)], [256, 128], "b_lat_prep")

    gr["w_in1"] = _matmul(sv["h"], dp1, "tn", [F32], "b_w_in_lat")
    dh = _matmul(dp1, w["w_in1"], "nt", [F32], "b_proj_lat")
    gr["w_in2"] = []
    add_prev = lambda acc, prev: (acc + prev,)
    for piece, wp, tag in zip((dv_swa, dq_sb, dk_sb, dv_sb), w["w_in2_parts"], ("vswa", "qsb", "ksb", "vsb")):
        gr["w_in2"].append(_matmul(sv["h"], piece, "tn", [F32], "b_w_in_" + tag))
        dh = _matmul(piece, wp, "nt", [F32], "b_proj_" + tag, epilogue=add_prev, extras=[dh])
    gr["w_in3"] = _matmul(sv["h"], dlogit, "tn", [F32], "b_w_in_gate")
    dh = _matmul(dlogit, w["w_in3"], "nt", [F32], "b_proj_gate", epilogue=add_prev, extras=[dh])
    dx, gr["g_mix_pre"] = _rowwise(f_norm_bwd_res, [sv["x"], dh, dx1], [w["g_mix_pre"]], [(D_MODEL, F32)], [D_MODEL], "b_norm_mix_pre")
    return dx, gr


def _local_step(x, positions, loss_target, full):
    mc, mu, md = _rope_tables(positions, MLA_NOPE, MLA_ROPE, True)
    kc, ku, kd = _rope_tables(positions, MLA_NOPE, MLA_ROPE, False)
    sc, su, sd = _rope_tables(positions, 0, SWA_HEAD_DIM, False)
    mla_tab = {"q": (mc, mu, md), "k": (kc, ku, kd), "q_inv": (mc, -mu, -md), "k_inv": (kc, -ku, -kd)}
    swa_tab = {"f": (sc, su, sd), "inv": (sc, -su, -sd)}
    tabs = (mla_tab, swa_tab)

    layers = []
    for l in range(DEPTH):
        w1, w2, w3 = _w_in_internal(full["w_in"][l].astype(BF16))
        uk, uv = _w_ukv_internal(full["w_ukv"][l].astype(BF16))
        layers.append({
            "w_in1": w1, "w_in2": jnp.concatenate(w2, axis=1), "w_in2_parts": w2, "w_in3": w3,
            "w_uq": _pad_cols(full["w_uq"][l].astype(BF16), MLA_HEADS, MLA_NOPE + MLA_ROPE),
            "w_ukv_k": uk, "w_ukv_v": uv,
            "w_o_mla": _pad_rows(full["w_o_mla"][l].astype(BF16), 8, 64),
            "w_o_swa": _pad_rows(full["w_o_swa"][l].astype(BF16), 8, 64),
            "w_o_sb": _pad_rows(full["w_o_sb"][l].astype(BF16), 8, 64),
            "w_out": full["w_out"][l].astype(BF16), "w_up": full["w_up"][l].astype(BF16),
            "w_down": full["w_down"][l].astype(BF16),
            "g_mix_pre": full["g_mix_pre"][l][None], "b_gate": full["b_gate"][l][None],
            "g_q_lat": full["g_q_lat"][l][None], "g_kv_lat": full["g_kv_lat"][l][None],
            "g_mix_post": full["g_mix_post"][l][None], "g_mlp_pre": full["g_mlp_pre"][l][None],
            "g_mlp_post": full["g_mlp_post"][l][None],
            "sink_b": jnp.repeat(full["swa_sinks"][l], LANES)[None],
        })

    saved = []
    h = x
    for l in range(DEPTH):
        h, sv = _layer_fwd(h, layers[l], tabs)
        saved.append(sv)

    def f_loss(rows, consts):
        err = rows[0] - rows[1]
        return [err * (1.0 / D_MODEL)], [jnp.sum(err * err, axis=1, keepdims=True)]

    dy, sq = _rowwise(f_loss, [h, loss_target], [], [(D_MODEL, F32)], [1], "loss_head")
    loss_part = sq * (0.5 / D_MODEL)

    grads = [None] * DEPTH
    d = dy
    for l in reversed(range(DEPTH)):
        d, gr = _layer_bwd(d, layers[l], saved[l], tabs)
        grads[l] = {
            "g_mix_pre": gr["g_mix_pre"][0], "w_in": _w_in_reference(gr["w_in1"], gr["w_in2"], gr["w_in3"]),
            "b_gate": gr["b_gate"][0], "g_q_lat": gr["g_q_lat"][0], "g_kv_lat": gr["g_kv_lat"][0],
            "w_uq": _unpad_cols(gr["w_uq"], MLA_HEADS, MLA_NOPE + MLA_ROPE),
            "w_ukv": _w_ukv_reference(gr["w_ukv_k"], gr["w_ukv_v"]), "swa_sinks": gr["swa_sinks"],
            "w_o_mla": _unpad_rows(gr["w_o_mla"], 8, 64), "w_o_swa": _unpad_rows(gr["w_o_swa"], 8, 64),
            "w_o_sb": _unpad_rows(gr["w_o_sb"], 8, 64), "w_out": gr["w_out"], "g_mix_post": gr["g_mix_post"][0],
            "g_mlp_pre": gr["g_mlp_pre"][0], "w_up": gr["w_up"], "w_down": gr["w_down"], "g_mlp_post": gr["g_mlp_post"][0],
        }
    stacked = {n: jnp.stack([grads[l][n] for l in range(DEPTH)]) for n in WEIGHTS}
    return loss_part, d, stacked


def _rows_of(a):
    return a.reshape(-1, LANES)


def _small_rows(d):
    parts = []
    for n in SMALL:
        a = d[n]
        if a.shape[1] < LANES:
            a = jnp.pad(a, ((0, 0), (0, LANES - a.shape[1])))
        parts.append(_rows_of(a))
    return parts


def _pack(shards, small, dtype):
    parts = [_rows_of(shards[n]) for n in SHARDED]
    if small is not None:
        parts += _small_rows(small)
    slab = jnp.concatenate(parts, axis=0).astype(dtype)
    pad = (-slab.shape[0]) % SLAB_ROW_ALIGN
    return jnp.pad(slab, ((0, pad), (0, 0)))


def _unpack(slab, shard_shapes, small_shapes):
    out, r = {}, 0
    for n in SHARDED:
        rows = int(np.prod(shard_shapes[n])) // LANES
        out[n] = slab[r:r + rows].reshape(shard_shapes[n])
        r += rows
    if small_shapes is not None:
        for n in SMALL:
            depth, width = small_shapes[n]
            rows = depth * max(width, LANES) // LANES
            out[n] = slab[r:r + rows].reshape(depth, max(width, LANES))[:, :width]
            r += rows
    return out


def _chip_exchange(src, name):
    rows = src.shape[-2]

    def body(src_ref, out_ref, send_sems, recv_sems):
        x, y, c = lax.axis_index("x"), lax.axis_index("y"), lax.axis_index("c")
        me = 2 * x + y
        chips = [(1 - x, y), (x, 1 - y), (1 - x, 1 - y)]
        sends = []
        for k, (cx, cy) in enumerate(chips):
            cp = pltpu.make_async_remote_copy(
                src_ref=src_ref.at[2 * cx + cy], dst_ref=out_ref.at[me], send_sem=send_sems.at[k],
                recv_sem=recv_sems.at[k], device_id=(cx, cy, c), device_id_type=pl.DeviceIdType.MESH)
            cp.start()
            sends.append(cp)
        for k, (cx, cy) in enumerate(chips):
            pltpu.make_async_remote_copy(
                src_ref=src_ref.at[me], dst_ref=out_ref.at[2 * cx + cy], send_sem=send_sems.at[k],
                recv_sem=recv_sems.at[k], device_id=(cx, cy, c), device_id_type=pl.DeviceIdType.MESH).wait_recv()
        for cp in sends:
            cp.wait_send()

    return pl.pallas_call(
        body,
        name=name,
        in_specs=[pl.BlockSpec(memory_space=pl.ANY)],
        out_specs=pl.BlockSpec(memory_space=pl.ANY),
        out_shape=jax.ShapeDtypeStruct((N_CHIPS, rows, LANES), src.dtype),
        scratch_shapes=[pltpu.SemaphoreType.DMA((3,)), pltpu.SemaphoreType.DMA((3,))],
    )(src)


def _half_rows(c, half):
    return pl.ds(pl.multiple_of(c * half, SLAB_ROW_ALIGN // 2), half)


def _gather_weights(src, name):
    rows = src.shape[0]
    half = rows // 2

    def body(src_ref, out_ref, send_sems, recv_sems):
        x, y, c = lax.axis_index("x"), lax.axis_index("y"), lax.axis_index("c")
        me = 2 * x + y
        chips = [(1 - x, y), (x, 1 - y), (1 - x, 1 - y)]

        def copy(k, src_view, slab, part, to):
            return pltpu.make_async_remote_copy(
                src_ref=src_view, dst_ref=out_ref.at[slab, _half_rows(part, half), :], send_sem=send_sems.at[k],
                recv_sem=recv_sems.at[k], device_id=to, device_id_type=pl.DeviceIdType.MESH)

        sends = [copy(k, src_ref.at[_half_rows(c, half), :], me, c, (cx, cy, c)) for k, (cx, cy) in enumerate(chips)]
        for cp in sends:
            cp.start()
        for k, (cx, cy) in enumerate(chips):
            j = 2 * cx + cy
            landed = out_ref.at[j, _half_rows(c, half), :]
            copy(k, landed, j, c, (cx, cy, c)).wait_recv()
            fwd = copy(3 + k, landed, j, c, (x, y, 1 - c))
            fwd.start()
            sends.append(fwd)
        for k, (cx, cy) in enumerate(chips):
            j = 2 * cx + cy
            copy(3 + k, out_ref.at[j, _half_rows(1 - c, half), :], j, 1 - c, (x, y, 1 - c)).wait_recv()
        for cp in sends:
            cp.wait_send()

    return pl.pallas_call(
        body,
        name=name,
        in_specs=[pl.BlockSpec(memory_space=pl.ANY)],
        out_specs=pl.BlockSpec(memory_space=pl.ANY),
        out_shape=jax.ShapeDtypeStruct((N_CHIPS, rows, LANES), src.dtype),
        scratch_shapes=[pltpu.SemaphoreType.DMA((6,)), pltpu.SemaphoreType.DMA((6,))],
    )(src)


def _sibling_halves(src, name):
    n, rows, _ = src.shape
    half = rows // 2

    def body(src_ref, out_ref, send_sem, recv_sem):
        x, y, c = lax.axis_index("x"), lax.axis_index("y"), lax.axis_index("c")
        cp = pltpu.make_async_remote_copy(
            src_ref=src_ref.at[:, _half_rows(1 - c, half), :], dst_ref=out_ref, send_sem=send_sem, recv_sem=recv_sem,
            device_id=(x, y, 1 - c), device_id_type=pl.DeviceIdType.MESH)
        cp.start()
        cp.wait()

    return pl.pallas_call(
        body,
        name=name,
        in_specs=[pl.BlockSpec(memory_space=pl.ANY)],
        out_specs=pl.BlockSpec(memory_space=pl.ANY),
        out_shape=jax.ShapeDtypeStruct((n, half, LANES), src.dtype),
        scratch_shapes=[pltpu.SemaphoreType.DMA, pltpu.SemaphoreType.DMA],
    )(src)


def _sibling_join(src, name):
    half = src.shape[0]

    def body(src_ref, out_ref, send_sem, recv_sem):
        x, y, c = lax.axis_index("x"), lax.axis_index("y"), lax.axis_index("c")
        cp = pltpu.make_async_remote_copy(
            src_ref=src_ref, dst_ref=out_ref.at[_half_rows(c, half), :], send_sem=send_sem, recv_sem=recv_sem,
            device_id=(x, y, 1 - c), device_id_type=pl.DeviceIdType.MESH)
        cp.start()
        pltpu.make_async_remote_copy(
            src_ref=src_ref, dst_ref=out_ref.at[_half_rows(1 - c, half), :], send_sem=send_sem, recv_sem=recv_sem,
            device_id=(x, y, 1 - c), device_id_type=pl.DeviceIdType.MESH).wait_recv()
        cp.wait_send()

    return pl.pallas_call(
        body,
        name=name,
        in_specs=[pl.BlockSpec(memory_space=pl.ANY)],
        out_specs=pl.BlockSpec(memory_space=pl.ANY),
        out_shape=jax.ShapeDtypeStruct((2 * half, LANES), src.dtype),
        scratch_shapes=[pltpu.SemaphoreType.DMA, pltpu.SemaphoreType.DMA],
    )(src)


SUM_ROWS = 1024


def _pair_sum(mine, theirs, c, name):
    n, half, _ = theirs.shape
    blocks = half // SUM_ROWS

    def body(c_ref, a_ref, b_ref, o_ref):
        o_ref[...] = (a_ref[...].astype(F32) + b_ref[...].astype(F32)).astype(o_ref.dtype)

    return pl.pallas_call(
        body,
        name=name,
        grid_spec=pltpu.PrefetchScalarGridSpec(
            num_scalar_prefetch=1,
            grid=(blocks,),
            in_specs=[pl.BlockSpec((n, SUM_ROWS, LANES), lambda i, c_ref: (0, c_ref[0] * blocks + i, 0)),
                      pl.BlockSpec((n, SUM_ROWS, LANES), lambda i, c_ref: (0, i, 0))],
            out_specs=pl.BlockSpec((n, SUM_ROWS, LANES), lambda i, c_ref: (0, i, 0)),
        ),
        out_shape=jax.ShapeDtypeStruct((n, half, LANES), BF16),
        compiler_params=_params(("arbitrary",)),
    )(jnp.reshape(c, (1,)).astype(jnp.int32), mine, theirs)


def _sum_chips(own, landed, me, name):
    rows = landed.shape[1]

    def body(me_ref, a_ref, b_ref, o_ref):
        t = [jnp.where(me_ref[0] == j, a_ref[j], b_ref[j]).astype(F32) for j in range(N_CHIPS)]
        o_ref[...] = ((t[0] + t[1]) + t[2]) + t[3]

    slabs = pl.BlockSpec((N_CHIPS, SUM_ROWS, LANES), lambda i, me_ref: (0, i, 0))
    return pl.pallas_call(
        body,
        name=name,
        grid_spec=pltpu.PrefetchScalarGridSpec(
            num_scalar_prefetch=1,
            grid=(rows // SUM_ROWS,),
            in_specs=[slabs, slabs],
            out_specs=pl.BlockSpec((SUM_ROWS, LANES), lambda i, me_ref: (i, 0)),
        ),
        out_shape=jax.ShapeDtypeStruct((rows, LANES), F32),
        compiler_params=_params(("arbitrary",)),
    )(jnp.reshape(me, (1,)).astype(jnp.int32), own, landed)


def _adamw(w, m, v, g, name):
    shape = w.shape
    flat = lambda a: a.reshape(-1, shape[-1])

    def fn(rows, consts):
        w_, m_, v_, g_ = rows
        m_new = ADAM_B1 * m_ + (1.0 - ADAM_B1) * g_
        v_new = ADAM_B2 * v_ + (1.0 - ADAM_B2) * (g_ * g_)
        m_hat = m_new / (1.0 - ADAM_B1 ** ADAM_STEP)
        v_hat = v_new / (1.0 - ADAM_B2 ** ADAM_STEP)
        delta = -ADAM_LR * (m_hat / (jnp.sqrt(v_hat) + ADAM_EPS) + ADAM_WD * w_)
        return [delta, m_new, v_new], []

    outs = _rowwise(fn, [flat(w), flat(m), flat(v), flat(g)], [], [(shape[-1], F32)] * 3, [], name)
    return [o.reshape(shape) for o in outs]


def kernel(x, positions, g_mix_pre, w_in, b_gate, g_q_lat, g_kv_lat, w_uq, w_ukv, swa_sinks, w_o_mla, w_o_swa, w_o_sb, w_out, g_mix_post, g_mlp_pre, w_up, w_down, g_mlp_post, loss_target, m_g_mix_pre, m_w_in, m_b_gate, m_g_q_lat, m_g_kv_lat, m_w_uq, m_w_ukv, m_swa_sinks, m_w_o_mla, m_w_o_swa, m_w_o_sb, m_w_out, m_g_mix_post, m_g_mlp_pre, m_w_up, m_w_down, m_g_mlp_post, v_g_mix_pre, v_w_in, v_b_gate, v_g_q_lat, v_g_kv_lat, v_w_uq, v_w_ukv, v_swa_sinks, v_w_o_mla, v_w_o_swa, v_w_o_sb, v_w_out, v_g_mix_post, v_g_mlp_pre, v_w_up, v_w_down, v_g_mlp_post):
    given = dict(locals())
    wts = {n: given[n] for n in WEIGHTS}
    mom_m = {n: given["m_" + n] for n in WEIGHTS}
    mom_v = {n: given["v_" + n] for n in WEIGHTS}
    shard_shapes = {n: wts[n].shape for n in SHARDED}
    small_shapes = {n: wts[n].shape for n in SMALL}

    me = 2 * lax.axis_index("x") + lax.axis_index("y")
    core = lax.axis_index("c")
    gathered = _gather_weights(_pack(wts, None, BF16), "gather_weights")
    full = {n: wts[n] for n in SMALL}
    per_chip = [_unpack(gathered[j], shard_shapes, None) for j in range(N_CHIPS)]
    for n in SHARDED:
        own = wts[n].astype(BF16)
        full[n] = jnp.concatenate([jnp.where(me == j, own, per_chip[j][n]) for j in range(N_CHIPS)], axis=SHARD_AXIS[n])

    loss_part, grad_x, grads = _local_step(x[0], positions[0], loss_target[0], full)
    loss = lax.psum(loss_part[0, 0], ("x", "y", "c"))

    small_g = {n: grads[n] for n in SMALL}
    slabs = []
    for j in range(N_CHIPS):
        shard = {n: jnp.split(grads[n], N_CHIPS, axis=SHARD_AXIS[n])[j] for n in SHARDED}
        slabs.append(_pack(shard, small_g, BF16))
    per_chip_g = jnp.stack(slabs)
    theirs = _sibling_halves(per_chip_g, "pair_grads")
    pair = _pair_sum(per_chip_g, theirs, core, "sum_pair")
    landed = _chip_exchange(pair, "scatter_grads")
    my_half = _sum_chips(pair, landed, me, "sum_chips")
    g_slab = lax.dynamic_update_slice(_sibling_join(my_half, "join_grads"), my_half, (core * my_half.shape[0], 0))

    g = _unpack(g_slab, shard_shapes, small_shapes)
    stepped = {n: _adamw(wts[n], mom_m[n], mom_v[n], g[n], "adamw_" + n) for n in WEIGHTS}
    outs = [loss, grad_x[None]] + [g[n] for n in WEIGHTS]
    for part in range(3):
        outs += [stepped[n][part] for n in WEIGHTS]
    return tuple(outs)
```

```python
import numpy as np
import jax
import jax.numpy as jnp
from jax import lax
from jax.experimental import pallas as pl
from jax.experimental.pallas import tpu as pltpu

F32 = jnp.float32
BF16 = jnp.bfloat16

D_MODEL = 1024
DEPTH = 4
MLA_HEADS, MLA_Q_LORA, MLA_KV_LORA, MLA_NOPE, MLA_ROPE, MLA_V = 8, 256, 128, 64, 32, 64
SWA_HEADS, SWA_KV_HEADS, SWA_HEAD_DIM, SWA_WINDOW = 8, 2, 64, 128
SB_HEADS, SB_HEAD_DIM = 8, 64
D_FF = 4 * D_MODEL
ROPE_THETA = 10000.0
EPS = 1e-6
SPLIT_SIZES = (256, 128, 32, 512, 128, 128, 512, 512, 512, 3 * D_MODEL)
SPLIT_POINTS = [int(v) for v in np.cumsum(SPLIT_SIZES)[:-1]]

ADAM_LR, ADAM_B1, ADAM_B2, ADAM_EPS, ADAM_WD, ADAM_STEP = 0.001, 0.9, 0.999, 1e-08, 0.01, 10

LANES = 128
V7X_VMEM_BYTES = 64 * 1024 * 1024
VMEM_LIMIT = V7X_VMEM_BYTES - 8 * 1024 * 1024
MATMUL_VMEM_BUDGET = 36 * 1024 * 1024
N_CHIPS = 4
SLAB_ROW_ALIGN = 2048

P1_W = 256 + 128 + 128 + 1024 + 256
P2_W = 256 + 1024 + 1024 + 1024
P3_W = 3 * D_MODEL

SHARDED = ("w_in", "w_uq", "w_ukv", "w_o_mla", "w_o_swa", "w_o_sb", "w_out", "w_up", "w_down")
SHARD_AXIS = {"w_in": 2, "w_uq": 2, "w_ukv": 2, "w_o_mla": 2, "w_o_swa": 2, "w_o_sb": 2, "w_out": 1, "w_up": 2, "w_down": 1}
SMALL = ("g_mix_pre", "b_gate", "g_q_lat", "g_kv_lat", "swa_sinks", "g_mix_post", "g_mlp_pre", "g_mlp_post")
WEIGHTS = ("g_mix_pre", "w_in", "b_gate", "g_q_lat", "g_kv_lat", "w_uq", "w_ukv", "swa_sinks", "w_o_mla", "w_o_swa",
           "w_o_sb", "w_out", "g_mix_post", "g_mlp_pre", "w_up", "w_down", "g_mlp_post")

NN = (((1,), (0,)), ((), ()))
NT = (((1,), (1,)), ((), ()))
TN = (((0,), (0,)), ((), ()))


def _dot(a, b, dims):
    return lax.dot_general(a, b, dims, preferred_element_type=F32)


def _params(sem):
    return pltpu.CompilerParams(dimension_semantics=sem, vmem_limit_bytes=VMEM_LIMIT)


def _largest_tile(n, cap):
    if n <= cap:
        return n
    best = LANES
    for t in range(LANES, cap + 1, LANES):
        if n % t == 0:
            best = t
    return best


def _matmul_tiles(M, N, K, a_bytes, b_bytes, out_bytes, extra_bytes):
    tn = _largest_tile(N, 1792)
    tm = _largest_tile(M, 1024 if tn <= 1024 else 512)
    tk = _largest_tile(K, 2048)

    def need(tm_, tk_):
        acc = 4 * tm_ * tn if tk_ < K else 0
        return 2 * (tm_ * tk_ * a_bytes + tk_ * tn * b_bytes + tm_ * tn * (out_bytes + extra_bytes)) + acc

    while need(tm, tk) > MATMUL_VMEM_BUDGET:
        if tk >= tm and tk % 256 == 0:
            tk //= 2
        elif tm % 256 == 0:
            tm //= 2
        else:
            break
    return tm, tn, tk


def _matmul(a, b, mode, out_dtypes, name, epilogue=None, extras=(), row_extras=()):
    if mode == "nn":
        (M, K), (K2, N) = a.shape, b.shape
    elif mode == "nt":
        (M, K), (N, K2) = a.shape, b.shape
    else:
        (K, M), (K2, N) = a.shape, b.shape
    assert K == K2, (name, a.shape, b.shape)
    tm, tn, tk = _matmul_tiles(
        M, N, K, a.dtype.itemsize, b.dtype.itemsize, sum(jnp.dtype(d).itemsize for d in out_dtypes),
        sum(e.dtype.itemsize for e in extras))
    assert M % tm == 0 and N % tn == 0 and K % tk == 0, (name, M, N, K, tm, tn, tk)
    nk = K // tk
    if mode == "tn":
        a_spec = pl.BlockSpec((tk, tm), lambda i, j, k: (k, i))
    else:
        a_spec = pl.BlockSpec((tm, tk), lambda i, j, k: (i, k))
    if mode == "nt":
        b_spec = pl.BlockSpec((tn, tk), lambda i, j, k: (j, k))
    else:
        b_spec = pl.BlockSpec((tk, tn), lambda i, j, k: (k, j))
    dims = {"nn": NN, "nt": NT, "tn": TN}[mode]
    n_ex, n_rex, n_out = len(extras), len(row_extras), len(out_dtypes)

    def body(*refs):
        a_ref, b_ref = refs[:2]
        ex = refs[2:2 + n_ex]
        rex = refs[2 + n_ex:2 + n_ex + n_rex]
        outs = refs[2 + n_ex + n_rex:2 + n_ex + n_rex + n_out]

        def finish(total):
            res = (total,) if epilogue is None else epilogue(total, *[e[...] for e in ex], *[e[...] for e in rex])
            for o, r in zip(outs, res):
                o[...] = r.astype(o.dtype)

        part = _dot(a_ref[...].astype(BF16), b_ref[...].astype(BF16), dims)
        if nk == 1:
            finish(part)
            return
        acc = refs[-1]
        k = pl.program_id(2)

        @pl.when(k == 0)
        def _():
            acc[...] = part

        @pl.when(k > 0)
        def _():
            acc[...] += part

        @pl.when(k == nk - 1)
        def _():
            finish(acc[...])

    in_specs = [a_spec, b_spec]
    in_specs += [pl.BlockSpec((tm, tn), lambda i, j, k: (i, j)) for _ in extras]
    in_specs += [pl.BlockSpec((1, tn), lambda i, j, k: (0, j)) for _ in row_extras]
    out = pl.pallas_call(
        body,
        name=name,
        grid=(M // tm, N // tn, nk),
        in_specs=in_specs,
        out_specs=[pl.BlockSpec((tm, tn), lambda i, j, k: (i, j)) for _ in out_dtypes],
        out_shape=[jax.ShapeDtypeStruct((M, N), dt) for dt in out_dtypes],
        scratch_shapes=[pltpu.VMEM((tm, tn), F32)] if nk > 1 else [],
        compiler_params=_params(("parallel", "parallel", "arbitrary")),
    )(a, b, *extras, *row_extras)
    return out[0] if n_out == 1 else out


ROWWISE_ROW_BYTES = 16 * 1024


def _rowwise(fn, rows, consts, out_defs, sum_widths, name):
    R = rows[0].shape[0]
    per_row = sum(r.shape[1] * r.dtype.itemsize for r in rows) + sum(w * jnp.dtype(dt).itemsize for w, dt in out_defs)
    bm = 512 if per_row <= ROWWISE_ROW_BYTES else 256
    while R % bm:
        bm //= 2
    bm = max(bm, 1)
    n_r, n_c, n_o = len(rows), len(consts), len(out_defs)
    n_s = len(sum_widths)

    def body(*refs):
        r_in = refs[:n_r]
        c_in = refs[n_r:n_r + n_c]
        o_refs = refs[n_r + n_c:n_r + n_c + n_o]
        s_refs = refs[n_r + n_c + n_o:]
        outs, sums = fn([r[...] for r in r_in], [c[...] for c in c_in])
        for o, val in zip(o_refs, outs):
            o[...] = val.astype(o.dtype)
        if n_s:
            @pl.when(pl.program_id(0) == 0)
            def _():
                for s in s_refs:
                    s[...] = jnp.zeros_like(s)

            for s, val in zip(s_refs, sums):
                s[...] += jnp.sum(val, axis=0, keepdims=True)

    in_specs = [pl.BlockSpec((bm, r.shape[1]), lambda i: (i, 0)) for r in rows]
    in_specs += [pl.BlockSpec(c.shape, lambda i: (0, 0)) for c in consts]
    out_specs = [pl.BlockSpec((bm, w), lambda i: (i, 0)) for w, _ in out_defs]
    out_specs += [pl.BlockSpec((1, w), lambda i: (0, 0)) for w in sum_widths]
    out_shape = [jax.ShapeDtypeStruct((R, w), dt) for w, dt in out_defs]
    out_shape += [jax.ShapeDtypeStruct((1, w), F32) for w in sum_widths]
    return pl.pallas_call(
        body,
        name=name,
        grid=(R // bm,),
        in_specs=in_specs,
        out_specs=out_specs,
        out_shape=out_shape,
        compiler_params=_params(("arbitrary",)),
    )(*rows, *consts)


def _rms(x, g):
    r = lax.rsqrt(jnp.mean(x * x, axis=-1, keepdims=True) + EPS)
    return x * r * g


def _rms_bwd(x, g, dy):
    r = lax.rsqrt(jnp.mean(x * x, axis=-1, keepdims=True) + EPS)
    n = x * r
    dn = dy * g
    dx = r * (dn - n * jnp.mean(dn * n, axis=-1, keepdims=True))
    return dx, dy * n


def _rope(x, c, s_up, s_dn, half):
    return x * c + pltpu.roll(x, half, 1) * s_up + pltpu.roll(x, LANES - half, 1) * s_dn


def _rope_tables(positions, lo, d, nope_pass):
    S = positions.shape[0]
    half = d // 2
    inv = 1.0 / (ROPE_THETA ** (jnp.arange(0, d, 2, dtype=F32) / d))
    ang = positions.astype(F32)[:, None] * inv
    cos, sin = jnp.cos(ang), jnp.sin(ang)
    z = lambda n: jnp.zeros((S, n), F32)
    head = jnp.ones((S, lo), F32) if nope_pass else z(lo)
    tail = LANES - lo - d
    c = jnp.concatenate([head, cos, cos, z(tail)], axis=1)
    s_up = jnp.concatenate([z(lo), z(half), sin, z(tail)], axis=1)
    s_dn = jnp.concatenate([z(lo), -sin, z(half), z(tail)], axis=1)
    return c, s_up, s_dn


MLA_FWD_CFG = (2, 1024)
MLA_BWD_CFG = (2, 512)
SB_FWD_CFG = (2, 256)
SB_BWD_CFG = (4, 256)


def _tile_mask(bk, strict):
    row = lax.broadcasted_iota(jnp.int32, (bk, bk), 0)
    col = lax.broadcasted_iota(jnp.int32, (bk, bk), 1)
    return (col < row) if strict else (col <= row)


def _att_layout(S, cfg):
    nch, bk = cfg
    bq = nch * bk
    assert S % bq == 0, (S, cfg)
    rows = [slice(r * bk, (r + 1) * bk) for r in range(nch)]
    q_spec = lambda off=0: pl.BlockSpec((bq, LANES), lambda h, i: (i, off + h))
    kv_spec = lambda off=0: pl.BlockSpec((S, LANES), lambda h, i: (0, off + h))
    return bq, rows, q_spec, kv_spec


def _total(terms):
    terms = list(terms)
    out = terms[0]
    for t in terms[1:]:
        out = out + t
    return out


def _walk(nch, i, step, carry, leftward, alive=None):
    everyone = range(nch)
    if leftward:
        for d in reversed(everyone):
            carry = step(nch * i + d, carry, range(d, nch), {d})
        if alive is None:
            return lax.fori_loop(0, nch * i, lambda t, c: step(nch * i - 1 - t, c, everyone, set()), carry)
        more = lambda tc: jnp.logical_and(tc[0] < nch * i, alive(tc[1]))
        left = lambda tc: (tc[0] + 1, step(nch * i - 1 - tc[0], tc[1], everyone, set()))
        return lax.while_loop(more, left, (jnp.int32(0), carry))[1]
    carry = lax.fori_loop(0, nch * i, lambda kb, c: step(kb, c, everyone, set()), carry)
    for d in everyone:
        carry = step(nch * i + d, carry, range(d, nch), {d})
    return carry


ONES_LANE = MLA_V


def _softmax_attn_fwd(q, k, v, heads, name, q_off=0, k_off=0, v_off=0):
    S = q.shape[0]
    nch, bk = MLA_FWD_CFG
    bq, rows, q_spec, kv_spec = _att_layout(S, MLA_FWD_CFG)

    def body(q_ref, k_ref, v_ref, o_ref, lse_ref):
        i = pl.program_id(1)
        qs = [q_ref[rw, :] for rw in rows]

        def step(kb, cs, active, masked):
            off = pl.multiple_of(kb * bk, bk)
            ks, vs = k_ref[pl.ds(off, bk), :], v_ref[pl.ds(off, bk), :]
            A = list(active)
            s = {r: _dot(qs[r], ks, NT) for r in A}
            s = {r: (jnp.where(_tile_mask(bk, False), s[r], -1e30) if r in masked else s[r]) for r in A}
            m_new = {r: jnp.maximum(cs[r][0], jnp.max(s[r], axis=1, keepdims=True)) for r in A}
            p = {r: jnp.exp(s[r] - m_new[r]) for r in A}
            alpha = {r: jnp.exp(cs[r][0] - m_new[r]) for r in A}
            new = list(cs)
            for r in A:
                new[r] = (m_new[r], alpha[r] * cs[r][1] + _dot(p[r].astype(BF16), vs, NN))
            return tuple(new)

        init = (jnp.full((bk, 1), -1e30, F32), jnp.zeros((bk, LANES), F32))
        cs = _walk(nch, i, step, tuple(init for _ in rows), False)
        for r, (m, acc) in enumerate(cs):
            l = acc[:, ONES_LANE:ONES_LANE + 1]
            o_ref[rows[r], :] = (acc / l).astype(o_ref.dtype)
            lse_ref[rows[r], :] = m + jnp.log(l)

    return pl.pallas_call(
        body,
        name=name,
        grid=(heads, S // bq),
        in_specs=[q_spec(q_off), kv_spec(k_off), kv_spec(v_off)],
        out_specs=[q_spec(), pl.BlockSpec((None, bq, 1), lambda h, i: (h, i, 0))],
        out_shape=[jax.ShapeDtypeStruct((S, heads * LANES), BF16), jax.ShapeDtypeStruct((heads, S, 1), F32)],
        compiler_params=_params(("parallel", "arbitrary")),
    )(q, k, v)


def _softmax_attn_bwd(q, k, v, o, lse, do, heads, scale, name, q_off=0, k_off=0, v_off=0):
    S = q.shape[0]
    nch, bk = MLA_BWD_CFG
    bq, rows, q_spec, kv_spec = _att_layout(S, MLA_BWD_CFG)

    def body(q_ref, k_ref, v_ref, o_ref, lse_ref, do_ref, dq_ref, dk_ref, dv_ref):
        i = pl.program_id(1)

        @pl.when(i == 0)
        def _():
            dk_ref[...] = jnp.zeros_like(dk_ref)
            dv_ref[...] = jnp.zeros_like(dv_ref)

        qs = [q_ref[rw, :] for rw in rows]
        dos = [do_ref[rw, :] for rw in rows]
        lses = [lse_ref[rw, :] for rw in rows]
        deltas = [jnp.sum(dos[r].astype(F32) * o_ref[rows[r], :].astype(F32), axis=1, keepdims=True) for r in range(nch)]

        def step(kb, dqs, active, masked):
            off = pl.multiple_of(kb * bk, bk)
            ks, vs = k_ref[pl.ds(off, bk), :], v_ref[pl.ds(off, bk), :]
            A = list(active)
            s = {r: _dot(qs[r], ks, NT) for r in A}
            s = {r: (jnp.where(_tile_mask(bk, False), s[r], -1e30) if r in masked else s[r]) for r in A}
            p = {r: jnp.exp(s[r] - lses[r]) for r in A}
            dp = {r: _dot(dos[r], vs, NT) for r in A}
            ds = {r: (p[r] * (dp[r] - deltas[r])).astype(BF16) for r in A}
            dv_c = _total(_dot(p[r].astype(BF16), dos[r], TN) for r in A)
            dk_c = _total(_dot(ds[r], qs[r], TN) for r in A)
            dk_ref[pl.ds(off, bk), :] += dk_c
            dv_ref[pl.ds(off, bk), :] += dv_c
            new = list(dqs)
            for r in A:
                new[r] = dqs[r] + _dot(ds[r], ks, NN)
            return tuple(new)

        dqs = _walk(nch, i, step, tuple(jnp.zeros((bk, LANES), F32) for _ in rows), False)
        for r in range(nch):
            dq_ref[rows[r], :] = dqs[r] * scale

    return pl.pallas_call(
        body,
        name=name,
        grid=(heads, S // bq),
        in_specs=[q_spec(q_off), kv_spec(k_off), kv_spec(v_off), q_spec(),
                  pl.BlockSpec((None, bq, 1), lambda h, i: (h, i, 0)), q_spec()],
        out_specs=[q_spec(), kv_spec(), kv_spec()],
        out_shape=[jax.ShapeDtypeStruct((S, heads * LANES), F32)] * 3,
        compiler_params=_params(("parallel", "arbitrary")),
    )(q, k, v, o, lse, do)


def _tri(n, inclusive):
    r = lax.broadcasted_iota(jnp.int32, (n, n), 0)
    c = lax.broadcasted_iota(jnp.int32, (n, n), 1)
    return jnp.where((r >= c) if inclusive else (r > c), 1.0, 0.0).astype(BF16)


def _suffix_sum(x, tri):
    hi = x.astype(BF16)
    lo = (x - hi.astype(F32)).astype(BF16)
    return _dot(hi, tri, NN) + _dot(lo, tri, NN)


def _sb_logs(z):
    lg = jnp.log(1.0 + jnp.exp(-jnp.abs(z)))
    l1m = -(jnp.maximum(z, 0.0) + lg)
    return l1m, l1m + z


SB_SCALE = SB_HEAD_DIM ** -0.5
assert SB_SCALE == 0.125
SB_DEAD = -110.0


def _sb_alive(cs):
    top = cs[0][0]
    for c in cs[1:]:
        top = jnp.maximum(top, c[0])
    return jnp.max(top) > SB_DEAD


def _sb_attn_fwd(qkv, heads, name, q_off, k_off, v_off):
    S = qkv.shape[0]
    nch, bk = SB_FWD_CFG
    bq, rows, q_spec, kv_spec = _att_layout(S, SB_FWD_CFG)

    def body(q_ref, k_ref, v_ref, o_ref):
        i = pl.program_id(1)
        qs = [q_ref[rw, :] * SB_SCALE for rw in rows]
        tri = _tri(bk, False)

        def step(kb, cs, active, masked):
            off = pl.multiple_of(kb * bk, bk)
            ks, vs = k_ref[pl.ds(off, bk), :], v_ref[pl.ds(off, bk), :]
            A = list(active)
            lg = {r: _sb_logs(_dot(qs[r], ks, NT)) for r in A}
            l1m = {r: (jnp.where(_tile_mask(bk, True), lg[r][0], 0.0) if r in masked else lg[r][0]) for r in A}
            suf = {r: _suffix_sum(l1m[r], tri) for r in A}
            ex = {r: lg[r][1] + cs[r][0] + suf[r] for r in A}
            ex = {r: (jnp.where(_tile_mask(bk, True), ex[r], -1e30) if r in masked else ex[r]) for r in A}
            ab = {r: jnp.exp(ex[r]).astype(BF16) for r in A}
            new = list(cs)
            for r in A:
                new[r] = (cs[r][0] + jnp.sum(l1m[r], axis=1, keepdims=True), cs[r][1] + _dot(ab[r], vs, NN))
            return tuple(new)

        init = (jnp.zeros((bk, 1), F32), jnp.zeros((bk, LANES), F32))
        cs = _walk(nch, i, step, tuple(init for _ in rows), True, _sb_alive)
        for r in range(nch):
            o_ref[rows[r], :] = cs[r][1]

    return pl.pallas_call(
        body,
        name=name,
        grid=(heads, S // bq),
        in_specs=[q_spec(q_off), kv_spec(k_off), kv_spec(v_off)],
        out_specs=q_spec(),
        out_shape=jax.ShapeDtypeStruct((S, heads * LANES), F32),
        compiler_params=_params(("parallel", "arbitrary")),
    )(qkv, qkv, qkv)


def _sb_attn_bwd(qkv, o, do, heads, name, q_off, k_off, v_off):
    S = qkv.shape[0]
    nch, bk = SB_BWD_CFG
    bq, rows, q_spec, kv_spec = _att_layout(S, SB_BWD_CFG)

    def body(q_ref, k_ref, v_ref, o_ref, do_ref, dq_ref, dk_ref, dv_ref):
        i = pl.program_id(1)

        @pl.when(i == 0)
        def _():
            dk_ref[...] = jnp.zeros_like(dk_ref)
            dv_ref[...] = jnp.zeros_like(dv_ref)

        tri = _tri(bk, False)
        qs = [q_ref[rw, :] * SB_SCALE for rw in rows]
        dos = [do_ref[rw, :] for rw in rows]
        deltas = [jnp.sum(dos[r].astype(F32) * o_ref[rows[r], :], axis=1, keepdims=True) for r in range(nch)]

        def step(kb, cs, active, masked):
            off = pl.multiple_of(kb * bk, bk)
            ks, vs = k_ref[pl.ds(off, bk), :], v_ref[pl.ds(off, bk), :]
            A = list(active)
            lg = {r: _sb_logs(_dot(qs[r], ks, NT)) for r in A}
            l1m = {r: (jnp.where(_tile_mask(bk, True), lg[r][0], 0.0) if r in masked else lg[r][0]) for r in A}
            suf = {r: _suffix_sum(l1m[r], tri) for r in A}
            ex = {r: lg[r][1] + cs[r][0] + suf[r] for r in A}
            ex = {r: (jnp.where(_tile_mask(bk, True), ex[r], -1e30) if r in masked else ex[r]) for r in A}
            ab = {r: jnp.exp(ex[r]).astype(BF16) for r in A}
            da = {r: _dot(dos[r], vs, NT) for r in A}
            g = {r: ab[r].astype(F32) * da[r] for r in A}
            gs = {r: _suffix_sum(g[r], tri) for r in A}
            beta = {r: jnp.exp(lg[r][1]) for r in A}
            dz = {r: g[r] - beta[r] * (deltas[r] - cs[r][1] - gs[r]) for r in A}
            dz = {r: (jnp.where(_tile_mask(bk, True), dz[r], 0.0) if r in masked else dz[r]) for r in A}
            dzb = {r: dz[r].astype(BF16) for r in A}
            dv_c = _total(_dot(ab[r], dos[r], TN) for r in A)
            dk_c = _total(_dot(dzb[r], qs[r], TN) for r in A)
            dk_ref[pl.ds(off, bk), :] += dk_c
            dv_ref[pl.ds(off, bk), :] += dv_c
            new = list(cs)
            for r in A:
                new[r] = (cs[r][0] + jnp.sum(l1m[r], axis=1, keepdims=True),
                          cs[r][1] + jnp.sum(g[r], axis=1, keepdims=True), cs[r][2] + _dot(dzb[r], ks, NN))
            return tuple(new)

        zcol = jnp.zeros((bk, 1), F32)
        init = (zcol, zcol, jnp.zeros((bk, LANES), F32))
        cs = _walk(nch, i, step, tuple(init for _ in rows), True, _sb_alive)
        for r in range(nch):
            dq_ref[rows[r], :] = cs[r][2] * SB_SCALE

    return pl.pallas_call(
        body,
        name=name,
        grid=(heads, S // bq),
        in_specs=[q_spec(q_off), kv_spec(k_off), kv_spec(v_off), q_spec(), q_spec()],
        out_specs=[q_spec(), kv_spec(), kv_spec()],
        out_shape=[jax.ShapeDtypeStruct((S, heads * LANES), F32)] * 3,
        compiler_params=_params(("parallel", "arbitrary")),
    )(qkv, qkv, qkv, o, do)


SWA_BLK = 128
SWA_GROUP = SWA_HEADS // SWA_KV_HEADS


SWA_NB = 4
SWA_ROWS = SWA_NB * SWA_BLK


def _swa_band_mask(first):
    row = lax.broadcasted_iota(jnp.int32, (SWA_BLK, 2 * SWA_BLK), 0)
    col = lax.broadcasted_iota(jnp.int32, (SWA_BLK, 2 * SWA_BLK), 1)
    return (col > row) & (col <= row + SWA_WINDOW) & (jnp.logical_not(first) | (col >= SWA_BLK))


def _swa_in_specs(v_off):
    gw = SWA_GROUP * LANES
    before = lambda h, n: (jnp.maximum(SWA_NB * n - 1, 0), h)
    return [
        pl.BlockSpec((SWA_ROWS, gw), lambda h, n: (n, h)),
        pl.BlockSpec((SWA_BLK, LANES), before),
        pl.BlockSpec((SWA_ROWS, LANES), lambda h, n: (n, h)),
        pl.BlockSpec((SWA_BLK, LANES), lambda h, n: (jnp.maximum(SWA_NB * n - 1, 0), v_off + h)),
        pl.BlockSpec((SWA_ROWS, LANES), lambda h, n: (n, v_off + h)),
        pl.BlockSpec((1, gw), lambda h, n: (0, h)),
    ]


def _swa_bands(n, kp_ref, kc_ref, vp_ref, vc_ref):
    k_all = jnp.concatenate([kp_ref[...], kc_ref[...]], axis=0)
    v_all = jnp.concatenate([vp_ref[...], vc_ref[...]], axis=0)
    bands = []
    for j in range(SWA_NB):
        rows = slice(j * SWA_BLK, (j + 2) * SWA_BLK)
        bands.append((k_all[rows], v_all[rows], _swa_band_mask((n == 0) if j == 0 else False)))
    return bands


def _swa_fwd(q, k, v, v_off, sink_b, name):
    S = q.shape[0]
    assert S % SWA_ROWS == 0
    scale = SWA_HEAD_DIM ** -0.5
    gw = SWA_GROUP * LANES

    def body(q_ref, kp_ref, kc_ref, vp_ref, vc_ref, sink_ref, o_ref, lse_ref):
        n = pl.program_id(1)
        bands = _swa_bands(n, kp_ref, kc_ref, vp_ref, vc_ref)
        P = [(j, g) for j in range(SWA_NB) for g in range(SWA_GROUP)]
        rows = lambda j: slice(j * SWA_BLK, (j + 1) * SWA_BLK)
        lanes = lambda g: slice(g * LANES, (g + 1) * LANES)
        sk = {g: sink_ref[:, g * LANES:g * LANES + 1] for g in range(SWA_GROUP)}
        s = {(j, g): jnp.where(bands[j][2], _dot(q_ref[rows(j), lanes(g)], bands[j][0], NT) * scale, -1e30) for j, g in P}
        m = {(j, g): jnp.maximum(jnp.max(s[j, g], axis=1, keepdims=True), sk[g]) for j, g in P}
        p = {(j, g): jnp.exp(s[j, g] - m[j, g]) for j, g in P}
        den = {(j, g): jnp.sum(p[j, g], axis=1, keepdims=True) + jnp.exp(sk[g] - m[j, g]) for j, g in P}
        for j, g in P:
            o_ref[rows(j), lanes(g)] = _dot((p[j, g] / den[j, g]).astype(BF16), bands[j][1], NN).astype(o_ref.dtype)
            lse_ref[g, rows(j), :] = m[j, g] + jnp.log(den[j, g])

    return pl.pallas_call(
        body,
        name=name,
        grid=(SWA_KV_HEADS, S // SWA_ROWS),
        in_specs=_swa_in_specs(v_off),
        out_specs=[
            pl.BlockSpec((SWA_ROWS, gw), lambda h, n: (n, h)),
            pl.BlockSpec((SWA_GROUP, SWA_ROWS, 1), lambda h, n: (h, n, 0)),
        ],
        out_shape=[jax.ShapeDtypeStruct((S, SWA_HEADS * LANES), BF16), jax.ShapeDtypeStruct((SWA_HEADS, S, 1), F32)],
        compiler_params=_params(("parallel", "arbitrary")),
    )(q, k, k, v, v, sink_b)


def _swa_bwd(q, k, v, v_off, sink_b, o, lse, do, name):
    S = q.shape[0]
    assert S % SWA_ROWS == 0
    scale = SWA_HEAD_DIM ** -0.5
    gw = SWA_GROUP * LANES

    def body(q_ref, kp_ref, kc_ref, vp_ref, vc_ref, sink_ref, o_ref, lse_ref, do_ref, dq_ref, dk_ref, dv_ref, dsink_ref):
        n = pl.program_id(1)

        @pl.when(n == 0)
        def _():
            dk_ref[...] = jnp.zeros_like(dk_ref)
            dv_ref[...] = jnp.zeros_like(dv_ref)
            dsink_ref[...] = jnp.zeros_like(dsink_ref)

        bands = _swa_bands(n, kp_ref, kc_ref, vp_ref, vc_ref)
        P = [(j, g) for j in range(SWA_NB) for g in range(SWA_GROUP)]
        rows = lambda j: slice(j * SWA_BLK, (j + 1) * SWA_BLK)
        lanes = lambda g: slice(g * LANES, (g + 1) * LANES)
        qs = {(j, g): q_ref[rows(j), lanes(g)] for j, g in P}
        dos = {(j, g): do_ref[rows(j), lanes(g)] for j, g in P}
        lses = {(j, g): lse_ref[g, rows(j), :] for j, g in P}
        delta = {(j, g): jnp.sum(dos[j, g].astype(F32) * o_ref[rows(j), lanes(g)].astype(F32), axis=1, keepdims=True)
                 for j, g in P}
        s = {(j, g): jnp.where(bands[j][2], _dot(qs[j, g], bands[j][0], NT) * scale, -1e30) for j, g in P}
        p = {(j, g): jnp.exp(s[j, g] - lses[j, g]) for j, g in P}
        dp = {(j, g): _dot(dos[j, g], bands[j][1], NT) for j, g in P}
        ds = {(j, g): (p[j, g] * (dp[j, g] - delta[j, g]) * scale).astype(BF16) for j, g in P}
        for j, g in P:
            dq_ref[rows(j), lanes(g)] = _dot(ds[j, g], bands[j][0], NN)
        for g in range(SWA_GROUP):
            p_sink = [jnp.exp(sink_ref[:, g * LANES:g * LANES + 1] - lses[j, g]) * delta[j, g] for j in range(SWA_NB)]
            dsink_ref[:, lanes(g)] += jnp.zeros((1, LANES), F32) - jnp.sum(_total(p_sink), axis=0, keepdims=True)
        dkb = [_total(_dot(ds[j, g], qs[j, g], TN) for g in range(SWA_GROUP)) for j in range(SWA_NB)]
        dvb = [_total(_dot(p[j, g].astype(BF16), dos[j, g], TN) for g in range(SWA_GROUP)) for j in range(SWA_NB)]
        base = pl.multiple_of(n * SWA_ROWS, SWA_ROWS)
        for j in range(SWA_NB):
            own = pl.ds(base + j * SWA_BLK, SWA_BLK)
            after = j + 1 < SWA_NB
            dk_ref[own, :] += dkb[j][SWA_BLK:] + dkb[j + 1][:SWA_BLK] if after else dkb[j][SWA_BLK:]
            dv_ref[own, :] += dvb[j][SWA_BLK:] + dvb[j + 1][:SWA_BLK] if after else dvb[j][SWA_BLK:]

        @pl.when(n > 0)
        def _():
            before = pl.ds(pl.multiple_of(n * SWA_ROWS - SWA_BLK, SWA_BLK), SWA_BLK)
            dk_ref[before, :] += dkb[0][:SWA_BLK]
            dv_ref[before, :] += dvb[0][:SWA_BLK]

    return pl.pallas_call(
        body,
        name=name,
        grid=(SWA_KV_HEADS, S // SWA_ROWS),
        in_specs=_swa_in_specs(v_off) + [
            pl.BlockSpec((SWA_ROWS, gw), lambda h, n: (n, h)),
            pl.BlockSpec((SWA_GROUP, SWA_ROWS, 1), lambda h, n: (h, n, 0)),
            pl.BlockSpec((SWA_ROWS, gw), lambda h, n: (n, h)),
        ],
        out_specs=[
            pl.BlockSpec((SWA_ROWS, gw), lambda h, n: (n, h)),
            pl.BlockSpec((S, LANES), lambda h, n: (0, h)),
            pl.BlockSpec((S, LANES), lambda h, n: (0, h)),
            pl.BlockSpec((1, gw), lambda h, n: (0, h)),
        ],
        out_shape=[
            jax.ShapeDtypeStruct((S, SWA_HEADS * LANES), F32),
            jax.ShapeDtypeStruct((S, SWA_KV_HEADS * LANES), F32),
            jax.ShapeDtypeStruct((S, SWA_KV_HEADS * LANES), F32),
            jax.ShapeDtypeStruct((1, SWA_HEADS * LANES), F32),
        ],
        compiler_params=_params(("parallel", "arbitrary")),
    )(q, k, k, v, v, sink_b, o, lse, do)


def _pad_cols(w, heads, real):
    k = w.shape[0]
    return jnp.pad(w.reshape(k, heads, real), ((0, 0), (0, 0), (0, LANES - real))).reshape(k, heads * LANES)


def _unpad_cols(g, heads, real):
    k = g.shape[0]
    return g.reshape(k, heads, LANES)[:, :, :real].reshape(k, heads * real)


def _pad_rows(w, heads, real):
    n = w.shape[1]
    return jnp.pad(w.reshape(heads, real, n), ((0, 0), (0, LANES - real), (0, 0))).reshape(heads * LANES, n)


def _unpad_rows(g, heads, real):
    n = g.shape[1]
    return g.reshape(heads, LANES, n)[:, :real, :].reshape(heads * real, n)


def _w_in_internal(w_in):
    c_q, c_kv, k_r, q_swa, k_swa, v_swa, q_sb, k_sb, v_sb, gate = jnp.split(w_in, SPLIT_POINTS, axis=1)
    k_r = jnp.pad(k_r, ((0, 0), (MLA_NOPE, LANES - MLA_NOPE - MLA_ROPE)))
    w1 = jnp.concatenate([c_q, c_kv, k_r, _pad_cols(q_swa, 8, 64), _pad_cols(k_swa, 2, 64)], axis=1)
    w2 = [_pad_cols(v_swa, 2, 64), _pad_cols(q_sb, 8, 64), _pad_cols(k_sb, 8, 64), _pad_cols(v_sb, 8, 64)]
    return w1, w2, gate


def _w_in_reference(g1, g2, g3):
    c_q, c_kv, k_r, q_swa, k_swa = jnp.split(g1, [256, 384, 512, 1536], axis=1)
    v_swa, q_sb, k_sb, v_sb = g2
    return jnp.concatenate([
        c_q, c_kv, k_r[:, MLA_NOPE:MLA_NOPE + MLA_ROPE], _unpad_cols(q_swa, 8, 64), _unpad_cols(k_swa, 2, 64),
        _unpad_cols(v_swa, 2, 64), _unpad_cols(q_sb, 8, 64), _unpad_cols(k_sb, 8, 64), _unpad_cols(v_sb, 8, 64),
        g3], axis=1)


def _w_ukv_internal(w):
    w3 = w.reshape(MLA_KV_LORA, MLA_HEADS, MLA_NOPE + MLA_V)
    pad = lambda t: jnp.pad(t, ((0, 0), (0, 0), (0, LANES - t.shape[2]))).reshape(MLA_KV_LORA, MLA_HEADS * LANES)
    return pad(w3[:, :, :MLA_NOPE]), pad(w3[:, :, MLA_NOPE:])


def _w_ukv_reference(gk, gv):
    gk = gk.reshape(MLA_KV_LORA, MLA_HEADS, LANES)[:, :, :MLA_NOPE]
    gv = gv.reshape(MLA_KV_LORA, MLA_HEADS, LANES)[:, :, :MLA_V]
    return jnp.concatenate([gk, gv], axis=2).reshape(MLA_KV_LORA, MLA_HEADS * (MLA_NOPE + MLA_V))


def _layer_fwd(x, w, tabs):
    mla_tab, swa_tab = tabs
    sv = {"x": x}

    def f_norm(rows, consts):
        return [_rms(rows[0], consts[0])], []

    (h,) = _rowwise(f_norm, [x], [w["g_mix_pre"]], [(D_MODEL, BF16)], [], "norm_mix_pre")
    p1 = _matmul(h, w["w_in1"], "nn", [F32], "proj_lat")
    p2 = _matmul(h, w["w_in2"], "nn", [BF16], "proj_qkv")
    gates = _matmul(h, w["w_in3"], "nn", [BF16], "proj_gate",
                    epilogue=lambda acc, b: (1.0 / (1.0 + jnp.exp(-(acc + b))),), row_extras=[w["b_gate"]])

    def f_prep(rows, consts):
        t = rows[0]
        gq, gkv = consts[0], consts[1]
        mc, mu, md = rows[1], rows[2], rows[3]
        sc, su, sd = rows[4], rows[5], rows[6]
        cq_n = _rms(t[:, 0:256], gq)
        ckv_n = _rms(t[:, 256:384], gkv)
        kr = _rope(t[:, 384:512], mc, mu, md, MLA_ROPE // 2)
        qs = [_rope(t[:, 512 + j * LANES:512 + (j + 1) * LANES], sc, su, sd, SWA_HEAD_DIM // 2) for j in range(8)]
        ks = [_rope(t[:, 1536 + j * LANES:1536 + (j + 1) * LANES], sc, su, sd, SWA_HEAD_DIM // 2) for j in range(2)]
        return [cq_n, ckv_n, kr, jnp.concatenate(qs, axis=1), jnp.concatenate(ks, axis=1)], []

    cq_n, ckv_n, kr, q_swa, k_swa = _rowwise(
        f_prep, [p1, *mla_tab["k"], *swa_tab["f"]], [w["g_q_lat"], w["g_kv_lat"]],
        [(256, BF16), (128, BF16), (LANES, F32), (1024, BF16), (256, BF16)], [], "lat_prep")

    q_lat = _matmul(cq_n, w["w_uq"], "nn", [F32], "mla_q_up")
    k_lat = _matmul(ckv_n, w["w_ukv_k"], "nn", [F32], "mla_k_up")
    def ones_lane(acc):
        lane = lax.broadcasted_iota(jnp.int32, acc.shape, 1) % LANES
        return (jnp.where(lane == ONES_LANE, 1.0, acc),)

    v_mla = _matmul(ckv_n, w["w_ukv_v"], "nn", [BF16], "mla_v_up", epilogue=ones_lane)
    mla_scale = (MLA_NOPE + MLA_ROPE) ** -0.5

    def f_mla_prep(rows, consts):
        ql, kl, krr, mc, mu, md = rows
        qs = [_rope(ql[:, j * LANES:(j + 1) * LANES], mc, mu, md, MLA_ROPE // 2) * mla_scale for j in range(8)]
        ks = [kl[:, j * LANES:(j + 1) * LANES] + krr for j in range(8)]
        return [jnp.concatenate(qs, axis=1), jnp.concatenate(ks, axis=1)], []

    q_mla, k_mla = _rowwise(f_mla_prep, [q_lat, k_lat, kr, *mla_tab["q"]], [], [(1024, BF16), (1024, BF16)], [], "mla_prep")

    o_mla, lse_mla = _softmax_attn_fwd(q_mla, k_mla, v_mla, MLA_HEADS, "mla_fwd")
    o_swa, lse_swa = _swa_fwd(q_swa, k_swa, p2, 0, w["sink_b"], "swa_fwd")
    o_sb = _sb_attn_fwd(p2, SB_HEADS, "sb_fwd", 2, 10, 18)

    oa = _matmul(o_mla, w["w_o_mla"], "nn", [F32], "o_proj_mla")
    ob = _matmul(o_swa, w["w_o_swa"], "nn", [F32], "o_proj_swa")
    oc = _matmul(o_sb, w["w_o_sb"], "nn", [F32], "o_proj_sb")

    def f_mix(rows, consts):
        a, b, c, g = rows
        g = g.astype(F32)
        return [g[:, 0:1024] * a + g[:, 1024:2048] * b + g[:, 2048:3072] * c], []

    (mixed,) = _rowwise(f_mix, [oa, ob, oc, gates], [], [(D_MODEL, BF16)], [], "gate_mix")
    y = _matmul(mixed, w["w_out"], "nn", [F32], "out_proj")

    def f_res_norm(rows, consts):
        return [rows[0] + _rms(rows[1], consts[0])], []

    (x1,) = _rowwise(f_res_norm, [x, y], [w["g_mix_post"]], [(D_MODEL, F32)], [], "res_norm_mix")
    (h2,) = _rowwise(f_norm, [x1], [w["g_mlp_pre"]], [(D_MODEL, BF16)], [], "norm_mlp_pre")

    def relu2(acc):
        r = jnp.maximum(acc, 0.0)
        return acc, r * r

    up, u = _matmul(h2, w["w_up"], "nn", [BF16, BF16], "mlp_up", epilogue=relu2)
    zd = _matmul(u, w["w_down"], "nn", [F32], "mlp_down")
    (x2,) = _rowwise(f_res_norm, [x1, zd], [w["g_mlp_post"]], [(D_MODEL, F32)], [], "res_norm_mlp")

    sv.update(h=h, p1=p1, p2=p2, gates=gates, cq_n=cq_n, ckv_n=ckv_n, q_swa=q_swa, k_swa=k_swa, q_mla=q_mla,
              k_mla=k_mla, v_mla=v_mla, o_mla=o_mla, lse_mla=lse_mla, o_swa=o_swa, lse_swa=lse_swa, o_sb=o_sb,
              oa=oa, ob=ob, oc=oc, mixed=mixed, y=y, x1=x1, h2=h2, up=up, u=u, zd=zd)
    return x2, sv


def _layer_bwd(dx2, w, sv, tabs):
    mla_tab, swa_tab = tabs
    gr = {}

    def f_norm_bwd(rows, consts):
        dx, dg = _rms_bwd(rows[0], consts[0], rows[1])
        return [dx], [dg]

    def f_norm_bwd_res(rows, consts):
        dx, dg = _rms_bwd(rows[0], consts[0], rows[1])
        return [rows[2] + dx], [dg]

    dzd, gr["g_mlp_post"] = _rowwise(f_norm_bwd, [sv["zd"], dx2], [w["g_mlp_post"]], [(D_MODEL, BF16)], [D_MODEL], "b_norm_mlp_post")
    gr["w_down"] = _matmul(sv["u"], dzd, "tn", [F32], "b_w_down")
    dup = _matmul(dzd, w["w_down"], "nt", [BF16], "b_mlp_down",
                  epilogue=lambda acc, up: (acc * 2.0 * jnp.maximum(up.astype(F32), 0.0),), extras=[sv["up"]])
    gr["w_up"] = _matmul(sv["h2"], dup, "tn", [F32], "b_w_up")
    dh2 = _matmul(dup, w["w_up"], "nt", [F32], "b_mlp_up")
    dx1, gr["g_mlp_pre"] = _rowwise(f_norm_bwd_res, [sv["x1"], dh2, dx2], [w["g_mlp_pre"]], [(D_MODEL, F32)], [D_MODEL], "b_norm_mlp_pre")

    dy, gr["g_mix_post"] = _rowwise(f_norm_bwd, [sv["y"], dx1], [w["g_mix_post"]], [(D_MODEL, BF16)], [D_MODEL], "b_norm_mix_post")
    gr["w_out"] = _matmul(sv["mixed"], dy, "tn", [F32], "b_w_out")
    dmixed = _matmul(dy, w["w_out"], "nt", [F32], "b_out_proj")

    def f_mix_bwd(rows, consts):
        dm, a, b, c, g = rows
        g = g.astype(F32)
        outs, dls = [], []
        for j, o in enumerate((a, b, c)):
            gj = g[:, j * D_MODEL:(j + 1) * D_MODEL]
            outs.append(dm * gj)
            dls.append(dm * o * gj * (1.0 - gj))
        dl = jnp.concatenate(dls, axis=1)
        return outs + [dl], [dl]

    doa, dob, doc, dlogit, gr["b_gate"] = _rowwise(
        f_mix_bwd, [dmixed, sv["oa"], sv["ob"], sv["oc"], sv["gates"]], [],
        [(D_MODEL, BF16)] * 3 + [(P3_W, BF16)], [P3_W], "b_gate_mix")

    gr["w_o_mla"] = _matmul(sv["o_mla"], doa, "tn", [F32], "b_w_o_mla")
    gr["w_o_swa"] = _matmul(sv["o_swa"], dob, "tn", [F32], "b_w_o_swa")
    gr["w_o_sb"] = _matmul(sv["o_sb"], doc, "tn", [F32], "b_w_o_sb")
    do_mla = _matmul(doa, w["w_o_mla"], "nt", [BF16], "b_o_proj_mla")
    do_swa = _matmul(dob, w["w_o_swa"], "nt", [BF16], "b_o_proj_swa")
    do_sb = _matmul(doc, w["w_o_sb"], "nt", [BF16], "b_o_proj_sb")

    dq_sb, dk_sb, dv_sb = _sb_attn_bwd(sv["p2"], sv["o_sb"], do_sb, SB_HEADS, "sb_bwd", 2, 10, 18)
    dq_swa, dk_swa, dv_swa, dsink = _swa_bwd(sv["q_swa"], sv["k_swa"], sv["p2"], 0, w["sink_b"], sv["o_swa"],
                                             sv["lse_swa"], do_swa, "swa_bwd")
    gr["swa_sinks"] = dsink.reshape(SWA_HEADS, LANES)[:, 0]
    dq_mla, dk_mla, dv_mla = _softmax_attn_bwd(sv["q_mla"], sv["k_mla"], sv["v_mla"], sv["o_mla"], sv["lse_mla"], do_mla,
                                               MLA_HEADS, (MLA_NOPE + MLA_ROPE) ** -0.5, "mla_bwd")

    def f_mla_post(rows, consts):
        dq, dk, qc, qu, qd, kc, ku, kd = rows
        dqs = [_rope(dq[:, j * LANES:(j + 1) * LANES], qc, qu, qd, MLA_ROPE // 2) for j in range(8)]
        dkr = dk[:, 0:LANES]
        for j in range(1, 8):
            dkr = dkr + dk[:, j * LANES:(j + 1) * LANES]
        return [jnp.concatenate(dqs, axis=1), _rope(dkr, kc, ku, kd, MLA_ROPE // 2)], []

    dq_lat, dkr = _rowwise(f_mla_post, [dq_mla, dk_mla, *mla_tab["q_inv"], *mla_tab["k_inv"]], [],
                           [(1024, BF16), (LANES, F32)], [], "b_mla_post")
    gr["w_uq"] = _matmul(sv["cq_n"], dq_lat, "tn", [F32], "b_w_uq")
    gr["w_ukv_k"] = _matmul(sv["ckv_n"], dk_mla, "tn", [F32], "b_w_ukv_k")
    gr["w_ukv_v"] = _matmul(sv["ckv_n"], dv_mla, "tn", [F32], "b_w_ukv_v")
    dcq_n = _matmul(dq_lat, w["w_uq"], "nt", [F32], "b_mla_q_up")
    dckv_a = _matmul(dk_mla, w["w_ukv_k"], "nt", [F32], "b_mla_k_up")
    dckv_b = _matmul(dv_mla, w["w_ukv_v"], "nt", [F32], "b_mla_v_up")

    def f_prep_bwd(rows, consts):
        t, dcq, dca, dcb, dkr_, dqs, dks, sc, su, sd = rows
        gq, gkv = consts
        dc_q, dgq = _rms_bwd(t[:, 0:256], gq, dcq)
        dc_kv, dgkv = _rms_bwd(t[:, 256:384], gkv, dca + dcb)
        q_parts = [_rope(dqs[:, j * LANES:(j + 1) * LANES], sc, su, sd, SWA_HEAD_DIM // 2) for j in range(8)]
        k_parts = [_rope(dks[:, j * LANES:(j + 1) * LANES], sc, su, sd, SWA_HEAD_DIM // 2) for j in range(2)]
        return [jnp.concatenate([dc_q, dc_kv, dkr_] + q_parts + k_parts, axis=1)], [dgq, dgkv]

    dp1, gr["g_q_lat"], gr["g_kv_lat"] = _rowwise(
        f_prep_bwd, [sv["p1"], dcq_n, dckv_a, dckv_b, dkr, dq_swa, dk_swa, *swa_tab["inv"]], [w["g_q_lat"], w["g_kv_lat"]],
        [(P1_W, BF16)], [256, 128], "b_lat_prep")

    gr["w_in1"] = _matmul(sv["h"], dp1, "tn", [F32], "b_w_in_lat")
    dh = _matmul(dp1, w["w_in1"], "nt", [F32], "b_proj_lat")
    gr["w_in2"] = []
    add_prev = lambda acc, prev: (acc + prev,)
    for piece, wp, tag in zip((dv_swa, dq_sb, dk_sb, dv_sb), w["w_in2_parts"], ("vswa", "qsb", "ksb", "vsb")):
        gr["w_in2"].append(_matmul(sv["h"], piece, "tn", [F32], "b_w_in_" + tag))
        dh = _matmul(piece, wp, "nt", [F32], "b_proj_" + tag, epilogue=add_prev, extras=[dh])
    gr["w_in3"] = _matmul(sv["h"], dlogit, "tn", [F32], "b_w_in_gate")
    dh = _matmul(dlogit, w["w_in3"], "nt", [F32], "b_proj_gate", epilogue=add_prev, extras=[dh])
    dx, gr["g_mix_pre"] = _rowwise(f_norm_bwd_res, [sv["x"], dh, dx1], [w["g_mix_pre"]], [(D_MODEL, F32)], [D_MODEL], "b_norm_mix_pre")
    return dx, gr


def _local_step(x, positions, loss_target, full):
    mc, mu, md = _rope_tables(positions, MLA_NOPE, MLA_ROPE, True)
    kc, ku, kd = _rope_tables(positions, MLA_NOPE, MLA_ROPE, False)
    sc, su, sd = _rope_tables(positions, 0, SWA_HEAD_DIM, False)
    mla_tab = {"q": (mc, mu, md), "k": (kc, ku, kd), "q_inv": (mc, -mu, -md), "k_inv": (kc, -ku, -kd)}
    swa_tab = {"f": (sc, su, sd), "inv": (sc, -su, -sd)}
    tabs = (mla_tab, swa_tab)

    layers = []
    for l in range(DEPTH):
        w1, w2, w3 = _w_in_internal(full["w_in"][l].astype(BF16))
        uk, uv = _w_ukv_internal(full["w_ukv"][l].astype(BF16))
        layers.append({
            "w_in1": w1, "w_in2": jnp.concatenate(w2, axis=1), "w_in2_parts": w2, "w_in3": w3,
            "w_uq": _pad_cols(full["w_uq"][l].astype(BF16), MLA_HEADS, MLA_NOPE + MLA_ROPE),
            "w_ukv_k": uk, "w_ukv_v": uv,
            "w_o_mla": _pad_rows(full["w_o_mla"][l].astype(BF16), 8, 64),
            "w_o_swa": _pad_rows(full["w_o_swa"][l].astype(BF16), 8, 64),
            "w_o_sb": _pad_rows(full["w_o_sb"][l].astype(BF16), 8, 64),
            "w_out": full["w_out"][l].astype(BF16), "w_up": full["w_up"][l].astype(BF16),
            "w_down": full["w_down"][l].astype(BF16),
            "g_mix_pre": full["g_mix_pre"][l][None], "b_gate": full["b_gate"][l][None],
            "g_q_lat": full["g_q_lat"][l][None], "g_kv_lat": full["g_kv_lat"][l][None],
            "g_mix_post": full["g_mix_post"][l][None], "g_mlp_pre": full["g_mlp_pre"][l][None],
            "g_mlp_post": full["g_mlp_post"][l][None],
            "sink_b": jnp.repeat(full["swa_sinks"][l], LANES)[None],
        })

    saved = []
    h = x
    for l in range(DEPTH):
        h, sv = _layer_fwd(h, layers[l], tabs)
        saved.append(sv)

    def f_loss(rows, consts):
        err = rows[0] - rows[1]
        return [err * (1.0 / D_MODEL)], [jnp.sum(err * err, axis=1, keepdims=True)]

    dy, sq = _rowwise(f_loss, [h, loss_target], [], [(D_MODEL, F32)], [1], "loss_head")
    loss_part = sq * (0.5 / D_MODEL)

    grads = [None] * DEPTH
    d = dy
    for l in reversed(range(DEPTH)):
        d, gr = _layer_bwd(d, layers[l], saved[l], tabs)
        grads[l] = {
            "g_mix_pre": gr["g_mix_pre"][0], "w_in": _w_in_reference(gr["w_in1"], gr["w_in2"], gr["w_in3"]),
            "b_gate": gr["b_gate"][0], "g_q_lat": gr["g_q_lat"][0], "g_kv_lat": gr["g_kv_lat"][0],
            "w_uq": _unpad_cols(gr["w_uq"], MLA_HEADS, MLA_NOPE + MLA_ROPE),
            "w_ukv": _w_ukv_reference(gr["w_ukv_k"], gr["w_ukv_v"]), "swa_sinks": gr["swa_sinks"],
            "w_o_mla": _unpad_rows(gr["w_o_mla"], 8, 64), "w_o_swa": _unpad_rows(gr["w_o_swa"], 8, 64),
            "w_o_sb": _unpad_rows(gr["w_o_sb"], 8, 64), "w_out": gr["w_out"], "g_mix_post": gr["g_mix_post"][0],
            "g_mlp_pre": gr["g_mlp_pre"][0], "w_up": gr["w_up"], "w_down": gr["w_down"], "g_mlp_post": gr["g_mlp_post"][0],
        }
    stacked = {n: jnp.stack([grads[l][n] for l in range(DEPTH)]) for n in WEIGHTS}
    return loss_part, d, stacked


def _lane_padded(width):
    return -(-width // LANES) * LANES


def _rows_of(a, dtype):
    extra = _lane_padded(a.shape[-1]) - a.shape[-1]
    if extra:
        a = jnp.pad(a, [(0, 0)] * (a.ndim - 1) + [(0, extra)])
    return a.astype(dtype).reshape(-1, LANES)


def _pack(shards, small, dtype):
    parts = [_rows_of(shards[n], dtype) for n in SHARDED]
    if small is not None:
        parts += [_rows_of(small[n], dtype) for n in SMALL]
    slab = jnp.concatenate(parts, axis=0)
    pad = (-slab.shape[0]) % SLAB_ROW_ALIGN
    return jnp.pad(slab, ((0, pad), (0, 0)))


def _unpack(slab, shard_shapes, small_shapes):
    out, r = {}, 0
    shapes = [(n, shard_shapes[n]) for n in SHARDED]
    if small_shapes is not None:
        shapes += [(n, small_shapes[n]) for n in SMALL]
    for n, shape in shapes:
        wide = shape[:-1] + (_lane_padded(shape[-1]),)
        rows = int(np.prod(wide)) // LANES
        out[n] = slab[r:r + rows].reshape(wide)[..., :shape[-1]]
        r += rows
    return out


def _chip_exchange(src, name):
    rows = src.shape[-2]

    def body(src_ref, out_ref, send_sems, recv_sems):
        x, y, c = lax.axis_index("x"), lax.axis_index("y"), lax.axis_index("c")
        me = 2 * x + y
        chips = [(1 - x, y), (x, 1 - y), (1 - x, 1 - y)]
        sends = []
        for k, (cx, cy) in enumerate(chips):
            cp = pltpu.make_async_remote_copy(
                src_ref=src_ref.at[2 * cx + cy], dst_ref=out_ref.at[me], send_sem=send_sems.at[k],
                recv_sem=recv_sems.at[k], device_id=(cx, cy, c), device_id_type=pl.DeviceIdType.MESH)
            cp.start()
            sends.append(cp)
        for k, (cx, cy) in enumerate(chips):
            pltpu.make_async_remote_copy(
                src_ref=src_ref.at[me], dst_ref=out_ref.at[2 * cx + cy], send_sem=send_sems.at[k],
                recv_sem=recv_sems.at[k], device_id=(cx, cy, c), device_id_type=pl.DeviceIdType.MESH).wait_recv()
        for cp in sends:
            cp.wait_send()

    return pl.pallas_call(
        body,
        name=name,
        in_specs=[pl.BlockSpec(memory_space=pl.ANY)],
        out_specs=pl.BlockSpec(memory_space=pl.ANY),
        out_shape=jax.ShapeDtypeStruct((N_CHIPS, rows, LANES), src.dtype),
        scratch_shapes=[pltpu.SemaphoreType.DMA((3,)), pltpu.SemaphoreType.DMA((3,))],
    )(src)


def _half_rows(c, half):
    return pl.ds(pl.multiple_of(c * half, SLAB_ROW_ALIGN // 2), half)


def _gather_weights(src, name):
    rows = src.shape[0]
    half = rows // 2

    def body(src_ref, out_ref, send_sems, recv_sems):
        x, y, c = lax.axis_index("x"), lax.axis_index("y"), lax.axis_index("c")
        me = 2 * x + y
        chips = [(1 - x, y), (x, 1 - y), (1 - x, 1 - y)]

        def copy(k, src_view, slab, part, to):
            return pltpu.make_async_remote_copy(
                src_ref=src_view, dst_ref=out_ref.at[slab, _half_rows(part, half), :], send_sem=send_sems.at[k],
                recv_sem=recv_sems.at[k], device_id=to, device_id_type=pl.DeviceIdType.MESH)

        sends = [copy(k, src_ref.at[_half_rows(c, half), :], me, c, (cx, cy, c)) for k, (cx, cy) in enumerate(chips)]
        for cp in sends:
            cp.start()
        for k, (cx, cy) in enumerate(chips):
            j = 2 * cx + cy
            landed = out_ref.at[j, _half_rows(c, half), :]
            copy(k, landed, j, c, (cx, cy, c)).wait_recv()
            fwd = copy(3 + k, landed, j, c, (x, y, 1 - c))
            fwd.start()
            sends.append(fwd)
        for k, (cx, cy) in enumerate(chips):
            j = 2 * cx + cy
            copy(3 + k, out_ref.at[j, _half_rows(1 - c, half), :], j, 1 - c, (x, y, 1 - c)).wait_recv()
        for cp in sends:
            cp.wait_send()

    return pl.pallas_call(
        body,
        name=name,
        in_specs=[pl.BlockSpec(memory_space=pl.ANY)],
        out_specs=pl.BlockSpec(memory_space=pl.ANY),
        out_shape=jax.ShapeDtypeStruct((N_CHIPS, rows, LANES), src.dtype),
        scratch_shapes=[pltpu.SemaphoreType.DMA((6,)), pltpu.SemaphoreType.DMA((6,))],
    )(src)


def _sibling_halves(src, name):
    n, rows, _ = src.shape
    half = rows // 2

    def body(src_ref, out_ref, send_sem, recv_sem):
        x, y, c = lax.axis_index("x"), lax.axis_index("y"), lax.axis_index("c")
        cp = pltpu.make_async_remote_copy(
            src_ref=src_ref.at[:, _half_rows(1 - c, half), :], dst_ref=out_ref, send_sem=send_sem, recv_sem=recv_sem,
            device_id=(x, y, 1 - c), device_id_type=pl.DeviceIdType.MESH)
        cp.start()
        cp.wait()

    return pl.pallas_call(
        body,
        name=name,
        in_specs=[pl.BlockSpec(memory_space=pl.ANY)],
        out_specs=pl.BlockSpec(memory_space=pl.ANY),
        out_shape=jax.ShapeDtypeStruct((n, half, LANES), src.dtype),
        scratch_shapes=[pltpu.SemaphoreType.DMA, pltpu.SemaphoreType.DMA],
    )(src)


def _sibling_join(src, name):
    half = src.shape[0]

    def body(src_ref, out_ref, send_sem, recv_sem):
        x, y, c = lax.axis_index("x"), lax.axis_index("y"), lax.axis_index("c")
        cp = pltpu.make_async_remote_copy(
            src_ref=src_ref, dst_ref=out_ref.at[_half_rows(c, half), :], send_sem=send_sem, recv_sem=recv_sem,
            device_id=(x, y, 1 - c), device_id_type=pl.DeviceIdType.MESH)
        cp.start()
        pltpu.make_async_remote_copy(
            src_ref=src_ref, dst_ref=out_ref.at[_half_rows(1 - c, half), :], send_sem=send_sem, recv_sem=recv_sem,
            device_id=(x, y, 1 - c), device_id_type=pl.DeviceIdType.MESH).wait_recv()
        cp.wait_send()

    return pl.pallas_call(
        body,
        name=name,
        in_specs=[pl.BlockSpec(memory_space=pl.ANY)],
        out_specs=pl.BlockSpec(memory_space=pl.ANY),
        out_shape=jax.ShapeDtypeStruct((2 * half, LANES), src.dtype),
        scratch_shapes=[pltpu.SemaphoreType.DMA, pltpu.SemaphoreType.DMA],
    )(src)


SUM_ROWS = 1024


def _pair_sum(mine, theirs, c, name):
    n, half, _ = theirs.shape
    blocks = half // SUM_ROWS

    def body(c_ref, a_ref, b_ref, o_ref):
        o_ref[...] = (a_ref[...].astype(F32) + b_ref[...].astype(F32)).astype(o_ref.dtype)

    return pl.pallas_call(
        body,
        name=name,
        grid_spec=pltpu.PrefetchScalarGridSpec(
            num_scalar_prefetch=1,
            grid=(blocks,),
            in_specs=[pl.BlockSpec((n, SUM_ROWS, LANES), lambda i, c_ref: (0, c_ref[0] * blocks + i, 0)),
                      pl.BlockSpec((n, SUM_ROWS, LANES), lambda i, c_ref: (0, i, 0))],
            out_specs=pl.BlockSpec((n, SUM_ROWS, LANES), lambda i, c_ref: (0, i, 0)),
        ),
        out_shape=jax.ShapeDtypeStruct((n, half, LANES), BF16),
        compiler_params=_params(("arbitrary",)),
    )(jnp.reshape(c, (1,)).astype(jnp.int32), mine, theirs)


def _sum_chips(own, landed, me, name):
    rows = landed.shape[1]

    def body(me_ref, a_ref, b_ref, o_ref):
        t = [jnp.where(me_ref[0] == j, a_ref[j], b_ref[j]).astype(F32) for j in range(N_CHIPS)]
        o_ref[...] = ((t[0] + t[1]) + t[2]) + t[3]

    slabs = pl.BlockSpec((N_CHIPS, SUM_ROWS, LANES), lambda i, me_ref: (0, i, 0))
    return pl.pallas_call(
        body,
        name=name,
        grid_spec=pltpu.PrefetchScalarGridSpec(
            num_scalar_prefetch=1,
            grid=(rows // SUM_ROWS,),
            in_specs=[slabs, slabs],
            out_specs=pl.BlockSpec((SUM_ROWS, LANES), lambda i, me_ref: (i, 0)),
        ),
        out_shape=jax.ShapeDtypeStruct((rows, LANES), F32),
        compiler_params=_params(("arbitrary",)),
    )(jnp.reshape(me, (1,)).astype(jnp.int32), own, landed)


def _adamw(w, m, v, g, name):
    shape = w.shape
    flat = lambda a: a.reshape(-1, shape[-1])

    def fn(rows, consts):
        w_, m_, v_, g_ = rows
        m_new = ADAM_B1 * m_ + (1.0 - ADAM_B1) * g_
        v_new = ADAM_B2 * v_ + (1.0 - ADAM_B2) * (g_ * g_)
        m_hat = m_new / (1.0 - ADAM_B1 ** ADAM_STEP)
        v_hat = v_new / (1.0 - ADAM_B2 ** ADAM_STEP)
        delta = -ADAM_LR * (m_hat / (jnp.sqrt(v_hat) + ADAM_EPS) + ADAM_WD * w_)
        return [delta, m_new, v_new], []

    outs = _rowwise(fn, [flat(w), flat(m), flat(v), flat(g)], [], [(shape[-1], F32)] * 3, [], name)
    return [o.reshape(shape) for o in outs]


def kernel(x, positions, g_mix_pre, w_in, b_gate, g_q_lat, g_kv_lat, w_uq, w_ukv, swa_sinks, w_o_mla, w_o_swa, w_o_sb, w_out, g_mix_post, g_mlp_pre, w_up, w_down, g_mlp_post, loss_target, m_g_mix_pre, m_w_in, m_b_gate, m_g_q_lat, m_g_kv_lat, m_w_uq, m_w_ukv, m_swa_sinks, m_w_o_mla, m_w_o_swa, m_w_o_sb, m_w_out, m_g_mix_post, m_g_mlp_pre, m_w_up, m_w_down, m_g_mlp_post, v_g_mix_pre, v_w_in, v_b_gate, v_g_q_lat, v_g_kv_lat, v_w_uq, v_w_ukv, v_swa_sinks, v_w_o_mla, v_w_o_swa, v_w_o_sb, v_w_out, v_g_mix_post, v_g_mlp_pre, v_w_up, v_w_down, v_g_mlp_post):
    given = dict(locals())
    wts = {n: given[n] for n in WEIGHTS}
    mom_m = {n: given["m_" + n] for n in WEIGHTS}
    mom_v = {n: given["v_" + n] for n in WEIGHTS}
    shard_shapes = {n: wts[n].shape for n in SHARDED}
    small_shapes = {n: wts[n].shape for n in SMALL}

    me = 2 * lax.axis_index("x") + lax.axis_index("y")
    core = lax.axis_index("c")
    gathered = _gather_weights(_pack(wts, None, BF16), "gather_weights")
    full = {n: wts[n] for n in SMALL}
    per_chip = [_unpack(gathered[j], shard_shapes, None) for j in range(N_CHIPS)]
    for n in SHARDED:
        own = wts[n].astype(BF16)
        full[n] = jnp.concatenate([jnp.where(me == j, own, per_chip[j][n]) for j in range(N_CHIPS)], axis=SHARD_AXIS[n])

    loss_part, grad_x, grads = _local_step(x[0], positions[0], loss_target[0], full)
    loss = lax.psum(loss_part[0, 0], ("x", "y", "c"))

    small_g = {n: grads[n] for n in SMALL}
    slabs = []
    for j in range(N_CHIPS):
        shard = {n: jnp.split(grads[n], N_CHIPS, axis=SHARD_AXIS[n])[j] for n in SHARDED}
        slabs.append(_pack(shard, small_g, BF16))
    per_chip_g = jnp.stack(slabs)
    theirs = _sibling_halves(per_chip_g, "pair_grads")
    pair = _pair_sum(per_chip_g, theirs, core, "sum_pair")
    landed = _chip_exchange(pair, "scatter_grads")
    my_half = _sum_chips(pair, landed, me, "sum_chips")
    g_slab = lax.dynamic_update_slice(_sibling_join(my_half, "join_grads"), my_half, (core * my_half.shape[0], 0))

    g = _unpack(g_slab, shard_shapes, small_shapes)
    stepped = {n: _adamw(wts[n], mom_m[n], mom_v[n], g[n], "adamw_" + n) for n in WEIGHTS}
    outs = [loss, grad_x[None]] + [g[n] for n in WEIGHTS]
    for part in range(3):
        outs += [stepped[n][part] for n in WEIGHTS]
    return tuple(outs)
```

```python
import numpy as np
import jax
import jax.numpy as jnp
from jax import lax
from jax.experimental import pallas as pl
from jax.experimental.pallas import tpu as pltpu

F32 = jnp.float32
BF16 = jnp.bfloat16

D_MODEL = 1024
DEPTH = 4
MLA_HEADS, MLA_Q_LORA, MLA_KV_LORA, MLA_NOPE, MLA_ROPE, MLA_V = 8, 256, 128, 64, 32, 64
SWA_HEADS, SWA_KV_HEADS, SWA_HEAD_DIM, SWA_WINDOW = 8, 2, 64, 128
SB_HEADS, SB_HEAD_DIM = 8, 64
D_FF = 4 * D_MODEL
ROPE_THETA = 10000.0
EPS = 1e-6
SPLIT_SIZES = (256, 128, 32, 512, 128, 128, 512, 512, 512, 3 * D_MODEL)
SPLIT_POINTS = [int(v) for v in np.cumsum(SPLIT_SIZES)[:-1]]

ADAM_LR, ADAM_B1, ADAM_B2, ADAM_EPS, ADAM_WD, ADAM_STEP = 0.001, 0.9, 0.999, 1e-08, 0.01, 10

LANES = 128
V7X_VMEM_BYTES = 64 * 1024 * 1024
VMEM_LIMIT = V7X_VMEM_BYTES - 8 * 1024 * 1024
MATMUL_VMEM_BUDGET = 36 * 1024 * 1024
N_CHIPS = 4
SLAB_ROW_ALIGN = 2048

P1_W = 256 + 128 + 128 + 1024 + 256
P2_W = 256 + 1024 + 1024 + 1024
P3_W = 3 * D_MODEL

SHARDED = ("w_in", "w_uq", "w_ukv", "w_o_mla", "w_o_swa", "w_o_sb", "w_out", "w_up", "w_down")
SHARD_AXIS = {"w_in": 2, "w_uq": 2, "w_ukv": 2, "w_o_mla": 2, "w_o_swa": 2, "w_o_sb": 2, "w_out": 1, "w_up": 2, "w_down": 1}
SMALL = ("g_mix_pre", "b_gate", "g_q_lat", "g_kv_lat", "swa_sinks", "g_mix_post", "g_mlp_pre", "g_mlp_post")
WEIGHTS = ("g_mix_pre", "w_in", "b_gate", "g_q_lat", "g_kv_lat", "w_uq", "w_ukv", "swa_sinks", "w_o_mla", "w_o_swa",
           "w_o_sb", "w_out", "g_mix_post", "g_mlp_pre", "w_up", "w_down", "g_mlp_post")

NN = (((1,), (0,)), ((), ()))
NT = (((1,), (1,)), ((), ()))
TN = (((0,), (0,)), ((), ()))


def _dot(a, b, dims):
    return lax.dot_general(a, b, dims, preferred_element_type=F32)


def _params(sem):
    return pltpu.CompilerParams(dimension_semantics=sem, vmem_limit_bytes=VMEM_LIMIT)


def _largest_tile(n, cap):
    if n <= cap:
        return n
    best = LANES
    for t in range(LANES, cap + 1, LANES):
        if n % t == 0:
            best = t
    return best


def _matmul_tiles(M, N, K, a_bytes, b_bytes, out_bytes, extra_bytes):
    tn = _largest_tile(N, 1792)
    tm = _largest_tile(M, 1024 if tn <= 1024 else 512)
    tk = _largest_tile(K, 2048)

    def need(tm_, tk_):
        acc = 4 * tm_ * tn if tk_ < K else 0
        return 2 * (tm_ * tk_ * a_bytes + tk_ * tn * b_bytes + tm_ * tn * (out_bytes + extra_bytes)) + acc

    while need(tm, tk) > MATMUL_VMEM_BUDGET:
        if tk >= tm and tk % 256 == 0:
            tk //= 2
        elif tm % 256 == 0:
            tm //= 2
        else:
            break
    return tm, tn, tk


def _matmul(a, b, mode, out_dtypes, name, epilogue=None, extras=(), row_extras=()):
    b_layer = None
    if isinstance(b, tuple):
        b, b_layer = b
    b_shape = b.shape[-2:]
    if mode == "nn":
        (M, K), (K2, N) = a.shape, b_shape
    elif mode == "nt":
        (M, K), (N, K2) = a.shape, b_shape
    else:
        (K, M), (K2, N) = a.shape, b_shape
    assert K == K2, (name, a.shape, b.shape)
    tm, tn, tk = _matmul_tiles(
        M, N, K, a.dtype.itemsize, b.dtype.itemsize, sum(jnp.dtype(d).itemsize for d in out_dtypes),
        sum(e.dtype.itemsize for e in extras))
    assert M % tm == 0 and N % tn == 0 and K % tk == 0, (name, M, N, K, tm, tn, tk)
    nk = K // tk
    if mode == "tn":
        a_spec = pl.BlockSpec((tk, tm), lambda i, j, k: (k, i))
    else:
        a_spec = pl.BlockSpec((tm, tk), lambda i, j, k: (i, k))
    b_block, b_index = ((tn, tk), lambda i, j, k: (j, k)) if mode == "nt" else ((tk, tn), lambda i, j, k: (k, j))
    if b_layer is None:
        b_spec = pl.BlockSpec(b_block, b_index)
    else:
        b_spec = pl.BlockSpec((None,) + b_block, lambda i, j, k: (b_layer,) + b_index(i, j, k))
    dims = {"nn": NN, "nt": NT, "tn": TN}[mode]
    n_ex, n_rex, n_out = len(extras), len(row_extras), len(out_dtypes)

    def body(*refs):
        a_ref, b_ref = refs[:2]
        ex = refs[2:2 + n_ex]
        rex = refs[2 + n_ex:2 + n_ex + n_rex]
        outs = refs[2 + n_ex + n_rex:2 + n_ex + n_rex + n_out]

        def finish(total):
            res = (total,) if epilogue is None else epilogue(total, *[e[...] for e in ex], *[e[...] for e in rex])
            for o, r in zip(outs, res):
                o[...] = r.astype(o.dtype)

        part = _dot(a_ref[...].astype(BF16), b_ref[...].astype(BF16), dims)
        if nk == 1:
            finish(part)
            return
        acc = refs[-1]
        k = pl.program_id(2)

        @pl.when(k == 0)
        def _():
            acc[...] = part

        @pl.when(k > 0)
        def _():
            acc[...] += part

        @pl.when(k == nk - 1)
        def _():
            finish(acc[...])

    in_specs = [a_spec, b_spec]
    in_specs += [pl.BlockSpec((tm, tn), lambda i, j, k: (i, j)) for _ in extras]
    in_specs += [pl.BlockSpec((1, tn), lambda i, j, k: (0, j)) for _ in row_extras]
    out = pl.pallas_call(
        body,
        name=name,
        grid=(M // tm, N // tn, nk),
        in_specs=in_specs,
        out_specs=[pl.BlockSpec((tm, tn), lambda i, j, k: (i, j)) for _ in out_dtypes],
        out_shape=[jax.ShapeDtypeStruct((M, N), dt) for dt in out_dtypes],
        scratch_shapes=[pltpu.VMEM((tm, tn), F32)] if nk > 1 else [],
        compiler_params=_params(("parallel", "parallel", "arbitrary")),
    )(a, b, *extras, *row_extras)
    return out[0] if n_out == 1 else out


ROWWISE_ROW_BYTES = 16 * 1024


def _rowwise(fn, rows, consts, out_defs, sum_widths, name):
    R = rows[0].shape[0]
    per_row = sum(r.shape[1] * r.dtype.itemsize for r in rows) + sum(w * jnp.dtype(dt).itemsize for w, dt in out_defs)
    bm = 512 if per_row <= ROWWISE_ROW_BYTES else 256
    while R % bm:
        bm //= 2
    bm = max(bm, 1)
    n_r, n_c, n_o = len(rows), len(consts), len(out_defs)
    n_s = len(sum_widths)

    def body(*refs):
        r_in = refs[:n_r]
        c_in = refs[n_r:n_r + n_c]
        o_refs = refs[n_r + n_c:n_r + n_c + n_o]
        s_refs = refs[n_r + n_c + n_o:]
        outs, sums = fn([r[...] for r in r_in], [c[...] for c in c_in])
        for o, val in zip(o_refs, outs):
            o[...] = val.astype(o.dtype)
        if n_s:
            @pl.when(pl.program_id(0) == 0)
            def _():
                for s in s_refs:
                    s[...] = jnp.zeros_like(s)

            for s, val in zip(s_refs, sums):
                s[...] += jnp.sum(val, axis=0, keepdims=True)

    in_specs = [pl.BlockSpec((bm, r.shape[1]), lambda i: (i, 0)) for r in rows]
    in_specs += [pl.BlockSpec(c.shape, lambda i: (0, 0)) for c in consts]
    out_specs = [pl.BlockSpec((bm, w), lambda i: (i, 0)) for w, _ in out_defs]
    out_specs += [pl.BlockSpec((1, w), lambda i: (0, 0)) for w in sum_widths]
    out_shape = [jax.ShapeDtypeStruct((R, w), dt) for w, dt in out_defs]
    out_shape += [jax.ShapeDtypeStruct((1, w), F32) for w in sum_widths]
    return pl.pallas_call(
        body,
        name=name,
        grid=(R // bm,),
        in_specs=in_specs,
        out_specs=out_specs,
        out_shape=out_shape,
        compiler_params=_params(("arbitrary",)),
    )(*rows, *consts)


def _rms(x, g):
    r = lax.rsqrt(jnp.mean(x * x, axis=-1, keepdims=True) + EPS)
    return x * r * g


def _rms_bwd(x, g, dy):
    r = lax.rsqrt(jnp.mean(x * x, axis=-1, keepdims=True) + EPS)
    n = x * r
    dn = dy * g
    dx = r * (dn - n * jnp.mean(dn * n, axis=-1, keepdims=True))
    return dx, dy * n


def _rope(x, c, s_up, s_dn, half):
    return x * c + pltpu.roll(x, half, 1) * s_up + pltpu.roll(x, LANES - half, 1) * s_dn


def _rope_tables(positions, lo, d, nope_pass):
    S = positions.shape[0]
    half = d // 2
    inv = 1.0 / (ROPE_THETA ** (jnp.arange(0, d, 2, dtype=F32) / d))
    ang = positions.astype(F32)[:, None] * inv
    cos, sin = jnp.cos(ang), jnp.sin(ang)
    z = lambda n: jnp.zeros((S, n), F32)
    head = jnp.ones((S, lo), F32) if nope_pass else z(lo)
    tail = LANES - lo - d
    c = jnp.concatenate([head, cos, cos, z(tail)], axis=1)
    s_up = jnp.concatenate([z(lo), z(half), sin, z(tail)], axis=1)
    s_dn = jnp.concatenate([z(lo), -sin, z(half), z(tail)], axis=1)
    return c, s_up, s_dn


MLA_FWD_CFG = (2, 1024)
MLA_BWD_CFG = (2, 512)
SB_FWD_CFG = (2, 256)
SB_BWD_CFG = (4, 256)


def _tile_mask(bk, strict):
    row = lax.broadcasted_iota(jnp.int32, (bk, bk), 0)
    col = lax.broadcasted_iota(jnp.int32, (bk, bk), 1)
    return (col < row) if strict else (col <= row)


def _att_layout(S, cfg):
    nch, bk = cfg
    bq = nch * bk
    assert S % bq == 0, (S, cfg)
    rows = [slice(r * bk, (r + 1) * bk) for r in range(nch)]
    q_spec = lambda off=0: pl.BlockSpec((bq, LANES), lambda h, i: (i, off + h))
    kv_spec = lambda off=0: pl.BlockSpec((S, LANES), lambda h, i: (0, off + h))
    return bq, rows, q_spec, kv_spec


def _total(terms):
    terms = list(terms)
    out = terms[0]
    for t in terms[1:]:
        out = out + t
    return out


def _walk(nch, i, step, carry, leftward, alive=None):
    everyone = range(nch)
    if leftward:
        for d in reversed(everyone):
            carry = step(nch * i + d, carry, range(d, nch), {d})
        if alive is None:
            return lax.fori_loop(0, nch * i, lambda t, c: step(nch * i - 1 - t, c, everyone, set()), carry)
        more = lambda tc: jnp.logical_and(tc[0] < nch * i, alive(tc[1]))
        left = lambda tc: (tc[0] + 1, step(nch * i - 1 - tc[0], tc[1], everyone, set()))
        return lax.while_loop(more, left, (jnp.int32(0), carry))[1]
    carry = lax.fori_loop(0, nch * i, lambda kb, c: step(kb, c, everyone, set()), carry)
    for d in everyone:
        carry = step(nch * i + d, carry, range(d, nch), {d})
    return carry


ONES_LANE = MLA_V


def _softmax_attn_fwd(q, k, v, heads, name, q_off=0, k_off=0, v_off=0):
    S = q.shape[0]
    nch, bk = MLA_FWD_CFG
    bq, rows, q_spec, kv_spec = _att_layout(S, MLA_FWD_CFG)

    def body(q_ref, k_ref, v_ref, o_ref, lse_ref):
        i = pl.program_id(1)
        qs = [q_ref[rw, :] for rw in rows]

        def step(kb, cs, active, masked):
            off = pl.multiple_of(kb * bk, bk)
            ks, vs = k_ref[pl.ds(off, bk), :], v_ref[pl.ds(off, bk), :]
            A = list(active)
            s = {r: _dot(qs[r], ks, NT) for r in A}
            s = {r: (jnp.where(_tile_mask(bk, False), s[r], -1e30) if r in masked else s[r]) for r in A}
            m_new = {r: jnp.maximum(cs[r][0], jnp.max(s[r], axis=1, keepdims=True)) for r in A}
            p = {r: jnp.exp(s[r] - m_new[r]) for r in A}
            alpha = {r: jnp.exp(cs[r][0] - m_new[r]) for r in A}
            new = list(cs)
            for r in A:
                new[r] = (m_new[r], alpha[r] * cs[r][1] + _dot(p[r].astype(BF16), vs, NN))
            return tuple(new)

        init = (jnp.full((bk, 1), -1e30, F32), jnp.zeros((bk, LANES), F32))
        cs = _walk(nch, i, step, tuple(init for _ in rows), False)
        for r, (m, acc) in enumerate(cs):
            l = acc[:, ONES_LANE:ONES_LANE + 1]
            o_ref[rows[r], :] = (acc / l).astype(o_ref.dtype)
            lse_ref[rows[r], :] = m + jnp.log(l)

    return pl.pallas_call(
        body,
        name=name,
        grid=(heads, S // bq),
        in_specs=[q_spec(q_off), kv_spec(k_off), kv_spec(v_off)],
        out_specs=[q_spec(), pl.BlockSpec((None, bq, 1), lambda h, i: (h, i, 0))],
        out_shape=[jax.ShapeDtypeStruct((S, heads * LANES), BF16), jax.ShapeDtypeStruct((heads, S, 1), F32)],
        compiler_params=_params(("parallel", "arbitrary")),
    )(q, k, v)


def _softmax_attn_bwd(q, k, v, o, lse, do, heads, scale, name, q_off=0, k_off=0, v_off=0):
    S = q.shape[0]
    nch, bk = MLA_BWD_CFG
    bq, rows, q_spec, kv_spec = _att_layout(S, MLA_BWD_CFG)

    def body(q_ref, k_ref, v_ref, o_ref, lse_ref, do_ref, dq_ref, dk_ref, dv_ref):
        i = pl.program_id(1)

        @pl.when(i == 0)
        def _():
            dk_ref[...] = jnp.zeros_like(dk_ref)
            dv_ref[...] = jnp.zeros_like(dv_ref)

        qs = [q_ref[rw, :] for rw in rows]
        dos = [do_ref[rw, :] for rw in rows]
        lses = [lse_ref[rw, :] for rw in rows]
        deltas = [jnp.sum(dos[r].astype(F32) * o_ref[rows[r], :].astype(F32), axis=1, keepdims=True) for r in range(nch)]

        def step(kb, dqs, active, masked):
            off = pl.multiple_of(kb * bk, bk)
            ks, vs = k_ref[pl.ds(off, bk), :], v_ref[pl.ds(off, bk), :]
            A = list(active)
            s = {r: _dot(qs[r], ks, NT) for r in A}
            s = {r: (jnp.where(_tile_mask(bk, False), s[r], -1e30) if r in masked else s[r]) for r in A}
            p = {r: jnp.exp(s[r] - lses[r]) for r in A}
            dp = {r: _dot(dos[r], vs, NT) for r in A}
            ds = {r: (p[r] * (dp[r] - deltas[r])).astype(BF16) for r in A}
            dv_c = _total(_dot(p[r].astype(BF16), dos[r], TN) for r in A)
            dk_c = _total(_dot(ds[r], qs[r], TN) for r in A)
            dk_ref[pl.ds(off, bk), :] += dk_c
            dv_ref[pl.ds(off, bk), :] += dv_c
            new = list(dqs)
            for r in A:
                new[r] = dqs[r] + _dot(ds[r], ks, NN)
            return tuple(new)

        dqs = _walk(nch, i, step, tuple(jnp.zeros((bk, LANES), F32) for _ in rows), False)
        for r in range(nch):
            dq_ref[rows[r], :] = dqs[r] * scale

    return pl.pallas_call(
        body,
        name=name,
        grid=(heads, S // bq),
        in_specs=[q_spec(q_off), kv_spec(k_off), kv_spec(v_off), q_spec(),
                  pl.BlockSpec((None, bq, 1), lambda h, i: (h, i, 0)), q_spec()],
        out_specs=[q_spec(), kv_spec(), kv_spec()],
        out_shape=[jax.ShapeDtypeStruct((S, heads * LANES), F32)] * 3,
        compiler_params=_params(("parallel", "arbitrary")),
    )(q, k, v, o, lse, do)


def _tri(n, inclusive):
    r = lax.broadcasted_iota(jnp.int32, (n, n), 0)
    c = lax.broadcasted_iota(jnp.int32, (n, n), 1)
    return jnp.where((r >= c) if inclusive else (r > c), 1.0, 0.0).astype(BF16)


def _suffix_sum(x, tri):
    hi = x.astype(BF16)
    lo = (x - hi.astype(F32)).astype(BF16)
    return _dot(hi, tri, NN) + _dot(lo, tri, NN)


def _sb_logs(z):
    lg = jnp.log(1.0 + jnp.exp(-jnp.abs(z)))
    l1m = -(jnp.maximum(z, 0.0) + lg)
    return l1m, l1m + z


SB_SCALE = SB_HEAD_DIM ** -0.5
assert SB_SCALE == 0.125
SB_DEAD = -110.0


def _sb_alive(cs):
    top = cs[0][0]
    for c in cs[1:]:
        top = jnp.maximum(top, c[0])
    return jnp.max(top) > SB_DEAD


def _sb_attn_fwd(qkv, heads, name, q_off, k_off, v_off):
    S = qkv.shape[0]
    nch, bk = SB_FWD_CFG
    bq, rows, q_spec, kv_spec = _att_layout(S, SB_FWD_CFG)

    def body(q_ref, k_ref, v_ref, o_ref):
        i = pl.program_id(1)
        qs = [q_ref[rw, :] * SB_SCALE for rw in rows]
        tri = _tri(bk, False)

        def step(kb, cs, active, masked):
            off = pl.multiple_of(kb * bk, bk)
            ks, vs = k_ref[pl.ds(off, bk), :], v_ref[pl.ds(off, bk), :]
            A = list(active)
            lg = {r: _sb_logs(_dot(qs[r], ks, NT)) for r in A}
            l1m = {r: (jnp.where(_tile_mask(bk, True), lg[r][0], 0.0) if r in masked else lg[r][0]) for r in A}
            suf = {r: _suffix_sum(l1m[r], tri) for r in A}
            ex = {r: lg[r][1] + cs[r][0] + suf[r] for r in A}
            ex = {r: (jnp.where(_tile_mask(bk, True), ex[r], -1e30) if r in masked else ex[r]) for r in A}
            ab = {r: jnp.exp(ex[r]).astype(BF16) for r in A}
            new = list(cs)
            for r in A:
                new[r] = (cs[r][0] + jnp.sum(l1m[r], axis=1, keepdims=True), cs[r][1] + _dot(ab[r], vs, NN))
            return tuple(new)

        init = (jnp.zeros((bk, 1), F32), jnp.zeros((bk, LANES), F32))
        cs = _walk(nch, i, step, tuple(init for _ in rows), True, _sb_alive)
        for r in range(nch):
            o_ref[rows[r], :] = cs[r][1]

    return pl.pallas_call(
        body,
        name=name,
        grid=(heads, S // bq),
        in_specs=[q_spec(q_off), kv_spec(k_off), kv_spec(v_off)],
        out_specs=q_spec(),
        out_shape=jax.ShapeDtypeStruct((S, heads * LANES), F32),
        compiler_params=_params(("parallel", "arbitrary")),
    )(qkv, qkv, qkv)


def _sb_attn_bwd(qkv, o, do, heads, name, q_off, k_off, v_off):
    S = qkv.shape[0]
    nch, bk = SB_BWD_CFG
    bq, rows, q_spec, kv_spec = _att_layout(S, SB_BWD_CFG)

    def body(q_ref, k_ref, v_ref, o_ref, do_ref, dq_ref, dk_ref, dv_ref):
        i = pl.program_id(1)

        @pl.when(i == 0)
        def _():
            dk_ref[...] = jnp.zeros_like(dk_ref)
            dv_ref[...] = jnp.zeros_like(dv_ref)

        tri = _tri(bk, False)
        qs = [q_ref[rw, :] * SB_SCALE for rw in rows]
        dos = [do_ref[rw, :] for rw in rows]
        deltas = [jnp.sum(dos[r].astype(F32) * o_ref[rows[r], :], axis=1, keepdims=True) for r in range(nch)]

        def step(kb, cs, active, masked):
            off = pl.multiple_of(kb * bk, bk)
            ks, vs = k_ref[pl.ds(off, bk), :], v_ref[pl.ds(off, bk), :]
            A = list(active)
            lg = {r: _sb_logs(_dot(qs[r], ks, NT)) for r in A}
            l1m = {r: (jnp.where(_tile_mask(bk, True), lg[r][0], 0.0) if r in masked else lg[r][0]) for r in A}
            suf = {r: _suffix_sum(l1m[r], tri) for r in A}
            ex = {r: lg[r][1] + cs[r][0] + suf[r] for r in A}
            ex = {r: (jnp.where(_tile_mask(bk, True), ex[r], -1e30) if r in masked else ex[r]) for r in A}
            ab = {r: jnp.exp(ex[r]).astype(BF16) for r in A}
            da = {r: _dot(dos[r], vs, NT) for r in A}
            g = {r: ab[r].astype(F32) * da[r] for r in A}
            gs = {r: _suffix_sum(g[r], tri) for r in A}
            beta = {r: jnp.exp(lg[r][1]) for r in A}
            dz = {r: g[r] - beta[r] * (deltas[r] - cs[r][1] - gs[r]) for r in A}
            dz = {r: (jnp.where(_tile_mask(bk, True), dz[r], 0.0) if r in masked else dz[r]) for r in A}
            dzb = {r: dz[r].astype(BF16) for r in A}
            dv_c = _total(_dot(ab[r], dos[r], TN) for r in A)
            dk_c = _total(_dot(dzb[r], qs[r], TN) for r in A)
            dk_ref[pl.ds(off, bk), :] += dk_c
            dv_ref[pl.ds(off, bk), :] += dv_c
            new = list(cs)
            for r in A:
                new[r] = (cs[r][0] + jnp.sum(l1m[r], axis=1, keepdims=True),
                          cs[r][1] + jnp.sum(g[r], axis=1, keepdims=True), cs[r][2] + _dot(dzb[r], ks, NN))
            return tuple(new)

        zcol = jnp.zeros((bk, 1), F32)
        init = (zcol, zcol, jnp.zeros((bk, LANES), F32))
        cs = _walk(nch, i, step, tuple(init for _ in rows), True, _sb_alive)
        for r in range(nch):
            dq_ref[rows[r], :] = cs[r][2] * SB_SCALE

    return pl.pallas_call(
        body,
        name=name,
        grid=(heads, S // bq),
        in_specs=[q_spec(q_off), kv_spec(k_off), kv_spec(v_off), q_spec(), q_spec()],
        out_specs=[q_spec(), kv_spec(), kv_spec()],
        out_shape=[jax.ShapeDtypeStruct((S, heads * LANES), F32)] * 3,
        compiler_params=_params(("parallel", "arbitrary")),
    )(qkv, qkv, qkv, o, do)


SWA_BLK = 128
SWA_GROUP = SWA_HEADS // SWA_KV_HEADS


SWA_NB = 4
SWA_ROWS = SWA_NB * SWA_BLK


def _swa_band_mask(first):
    row = lax.broadcasted_iota(jnp.int32, (SWA_BLK, 2 * SWA_BLK), 0)
    col = lax.broadcasted_iota(jnp.int32, (SWA_BLK, 2 * SWA_BLK), 1)
    return (col > row) & (col <= row + SWA_WINDOW) & (jnp.logical_not(first) | (col >= SWA_BLK))


def _swa_in_specs(v_off):
    gw = SWA_GROUP * LANES
    before = lambda h, n: (jnp.maximum(SWA_NB * n - 1, 0), h)
    return [
        pl.BlockSpec((SWA_ROWS, gw), lambda h, n: (n, h)),
        pl.BlockSpec((SWA_BLK, LANES), before),
        pl.BlockSpec((SWA_ROWS, LANES), lambda h, n: (n, h)),
        pl.BlockSpec((SWA_BLK, LANES), lambda h, n: (jnp.maximum(SWA_NB * n - 1, 0), v_off + h)),
        pl.BlockSpec((SWA_ROWS, LANES), lambda h, n: (n, v_off + h)),
        pl.BlockSpec((1, gw), lambda h, n: (0, h)),
    ]


def _swa_bands(n, kp_ref, kc_ref, vp_ref, vc_ref):
    k_all = jnp.concatenate([kp_ref[...], kc_ref[...]], axis=0)
    v_all = jnp.concatenate([vp_ref[...], vc_ref[...]], axis=0)
    bands = []
    for j in range(SWA_NB):
        rows = slice(j * SWA_BLK, (j + 2) * SWA_BLK)
        bands.append((k_all[rows], v_all[rows], _swa_band_mask((n == 0) if j == 0 else False)))
    return bands


def _swa_fwd(q, k, v, v_off, sink_b, name):
    S = q.shape[0]
    assert S % SWA_ROWS == 0
    scale = SWA_HEAD_DIM ** -0.5
    gw = SWA_GROUP * LANES

    def body(q_ref, kp_ref, kc_ref, vp_ref, vc_ref, sink_ref, o_ref, lse_ref):
        n = pl.program_id(1)
        bands = _swa_bands(n, kp_ref, kc_ref, vp_ref, vc_ref)
        P = [(j, g) for j in range(SWA_NB) for g in range(SWA_GROUP)]
        rows = lambda j: slice(j * SWA_BLK, (j + 1) * SWA_BLK)
        lanes = lambda g: slice(g * LANES, (g + 1) * LANES)
        sk = {g: sink_ref[:, g * LANES:g * LANES + 1] for g in range(SWA_GROUP)}
        s = {(j, g): jnp.where(bands[j][2], _dot(q_ref[rows(j), lanes(g)], bands[j][0], NT) * scale, -1e30) for j, g in P}
        m = {(j, g): jnp.maximum(jnp.max(s[j, g], axis=1, keepdims=True), sk[g]) for j, g in P}
        p = {(j, g): jnp.exp(s[j, g] - m[j, g]) for j, g in P}
        den = {(j, g): jnp.sum(p[j, g], axis=1, keepdims=True) + jnp.exp(sk[g] - m[j, g]) for j, g in P}
        for j, g in P:
            o_ref[rows(j), lanes(g)] = _dot((p[j, g] / den[j, g]).astype(BF16), bands[j][1], NN).astype(o_ref.dtype)
            lse_ref[g, rows(j), :] = m[j, g] + jnp.log(den[j, g])

    return pl.pallas_call(
        body,
        name=name,
        grid=(SWA_KV_HEADS, S // SWA_ROWS),
        in_specs=_swa_in_specs(v_off),
        out_specs=[
            pl.BlockSpec((SWA_ROWS, gw), lambda h, n: (n, h)),
            pl.BlockSpec((SWA_GROUP, SWA_ROWS, 1), lambda h, n: (h, n, 0)),
        ],
        out_shape=[jax.ShapeDtypeStruct((S, SWA_HEADS * LANES), BF16), jax.ShapeDtypeStruct((SWA_HEADS, S, 1), F32)],
        compiler_params=_params(("parallel", "arbitrary")),
    )(q, k, k, v, v, sink_b)


def _swa_bwd(q, k, v, v_off, sink_b, o, lse, do, name):
    S = q.shape[0]
    assert S % SWA_ROWS == 0
    scale = SWA_HEAD_DIM ** -0.5
    gw = SWA_GROUP * LANES

    def body(q_ref, kp_ref, kc_ref, vp_ref, vc_ref, sink_ref, o_ref, lse_ref, do_ref, dq_ref, dk_ref, dv_ref, dsink_ref):
        n = pl.program_id(1)

        @pl.when(n == 0)
        def _():
            dk_ref[...] = jnp.zeros_like(dk_ref)
            dv_ref[...] = jnp.zeros_like(dv_ref)
            dsink_ref[...] = jnp.zeros_like(dsink_ref)

        bands = _swa_bands(n, kp_ref, kc_ref, vp_ref, vc_ref)
        P = [(j, g) for j in range(SWA_NB) for g in range(SWA_GROUP)]
        rows = lambda j: slice(j * SWA_BLK, (j + 1) * SWA_BLK)
        lanes = lambda g: slice(g * LANES, (g + 1) * LANES)
        qs = {(j, g): q_ref[rows(j), lanes(g)] for j, g in P}
        dos = {(j, g): do_ref[rows(j), lanes(g)] for j, g in P}
        lses = {(j, g): lse_ref[g, rows(j), :] for j, g in P}
        delta = {(j, g): jnp.sum(dos[j, g].astype(F32) * o_ref[rows(j), lanes(g)].astype(F32), axis=1, keepdims=True)
                 for j, g in P}
        s = {(j, g): jnp.where(bands[j][2], _dot(qs[j, g], bands[j][0], NT) * scale, -1e30) for j, g in P}
        p = {(j, g): jnp.exp(s[j, g] - lses[j, g]) for j, g in P}
        dp = {(j, g): _dot(dos[j, g], bands[j][1], NT) for j, g in P}
        ds = {(j, g): (p[j, g] * (dp[j, g] - delta[j, g]) * scale).astype(BF16) for j, g in P}
        for j, g in P:
            dq_ref[rows(j), lanes(g)] = _dot(ds[j, g], bands[j][0], NN)
        for g in range(SWA_GROUP):
            p_sink = [jnp.exp(sink_ref[:, g * LANES:g * LANES + 1] - lses[j, g]) * delta[j, g] for j in range(SWA_NB)]
            dsink_ref[:, lanes(g)] += jnp.zeros((1, LANES), F32) - jnp.sum(_total(p_sink), axis=0, keepdims=True)
        dkb = [_total(_dot(ds[j, g], qs[j, g], TN) for g in range(SWA_GROUP)) for j in range(SWA_NB)]
        dvb = [_total(_dot(p[j, g].astype(BF16), dos[j, g], TN) for g in range(SWA_GROUP)) for j in range(SWA_NB)]
        base = pl.multiple_of(n * SWA_ROWS, SWA_ROWS)
        for j in range(SWA_NB):
            own = pl.ds(base + j * SWA_BLK, SWA_BLK)
            after = j + 1 < SWA_NB
            dk_ref[own, :] += dkb[j][SWA_BLK:] + dkb[j + 1][:SWA_BLK] if after else dkb[j][SWA_BLK:]
            dv_ref[own, :] += dvb[j][SWA_BLK:] + dvb[j + 1][:SWA_BLK] if after else dvb[j][SWA_BLK:]

        @pl.when(n > 0)
        def _():
            before = pl.ds(pl.multiple_of(n * SWA_ROWS - SWA_BLK, SWA_BLK), SWA_BLK)
            dk_ref[before, :] += dkb[0][:SWA_BLK]
            dv_ref[before, :] += dvb[0][:SWA_BLK]

    return pl.pallas_call(
        body,
        name=name,
        grid=(SWA_KV_HEADS, S // SWA_ROWS),
        in_specs=_swa_in_specs(v_off) + [
            pl.BlockSpec((SWA_ROWS, gw), lambda h, n: (n, h)),
            pl.BlockSpec((SWA_GROUP, SWA_ROWS, 1), lambda h, n: (h, n, 0)),
            pl.BlockSpec((SWA_ROWS, gw), lambda h, n: (n, h)),
        ],
        out_specs=[
            pl.BlockSpec((SWA_ROWS, gw), lambda h, n: (n, h)),
            pl.BlockSpec((S, LANES), lambda h, n: (0, h)),
            pl.BlockSpec((S, LANES), lambda h, n: (0, h)),
            pl.BlockSpec((1, gw), lambda h, n: (0, h)),
        ],
        out_shape=[
            jax.ShapeDtypeStruct((S, SWA_HEADS * LANES), F32),
            jax.ShapeDtypeStruct((S, SWA_KV_HEADS * LANES), F32),
            jax.ShapeDtypeStruct((S, SWA_KV_HEADS * LANES), F32),
            jax.ShapeDtypeStruct((1, SWA_HEADS * LANES), F32),
        ],
        compiler_params=_params(("parallel", "arbitrary")),
    )(q, k, k, v, v, sink_b, o, lse, do)


def _pad_last(t, width):
    return jnp.pad(t, [(0, 0)] * (t.ndim - 1) + [(0, width - t.shape[-1])])


def _pad_cols(w, heads, real):
    lead = w.shape[:-1]
    return _pad_last(w.reshape(*lead, heads, real), LANES).reshape(*lead, heads * LANES)


def _unpad_cols(g, heads, real):
    lead = g.shape[:-1]
    return g.reshape(*lead, heads, LANES)[..., :real].reshape(*lead, heads * real)


def _pad_rows(w, heads, real):
    lead, n = w.shape[:-2], w.shape[-1]
    w = w.reshape(*lead, heads, real, n)
    return jnp.pad(w, [(0, 0)] * (w.ndim - 2) + [(0, LANES - real), (0, 0)]).reshape(*lead, heads * LANES, n)


def _unpad_rows(g, heads, real):
    lead, n = g.shape[:-2], g.shape[-1]
    return g.reshape(*lead, heads, LANES, n)[..., :real, :].reshape(*lead, heads * real, n)


def _w_in_internal(w_in):
    c_q, c_kv, k_r, q_swa, k_swa, v_swa, q_sb, k_sb, v_sb, gate = jnp.split(w_in, SPLIT_POINTS, axis=-1)
    k_r = jnp.pad(k_r, [(0, 0)] * (k_r.ndim - 1) + [(MLA_NOPE, LANES - MLA_NOPE - MLA_ROPE)])
    w1 = jnp.concatenate([c_q, c_kv, k_r, _pad_cols(q_swa, 8, 64), _pad_cols(k_swa, 2, 64)], axis=-1)
    w2 = [_pad_cols(v_swa, 2, 64), _pad_cols(q_sb, 8, 64), _pad_cols(k_sb, 8, 64), _pad_cols(v_sb, 8, 64)]
    return w1, w2, gate


def _w_in_reference(g1, g2, g3):
    c_q, c_kv, k_r, q_swa, k_swa = jnp.split(g1, [256, 384, 512, 1536], axis=-1)
    v_swa, q_sb, k_sb, v_sb = g2
    return jnp.concatenate([
        c_q, c_kv, k_r[..., MLA_NOPE:MLA_NOPE + MLA_ROPE], _unpad_cols(q_swa, 8, 64), _unpad_cols(k_swa, 2, 64),
        _unpad_cols(v_swa, 2, 64), _unpad_cols(q_sb, 8, 64), _unpad_cols(k_sb, 8, 64), _unpad_cols(v_sb, 8, 64),
        g3], axis=-1)


def _w_ukv_internal(w):
    lead = w.shape[:-1]
    w3 = w.reshape(*lead, MLA_HEADS, MLA_NOPE + MLA_V)
    pad = lambda t: _pad_last(t, LANES).reshape(*lead, MLA_HEADS * LANES)
    return pad(w3[..., :MLA_NOPE]), pad(w3[..., MLA_NOPE:])


def _w_ukv_reference(gk, gv):
    lead = gk.shape[:-1]
    gk = gk.reshape(*lead, MLA_HEADS, LANES)[..., :MLA_NOPE]
    gv = gv.reshape(*lead, MLA_HEADS, LANES)[..., :MLA_V]
    return jnp.concatenate([gk, gv], axis=-1).reshape(*lead, MLA_HEADS * (MLA_NOPE + MLA_V))


def _layer_fwd(x, w, tabs):
    mla_tab, swa_tab = tabs
    sv = {"x": x}

    def f_norm(rows, consts):
        return [_rms(rows[0], consts[0])], []

    (h,) = _rowwise(f_norm, [x], [w["g_mix_pre"]], [(D_MODEL, BF16)], [], "norm_mix_pre")
    p1 = _matmul(h, w["w_in1"], "nn", [F32], "proj_lat")
    p2 = _matmul(h, w["w_in2"], "nn", [BF16], "proj_qkv")
    gates = _matmul(h, w["w_in3"], "nn", [BF16], "proj_gate",
                    epilogue=lambda acc, b: (1.0 / (1.0 + jnp.exp(-(acc + b))),), row_extras=[w["b_gate"]])

    def f_prep(rows, consts):
        t = rows[0]
        gq, gkv = consts[0], consts[1]
        mc, mu, md = rows[1], rows[2], rows[3]
        sc, su, sd = rows[4], rows[5], rows[6]
        cq_n = _rms(t[:, 0:256], gq)
        ckv_n = _rms(t[:, 256:384], gkv)
        kr = _rope(t[:, 384:512], mc, mu, md, MLA_ROPE // 2)
        qs = [_rope(t[:, 512 + j * LANES:512 + (j + 1) * LANES], sc, su, sd, SWA_HEAD_DIM // 2) for j in range(8)]
        ks = [_rope(t[:, 1536 + j * LANES:1536 + (j + 1) * LANES], sc, su, sd, SWA_HEAD_DIM // 2) for j in range(2)]
        return [cq_n, ckv_n, kr, jnp.concatenate(qs, axis=1), jnp.concatenate(ks, axis=1)], []

    cq_n, ckv_n, kr, q_swa, k_swa = _rowwise(
        f_prep, [p1, *mla_tab["k"], *swa_tab["f"]], [w["g_q_lat"], w["g_kv_lat"]],
        [(256, BF16), (128, BF16), (LANES, F32), (1024, BF16), (256, BF16)], [], "lat_prep")

    q_lat = _matmul(cq_n, w["w_uq"], "nn", [F32], "mla_q_up")
    k_lat = _matmul(ckv_n, w["w_ukv_k"], "nn", [F32], "mla_k_up")
    def ones_lane(acc):
        lane = lax.broadcasted_iota(jnp.int32, acc.shape, 1) % LANES
        return (jnp.where(lane == ONES_LANE, 1.0, acc),)

    v_mla = _matmul(ckv_n, w["w_ukv_v"], "nn", [BF16], "mla_v_up", epilogue=ones_lane)
    mla_scale = (MLA_NOPE + MLA_ROPE) ** -0.5

    def f_mla_prep(rows, consts):
        ql, kl, krr, mc, mu, md = rows
        qs = [_rope(ql[:, j * LANES:(j + 1) * LANES], mc, mu, md, MLA_ROPE // 2) * mla_scale for j in range(8)]
        ks = [kl[:, j * LANES:(j + 1) * LANES] + krr for j in range(8)]
        return [jnp.concatenate(qs, axis=1), jnp.concatenate(ks, axis=1)], []

    q_mla, k_mla = _rowwise(f_mla_prep, [q_lat, k_lat, kr, *mla_tab["q"]], [], [(1024, BF16), (1024, BF16)], [], "mla_prep")

    o_mla, lse_mla = _softmax_attn_fwd(q_mla, k_mla, v_mla, MLA_HEADS, "mla_fwd")
    o_swa, lse_swa = _swa_fwd(q_swa, k_swa, p2, 0, w["sink_b"], "swa_fwd")
    o_sb = _sb_attn_fwd(p2, SB_HEADS, "sb_fwd", 2, 10, 18)

    oa = _matmul(o_mla, w["w_o_mla"], "nn", [F32], "o_proj_mla")
    ob = _matmul(o_swa, w["w_o_swa"], "nn", [F32], "o_proj_swa")
    oc = _matmul(o_sb, w["w_o_sb"], "nn", [F32], "o_proj_sb")

    def f_mix(rows, consts):
        a, b, c, g = rows
        g = g.astype(F32)
        return [g[:, 0:1024] * a + g[:, 1024:2048] * b + g[:, 2048:3072] * c], []

    (mixed,) = _rowwise(f_mix, [oa, ob, oc, gates], [], [(D_MODEL, BF16)], [], "gate_mix")
    y = _matmul(mixed, w["w_out"], "nn", [F32], "out_proj")

    def f_res_norm(rows, consts):
        return [rows[0] + _rms(rows[1], consts[0])], []

    (x1,) = _rowwise(f_res_norm, [x, y], [w["g_mix_post"]], [(D_MODEL, F32)], [], "res_norm_mix")
    (h2,) = _rowwise(f_norm, [x1], [w["g_mlp_pre"]], [(D_MODEL, BF16)], [], "norm_mlp_pre")

    def relu2(acc):
        r = jnp.maximum(acc, 0.0)
        return acc, r * r

    up, u = _matmul(h2, w["w_up"], "nn", [BF16, BF16], "mlp_up", epilogue=relu2)
    zd = _matmul(u, w["w_down"], "nn", [F32], "mlp_down")
    (x2,) = _rowwise(f_res_norm, [x1, zd], [w["g_mlp_post"]], [(D_MODEL, F32)], [], "res_norm_mlp")

    sv.update(h=h, p1=p1, p2=p2, gates=gates, cq_n=cq_n, ckv_n=ckv_n, q_swa=q_swa, k_swa=k_swa, q_mla=q_mla,
              k_mla=k_mla, v_mla=v_mla, o_mla=o_mla, lse_mla=lse_mla, o_swa=o_swa, lse_swa=lse_swa, o_sb=o_sb,
              oa=oa, ob=ob, oc=oc, mixed=mixed, y=y, x1=x1, h2=h2, up=up, u=u, zd=zd)
    return x2, sv


def _layer_bwd(dx2, w, sv, tabs):
    mla_tab, swa_tab = tabs
    gr = {}

    def f_norm_bwd(rows, consts):
        dx, dg = _rms_bwd(rows[0], consts[0], rows[1])
        return [dx], [dg]

    def f_norm_bwd_res(rows, consts):
        dx, dg = _rms_bwd(rows[0], consts[0], rows[1])
        return [rows[2] + dx], [dg]

    dzd, gr["g_mlp_post"] = _rowwise(f_norm_bwd, [sv["zd"], dx2], [w["g_mlp_post"]], [(D_MODEL, BF16)], [D_MODEL], "b_norm_mlp_post")
    gr["w_down"] = _matmul(sv["u"], dzd, "tn", [BF16], "b_w_down")
    dup = _matmul(dzd, w["w_down"], "nt", [BF16], "b_mlp_down",
                  epilogue=lambda acc, up: (acc * 2.0 * jnp.maximum(up.astype(F32), 0.0),), extras=[sv["up"]])
    gr["w_up"] = _matmul(sv["h2"], dup, "tn", [BF16], "b_w_up")
    dh2 = _matmul(dup, w["w_up"], "nt", [F32], "b_mlp_up")
    dx1, gr["g_mlp_pre"] = _rowwise(f_norm_bwd_res, [sv["x1"], dh2, dx2], [w["g_mlp_pre"]], [(D_MODEL, F32)], [D_MODEL], "b_norm_mlp_pre")

    dy, gr["g_mix_post"] = _rowwise(f_norm_bwd, [sv["y"], dx1], [w["g_mix_post"]], [(D_MODEL, BF16)], [D_MODEL], "b_norm_mix_post")
    gr["w_out"] = _matmul(sv["mixed"], dy, "tn", [BF16], "b_w_out")
    dmixed = _matmul(dy, w["w_out"], "nt", [F32], "b_out_proj")

    def f_mix_bwd(rows, consts):
        dm, a, b, c, g = rows
        g = g.astype(F32)
        outs, dls = [], []
        for j, o in enumerate((a, b, c)):
            gj = g[:, j * D_MODEL:(j + 1) * D_MODEL]
            outs.append(dm * gj)
            dls.append(dm * o * gj * (1.0 - gj))
        dl = jnp.concatenate(dls, axis=1)
        return outs + [dl], [dl]

    doa, dob, doc, dlogit, gr["b_gate"] = _rowwise(
        f_mix_bwd, [dmixed, sv["oa"], sv["ob"], sv["oc"], sv["gates"]], [],
        [(D_MODEL, BF16)] * 3 + [(P3_W, BF16)], [P3_W], "b_gate_mix")

    gr["w_o_mla"] = _matmul(sv["o_mla"], doa, "tn", [BF16], "b_w_o_mla")
    gr["w_o_swa"] = _matmul(sv["o_swa"], dob, "tn", [BF16], "b_w_o_swa")
    gr["w_o_sb"] = _matmul(sv["o_sb"], doc, "tn", [BF16], "b_w_o_sb")
    do_mla = _matmul(doa, w["w_o_mla"], "nt", [BF16], "b_o_proj_mla")
    do_swa = _matmul(dob, w["w_o_swa"], "nt", [BF16], "b_o_proj_swa")
    do_sb = _matmul(doc, w["w_o_sb"], "nt", [BF16], "b_o_proj_sb")

    dq_sb, dk_sb, dv_sb = _sb_attn_bwd(sv["p2"], sv["o_sb"], do_sb, SB_HEADS, "sb_bwd", 2, 10, 18)
    dq_swa, dk_swa, dv_swa, dsink = _swa_bwd(sv["q_swa"], sv["k_swa"], sv["p2"], 0, w["sink_b"], sv["o_swa"],
                                             sv["lse_swa"], do_swa, "swa_bwd")
    gr["swa_sinks"] = dsink.reshape(SWA_HEADS, LANES)[:, 0]
    dq_mla, dk_mla, dv_mla = _softmax_attn_bwd(sv["q_mla"], sv["k_mla"], sv["v_mla"], sv["o_mla"], sv["lse_mla"], do_mla,
                                               MLA_HEADS, (MLA_NOPE + MLA_ROPE) ** -0.5, "mla_bwd")

    def f_mla_post(rows, consts):
        dq, dk, qc, qu, qd, kc, ku, kd = rows
        dqs = [_rope(dq[:, j * LANES:(j + 1) * LANES], qc, qu, qd, MLA_ROPE // 2) for j in range(8)]
        dkr = dk[:, 0:LANES]
        for j in range(1, 8):
            dkr = dkr + dk[:, j * LANES:(j + 1) * LANES]
        return [jnp.concatenate(dqs, axis=1), _rope(dkr, kc, ku, kd, MLA_ROPE // 2)], []

    dq_lat, dkr = _rowwise(f_mla_post, [dq_mla, dk_mla, *mla_tab["q_inv"], *mla_tab["k_inv"]], [],
                           [(1024, BF16), (LANES, F32)], [], "b_mla_post")
    gr["w_uq"] = _matmul(sv["cq_n"], dq_lat, "tn", [BF16], "b_w_uq")
    gr["w_ukv_k"] = _matmul(sv["ckv_n"], dk_mla, "tn", [BF16], "b_w_ukv_k")
    gr["w_ukv_v"] = _matmul(sv["ckv_n"], dv_mla, "tn", [BF16], "b_w_ukv_v")
    dcq_n = _matmul(dq_lat, w["w_uq"], "nt", [F32], "b_mla_q_up")
    dckv_a = _matmul(dk_mla, w["w_ukv_k"], "nt", [F32], "b_mla_k_up")
    dckv_b = _matmul(dv_mla, w["w_ukv_v"], "nt", [F32], "b_mla_v_up")

    def f_prep_bwd(rows, consts):
        t, dcq, dca, dcb, dkr_, dqs, dks, sc, su, sd = rows
        gq, gkv = consts
        dc_q, dgq = _rms_bwd(t[:, 0:256], gq, dcq)
        dc_kv, dgkv = _rms_bwd(t[:, 256:384], gkv, dca + dcb)
        q_parts = [_rope(dqs[:, j * LANES:(j + 1) * LANES], sc, su, sd, SWA_HEAD_DIM // 2) for j in range(8)]
        k_parts = [_rope(dks[:, j * LANES:(j + 1) * LANES], sc, su, sd, SWA_HEAD_DIM // 2) for j in range(2)]
        return [jnp.concatenate([dc_q, dc_kv, dkr_] + q_parts + k_parts, axis=1)], [dgq, dgkv]

    dp1, gr["g_q_lat"], gr["g_kv_lat"] = _rowwise(
        f_prep_bwd, [sv["p1"], dcq_n, dckv_a, dckv_b, dkr, dq_swa, dk_swa, *swa_tab["inv"]], [w["g_q_lat"], w["g_kv_lat"]],
        [(P1_W, BF16)], [256, 128], "b_lat_prep")

    gr["w_in1"] = _matmul(sv["h"], dp1, "tn", [BF16], "b_w_in_lat")
    dh = _matmul(dp1, w["w_in1"], "nt", [F32], "b_proj_lat")
    gr["w_in2"] = []
    add_prev = lambda acc, prev: (acc + prev,)
    for piece, wp, tag in zip((dv_swa, dq_sb, dk_sb, dv_sb), w["w_in2_parts"], ("vswa", "qsb", "ksb", "vsb")):
        gr["w_in2"].append(_matmul(sv["h"], piece, "tn", [BF16], "b_w_in_" + tag))
        dh = _matmul(piece, wp, "nt", [F32], "b_proj_" + tag, epilogue=add_prev, extras=[dh])
    gr["w_in3"] = _matmul(sv["h"], dlogit, "tn", [BF16], "b_w_in_gate")
    dh = _matmul(dlogit, w["w_in3"], "nt", [F32], "b_proj_gate", epilogue=add_prev, extras=[dh])
    dx, gr["g_mix_pre"] = _rowwise(f_norm_bwd_res, [sv["x"], dh, dx1], [w["g_mix_pre"]], [(D_MODEL, F32)], [D_MODEL], "b_norm_mix_pre")
    return dx, gr


def _local_step(x, positions, loss_target, full):
    mc, mu, md = _rope_tables(positions, MLA_NOPE, MLA_ROPE, True)
    kc, ku, kd = _rope_tables(positions, MLA_NOPE, MLA_ROPE, False)
    sc, su, sd = _rope_tables(positions, 0, SWA_HEAD_DIM, False)
    mla_tab = {"q": (mc, mu, md), "k": (kc, ku, kd), "q_inv": (mc, -mu, -md), "k_inv": (kc, -ku, -kd)}
    swa_tab = {"f": (sc, su, sd), "inv": (sc, -su, -sd)}
    tabs = (mla_tab, swa_tab)

    big = {n: full[n].astype(BF16) for n in SHARDED}
    w1, w2, w3 = _w_in_internal(big["w_in"])
    uk, uv = _w_ukv_internal(big["w_ukv"])
    stacks = {
        "w_in1": w1, "w_in2": jnp.concatenate(w2, axis=-1), "w_in3": w3,
        "w_uq": _pad_cols(big["w_uq"], MLA_HEADS, MLA_NOPE + MLA_ROPE), "w_ukv_k": uk, "w_ukv_v": uv,
        "w_o_mla": _pad_rows(big["w_o_mla"], 8, 64), "w_o_swa": _pad_rows(big["w_o_swa"], 8, 64),
        "w_o_sb": _pad_rows(big["w_o_sb"], 8, 64), "w_out": big["w_out"], "w_up": big["w_up"], "w_down": big["w_down"],
    }
    layers = []
    for l in range(DEPTH):
        layers.append({
            **{n: (t, l) for n, t in stacks.items()}, "w_in2_parts": [(t, l) for t in w2],
            "g_mix_pre": full["g_mix_pre"][l][None], "b_gate": full["b_gate"][l][None],
            "g_q_lat": full["g_q_lat"][l][None], "g_kv_lat": full["g_kv_lat"][l][None],
            "g_mix_post": full["g_mix_post"][l][None], "g_mlp_pre": full["g_mlp_pre"][l][None],
            "g_mlp_post": full["g_mlp_post"][l][None],
            "sink_b": jnp.repeat(full["swa_sinks"][l], LANES)[None],
        })

    saved = []
    h = x
    for l in range(DEPTH):
        h, sv = _layer_fwd(h, layers[l], tabs)
        saved.append(sv)

    def f_loss(rows, consts):
        err = rows[0] - rows[1]
        return [err * (1.0 / D_MODEL)], [jnp.sum(err * err, axis=1, keepdims=True)]

    dy, sq = _rowwise(f_loss, [h, loss_target], [], [(D_MODEL, F32)], [1], "loss_head")
    loss_part = sq * (0.5 / D_MODEL)

    grs = [None] * DEPTH
    d = dy
    for l in reversed(range(DEPTH)):
        d, grs[l] = _layer_bwd(d, layers[l], saved[l], tabs)
    st = lambda pick: jnp.stack([pick(grs[l]) for l in range(DEPTH)])
    vec = lambda n: st(lambda gr: gr[n][0] if gr[n].ndim == 2 else gr[n])
    stacked = {n: vec(n) for n in SMALL}
    stacked.update({n: st(lambda gr: gr[n]) for n in ("w_out", "w_up", "w_down")})
    stacked["w_in"] = _w_in_reference(st(lambda gr: gr["w_in1"]), [st(lambda gr: gr["w_in2"][p]) for p in range(4)],
                                      st(lambda gr: gr["w_in3"]))
    stacked["w_uq"] = _unpad_cols(st(lambda gr: gr["w_uq"]), MLA_HEADS, MLA_NOPE + MLA_ROPE)
    stacked["w_ukv"] = _w_ukv_reference(st(lambda gr: gr["w_ukv_k"]), st(lambda gr: gr["w_ukv_v"]))
    for n in ("w_o_mla", "w_o_swa", "w_o_sb"):
        stacked[n] = _unpad_rows(st(lambda gr: gr[n]), 8, 64)
    return loss_part, d, stacked


def _lane_padded(width):
    return -(-width // LANES) * LANES


def _rows_of(a, dtype):
    extra = _lane_padded(a.shape[-1]) - a.shape[-1]
    if extra:
        a = jnp.pad(a, [(0, 0)] * (a.ndim - 1) + [(0, extra)])
    return a.astype(dtype).reshape(-1, LANES)


def _pack(shards, small, dtype):
    parts = [_rows_of(shards[n], dtype) for n in SHARDED]
    if small is not None:
        parts += [_rows_of(small[n], dtype) for n in SMALL]
    slab = jnp.concatenate(parts, axis=0)
    pad = (-slab.shape[0]) % SLAB_ROW_ALIGN
    return jnp.pad(slab, ((0, pad), (0, 0)))


def _unpack(slab, shard_shapes, small_shapes):
    out, r = {}, 0
    shapes = [(n, shard_shapes[n]) for n in SHARDED]
    if small_shapes is not None:
        shapes += [(n, small_shapes[n]) for n in SMALL]
    for n, shape in shapes:
        wide = shape[:-1] + (_lane_padded(shape[-1]),)
        rows = int(np.prod(wide)) // LANES
        out[n] = slab[r:r + rows].reshape(wide)[..., :shape[-1]]
        r += rows
    return out


def _chip_exchange(src, name):
    rows = src.shape[-2]

    def body(src_ref, out_ref, send_sems, recv_sems):
        x, y, c = lax.axis_index("x"), lax.axis_index("y"), lax.axis_index("c")
        me = 2 * x + y
        chips = [(1 - x, y), (x, 1 - y), (1 - x, 1 - y)]
        sends = []
        for k, (cx, cy) in enumerate(chips):
            cp = pltpu.make_async_remote_copy(
                src_ref=src_ref.at[2 * cx + cy], dst_ref=out_ref.at[me], send_sem=send_sems.at[k],
                recv_sem=recv_sems.at[k], device_id=(cx, cy, c), device_id_type=pl.DeviceIdType.MESH)
            cp.start()
            sends.append(cp)
        for k, (cx, cy) in enumerate(chips):
            pltpu.make_async_remote_copy(
                src_ref=src_ref.at[me], dst_ref=out_ref.at[2 * cx + cy], send_sem=send_sems.at[k],
                recv_sem=recv_sems.at[k], device_id=(cx, cy, c), device_id_type=pl.DeviceIdType.MESH).wait_recv()
        for cp in sends:
            cp.wait_send()

    return pl.pallas_call(
        body,
        name=name,
        in_specs=[pl.BlockSpec(memory_space=pl.ANY)],
        out_specs=pl.BlockSpec(memory_space=pl.ANY),
        out_shape=jax.ShapeDtypeStruct((N_CHIPS, rows, LANES), src.dtype),
        scratch_shapes=[pltpu.SemaphoreType.DMA((3,)), pltpu.SemaphoreType.DMA((3,))],
    )(src)


def _half_rows(c, half):
    return pl.ds(pl.multiple_of(c * half, SLAB_ROW_ALIGN // 2), half)


def _gather_weights(src, name):
    rows = src.shape[0]
    half = rows // 2

    def body(src_ref, out_ref, send_sems, recv_sems):
        x, y, c = lax.axis_index("x"), lax.axis_index("y"), lax.axis_index("c")
        me = 2 * x + y
        chips = [(1 - x, y), (x, 1 - y), (1 - x, 1 - y)]

        def copy(k, src_view, slab, part, to):
            return pltpu.make_async_remote_copy(
                src_ref=src_view, dst_ref=out_ref.at[slab, _half_rows(part, half), :], send_sem=send_sems.at[k],
                recv_sem=recv_sems.at[k], device_id=to, device_id_type=pl.DeviceIdType.MESH)

        sends = [copy(k, src_ref.at[_half_rows(c, half), :], me, c, (cx, cy, c)) for k, (cx, cy) in enumerate(chips)]
        for cp in sends:
            cp.start()
        for k, (cx, cy) in enumerate(chips):
            j = 2 * cx + cy
            landed = out_ref.at[j, _half_rows(c, half), :]
            copy(k, landed, j, c, (cx, cy, c)).wait_recv()
            fwd = copy(3 + k, landed, j, c, (x, y, 1 - c))
            fwd.start()
            sends.append(fwd)
        for k, (cx, cy) in enumerate(chips):
            j = 2 * cx + cy
            copy(3 + k, out_ref.at[j, _half_rows(1 - c, half), :], j, 1 - c, (x, y, 1 - c)).wait_recv()
        for cp in sends:
            cp.wait_send()

    return pl.pallas_call(
        body,
        name=name,
        in_specs=[pl.BlockSpec(memory_space=pl.ANY)],
        out_specs=pl.BlockSpec(memory_space=pl.ANY),
        out_shape=jax.ShapeDtypeStruct((N_CHIPS, rows, LANES), src.dtype),
        scratch_shapes=[pltpu.SemaphoreType.DMA((6,)), pltpu.SemaphoreType.DMA((6,))],
    )(src)


def _sibling_halves(src, name):
    n, rows, _ = src.shape
    half = rows // 2

    def body(src_ref, out_ref, send_sem, recv_sem):
        x, y, c = lax.axis_index("x"), lax.axis_index("y"), lax.axis_index("c")
        cp = pltpu.make_async_remote_copy(
            src_ref=src_ref.at[:, _half_rows(1 - c, half), :], dst_ref=out_ref, send_sem=send_sem, recv_sem=recv_sem,
            device_id=(x, y, 1 - c), device_id_type=pl.DeviceIdType.MESH)
        cp.start()
        cp.wait()

    return pl.pallas_call(
        body,
        name=name,
        in_specs=[pl.BlockSpec(memory_space=pl.ANY)],
        out_specs=pl.BlockSpec(memory_space=pl.ANY),
        out_shape=jax.ShapeDtypeStruct((n, half, LANES), src.dtype),
        scratch_shapes=[pltpu.SemaphoreType.DMA, pltpu.SemaphoreType.DMA],
    )(src)


def _sibling_join(src, name):
    half = src.shape[0]

    def body(src_ref, out_ref, send_sem, recv_sem):
        x, y, c = lax.axis_index("x"), lax.axis_index("y"), lax.axis_index("c")
        cp = pltpu.make_async_remote_copy(
            src_ref=src_ref, dst_ref=out_ref.at[_half_rows(c, half), :], send_sem=send_sem, recv_sem=recv_sem,
            device_id=(x, y, 1 - c), device_id_type=pl.DeviceIdType.MESH)
        cp.start()
        pltpu.make_async_remote_copy(
            src_ref=src_ref, dst_ref=out_ref.at[_half_rows(1 - c, half), :], send_sem=send_sem, recv_sem=recv_sem,
            device_id=(x, y, 1 - c), device_id_type=pl.DeviceIdType.MESH).wait_recv()
        cp.wait_send()

    return pl.pallas_call(
        body,
        name=name,
        in_specs=[pl.BlockSpec(memory_space=pl.ANY)],
        out_specs=pl.BlockSpec(memory_space=pl.ANY),
        out_shape=jax.ShapeDtypeStruct((2 * half, LANES), src.dtype),
        scratch_shapes=[pltpu.SemaphoreType.DMA, pltpu.SemaphoreType.DMA],
    )(src)


SUM_ROWS = 1024


def _pair_sum(mine, theirs, c, name):
    n, half, _ = theirs.shape
    blocks = half // SUM_ROWS

    def body(c_ref, a_ref, b_ref, o_ref):
        o_ref[...] = (a_ref[...].astype(F32) + b_ref[...].astype(F32)).astype(o_ref.dtype)

    return pl.pallas_call(
        body,
        name=name,
        grid_spec=pltpu.PrefetchScalarGridSpec(
            num_scalar_prefetch=1,
            grid=(blocks,),
            in_specs=[pl.BlockSpec((n, SUM_ROWS, LANES), lambda i, c_ref: (0, c_ref[0] * blocks + i, 0)),
                      pl.BlockSpec((n, SUM_ROWS, LANES), lambda i, c_ref: (0, i, 0))],
            out_specs=pl.BlockSpec((n, SUM_ROWS, LANES), lambda i, c_ref: (0, i, 0)),
        ),
        out_shape=jax.ShapeDtypeStruct((n, half, LANES), BF16),
        compiler_params=_params(("arbitrary",)),
    )(jnp.reshape(c, (1,)).astype(jnp.int32), mine, theirs)


def _sum_chips(own, landed, me, name):
    rows = landed.shape[1]

    def body(me_ref, a_ref, b_ref, o_ref):
        t = [jnp.where(me_ref[0] == j, a_ref[j], b_ref[j]).astype(F32) for j in range(N_CHIPS)]
        o_ref[...] = ((t[0] + t[1]) + t[2]) + t[3]

    slabs = pl.BlockSpec((N_CHIPS, SUM_ROWS, LANES), lambda i, me_ref: (0, i, 0))
    return pl.pallas_call(
        body,
        name=name,
        grid_spec=pltpu.PrefetchScalarGridSpec(
            num_scalar_prefetch=1,
            grid=(rows // SUM_ROWS,),
            in_specs=[slabs, slabs],
            out_specs=pl.BlockSpec((SUM_ROWS, LANES), lambda i, me_ref: (i, 0)),
        ),
        out_shape=jax.ShapeDtypeStruct((rows, LANES), F32),
        compiler_params=_params(("arbitrary",)),
    )(jnp.reshape(me, (1,)).astype(jnp.int32), own, landed)


def _adamw(w, m, v, g, name):
    shape = w.shape
    flat = lambda a: a.reshape(-1, shape[-1])

    def fn(rows, consts):
        w_, m_, v_, g_ = rows
        m_new = ADAM_B1 * m_ + (1.0 - ADAM_B1) * g_
        v_new = ADAM_B2 * v_ + (1.0 - ADAM_B2) * (g_ * g_)
        m_hat = m_new / (1.0 - ADAM_B1 ** ADAM_STEP)
        v_hat = v_new / (1.0 - ADAM_B2 ** ADAM_STEP)
        delta = -ADAM_LR * (m_hat / (jnp.sqrt(v_hat) + ADAM_EPS) + ADAM_WD * w_)
        return [delta, m_new, v_new], []

    outs = _rowwise(fn, [flat(w), flat(m), flat(v), flat(g)], [], [(shape[-1], F32)] * 3, [], name)
    return [o.reshape(shape) for o in outs]


def kernel(x, positions, g_mix_pre, w_in, b_gate, g_q_lat, g_kv_lat, w_uq, w_ukv, swa_sinks, w_o_mla, w_o_swa, w_o_sb, w_out, g_mix_post, g_mlp_pre, w_up, w_down, g_mlp_post, loss_target, m_g_mix_pre, m_w_in, m_b_gate, m_g_q_lat, m_g_kv_lat, m_w_uq, m_w_ukv, m_swa_sinks, m_w_o_mla, m_w_o_swa, m_w_o_sb, m_w_out, m_g_mix_post, m_g_mlp_pre, m_w_up, m_w_down, m_g_mlp_post, v_g_mix_pre, v_w_in, v_b_gate, v_g_q_lat, v_g_kv_lat, v_w_uq, v_w_ukv, v_swa_sinks, v_w_o_mla, v_w_o_swa, v_w_o_sb, v_w_out, v_g_mix_post, v_g_mlp_pre, v_w_up, v_w_down, v_g_mlp_post):
    given = dict(locals())
    wts = {n: given[n] for n in WEIGHTS}
    mom_m = {n: given["m_" + n] for n in WEIGHTS}
    mom_v = {n: given["v_" + n] for n in WEIGHTS}
    shard_shapes = {n: wts[n].shape for n in SHARDED}
    small_shapes = {n: wts[n].shape for n in SMALL}

    me = 2 * lax.axis_index("x") + lax.axis_index("y")
    core = lax.axis_index("c")
    gathered = _gather_weights(_pack(wts, None, BF16), "gather_weights")
    full = {n: wts[n] for n in SMALL}
    per_chip = [_unpack(gathered[j], shard_shapes, None) for j in range(N_CHIPS)]
    for n in SHARDED:
        own = wts[n].astype(BF16)
        full[n] = jnp.concatenate([jnp.where(me == j, own, per_chip[j][n]) for j in range(N_CHIPS)], axis=SHARD_AXIS[n])

    loss_part, grad_x, grads = _local_step(x[0], positions[0], loss_target[0], full)
    loss = lax.psum(loss_part[0, 0], ("x", "y", "c"))

    small_g = {n: grads[n] for n in SMALL}
    slabs = []
    for j in range(N_CHIPS):
        shard = {n: jnp.split(grads[n], N_CHIPS, axis=SHARD_AXIS[n])[j] for n in SHARDED}
        slabs.append(_pack(shard, small_g, BF16))
    per_chip_g = jnp.stack(slabs)
    theirs = _sibling_halves(per_chip_g, "pair_grads")
    pair = _pair_sum(per_chip_g, theirs, core, "sum_pair")
    landed = _chip_exchange(pair, "scatter_grads")
    my_half = _sum_chips(pair, landed, me, "sum_chips")
    g_slab = lax.dynamic_update_slice(_sibling_join(my_half, "join_grads"), my_half, (core * my_half.shape[0], 0))

    g = _unpack(g_slab, shard_shapes, small_shapes)
    stepped = {n: _adamw(wts[n], mom_m[n], mom_v[n], g[n], "adamw_" + n) for n in WEIGHTS}
    outs = [loss, grad_x[None]] + [g[n] for n in WEIGHTS]
    for part in range(3):
        outs += [stepped[n][part] for n in WEIGHTS]
    return tuple(outs)
```

```python
import numpy as np
import jax
import jax.numpy as jnp
from jax import lax
from jax.experimental import pallas as pl
from jax.experimental.pallas import tpu as pltpu

F32 = jnp.float32
BF16 = jnp.bfloat16

D_MODEL = 1024
DEPTH = 4
MLA_HEADS, MLA_Q_LORA, MLA_KV_LORA, MLA_NOPE, MLA_ROPE, MLA_V = 8, 256, 128, 64, 32, 64
SWA_HEADS, SWA_KV_HEADS, SWA_HEAD_DIM, SWA_WINDOW = 8, 2, 64, 128
SB_HEADS, SB_HEAD_DIM = 8, 64
D_FF = 4 * D_MODEL
ROPE_THETA = 10000.0
EPS = 1e-6
SPLIT_SIZES = (256, 128, 32, 512, 128, 128, 512, 512, 512, 3 * D_MODEL)
SPLIT_POINTS = [int(v) for v in np.cumsum(SPLIT_SIZES)[:-1]]

ADAM_LR, ADAM_B1, ADAM_B2, ADAM_EPS, ADAM_WD, ADAM_STEP = 0.001, 0.9, 0.999, 1e-08, 0.01, 10

LANES = 128
V7X_VMEM_BYTES = 64 * 1024 * 1024
VMEM_LIMIT = V7X_VMEM_BYTES - 8 * 1024 * 1024
MATMUL_VMEM_BUDGET = 36 * 1024 * 1024
N_CHIPS = 4
SLAB_ROW_ALIGN = 2048

P1_W = 256 + 128 + 128 + 1024 + 256
P2_W = 256 + 1024 + 1024 + 1024
P3_W = 3 * D_MODEL

SHARDED = ("w_in", "w_uq", "w_ukv", "w_o_mla", "w_o_swa", "w_o_sb", "w_out", "w_up", "w_down")
SHARD_AXIS = {"w_in": 2, "w_uq": 2, "w_ukv": 2, "w_o_mla": 2, "w_o_swa": 2, "w_o_sb": 2, "w_out": 1, "w_up": 2, "w_down": 1}
SMALL = ("g_mix_pre", "b_gate", "g_q_lat", "g_kv_lat", "swa_sinks", "g_mix_post", "g_mlp_pre", "g_mlp_post")
WEIGHTS = ("g_mix_pre", "w_in", "b_gate", "g_q_lat", "g_kv_lat", "w_uq", "w_ukv", "swa_sinks", "w_o_mla", "w_o_swa",
           "w_o_sb", "w_out", "g_mix_post", "g_mlp_pre", "w_up", "w_down", "g_mlp_post")

NN = (((1,), (0,)), ((), ()))
NT = (((1,), (1,)), ((), ()))
TN = (((0,), (0,)), ((), ()))


def _dot(a, b, dims):
    return lax.dot_general(a, b, dims, preferred_element_type=F32)


def _params(sem):
    return pltpu.CompilerParams(dimension_semantics=sem, vmem_limit_bytes=VMEM_LIMIT)


def _largest_tile(n, cap):
    if n <= cap:
        return n
    best = LANES
    for t in range(LANES, cap + 1, LANES):
        if n % t == 0:
            best = t
    return best


def _matmul_tiles(M, N, K, a_bytes, b_bytes, out_bytes, extra_bytes):
    tn = _largest_tile(N, 1792)
    tm = _largest_tile(M, 1024 if tn <= 1024 else 512)
    tk = _largest_tile(K, 2048)

    def need(tm_, tk_):
        acc = 4 * tm_ * tn if tk_ < K else 0
        return 2 * (tm_ * tk_ * a_bytes + tk_ * tn * b_bytes + tm_ * tn * (out_bytes + extra_bytes)) + acc

    while need(tm, tk) > MATMUL_VMEM_BUDGET:
        if tk >= tm and tk % 256 == 0:
            tk //= 2
        elif tm % 256 == 0:
            tm //= 2
        else:
            break
    return tm, tn, tk


def _matmul(a, b, mode, out_dtypes, name, epilogue=None, extras=(), row_extras=()):
    b_layer = None
    if isinstance(b, tuple):
        b, b_layer = b
    b_shape = b.shape[-2:]
    if mode == "nn":
        (M, K), (K2, N) = a.shape, b_shape
    elif mode == "nt":
        (M, K), (N, K2) = a.shape, b_shape
    else:
        (K, M), (K2, N) = a.shape, b_shape
    assert K == K2, (name, a.shape, b.shape)
    tm, tn, tk = _matmul_tiles(
        M, N, K, a.dtype.itemsize, b.dtype.itemsize, sum(jnp.dtype(d).itemsize for d in out_dtypes),
        sum(e.dtype.itemsize for e in extras))
    assert M % tm == 0 and N % tn == 0 and K % tk == 0, (name, M, N, K, tm, tn, tk)
    nk = K // tk
    if mode == "tn":
        a_spec = pl.BlockSpec((tk, tm), lambda i, j, k: (k, i))
    else:
        a_spec = pl.BlockSpec((tm, tk), lambda i, j, k: (i, k))
    b_block, b_index = ((tn, tk), lambda i, j, k: (j, k)) if mode == "nt" else ((tk, tn), lambda i, j, k: (k, j))
    if b_layer is None:
        b_spec = pl.BlockSpec(b_block, b_index)
    else:
        b_spec = pl.BlockSpec((None,) + b_block, lambda i, j, k: (b_layer,) + b_index(i, j, k))
    dims = {"nn": NN, "nt": NT, "tn": TN}[mode]
    n_ex, n_rex, n_out = len(extras), len(row_extras), len(out_dtypes)

    def body(*refs):
        a_ref, b_ref = refs[:2]
        ex = refs[2:2 + n_ex]
        rex = refs[2 + n_ex:2 + n_ex + n_rex]
        outs = refs[2 + n_ex + n_rex:2 + n_ex + n_rex + n_out]

        def finish(total):
            res = (total,) if epilogue is None else epilogue(total, *[e[...] for e in ex], *[e[...] for e in rex])
            for o, r in zip(outs, res):
                o[...] = r.astype(o.dtype)

        part = _dot(a_ref[...].astype(BF16), b_ref[...].astype(BF16), dims)
        if nk == 1:
            finish(part)
            return
        acc = refs[-1]
        k = pl.program_id(2)

        @pl.when(k == 0)
        def _():
            acc[...] = part

        @pl.when(k > 0)
        def _():
            acc[...] += part

        @pl.when(k == nk - 1)
        def _():
            finish(acc[...])

    in_specs = [a_spec, b_spec]
    in_specs += [pl.BlockSpec((tm, tn), lambda i, j, k: (i, j)) for _ in extras]
    in_specs += [pl.BlockSpec((1, tn), lambda i, j, k: (0, j)) for _ in row_extras]
    out = pl.pallas_call(
        body,
        name=name,
        grid=(M // tm, N // tn, nk),
        in_specs=in_specs,
        out_specs=[pl.BlockSpec((tm, tn), lambda i, j, k: (i, j)) for _ in out_dtypes],
        out_shape=[jax.ShapeDtypeStruct((M, N), dt) for dt in out_dtypes],
        scratch_shapes=[pltpu.VMEM((tm, tn), F32)] if nk > 1 else [],
        compiler_params=_params(("parallel", "parallel", "arbitrary")),
    )(a, b, *extras, *row_extras)
    return out[0] if n_out == 1 else out


ROWWISE_ROW_BYTES = 16 * 1024


def _rowwise(fn, rows, consts, out_defs, sum_widths, name):
    R = rows[0].shape[0]
    per_row = sum(r.shape[1] * r.dtype.itemsize for r in rows) + sum(w * jnp.dtype(dt).itemsize for w, dt in out_defs)
    bm = 512 if per_row <= ROWWISE_ROW_BYTES else 256
    while R % bm:
        bm //= 2
    bm = max(bm, 1)
    n_r, n_c, n_o = len(rows), len(consts), len(out_defs)
    n_s = len(sum_widths)

    def body(*refs):
        r_in = refs[:n_r]
        c_in = refs[n_r:n_r + n_c]
        o_refs = refs[n_r + n_c:n_r + n_c + n_o]
        s_refs = refs[n_r + n_c + n_o:]
        outs, sums = fn([r[...] for r in r_in], [c[...] for c in c_in])
        for o, val in zip(o_refs, outs):
            o[...] = val.astype(o.dtype)
        if n_s:
            @pl.when(pl.program_id(0) == 0)
            def _():
                for s in s_refs:
                    s[...] = jnp.zeros_like(s)

            for s, val in zip(s_refs, sums):
                s[...] += jnp.sum(val, axis=0, keepdims=True)

    in_specs = [pl.BlockSpec((bm, r.shape[1]), lambda i: (i, 0)) for r in rows]
    in_specs += [pl.BlockSpec(c.shape, lambda i: (0, 0)) for c in consts]
    out_specs = [pl.BlockSpec((bm, w), lambda i: (i, 0)) for w, _ in out_defs]
    out_specs += [pl.BlockSpec((1, w), lambda i: (0, 0)) for w in sum_widths]
    out_shape = [jax.ShapeDtypeStruct((R, w), dt) for w, dt in out_defs]
    out_shape += [jax.ShapeDtypeStruct((1, w), F32) for w in sum_widths]
    return pl.pallas_call(
        body,
        name=name,
        grid=(R // bm,),
        in_specs=in_specs,
        out_specs=out_specs,
        out_shape=out_shape,
        compiler_params=_params(("arbitrary",)),
    )(*rows, *consts)


def _rms(x, g):
    r = lax.rsqrt(jnp.mean(x * x, axis=-1, keepdims=True) + EPS)
    return x * r * g


def _rms_bwd(x, g, dy):
    r = lax.rsqrt(jnp.mean(x * x, axis=-1, keepdims=True) + EPS)
    n = x * r
    dn = dy * g
    dx = r * (dn - n * jnp.mean(dn * n, axis=-1, keepdims=True))
    return dx, dy * n


def _rope(x, c, s_up, s_dn, half):
    return x * c + pltpu.roll(x, half, 1) * s_up + pltpu.roll(x, LANES - half, 1) * s_dn


def _rope_tables(positions, lo, d, nope_pass):
    S = positions.shape[0]
    half = d // 2
    inv = 1.0 / (ROPE_THETA ** (jnp.arange(0, d, 2, dtype=F32) / d))
    ang = positions.astype(F32)[:, None] * inv
    cos, sin = jnp.cos(ang), jnp.sin(ang)
    z = lambda n: jnp.zeros((S, n), F32)
    head = jnp.ones((S, lo), F32) if nope_pass else z(lo)
    tail = LANES - lo - d
    c = jnp.concatenate([head, cos, cos, z(tail)], axis=1)
    s_up = jnp.concatenate([z(lo), z(half), sin, z(tail)], axis=1)
    s_dn = jnp.concatenate([z(lo), -sin, z(half), z(tail)], axis=1)
    return c, s_up, s_dn


MLA_FWD_CFG = (2, 1024)
MLA_BWD_CFG = (2, 512)
SB_FWD_CFG = (2, 256)
SB_BWD_CFG = (4, 256)


def _tile_mask(bk, strict):
    row = lax.broadcasted_iota(jnp.int32, (bk, bk), 0)
    col = lax.broadcasted_iota(jnp.int32, (bk, bk), 1)
    return (col < row) if strict else (col <= row)


def _att_layout(S, cfg):
    nch, bk = cfg
    bq = nch * bk
    assert S % bq == 0, (S, cfg)
    rows = [slice(r * bk, (r + 1) * bk) for r in range(nch)]
    q_spec = lambda off=0: pl.BlockSpec((bq, LANES), lambda h, i: (i, off + h))
    kv_spec = lambda off=0: pl.BlockSpec((S, LANES), lambda h, i: (0, off + h))
    return bq, rows, q_spec, kv_spec


def _total(terms):
    terms = list(terms)
    out = terms[0]
    for t in terms[1:]:
        out = out + t
    return out


def _walk(nch, i, step, carry, leftward, alive=None):
    everyone = range(nch)
    if leftward:
        for d in reversed(everyone):
            carry = step(nch * i + d, carry, range(d, nch), {d})
        if alive is None:
            return lax.fori_loop(0, nch * i, lambda t, c: step(nch * i - 1 - t, c, everyone, set()), carry)
        more = lambda tc: jnp.logical_and(tc[0] < nch * i, alive(tc[1]))
        left = lambda tc: (tc[0] + 1, step(nch * i - 1 - tc[0], tc[1], everyone, set()))
        return lax.while_loop(more, left, (jnp.int32(0), carry))[1]
    carry = lax.fori_loop(0, nch * i, lambda kb, c: step(kb, c, everyone, set()), carry)
    for d in everyone:
        carry = step(nch * i + d, carry, range(d, nch), {d})
    return carry


ONES_LANE = MLA_V


def _softmax_attn_fwd(q, k, v, heads, name, q_off=0, k_off=0, v_off=0):
    S = q.shape[0]
    nch, bk = MLA_FWD_CFG
    bq, rows, q_spec, kv_spec = _att_layout(S, MLA_FWD_CFG)

    def body(q_ref, k_ref, v_ref, o_ref, lse_ref):
        i = pl.program_id(1)
        qs = [q_ref[rw, :] for rw in rows]

        def step(kb, cs, active, masked):
            off = pl.multiple_of(kb * bk, bk)
            ks, vs = k_ref[pl.ds(off, bk), :], v_ref[pl.ds(off, bk), :]
            A = list(active)
            s = {r: _dot(qs[r], ks, NT) for r in A}
            s = {r: (jnp.where(_tile_mask(bk, False), s[r], -1e30) if r in masked else s[r]) for r in A}
            m_new = {r: jnp.maximum(cs[r][0], jnp.max(s[r], axis=1, keepdims=True)) for r in A}
            p = {r: jnp.exp(s[r] - m_new[r]) for r in A}
            alpha = {r: jnp.exp(cs[r][0] - m_new[r]) for r in A}
            new = list(cs)
            for r in A:
                new[r] = (m_new[r], alpha[r] * cs[r][1] + _dot(p[r].astype(BF16), vs, NN))
            return tuple(new)

        init = (jnp.full((bk, 1), -1e30, F32), jnp.zeros((bk, LANES), F32))
        cs = _walk(nch, i, step, tuple(init for _ in rows), False)
        for r, (m, acc) in enumerate(cs):
            l = acc[:, ONES_LANE:ONES_LANE + 1]
            o_ref[rows[r], :] = (acc / l).astype(o_ref.dtype)
            lse_ref[rows[r], :] = m + jnp.log(l)

    return pl.pallas_call(
        body,
        name=name,
        grid=(heads, S // bq),
        in_specs=[q_spec(q_off), kv_spec(k_off), kv_spec(v_off)],
        out_specs=[q_spec(), pl.BlockSpec((None, bq, 1), lambda h, i: (h, i, 0))],
        out_shape=[jax.ShapeDtypeStruct((S, heads * LANES), BF16), jax.ShapeDtypeStruct((heads, S, 1), F32)],
        compiler_params=_params(("parallel", "arbitrary")),
    )(q, k, v)


def _softmax_attn_bwd(q, k, v, o, lse, do, heads, scale, name, q_off=0, k_off=0, v_off=0):
    S = q.shape[0]
    nch, bk = MLA_BWD_CFG
    bq, rows, q_spec, kv_spec = _att_layout(S, MLA_BWD_CFG)

    def body(q_ref, k_ref, v_ref, o_ref, lse_ref, do_ref, dq_ref, dk_ref, dv_ref):
        i = pl.program_id(1)

        @pl.when(i == 0)
        def _():
            dk_ref[...] = jnp.zeros_like(dk_ref)
            dv_ref[...] = jnp.zeros_like(dv_ref)

        qs = [q_ref[rw, :] for rw in rows]
        dos = [do_ref[rw, :] for rw in rows]
        lses = [lse_ref[rw, :] for rw in rows]
        deltas = [jnp.sum(dos[r].astype(F32) * o_ref[rows[r], :].astype(F32), axis=1, keepdims=True) for r in range(nch)]

        def step(kb, dqs, active, masked):
            off = pl.multiple_of(kb * bk, bk)
            ks, vs = k_ref[pl.ds(off, bk), :], v_ref[pl.ds(off, bk), :]
            A = list(active)
            s = {r: _dot(qs[r], ks, NT) for r in A}
            s = {r: (jnp.where(_tile_mask(bk, False), s[r], -1e30) if r in masked else s[r]) for r in A}
            p = {r: jnp.exp(s[r] - lses[r]) for r in A}
            dp = {r: _dot(dos[r], vs, NT) for r in A}
            ds = {r: (p[r] * (dp[r] - deltas[r])).astype(BF16) for r in A}
            dv_c = _total(_dot(p[r].astype(BF16), dos[r], TN) for r in A)
            dk_c = _total(_dot(ds[r], qs[r], TN) for r in A)
            dk_ref[pl.ds(off, bk), :] += dk_c
            dv_ref[pl.ds(off, bk), :] += dv_c
            new = list(dqs)
            for r in A:
                new[r] = dqs[r] + _dot(ds[r], ks, NN)
            return tuple(new)

        dqs = _walk(nch, i, step, tuple(jnp.zeros((bk, LANES), F32) for _ in rows), False)
        for r in range(nch):
            dq_ref[rows[r], :] = dqs[r] * scale

    return pl.pallas_call(
        body,
        name=name,
        grid=(heads, S // bq),
        in_specs=[q_spec(q_off), kv_spec(k_off), kv_spec(v_off), q_spec(),
                  pl.BlockSpec((None, bq, 1), lambda h, i: (h, i, 0)), q_spec()],
        out_specs=[q_spec(), kv_spec(), kv_spec()],
        out_shape=[jax.ShapeDtypeStruct((S, heads * LANES), F32)] * 3,
        compiler_params=_params(("parallel", "arbitrary")),
    )(q, k, v, o, lse, do)


def _tri(n, inclusive):
    r = lax.broadcasted_iota(jnp.int32, (n, n), 0)
    c = lax.broadcasted_iota(jnp.int32, (n, n), 1)
    return jnp.where((r >= c) if inclusive else (r > c), 1.0, 0.0).astype(BF16)


def _suffix_sum(x, tri):
    hi = x.astype(BF16)
    lo = (x - hi.astype(F32)).astype(BF16)
    return _dot(hi, tri, NN) + _dot(lo, tri, NN)


def _sb_logs(z):
    lg = jnp.log(1.0 + jnp.exp(-jnp.abs(z)))
    l1m = -(jnp.maximum(z, 0.0) + lg)
    return l1m, l1m + z


SB_SCALE = SB_HEAD_DIM ** -0.5
assert SB_SCALE == 0.125
SB_DEAD = -110.0


def _sb_alive(cs):
    top = cs[0][0]
    for c in cs[1:]:
        top = jnp.maximum(top, c[0])
    return jnp.max(top) > SB_DEAD


def _sb_attn_fwd(qkv, heads, name, q_off, k_off, v_off):
    S = qkv.shape[0]
    nch, bk = SB_FWD_CFG
    bq, rows, q_spec, kv_spec = _att_layout(S, SB_FWD_CFG)

    def body(q_ref, k_ref, v_ref, o_ref):
        i = pl.program_id(1)
        qs = [q_ref[rw, :] * SB_SCALE for rw in rows]
        tri = _tri(bk, False)

        def step(kb, cs, active, masked):
            off = pl.multiple_of(kb * bk, bk)
            ks, vs = k_ref[pl.ds(off, bk), :], v_ref[pl.ds(off, bk), :]
            A = list(active)
            lg = {r: _sb_logs(_dot(qs[r], ks, NT)) for r in A}
            l1m = {r: (jnp.where(_tile_mask(bk, True), lg[r][0], 0.0) if r in masked else lg[r][0]) for r in A}
            suf = {r: _suffix_sum(l1m[r], tri) for r in A}
            ex = {r: lg[r][1] + cs[r][0] + suf[r] for r in A}
            ex = {r: (jnp.where(_tile_mask(bk, True), ex[r], -1e30) if r in masked else ex[r]) for r in A}
            ab = {r: jnp.exp(ex[r]).astype(BF16) for r in A}
            new = list(cs)
            for r in A:
                new[r] = (cs[r][0] + jnp.sum(l1m[r], axis=1, keepdims=True), cs[r][1] + _dot(ab[r], vs, NN))
            return tuple(new)

        init = (jnp.zeros((bk, 1), F32), jnp.zeros((bk, LANES), F32))
        cs = _walk(nch, i, step, tuple(init for _ in rows), True, _sb_alive)
        for r in range(nch):
            o_ref[rows[r], :] = cs[r][1]

    return pl.pallas_call(
        body,
        name=name,
        grid=(heads, S // bq),
        in_specs=[q_spec(q_off), kv_spec(k_off), kv_spec(v_off)],
        out_specs=q_spec(),
        out_shape=jax.ShapeDtypeStruct((S, heads * LANES), F32),
        compiler_params=_params(("parallel", "arbitrary")),
    )(qkv, qkv, qkv)


def _sb_attn_bwd(qkv, o, do, heads, name, q_off, k_off, v_off):
    S = qkv.shape[0]
    nch, bk = SB_BWD_CFG
    bq, rows, q_spec, kv_spec = _att_layout(S, SB_BWD_CFG)

    def body(q_ref, k_ref, v_ref, o_ref, do_ref, dq_ref, dk_ref, dv_ref):
        i = pl.program_id(1)

        @pl.when(i == 0)
        def _():
            dk_ref[...] = jnp.zeros_like(dk_ref)
            dv_ref[...] = jnp.zeros_like(dv_ref)

        tri = _tri(bk, False)
        qs = [q_ref[rw, :] * SB_SCALE for rw in rows]
        dos = [do_ref[rw, :] for rw in rows]
        deltas = [jnp.sum(dos[r].astype(F32) * o_ref[rows[r], :], axis=1, keepdims=True) for r in range(nch)]

        def step(kb, cs, active, masked):
            off = pl.multiple_of(kb * bk, bk)
            ks, vs = k_ref[pl.ds(off, bk), :], v_ref[pl.ds(off, bk), :]
            A = list(active)
            lg = {r: _sb_logs(_dot(qs[r], ks, NT)) for r in A}
            l1m = {r: (jnp.where(_tile_mask(bk, True), lg[r][0], 0.0) if r in masked else lg[r][0]) for r in A}
            suf = {r: _suffix_sum(l1m[r], tri) for r in A}
            ex = {r: lg[r][1] + cs[r][0] + suf[r] for r in A}
            ex = {r: (jnp.where(_tile_mask(bk, True), ex[r], -1e30) if r in masked else ex[r]) for r in A}
            ab = {r: jnp.exp(ex[r]).astype(BF16) for r in A}
            da = {r: _dot(dos[r], vs, NT) for r in A}
            g = {r: ab[r].astype(F32) * da[r] for r in A}
            gs = {r: _suffix_sum(g[r], tri) for r in A}
            beta = {r: jnp.exp(lg[r][1]) for r in A}
            dz = {r: g[r] - beta[r] * (deltas[r] - cs[r][1] - gs[r]) for r in A}
            dz = {r: (jnp.where(_tile_mask(bk, True), dz[r], 0.0) if r in masked else dz[r]) for r in A}
            dzb = {r: dz[r].astype(BF16) for r in A}
            dv_c = _total(_dot(ab[r], dos[r], TN) for r in A)
            dk_c = _total(_dot(dzb[r], qs[r], TN) for r in A)
            dk_ref[pl.ds(off, bk), :] += dk_c
            dv_ref[pl.ds(off, bk), :] += dv_c
            new = list(cs)
            for r in A:
                new[r] = (cs[r][0] + jnp.sum(l1m[r], axis=1, keepdims=True),
                          cs[r][1] + jnp.sum(g[r], axis=1, keepdims=True), cs[r][2] + _dot(dzb[r], ks, NN))
            return tuple(new)

        zcol = jnp.zeros((bk, 1), F32)
        init = (zcol, zcol, jnp.zeros((bk, LANES), F32))
        cs = _walk(nch, i, step, tuple(init for _ in rows), True, _sb_alive)
        for r in range(nch):
            dq_ref[rows[r], :] = cs[r][2] * SB_SCALE

    return pl.pallas_call(
        body,
        name=name,
        grid=(heads, S // bq),
        in_specs=[q_spec(q_off), kv_spec(k_off), kv_spec(v_off), q_spec(), q_spec()],
        out_specs=[q_spec(), kv_spec(), kv_spec()],
        out_shape=[jax.ShapeDtypeStruct((S, heads * LANES), F32)] * 3,
        compiler_params=_params(("parallel", "arbitrary")),
    )(qkv, qkv, qkv, o, do)


SWA_BLK = 128
SWA_GROUP = SWA_HEADS // SWA_KV_HEADS


SWA_NB = 4
SWA_ROWS = SWA_NB * SWA_BLK


def _swa_band_mask(first):
    row = lax.broadcasted_iota(jnp.int32, (SWA_BLK, 2 * SWA_BLK), 0)
    col = lax.broadcasted_iota(jnp.int32, (SWA_BLK, 2 * SWA_BLK), 1)
    return (col > row) & (col <= row + SWA_WINDOW) & (jnp.logical_not(first) | (col >= SWA_BLK))


def _swa_in_specs(v_off):
    gw = SWA_GROUP * LANES
    before = lambda h, n: (jnp.maximum(SWA_NB * n - 1, 0), h)
    return [
        pl.BlockSpec((SWA_ROWS, gw), lambda h, n: (n, h)),
        pl.BlockSpec((SWA_BLK, LANES), before),
        pl.BlockSpec((SWA_ROWS, LANES), lambda h, n: (n, h)),
        pl.BlockSpec((SWA_BLK, LANES), lambda h, n: (jnp.maximum(SWA_NB * n - 1, 0), v_off + h)),
        pl.BlockSpec((SWA_ROWS, LANES), lambda h, n: (n, v_off + h)),
        pl.BlockSpec((1, gw), lambda h, n: (0, h)),
    ]


def _swa_bands(n, kp_ref, kc_ref, vp_ref, vc_ref):
    k_all = jnp.concatenate([kp_ref[...], kc_ref[...]], axis=0)
    v_all = jnp.concatenate([vp_ref[...], vc_ref[...]], axis=0)
    bands = []
    for j in range(SWA_NB):
        rows = slice(j * SWA_BLK, (j + 2) * SWA_BLK)
        bands.append((k_all[rows], v_all[rows], _swa_band_mask((n == 0) if j == 0 else False)))
    return bands


def _swa_fwd(q, k, v, v_off, sink_b, name):
    S = q.shape[0]
    assert S % SWA_ROWS == 0
    scale = SWA_HEAD_DIM ** -0.5
    gw = SWA_GROUP * LANES

    def body(q_ref, kp_ref, kc_ref, vp_ref, vc_ref, sink_ref, o_ref, lse_ref):
        n = pl.program_id(1)
        bands = _swa_bands(n, kp_ref, kc_ref, vp_ref, vc_ref)
        P = [(j, g) for j in range(SWA_NB) for g in range(SWA_GROUP)]
        rows = lambda j: slice(j * SWA_BLK, (j + 1) * SWA_BLK)
        lanes = lambda g: slice(g * LANES, (g + 1) * LANES)
        sk = {g: sink_ref[:, g * LANES:g * LANES + 1] for g in range(SWA_GROUP)}
        s = {(j, g): jnp.where(bands[j][2], _dot(q_ref[rows(j), lanes(g)], bands[j][0], NT) * scale, -1e30) for j, g in P}
        m = {(j, g): jnp.maximum(jnp.max(s[j, g], axis=1, keepdims=True), sk[g]) for j, g in P}
        p = {(j, g): jnp.exp(s[j, g] - m[j, g]) for j, g in P}
        den = {(j, g): jnp.sum(p[j, g], axis=1, keepdims=True) + jnp.exp(sk[g] - m[j, g]) for j, g in P}
        for j, g in P:
            o_ref[rows(j), lanes(g)] = _dot((p[j, g] / den[j, g]).astype(BF16), bands[j][1], NN).astype(o_ref.dtype)
            lse_ref[g, rows(j), :] = m[j, g] + jnp.log(den[j, g])

    return pl.pallas_call(
        body,
        name=name,
        grid=(SWA_KV_HEADS, S // SWA_ROWS),
        in_specs=_swa_in_specs(v_off),
        out_specs=[
            pl.BlockSpec((SWA_ROWS, gw), lambda h, n: (n, h)),
            pl.BlockSpec((SWA_GROUP, SWA_ROWS, 1), lambda h, n: (h, n, 0)),
        ],
        out_shape=[jax.ShapeDtypeStruct((S, SWA_HEADS * LANES), BF16), jax.ShapeDtypeStruct((SWA_HEADS, S, 1), F32)],
        compiler_params=_params(("parallel", "arbitrary")),
    )(q, k, k, v, v, sink_b)


def _swa_bwd(q, k, v, v_off, sink_b, o, lse, do, name):
    S = q.shape[0]
    assert S % SWA_ROWS == 0
    scale = SWA_HEAD_DIM ** -0.5
    gw = SWA_GROUP * LANES

    def body(q_ref, kp_ref, kc_ref, vp_ref, vc_ref, sink_ref, o_ref, lse_ref, do_ref, dq_ref, dk_ref, dv_ref, dsink_ref):
        n = pl.program_id(1)

        @pl.when(n == 0)
        def _():
            dk_ref[...] = jnp.zeros_like(dk_ref)
            dv_ref[...] = jnp.zeros_like(dv_ref)
            dsink_ref[...] = jnp.zeros_like(dsink_ref)

        bands = _swa_bands(n, kp_ref, kc_ref, vp_ref, vc_ref)
        P = [(j, g) for j in range(SWA_NB) for g in range(SWA_GROUP)]
        rows = lambda j: slice(j * SWA_BLK, (j + 1) * SWA_BLK)
        lanes = lambda g: slice(g * LANES, (g + 1) * LANES)
        qs = {(j, g): q_ref[rows(j), lanes(g)] for j, g in P}
        dos = {(j, g): do_ref[rows(j), lanes(g)] for j, g in P}
        lses = {(j, g): lse_ref[g, rows(j), :] for j, g in P}
        delta = {(j, g): jnp.sum(dos[j, g].astype(F32) * o_ref[rows(j), lanes(g)].astype(F32), axis=1, keepdims=True)
                 for j, g in P}
        s = {(j, g): jnp.where(bands[j][2], _dot(qs[j, g], bands[j][0], NT) * scale, -1e30) for j, g in P}
        p = {(j, g): jnp.exp(s[j, g] - lses[j, g]) for j, g in P}
        dp = {(j, g): _dot(dos[j, g], bands[j][1], NT) for j, g in P}
        ds = {(j, g): (p[j, g] * (dp[j, g] - delta[j, g]) * scale).astype(BF16) for j, g in P}
        for j, g in P:
            dq_ref[rows(j), lanes(g)] = _dot(ds[j, g], bands[j][0], NN)
        for g in range(SWA_GROUP):
            p_sink = [jnp.exp(sink_ref[:, g * LANES:g * LANES + 1] - lses[j, g]) * delta[j, g] for j in range(SWA_NB)]
            dsink_ref[:, lanes(g)] += jnp.zeros((1, LANES), F32) - jnp.sum(_total(p_sink), axis=0, keepdims=True)
        dkb = [_total(_dot(ds[j, g], qs[j, g], TN) for g in range(SWA_GROUP)) for j in range(SWA_NB)]
        dvb = [_total(_dot(p[j, g].astype(BF16), dos[j, g], TN) for g in range(SWA_GROUP)) for j in range(SWA_NB)]
        base = pl.multiple_of(n * SWA_ROWS, SWA_ROWS)
        for j in range(SWA_NB):
            own = pl.ds(base + j * SWA_BLK, SWA_BLK)
            after = j + 1 < SWA_NB
            dk_ref[own, :] += dkb[j][SWA_BLK:] + dkb[j + 1][:SWA_BLK] if after else dkb[j][SWA_BLK:]
            dv_ref[own, :] += dvb[j][SWA_BLK:] + dvb[j + 1][:SWA_BLK] if after else dvb[j][SWA_BLK:]

        @pl.when(n > 0)
        def _():
            before = pl.ds(pl.multiple_of(n * SWA_ROWS - SWA_BLK, SWA_BLK), SWA_BLK)
            dk_ref[before, :] += dkb[0][:SWA_BLK]
            dv_ref[before, :] += dvb[0][:SWA_BLK]

    return pl.pallas_call(
        body,
        name=name,
        grid=(SWA_KV_HEADS, S // SWA_ROWS),
        in_specs=_swa_in_specs(v_off) + [
            pl.BlockSpec((SWA_ROWS, gw), lambda h, n: (n, h)),
            pl.BlockSpec((SWA_GROUP, SWA_ROWS, 1), lambda h, n: (h, n, 0)),
            pl.BlockSpec((SWA_ROWS, gw), lambda h, n: (n, h)),
        ],
        out_specs=[
            pl.BlockSpec((SWA_ROWS, gw), lambda h, n: (n, h)),
            pl.BlockSpec((S, LANES), lambda h, n: (0, h)),
            pl.BlockSpec((S, LANES), lambda h, n: (0, h)),
            pl.BlockSpec((1, gw), lambda h, n: (0, h)),
        ],
        out_shape=[
            jax.ShapeDtypeStruct((S, SWA_HEADS * LANES), F32),
            jax.ShapeDtypeStruct((S, SWA_KV_HEADS * LANES), F32),
            jax.ShapeDtypeStruct((S, SWA_KV_HEADS * LANES), F32),
            jax.ShapeDtypeStruct((1, SWA_HEADS * LANES), F32),
        ],
        compiler_params=_params(("parallel", "arbitrary")),
    )(q, k, k, v, v, sink_b, o, lse, do)


def _pad_last(t, width):
    return jnp.pad(t, [(0, 0)] * (t.ndim - 1) + [(0, width - t.shape[-1])])


def _pad_cols(w, heads, real):
    lead = w.shape[:-1]
    return _pad_last(w.reshape(*lead, heads, real), LANES).reshape(*lead, heads * LANES)


def _unpad_cols(g, heads, real):
    lead = g.shape[:-1]
    return g.reshape(*lead, heads, LANES)[..., :real].reshape(*lead, heads * real)


def _pad_rows(w, heads, real):
    lead, n = w.shape[:-2], w.shape[-1]
    w = w.reshape(*lead, heads, real, n)
    return jnp.pad(w, [(0, 0)] * (w.ndim - 2) + [(0, LANES - real), (0, 0)]).reshape(*lead, heads * LANES, n)


def _unpad_rows(g, heads, real):
    lead, n = g.shape[:-2], g.shape[-1]
    return g.reshape(*lead, heads, LANES, n)[..., :real, :].reshape(*lead, heads * real, n)


def _w_in_internal(w_in):
    c_q, c_kv, k_r, q_swa, k_swa, v_swa, q_sb, k_sb, v_sb, gate = jnp.split(w_in, SPLIT_POINTS, axis=-1)
    k_r = jnp.pad(k_r, [(0, 0)] * (k_r.ndim - 1) + [(MLA_NOPE, LANES - MLA_NOPE - MLA_ROPE)])
    w1 = jnp.concatenate([c_q, c_kv, k_r, _pad_cols(q_swa, 8, 64), _pad_cols(k_swa, 2, 64)], axis=-1)
    w2 = [_pad_cols(v_swa, 2, 64), _pad_cols(q_sb, 8, 64), _pad_cols(k_sb, 8, 64), _pad_cols(v_sb, 8, 64)]
    return w1, w2, gate


def _w_in_reference(g1, g2, g3):
    c_q, c_kv, k_r, q_swa, k_swa = jnp.split(g1, [256, 384, 512, 1536], axis=-1)
    v_swa, q_sb, k_sb, v_sb = g2
    return jnp.concatenate([
        c_q, c_kv, k_r[..., MLA_NOPE:MLA_NOPE + MLA_ROPE], _unpad_cols(q_swa, 8, 64), _unpad_cols(k_swa, 2, 64),
        _unpad_cols(v_swa, 2, 64), _unpad_cols(q_sb, 8, 64), _unpad_cols(k_sb, 8, 64), _unpad_cols(v_sb, 8, 64),
        g3], axis=-1)


def _w_ukv_internal(w):
    lead = w.shape[:-1]
    w3 = w.reshape(*lead, MLA_HEADS, MLA_NOPE + MLA_V)
    pad = lambda t: _pad_last(t, LANES).reshape(*lead, MLA_HEADS * LANES)
    return pad(w3[..., :MLA_NOPE]), pad(w3[..., MLA_NOPE:])


def _w_ukv_reference(gk, gv):
    lead = gk.shape[:-1]
    gk = gk.reshape(*lead, MLA_HEADS, LANES)[..., :MLA_NOPE]
    gv = gv.reshape(*lead, MLA_HEADS, LANES)[..., :MLA_V]
    return jnp.concatenate([gk, gv], axis=-1).reshape(*lead, MLA_HEADS * (MLA_NOPE + MLA_V))


def _layer_fwd(x, w, tabs):
    mla_tab, swa_tab = tabs
    sv = {"x": x}

    def f_norm(rows, consts):
        return [_rms(rows[0], consts[0])], []

    (h,) = _rowwise(f_norm, [x], [w["g_mix_pre"]], [(D_MODEL, BF16)], [], "norm_mix_pre")
    p1 = _matmul(h, w["w_in1"], "nn", [F32], "proj_lat")
    p2 = _matmul(h, w["w_in2"], "nn", [BF16], "proj_qkv")
    gates = _matmul(h, w["w_in3"], "nn", [BF16], "proj_gate",
                    epilogue=lambda acc, b: (1.0 / (1.0 + jnp.exp(-(acc + b))),), row_extras=[w["b_gate"]])

    def f_prep(rows, consts):
        t = rows[0]
        gq, gkv = consts[0], consts[1]
        mc, mu, md = rows[1], rows[2], rows[3]
        sc, su, sd = rows[4], rows[5], rows[6]
        cq_n = _rms(t[:, 0:256], gq)
        ckv_n = _rms(t[:, 256:384], gkv)
        kr = _rope(t[:, 384:512], mc, mu, md, MLA_ROPE // 2)
        qs = [_rope(t[:, 512 + j * LANES:512 + (j + 1) * LANES], sc, su, sd, SWA_HEAD_DIM // 2) for j in range(8)]
        ks = [_rope(t[:, 1536 + j * LANES:1536 + (j + 1) * LANES], sc, su, sd, SWA_HEAD_DIM // 2) for j in range(2)]
        return [cq_n, ckv_n, kr, jnp.concatenate(qs, axis=1), jnp.concatenate(ks, axis=1)], []

    cq_n, ckv_n, kr, q_swa, k_swa = _rowwise(
        f_prep, [p1, *mla_tab["k"], *swa_tab["f"]], [w["g_q_lat"], w["g_kv_lat"]],
        [(256, BF16), (128, BF16), (LANES, F32), (1024, BF16), (256, BF16)], [], "lat_prep")

    q_lat = _matmul(cq_n, w["w_uq"], "nn", [F32], "mla_q_up")
    k_lat = _matmul(ckv_n, w["w_ukv_k"], "nn", [F32], "mla_k_up")
    def ones_lane(acc):
        lane = lax.broadcasted_iota(jnp.int32, acc.shape, 1) % LANES
        return (jnp.where(lane == ONES_LANE, 1.0, acc),)

    v_mla = _matmul(ckv_n, w["w_ukv_v"], "nn", [BF16], "mla_v_up", epilogue=ones_lane)
    mla_scale = (MLA_NOPE + MLA_ROPE) ** -0.5

    def f_mla_prep(rows, consts):
        ql, kl, krr, mc, mu, md = rows
        qs = [_rope(ql[:, j * LANES:(j + 1) * LANES], mc, mu, md, MLA_ROPE // 2) * mla_scale for j in range(8)]
        ks = [kl[:, j * LANES:(j + 1) * LANES] + krr for j in range(8)]
        return [jnp.concatenate(qs, axis=1), jnp.concatenate(ks, axis=1)], []

    q_mla, k_mla = _rowwise(f_mla_prep, [q_lat, k_lat, kr, *mla_tab["q"]], [], [(1024, BF16), (1024, BF16)], [], "mla_prep")

    o_mla, lse_mla = _softmax_attn_fwd(q_mla, k_mla, v_mla, MLA_HEADS, "mla_fwd")
    o_swa, lse_swa = _swa_fwd(q_swa, k_swa, p2, 0, w["sink_b"], "swa_fwd")
    o_sb = _sb_attn_fwd(p2, SB_HEADS, "sb_fwd", 2, 10, 18)

    oa = _matmul(o_mla, w["w_o_mla"], "nn", [BF16], "o_proj_mla")
    ob = _matmul(o_swa, w["w_o_swa"], "nn", [BF16], "o_proj_swa")
    oc = _matmul(o_sb, w["w_o_sb"], "nn", [BF16], "o_proj_sb")

    def f_mix(rows, consts):
        a, b, c, g = rows
        g = g.astype(F32)
        return [g[:, 0:1024] * a + g[:, 1024:2048] * b + g[:, 2048:3072] * c], []

    (mixed,) = _rowwise(f_mix, [oa, ob, oc, gates], [], [(D_MODEL, BF16)], [], "gate_mix")
    y = _matmul(mixed, w["w_out"], "nn", [F32], "out_proj")

    def f_res_norm(rows, consts):
        return [rows[0] + _rms(rows[1], consts[0])], []

    (x1,) = _rowwise(f_res_norm, [x, y], [w["g_mix_post"]], [(D_MODEL, F32)], [], "res_norm_mix")
    (h2,) = _rowwise(f_norm, [x1], [w["g_mlp_pre"]], [(D_MODEL, BF16)], [], "norm_mlp_pre")

    def relu2(acc):
        r = jnp.maximum(acc, 0.0)
        return acc, r * r

    up, u = _matmul(h2, w["w_up"], "nn", [BF16, BF16], "mlp_up", epilogue=relu2)
    zd = _matmul(u, w["w_down"], "nn", [F32], "mlp_down")
    (x2,) = _rowwise(f_res_norm, [x1, zd], [w["g_mlp_post"]], [(D_MODEL, F32)], [], "res_norm_mlp")

    sv.update(h=h, p1=p1, p2=p2, gates=gates, cq_n=cq_n, ckv_n=ckv_n, q_swa=q_swa, k_swa=k_swa, q_mla=q_mla,
              k_mla=k_mla, v_mla=v_mla, o_mla=o_mla, lse_mla=lse_mla, o_swa=o_swa, lse_swa=lse_swa, o_sb=o_sb,
              oa=oa, ob=ob, oc=oc, mixed=mixed, y=y, x1=x1, h2=h2, up=up, u=u, zd=zd)
    return x2, sv


def _layer_bwd(dx2, w, sv, tabs):
    mla_tab, swa_tab = tabs
    gr = {}

    def f_norm_bwd(rows, consts):
        dx, dg = _rms_bwd(rows[0], consts[0], rows[1])
        return [dx], [dg]

    def f_norm_bwd_res(rows, consts):
        dx, dg = _rms_bwd(rows[0], consts[0], rows[1])
        return [rows[2] + dx], [dg]

    dzd, gr["g_mlp_post"] = _rowwise(f_norm_bwd, [sv["zd"], dx2], [w["g_mlp_post"]], [(D_MODEL, BF16)], [D_MODEL], "b_norm_mlp_post")
    gr["w_down"] = _matmul(sv["u"], dzd, "tn", [BF16], "b_w_down")
    dup = _matmul(dzd, w["w_down"], "nt", [BF16], "b_mlp_down",
                  epilogue=lambda acc, up: (acc * 2.0 * jnp.maximum(up.astype(F32), 0.0),), extras=[sv["up"]])
    gr["w_up"] = _matmul(sv["h2"], dup, "tn", [BF16], "b_w_up")
    dh2 = _matmul(dup, w["w_up"], "nt", [F32], "b_mlp_up")
    dx1, gr["g_mlp_pre"] = _rowwise(f_norm_bwd_res, [sv["x1"], dh2, dx2], [w["g_mlp_pre"]], [(D_MODEL, F32)], [D_MODEL], "b_norm_mlp_pre")

    dy, gr["g_mix_post"] = _rowwise(f_norm_bwd, [sv["y"], dx1], [w["g_mix_post"]], [(D_MODEL, BF16)], [D_MODEL], "b_norm_mix_post")
    gr["w_out"] = _matmul(sv["mixed"], dy, "tn", [BF16], "b_w_out")
    dmixed = _matmul(dy, w["w_out"], "nt", [F32], "b_out_proj")

    def f_mix_bwd(rows, consts):
        dm, a, b, c, g = rows
        g = g.astype(F32)
        outs, dls = [], []
        for j, o in enumerate((a, b, c)):
            gj = g[:, j * D_MODEL:(j + 1) * D_MODEL]
            outs.append(dm * gj)
            dls.append(dm * o * gj * (1.0 - gj))
        dl = jnp.concatenate(dls, axis=1)
        return outs + [dl], [dl]

    doa, dob, doc, dlogit, gr["b_gate"] = _rowwise(
        f_mix_bwd, [dmixed, sv["oa"], sv["ob"], sv["oc"], sv["gates"]], [],
        [(D_MODEL, BF16)] * 3 + [(P3_W, BF16)], [P3_W], "b_gate_mix")

    gr["w_o_mla"] = _matmul(sv["o_mla"], doa, "tn", [BF16], "b_w_o_mla")
    gr["w_o_swa"] = _matmul(sv["o_swa"], dob, "tn", [BF16], "b_w_o_swa")
    gr["w_o_sb"] = _matmul(sv["o_sb"], doc, "tn", [BF16], "b_w_o_sb")
    do_mla = _matmul(doa, w["w_o_mla"], "nt", [BF16], "b_o_proj_mla")
    do_swa = _matmul(dob, w["w_o_swa"], "nt", [BF16], "b_o_proj_swa")
    do_sb = _matmul(doc, w["w_o_sb"], "nt", [BF16], "b_o_proj_sb")

    dq_sb, dk_sb, dv_sb = _sb_attn_bwd(sv["p2"], sv["o_sb"], do_sb, SB_HEADS, "sb_bwd", 2, 10, 18)
    dq_swa, dk_swa, dv_swa, dsink = _swa_bwd(sv["q_swa"], sv["k_swa"], sv["p2"], 0, w["sink_b"], sv["o_swa"],
                                             sv["lse_swa"], do_swa, "swa_bwd")
    gr["swa_sinks"] = dsink.reshape(SWA_HEADS, LANES)[:, 0]
    dq_mla, dk_mla, dv_mla = _softmax_attn_bwd(sv["q_mla"], sv["k_mla"], sv["v_mla"], sv["o_mla"], sv["lse_mla"], do_mla,
                                               MLA_HEADS, (MLA_NOPE + MLA_ROPE) ** -0.5, "mla_bwd")

    def f_mla_post(rows, consts):
        dq, dk, qc, qu, qd, kc, ku, kd = rows
        dqs = [_rope(dq[:, j * LANES:(j + 1) * LANES], qc, qu, qd, MLA_ROPE // 2) for j in range(8)]
        dkr = dk[:, 0:LANES]
        for j in range(1, 8):
            dkr = dkr + dk[:, j * LANES:(j + 1) * LANES]
        return [jnp.concatenate(dqs, axis=1), _rope(dkr, kc, ku, kd, MLA_ROPE // 2)], []

    dq_lat, dkr = _rowwise(f_mla_post, [dq_mla, dk_mla, *mla_tab["q_inv"], *mla_tab["k_inv"]], [],
                           [(1024, BF16), (LANES, F32)], [], "b_mla_post")
    gr["w_uq"] = _matmul(sv["cq_n"], dq_lat, "tn", [BF16], "b_w_uq")
    gr["w_ukv_k"] = _matmul(sv["ckv_n"], dk_mla, "tn", [BF16], "b_w_ukv_k")
    gr["w_ukv_v"] = _matmul(sv["ckv_n"], dv_mla, "tn", [BF16], "b_w_ukv_v")
    dcq_n = _matmul(dq_lat, w["w_uq"], "nt", [F32], "b_mla_q_up")
    dckv_a = _matmul(dk_mla, w["w_ukv_k"], "nt", [F32], "b_mla_k_up")
    dckv_b = _matmul(dv_mla, w["w_ukv_v"], "nt", [F32], "b_mla_v_up")

    def f_prep_bwd(rows, consts):
        t, dcq, dca, dcb, dkr_, dqs, dks, sc, su, sd = rows
        gq, gkv = consts
        dc_q, dgq = _rms_bwd(t[:, 0:256], gq, dcq)
        dc_kv, dgkv = _rms_bwd(t[:, 256:384], gkv, dca + dcb)
        q_parts = [_rope(dqs[:, j * LANES:(j + 1) * LANES], sc, su, sd, SWA_HEAD_DIM // 2) for j in range(8)]
        k_parts = [_rope(dks[:, j * LANES:(j + 1) * LANES], sc, su, sd, SWA_HEAD_DIM // 2) for j in range(2)]
        return [jnp.concatenate([dc_q, dc_kv, dkr_] + q_parts + k_parts, axis=1)], [dgq, dgkv]

    dp1, gr["g_q_lat"], gr["g_kv_lat"] = _rowwise(
        f_prep_bwd, [sv["p1"], dcq_n, dckv_a, dckv_b, dkr, dq_swa, dk_swa, *swa_tab["inv"]], [w["g_q_lat"], w["g_kv_lat"]],
        [(P1_W, BF16)], [256, 128], "b_lat_prep")

    gr["w_in1"] = _matmul(sv["h"], dp1, "tn", [BF16], "b_w_in_lat")
    dh = _matmul(dp1, w["w_in1"], "nt", [F32], "b_proj_lat")
    gr["w_in2"] = []
    add_prev = lambda acc, prev: (acc + prev,)
    for piece, wp, tag in zip((dv_swa, dq_sb, dk_sb, dv_sb), w["w_in2_parts"], ("vswa", "qsb", "ksb", "vsb")):
        gr["w_in2"].append(_matmul(sv["h"], piece, "tn", [BF16], "b_w_in_" + tag))
        dh = _matmul(piece, wp, "nt", [F32], "b_proj_" + tag, epilogue=add_prev, extras=[dh])
    gr["w_in3"] = _matmul(sv["h"], dlogit, "tn", [BF16], "b_w_in_gate")
    dh = _matmul(dlogit, w["w_in3"], "nt", [F32], "b_proj_gate", epilogue=add_prev, extras=[dh])
    dx, gr["g_mix_pre"] = _rowwise(f_norm_bwd_res, [sv["x"], dh, dx1], [w["g_mix_pre"]], [(D_MODEL, F32)], [D_MODEL], "b_norm_mix_pre")
    return dx, gr


def _local_step(x, positions, loss_target, full):
    mc, mu, md = _rope_tables(positions, MLA_NOPE, MLA_ROPE, True)
    kc, ku, kd = _rope_tables(positions, MLA_NOPE, MLA_ROPE, False)
    sc, su, sd = _rope_tables(positions, 0, SWA_HEAD_DIM, False)
    mla_tab = {"q": (mc, mu, md), "k": (kc, ku, kd), "q_inv": (mc, -mu, -md), "k_inv": (kc, -ku, -kd)}
    swa_tab = {"f": (sc, su, sd), "inv": (sc, -su, -sd)}
    tabs = (mla_tab, swa_tab)

    big = {n: full[n].astype(BF16) for n in SHARDED}
    w1, w2, w3 = _w_in_internal(big["w_in"])
    uk, uv = _w_ukv_internal(big["w_ukv"])
    stacks = {
        "w_in1": w1, "w_in2": jnp.concatenate(w2, axis=-1), "w_in3": w3,
        "w_uq": _pad_cols(big["w_uq"], MLA_HEADS, MLA_NOPE + MLA_ROPE), "w_ukv_k": uk, "w_ukv_v": uv,
        "w_o_mla": _pad_rows(big["w_o_mla"], 8, 64), "w_o_swa": _pad_rows(big["w_o_swa"], 8, 64),
        "w_o_sb": _pad_rows(big["w_o_sb"], 8, 64), "w_out": big["w_out"], "w_up": big["w_up"], "w_down": big["w_down"],
    }
    layers = []
    for l in range(DEPTH):
        layers.append({
            **{n: (t, l) for n, t in stacks.items()}, "w_in2_parts": [(t, l) for t in w2],
            "g_mix_pre": full["g_mix_pre"][l][None], "b_gate": full["b_gate"][l][None],
            "g_q_lat": full["g_q_lat"][l][None], "g_kv_lat": full["g_kv_lat"][l][None],
            "g_mix_post": full["g_mix_post"][l][None], "g_mlp_pre": full["g_mlp_pre"][l][None],
            "g_mlp_post": full["g_mlp_post"][l][None],
            "sink_b": jnp.repeat(full["swa_sinks"][l], LANES)[None],
        })

    saved = []
    h = x
    for l in range(DEPTH):
        h, sv = _layer_fwd(h, layers[l], tabs)
        saved.append(sv)

    def f_loss(rows, consts):
        err = rows[0] - rows[1]
        return [err * (1.0 / D_MODEL)], [jnp.sum(err * err, axis=1, keepdims=True)]

    dy, sq = _rowwise(f_loss, [h, loss_target], [], [(D_MODEL, F32)], [1], "loss_head")
    loss_part = sq * (0.5 / D_MODEL)

    grs = [None] * DEPTH
    d = dy
    for l in reversed(range(DEPTH)):
        d, grs[l] = _layer_bwd(d, layers[l], saved[l], tabs)
    st = lambda pick: jnp.stack([pick(grs[l]) for l in range(DEPTH)])
    vec = lambda n: st(lambda gr: gr[n][0] if gr[n].ndim == 2 else gr[n])
    stacked = {n: vec(n) for n in SMALL}
    stacked.update({n: st(lambda gr: gr[n]) for n in ("w_out", "w_up", "w_down")})
    stacked["w_in"] = _w_in_reference(st(lambda gr: gr["w_in1"]), [st(lambda gr: gr["w_in2"][p]) for p in range(4)],
                                      st(lambda gr: gr["w_in3"]))
    stacked["w_uq"] = _unpad_cols(st(lambda gr: gr["w_uq"]), MLA_HEADS, MLA_NOPE + MLA_ROPE)
    stacked["w_ukv"] = _w_ukv_reference(st(lambda gr: gr["w_ukv_k"]), st(lambda gr: gr["w_ukv_v"]))
    for n in ("w_o_mla", "w_o_swa", "w_o_sb"):
        stacked[n] = _unpad_rows(st(lambda gr: gr[n]), 8, 64)
    return loss_part, d, stacked


def _lane_padded(width):
    return max(width, LANES)


def _rows_of(a, dtype):
    extra = _lane_padded(a.shape[-1]) - a.shape[-1]
    if extra:
        a = jnp.pad(a, [(0, 0)] * (a.ndim - 1) + [(0, extra)])
    return a.astype(dtype).reshape(-1, LANES)


def _pack(shards, small, dtype):
    parts = [_rows_of(shards[n], dtype) for n in SHARDED]
    if small is not None:
        parts += [_rows_of(small[n], dtype) for n in SMALL]
    slab = jnp.concatenate(parts, axis=0)
    pad = (-slab.shape[0]) % SLAB_ROW_ALIGN
    return jnp.pad(slab, ((0, pad), (0, 0)))


def _unpack(slab, shard_shapes, small_shapes):
    out, r = {}, 0
    shapes = [(n, shard_shapes[n]) for n in SHARDED]
    if small_shapes is not None:
        shapes += [(n, small_shapes[n]) for n in SMALL]
    for n, shape in shapes:
        wide = shape[:-1] + (_lane_padded(shape[-1]),)
        rows = int(np.prod(wide)) // LANES
        out[n] = slab[r:r + rows].reshape(wide)[..., :shape[-1]]
        r += rows
    return out


def _chip_exchange(src, name):
    rows = src.shape[-2]

    def body(src_ref, out_ref, send_sems, recv_sems):
        x, y, c = lax.axis_index("x"), lax.axis_index("y"), lax.axis_index("c")
        me = 2 * x + y
        chips = [(1 - x, y), (x, 1 - y), (1 - x, 1 - y)]
        sends = []
        for k, (cx, cy) in enumerate(chips):
            cp = pltpu.make_async_remote_copy(
                src_ref=src_ref.at[2 * cx + cy], dst_ref=out_ref.at[me], send_sem=send_sems.at[k],
                recv_sem=recv_sems.at[k], device_id=(cx, cy, c), device_id_type=pl.DeviceIdType.MESH)
            cp.start()
            sends.append(cp)
        for k, (cx, cy) in enumerate(chips):
            pltpu.make_async_remote_copy(
                src_ref=src_ref.at[me], dst_ref=out_ref.at[2 * cx + cy], send_sem=send_sems.at[k],
                recv_sem=recv_sems.at[k], device_id=(cx, cy, c), device_id_type=pl.DeviceIdType.MESH).wait_recv()
        for cp in sends:
            cp.wait_send()

    return pl.pallas_call(
        body,
        name=name,
        in_specs=[pl.BlockSpec(memory_space=pl.ANY)],
        out_specs=pl.BlockSpec(memory_space=pl.ANY),
        out_shape=jax.ShapeDtypeStruct((N_CHIPS, rows, LANES), src.dtype),
        scratch_shapes=[pltpu.SemaphoreType.DMA((3,)), pltpu.SemaphoreType.DMA((3,))],
    )(src)


def _half_rows(c, half):
    return pl.ds(pl.multiple_of(c * half, SLAB_ROW_ALIGN // 2), half)


def _gather_weights(src, name):
    rows = src.shape[0]
    half = rows // 2

    def body(src_ref, out_ref, send_sems, recv_sems):
        x, y, c = lax.axis_index("x"), lax.axis_index("y"), lax.axis_index("c")
        me = 2 * x + y
        chips = [(1 - x, y), (x, 1 - y), (1 - x, 1 - y)]

        def copy(k, src_view, slab, part, to):
            return pltpu.make_async_remote_copy(
                src_ref=src_view, dst_ref=out_ref.at[slab, _half_rows(part, half), :], send_sem=send_sems.at[k],
                recv_sem=recv_sems.at[k], device_id=to, device_id_type=pl.DeviceIdType.MESH)

        sends = [copy(k, src_ref.at[_half_rows(c, half), :], me, c, (cx, cy, c)) for k, (cx, cy) in enumerate(chips)]
        for cp in sends:
            cp.start()
        for k, (cx, cy) in enumerate(chips):
            j = 2 * cx + cy
            landed = out_ref.at[j, _half_rows(c, half), :]
            copy(k, landed, j, c, (cx, cy, c)).wait_recv()
            fwd = copy(3 + k, landed, j, c, (x, y, 1 - c))
            fwd.start()
            sends.append(fwd)
        for k, (cx, cy) in enumerate(chips):
            j = 2 * cx + cy
            copy(3 + k, out_ref.at[j, _half_rows(1 - c, half), :], j, 1 - c, (x, y, 1 - c)).wait_recv()
        for cp in sends:
            cp.wait_send()

    return pl.pallas_call(
        body,
        name=name,
        in_specs=[pl.BlockSpec(memory_space=pl.ANY)],
        out_specs=pl.BlockSpec(memory_space=pl.ANY),
        out_shape=jax.ShapeDtypeStruct((N_CHIPS, rows, LANES), src.dtype),
        scratch_shapes=[pltpu.SemaphoreType.DMA((6,)), pltpu.SemaphoreType.DMA((6,))],
    )(src)


def _sibling_halves(src, name):
    n, rows, _ = src.shape
    half = rows // 2

    def body(src_ref, out_ref, send_sem, recv_sem):
        x, y, c = lax.axis_index("x"), lax.axis_index("y"), lax.axis_index("c")
        cp = pltpu.make_async_remote_copy(
            src_ref=src_ref.at[:, _half_rows(1 - c, half), :], dst_ref=out_ref, send_sem=send_sem, recv_sem=recv_sem,
            device_id=(x, y, 1 - c), device_id_type=pl.DeviceIdType.MESH)
        cp.start()
        cp.wait()

    return pl.pallas_call(
        body,
        name=name,
        in_specs=[pl.BlockSpec(memory_space=pl.ANY)],
        out_specs=pl.BlockSpec(memory_space=pl.ANY),
        out_shape=jax.ShapeDtypeStruct((n, half, LANES), src.dtype),
        scratch_shapes=[pltpu.SemaphoreType.DMA, pltpu.SemaphoreType.DMA],
    )(src)


def _sibling_join(src, name):
    half = src.shape[0]

    def body(src_ref, out_ref, send_sem, recv_sem):
        x, y, c = lax.axis_index("x"), lax.axis_index("y"), lax.axis_index("c")
        cp = pltpu.make_async_remote_copy(
            src_ref=src_ref, dst_ref=out_ref.at[_half_rows(c, half), :], send_sem=send_sem, recv_sem=recv_sem,
            device_id=(x, y, 1 - c), device_id_type=pl.DeviceIdType.MESH)
        cp.start()
        pltpu.make_async_remote_copy(
            src_ref=src_ref, dst_ref=out_ref.at[_half_rows(1 - c, half), :], send_sem=send_sem, recv_sem=recv_sem,
            device_id=(x, y, 1 - c), device_id_type=pl.DeviceIdType.MESH).wait_recv()
        cp.wait_send()

    return pl.pallas_call(
        body,
        name=name,
        in_specs=[pl.BlockSpec(memory_space=pl.ANY)],
        out_specs=pl.BlockSpec(memory_space=pl.ANY),
        out_shape=jax.ShapeDtypeStruct((2 * half, LANES), src.dtype),
        scratch_shapes=[pltpu.SemaphoreType.DMA, pltpu.SemaphoreType.DMA],
    )(src)


SUM_ROWS = 1024


def _pair_sum(mine, theirs, c, name):
    n, half, _ = theirs.shape
    blocks = half // SUM_ROWS

    def body(c_ref, a_ref, b_ref, o_ref):
        o_ref[...] = (a_ref[...].astype(F32) + b_ref[...].astype(F32)).astype(o_ref.dtype)

    return pl.pallas_call(
        body,
        name=name,
        grid_spec=pltpu.PrefetchScalarGridSpec(
            num_scalar_prefetch=1,
            grid=(blocks,),
            in_specs=[pl.BlockSpec((n, SUM_ROWS, LANES), lambda i, c_ref: (0, c_ref[0] * blocks + i, 0)),
                      pl.BlockSpec((n, SUM_ROWS, LANES), lambda i, c_ref: (0, i, 0))],
            out_specs=pl.BlockSpec((n, SUM_ROWS, LANES), lambda i, c_ref: (0, i, 0)),
        ),
        out_shape=jax.ShapeDtypeStruct((n, half, LANES), BF16),
        compiler_params=_params(("arbitrary",)),
    )(jnp.reshape(c, (1,)).astype(jnp.int32), mine, theirs)


def _sum_chips(own, landed, me, name):
    rows = landed.shape[1]

    def body(me_ref, a_ref, b_ref, o_ref):
        t = [jnp.where(me_ref[0] == j, a_ref[j], b_ref[j]).astype(F32) for j in range(N_CHIPS)]
        o_ref[...] = ((t[0] + t[1]) + t[2]) + t[3]

    slabs = pl.BlockSpec((N_CHIPS, SUM_ROWS, LANES), lambda i, me_ref: (0, i, 0))
    return pl.pallas_call(
        body,
        name=name,
        grid_spec=pltpu.PrefetchScalarGridSpec(
            num_scalar_prefetch=1,
            grid=(rows // SUM_ROWS,),
            in_specs=[slabs, slabs],
            out_specs=pl.BlockSpec((SUM_ROWS, LANES), lambda i, me_ref: (i, 0)),
        ),
        out_shape=jax.ShapeDtypeStruct((rows, LANES), F32),
        compiler_params=_params(("arbitrary",)),
    )(jnp.reshape(me, (1,)).astype(jnp.int32), own, landed)


def _adamw(w, m, v, g, name):
    shape = w.shape
    flat = lambda a: a.reshape(-1, shape[-1])

    def fn(rows, consts):
        w_, m_, v_, g_ = rows
        m_new = ADAM_B1 * m_ + (1.0 - ADAM_B1) * g_
        v_new = ADAM_B2 * v_ + (1.0 - ADAM_B2) * (g_ * g_)
        m_hat = m_new / (1.0 - ADAM_B1 ** ADAM_STEP)
        v_hat = v_new / (1.0 - ADAM_B2 ** ADAM_STEP)
        delta = -ADAM_LR * (m_hat / (jnp.sqrt(v_hat) + ADAM_EPS) + ADAM_WD * w_)
        return [delta, m_new, v_new], []

    outs = _rowwise(fn, [flat(w), flat(m), flat(v), flat(g)], [], [(shape[-1], F32)] * 3, [], name)
    return [o.reshape(shape) for o in outs]


def kernel(x, positions, g_mix_pre, w_in, b_gate, g_q_lat, g_kv_lat, w_uq, w_ukv, swa_sinks, w_o_mla, w_o_swa, w_o_sb, w_out, g_mix_post, g_mlp_pre, w_up, w_down, g_mlp_post, loss_target, m_g_mix_pre, m_w_in, m_b_gate, m_g_q_lat, m_g_kv_lat, m_w_uq, m_w_ukv, m_swa_sinks, m_w_o_mla, m_w_o_swa, m_w_o_sb, m_w_out, m_g_mix_post, m_g_mlp_pre, m_w_up, m_w_down, m_g_mlp_post, v_g_mix_pre, v_w_in, v_b_gate, v_g_q_lat, v_g_kv_lat, v_w_uq, v_w_ukv, v_swa_sinks, v_w_o_mla, v_w_o_swa, v_w_o_sb, v_w_out, v_g_mix_post, v_g_mlp_pre, v_w_up, v_w_down, v_g_mlp_post):
    given = dict(locals())
    wts = {n: given[n] for n in WEIGHTS}
    mom_m = {n: given["m_" + n] for n in WEIGHTS}
    mom_v = {n: given["v_" + n] for n in WEIGHTS}
    shard_shapes = {n: wts[n].shape for n in SHARDED}
    small_shapes = {n: wts[n].shape for n in SMALL}

    me = 2 * lax.axis_index("x") + lax.axis_index("y")
    core = lax.axis_index("c")
    gathered = _gather_weights(_pack(wts, None, BF16), "gather_weights")
    full = {n: wts[n] for n in SMALL}
    per_chip = [_unpack(gathered[j], shard_shapes, None) for j in range(N_CHIPS)]
    for n in SHARDED:
        own = wts[n].astype(BF16)
        full[n] = jnp.concatenate([jnp.where(me == j, own, per_chip[j][n]) for j in range(N_CHIPS)], axis=SHARD_AXIS[n])

    loss_part, grad_x, grads = _local_step(x[0], positions[0], loss_target[0], full)
    loss = lax.psum(loss_part[0, 0], ("x", "y", "c"))

    small_g = {n: grads[n] for n in SMALL}
    slabs = []
    for j in range(N_CHIPS):
        shard = {n: jnp.split(grads[n], N_CHIPS, axis=SHARD_AXIS[n])[j] for n in SHARDED}
        slabs.append(_pack(shard, small_g, BF16))
    per_chip_g = jnp.stack(slabs)
    theirs = _sibling_halves(per_chip_g, "pair_grads")
    pair = _pair_sum(per_chip_g, theirs, core, "sum_pair")
    landed = _chip_exchange(pair, "scatter_grads")
    my_half = _sum_chips(pair, landed, me, "sum_chips")
    g_slab = lax.dynamic_update_slice(_sibling_join(my_half, "join_grads"), my_half, (core * my_half.shape[0], 0))

    g = _unpack(g_slab, shard_shapes, small_shapes)
    stepped = {n: _adamw(wts[n], mom_m[n], mom_v[n], g[n], "adamw_" + n) for n in WEIGHTS}
    outs = [loss, grad_x[None]] + [g[n] for n in WEIGHTS]
    for part in range(3):
        outs += [stepped[n][part] for n in WEIGHTS]
    return tuple(outs)
```

```python
import numpy as np
import jax
import jax.numpy as jnp
from jax import lax
from jax.experimental import pallas as pl
from jax.experimental.pallas import tpu as pltpu

F32 = jnp.float32
BF16 = jnp.bfloat16

D_MODEL = 1024
DEPTH = 4
MLA_HEADS, MLA_Q_LORA, MLA_KV_LORA, MLA_NOPE, MLA_ROPE, MLA_V = 8, 256, 128, 64, 32, 64
SWA_HEADS, SWA_KV_HEADS, SWA_HEAD_DIM, SWA_WINDOW = 8, 2, 64, 128
SB_HEADS, SB_HEAD_DIM = 8, 64
D_FF = 4 * D_MODEL
ROPE_THETA = 10000.0
EPS = 1e-6
SPLIT_SIZES = (256, 128, 32, 512, 128, 128, 512, 512, 512, 3 * D_MODEL)
SPLIT_POINTS = [int(v) for v in np.cumsum(SPLIT_SIZES)[:-1]]

ADAM_LR, ADAM_B1, ADAM_B2, ADAM_EPS, ADAM_WD, ADAM_STEP = 0.001, 0.9, 0.999, 1e-08, 0.01, 10

LANES = 128
V7X_VMEM_BYTES = 64 * 1024 * 1024
VMEM_LIMIT = V7X_VMEM_BYTES - 8 * 1024 * 1024
MATMUL_VMEM_BUDGET = 36 * 1024 * 1024
N_CHIPS = 4
SLAB_ROW_ALIGN = 2048

P1_W = 256 + 128 + 128 + 1024 + 256
P2_W = 256 + 1024 + 1024 + 1024
P3_W = 3 * D_MODEL

SHARDED = ("w_in", "w_uq", "w_ukv", "w_o_mla", "w_o_swa", "w_o_sb", "w_out", "w_up", "w_down")
SHARD_AXIS = {"w_in": 2, "w_uq": 2, "w_ukv": 2, "w_o_mla": 2, "w_o_swa": 2, "w_o_sb": 2, "w_out": 1, "w_up": 2, "w_down": 1}
SMALL = ("g_mix_pre", "b_gate", "g_q_lat", "g_kv_lat", "swa_sinks", "g_mix_post", "g_mlp_pre", "g_mlp_post")
WEIGHTS = ("g_mix_pre", "w_in", "b_gate", "g_q_lat", "g_kv_lat", "w_uq", "w_ukv", "swa_sinks", "w_o_mla", "w_o_swa",
           "w_o_sb", "w_out", "g_mix_post", "g_mlp_pre", "w_up", "w_down", "g_mlp_post")

NN = (((1,), (0,)), ((), ()))
NT = (((1,), (1,)), ((), ()))
TN = (((0,), (0,)), ((), ()))


def _dot(a, b, dims):
    return lax.dot_general(a, b, dims, preferred_element_type=F32)


def _params(sem):
    return pltpu.CompilerParams(dimension_semantics=sem, vmem_limit_bytes=VMEM_LIMIT)


def _largest_tile(n, cap):
    if n <= cap:
        return n
    best = LANES
    for t in range(LANES, cap + 1, LANES):
        if n % t == 0:
            best = t
    return best


def _matmul_tiles(M, N, K, a_bytes, b_bytes, out_bytes, extra_bytes):
    tn = _largest_tile(N, 1792)
    tm = _largest_tile(M, 1024 if tn <= 1024 else 512)
    tk = _largest_tile(K, 2048)

    def need(tm_, tk_):
        acc = 4 * tm_ * tn if tk_ < K else 0
        return 2 * (tm_ * tk_ * a_bytes + tk_ * tn * b_bytes + tm_ * tn * (out_bytes + extra_bytes)) + acc

    while need(tm, tk) > MATMUL_VMEM_BUDGET:
        if tk >= tm and tk % 256 == 0:
            tk //= 2
        elif tm % 256 == 0:
            tm //= 2
        else:
            break
    return tm, tn, tk


def _matmul(a, b, mode, out_dtypes, name, epilogue=None, extras=(), row_extras=()):
    b_layer = None
    if isinstance(b, tuple):
        b, b_layer = b
    b_shape = b.shape[-2:]
    if mode == "nn":
        (M, K), (K2, N) = a.shape, b_shape
    elif mode == "nt":
        (M, K), (N, K2) = a.shape, b_shape
    else:
        (K, M), (K2, N) = a.shape, b_shape
    assert K == K2, (name, a.shape, b.shape)
    tm, tn, tk = _matmul_tiles(
        M, N, K, a.dtype.itemsize, b.dtype.itemsize, sum(jnp.dtype(d).itemsize for d in out_dtypes),
        sum(e.dtype.itemsize for e in extras))
    assert M % tm == 0 and N % tn == 0 and K % tk == 0, (name, M, N, K, tm, tn, tk)
    nk = K // tk
    if mode == "tn":
        a_spec = pl.BlockSpec((tk, tm), lambda i, j, k: (k, i))
    else:
        a_spec = pl.BlockSpec((tm, tk), lambda i, j, k: (i, k))
    b_block, b_index = ((tn, tk), lambda i, j, k: (j, k)) if mode == "nt" else ((tk, tn), lambda i, j, k: (k, j))
    if b_layer is None:
        b_spec = pl.BlockSpec(b_block, b_index)
    else:
        b_spec = pl.BlockSpec((None,) + b_block, lambda i, j, k: (b_layer,) + b_index(i, j, k))
    dims = {"nn": NN, "nt": NT, "tn": TN}[mode]
    n_ex, n_rex, n_out = len(extras), len(row_extras), len(out_dtypes)

    def body(*refs):
        a_ref, b_ref = refs[:2]
        ex = refs[2:2 + n_ex]
        rex = refs[2 + n_ex:2 + n_ex + n_rex]
        outs = refs[2 + n_ex + n_rex:2 + n_ex + n_rex + n_out]

        def finish(total):
            res = (total,) if epilogue is None else epilogue(total, *[e[...] for e in ex], *[e[...] for e in rex])
            for o, r in zip(outs, res):
                o[...] = r.astype(o.dtype)

        part = _dot(a_ref[...].astype(BF16), b_ref[...].astype(BF16), dims)
        if nk == 1:
            finish(part)
            return
        acc = refs[-1]
        k = pl.program_id(2)

        @pl.when(k == 0)
        def _():
            acc[...] = part

        @pl.when(k > 0)
        def _():
            acc[...] += part

        @pl.when(k == nk - 1)
        def _():
            finish(acc[...])

    in_specs = [a_spec, b_spec]
    in_specs += [pl.BlockSpec((tm, tn), lambda i, j, k: (i, j)) for _ in extras]
    in_specs += [pl.BlockSpec((1, tn), lambda i, j, k: (0, j)) for _ in row_extras]
    out = pl.pallas_call(
        body,
        name=name,
        grid=(M // tm, N // tn, nk),
        in_specs=in_specs,
        out_specs=[pl.BlockSpec((tm, tn), lambda i, j, k: (i, j)) for _ in out_dtypes],
        out_shape=[jax.ShapeDtypeStruct((M, N), dt) for dt in out_dtypes],
        scratch_shapes=[pltpu.VMEM((tm, tn), F32)] if nk > 1 else [],
        compiler_params=_params(("parallel", "parallel", "arbitrary")),
    )(a, b, *extras, *row_extras)
    return out[0] if n_out == 1 else out


ROWWISE_ROW_BYTES = 16 * 1024


def _rowwise(fn, rows, consts, out_defs, sum_widths, name):
    R = rows[0].shape[0]
    per_row = sum(r.shape[1] * r.dtype.itemsize for r in rows) + sum(w * jnp.dtype(dt).itemsize for w, dt in out_defs)
    bm = 512 if per_row <= ROWWISE_ROW_BYTES else 256
    while R % bm:
        bm //= 2
    bm = max(bm, 1)
    n_r, n_c, n_o = len(rows), len(consts), len(out_defs)
    n_s = len(sum_widths)

    def body(*refs):
        r_in = refs[:n_r]
        c_in = refs[n_r:n_r + n_c]
        o_refs = refs[n_r + n_c:n_r + n_c + n_o]
        s_refs = refs[n_r + n_c + n_o:]
        outs, sums = fn([r[...] for r in r_in], [c[...] for c in c_in])
        for o, val in zip(o_refs, outs):
            o[...] = val.astype(o.dtype)
        if n_s:
            @pl.when(pl.program_id(0) == 0)
            def _():
                for s in s_refs:
                    s[...] = jnp.zeros_like(s)

            for s, val in zip(s_refs, sums):
                s[...] += jnp.sum(val, axis=0, keepdims=True)

    in_specs = [pl.BlockSpec((bm, r.shape[1]), lambda i: (i, 0)) for r in rows]
    in_specs += [pl.BlockSpec(c.shape, lambda i: (0, 0)) for c in consts]
    out_specs = [pl.BlockSpec((bm, w), lambda i: (i, 0)) for w, _ in out_defs]
    out_specs += [pl.BlockSpec((1, w), lambda i: (0, 0)) for w in sum_widths]
    out_shape = [jax.ShapeDtypeStruct((R, w), dt) for w, dt in out_defs]
    out_shape += [jax.ShapeDtypeStruct((1, w), F32) for w in sum_widths]
    return pl.pallas_call(
        body,
        name=name,
        grid=(R // bm,),
        in_specs=in_specs,
        out_specs=out_specs,
        out_shape=out_shape,
        compiler_params=_params(("arbitrary",)),
    )(*rows, *consts)


def _rms(x, g):
    r = lax.rsqrt(jnp.mean(x * x, axis=-1, keepdims=True) + EPS)
    return x * r * g


def _rms_bwd(x, g, dy):
    r = lax.rsqrt(jnp.mean(x * x, axis=-1, keepdims=True) + EPS)
    n = x * r
    dn = dy * g
    dx = r * (dn - n * jnp.mean(dn * n, axis=-1, keepdims=True))
    return dx, dy * n


def _rope(x, c, s_up, s_dn, half):
    return x * c + pltpu.roll(x, half, 1) * s_up + pltpu.roll(x, LANES - half, 1) * s_dn


def _rope_tables(positions, lo, d, nope_pass):
    S = positions.shape[0]
    half = d // 2
    inv = 1.0 / (ROPE_THETA ** (jnp.arange(0, d, 2, dtype=F32) / d))
    ang = positions.astype(F32)[:, None] * inv
    cos, sin = jnp.cos(ang), jnp.sin(ang)
    z = lambda n: jnp.zeros((S, n), F32)
    head = jnp.ones((S, lo), F32) if nope_pass else z(lo)
    tail = LANES - lo - d
    c = jnp.concatenate([head, cos, cos, z(tail)], axis=1)
    s_up = jnp.concatenate([z(lo), z(half), sin, z(tail)], axis=1)
    s_dn = jnp.concatenate([z(lo), -sin, z(half), z(tail)], axis=1)
    return c, s_up, s_dn


MLA_FWD_CFG = (2, 1024)
MLA_BWD_CFG = (2, 512)
SB_FWD_CFG = (2, 256)
SB_BWD_CFG = (4, 256)


def _tile_mask(bk, strict):
    row = lax.broadcasted_iota(jnp.int32, (bk, bk), 0)
    col = lax.broadcasted_iota(jnp.int32, (bk, bk), 1)
    return (col < row) if strict else (col <= row)


def _att_layout(S, cfg):
    nch, bk = cfg
    bq = nch * bk
    assert S % bq == 0, (S, cfg)
    rows = [slice(r * bk, (r + 1) * bk) for r in range(nch)]
    q_spec = lambda off=0: pl.BlockSpec((bq, LANES), lambda h, i: (i, off + h))
    kv_spec = lambda off=0: pl.BlockSpec((S, LANES), lambda h, i: (0, off + h))
    return bq, rows, q_spec, kv_spec


def _total(terms):
    terms = list(terms)
    out = terms[0]
    for t in terms[1:]:
        out = out + t
    return out


def _walk(nch, i, step, carry, leftward, alive=None):
    everyone = range(nch)
    if leftward:
        for d in reversed(everyone):
            carry = step(nch * i + d, carry, range(d, nch), {d})
        if alive is None:
            return lax.fori_loop(0, nch * i, lambda t, c: step(nch * i - 1 - t, c, everyone, set()), carry)
        more = lambda tc: jnp.logical_and(tc[0] < nch * i, alive(tc[1]))
        left = lambda tc: (tc[0] + 1, step(nch * i - 1 - tc[0], tc[1], everyone, set()))
        return lax.while_loop(more, left, (jnp.int32(0), carry))[1]
    carry = lax.fori_loop(0, nch * i, lambda kb, c: step(kb, c, everyone, set()), carry)
    for d in everyone:
        carry = step(nch * i + d, carry, range(d, nch), {d})
    return carry


ONES_LANE = MLA_V


def _softmax_attn_fwd(q, k, v, heads, name, q_off=0, k_off=0, v_off=0):
    S = q.shape[0]
    nch, bk = MLA_FWD_CFG
    bq, rows, q_spec, kv_spec = _att_layout(S, MLA_FWD_CFG)

    def body(q_ref, k_ref, v_ref, o_ref, lse_ref):
        i = pl.program_id(1)
        qs = [q_ref[rw, :] for rw in rows]

        def step(kb, cs, active, masked):
            off = pl.multiple_of(kb * bk, bk)
            ks, vs = k_ref[pl.ds(off, bk), :], v_ref[pl.ds(off, bk), :]
            A = list(active)
            s = {r: _dot(qs[r], ks, NT) for r in A}
            s = {r: (jnp.where(_tile_mask(bk, False), s[r], -1e30) if r in masked else s[r]) for r in A}
            m_new = {r: jnp.maximum(cs[r][0], jnp.max(s[r], axis=1, keepdims=True)) for r in A}
            p = {r: jnp.exp(s[r] - m_new[r]) for r in A}
            alpha = {r: jnp.exp(cs[r][0] - m_new[r]) for r in A}
            new = list(cs)
            for r in A:
                new[r] = (m_new[r], alpha[r] * cs[r][1] + _dot(p[r].astype(BF16), vs, NN))
            return tuple(new)

        init = (jnp.full((bk, 1), -1e30, F32), jnp.zeros((bk, LANES), F32))
        cs = _walk(nch, i, step, tuple(init for _ in rows), False)
        for r, (m, acc) in enumerate(cs):
            l = acc[:, ONES_LANE:ONES_LANE + 1]
            o_ref[rows[r], :] = (acc / l).astype(o_ref.dtype)
            lse_ref[rows[r], :] = m + jnp.log(l)

    return pl.pallas_call(
        body,
        name=name,
        grid=(heads, S // bq),
        in_specs=[q_spec(q_off), kv_spec(k_off), kv_spec(v_off)],
        out_specs=[q_spec(), pl.BlockSpec((None, bq, 1), lambda h, i: (h, i, 0))],
        out_shape=[jax.ShapeDtypeStruct((S, heads * LANES), BF16), jax.ShapeDtypeStruct((heads, S, 1), F32)],
        compiler_params=_params(("parallel", "arbitrary")),
    )(q, k, v)


def _softmax_attn_bwd(q, k, v, o, lse, do, heads, scale, name, q_off=0, k_off=0, v_off=0):
    S = q.shape[0]
    nch, bk = MLA_BWD_CFG
    bq, rows, q_spec, kv_spec = _att_layout(S, MLA_BWD_CFG)

    def body(q_ref, k_ref, v_ref, o_ref, lse_ref, do_ref, dq_ref, dk_ref, dv_ref):
        i = pl.program_id(1)

        @pl.when(i == 0)
        def _():
            dk_ref[...] = jnp.zeros_like(dk_ref)
            dv_ref[...] = jnp.zeros_like(dv_ref)

        qs = [q_ref[rw, :] for rw in rows]
        dos = [do_ref[rw, :] for rw in rows]
        lses = [lse_ref[rw, :] for rw in rows]
        deltas = [jnp.sum(dos[r].astype(F32) * o_ref[rows[r], :].astype(F32), axis=1, keepdims=True) for r in range(nch)]

        def step(kb, dqs, active, masked):
            off = pl.multiple_of(kb * bk, bk)
            ks, vs = k_ref[pl.ds(off, bk), :], v_ref[pl.ds(off, bk), :]
            A = list(active)
            s = {r: _dot(qs[r], ks, NT) for r in A}
            s = {r: (jnp.where(_tile_mask(bk, False), s[r], -1e30) if r in masked else s[r]) for r in A}
            p = {r: jnp.exp(s[r] - lses[r]) for r in A}
            dp = {r: _dot(dos[r], vs, NT) for r in A}
            ds = {r: (p[r] * (dp[r] - deltas[r])).astype(BF16) for r in A}
            dv_c = _total(_dot(p[r].astype(BF16), dos[r], TN) for r in A)
            dk_c = _total(_dot(ds[r], qs[r], TN) for r in A)
            dk_ref[pl.ds(off, bk), :] += dk_c
            dv_ref[pl.ds(off, bk), :] += dv_c
            new = list(dqs)
            for r in A:
                new[r] = dqs[r] + _dot(ds[r], ks, NN)
            return tuple(new)

        dqs = _walk(nch, i, step, tuple(jnp.zeros((bk, LANES), F32) for _ in rows), False)
        for r in range(nch):
            dq_ref[rows[r], :] = dqs[r] * scale

    return pl.pallas_call(
        body,
        name=name,
        grid=(heads, S // bq),
        in_specs=[q_spec(q_off), kv_spec(k_off), kv_spec(v_off), q_spec(),
                  pl.BlockSpec((None, bq, 1), lambda h, i: (h, i, 0)), q_spec()],
        out_specs=[q_spec(), kv_spec(), kv_spec()],
        out_shape=[jax.ShapeDtypeStruct((S, heads * LANES), F32)] * 3,
        compiler_params=_params(("parallel", "arbitrary")),
    )(q, k, v, o, lse, do)


def _tri(n, inclusive):
    r = lax.broadcasted_iota(jnp.int32, (n, n), 0)
    c = lax.broadcasted_iota(jnp.int32, (n, n), 1)
    return jnp.where((r >= c) if inclusive else (r > c), 1.0, 0.0).astype(BF16)


def _suffix_sum(x, tri):
    hi = x.astype(BF16)
    lo = (x - hi.astype(F32)).astype(BF16)
    return _dot(hi, tri, NN) + _dot(lo, tri, NN)


def _sb_logs(z):
    lg = jnp.log(1.0 + jnp.exp(-jnp.abs(z)))
    l1m = -(jnp.maximum(z, 0.0) + lg)
    return l1m, l1m + z


SB_SCALE = SB_HEAD_DIM ** -0.5
assert SB_SCALE == 0.125
SB_DEAD = -110.0


def _sb_alive(cs):
    top = cs[0][0]
    for c in cs[1:]:
        top = jnp.maximum(top, c[0])
    return jnp.max(top) > SB_DEAD


def _sb_attn_fwd(qkv, heads, name, q_off, k_off, v_off):
    S = qkv.shape[0]
    nch, bk = SB_FWD_CFG
    bq, rows, q_spec, kv_spec = _att_layout(S, SB_FWD_CFG)

    def body(q_ref, k_ref, v_ref, o_ref):
        i = pl.program_id(1)
        qs = [q_ref[rw, :] * SB_SCALE for rw in rows]
        tri = _tri(bk, False)

        def step(kb, cs, active, masked):
            off = pl.multiple_of(kb * bk, bk)
            ks, vs = k_ref[pl.ds(off, bk), :], v_ref[pl.ds(off, bk), :]
            A = list(active)
            lg = {r: _sb_logs(_dot(qs[r], ks, NT)) for r in A}
            l1m = {r: (jnp.where(_tile_mask(bk, True), lg[r][0], 0.0) if r in masked else lg[r][0]) for r in A}
            suf = {r: _suffix_sum(l1m[r], tri) for r in A}
            ex = {r: lg[r][1] + cs[r][0] + suf[r] for r in A}
            ex = {r: (jnp.where(_tile_mask(bk, True), ex[r], -1e30) if r in masked else ex[r]) for r in A}
            ab = {r: jnp.exp(ex[r]).astype(BF16) for r in A}
            new = list(cs)
            for r in A:
                new[r] = (cs[r][0] + jnp.sum(l1m[r], axis=1, keepdims=True), cs[r][1] + _dot(ab[r], vs, NN))
            return tuple(new)

        init = (jnp.zeros((bk, 1), F32), jnp.zeros((bk, LANES), F32))
        cs = _walk(nch, i, step, tuple(init for _ in rows), True, _sb_alive)
        for r in range(nch):
            o_ref[rows[r], :] = cs[r][1]

    return pl.pallas_call(
        body,
        name=name,
        grid=(heads, S // bq),
        in_specs=[q_spec(q_off), kv_spec(k_off), kv_spec(v_off)],
        out_specs=q_spec(),
        out_shape=jax.ShapeDtypeStruct((S, heads * LANES), F32),
        compiler_params=_params(("parallel", "arbitrary")),
    )(qkv, qkv, qkv)


def _sb_attn_bwd(qkv, o, do, heads, name, q_off, k_off, v_off):
    S = qkv.shape[0]
    nch, bk = SB_BWD_CFG
    bq, rows, q_spec, kv_spec = _att_layout(S, SB_BWD_CFG)

    def body(q_ref, k_ref, v_ref, o_ref, do_ref, dq_ref, dk_out, dv_out, dk_ref, dv_ref):
        i = pl.program_id(1)

        @pl.when(i == 0)
        def _():
            dk_ref[...] = jnp.zeros_like(dk_ref)
            dv_ref[...] = jnp.zeros_like(dv_ref)

        tri = _tri(bk, False)
        qs = [q_ref[rw, :] * SB_SCALE for rw in rows]
        dos = [do_ref[rw, :] for rw in rows]
        deltas = [jnp.sum(dos[r].astype(F32) * o_ref[rows[r], :], axis=1, keepdims=True) for r in range(nch)]

        def step(kb, cs, active, masked):
            off = pl.multiple_of(kb * bk, bk)
            ks, vs = k_ref[pl.ds(off, bk), :], v_ref[pl.ds(off, bk), :]
            A = list(active)
            lg = {r: _sb_logs(_dot(qs[r], ks, NT)) for r in A}
            l1m = {r: (jnp.where(_tile_mask(bk, True), lg[r][0], 0.0) if r in masked else lg[r][0]) for r in A}
            suf = {r: _suffix_sum(l1m[r], tri) for r in A}
            ex = {r: lg[r][1] + cs[r][0] + suf[r] for r in A}
            ex = {r: (jnp.where(_tile_mask(bk, True), ex[r], -1e30) if r in masked else ex[r]) for r in A}
            ab = {r: jnp.exp(ex[r]).astype(BF16) for r in A}
            da = {r: _dot(dos[r], vs, NT) for r in A}
            g = {r: ab[r].astype(F32) * da[r] for r in A}
            gs = {r: _suffix_sum(g[r], tri) for r in A}
            beta = {r: jnp.exp(lg[r][1]) for r in A}
            dz = {r: g[r] - beta[r] * (deltas[r] - cs[r][1] - gs[r]) for r in A}
            dz = {r: (jnp.where(_tile_mask(bk, True), dz[r], 0.0) if r in masked else dz[r]) for r in A}
            dzb = {r: dz[r].astype(BF16) for r in A}
            dv_c = _total(_dot(ab[r], dos[r], TN) for r in A)
            dk_c = _total(_dot(dzb[r], qs[r], TN) for r in A)
            dk_ref[pl.ds(off, bk), :] += dk_c
            dv_ref[pl.ds(off, bk), :] += dv_c
            new = list(cs)
            for r in A:
                new[r] = (cs[r][0] + jnp.sum(l1m[r], axis=1, keepdims=True),
                          cs[r][1] + jnp.sum(g[r], axis=1, keepdims=True), cs[r][2] + _dot(dzb[r], ks, NN))
            return tuple(new)

        zcol = jnp.zeros((bk, 1), F32)
        init = (zcol, zcol, jnp.zeros((bk, LANES), F32))
        cs = _walk(nch, i, step, tuple(init for _ in rows), True, _sb_alive)
        for r in range(nch):
            dq_ref[rows[r], :] = (cs[r][2] * SB_SCALE).astype(dq_ref.dtype)

        @pl.when(i == S // bq - 1)
        def _():
            dk_out[...] = dk_ref[...].astype(dk_out.dtype)
            dv_out[...] = dv_ref[...].astype(dv_out.dtype)

    return pl.pallas_call(
        body,
        name=name,
        grid=(heads, S // bq),
        in_specs=[q_spec(q_off), kv_spec(k_off), kv_spec(v_off), q_spec(), q_spec()],
        out_specs=[q_spec(), kv_spec(), kv_spec()],
        out_shape=[jax.ShapeDtypeStruct((S, heads * LANES), BF16)] * 3,
        scratch_shapes=[pltpu.VMEM((S, LANES), F32), pltpu.VMEM((S, LANES), F32)],
        compiler_params=_params(("parallel", "arbitrary")),
    )(qkv, qkv, qkv, o, do)


SWA_BLK = 128
SWA_GROUP = SWA_HEADS // SWA_KV_HEADS


SWA_NB = 4
SWA_ROWS = SWA_NB * SWA_BLK


def _swa_band_mask(first):
    row = lax.broadcasted_iota(jnp.int32, (SWA_BLK, 2 * SWA_BLK), 0)
    col = lax.broadcasted_iota(jnp.int32, (SWA_BLK, 2 * SWA_BLK), 1)
    return (col > row) & (col <= row + SWA_WINDOW) & (jnp.logical_not(first) | (col >= SWA_BLK))


def _swa_in_specs(v_off):
    gw = SWA_GROUP * LANES
    before = lambda h, n: (jnp.maximum(SWA_NB * n - 1, 0), h)
    return [
        pl.BlockSpec((SWA_ROWS, gw), lambda h, n: (n, h)),
        pl.BlockSpec((SWA_BLK, LANES), before),
        pl.BlockSpec((SWA_ROWS, LANES), lambda h, n: (n, h)),
        pl.BlockSpec((SWA_BLK, LANES), lambda h, n: (jnp.maximum(SWA_NB * n - 1, 0), v_off + h)),
        pl.BlockSpec((SWA_ROWS, LANES), lambda h, n: (n, v_off + h)),
        pl.BlockSpec((1, gw), lambda h, n: (0, h)),
    ]


def _swa_bands(n, kp_ref, kc_ref, vp_ref, vc_ref):
    k_all = jnp.concatenate([kp_ref[...], kc_ref[...]], axis=0)
    v_all = jnp.concatenate([vp_ref[...], vc_ref[...]], axis=0)
    bands = []
    for j in range(SWA_NB):
        rows = slice(j * SWA_BLK, (j + 2) * SWA_BLK)
        bands.append((k_all[rows], v_all[rows], _swa_band_mask((n == 0) if j == 0 else False)))
    return bands


def _swa_fwd(q, k, v, v_off, sink_b, name):
    S = q.shape[0]
    assert S % SWA_ROWS == 0
    scale = SWA_HEAD_DIM ** -0.5
    gw = SWA_GROUP * LANES

    def body(q_ref, kp_ref, kc_ref, vp_ref, vc_ref, sink_ref, o_ref, lse_ref):
        n = pl.program_id(1)
        bands = _swa_bands(n, kp_ref, kc_ref, vp_ref, vc_ref)
        P = [(j, g) for j in range(SWA_NB) for g in range(SWA_GROUP)]
        rows = lambda j: slice(j * SWA_BLK, (j + 1) * SWA_BLK)
        lanes = lambda g: slice(g * LANES, (g + 1) * LANES)
        sk = {g: sink_ref[:, g * LANES:g * LANES + 1] for g in range(SWA_GROUP)}
        s = {(j, g): jnp.where(bands[j][2], _dot(q_ref[rows(j), lanes(g)], bands[j][0], NT) * scale, -1e30) for j, g in P}
        m = {(j, g): jnp.maximum(jnp.max(s[j, g], axis=1, keepdims=True), sk[g]) for j, g in P}
        p = {(j, g): jnp.exp(s[j, g] - m[j, g]) for j, g in P}
        den = {(j, g): jnp.sum(p[j, g], axis=1, keepdims=True) + jnp.exp(sk[g] - m[j, g]) for j, g in P}
        for j, g in P:
            o_ref[rows(j), lanes(g)] = _dot((p[j, g] / den[j, g]).astype(BF16), bands[j][1], NN).astype(o_ref.dtype)
            lse_ref[g, rows(j), :] = m[j, g] + jnp.log(den[j, g])

    return pl.pallas_call(
        body,
        name=name,
        grid=(SWA_KV_HEADS, S // SWA_ROWS),
        in_specs=_swa_in_specs(v_off),
        out_specs=[
            pl.BlockSpec((SWA_ROWS, gw), lambda h, n: (n, h)),
            pl.BlockSpec((SWA_GROUP, SWA_ROWS, 1), lambda h, n: (h, n, 0)),
        ],
        out_shape=[jax.ShapeDtypeStruct((S, SWA_HEADS * LANES), BF16), jax.ShapeDtypeStruct((SWA_HEADS, S, 1), F32)],
        compiler_params=_params(("parallel", "arbitrary")),
    )(q, k, k, v, v, sink_b)


def _swa_bwd(q, k, v, v_off, sink_b, o, lse, do, name):
    S = q.shape[0]
    assert S % SWA_ROWS == 0
    scale = SWA_HEAD_DIM ** -0.5
    gw = SWA_GROUP * LANES

    def body(q_ref, kp_ref, kc_ref, vp_ref, vc_ref, sink_ref, o_ref, lse_ref, do_ref, dq_ref, dk_ref, dv_ref, dsink_ref):
        n = pl.program_id(1)

        @pl.when(n == 0)
        def _():
            dk_ref[...] = jnp.zeros_like(dk_ref)
            dv_ref[...] = jnp.zeros_like(dv_ref)
            dsink_ref[...] = jnp.zeros_like(dsink_ref)

        bands = _swa_bands(n, kp_ref, kc_ref, vp_ref, vc_ref)
        P = [(j, g) for j in range(SWA_NB) for g in range(SWA_GROUP)]
        rows = lambda j: slice(j * SWA_BLK, (j + 1) * SWA_BLK)
        lanes = lambda g: slice(g * LANES, (g + 1) * LANES)
        qs = {(j, g): q_ref[rows(j), lanes(g)] for j, g in P}
        dos = {(j, g): do_ref[rows(j), lanes(g)] for j, g in P}
        lses = {(j, g): lse_ref[g, rows(j), :] for j, g in P}
        delta = {(j, g): jnp.sum(dos[j, g].astype(F32) * o_ref[rows(j), lanes(g)].astype(F32), axis=1, keepdims=True)
                 for j, g in P}
        s = {(j, g): jnp.where(bands[j][2], _dot(qs[j, g], bands[j][0], NT) * scale, -1e30) for j, g in P}
        p = {(j, g): jnp.exp(s[j, g] - lses[j, g]) for j, g in P}
        dp = {(j, g): _dot(dos[j, g], bands[j][1], NT) for j, g in P}
        ds = {(j, g): (p[j, g] * (dp[j, g] - delta[j, g]) * scale).astype(BF16) for j, g in P}
        for j, g in P:
            dq_ref[rows(j), lanes(g)] = _dot(ds[j, g], bands[j][0], NN)
        for g in range(SWA_GROUP):
            p_sink = [jnp.exp(sink_ref[:, g * LANES:g * LANES + 1] - lses[j, g]) * delta[j, g] for j in range(SWA_NB)]
            dsink_ref[:, lanes(g)] += jnp.zeros((1, LANES), F32) - jnp.sum(_total(p_sink), axis=0, keepdims=True)
        dkb = [_total(_dot(ds[j, g], qs[j, g], TN) for g in range(SWA_GROUP)) for j in range(SWA_NB)]
        dvb = [_total(_dot(p[j, g].astype(BF16), dos[j, g], TN) for g in range(SWA_GROUP)) for j in range(SWA_NB)]
        base = pl.multiple_of(n * SWA_ROWS, SWA_ROWS)
        for j in range(SWA_NB):
            own = pl.ds(base + j * SWA_BLK, SWA_BLK)
            after = j + 1 < SWA_NB
            dk_ref[own, :] += dkb[j][SWA_BLK:] + dkb[j + 1][:SWA_BLK] if after else dkb[j][SWA_BLK:]
            dv_ref[own, :] += dvb[j][SWA_BLK:] + dvb[j + 1][:SWA_BLK] if after else dvb[j][SWA_BLK:]

        @pl.when(n > 0)
        def _():
            before = pl.ds(pl.multiple_of(n * SWA_ROWS - SWA_BLK, SWA_BLK), SWA_BLK)
            dk_ref[before, :] += dkb[0][:SWA_BLK]
            dv_ref[before, :] += dvb[0][:SWA_BLK]

    return pl.pallas_call(
        body,
        name=name,
        grid=(SWA_KV_HEADS, S // SWA_ROWS),
        in_specs=_swa_in_specs(v_off) + [
            pl.BlockSpec((SWA_ROWS, gw), lambda h, n: (n, h)),
            pl.BlockSpec((SWA_GROUP, SWA_ROWS, 1), lambda h, n: (h, n, 0)),
            pl.BlockSpec((SWA_ROWS, gw), lambda h, n: (n, h)),
        ],
        out_specs=[
            pl.BlockSpec((SWA_ROWS, gw), lambda h, n: (n, h)),
            pl.BlockSpec((S, LANES), lambda h, n: (0, h)),
            pl.BlockSpec((S, LANES), lambda h, n: (0, h)),
            pl.BlockSpec((1, gw), lambda h, n: (0, h)),
        ],
        out_shape=[
            jax.ShapeDtypeStruct((S, SWA_HEADS * LANES), F32),
            jax.ShapeDtypeStruct((S, SWA_KV_HEADS * LANES), F32),
            jax.ShapeDtypeStruct((S, SWA_KV_HEADS * LANES), F32),
            jax.ShapeDtypeStruct((1, SWA_HEADS * LANES), F32),
        ],
        compiler_params=_params(("parallel", "arbitrary")),
    )(q, k, k, v, v, sink_b, o, lse, do)


def _pad_last(t, width):
    return jnp.pad(t, [(0, 0)] * (t.ndim - 1) + [(0, width - t.shape[-1])])


def _pad_cols(w, heads, real):
    lead = w.shape[:-1]
    return _pad_last(w.reshape(*lead, heads, real), LANES).reshape(*lead, heads * LANES)


def _unpad_cols(g, heads, real):
    lead = g.shape[:-1]
    return g.reshape(*lead, heads, LANES)[..., :real].reshape(*lead, heads * real)


def _pad_rows(w, heads, real):
    lead, n = w.shape[:-2], w.shape[-1]
    w = w.reshape(*lead, heads, real, n)
    return jnp.pad(w, [(0, 0)] * (w.ndim - 2) + [(0, LANES - real), (0, 0)]).reshape(*lead, heads * LANES, n)


def _unpad_rows(g, heads, real):
    lead, n = g.shape[:-2], g.shape[-1]
    return g.reshape(*lead, heads, LANES, n)[..., :real, :].reshape(*lead, heads * real, n)


def _w_in_internal(w_in):
    c_q, c_kv, k_r, q_swa, k_swa, v_swa, q_sb, k_sb, v_sb, gate = jnp.split(w_in, SPLIT_POINTS, axis=-1)
    k_r = jnp.pad(k_r, [(0, 0)] * (k_r.ndim - 1) + [(MLA_NOPE, LANES - MLA_NOPE - MLA_ROPE)])
    w1 = jnp.concatenate([c_q, c_kv, k_r, _pad_cols(q_swa, 8, 64), _pad_cols(k_swa, 2, 64)], axis=-1)
    w2 = [_pad_cols(v_swa, 2, 64), _pad_cols(q_sb, 8, 64), _pad_cols(k_sb, 8, 64), _pad_cols(v_sb, 8, 64)]
    return w1, w2, gate


def _w_in_reference(g1, g2, g3):
    c_q, c_kv, k_r, q_swa, k_swa = jnp.split(g1, [256, 384, 512, 1536], axis=-1)
    v_swa, q_sb, k_sb, v_sb = g2
    return jnp.concatenate([
        c_q, c_kv, k_r[..., MLA_NOPE:MLA_NOPE + MLA_ROPE], _unpad_cols(q_swa, 8, 64), _unpad_cols(k_swa, 2, 64),
        _unpad_cols(v_swa, 2, 64), _unpad_cols(q_sb, 8, 64), _unpad_cols(k_sb, 8, 64), _unpad_cols(v_sb, 8, 64),
        g3], axis=-1)


def _w_ukv_internal(w):
    lead = w.shape[:-1]
    w3 = w.reshape(*lead, MLA_HEADS, MLA_NOPE + MLA_V)
    pad = lambda t: _pad_last(t, LANES).reshape(*lead, MLA_HEADS * LANES)
    return pad(w3[..., :MLA_NOPE]), pad(w3[..., MLA_NOPE:])


def _w_ukv_reference(gk, gv):
    lead = gk.shape[:-1]
    gk = gk.reshape(*lead, MLA_HEADS, LANES)[..., :MLA_NOPE]
    gv = gv.reshape(*lead, MLA_HEADS, LANES)[..., :MLA_V]
    return jnp.concatenate([gk, gv], axis=-1).reshape(*lead, MLA_HEADS * (MLA_NOPE + MLA_V))


def _layer_fwd(x, w, tabs):
    mla_tab, swa_tab = tabs
    sv = {"x": x}

    def f_norm(rows, consts):
        return [_rms(rows[0], consts[0])], []

    (h,) = _rowwise(f_norm, [x], [w["g_mix_pre"]], [(D_MODEL, BF16)], [], "norm_mix_pre")
    p1 = _matmul(h, w["w_in1"], "nn", [F32], "proj_lat")
    p2 = _matmul(h, w["w_in2"], "nn", [BF16], "proj_qkv")
    gates = _matmul(h, w["w_in3"], "nn", [BF16], "proj_gate",
                    epilogue=lambda acc, b: (1.0 / (1.0 + jnp.exp(-(acc + b))),), row_extras=[w["b_gate"]])

    def f_prep(rows, consts):
        t = rows[0]
        gq, gkv = consts[0], consts[1]
        mc, mu, md = rows[1], rows[2], rows[3]
        sc, su, sd = rows[4], rows[5], rows[6]
        cq_n = _rms(t[:, 0:256], gq)
        ckv_n = _rms(t[:, 256:384], gkv)
        kr = _rope(t[:, 384:512], mc, mu, md, MLA_ROPE // 2)
        qs = [_rope(t[:, 512 + j * LANES:512 + (j + 1) * LANES], sc, su, sd, SWA_HEAD_DIM // 2) for j in range(8)]
        ks = [_rope(t[:, 1536 + j * LANES:1536 + (j + 1) * LANES], sc, su, sd, SWA_HEAD_DIM // 2) for j in range(2)]
        return [cq_n, ckv_n, kr, jnp.concatenate(qs, axis=1), jnp.concatenate(ks, axis=1)], []

    cq_n, ckv_n, kr, q_swa, k_swa = _rowwise(
        f_prep, [p1, *mla_tab["k"], *swa_tab["f"]], [w["g_q_lat"], w["g_kv_lat"]],
        [(256, BF16), (128, BF16), (LANES, F32), (1024, BF16), (256, BF16)], [], "lat_prep")

    q_lat = _matmul(cq_n, w["w_uq"], "nn", [F32], "mla_q_up")
    k_lat = _matmul(ckv_n, w["w_ukv_k"], "nn", [F32], "mla_k_up")
    def ones_lane(acc):
        lane = lax.broadcasted_iota(jnp.int32, acc.shape, 1) % LANES
        return (jnp.where(lane == ONES_LANE, 1.0, acc),)

    v_mla = _matmul(ckv_n, w["w_ukv_v"], "nn", [BF16], "mla_v_up", epilogue=ones_lane)
    mla_scale = (MLA_NOPE + MLA_ROPE) ** -0.5

    def f_mla_prep(rows, consts):
        ql, kl, krr, mc, mu, md = rows
        qs = [_rope(ql[:, j * LANES:(j + 1) * LANES], mc, mu, md, MLA_ROPE // 2) * mla_scale for j in range(8)]
        ks = [kl[:, j * LANES:(j + 1) * LANES] + krr for j in range(8)]
        return [jnp.concatenate(qs, axis=1), jnp.concatenate(ks, axis=1)], []

    q_mla, k_mla = _rowwise(f_mla_prep, [q_lat, k_lat, kr, *mla_tab["q"]], [], [(1024, BF16), (1024, BF16)], [], "mla_prep")

    o_mla, lse_mla = _softmax_attn_fwd(q_mla, k_mla, v_mla, MLA_HEADS, "mla_fwd")
    o_swa, lse_swa = _swa_fwd(q_swa, k_swa, p2, 0, w["sink_b"], "swa_fwd")
    o_sb = _sb_attn_fwd(p2, SB_HEADS, "sb_fwd", 2, 10, 18)

    oa = _matmul(o_mla, w["w_o_mla"], "nn", [BF16], "o_proj_mla")
    ob = _matmul(o_swa, w["w_o_swa"], "nn", [BF16], "o_proj_swa")
    oc = _matmul(o_sb, w["w_o_sb"], "nn", [BF16], "o_proj_sb")

    def f_mix(rows, consts):
        a, b, c, g = rows
        g = g.astype(F32)
        return [g[:, 0:1024] * a + g[:, 1024:2048] * b + g[:, 2048:3072] * c], []

    (mixed,) = _rowwise(f_mix, [oa, ob, oc, gates], [], [(D_MODEL, BF16)], [], "gate_mix")
    y = _matmul(mixed, w["w_out"], "nn", [F32], "out_proj")

    def f_res_norm(rows, consts):
        return [rows[0] + _rms(rows[1], consts[0])], []

    (x1,) = _rowwise(f_res_norm, [x, y], [w["g_mix_post"]], [(D_MODEL, F32)], [], "res_norm_mix")
    (h2,) = _rowwise(f_norm, [x1], [w["g_mlp_pre"]], [(D_MODEL, BF16)], [], "norm_mlp_pre")

    def relu2(acc):
        r = jnp.maximum(acc, 0.0)
        return acc, r * r

    up, u = _matmul(h2, w["w_up"], "nn", [BF16, BF16], "mlp_up", epilogue=relu2)
    zd = _matmul(u, w["w_down"], "nn", [F32], "mlp_down")
    (x2,) = _rowwise(f_res_norm, [x1, zd], [w["g_mlp_post"]], [(D_MODEL, F32)], [], "res_norm_mlp")

    sv.update(h=h, p1=p1, p2=p2, gates=gates, cq_n=cq_n, ckv_n=ckv_n, q_swa=q_swa, k_swa=k_swa, q_mla=q_mla,
              k_mla=k_mla, v_mla=v_mla, o_mla=o_mla, lse_mla=lse_mla, o_swa=o_swa, lse_swa=lse_swa, o_sb=o_sb,
              oa=oa, ob=ob, oc=oc, mixed=mixed, y=y, x1=x1, h2=h2, up=up, u=u, zd=zd)
    return x2, sv


def _layer_bwd(dx2, w, sv, tabs):
    mla_tab, swa_tab = tabs
    gr = {}

    def f_norm_bwd(rows, consts):
        dx, dg = _rms_bwd(rows[0], consts[0], rows[1])
        return [dx], [dg]

    def f_norm_bwd_res(rows, consts):
        dx, dg = _rms_bwd(rows[0], consts[0], rows[1])
        return [rows[2] + dx], [dg]

    dzd, gr["g_mlp_post"] = _rowwise(f_norm_bwd, [sv["zd"], dx2], [w["g_mlp_post"]], [(D_MODEL, BF16)], [D_MODEL], "b_norm_mlp_post")
    gr["w_down"] = _matmul(sv["u"], dzd, "tn", [BF16], "b_w_down")
    dup = _matmul(dzd, w["w_down"], "nt", [BF16], "b_mlp_down",
                  epilogue=lambda acc, up: (acc * 2.0 * jnp.maximum(up.astype(F32), 0.0),), extras=[sv["up"]])
    gr["w_up"] = _matmul(sv["h2"], dup, "tn", [BF16], "b_w_up")
    dh2 = _matmul(dup, w["w_up"], "nt", [F32], "b_mlp_up")
    dx1, gr["g_mlp_pre"] = _rowwise(f_norm_bwd_res, [sv["x1"], dh2, dx2], [w["g_mlp_pre"]], [(D_MODEL, F32)], [D_MODEL], "b_norm_mlp_pre")

    dy, gr["g_mix_post"] = _rowwise(f_norm_bwd, [sv["y"], dx1], [w["g_mix_post"]], [(D_MODEL, BF16)], [D_MODEL], "b_norm_mix_post")
    gr["w_out"] = _matmul(sv["mixed"], dy, "tn", [BF16], "b_w_out")
    dmixed = _matmul(dy, w["w_out"], "nt", [F32], "b_out_proj")

    def f_mix_bwd(rows, consts):
        dm, a, b, c, g = rows
        g = g.astype(F32)
        outs, dls = [], []
        for j, o in enumerate((a, b, c)):
            gj = g[:, j * D_MODEL:(j + 1) * D_MODEL]
            outs.append(dm * gj)
            dls.append(dm * o * gj * (1.0 - gj))
        dl = jnp.concatenate(dls, axis=1)
        return outs + [dl], [dl]

    doa, dob, doc, dlogit, gr["b_gate"] = _rowwise(
        f_mix_bwd, [dmixed, sv["oa"], sv["ob"], sv["oc"], sv["gates"]], [],
        [(D_MODEL, BF16)] * 3 + [(P3_W, BF16)], [P3_W], "b_gate_mix")

    gr["w_o_mla"] = _matmul(sv["o_mla"], doa, "tn", [BF16], "b_w_o_mla")
    gr["w_o_swa"] = _matmul(sv["o_swa"], dob, "tn", [BF16], "b_w_o_swa")
    gr["w_o_sb"] = _matmul(sv["o_sb"], doc, "tn", [BF16], "b_w_o_sb")
    do_mla = _matmul(doa, w["w_o_mla"], "nt", [BF16], "b_o_proj_mla")
    do_swa = _matmul(dob, w["w_o_swa"], "nt", [BF16], "b_o_proj_swa")
    do_sb = _matmul(doc, w["w_o_sb"], "nt", [BF16], "b_o_proj_sb")

    dq_sb, dk_sb, dv_sb = _sb_attn_bwd(sv["p2"], sv["o_sb"], do_sb, SB_HEADS, "sb_bwd", 2, 10, 18)
    dq_swa, dk_swa, dv_swa, dsink = _swa_bwd(sv["q_swa"], sv["k_swa"], sv["p2"], 0, w["sink_b"], sv["o_swa"],
                                             sv["lse_swa"], do_swa, "swa_bwd")
    gr["swa_sinks"] = dsink.reshape(SWA_HEADS, LANES)[:, 0]
    dq_mla, dk_mla, dv_mla = _softmax_attn_bwd(sv["q_mla"], sv["k_mla"], sv["v_mla"], sv["o_mla"], sv["lse_mla"], do_mla,
                                               MLA_HEADS, (MLA_NOPE + MLA_ROPE) ** -0.5, "mla_bwd")

    def f_mla_post(rows, consts):
        dq, dk, qc, qu, qd, kc, ku, kd = rows
        dqs = [_rope(dq[:, j * LANES:(j + 1) * LANES], qc, qu, qd, MLA_ROPE // 2) for j in range(8)]
        dkr = dk[:, 0:LANES]
        for j in range(1, 8):
            dkr = dkr + dk[:, j * LANES:(j + 1) * LANES]
        return [jnp.concatenate(dqs, axis=1), _rope(dkr, kc, ku, kd, MLA_ROPE // 2)], []

    dq_lat, dkr = _rowwise(f_mla_post, [dq_mla, dk_mla, *mla_tab["q_inv"], *mla_tab["k_inv"]], [],
                           [(1024, BF16), (LANES, F32)], [], "b_mla_post")
    gr["w_uq"] = _matmul(sv["cq_n"], dq_lat, "tn", [BF16], "b_w_uq")
    gr["w_ukv_k"] = _matmul(sv["ckv_n"], dk_mla, "tn", [BF16], "b_w_ukv_k")
    gr["w_ukv_v"] = _matmul(sv["ckv_n"], dv_mla, "tn", [BF16], "b_w_ukv_v")
    dcq_n = _matmul(dq_lat, w["w_uq"], "nt", [F32], "b_mla_q_up")
    dckv_a = _matmul(dk_mla, w["w_ukv_k"], "nt", [F32], "b_mla_k_up")
    dckv_b = _matmul(dv_mla, w["w_ukv_v"], "nt", [F32], "b_mla_v_up")

    def f_prep_bwd(rows, consts):
        t, dcq, dca, dcb, dkr_, dqs, dks, sc, su, sd = rows
        gq, gkv = consts
        dc_q, dgq = _rms_bwd(t[:, 0:256], gq, dcq)
        dc_kv, dgkv = _rms_bwd(t[:, 256:384], gkv, dca + dcb)
        q_parts = [_rope(dqs[:, j * LANES:(j + 1) * LANES], sc, su, sd, SWA_HEAD_DIM // 2) for j in range(8)]
        k_parts = [_rope(dks[:, j * LANES:(j + 1) * LANES], sc, su, sd, SWA_HEAD_DIM // 2) for j in range(2)]
        return [jnp.concatenate([dc_q, dc_kv, dkr_] + q_parts + k_parts, axis=1)], [dgq, dgkv]

    dp1, gr["g_q_lat"], gr["g_kv_lat"] = _rowwise(
        f_prep_bwd, [sv["p1"], dcq_n, dckv_a, dckv_b, dkr, dq_swa, dk_swa, *swa_tab["inv"]], [w["g_q_lat"], w["g_kv_lat"]],
        [(P1_W, BF16)], [256, 128], "b_lat_prep")

    gr["w_in1"] = _matmul(sv["h"], dp1, "tn", [BF16], "b_w_in_lat")
    dh = _matmul(dp1, w["w_in1"], "nt", [F32], "b_proj_lat")
    gr["w_in2"] = []
    add_prev = lambda acc, prev: (acc + prev,)
    for piece, wp, tag in zip((dv_swa, dq_sb, dk_sb, dv_sb), w["w_in2_parts"], ("vswa", "qsb", "ksb", "vsb")):
        gr["w_in2"].append(_matmul(sv["h"], piece, "tn", [BF16], "b_w_in_" + tag))
        dh = _matmul(piece, wp, "nt", [F32], "b_proj_" + tag, epilogue=add_prev, extras=[dh])
    gr["w_in3"] = _matmul(sv["h"], dlogit, "tn", [BF16], "b_w_in_gate")
    dh = _matmul(dlogit, w["w_in3"], "nt", [F32], "b_proj_gate", epilogue=add_prev, extras=[dh])
    dx, gr["g_mix_pre"] = _rowwise(f_norm_bwd_res, [sv["x"], dh, dx1], [w["g_mix_pre"]], [(D_MODEL, F32)], [D_MODEL], "b_norm_mix_pre")
    return dx, gr


def _local_step(x, positions, loss_target, full):
    mc, mu, md = _rope_tables(positions, MLA_NOPE, MLA_ROPE, True)
    kc, ku, kd = _rope_tables(positions, MLA_NOPE, MLA_ROPE, False)
    sc, su, sd = _rope_tables(positions, 0, SWA_HEAD_DIM, False)
    mla_tab = {"q": (mc, mu, md), "k": (kc, ku, kd), "q_inv": (mc, -mu, -md), "k_inv": (kc, -ku, -kd)}
    swa_tab = {"f": (sc, su, sd), "inv": (sc, -su, -sd)}
    tabs = (mla_tab, swa_tab)

    big = {n: full[n].astype(BF16) for n in SHARDED}
    w1, w2, w3 = _w_in_internal(big["w_in"])
    uk, uv = _w_ukv_internal(big["w_ukv"])
    stacks = {
        "w_in1": w1, "w_in2": jnp.concatenate(w2, axis=-1), "w_in3": w3,
        "w_uq": _pad_cols(big["w_uq"], MLA_HEADS, MLA_NOPE + MLA_ROPE), "w_ukv_k": uk, "w_ukv_v": uv,
        "w_o_mla": _pad_rows(big["w_o_mla"], 8, 64), "w_o_swa": _pad_rows(big["w_o_swa"], 8, 64),
        "w_o_sb": _pad_rows(big["w_o_sb"], 8, 64), "w_out": big["w_out"], "w_up": big["w_up"], "w_down": big["w_down"],
    }
    layers = []
    for l in range(DEPTH):
        layers.append({
            **{n: (t, l) for n, t in stacks.items()}, "w_in2_parts": [(t, l) for t in w2],
            "g_mix_pre": full["g_mix_pre"][l][None], "b_gate": full["b_gate"][l][None],
            "g_q_lat": full["g_q_lat"][l][None], "g_kv_lat": full["g_kv_lat"][l][None],
            "g_mix_post": full["g_mix_post"][l][None], "g_mlp_pre": full["g_mlp_pre"][l][None],
            "g_mlp_post": full["g_mlp_post"][l][None],
            "sink_b": jnp.repeat(full["swa_sinks"][l], LANES)[None],
        })

    saved = []
    h = x
    for l in range(DEPTH):
        h, sv = _layer_fwd(h, layers[l], tabs)
        saved.append(sv)

    def f_loss(rows, consts):
        err = rows[0] - rows[1]
        return [err * (1.0 / D_MODEL)], [jnp.sum(err * err, axis=1, keepdims=True)]

    dy, sq = _rowwise(f_loss, [h, loss_target], [], [(D_MODEL, F32)], [1], "loss_head")
    loss_part = sq * (0.5 / D_MODEL)

    grs = [None] * DEPTH
    d = dy
    for l in reversed(range(DEPTH)):
        d, grs[l] = _layer_bwd(d, layers[l], saved[l], tabs)
    st = lambda pick: jnp.stack([pick(grs[l]) for l in range(DEPTH)])
    vec = lambda n: st(lambda gr: gr[n][0] if gr[n].ndim == 2 else gr[n])
    stacked = {n: vec(n) for n in SMALL}
    stacked.update({n: st(lambda gr: gr[n]) for n in ("w_out", "w_up", "w_down")})
    stacked["w_in"] = _w_in_reference(st(lambda gr: gr["w_in1"]), [st(lambda gr: gr["w_in2"][p]) for p in range(4)],
                                      st(lambda gr: gr["w_in3"]))
    stacked["w_uq"] = _unpad_cols(st(lambda gr: gr["w_uq"]), MLA_HEADS, MLA_NOPE + MLA_ROPE)
    stacked["w_ukv"] = _w_ukv_reference(st(lambda gr: gr["w_ukv_k"]), st(lambda gr: gr["w_ukv_v"]))
    for n in ("w_o_mla", "w_o_swa", "w_o_sb"):
        stacked[n] = _unpad_rows(st(lambda gr: gr[n]), 8, 64)
    return loss_part, d, stacked


def _lane_padded(width):
    return max(width, LANES)


def _rows_of(a, dtype):
    extra = _lane_padded(a.shape[-1]) - a.shape[-1]
    if extra:
        a = jnp.pad(a, [(0, 0)] * (a.ndim - 1) + [(0, extra)])
    return a.astype(dtype).reshape(-1, LANES)


def _pack(shards, small, dtype):
    parts = [_rows_of(shards[n], dtype) for n in SHARDED]
    if small is not None:
        parts += [_rows_of(small[n], dtype) for n in SMALL]
    slab = jnp.concatenate(parts, axis=0)
    pad = (-slab.shape[0]) % SLAB_ROW_ALIGN
    return jnp.pad(slab, ((0, pad), (0, 0)))


def _unpack(slab, shard_shapes, small_shapes):
    out, r = {}, 0
    shapes = [(n, shard_shapes[n]) for n in SHARDED]
    if small_shapes is not None:
        shapes += [(n, small_shapes[n]) for n in SMALL]
    for n, shape in shapes:
        wide = shape[:-1] + (_lane_padded(shape[-1]),)
        rows = int(np.prod(wide)) // LANES
        out[n] = slab[r:r + rows].reshape(wide)[..., :shape[-1]]
        r += rows
    return out


def _chip_exchange(src, name):
    rows = src.shape[-2]

    def body(src_ref, out_ref, send_sems, recv_sems):
        x, y, c = lax.axis_index("x"), lax.axis_index("y"), lax.axis_index("c")
        me = 2 * x + y
        chips = [(1 - x, y), (x, 1 - y), (1 - x, 1 - y)]
        sends = []
        for k, (cx, cy) in enumerate(chips):
            cp = pltpu.make_async_remote_copy(
                src_ref=src_ref.at[2 * cx + cy], dst_ref=out_ref.at[me], send_sem=send_sems.at[k],
                recv_sem=recv_sems.at[k], device_id=(cx, cy, c), device_id_type=pl.DeviceIdType.MESH)
            cp.start()
            sends.append(cp)
        for k, (cx, cy) in enumerate(chips):
            pltpu.make_async_remote_copy(
                src_ref=src_ref.at[me], dst_ref=out_ref.at[2 * cx + cy], send_sem=send_sems.at[k],
                recv_sem=recv_sems.at[k], device_id=(cx, cy, c), device_id_type=pl.DeviceIdType.MESH).wait_recv()
        for cp in sends:
            cp.wait_send()

    return pl.pallas_call(
        body,
        name=name,
        in_specs=[pl.BlockSpec(memory_space=pl.ANY)],
        out_specs=pl.BlockSpec(memory_space=pl.ANY),
        out_shape=jax.ShapeDtypeStruct((N_CHIPS, rows, LANES), src.dtype),
        scratch_shapes=[pltpu.SemaphoreType.DMA((3,)), pltpu.SemaphoreType.DMA((3,))],
    )(src)


def _half_rows(c, half):
    return pl.ds(pl.multiple_of(c * half, SLAB_ROW_ALIGN // 2), half)


def _gather_weights(src, name):
    rows = src.shape[0]
    half = rows // 2

    def body(src_ref, out_ref, send_sems, recv_sems):
        x, y, c = lax.axis_index("x"), lax.axis_index("y"), lax.axis_index("c")
        me = 2 * x + y
        chips = [(1 - x, y), (x, 1 - y), (1 - x, 1 - y)]

        def copy(k, src_view, slab, part, to):
            return pltpu.make_async_remote_copy(
                src_ref=src_view, dst_ref=out_ref.at[slab, _half_rows(part, half), :], send_sem=send_sems.at[k],
                recv_sem=recv_sems.at[k], device_id=to, device_id_type=pl.DeviceIdType.MESH)

        sends = [copy(k, src_ref.at[_half_rows(c, half), :], me, c, (cx, cy, c)) for k, (cx, cy) in enumerate(chips)]
        for cp in sends:
            cp.start()
        for k, (cx, cy) in enumerate(chips):
            j = 2 * cx + cy
            landed = out_ref.at[j, _half_rows(c, half), :]
            copy(k, landed, j, c, (cx, cy, c)).wait_recv()
            fwd = copy(3 + k, landed, j, c, (x, y, 1 - c))
            fwd.start()
            sends.append(fwd)
        for k, (cx, cy) in enumerate(chips):
            j = 2 * cx + cy
            copy(3 + k, out_ref.at[j, _half_rows(1 - c, half), :], j, 1 - c, (x, y, 1 - c)).wait_recv()
        for cp in sends:
            cp.wait_send()

    return pl.pallas_call(
        body,
        name=name,
        in_specs=[pl.BlockSpec(memory_space=pl.ANY)],
        out_specs=pl.BlockSpec(memory_space=pl.ANY),
        out_shape=jax.ShapeDtypeStruct((N_CHIPS, rows, LANES), src.dtype),
        scratch_shapes=[pltpu.SemaphoreType.DMA((6,)), pltpu.SemaphoreType.DMA((6,))],
    )(src)


def _sibling_halves(src, name):
    n, rows, _ = src.shape
    half = rows // 2

    def body(src_ref, out_ref, send_sem, recv_sem):
        x, y, c = lax.axis_index("x"), lax.axis_index("y"), lax.axis_index("c")
        cp = pltpu.make_async_remote_copy(
            src_ref=src_ref.at[:, _half_rows(1 - c, half), :], dst_ref=out_ref, send_sem=send_sem, recv_sem=recv_sem,
            device_id=(x, y, 1 - c), device_id_type=pl.DeviceIdType.MESH)
        cp.start()
        cp.wait()

    return pl.pallas_call(
        body,
        name=name,
        in_specs=[pl.BlockSpec(memory_space=pl.ANY)],
        out_specs=pl.BlockSpec(memory_space=pl.ANY),
        out_shape=jax.ShapeDtypeStruct((n, half, LANES), src.dtype),
        scratch_shapes=[pltpu.SemaphoreType.DMA, pltpu.SemaphoreType.DMA],
    )(src)


def _sibling_join(src, name):
    half = src.shape[0]

    def body(src_ref, out_ref, send_sem, recv_sem):
        x, y, c = lax.axis_index("x"), lax.axis_index("y"), lax.axis_index("c")
        cp = pltpu.make_async_remote_copy(
            src_ref=src_ref, dst_ref=out_ref.at[_half_rows(c, half), :], send_sem=send_sem, recv_sem=recv_sem,
            device_id=(x, y, 1 - c), device_id_type=pl.DeviceIdType.MESH)
        cp.start()
        pltpu.make_async_remote_copy(
            src_ref=src_ref, dst_ref=out_ref.at[_half_rows(1 - c, half), :], send_sem=send_sem, recv_sem=recv_sem,
            device_id=(x, y, 1 - c), device_id_type=pl.DeviceIdType.MESH).wait_recv()
        cp.wait_send()

    return pl.pallas_call(
        body,
        name=name,
        in_specs=[pl.BlockSpec(memory_space=pl.ANY)],
        out_specs=pl.BlockSpec(memory_space=pl.ANY),
        out_shape=jax.ShapeDtypeStruct((2 * half, LANES), src.dtype),
        scratch_shapes=[pltpu.SemaphoreType.DMA, pltpu.SemaphoreType.DMA],
    )(src)


SUM_ROWS = 1024


def _pair_sum(mine, theirs, c, name):
    n, half, _ = theirs.shape
    blocks = half // SUM_ROWS

    def body(c_ref, a_ref, b_ref, o_ref):
        o_ref[...] = (a_ref[...].astype(F32) + b_ref[...].astype(F32)).astype(o_ref.dtype)

    return pl.pallas_call(
        body,
        name=name,
        grid_spec=pltpu.PrefetchScalarGridSpec(
            num_scalar_prefetch=1,
            grid=(blocks,),
            in_specs=[pl.BlockSpec((n, SUM_ROWS, LANES), lambda i, c_ref: (0, c_ref[0] * blocks + i, 0)),
                      pl.BlockSpec((n, SUM_ROWS, LANES), lambda i, c_ref: (0, i, 0))],
            out_specs=pl.BlockSpec((n, SUM_ROWS, LANES), lambda i, c_ref: (0, i, 0)),
        ),
        out_shape=jax.ShapeDtypeStruct((n, half, LANES), BF16),
        compiler_params=_params(("arbitrary",)),
    )(jnp.reshape(c, (1,)).astype(jnp.int32), mine, theirs)


def _sum_chips(own, landed, me, name):
    rows = landed.shape[1]

    def body(me_ref, a_ref, b_ref, o_ref):
        t = [jnp.where(me_ref[0] == j, a_ref[j], b_ref[j]).astype(F32) for j in range(N_CHIPS)]
        o_ref[...] = ((t[0] + t[1]) + t[2]) + t[3]

    slabs = pl.BlockSpec((N_CHIPS, SUM_ROWS, LANES), lambda i, me_ref: (0, i, 0))
    return pl.pallas_call(
        body,
        name=name,
        grid_spec=pltpu.PrefetchScalarGridSpec(
            num_scalar_prefetch=1,
            grid=(rows // SUM_ROWS,),
            in_specs=[slabs, slabs],
            out_specs=pl.BlockSpec((SUM_ROWS, LANES), lambda i, me_ref: (i, 0)),
        ),
        out_shape=jax.ShapeDtypeStruct((rows, LANES), F32),
        compiler_params=_params(("arbitrary",)),
    )(jnp.reshape(me, (1,)).astype(jnp.int32), own, landed)


def _adamw(w, m, v, g, name):
    shape = w.shape
    flat = lambda a: a.reshape(-1, shape[-1])

    def fn(rows, consts):
        w_, m_, v_, g_ = rows
        m_new = ADAM_B1 * m_ + (1.0 - ADAM_B1) * g_
        v_new = ADAM_B2 * v_ + (1.0 - ADAM_B2) * (g_ * g_)
        m_hat = m_new / (1.0 - ADAM_B1 ** ADAM_STEP)
        v_hat = v_new / (1.0 - ADAM_B2 ** ADAM_STEP)
        delta = -ADAM_LR * (m_hat / (jnp.sqrt(v_hat) + ADAM_EPS) + ADAM_WD * w_)
        return [delta, m_new, v_new], []

    outs = _rowwise(fn, [flat(w), flat(m), flat(v), flat(g)], [], [(shape[-1], F32)] * 3, [], name)
    return [o.reshape(shape) for o in outs]


def kernel(x, positions, g_mix_pre, w_in, b_gate, g_q_lat, g_kv_lat, w_uq, w_ukv, swa_sinks, w_o_mla, w_o_swa, w_o_sb, w_out, g_mix_post, g_mlp_pre, w_up, w_down, g_mlp_post, loss_target, m_g_mix_pre, m_w_in, m_b_gate, m_g_q_lat, m_g_kv_lat, m_w_uq, m_w_ukv, m_swa_sinks, m_w_o_mla, m_w_o_swa, m_w_o_sb, m_w_out, m_g_mix_post, m_g_mlp_pre, m_w_up, m_w_down, m_g_mlp_post, v_g_mix_pre, v_w_in, v_b_gate, v_g_q_lat, v_g_kv_lat, v_w_uq, v_w_ukv, v_swa_sinks, v_w_o_mla, v_w_o_swa, v_w_o_sb, v_w_out, v_g_mix_post, v_g_mlp_pre, v_w_up, v_w_down, v_g_mlp_post):
    given = dict(locals())
    wts = {n: given[n] for n in WEIGHTS}
    mom_m = {n: given["m_" + n] for n in WEIGHTS}
    mom_v = {n: given["v_" + n] for n in WEIGHTS}
    shard_shapes = {n: wts[n].shape for n in SHARDED}
    small_shapes = {n: wts[n].shape for n in SMALL}

    me = 2 * lax.axis_index("x") + lax.axis_index("y")
    core = lax.axis_index("c")
    gathered = _gather_weights(_pack(wts, None, BF16), "gather_weights")
    full = {n: wts[n] for n in SMALL}
    per_chip = [_unpack(gathered[j], shard_shapes, None) for j in range(N_CHIPS)]
    for n in SHARDED:
        own = wts[n].astype(BF16)
        full[n] = jnp.concatenate([jnp.where(me == j, own, per_chip[j][n]) for j in range(N_CHIPS)], axis=SHARD_AXIS[n])

    loss_part, grad_x, grads = _local_step(x[0], positions[0], loss_target[0], full)
    loss = lax.psum(loss_part[0, 0], ("x", "y", "c"))

    small_g = {n: grads[n] for n in SMALL}
    slabs = []
    for j in range(N_CHIPS):
        shard = {n: jnp.split(grads[n], N_CHIPS, axis=SHARD_AXIS[n])[j] for n in SHARDED}
        slabs.append(_pack(shard, small_g, BF16))
    per_chip_g = jnp.stack(slabs)
    theirs = _sibling_halves(per_chip_g, "pair_grads")
    pair = _pair_sum(per_chip_g, theirs, core, "sum_pair")
    landed = _chip_exchange(pair, "scatter_grads")
    my_half = _sum_chips(pair, landed, me, "sum_chips")
    g_slab = lax.dynamic_update_slice(_sibling_join(my_half, "join_grads"), my_half, (core * my_half.shape[0], 0))

    g = _unpack(g_slab, shard_shapes, small_shapes)
    stepped = {n: _adamw(wts[n], mom_m[n], mom_v[n], g[n], "adamw_" + n) for n in WEIGHTS}
    outs = [loss, grad_x[None]] + [g[n] for n in WEIGHTS]
    for part in range(3):
        outs += [stepped[n][part] for n in WEIGHTS]
    return tuple(outs)
```

```python
import numpy as np
import jax
import jax.numpy as jnp
from jax import lax
from jax.experimental import pallas as pl
from jax.experimental.pallas import tpu as pltpu

F32 = jnp.float32
BF16 = jnp.bfloat16

D_MODEL = 1024
DEPTH = 4
MLA_HEADS, MLA_Q_LORA, MLA_KV_LORA, MLA_NOPE, MLA_ROPE, MLA_V = 8, 256, 128, 64, 32, 64
SWA_HEADS, SWA_KV_HEADS, SWA_HEAD_DIM, SWA_WINDOW = 8, 2, 64, 128
SB_HEADS, SB_HEAD_DIM = 8, 64
D_FF = 4 * D_MODEL
ROPE_THETA = 10000.0
EPS = 1e-6
SPLIT_SIZES = (256, 128, 32, 512, 128, 128, 512, 512, 512, 3 * D_MODEL)
SPLIT_POINTS = [int(v) for v in np.cumsum(SPLIT_SIZES)[:-1]]

ADAM_LR, ADAM_B1, ADAM_B2, ADAM_EPS, ADAM_WD, ADAM_STEP = 0.001, 0.9, 0.999, 1e-08, 0.01, 10

LANES = 128
V7X_VMEM_BYTES = 64 * 1024 * 1024
VMEM_LIMIT = V7X_VMEM_BYTES - 8 * 1024 * 1024
MATMUL_VMEM_BUDGET = 36 * 1024 * 1024
N_CHIPS = 4
SLAB_ROW_ALIGN = 2048

P1_W = 256 + 128 + 128 + 1024 + 256
P2_W = 256 + 1024 + 1024 + 1024
P3_W = 3 * D_MODEL

SHARDED = ("w_in", "w_uq", "w_ukv", "w_o_mla", "w_o_swa", "w_o_sb", "w_out", "w_up", "w_down")
SHARD_AXIS = {"w_in": 2, "w_uq": 2, "w_ukv": 2, "w_o_mla": 2, "w_o_swa": 2, "w_o_sb": 2, "w_out": 1, "w_up": 2, "w_down": 1}
SMALL = ("g_mix_pre", "b_gate", "g_q_lat", "g_kv_lat", "swa_sinks", "g_mix_post", "g_mlp_pre", "g_mlp_post")
WEIGHTS = ("g_mix_pre", "w_in", "b_gate", "g_q_lat", "g_kv_lat", "w_uq", "w_ukv", "swa_sinks", "w_o_mla", "w_o_swa",
           "w_o_sb", "w_out", "g_mix_post", "g_mlp_pre", "w_up", "w_down", "g_mlp_post")

NN = (((1,), (0,)), ((), ()))
NT = (((1,), (1,)), ((), ()))
TN = (((0,), (0,)), ((), ()))


def _dot(a, b, dims):
    return lax.dot_general(a, b, dims, preferred_element_type=F32)


def _params(sem):
    return pltpu.CompilerParams(dimension_semantics=sem, vmem_limit_bytes=VMEM_LIMIT)


def _largest_tile(n, cap):
    if n <= cap:
        return n
    best = LANES
    for t in range(LANES, cap + 1, LANES):
        if n % t == 0:
            best = t
    return best


def _matmul_tiles(M, N, K, a_bytes, b_bytes, out_bytes, extra_bytes):
    tn = _largest_tile(N, 1792)
    tm = _largest_tile(M, 1024 if tn <= 1024 else 512)
    tk = _largest_tile(K, 2048)

    def need(tm_, tk_):
        acc = 4 * tm_ * tn if tk_ < K else 0
        return 2 * (tm_ * tk_ * a_bytes + tk_ * tn * b_bytes + tm_ * tn * (out_bytes + extra_bytes)) + acc

    while need(tm, tk) > MATMUL_VMEM_BUDGET:
        if tk >= tm and tk % 256 == 0:
            tk //= 2
        elif tm % 256 == 0:
            tm //= 2
        else:
            break
    return tm, tn, tk


def _matmul(a, b, mode, out_dtypes, name, epilogue=None, extras=(), row_extras=()):
    b_layer = None
    if isinstance(b, tuple):
        b, b_layer = b
    b_shape = b.shape[-2:]
    if mode == "nn":
        (M, K), (K2, N) = a.shape, b_shape
    elif mode == "nt":
        (M, K), (N, K2) = a.shape, b_shape
    else:
        (K, M), (K2, N) = a.shape, b_shape
    assert K == K2, (name, a.shape, b.shape)
    tm, tn, tk = _matmul_tiles(
        M, N, K, a.dtype.itemsize, b.dtype.itemsize, sum(jnp.dtype(d).itemsize for d in out_dtypes),
        sum(e.dtype.itemsize for e in extras))
    assert M % tm == 0 and N % tn == 0 and K % tk == 0, (name, M, N, K, tm, tn, tk)
    nk = K // tk
    if mode == "tn":
        a_spec = pl.BlockSpec((tk, tm), lambda i, j, k: (k, i))
    else:
        a_spec = pl.BlockSpec((tm, tk), lambda i, j, k: (i, k))
    b_block, b_index = ((tn, tk), lambda i, j, k: (j, k)) if mode == "nt" else ((tk, tn), lambda i, j, k: (k, j))
    if b_layer is None:
        b_spec = pl.BlockSpec(b_block, b_index)
    else:
        b_spec = pl.BlockSpec((None,) + b_block, lambda i, j, k: (b_layer,) + b_index(i, j, k))
    dims = {"nn": NN, "nt": NT, "tn": TN}[mode]
    n_ex, n_rex, n_out = len(extras), len(row_extras), len(out_dtypes)

    def body(*refs):
        a_ref, b_ref = refs[:2]
        ex = refs[2:2 + n_ex]
        rex = refs[2 + n_ex:2 + n_ex + n_rex]
        outs = refs[2 + n_ex + n_rex:2 + n_ex + n_rex + n_out]

        def finish(total):
            res = (total,) if epilogue is None else epilogue(total, *[e[...] for e in ex], *[e[...] for e in rex])
            for o, r in zip(outs, res):
                o[...] = r.astype(o.dtype)

        part = _dot(a_ref[...].astype(BF16), b_ref[...].astype(BF16), dims)
        if nk == 1:
            finish(part)
            return
        acc = refs[-1]
        k = pl.program_id(2)

        @pl.when(k == 0)
        def _():
            acc[...] = part

        @pl.when(k > 0)
        def _():
            acc[...] += part

        @pl.when(k == nk - 1)
        def _():
            finish(acc[...])

    in_specs = [a_spec, b_spec]
    in_specs += [pl.BlockSpec((tm, tn), lambda i, j, k: (i, j)) for _ in extras]
    in_specs += [pl.BlockSpec((1, tn), lambda i, j, k: (0, j)) for _ in row_extras]
    out = pl.pallas_call(
        body,
        name=name,
        grid=(M // tm, N // tn, nk),
        in_specs=in_specs,
        out_specs=[pl.BlockSpec((tm, tn), lambda i, j, k: (i, j)) for _ in out_dtypes],
        out_shape=[jax.ShapeDtypeStruct((M, N), dt) for dt in out_dtypes],
        scratch_shapes=[pltpu.VMEM((tm, tn), F32)] if nk > 1 else [],
        compiler_params=_params(("parallel", "parallel", "arbitrary")),
    )(a, b, *extras, *row_extras)
    return out[0] if n_out == 1 else out


ROWWISE_ROW_BYTES = 16 * 1024


def _rowwise(fn, rows, consts, out_defs, sum_widths, name):
    R = rows[0].shape[0]
    per_row = sum(r.shape[1] * r.dtype.itemsize for r in rows) + sum(w * jnp.dtype(dt).itemsize for w, dt in out_defs)
    bm = 512 if per_row <= ROWWISE_ROW_BYTES else 256
    while R % bm:
        bm //= 2
    bm = max(bm, 1)
    n_r, n_c, n_o = len(rows), len(consts), len(out_defs)
    n_s = len(sum_widths)

    def body(*refs):
        r_in = refs[:n_r]
        c_in = refs[n_r:n_r + n_c]
        o_refs = refs[n_r + n_c:n_r + n_c + n_o]
        s_refs = refs[n_r + n_c + n_o:]
        outs, sums = fn([r[...] for r in r_in], [c[...] for c in c_in])
        for o, val in zip(o_refs, outs):
            o[...] = val.astype(o.dtype)
        if n_s:
            @pl.when(pl.program_id(0) == 0)
            def _():
                for s in s_refs:
                    s[...] = jnp.zeros_like(s)

            for s, val in zip(s_refs, sums):
                s[...] += jnp.sum(val, axis=0, keepdims=True)

    in_specs = [pl.BlockSpec((bm, r.shape[1]), lambda i: (i, 0)) for r in rows]
    in_specs += [pl.BlockSpec(c.shape, lambda i: (0, 0)) for c in consts]
    out_specs = [pl.BlockSpec((bm, w), lambda i: (i, 0)) for w, _ in out_defs]
    out_specs += [pl.BlockSpec((1, w), lambda i: (0, 0)) for w in sum_widths]
    out_shape = [jax.ShapeDtypeStruct((R, w), dt) for w, dt in out_defs]
    out_shape += [jax.ShapeDtypeStruct((1, w), F32) for w in sum_widths]
    return pl.pallas_call(
        body,
        name=name,
        grid=(R // bm,),
        in_specs=in_specs,
        out_specs=out_specs,
        out_shape=out_shape,
        compiler_params=_params(("arbitrary",)),
    )(*rows, *consts)


def _rms(x, g):
    r = lax.rsqrt(jnp.mean(x * x, axis=-1, keepdims=True) + EPS)
    return x * r * g


def _rms_bwd(x, g, dy):
    r = lax.rsqrt(jnp.mean(x * x, axis=-1, keepdims=True) + EPS)
    n = x * r
    dn = dy * g
    dx = r * (dn - n * jnp.mean(dn * n, axis=-1, keepdims=True))
    return dx, dy * n


def _rope(x, c, s_up, s_dn, half):
    return x * c + pltpu.roll(x, half, 1) * s_up + pltpu.roll(x, LANES - half, 1) * s_dn


def _rope_tables(positions, lo, d, nope_pass):
    S = positions.shape[0]
    half = d // 2
    inv = 1.0 / (ROPE_THETA ** (jnp.arange(0, d, 2, dtype=F32) / d))
    ang = positions.astype(F32)[:, None] * inv
    cos, sin = jnp.cos(ang), jnp.sin(ang)
    z = lambda n: jnp.zeros((S, n), F32)
    head = jnp.ones((S, lo), F32) if nope_pass else z(lo)
    tail = LANES - lo - d
    c = jnp.concatenate([head, cos, cos, z(tail)], axis=1)
    s_up = jnp.concatenate([z(lo), z(half), sin, z(tail)], axis=1)
    s_dn = jnp.concatenate([z(lo), -sin, z(half), z(tail)], axis=1)
    return c, s_up, s_dn


MLA_FWD_CFG = (2, 1024)
MLA_BWD_CFG = (2, 512)
SB_FWD_CFG = (2, 256)
SB_BWD_CFG = (4, 256)


def _tile_mask(bk, strict):
    row = lax.broadcasted_iota(jnp.int32, (bk, bk), 0)
    col = lax.broadcasted_iota(jnp.int32, (bk, bk), 1)
    return (col < row) if strict else (col <= row)


def _att_layout(S, cfg):
    nch, bk = cfg
    bq = nch * bk
    assert S % bq == 0, (S, cfg)
    rows = [slice(r * bk, (r + 1) * bk) for r in range(nch)]
    q_spec = lambda off=0: pl.BlockSpec((bq, LANES), lambda h, i: (i, off + h))
    kv_spec = lambda off=0: pl.BlockSpec((S, LANES), lambda h, i: (0, off + h))
    return bq, rows, q_spec, kv_spec


def _total(terms):
    terms = list(terms)
    out = terms[0]
    for t in terms[1:]:
        out = out + t
    return out


def _walk(nch, i, step, carry, leftward, alive=None):
    everyone = range(nch)
    if leftward:
        for d in reversed(everyone):
            carry = step(nch * i + d, carry, range(d, nch), {d})
        if alive is None:
            return lax.fori_loop(0, nch * i, lambda t, c: step(nch * i - 1 - t, c, everyone, set()), carry)
        more = lambda tc: jnp.logical_and(tc[0] < nch * i, alive(tc[1]))
        left = lambda tc: (tc[0] + 1, step(nch * i - 1 - tc[0], tc[1], everyone, set()))
        return lax.while_loop(more, left, (jnp.int32(0), carry))[1]
    carry = lax.fori_loop(0, nch * i, lambda kb, c: step(kb, c, everyone, set()), carry)
    for d in everyone:
        carry = step(nch * i + d, carry, range(d, nch), {d})
    return carry


ONES_LANE = MLA_V


def _softmax_attn_fwd(q, k, v, heads, name, q_off=0, k_off=0, v_off=0):
    S = q.shape[0]
    nch, bk = MLA_FWD_CFG
    bq, rows, q_spec, kv_spec = _att_layout(S, MLA_FWD_CFG)

    def body(q_ref, k_ref, v_ref, o_ref, lse_ref):
        i = pl.program_id(1)
        qs = [q_ref[rw, :] for rw in rows]

        def step(kb, cs, active, masked):
            off = pl.multiple_of(kb * bk, bk)
            ks, vs = k_ref[pl.ds(off, bk), :], v_ref[pl.ds(off, bk), :]
            A = list(active)
            s = {r: _dot(qs[r], ks, NT) for r in A}
            s = {r: (jnp.where(_tile_mask(bk, False), s[r], -1e30) if r in masked else s[r]) for r in A}
            m_new = {r: jnp.maximum(cs[r][0], jnp.max(s[r], axis=1, keepdims=True)) for r in A}
            p = {r: jnp.exp(s[r] - m_new[r]) for r in A}
            alpha = {r: jnp.exp(cs[r][0] - m_new[r]) for r in A}
            new = list(cs)
            for r in A:
                new[r] = (m_new[r], alpha[r] * cs[r][1] + _dot(p[r].astype(BF16), vs, NN))
            return tuple(new)

        init = (jnp.full((bk, 1), -1e30, F32), jnp.zeros((bk, LANES), F32))
        cs = _walk(nch, i, step, tuple(init for _ in rows), False)
        for r, (m, acc) in enumerate(cs):
            l = acc[:, ONES_LANE:ONES_LANE + 1]
            o_ref[rows[r], :] = (acc / l).astype(o_ref.dtype)
            lse_ref[rows[r], :] = m + jnp.log(l)

    return pl.pallas_call(
        body,
        name=name,
        grid=(heads, S // bq),
        in_specs=[q_spec(q_off), kv_spec(k_off), kv_spec(v_off)],
        out_specs=[q_spec(), pl.BlockSpec((None, bq, 1), lambda h, i: (h, i, 0))],
        out_shape=[jax.ShapeDtypeStruct((S, heads * LANES), BF16), jax.ShapeDtypeStruct((heads, S, 1), F32)],
        compiler_params=_params(("parallel", "arbitrary")),
    )(q, k, v)


def _softmax_attn_bwd(q, k, v, o, lse, do, heads, scale, name, q_off=0, k_off=0, v_off=0):
    S = q.shape[0]
    nch, bk = MLA_BWD_CFG
    bq, rows, q_spec, kv_spec = _att_layout(S, MLA_BWD_CFG)

    def body(q_ref, k_ref, v_ref, o_ref, lse_ref, do_ref, dq_ref, dk_ref, dv_ref):
        i = pl.program_id(1)

        @pl.when(i == 0)
        def _():
            dk_ref[...] = jnp.zeros_like(dk_ref)
            dv_ref[...] = jnp.zeros_like(dv_ref)

        qs = [q_ref[rw, :] for rw in rows]
        dos = [do_ref[rw, :] for rw in rows]
        lses = [lse_ref[rw, :] for rw in rows]
        deltas = [jnp.sum(dos[r].astype(F32) * o_ref[rows[r], :].astype(F32), axis=1, keepdims=True) for r in range(nch)]

        def step(kb, dqs, active, masked):
            off = pl.multiple_of(kb * bk, bk)
            ks, vs = k_ref[pl.ds(off, bk), :], v_ref[pl.ds(off, bk), :]
            A = list(active)
            s = {r: _dot(qs[r], ks, NT) for r in A}
            s = {r: (jnp.where(_tile_mask(bk, False), s[r], -1e30) if r in masked else s[r]) for r in A}
            p = {r: jnp.exp(s[r] - lses[r]) for r in A}
            dp = {r: _dot(dos[r], vs, NT) for r in A}
            ds = {r: (p[r] * (dp[r] - deltas[r])).astype(BF16) for r in A}
            dv_c = _total(_dot(p[r].astype(BF16), dos[r], TN) for r in A)
            dk_c = _total(_dot(ds[r], qs[r], TN) for r in A)
            dk_ref[pl.ds(off, bk), :] += dk_c
            dv_ref[pl.ds(off, bk), :] += dv_c
            new = list(dqs)
            for r in A:
                new[r] = dqs[r] + _dot(ds[r], ks, NN)
            return tuple(new)

        dqs = _walk(nch, i, step, tuple(jnp.zeros((bk, LANES), F32) for _ in rows), False)
        for r in range(nch):
            dq_ref[rows[r], :] = dqs[r] * scale

    return pl.pallas_call(
        body,
        name=name,
        grid=(heads, S // bq),
        in_specs=[q_spec(q_off), kv_spec(k_off), kv_spec(v_off), q_spec(),
                  pl.BlockSpec((None, bq, 1), lambda h, i: (h, i, 0)), q_spec()],
        out_specs=[q_spec(), kv_spec(), kv_spec()],
        out_shape=[jax.ShapeDtypeStruct((S, heads * LANES), F32)] * 3,
        compiler_params=_params(("parallel", "arbitrary")),
    )(q, k, v, o, lse, do)


def _tri(n, inclusive):
    r = lax.broadcasted_iota(jnp.int32, (n, n), 0)
    c = lax.broadcasted_iota(jnp.int32, (n, n), 1)
    return jnp.where((r >= c) if inclusive else (r > c), 1.0, 0.0).astype(BF16)


def _suffix_sum(x, tri, split=True):
    hi = x.astype(BF16)
    if not split:
        return _dot(hi, tri, NN)
    lo = (x - hi.astype(F32)).astype(BF16)
    return _dot(hi, tri, NN) + _dot(lo, tri, NN)


def _sb_logs(z):
    lg = jnp.log(1.0 + jnp.exp(-jnp.abs(z)))
    l1m = -(jnp.maximum(z, 0.0) + lg)
    return l1m, l1m + z


SB_SCALE = SB_HEAD_DIM ** -0.5
assert SB_SCALE == 0.125
SB_DEAD = -110.0


def _sb_alive(cs):
    top = cs[0][0]
    for c in cs[1:]:
        top = jnp.maximum(top, c[0])
    return jnp.max(top) > SB_DEAD


def _sb_attn_fwd(qkv, heads, name, q_off, k_off, v_off):
    S = qkv.shape[0]
    nch, bk = SB_FWD_CFG
    bq, rows, q_spec, kv_spec = _att_layout(S, SB_FWD_CFG)

    def body(q_ref, k_ref, v_ref, o_ref):
        i = pl.program_id(1)
        qs = [q_ref[rw, :] * SB_SCALE for rw in rows]
        tri = _tri(bk, False)

        def step(kb, cs, active, masked):
            off = pl.multiple_of(kb * bk, bk)
            ks, vs = k_ref[pl.ds(off, bk), :], v_ref[pl.ds(off, bk), :]
            A = list(active)
            lg = {r: _sb_logs(_dot(qs[r], ks, NT)) for r in A}
            l1m = {r: (jnp.where(_tile_mask(bk, True), lg[r][0], 0.0) if r in masked else lg[r][0]) for r in A}
            suf = {r: _suffix_sum(l1m[r], tri, split=False) for r in A}
            ex = {r: lg[r][1] + cs[r][0] + suf[r] for r in A}
            ex = {r: (jnp.where(_tile_mask(bk, True), ex[r], -1e30) if r in masked else ex[r]) for r in A}
            ab = {r: jnp.exp(ex[r]).astype(BF16) for r in A}
            new = list(cs)
            for r in A:
                new[r] = (cs[r][0] + jnp.sum(l1m[r], axis=1, keepdims=True), cs[r][1] + _dot(ab[r], vs, NN))
            return tuple(new)

        init = (jnp.zeros((bk, 1), F32), jnp.zeros((bk, LANES), F32))
        cs = _walk(nch, i, step, tuple(init for _ in rows), True, _sb_alive)
        for r in range(nch):
            o_ref[rows[r], :] = cs[r][1]

    return pl.pallas_call(
        body,
        name=name,
        grid=(heads, S // bq),
        in_specs=[q_spec(q_off), kv_spec(k_off), kv_spec(v_off)],
        out_specs=q_spec(),
        out_shape=jax.ShapeDtypeStruct((S, heads * LANES), F32),
        compiler_params=_params(("parallel", "arbitrary")),
    )(qkv, qkv, qkv)


def _sb_attn_bwd(qkv, o, do, heads, name, q_off, k_off, v_off):
    S = qkv.shape[0]
    nch, bk = SB_BWD_CFG
    bq, rows, q_spec, kv_spec = _att_layout(S, SB_BWD_CFG)

    def body(q_ref, k_ref, v_ref, o_ref, do_ref, dq_ref, dk_ref, dv_ref):
        i = pl.program_id(1)

        @pl.when(i == 0)
        def _():
            dk_ref[...] = jnp.zeros_like(dk_ref)
            dv_ref[...] = jnp.zeros_like(dv_ref)

        tri = _tri(bk, False)
        qs = [q_ref[rw, :] * SB_SCALE for rw in rows]
        dos = [do_ref[rw, :] for rw in rows]
        deltas = [jnp.sum(dos[r].astype(F32) * o_ref[rows[r], :], axis=1, keepdims=True) for r in range(nch)]

        def step(kb, cs, active, masked):
            off = pl.multiple_of(kb * bk, bk)
            ks, vs = k_ref[pl.ds(off, bk), :], v_ref[pl.ds(off, bk), :]
            A = list(active)
            lg = {r: _sb_logs(_dot(qs[r], ks, NT)) for r in A}
            l1m = {r: (jnp.where(_tile_mask(bk, True), lg[r][0], 0.0) if r in masked else lg[r][0]) for r in A}
            suf = {r: _suffix_sum(l1m[r], tri, split=False) for r in A}
            ex = {r: lg[r][1] + cs[r][0] + suf[r] for r in A}
            ex = {r: (jnp.where(_tile_mask(bk, True), ex[r], -1e30) if r in masked else ex[r]) for r in A}
            ab = {r: jnp.exp(ex[r]).astype(BF16) for r in A}
            da = {r: _dot(dos[r], vs, NT) for r in A}
            g = {r: ab[r].astype(F32) * da[r] for r in A}
            gs = {r: _suffix_sum(g[r], tri) for r in A}
            beta = {r: jnp.exp(lg[r][1]) for r in A}
            dz = {r: g[r] - beta[r] * (deltas[r] - cs[r][1] - gs[r]) for r in A}
            dz = {r: (jnp.where(_tile_mask(bk, True), dz[r], 0.0) if r in masked else dz[r]) for r in A}
            dzb = {r: dz[r].astype(BF16) for r in A}
            dv_c = _total(_dot(ab[r], dos[r], TN) for r in A)
            dk_c = _total(_dot(dzb[r], qs[r], TN) for r in A)
            dk_ref[pl.ds(off, bk), :] += dk_c
            dv_ref[pl.ds(off, bk), :] += dv_c
            new = list(cs)
            for r in A:
                new[r] = (cs[r][0] + jnp.sum(l1m[r], axis=1, keepdims=True),
                          cs[r][1] + jnp.sum(g[r], axis=1, keepdims=True), cs[r][2] + _dot(dzb[r], ks, NN))
            return tuple(new)

        zcol = jnp.zeros((bk, 1), F32)
        init = (zcol, zcol, jnp.zeros((bk, LANES), F32))
        cs = _walk(nch, i, step, tuple(init for _ in rows), True, _sb_alive)
        for r in range(nch):
            dq_ref[rows[r], :] = cs[r][2] * SB_SCALE

    return pl.pallas_call(
        body,
        name=name,
        grid=(heads, S // bq),
        in_specs=[q_spec(q_off), kv_spec(k_off), kv_spec(v_off), q_spec(), q_spec()],
        out_specs=[q_spec(), kv_spec(), kv_spec()],
        out_shape=[jax.ShapeDtypeStruct((S, heads * LANES), F32)] * 3,
        compiler_params=_params(("parallel", "arbitrary")),
    )(qkv, qkv, qkv, o, do)


SWA_BLK = 128
SWA_GROUP = SWA_HEADS // SWA_KV_HEADS


SWA_NB = 4
SWA_ROWS = SWA_NB * SWA_BLK


def _swa_band_mask(first):
    row = lax.broadcasted_iota(jnp.int32, (SWA_BLK, 2 * SWA_BLK), 0)
    col = lax.broadcasted_iota(jnp.int32, (SWA_BLK, 2 * SWA_BLK), 1)
    return (col > row) & (col <= row + SWA_WINDOW) & (jnp.logical_not(first) | (col >= SWA_BLK))


def _swa_in_specs(v_off):
    gw = SWA_GROUP * LANES
    before = lambda h, n: (jnp.maximum(SWA_NB * n - 1, 0), h)
    return [
        pl.BlockSpec((SWA_ROWS, gw), lambda h, n: (n, h)),
        pl.BlockSpec((SWA_BLK, LANES), before),
        pl.BlockSpec((SWA_ROWS, LANES), lambda h, n: (n, h)),
        pl.BlockSpec((SWA_BLK, LANES), lambda h, n: (jnp.maximum(SWA_NB * n - 1, 0), v_off + h)),
        pl.BlockSpec((SWA_ROWS, LANES), lambda h, n: (n, v_off + h)),
        pl.BlockSpec((1, gw), lambda h, n: (0, h)),
    ]


def _swa_bands(n, kp_ref, kc_ref, vp_ref, vc_ref):
    k_all = jnp.concatenate([kp_ref[...], kc_ref[...]], axis=0)
    v_all = jnp.concatenate([vp_ref[...], vc_ref[...]], axis=0)
    bands = []
    for j in range(SWA_NB):
        rows = slice(j * SWA_BLK, (j + 2) * SWA_BLK)
        bands.append((k_all[rows], v_all[rows], _swa_band_mask((n == 0) if j == 0 else False)))
    return bands


def _swa_fwd(q, k, v, v_off, sink_b, name):
    S = q.shape[0]
    assert S % SWA_ROWS == 0
    scale = SWA_HEAD_DIM ** -0.5
    gw = SWA_GROUP * LANES

    def body(q_ref, kp_ref, kc_ref, vp_ref, vc_ref, sink_ref, o_ref, lse_ref):
        n = pl.program_id(1)
        bands = _swa_bands(n, kp_ref, kc_ref, vp_ref, vc_ref)
        P = [(j, g) for j in range(SWA_NB) for g in range(SWA_GROUP)]
        rows = lambda j: slice(j * SWA_BLK, (j + 1) * SWA_BLK)
        lanes = lambda g: slice(g * LANES, (g + 1) * LANES)
        sk = {g: sink_ref[:, g * LANES:g * LANES + 1] for g in range(SWA_GROUP)}
        s = {(j, g): jnp.where(bands[j][2], _dot(q_ref[rows(j), lanes(g)], bands[j][0], NT) * scale, -1e30) for j, g in P}
        m = {(j, g): jnp.maximum(jnp.max(s[j, g], axis=1, keepdims=True), sk[g]) for j, g in P}
        p = {(j, g): jnp.exp(s[j, g] - m[j, g]) for j, g in P}
        den = {(j, g): jnp.sum(p[j, g], axis=1, keepdims=True) + jnp.exp(sk[g] - m[j, g]) for j, g in P}
        for j, g in P:
            o_ref[rows(j), lanes(g)] = _dot((p[j, g] / den[j, g]).astype(BF16), bands[j][1], NN).astype(o_ref.dtype)
            lse_ref[g, rows(j), :] = m[j, g] + jnp.log(den[j, g])

    return pl.pallas_call(
        body,
        name=name,
        grid=(SWA_KV_HEADS, S // SWA_ROWS),
        in_specs=_swa_in_specs(v_off),
        out_specs=[
            pl.BlockSpec((SWA_ROWS, gw), lambda h, n: (n, h)),
            pl.BlockSpec((SWA_GROUP, SWA_ROWS, 1), lambda h, n: (h, n, 0)),
        ],
        out_shape=[jax.ShapeDtypeStruct((S, SWA_HEADS * LANES), BF16), jax.ShapeDtypeStruct((SWA_HEADS, S, 1), F32)],
        compiler_params=_params(("parallel", "arbitrary")),
    )(q, k, k, v, v, sink_b)


def _swa_bwd(q, k, v, v_off, sink_b, o, lse, do, name):
    S = q.shape[0]
    assert S % SWA_ROWS == 0
    scale = SWA_HEAD_DIM ** -0.5
    gw = SWA_GROUP * LANES

    def body(q_ref, kp_ref, kc_ref, vp_ref, vc_ref, sink_ref, o_ref, lse_ref, do_ref, dq_ref, dk_ref, dv_ref, dsink_ref):
        n = pl.program_id(1)

        @pl.when(n == 0)
        def _():
            dk_ref[...] = jnp.zeros_like(dk_ref)
            dv_ref[...] = jnp.zeros_like(dv_ref)
            dsink_ref[...] = jnp.zeros_like(dsink_ref)

        bands = _swa_bands(n, kp_ref, kc_ref, vp_ref, vc_ref)
        P = [(j, g) for j in range(SWA_NB) for g in range(SWA_GROUP)]
        rows = lambda j: slice(j * SWA_BLK, (j + 1) * SWA_BLK)
        lanes = lambda g: slice(g * LANES, (g + 1) * LANES)
        qs = {(j, g): q_ref[rows(j), lanes(g)] for j, g in P}
        dos = {(j, g): do_ref[rows(j), lanes(g)] for j, g in P}
        lses = {(j, g): lse_ref[g, rows(j), :] for j, g in P}
        delta = {(j, g): jnp.sum(dos[j, g].astype(F32) * o_ref[rows(j), lanes(g)].astype(F32), axis=1, keepdims=True)
                 for j, g in P}
        s = {(j, g): jnp.where(bands[j][2], _dot(qs[j, g], bands[j][0], NT) * scale, -1e30) for j, g in P}
        p = {(j, g): jnp.exp(s[j, g] - lses[j, g]) for j, g in P}
        dp = {(j, g): _dot(dos[j, g], bands[j][1], NT) for j, g in P}
        ds = {(j, g): (p[j, g] * (dp[j, g] - delta[j, g]) * scale).astype(BF16) for j, g in P}
        for j, g in P:
            dq_ref[rows(j), lanes(g)] = _dot(ds[j, g], bands[j][0], NN)
        for g in range(SWA_GROUP):
            p_sink = [jnp.exp(sink_ref[:, g * LANES:g * LANES + 1] - lses[j, g]) * delta[j, g] for j in range(SWA_NB)]
            dsink_ref[:, lanes(g)] += jnp.zeros((1, LANES), F32) - jnp.sum(_total(p_sink), axis=0, keepdims=True)
        dkb = [_total(_dot(ds[j, g], qs[j, g], TN) for g in range(SWA_GROUP)) for j in range(SWA_NB)]
        dvb = [_total(_dot(p[j, g].astype(BF16), dos[j, g], TN) for g in range(SWA_GROUP)) for j in range(SWA_NB)]
        base = pl.multiple_of(n * SWA_ROWS, SWA_ROWS)
        for j in range(SWA_NB):
            own = pl.ds(base + j * SWA_BLK, SWA_BLK)
            after = j + 1 < SWA_NB
            dk_ref[own, :] += dkb[j][SWA_BLK:] + dkb[j + 1][:SWA_BLK] if after else dkb[j][SWA_BLK:]
            dv_ref[own, :] += dvb[j][SWA_BLK:] + dvb[j + 1][:SWA_BLK] if after else dvb[j][SWA_BLK:]

        @pl.when(n > 0)
        def _():
            before = pl.ds(pl.multiple_of(n * SWA_ROWS - SWA_BLK, SWA_BLK), SWA_BLK)
            dk_ref[before, :] += dkb[0][:SWA_BLK]
            dv_ref[before, :] += dvb[0][:SWA_BLK]

    return pl.pallas_call(
        body,
        name=name,
        grid=(SWA_KV_HEADS, S // SWA_ROWS),
        in_specs=_swa_in_specs(v_off) + [
            pl.BlockSpec((SWA_ROWS, gw), lambda h, n: (n, h)),
            pl.BlockSpec((SWA_GROUP, SWA_ROWS, 1), lambda h, n: (h, n, 0)),
            pl.BlockSpec((SWA_ROWS, gw), lambda h, n: (n, h)),
        ],
        out_specs=[
            pl.BlockSpec((SWA_ROWS, gw), lambda h, n: (n, h)),
            pl.BlockSpec((S, LANES), lambda h, n: (0, h)),
            pl.BlockSpec((S, LANES), lambda h, n: (0, h)),
            pl.BlockSpec((1, gw), lambda h, n: (0, h)),
        ],
        out_shape=[
            jax.ShapeDtypeStruct((S, SWA_HEADS * LANES), F32),
            jax.ShapeDtypeStruct((S, SWA_KV_HEADS * LANES), F32),
            jax.ShapeDtypeStruct((S, SWA_KV_HEADS * LANES), F32),
            jax.ShapeDtypeStruct((1, SWA_HEADS * LANES), F32),
        ],
        compiler_params=_params(("parallel", "arbitrary")),
    )(q, k, k, v, v, sink_b, o, lse, do)


def _pad_last(t, width):
    return jnp.pad(t, [(0, 0)] * (t.ndim - 1) + [(0, width - t.shape[-1])])


def _pad_cols(w, heads, real):
    lead = w.shape[:-1]
    return _pad_last(w.reshape(*lead, heads, real), LANES).reshape(*lead, heads * LANES)


def _unpad_cols(g, heads, real):
    lead = g.shape[:-1]
    return g.reshape(*lead, heads, LANES)[..., :real].reshape(*lead, heads * real)


def _pad_rows(w, heads, real):
    lead, n = w.shape[:-2], w.shape[-1]
    w = w.reshape(*lead, heads, real, n)
    return jnp.pad(w, [(0, 0)] * (w.ndim - 2) + [(0, LANES - real), (0, 0)]).reshape(*lead, heads * LANES, n)


def _unpad_rows(g, heads, real):
    lead, n = g.shape[:-2], g.shape[-1]
    return g.reshape(*lead, heads, LANES, n)[..., :real, :].reshape(*lead, heads * real, n)


def _w_in_internal(w_in):
    c_q, c_kv, k_r, q_swa, k_swa, v_swa, q_sb, k_sb, v_sb, gate = jnp.split(w_in, SPLIT_POINTS, axis=-1)
    k_r = jnp.pad(k_r, [(0, 0)] * (k_r.ndim - 1) + [(MLA_NOPE, LANES - MLA_NOPE - MLA_ROPE)])
    w1 = jnp.concatenate([c_q, c_kv, k_r, _pad_cols(q_swa, 8, 64), _pad_cols(k_swa, 2, 64)], axis=-1)
    w2 = [_pad_cols(v_swa, 2, 64), _pad_cols(q_sb, 8, 64), _pad_cols(k_sb, 8, 64), _pad_cols(v_sb, 8, 64)]
    return w1, w2, gate


def _w_in_reference(g1, g2, g3):
    c_q, c_kv, k_r, q_swa, k_swa = jnp.split(g1, [256, 384, 512, 1536], axis=-1)
    v_swa, q_sb, k_sb, v_sb = g2
    return jnp.concatenate([
        c_q, c_kv, k_r[..., MLA_NOPE:MLA_NOPE + MLA_ROPE], _unpad_cols(q_swa, 8, 64), _unpad_cols(k_swa, 2, 64),
        _unpad_cols(v_swa, 2, 64), _unpad_cols(q_sb, 8, 64), _unpad_cols(k_sb, 8, 64), _unpad_cols(v_sb, 8, 64),
        g3], axis=-1)


def _w_ukv_internal(w):
    lead = w.shape[:-1]
    w3 = w.reshape(*lead, MLA_HEADS, MLA_NOPE + MLA_V)
    pad = lambda t: _pad_last(t, LANES).reshape(*lead, MLA_HEADS * LANES)
    return pad(w3[..., :MLA_NOPE]), pad(w3[..., MLA_NOPE:])


def _w_ukv_reference(gk, gv):
    lead = gk.shape[:-1]
    gk = gk.reshape(*lead, MLA_HEADS, LANES)[..., :MLA_NOPE]
    gv = gv.reshape(*lead, MLA_HEADS, LANES)[..., :MLA_V]
    return jnp.concatenate([gk, gv], axis=-1).reshape(*lead, MLA_HEADS * (MLA_NOPE + MLA_V))


def _layer_fwd(x, w, tabs):
    mla_tab, swa_tab = tabs
    sv = {"x": x}

    def f_norm(rows, consts):
        return [_rms(rows[0], consts[0])], []

    (h,) = _rowwise(f_norm, [x], [w["g_mix_pre"]], [(D_MODEL, BF16)], [], "norm_mix_pre")
    p1 = _matmul(h, w["w_in1"], "nn", [F32], "proj_lat")
    p2 = _matmul(h, w["w_in2"], "nn", [BF16], "proj_qkv")
    gates = _matmul(h, w["w_in3"], "nn", [BF16], "proj_gate",
                    epilogue=lambda acc, b: (1.0 / (1.0 + jnp.exp(-(acc + b))),), row_extras=[w["b_gate"]])

    def f_prep(rows, consts):
        t = rows[0]
        gq, gkv = consts[0], consts[1]
        mc, mu, md = rows[1], rows[2], rows[3]
        sc, su, sd = rows[4], rows[5], rows[6]
        cq_n = _rms(t[:, 0:256], gq)
        ckv_n = _rms(t[:, 256:384], gkv)
        kr = _rope(t[:, 384:512], mc, mu, md, MLA_ROPE // 2)
        qs = [_rope(t[:, 512 + j * LANES:512 + (j + 1) * LANES], sc, su, sd, SWA_HEAD_DIM // 2) for j in range(8)]
        ks = [_rope(t[:, 1536 + j * LANES:1536 + (j + 1) * LANES], sc, su, sd, SWA_HEAD_DIM // 2) for j in range(2)]
        return [cq_n, ckv_n, kr, jnp.concatenate(qs, axis=1), jnp.concatenate(ks, axis=1)], []

    cq_n, ckv_n, kr, q_swa, k_swa = _rowwise(
        f_prep, [p1, *mla_tab["k"], *swa_tab["f"]], [w["g_q_lat"], w["g_kv_lat"]],
        [(256, BF16), (128, BF16), (LANES, F32), (1024, BF16), (256, BF16)], [], "lat_prep")

    q_lat = _matmul(cq_n, w["w_uq"], "nn", [F32], "mla_q_up")
    k_lat = _matmul(ckv_n, w["w_ukv_k"], "nn", [F32], "mla_k_up")
    def ones_lane(acc):
        lane = lax.broadcasted_iota(jnp.int32, acc.shape, 1) % LANES
        return (jnp.where(lane == ONES_LANE, 1.0, acc),)

    v_mla = _matmul(ckv_n, w["w_ukv_v"], "nn", [BF16], "mla_v_up", epilogue=ones_lane)
    mla_scale = (MLA_NOPE + MLA_ROPE) ** -0.5

    def f_mla_prep(rows, consts):
        ql, kl, krr, mc, mu, md = rows
        qs = [_rope(ql[:, j * LANES:(j + 1) * LANES], mc, mu, md, MLA_ROPE // 2) * mla_scale for j in range(8)]
        ks = [kl[:, j * LANES:(j + 1) * LANES] + krr for j in range(8)]
        return [jnp.concatenate(qs, axis=1), jnp.concatenate(ks, axis=1)], []

    q_mla, k_mla = _rowwise(f_mla_prep, [q_lat, k_lat, kr, *mla_tab["q"]], [], [(1024, BF16), (1024, BF16)], [], "mla_prep")

    o_mla, lse_mla = _softmax_attn_fwd(q_mla, k_mla, v_mla, MLA_HEADS, "mla_fwd")
    o_swa, lse_swa = _swa_fwd(q_swa, k_swa, p2, 0, w["sink_b"], "swa_fwd")
    o_sb = _sb_attn_fwd(p2, SB_HEADS, "sb_fwd", 2, 10, 18)

    oa = _matmul(o_mla, w["w_o_mla"], "nn", [BF16], "o_proj_mla")
    ob = _matmul(o_swa, w["w_o_swa"], "nn", [BF16], "o_proj_swa")
    oc = _matmul(o_sb, w["w_o_sb"], "nn", [BF16], "o_proj_sb")

    def f_mix(rows, consts):
        a, b, c, g = rows
        g = g.astype(F32)
        return [g[:, 0:1024] * a + g[:, 1024:2048] * b + g[:, 2048:3072] * c], []

    (mixed,) = _rowwise(f_mix, [oa, ob, oc, gates], [], [(D_MODEL, BF16)], [], "gate_mix")
    y = _matmul(mixed, w["w_out"], "nn", [F32], "out_proj")

    def f_res_norm(rows, consts):
        return [rows[0] + _rms(rows[1], consts[0])], []

    (x1,) = _rowwise(f_res_norm, [x, y], [w["g_mix_post"]], [(D_MODEL, F32)], [], "res_norm_mix")
    (h2,) = _rowwise(f_norm, [x1], [w["g_mlp_pre"]], [(D_MODEL, BF16)], [], "norm_mlp_pre")

    def relu2(acc):
        r = jnp.maximum(acc, 0.0)
        return acc, r * r

    up, u = _matmul(h2, w["w_up"], "nn", [BF16, BF16], "mlp_up", epilogue=relu2)
    zd = _matmul(u, w["w_down"], "nn", [F32], "mlp_down")
    (x2,) = _rowwise(f_res_norm, [x1, zd], [w["g_mlp_post"]], [(D_MODEL, F32)], [], "res_norm_mlp")

    sv.update(h=h, p1=p1, p2=p2, gates=gates, cq_n=cq_n, ckv_n=ckv_n, q_swa=q_swa, k_swa=k_swa, q_mla=q_mla,
              k_mla=k_mla, v_mla=v_mla, o_mla=o_mla, lse_mla=lse_mla, o_swa=o_swa, lse_swa=lse_swa, o_sb=o_sb,
              oa=oa, ob=ob, oc=oc, mixed=mixed, y=y, x1=x1, h2=h2, up=up, u=u, zd=zd)
    return x2, sv


def _layer_bwd(dx2, w, sv, tabs):
    mla_tab, swa_tab = tabs
    gr = {}

    def f_norm_bwd(rows, consts):
        dx, dg = _rms_bwd(rows[0], consts[0], rows[1])
        return [dx], [dg]

    def f_norm_bwd_res(rows, consts):
        dx, dg = _rms_bwd(rows[0], consts[0], rows[1])
        return [rows[2] + dx], [dg]

    dzd, gr["g_mlp_post"] = _rowwise(f_norm_bwd, [sv["zd"], dx2], [w["g_mlp_post"]], [(D_MODEL, BF16)], [D_MODEL], "b_norm_mlp_post")
    gr["w_down"] = _matmul(sv["u"], dzd, "tn", [BF16], "b_w_down")
    dup = _matmul(dzd, w["w_down"], "nt", [BF16], "b_mlp_down",
                  epilogue=lambda acc, up: (acc * 2.0 * jnp.maximum(up.astype(F32), 0.0),), extras=[sv["up"]])
    gr["w_up"] = _matmul(sv["h2"], dup, "tn", [BF16], "b_w_up")
    dh2 = _matmul(dup, w["w_up"], "nt", [F32], "b_mlp_up")
    dx1, gr["g_mlp_pre"] = _rowwise(f_norm_bwd_res, [sv["x1"], dh2, dx2], [w["g_mlp_pre"]], [(D_MODEL, F32)], [D_MODEL], "b_norm_mlp_pre")

    dy, gr["g_mix_post"] = _rowwise(f_norm_bwd, [sv["y"], dx1], [w["g_mix_post"]], [(D_MODEL, BF16)], [D_MODEL], "b_norm_mix_post")
    gr["w_out"] = _matmul(sv["mixed"], dy, "tn", [BF16], "b_w_out")
    dmixed = _matmul(dy, w["w_out"], "nt", [F32], "b_out_proj")

    def f_mix_bwd(rows, consts):
        dm, a, b, c, g = rows
        g = g.astype(F32)
        outs, dls = [], []
        for j, o in enumerate((a, b, c)):
            gj = g[:, j * D_MODEL:(j + 1) * D_MODEL]
            outs.append(dm * gj)
            dls.append(dm * o * gj * (1.0 - gj))
        dl = jnp.concatenate(dls, axis=1)
        return outs + [dl], [dl]

    doa, dob, doc, dlogit, gr["b_gate"] = _rowwise(
        f_mix_bwd, [dmixed, sv["oa"], sv["ob"], sv["oc"], sv["gates"]], [],
        [(D_MODEL, BF16)] * 3 + [(P3_W, BF16)], [P3_W], "b_gate_mix")

    gr["w_o_mla"] = _matmul(sv["o_mla"], doa, "tn", [BF16], "b_w_o_mla")
    gr["w_o_swa"] = _matmul(sv["o_swa"], dob, "tn", [BF16], "b_w_o_swa")
    gr["w_o_sb"] = _matmul(sv["o_sb"], doc, "tn", [BF16], "b_w_o_sb")
    do_mla = _matmul(doa, w["w_o_mla"], "nt", [BF16], "b_o_proj_mla")
    do_swa = _matmul(dob, w["w_o_swa"], "nt", [BF16], "b_o_proj_swa")
    do_sb = _matmul(doc, w["w_o_sb"], "nt", [BF16], "b_o_proj_sb")

    dq_sb, dk_sb, dv_sb = _sb_attn_bwd(sv["p2"], sv["o_sb"], do_sb, SB_HEADS, "sb_bwd", 2, 10, 18)
    dq_swa, dk_swa, dv_swa, dsink = _swa_bwd(sv["q_swa"], sv["k_swa"], sv["p2"], 0, w["sink_b"], sv["o_swa"],
                                             sv["lse_swa"], do_swa, "swa_bwd")
    gr["swa_sinks"] = dsink.reshape(SWA_HEADS, LANES)[:, 0]
    dq_mla, dk_mla, dv_mla = _softmax_attn_bwd(sv["q_mla"], sv["k_mla"], sv["v_mla"], sv["o_mla"], sv["lse_mla"], do_mla,
                                               MLA_HEADS, (MLA_NOPE + MLA_ROPE) ** -0.5, "mla_bwd")

    def f_mla_post(rows, consts):
        dq, dk, qc, qu, qd, kc, ku, kd = rows
        dqs = [_rope(dq[:, j * LANES:(j + 1) * LANES], qc, qu, qd, MLA_ROPE // 2) for j in range(8)]
        dkr = dk[:, 0:LANES]
        for j in range(1, 8):
            dkr = dkr + dk[:, j * LANES:(j + 1) * LANES]
        return [jnp.concatenate(dqs, axis=1), _rope(dkr, kc, ku, kd, MLA_ROPE // 2)], []

    dq_lat, dkr = _rowwise(f_mla_post, [dq_mla, dk_mla, *mla_tab["q_inv"], *mla_tab["k_inv"]], [],
                           [(1024, BF16), (LANES, F32)], [], "b_mla_post")
    gr["w_uq"] = _matmul(sv["cq_n"], dq_lat, "tn", [BF16], "b_w_uq")
    gr["w_ukv_k"] = _matmul(sv["ckv_n"], dk_mla, "tn", [BF16], "b_w_ukv_k")
    gr["w_ukv_v"] = _matmul(sv["ckv_n"], dv_mla, "tn", [BF16], "b_w_ukv_v")
    dcq_n = _matmul(dq_lat, w["w_uq"], "nt", [F32], "b_mla_q_up")
    dckv_a = _matmul(dk_mla, w["w_ukv_k"], "nt", [F32], "b_mla_k_up")
    dckv_b = _matmul(dv_mla, w["w_ukv_v"], "nt", [F32], "b_mla_v_up")

    def f_prep_bwd(rows, consts):
        t, dcq, dca, dcb, dkr_, dqs, dks, sc, su, sd = rows
        gq, gkv = consts
        dc_q, dgq = _rms_bwd(t[:, 0:256], gq, dcq)
        dc_kv, dgkv = _rms_bwd(t[:, 256:384], gkv, dca + dcb)
        q_parts = [_rope(dqs[:, j * LANES:(j + 1) * LANES], sc, su, sd, SWA_HEAD_DIM // 2) for j in range(8)]
        k_parts = [_rope(dks[:, j * LANES:(j + 1) * LANES], sc, su, sd, SWA_HEAD_DIM // 2) for j in range(2)]
        return [jnp.concatenate([dc_q, dc_kv, dkr_] + q_parts + k_parts, axis=1)], [dgq, dgkv]

    dp1, gr["g_q_lat"], gr["g_kv_lat"] = _rowwise(
        f_prep_bwd, [sv["p1"], dcq_n, dckv_a, dckv_b, dkr, dq_swa, dk_swa, *swa_tab["inv"]], [w["g_q_lat"], w["g_kv_lat"]],
        [(P1_W, BF16)], [256, 128], "b_lat_prep")

    gr["w_in1"] = _matmul(sv["h"], dp1, "tn", [BF16], "b_w_in_lat")
    dh = _matmul(dp1, w["w_in1"], "nt", [F32], "b_proj_lat")
    gr["w_in2"] = []
    add_prev = lambda acc, prev: (acc + prev,)
    for piece, wp, tag in zip((dv_swa, dq_sb, dk_sb, dv_sb), w["w_in2_parts"], ("vswa", "qsb", "ksb", "vsb")):
        gr["w_in2"].append(_matmul(sv["h"], piece, "tn", [BF16], "b_w_in_" + tag))
        dh = _matmul(piece, wp, "nt", [F32], "b_proj_" + tag, epilogue=add_prev, extras=[dh])
    gr["w_in3"] = _matmul(sv["h"], dlogit, "tn", [BF16], "b_w_in_gate")
    dh = _matmul(dlogit, w["w_in3"], "nt", [F32], "b_proj_gate", epilogue=add_prev, extras=[dh])
    dx, gr["g_mix_pre"] = _rowwise(f_norm_bwd_res, [sv["x"], dh, dx1], [w["g_mix_pre"]], [(D_MODEL, F32)], [D_MODEL], "b_norm_mix_pre")
    return dx, gr


def _local_step(x, positions, loss_target, full):
    mc, mu, md = _rope_tables(positions, MLA_NOPE, MLA_ROPE, True)
    kc, ku, kd = _rope_tables(positions, MLA_NOPE, MLA_ROPE, False)
    sc, su, sd = _rope_tables(positions, 0, SWA_HEAD_DIM, False)
    mla_tab = {"q": (mc, mu, md), "k": (kc, ku, kd), "q_inv": (mc, -mu, -md), "k_inv": (kc, -ku, -kd)}
    swa_tab = {"f": (sc, su, sd), "inv": (sc, -su, -sd)}
    tabs = (mla_tab, swa_tab)

    big = {n: full[n].astype(BF16) for n in SHARDED}
    w1, w2, w3 = _w_in_internal(big["w_in"])
    uk, uv = _w_ukv_internal(big["w_ukv"])
    stacks = {
        "w_in1": w1, "w_in2": jnp.concatenate(w2, axis=-1), "w_in3": w3,
        "w_uq": _pad_cols(big["w_uq"], MLA_HEADS, MLA_NOPE + MLA_ROPE), "w_ukv_k": uk, "w_ukv_v": uv,
        "w_o_mla": _pad_rows(big["w_o_mla"], 8, 64), "w_o_swa": _pad_rows(big["w_o_swa"], 8, 64),
        "w_o_sb": _pad_rows(big["w_o_sb"], 8, 64), "w_out": big["w_out"], "w_up": big["w_up"], "w_down": big["w_down"],
    }
    layers = []
    for l in range(DEPTH):
        layers.append({
            **{n: (t, l) for n, t in stacks.items()}, "w_in2_parts": [(t, l) for t in w2],
            "g_mix_pre": full["g_mix_pre"][l][None], "b_gate": full["b_gate"][l][None],
            "g_q_lat": full["g_q_lat"][l][None], "g_kv_lat": full["g_kv_lat"][l][None],
            "g_mix_post": full["g_mix_post"][l][None], "g_mlp_pre": full["g_mlp_pre"][l][None],
            "g_mlp_post": full["g_mlp_post"][l][None],
            "sink_b": jnp.repeat(full["swa_sinks"][l], LANES)[None],
        })

    saved = []
    h = x
    for l in range(DEPTH):
        h, sv = _layer_fwd(h, layers[l], tabs)
        saved.append(sv)

    def f_loss(rows, consts):
        err = rows[0] - rows[1]
        return [err * (1.0 / D_MODEL)], [jnp.sum(err * err, axis=1, keepdims=True)]

    dy, sq = _rowwise(f_loss, [h, loss_target], [], [(D_MODEL, F32)], [1], "loss_head")
    loss_part = sq * (0.5 / D_MODEL)

    grs = [None] * DEPTH
    d = dy
    for l in reversed(range(DEPTH)):
        d, grs[l] = _layer_bwd(d, layers[l], saved[l], tabs)
    st = lambda pick: jnp.stack([pick(grs[l]) for l in range(DEPTH)])
    vec = lambda n: st(lambda gr: gr[n][0] if gr[n].ndim == 2 else gr[n])
    stacked = {n: vec(n) for n in SMALL}
    stacked.update({n: st(lambda gr: gr[n]) for n in ("w_out", "w_up", "w_down")})
    stacked["w_in"] = _w_in_reference(st(lambda gr: gr["w_in1"]), [st(lambda gr: gr["w_in2"][p]) for p in range(4)],
                                      st(lambda gr: gr["w_in3"]))
    stacked["w_uq"] = _unpad_cols(st(lambda gr: gr["w_uq"]), MLA_HEADS, MLA_NOPE + MLA_ROPE)
    stacked["w_ukv"] = _w_ukv_reference(st(lambda gr: gr["w_ukv_k"]), st(lambda gr: gr["w_ukv_v"]))
    for n in ("w_o_mla", "w_o_swa", "w_o_sb"):
        stacked[n] = _unpad_rows(st(lambda gr: gr[n]), 8, 64)
    return loss_part, d, stacked


def _lane_padded(width):
    return max(width, LANES)


def _rows_of(a, dtype):
    extra = _lane_padded(a.shape[-1]) - a.shape[-1]
    if extra:
        a = jnp.pad(a, [(0, 0)] * (a.ndim - 1) + [(0, extra)])
    return a.astype(dtype).reshape(-1, LANES)


def _pack(shards, small, dtype):
    parts = [_rows_of(shards[n], dtype) for n in SHARDED]
    if small is not None:
        parts += [_rows_of(small[n], dtype) for n in SMALL]
    slab = jnp.concatenate(parts, axis=0)
    pad = (-slab.shape[0]) % SLAB_ROW_ALIGN
    return jnp.pad(slab, ((0, pad), (0, 0)))


def _unpack(slab, shard_shapes, small_shapes):
    out, r = {}, 0
    shapes = [(n, shard_shapes[n]) for n in SHARDED]
    if small_shapes is not None:
        shapes += [(n, small_shapes[n]) for n in SMALL]
    for n, shape in shapes:
        wide = shape[:-1] + (_lane_padded(shape[-1]),)
        rows = int(np.prod(wide)) // LANES
        out[n] = slab[r:r + rows].reshape(wide)[..., :shape[-1]]
        r += rows
    return out


def _chip_exchange(src, name):
    rows = src.shape[-2]

    def body(src_ref, out_ref, send_sems, recv_sems):
        x, y, c = lax.axis_index("x"), lax.axis_index("y"), lax.axis_index("c")
        me = 2 * x + y
        chips = [(1 - x, y), (x, 1 - y), (1 - x, 1 - y)]
        sends = []
        for k, (cx, cy) in enumerate(chips):
            cp = pltpu.make_async_remote_copy(
                src_ref=src_ref.at[2 * cx + cy], dst_ref=out_ref.at[me], send_sem=send_sems.at[k],
                recv_sem=recv_sems.at[k], device_id=(cx, cy, c), device_id_type=pl.DeviceIdType.MESH)
            cp.start()
            sends.append(cp)
        for k, (cx, cy) in enumerate(chips):
            pltpu.make_async_remote_copy(
                src_ref=src_ref.at[me], dst_ref=out_ref.at[2 * cx + cy], send_sem=send_sems.at[k],
                recv_sem=recv_sems.at[k], device_id=(cx, cy, c), device_id_type=pl.DeviceIdType.MESH).wait_recv()
        for cp in sends:
            cp.wait_send()

    return pl.pallas_call(
        body,
        name=name,
        in_specs=[pl.BlockSpec(memory_space=pl.ANY)],
        out_specs=pl.BlockSpec(memory_space=pl.ANY),
        out_shape=jax.ShapeDtypeStruct((N_CHIPS, rows, LANES), src.dtype),
        scratch_shapes=[pltpu.SemaphoreType.DMA((3,)), pltpu.SemaphoreType.DMA((3,))],
    )(src)


def _half_rows(c, half):
    return pl.ds(pl.multiple_of(c * half, SLAB_ROW_ALIGN // 2), half)


def _gather_weights(src, name):
    rows = src.shape[0]
    half = rows // 2

    def body(src_ref, out_ref, send_sems, recv_sems):
        x, y, c = lax.axis_index("x"), lax.axis_index("y"), lax.axis_index("c")
        me = 2 * x + y
        chips = [(1 - x, y), (x, 1 - y), (1 - x, 1 - y)]

        def copy(k, src_view, slab, part, to):
            return pltpu.make_async_remote_copy(
                src_ref=src_view, dst_ref=out_ref.at[slab, _half_rows(part, half), :], send_sem=send_sems.at[k],
                recv_sem=recv_sems.at[k], device_id=to, device_id_type=pl.DeviceIdType.MESH)

        sends = [copy(k, src_ref.at[_half_rows(c, half), :], me, c, (cx, cy, c)) for k, (cx, cy) in enumerate(chips)]
        for cp in sends:
            cp.start()
        for k, (cx, cy) in enumerate(chips):
            j = 2 * cx + cy
            landed = out_ref.at[j, _half_rows(c, half), :]
            copy(k, landed, j, c, (cx, cy, c)).wait_recv()
            fwd = copy(3 + k, landed, j, c, (x, y, 1 - c))
            fwd.start()
            sends.append(fwd)
        for k, (cx, cy) in enumerate(chips):
            j = 2 * cx + cy
            copy(3 + k, out_ref.at[j, _half_rows(1 - c, half), :], j, 1 - c, (x, y, 1 - c)).wait_recv()
        for cp in sends:
            cp.wait_send()

    return pl.pallas_call(
        body,
        name=name,
        in_specs=[pl.BlockSpec(memory_space=pl.ANY)],
        out_specs=pl.BlockSpec(memory_space=pl.ANY),
        out_shape=jax.ShapeDtypeStruct((N_CHIPS, rows, LANES), src.dtype),
        scratch_shapes=[pltpu.SemaphoreType.DMA((6,)), pltpu.SemaphoreType.DMA((6,))],
    )(src)


def _sibling_halves(src, name):
    n, rows, _ = src.shape
    half = rows // 2

    def body(src_ref, out_ref, send_sem, recv_sem):
        x, y, c = lax.axis_index("x"), lax.axis_index("y"), lax.axis_index("c")
        cp = pltpu.make_async_remote_copy(
            src_ref=src_ref.at[:, _half_rows(1 - c, half), :], dst_ref=out_ref, send_sem=send_sem, recv_sem=recv_sem,
            device_id=(x, y, 1 - c), device_id_type=pl.DeviceIdType.MESH)
        cp.start()
        cp.wait()

    return pl.pallas_call(
        body,
        name=name,
        in_specs=[pl.BlockSpec(memory_space=pl.ANY)],
        out_specs=pl.BlockSpec(memory_space=pl.ANY),
        out_shape=jax.ShapeDtypeStruct((n, half, LANES), src.dtype),
        scratch_shapes=[pltpu.SemaphoreType.DMA, pltpu.SemaphoreType.DMA],
    )(src)


def _sibling_join(src, name):
    half = src.shape[0]

    def body(src_ref, out_ref, send_sem, recv_sem):
        x, y, c = lax.axis_index("x"), lax.axis_index("y"), lax.axis_index("c")
        cp = pltpu.make_async_remote_copy(
            src_ref=src_ref, dst_ref=out_ref.at[_half_rows(c, half), :], send_sem=send_sem, recv_sem=recv_sem,
            device_id=(x, y, 1 - c), device_id_type=pl.DeviceIdType.MESH)
        cp.start()
        pltpu.make_async_remote_copy(
            src_ref=src_ref, dst_ref=out_ref.at[_half_rows(1 - c, half), :], send_sem=send_sem, recv_sem=recv_sem,
            device_id=(x, y, 1 - c), device_id_type=pl.DeviceIdType.MESH).wait_recv()
        cp.wait_send()

    return pl.pallas_call(
        body,
        name=name,
        in_specs=[pl.BlockSpec(memory_space=pl.ANY)],
        out_specs=pl.BlockSpec(memory_space=pl.ANY),
        out_shape=jax.ShapeDtypeStruct((2 * half, LANES), src.dtype),
        scratch_shapes=[pltpu.SemaphoreType.DMA, pltpu.SemaphoreType.DMA],
    )(src)


SUM_ROWS = 1024


def _pair_sum(mine, theirs, c, name):
    n, half, _ = theirs.shape
    blocks = half // SUM_ROWS

    def body(c_ref, a_ref, b_ref, o_ref):
        o_ref[...] = (a_ref[...].astype(F32) + b_ref[...].astype(F32)).astype(o_ref.dtype)

    return pl.pallas_call(
        body,
        name=name,
        grid_spec=pltpu.PrefetchScalarGridSpec(
            num_scalar_prefetch=1,
            grid=(blocks,),
            in_specs=[pl.BlockSpec((n, SUM_ROWS, LANES), lambda i, c_ref: (0, c_ref[0] * blocks + i, 0)),
                      pl.BlockSpec((n, SUM_ROWS, LANES), lambda i, c_ref: (0, i, 0))],
            out_specs=pl.BlockSpec((n, SUM_ROWS, LANES), lambda i, c_ref: (0, i, 0)),
        ),
        out_shape=jax.ShapeDtypeStruct((n, half, LANES), BF16),
        compiler_params=_params(("arbitrary",)),
    )(jnp.reshape(c, (1,)).astype(jnp.int32), mine, theirs)


def _sum_chips(own, landed, me, name):
    rows = landed.shape[1]

    def body(me_ref, a_ref, b_ref, o_ref):
        t = [jnp.where(me_ref[0] == j, a_ref[j], b_ref[j]).astype(F32) for j in range(N_CHIPS)]
        o_ref[...] = ((t[0] + t[1]) + t[2]) + t[3]

    slabs = pl.BlockSpec((N_CHIPS, SUM_ROWS, LANES), lambda i, me_ref: (0, i, 0))
    return pl.pallas_call(
        body,
        name=name,
        grid_spec=pltpu.PrefetchScalarGridSpec(
            num_scalar_prefetch=1,
            grid=(rows // SUM_ROWS,),
            in_specs=[slabs, slabs],
            out_specs=pl.BlockSpec((SUM_ROWS, LANES), lambda i, me_ref: (i, 0)),
        ),
        out_shape=jax.ShapeDtypeStruct((rows, LANES), F32),
        compiler_params=_params(("arbitrary",)),
    )(jnp.reshape(me, (1,)).astype(jnp.int32), own, landed)


def _adamw(w, m, v, g, name):
    shape = w.shape
    flat = lambda a: a.reshape(-1, shape[-1])

    def fn(rows, consts):
        w_, m_, v_, g_ = rows
        m_new = ADAM_B1 * m_ + (1.0 - ADAM_B1) * g_
        v_new = ADAM_B2 * v_ + (1.0 - ADAM_B2) * (g_ * g_)
        m_hat = m_new / (1.0 - ADAM_B1 ** ADAM_STEP)
        v_hat = v_new / (1.0 - ADAM_B2 ** ADAM_STEP)
        delta = -ADAM_LR * (m_hat / (jnp.sqrt(v_hat) + ADAM_EPS) + ADAM_WD * w_)
        return [delta, m_new, v_new], []

    outs = _rowwise(fn, [flat(w), flat(m), flat(v), flat(g)], [], [(shape[-1], F32)] * 3, [], name)
    return [o.reshape(shape) for o in outs]


def kernel(x, positions, g_mix_pre, w_in, b_gate, g_q_lat, g_kv_lat, w_uq, w_ukv, swa_sinks, w_o_mla, w_o_swa, w_o_sb, w_out, g_mix_post, g_mlp_pre, w_up, w_down, g_mlp_post, loss_target, m_g_mix_pre, m_w_in, m_b_gate, m_g_q_lat, m_g_kv_lat, m_w_uq, m_w_ukv, m_swa_sinks, m_w_o_mla, m_w_o_swa, m_w_o_sb, m_w_out, m_g_mix_post, m_g_mlp_pre, m_w_up, m_w_down, m_g_mlp_post, v_g_mix_pre, v_w_in, v_b_gate, v_g_q_lat, v_g_kv_lat, v_w_uq, v_w_ukv, v_swa_sinks, v_w_o_mla, v_w_o_swa, v_w_o_sb, v_w_out, v_g_mix_post, v_g_mlp_pre, v_w_up, v_w_down, v_g_mlp_post):
    given = dict(locals())
    wts = {n: given[n] for n in WEIGHTS}
    mom_m = {n: given["m_" + n] for n in WEIGHTS}
    mom_v = {n: given["v_" + n] for n in WEIGHTS}
    shard_shapes = {n: wts[n].shape for n in SHARDED}
    small_shapes = {n: wts[n].shape for n in SMALL}

    me = 2 * lax.axis_index("x") + lax.axis_index("y")
    core = lax.axis_index("c")
    gathered = _gather_weights(_pack(wts, None, BF16), "gather_weights")
    full = {n: wts[n] for n in SMALL}
    per_chip = [_unpack(gathered[j], shard_shapes, None) for j in range(N_CHIPS)]
    for n in SHARDED:
        own = wts[n].astype(BF16)
        full[n] = jnp.concatenate([jnp.where(me == j, own, per_chip[j][n]) for j in range(N_CHIPS)], axis=SHARD_AXIS[n])

    loss_part, grad_x, grads = _local_step(x[0], positions[0], loss_target[0], full)
    loss = lax.psum(loss_part[0, 0], ("x", "y", "c"))

    small_g = {n: grads[n] for n in SMALL}
    slabs = []
    for j in range(N_CHIPS):
        shard = {n: jnp.split(grads[n], N_CHIPS, axis=SHARD_AXIS[n])[j] for n in SHARDED}
        slabs.append(_pack(shard, small_g, BF16))
    per_chip_g = jnp.stack(slabs)
    theirs = _sibling_halves(per_chip_g, "pair_grads")
    pair = _pair_sum(per_chip_g, theirs, core, "sum_pair")
    landed = _chip_exchange(pair, "scatter_grads")
    my_half = _sum_chips(pair, landed, me, "sum_chips")
    g_slab = lax.dynamic_update_slice(_sibling_join(my_half, "join_grads"), my_half, (core * my_half.shape[0], 0))

    g = _unpack(g_slab, shard_shapes, small_shapes)
    stepped = {n: _adamw(wts[n], mom_m[n], mom_v[n], g[n], "adamw_" + n) for n in WEIGHTS}
    outs = [loss, grad_x[None]] + [g[n] for n in WEIGHTS]
    for part in range(3):
        outs += [stepped[n][part] for n in WEIGHTS]
    return tuple(outs)
```

```python
import numpy as np
import jax
import jax.numpy as jnp
from jax import lax
from jax.experimental import pallas as pl
from jax.experimental.pallas import tpu as pltpu

F32 = jnp.float32
BF16 = jnp.bfloat16

D_MODEL = 1024
DEPTH = 4
MLA_HEADS, MLA_Q_LORA, MLA_KV_LORA, MLA_NOPE, MLA_ROPE, MLA_V = 8, 256, 128, 64, 32, 64
SWA_HEADS, SWA_KV_HEADS, SWA_HEAD_DIM, SWA_WINDOW = 8, 2, 64, 128
SB_HEADS, SB_HEAD_DIM = 8, 64
D_FF = 4 * D_MODEL
ROPE_THETA = 10000.0
EPS = 1e-6
SPLIT_SIZES = (256, 128, 32, 512, 128, 128, 512, 512, 512, 3 * D_MODEL)
SPLIT_POINTS = [int(v) for v in np.cumsum(SPLIT_SIZES)[:-1]]

ADAM_LR, ADAM_B1, ADAM_B2, ADAM_EPS, ADAM_WD, ADAM_STEP = 0.001, 0.9, 0.999, 1e-08, 0.01, 10

LANES = 128
V7X_VMEM_BYTES = 64 * 1024 * 1024
VMEM_LIMIT = V7X_VMEM_BYTES - 8 * 1024 * 1024
MATMUL_VMEM_BUDGET = 36 * 1024 * 1024
N_CHIPS = 4
SLAB_ROW_ALIGN = 2048

P1_W = 256 + 128 + 128 + 1024 + 256
P2_W = 256 + 1024 + 1024 + 1024
P3_W = 3 * D_MODEL

SHARDED = ("w_in", "w_uq", "w_ukv", "w_o_mla", "w_o_swa", "w_o_sb", "w_out", "w_up", "w_down")
SHARD_AXIS = {"w_in": 2, "w_uq": 2, "w_ukv": 2, "w_o_mla": 2, "w_o_swa": 2, "w_o_sb": 2, "w_out": 1, "w_up": 2, "w_down": 1}
SMALL = ("g_mix_pre", "b_gate", "g_q_lat", "g_kv_lat", "swa_sinks", "g_mix_post", "g_mlp_pre", "g_mlp_post")
WEIGHTS = ("g_mix_pre", "w_in", "b_gate", "g_q_lat", "g_kv_lat", "w_uq", "w_ukv", "swa_sinks", "w_o_mla", "w_o_swa",
           "w_o_sb", "w_out", "g_mix_post", "g_mlp_pre", "w_up", "w_down", "g_mlp_post")

NN = (((1,), (0,)), ((), ()))
NT = (((1,), (1,)), ((), ()))
TN = (((0,), (0,)), ((), ()))


def _dot(a, b, dims):
    return lax.dot_general(a, b, dims, preferred_element_type=F32)


def _params(sem):
    return pltpu.CompilerParams(dimension_semantics=sem, vmem_limit_bytes=VMEM_LIMIT)


def _largest_tile(n, cap):
    if n <= cap:
        return n
    best = LANES
    for t in range(LANES, cap + 1, LANES):
        if n % t == 0:
            best = t
    return best


def _matmul_tiles(M, N, K, a_bytes, b_bytes, out_bytes, extra_bytes):
    tn = _largest_tile(N, 1792)
    tm = _largest_tile(M, 1024 if tn <= 1024 else 512)
    tk = _largest_tile(K, 2048)

    def need(tm_, tk_):
        acc = 4 * tm_ * tn if tk_ < K else 0
        return 2 * (tm_ * tk_ * a_bytes + tk_ * tn * b_bytes + tm_ * tn * (out_bytes + extra_bytes)) + acc

    while need(tm, tk) > MATMUL_VMEM_BUDGET:
        if tk >= tm and tk % 256 == 0:
            tk //= 2
        elif tm % 256 == 0:
            tm //= 2
        else:
            break
    return tm, tn, tk


def _matmul(a, b, mode, out_dtypes, name, epilogue=None, extras=(), row_extras=()):
    b_layer = None
    if isinstance(b, tuple):
        b, b_layer = b
    b_shape = b.shape[-2:]
    if mode == "nn":
        (M, K), (K2, N) = a.shape, b_shape
    elif mode == "nt":
        (M, K), (N, K2) = a.shape, b_shape
    else:
        (K, M), (K2, N) = a.shape, b_shape
    assert K == K2, (name, a.shape, b.shape)
    tm, tn, tk = _matmul_tiles(
        M, N, K, a.dtype.itemsize, b.dtype.itemsize, sum(jnp.dtype(d).itemsize for d in out_dtypes),
        sum(e.dtype.itemsize for e in extras))
    assert M % tm == 0 and N % tn == 0 and K % tk == 0, (name, M, N, K, tm, tn, tk)
    nk = K // tk
    if mode == "tn":
        a_spec = pl.BlockSpec((tk, tm), lambda i, j, k: (k, i))
    else:
        a_spec = pl.BlockSpec((tm, tk), lambda i, j, k: (i, k))
    b_block, b_index = ((tn, tk), lambda i, j, k: (j, k)) if mode == "nt" else ((tk, tn), lambda i, j, k: (k, j))
    if b_layer is None:
        b_spec = pl.BlockSpec(b_block, b_index)
    else:
        b_spec = pl.BlockSpec((None,) + b_block, lambda i, j, k: (b_layer,) + b_index(i, j, k))
    dims = {"nn": NN, "nt": NT, "tn": TN}[mode]
    n_ex, n_rex, n_out = len(extras), len(row_extras), len(out_dtypes)

    def body(*refs):
        a_ref, b_ref = refs[:2]
        ex = refs[2:2 + n_ex]
        rex = refs[2 + n_ex:2 + n_ex + n_rex]
        outs = refs[2 + n_ex + n_rex:2 + n_ex + n_rex + n_out]

        def finish(total):
            res = (total,) if epilogue is None else epilogue(total, *[e[...] for e in ex], *[e[...] for e in rex])
            for o, r in zip(outs, res):
                o[...] = r.astype(o.dtype)

        part = _dot(a_ref[...].astype(BF16), b_ref[...].astype(BF16), dims)
        if nk == 1:
            finish(part)
            return
        acc = refs[-1]
        k = pl.program_id(2)

        @pl.when(k == 0)
        def _():
            acc[...] = part

        @pl.when(k > 0)
        def _():
            acc[...] += part

        @pl.when(k == nk - 1)
        def _():
            finish(acc[...])

    in_specs = [a_spec, b_spec]
    in_specs += [pl.BlockSpec((tm, tn), lambda i, j, k: (i, j)) for _ in extras]
    in_specs += [pl.BlockSpec((1, tn), lambda i, j, k: (0, j)) for _ in row_extras]
    out = pl.pallas_call(
        body,
        name=name,
        grid=(M // tm, N // tn, nk),
        in_specs=in_specs,
        out_specs=[pl.BlockSpec((tm, tn), lambda i, j, k: (i, j)) for _ in out_dtypes],
        out_shape=[jax.ShapeDtypeStruct((M, N), dt) for dt in out_dtypes],
        scratch_shapes=[pltpu.VMEM((tm, tn), F32)] if nk > 1 else [],
        compiler_params=_params(("parallel", "parallel", "arbitrary")),
    )(a, b, *extras, *row_extras)
    return out[0] if n_out == 1 else out


ROWWISE_ROW_BYTES = 16 * 1024


def _rowwise(fn, rows, consts, out_defs, sum_widths, name):
    R = rows[0].shape[0]
    per_row = sum(r.shape[1] * r.dtype.itemsize for r in rows) + sum(w * jnp.dtype(dt).itemsize for w, dt in out_defs)
    bm = 512 if per_row <= ROWWISE_ROW_BYTES else 256
    while R % bm:
        bm //= 2
    bm = max(bm, 1)
    n_r, n_c, n_o = len(rows), len(consts), len(out_defs)
    n_s = len(sum_widths)

    def body(*refs):
        r_in = refs[:n_r]
        c_in = refs[n_r:n_r + n_c]
        o_refs = refs[n_r + n_c:n_r + n_c + n_o]
        s_refs = refs[n_r + n_c + n_o:]
        outs, sums = fn([r[...] for r in r_in], [c[...] for c in c_in])
        for o, val in zip(o_refs, outs):
            o[...] = val.astype(o.dtype)
        if n_s:
            @pl.when(pl.program_id(0) == 0)
            def _():
                for s in s_refs:
                    s[...] = jnp.zeros_like(s)

            for s, val in zip(s_refs, sums):
                s[...] += jnp.sum(val, axis=0, keepdims=True)

    in_specs = [pl.BlockSpec((bm, r.shape[1]), lambda i: (i, 0)) for r in rows]
    in_specs += [pl.BlockSpec(c.shape, lambda i: (0, 0)) for c in consts]
    out_specs = [pl.BlockSpec((bm, w), lambda i: (i, 0)) for w, _ in out_defs]
    out_specs += [pl.BlockSpec((1, w), lambda i: (0, 0)) for w in sum_widths]
    out_shape = [jax.ShapeDtypeStruct((R, w), dt) for w, dt in out_defs]
    out_shape += [jax.ShapeDtypeStruct((1, w), F32) for w in sum_widths]
    return pl.pallas_call(
        body,
        name=name,
        grid=(R // bm,),
        in_specs=in_specs,
        out_specs=out_specs,
        out_shape=out_shape,
        compiler_params=_params(("arbitrary",)),
    )(*rows, *consts)


def _rms(x, g):
    r = lax.rsqrt(jnp.mean(x * x, axis=-1, keepdims=True) + EPS)
    return x * r * g


def _rms_bwd(x, g, dy):
    r = lax.rsqrt(jnp.mean(x * x, axis=-1, keepdims=True) + EPS)
    n = x * r
    dn = dy * g
    dx = r * (dn - n * jnp.mean(dn * n, axis=-1, keepdims=True))
    return dx, dy * n


def _rope(x, c, s_up, s_dn, half):
    return x * c + pltpu.roll(x, half, 1) * s_up + pltpu.roll(x, LANES - half, 1) * s_dn


def _rope_tables(positions, lo, d, nope_pass):
    S = positions.shape[0]
    half = d // 2
    inv = 1.0 / (ROPE_THETA ** (jnp.arange(0, d, 2, dtype=F32) / d))
    ang = positions.astype(F32)[:, None] * inv
    cos, sin = jnp.cos(ang), jnp.sin(ang)
    z = lambda n: jnp.zeros((S, n), F32)
    head = jnp.ones((S, lo), F32) if nope_pass else z(lo)
    tail = LANES - lo - d
    c = jnp.concatenate([head, cos, cos, z(tail)], axis=1)
    s_up = jnp.concatenate([z(lo), z(half), sin, z(tail)], axis=1)
    s_dn = jnp.concatenate([z(lo), -sin, z(half), z(tail)], axis=1)
    return c, s_up, s_dn


MLA_FWD_CFG = (2, 1024)
MLA_BWD_CFG = (2, 512)
SB_FWD_CFG = (2, 256)
SB_BWD_CFG = (4, 256)


def _tile_mask(bk, strict):
    row = lax.broadcasted_iota(jnp.int32, (bk, bk), 0)
    col = lax.broadcasted_iota(jnp.int32, (bk, bk), 1)
    return (col < row) if strict else (col <= row)


def _att_layout(S, cfg):
    nch, bk = cfg
    bq = nch * bk
    assert S % bq == 0, (S, cfg)
    rows = [slice(r * bk, (r + 1) * bk) for r in range(nch)]
    q_spec = lambda off=0: pl.BlockSpec((bq, LANES), lambda h, i: (i, off + h))
    kv_spec = lambda off=0: pl.BlockSpec((S, LANES), lambda h, i: (0, off + h))
    return bq, rows, q_spec, kv_spec


def _total(terms):
    terms = list(terms)
    out = terms[0]
    for t in terms[1:]:
        out = out + t
    return out


def _walk(nch, i, step, carry, leftward, alive=None):
    everyone = range(nch)
    if leftward:
        for d in reversed(everyone):
            carry = step(nch * i + d, carry, range(d, nch), {d})
        if alive is None:
            return lax.fori_loop(0, nch * i, lambda t, c: step(nch * i - 1 - t, c, everyone, set()), carry)
        more = lambda tc: jnp.logical_and(tc[0] < nch * i, alive(tc[1]))
        left = lambda tc: (tc[0] + 1, step(nch * i - 1 - tc[0], tc[1], everyone, set()))
        return lax.while_loop(more, left, (jnp.int32(0), carry))[1]
    carry = lax.fori_loop(0, nch * i, lambda kb, c: step(kb, c, everyone, set()), carry)
    for d in everyone:
        carry = step(nch * i + d, carry, range(d, nch), {d})
    return carry


ONES_LANE = MLA_V


def _softmax_attn_fwd(q, k, v, heads, name, q_off=0, k_off=0, v_off=0):
    S = q.shape[0]
    nch, bk = MLA_FWD_CFG
    bq, rows, q_spec, kv_spec = _att_layout(S, MLA_FWD_CFG)

    def body(q_ref, k_ref, v_ref, o_ref, lse_ref):
        i = pl.program_id(1)
        qs = [q_ref[rw, :] for rw in rows]

        def step(kb, cs, active, masked):
            off = pl.multiple_of(kb * bk, bk)
            ks, vs = k_ref[pl.ds(off, bk), :], v_ref[pl.ds(off, bk), :]
            A = list(active)
            s = {r: _dot(qs[r], ks, NT) for r in A}
            s = {r: (jnp.where(_tile_mask(bk, False), s[r], -1e30) if r in masked else s[r]) for r in A}
            m_new = {r: jnp.maximum(cs[r][0], jnp.max(s[r], axis=1, keepdims=True)) for r in A}
            p = {r: jnp.exp(s[r] - m_new[r]) for r in A}
            alpha = {r: jnp.exp(cs[r][0] - m_new[r]) for r in A}
            new = list(cs)
            for r in A:
                new[r] = (m_new[r], alpha[r] * cs[r][1] + _dot(p[r].astype(BF16), vs, NN))
            return tuple(new)

        init = (jnp.full((bk, 1), -1e30, F32), jnp.zeros((bk, LANES), F32))
        cs = _walk(nch, i, step, tuple(init for _ in rows), False)
        for r, (m, acc) in enumerate(cs):
            l = acc[:, ONES_LANE:ONES_LANE + 1]
            o_ref[rows[r], :] = (acc / l).astype(o_ref.dtype)
            lse_ref[rows[r], :] = m + jnp.log(l)

    return pl.pallas_call(
        body,
        name=name,
        grid=(heads, S // bq),
        in_specs=[q_spec(q_off), kv_spec(k_off), kv_spec(v_off)],
        out_specs=[q_spec(), pl.BlockSpec((None, bq, 1), lambda h, i: (h, i, 0))],
        out_shape=[jax.ShapeDtypeStruct((S, heads * LANES), BF16), jax.ShapeDtypeStruct((heads, S, 1), F32)],
        compiler_params=_params(("parallel", "arbitrary")),
    )(q, k, v)


def _softmax_attn_bwd(q, k, v, o, lse, do, heads, scale, name, q_off=0, k_off=0, v_off=0):
    S = q.shape[0]
    nch, bk = MLA_BWD_CFG
    bq, rows, q_spec, kv_spec = _att_layout(S, MLA_BWD_CFG)

    def body(q_ref, k_ref, v_ref, o_ref, lse_ref, do_ref, dq_ref, dk_ref, dv_ref):
        i = pl.program_id(1)

        @pl.when(i == 0)
        def _():
            dk_ref[...] = jnp.zeros_like(dk_ref)
            dv_ref[...] = jnp.zeros_like(dv_ref)

        qs = [q_ref[rw, :] for rw in rows]
        dos = [do_ref[rw, :] for rw in rows]
        lses = [lse_ref[rw, :] for rw in rows]
        deltas = [jnp.sum(dos[r].astype(F32) * o_ref[rows[r], :].astype(F32), axis=1, keepdims=True) for r in range(nch)]

        def step(kb, dqs, active, masked):
            off = pl.multiple_of(kb * bk, bk)
            ks, vs = k_ref[pl.ds(off, bk), :], v_ref[pl.ds(off, bk), :]
            A = list(active)
            s = {r: _dot(qs[r], ks, NT) for r in A}
            s = {r: (jnp.where(_tile_mask(bk, False), s[r], -1e30) if r in masked else s[r]) for r in A}
            p = {r: jnp.exp(s[r] - lses[r]) for r in A}
            dp = {r: _dot(dos[r], vs, NT) for r in A}
            ds = {r: (p[r] * (dp[r] - deltas[r])).astype(BF16) for r in A}
            dv_c = _total(_dot(p[r].astype(BF16), dos[r], TN) for r in A)
            dk_c = _total(_dot(ds[r], qs[r], TN) for r in A)
            dk_ref[pl.ds(off, bk), :] += dk_c
            dv_ref[pl.ds(off, bk), :] += dv_c
            new = list(dqs)
            for r in A:
                new[r] = dqs[r] + _dot(ds[r], ks, NN)
            return tuple(new)

        dqs = _walk(nch, i, step, tuple(jnp.zeros((bk, LANES), F32) for _ in rows), False)
        for r in range(nch):
            dq_ref[rows[r], :] = dqs[r] * scale

    return pl.pallas_call(
        body,
        name=name,
        grid=(heads, S // bq),
        in_specs=[q_spec(q_off), kv_spec(k_off), kv_spec(v_off), q_spec(),
                  pl.BlockSpec((None, bq, 1), lambda h, i: (h, i, 0)), q_spec()],
        out_specs=[q_spec(), kv_spec(), kv_spec()],
        out_shape=[jax.ShapeDtypeStruct((S, heads * LANES), F32)] * 3,
        compiler_params=_params(("parallel", "arbitrary")),
    )(q, k, v, o, lse, do)


def _tri(n, inclusive):
    r = lax.broadcasted_iota(jnp.int32, (n, n), 0)
    c = lax.broadcasted_iota(jnp.int32, (n, n), 1)
    return jnp.where((r >= c) if inclusive else (r > c), 1.0, 0.0).astype(BF16)


def _suffix_sum(x, tri, split=True):
    hi = x.astype(BF16)
    if not split:
        return _dot(hi, tri, NN)
    lo = (x - hi.astype(F32)).astype(BF16)
    return _dot(hi, tri, NN) + _dot(lo, tri, NN)


def _sb_logs(z):
    lg = jnp.log(1.0 + jnp.exp(-jnp.abs(z)))
    l1m = -(jnp.maximum(z, 0.0) + lg)
    return l1m, l1m + z


SB_SCALE = SB_HEAD_DIM ** -0.5
assert SB_SCALE == 0.125
SB_DEAD = -110.0


def _sb_alive(cs):
    top = cs[0][0]
    for c in cs[1:]:
        top = jnp.maximum(top, c[0])
    return jnp.max(top) > SB_DEAD


def _sb_attn_fwd(qkv, heads, name, q_off, k_off, v_off):
    S = qkv.shape[0]
    nch, bk = SB_FWD_CFG
    bq, rows, q_spec, kv_spec = _att_layout(S, SB_FWD_CFG)

    def body(q_ref, k_ref, v_ref, o_ref):
        i = pl.program_id(1)
        qs = [q_ref[rw, :] * SB_SCALE for rw in rows]
        tri = _tri(bk, False)

        def step(kb, cs, active, masked):
            off = pl.multiple_of(kb * bk, bk)
            ks, vs = k_ref[pl.ds(off, bk), :], v_ref[pl.ds(off, bk), :]
            A = list(active)
            lg = {r: _sb_logs(_dot(qs[r], ks, NT)) for r in A}
            l1m = {r: (jnp.where(_tile_mask(bk, True), lg[r][0], 0.0) if r in masked else lg[r][0]) for r in A}
            suf = {r: _suffix_sum(l1m[r], tri, split=False) for r in A}
            ex = {r: lg[r][1] + cs[r][0] + suf[r] for r in A}
            ex = {r: (jnp.where(_tile_mask(bk, True), ex[r], -1e30) if r in masked else ex[r]) for r in A}
            ab = {r: jnp.exp(ex[r]).astype(BF16) for r in A}
            new = list(cs)
            for r in A:
                new[r] = (cs[r][0] + jnp.sum(l1m[r], axis=1, keepdims=True), cs[r][1] + _dot(ab[r], vs, NN))
            return tuple(new)

        init = (jnp.zeros((bk, 1), F32), jnp.zeros((bk, LANES), F32))
        cs = _walk(nch, i, step, tuple(init for _ in rows), True, _sb_alive)
        for r in range(nch):
            o_ref[rows[r], :] = cs[r][1]

    return pl.pallas_call(
        body,
        name=name,
        grid=(heads, S // bq),
        in_specs=[q_spec(q_off), kv_spec(k_off), kv_spec(v_off)],
        out_specs=q_spec(),
        out_shape=jax.ShapeDtypeStruct((S, heads * LANES), F32),
        compiler_params=_params(("parallel", "arbitrary")),
    )(qkv, qkv, qkv)


def _sb_attn_bwd(qkv, o, do, heads, name, q_off, k_off, v_off):
    S = qkv.shape[0]
    nch, bk = SB_BWD_CFG
    bq, rows, q_spec, kv_spec = _att_layout(S, SB_BWD_CFG)

    def body(q_ref, k_ref, v_ref, o_ref, do_ref, dq_ref, dk_ref, dv_ref):
        i = pl.program_id(1)

        @pl.when(i == 0)
        def _():
            dk_ref[...] = jnp.zeros_like(dk_ref)
            dv_ref[...] = jnp.zeros_like(dv_ref)

        tri = _tri(bk, False)
        qs = [q_ref[rw, :] * SB_SCALE for rw in rows]
        dos = [do_ref[rw, :] for rw in rows]
        deltas = [jnp.sum(dos[r].astype(F32) * o_ref[rows[r], :], axis=1, keepdims=True) for r in range(nch)]

        def step(kb, cs, active, masked):
            off = pl.multiple_of(kb * bk, bk)
            ks, vs = k_ref[pl.ds(off, bk), :], v_ref[pl.ds(off, bk), :]
            A = list(active)
            lg = {r: _sb_logs(_dot(qs[r], ks, NT)) for r in A}
            l1m = {r: (jnp.where(_tile_mask(bk, True), lg[r][0], 0.0) if r in masked else lg[r][0]) for r in A}
            suf = {r: _suffix_sum(l1m[r], tri, split=False) for r in A}
            ex = {r: lg[r][1] + cs[r][0] + suf[r] for r in A}
            ex = {r: (jnp.where(_tile_mask(bk, True), ex[r], -1e30) if r in masked else ex[r]) for r in A}
            ab = {r: jnp.exp(ex[r]).astype(BF16) for r in A}
            da = {r: _dot(dos[r], vs, NT) for r in A}
            g = {r: ab[r].astype(F32) * da[r] for r in A}
            gs = {r: _suffix_sum(g[r], tri) for r in A}
            beta = {r: jnp.exp(lg[r][1]) for r in A}
            dz = {r: g[r] - beta[r] * (deltas[r] - cs[r][1] - gs[r]) for r in A}
            dz = {r: (jnp.where(_tile_mask(bk, True), dz[r], 0.0) if r in masked else dz[r]) for r in A}
            dzb = {r: dz[r].astype(BF16) for r in A}
            dv_c = _total(_dot(ab[r], dos[r], TN) for r in A)
            dk_c = _total(_dot(dzb[r], qs[r], TN) for r in A)
            dk_ref[pl.ds(off, bk), :] += dk_c
            dv_ref[pl.ds(off, bk), :] += dv_c
            new = list(cs)
            for r in A:
                new[r] = (cs[r][0] + jnp.sum(l1m[r], axis=1, keepdims=True),
                          cs[r][1] + jnp.sum(g[r], axis=1, keepdims=True), cs[r][2] + _dot(dzb[r], ks, NN))
            return tuple(new)

        zcol = jnp.zeros((bk, 1), F32)
        init = (zcol, zcol, jnp.zeros((bk, LANES), F32))
        cs = _walk(nch, i, step, tuple(init for _ in rows), True, _sb_alive)
        for r in range(nch):
            dq_ref[rows[r], :] = cs[r][2] * SB_SCALE

    return pl.pallas_call(
        body,
        name=name,
        grid=(heads, S // bq),
        in_specs=[q_spec(q_off), kv_spec(k_off), kv_spec(v_off), q_spec(), q_spec()],
        out_specs=[q_spec(), kv_spec(), kv_spec()],
        out_shape=[jax.ShapeDtypeStruct((S, heads * LANES), F32)] * 3,
        compiler_params=_params(("parallel", "arbitrary")),
    )(qkv, qkv, qkv, o, do)


SWA_BLK = 128
SWA_GROUP = SWA_HEADS // SWA_KV_HEADS


SWA_NB = 4
SWA_ROWS = SWA_NB * SWA_BLK


def _swa_band_mask(first):
    row = lax.broadcasted_iota(jnp.int32, (SWA_BLK, 2 * SWA_BLK), 0)
    col = lax.broadcasted_iota(jnp.int32, (SWA_BLK, 2 * SWA_BLK), 1)
    return (col > row) & (col <= row + SWA_WINDOW) & (jnp.logical_not(first) | (col >= SWA_BLK))


def _swa_in_specs(v_off):
    gw = SWA_GROUP * LANES
    before = lambda h, n: (jnp.maximum(SWA_NB * n - 1, 0), h)
    return [
        pl.BlockSpec((SWA_ROWS, gw), lambda h, n: (n, h)),
        pl.BlockSpec((SWA_BLK, LANES), before),
        pl.BlockSpec((SWA_ROWS, LANES), lambda h, n: (n, h)),
        pl.BlockSpec((SWA_BLK, LANES), lambda h, n: (jnp.maximum(SWA_NB * n - 1, 0), v_off + h)),
        pl.BlockSpec((SWA_ROWS, LANES), lambda h, n: (n, v_off + h)),
        pl.BlockSpec((1, gw), lambda h, n: (0, h)),
    ]


def _swa_bands(n, kp_ref, kc_ref, vp_ref, vc_ref):
    k_all = jnp.concatenate([kp_ref[...], kc_ref[...]], axis=0)
    v_all = jnp.concatenate([vp_ref[...], vc_ref[...]], axis=0)
    bands = []
    for j in range(SWA_NB):
        rows = slice(j * SWA_BLK, (j + 2) * SWA_BLK)
        bands.append((k_all[rows], v_all[rows], _swa_band_mask((n == 0) if j == 0 else False)))
    return bands


def _swa_fwd(q, k, v, v_off, sink_b, name):
    S = q.shape[0]
    assert S % SWA_ROWS == 0
    scale = SWA_HEAD_DIM ** -0.5
    gw = SWA_GROUP * LANES

    def body(q_ref, kp_ref, kc_ref, vp_ref, vc_ref, sink_ref, o_ref, lse_ref):
        n = pl.program_id(1)
        bands = _swa_bands(n, kp_ref, kc_ref, vp_ref, vc_ref)
        P = [(j, g) for j in range(SWA_NB) for g in range(SWA_GROUP)]
        rows = lambda j: slice(j * SWA_BLK, (j + 1) * SWA_BLK)
        lanes = lambda g: slice(g * LANES, (g + 1) * LANES)
        sk = {g: sink_ref[:, g * LANES:g * LANES + 1] for g in range(SWA_GROUP)}
        s = {(j, g): jnp.where(bands[j][2], _dot(q_ref[rows(j), lanes(g)], bands[j][0], NT) * scale, -1e30) for j, g in P}
        m = {(j, g): jnp.maximum(jnp.max(s[j, g], axis=1, keepdims=True), sk[g]) for j, g in P}
        p = {(j, g): jnp.exp(s[j, g] - m[j, g]) for j, g in P}
        den = {(j, g): jnp.sum(p[j, g], axis=1, keepdims=True) + jnp.exp(sk[g] - m[j, g]) for j, g in P}
        for j, g in P:
            o_ref[rows(j), lanes(g)] = _dot((p[j, g] / den[j, g]).astype(BF16), bands[j][1], NN).astype(o_ref.dtype)
            lse_ref[g, rows(j), :] = m[j, g] + jnp.log(den[j, g])

    return pl.pallas_call(
        body,
        name=name,
        grid=(SWA_KV_HEADS, S // SWA_ROWS),
        in_specs=_swa_in_specs(v_off),
        out_specs=[
            pl.BlockSpec((SWA_ROWS, gw), lambda h, n: (n, h)),
            pl.BlockSpec((SWA_GROUP, SWA_ROWS, 1), lambda h, n: (h, n, 0)),
        ],
        out_shape=[jax.ShapeDtypeStruct((S, SWA_HEADS * LANES), BF16), jax.ShapeDtypeStruct((SWA_HEADS, S, 1), F32)],
        compiler_params=_params(("parallel", "arbitrary")),
    )(q, k, k, v, v, sink_b)


def _swa_bwd(q, k, v, v_off, sink_b, o, lse, do, name):
    S = q.shape[0]
    assert S % SWA_ROWS == 0
    scale = SWA_HEAD_DIM ** -0.5
    gw = SWA_GROUP * LANES

    def body(q_ref, kp_ref, kc_ref, vp_ref, vc_ref, sink_ref, o_ref, lse_ref, do_ref, dq_ref, dk_ref, dv_ref, dsink_ref):
        n = pl.program_id(1)

        @pl.when(n == 0)
        def _():
            dk_ref[...] = jnp.zeros_like(dk_ref)
            dv_ref[...] = jnp.zeros_like(dv_ref)
            dsink_ref[...] = jnp.zeros_like(dsink_ref)

        bands = _swa_bands(n, kp_ref, kc_ref, vp_ref, vc_ref)
        P = [(j, g) for j in range(SWA_NB) for g in range(SWA_GROUP)]
        rows = lambda j: slice(j * SWA_BLK, (j + 1) * SWA_BLK)
        lanes = lambda g: slice(g * LANES, (g + 1) * LANES)
        qs = {(j, g): q_ref[rows(j), lanes(g)] for j, g in P}
        dos = {(j, g): do_ref[rows(j), lanes(g)] for j, g in P}
        lses = {(j, g): lse_ref[g, rows(j), :] for j, g in P}
        delta = {(j, g): jnp.sum(dos[j, g].astype(F32) * o_ref[rows(j), lanes(g)].astype(F32), axis=1, keepdims=True)
                 for j, g in P}
        s = {(j, g): jnp.where(bands[j][2], _dot(qs[j, g], bands[j][0], NT) * scale, -1e30) for j, g in P}
        p = {(j, g): jnp.exp(s[j, g] - lses[j, g]) for j, g in P}
        dp = {(j, g): _dot(dos[j, g], bands[j][1], NT) for j, g in P}
        ds = {(j, g): (p[j, g] * (dp[j, g] - delta[j, g]) * scale).astype(BF16) for j, g in P}
        for j, g in P:
            dq_ref[rows(j), lanes(g)] = _dot(ds[j, g], bands[j][0], NN)
        for g in range(SWA_GROUP):
            p_sink = [jnp.exp(sink_ref[:, g * LANES:g * LANES + 1] - lses[j, g]) * delta[j, g] for j in range(SWA_NB)]
            dsink_ref[:, lanes(g)] += jnp.zeros((1, LANES), F32) - jnp.sum(_total(p_sink), axis=0, keepdims=True)
        dkb = [_total(_dot(ds[j, g], qs[j, g], TN) for g in range(SWA_GROUP)) for j in range(SWA_NB)]
        dvb = [_total(_dot(p[j, g].astype(BF16), dos[j, g], TN) for g in range(SWA_GROUP)) for j in range(SWA_NB)]
        base = pl.multiple_of(n * SWA_ROWS, SWA_ROWS)
        for j in range(SWA_NB):
            own = pl.ds(base + j * SWA_BLK, SWA_BLK)
            after = j + 1 < SWA_NB
            dk_ref[own, :] += dkb[j][SWA_BLK:] + dkb[j + 1][:SWA_BLK] if after else dkb[j][SWA_BLK:]
            dv_ref[own, :] += dvb[j][SWA_BLK:] + dvb[j + 1][:SWA_BLK] if after else dvb[j][SWA_BLK:]

        @pl.when(n > 0)
        def _():
            before = pl.ds(pl.multiple_of(n * SWA_ROWS - SWA_BLK, SWA_BLK), SWA_BLK)
            dk_ref[before, :] += dkb[0][:SWA_BLK]
            dv_ref[before, :] += dvb[0][:SWA_BLK]

    return pl.pallas_call(
        body,
        name=name,
        grid=(SWA_KV_HEADS, S // SWA_ROWS),
        in_specs=_swa_in_specs(v_off) + [
            pl.BlockSpec((SWA_ROWS, gw), lambda h, n: (n, h)),
            pl.BlockSpec((SWA_GROUP, SWA_ROWS, 1), lambda h, n: (h, n, 0)),
            pl.BlockSpec((SWA_ROWS, gw), lambda h, n: (n, h)),
        ],
        out_specs=[
            pl.BlockSpec((SWA_ROWS, gw), lambda h, n: (n, h)),
            pl.BlockSpec((S, LANES), lambda h, n: (0, h)),
            pl.BlockSpec((S, LANES), lambda h, n: (0, h)),
            pl.BlockSpec((1, gw), lambda h, n: (0, h)),
        ],
        out_shape=[
            jax.ShapeDtypeStruct((S, SWA_HEADS * LANES), F32),
            jax.ShapeDtypeStruct((S, SWA_KV_HEADS * LANES), F32),
            jax.ShapeDtypeStruct((S, SWA_KV_HEADS * LANES), F32),
            jax.ShapeDtypeStruct((1, SWA_HEADS * LANES), F32),
        ],
        compiler_params=_params(("parallel", "arbitrary")),
    )(q, k, k, v, v, sink_b, o, lse, do)


def _pad_last(t, width):
    return jnp.pad(t, [(0, 0)] * (t.ndim - 1) + [(0, width - t.shape[-1])])


def _pad_cols(w, heads, real):
    lead = w.shape[:-1]
    return _pad_last(w.reshape(*lead, heads, real), LANES).reshape(*lead, heads * LANES)


def _unpad_cols(g, heads, real):
    lead = g.shape[:-1]
    return g.reshape(*lead, heads, LANES)[..., :real].reshape(*lead, heads * real)


def _pad_rows(w, heads, real):
    lead, n = w.shape[:-2], w.shape[-1]
    w = w.reshape(*lead, heads, real, n)
    return jnp.pad(w, [(0, 0)] * (w.ndim - 2) + [(0, LANES - real), (0, 0)]).reshape(*lead, heads * LANES, n)


def _unpad_rows(g, heads, real):
    lead, n = g.shape[:-2], g.shape[-1]
    return g.reshape(*lead, heads, LANES, n)[..., :real, :].reshape(*lead, heads * real, n)


def _w_in_internal(w_in):
    c_q, c_kv, k_r, q_swa, k_swa, v_swa, q_sb, k_sb, v_sb, gate = jnp.split(w_in, SPLIT_POINTS, axis=-1)
    k_r = jnp.pad(k_r, [(0, 0)] * (k_r.ndim - 1) + [(MLA_NOPE, LANES - MLA_NOPE - MLA_ROPE)])
    w1 = jnp.concatenate([c_q, c_kv, k_r, _pad_cols(q_swa, 8, 64), _pad_cols(k_swa, 2, 64)], axis=-1)
    w2 = [_pad_cols(v_swa, 2, 64), _pad_cols(q_sb, 8, 64), _pad_cols(k_sb, 8, 64), _pad_cols(v_sb, 8, 64)]
    return w1, w2, gate


def _w_in_reference(g1, g2, g3):
    c_q, c_kv, k_r, q_swa, k_swa = jnp.split(g1, [256, 384, 512, 1536], axis=-1)
    v_swa, q_sb, k_sb, v_sb = g2
    return jnp.concatenate([
        c_q, c_kv, k_r[..., MLA_NOPE:MLA_NOPE + MLA_ROPE], _unpad_cols(q_swa, 8, 64), _unpad_cols(k_swa, 2, 64),
        _unpad_cols(v_swa, 2, 64), _unpad_cols(q_sb, 8, 64), _unpad_cols(k_sb, 8, 64), _unpad_cols(v_sb, 8, 64),
        g3], axis=-1)


def _w_ukv_internal(w):
    lead = w.shape[:-1]
    w3 = w.reshape(*lead, MLA_HEADS, MLA_NOPE + MLA_V)
    pad = lambda t: _pad_last(t, LANES).reshape(*lead, MLA_HEADS * LANES)
    return pad(w3[..., :MLA_NOPE]), pad(w3[..., MLA_NOPE:])


def _w_ukv_reference(gk, gv):
    lead = gk.shape[:-1]
    gk = gk.reshape(*lead, MLA_HEADS, LANES)[..., :MLA_NOPE]
    gv = gv.reshape(*lead, MLA_HEADS, LANES)[..., :MLA_V]
    return jnp.concatenate([gk, gv], axis=-1).reshape(*lead, MLA_HEADS * (MLA_NOPE + MLA_V))


def _layer_fwd(x, h, w, tabs, g_next):
    mla_tab, swa_tab = tabs
    sv = {"x": x}

    def f_norm(rows, consts):
        return [_rms(rows[0], consts[0])], []

    def f_res_norm_norm(rows, consts):
        out = rows[0] + _rms(rows[1], consts[0])
        return [out, _rms(out, consts[1])], []

    if h is None:
        (h,) = _rowwise(f_norm, [x], [w["g_mix_pre"]], [(D_MODEL, BF16)], [], "norm_mix_pre")
    p1 = _matmul(h, w["w_in1"], "nn", [F32], "proj_lat")
    p2 = _matmul(h, w["w_in2"], "nn", [BF16], "proj_qkv")
    gates = _matmul(h, w["w_in3"], "nn", [BF16], "proj_gate",
                    epilogue=lambda acc, b: (1.0 / (1.0 + jnp.exp(-(acc + b))),), row_extras=[w["b_gate"]])

    def f_prep(rows, consts):
        t = rows[0]
        gq, gkv = consts[0], consts[1]
        mc, mu, md = rows[1], rows[2], rows[3]
        sc, su, sd = rows[4], rows[5], rows[6]
        cq_n = _rms(t[:, 0:256], gq)
        ckv_n = _rms(t[:, 256:384], gkv)
        kr = _rope(t[:, 384:512], mc, mu, md, MLA_ROPE // 2)
        qs = [_rope(t[:, 512 + j * LANES:512 + (j + 1) * LANES], sc, su, sd, SWA_HEAD_DIM // 2) for j in range(8)]
        ks = [_rope(t[:, 1536 + j * LANES:1536 + (j + 1) * LANES], sc, su, sd, SWA_HEAD_DIM // 2) for j in range(2)]
        return [cq_n, ckv_n, kr, jnp.concatenate(qs, axis=1), jnp.concatenate(ks, axis=1)], []

    cq_n, ckv_n, kr, q_swa, k_swa = _rowwise(
        f_prep, [p1, *mla_tab["k"], *swa_tab["f"]], [w["g_q_lat"], w["g_kv_lat"]],
        [(256, BF16), (128, BF16), (LANES, F32), (1024, BF16), (256, BF16)], [], "lat_prep")

    q_lat = _matmul(cq_n, w["w_uq"], "nn", [F32], "mla_q_up")
    k_lat = _matmul(ckv_n, w["w_ukv_k"], "nn", [F32], "mla_k_up")
    def ones_lane(acc):
        lane = lax.broadcasted_iota(jnp.int32, acc.shape, 1) % LANES
        return (jnp.where(lane == ONES_LANE, 1.0, acc),)

    v_mla = _matmul(ckv_n, w["w_ukv_v"], "nn", [BF16], "mla_v_up", epilogue=ones_lane)
    mla_scale = (MLA_NOPE + MLA_ROPE) ** -0.5

    def f_mla_prep(rows, consts):
        ql, kl, krr, mc, mu, md = rows
        qs = [_rope(ql[:, j * LANES:(j + 1) * LANES], mc, mu, md, MLA_ROPE // 2) * mla_scale for j in range(8)]
        ks = [kl[:, j * LANES:(j + 1) * LANES] + krr for j in range(8)]
        return [jnp.concatenate(qs, axis=1), jnp.concatenate(ks, axis=1)], []

    q_mla, k_mla = _rowwise(f_mla_prep, [q_lat, k_lat, kr, *mla_tab["q"]], [], [(1024, BF16), (1024, BF16)], [], "mla_prep")

    o_mla, lse_mla = _softmax_attn_fwd(q_mla, k_mla, v_mla, MLA_HEADS, "mla_fwd")
    o_swa, lse_swa = _swa_fwd(q_swa, k_swa, p2, 0, w["sink_b"], "swa_fwd")
    o_sb = _sb_attn_fwd(p2, SB_HEADS, "sb_fwd", 2, 10, 18)

    oa = _matmul(o_mla, w["w_o_mla"], "nn", [BF16], "o_proj_mla")
    ob = _matmul(o_swa, w["w_o_swa"], "nn", [BF16], "o_proj_swa")
    oc = _matmul(o_sb, w["w_o_sb"], "nn", [BF16], "o_proj_sb")

    def f_mix(rows, consts):
        a, b, c, g = rows
        g = g.astype(F32)
        return [g[:, 0:1024] * a + g[:, 1024:2048] * b + g[:, 2048:3072] * c], []

    (mixed,) = _rowwise(f_mix, [oa, ob, oc, gates], [], [(D_MODEL, BF16)], [], "gate_mix")
    y = _matmul(mixed, w["w_out"], "nn", [F32], "out_proj")

    def f_res_norm(rows, consts):
        return [rows[0] + _rms(rows[1], consts[0])], []

    x1, h2 = _rowwise(f_res_norm_norm, [x, y], [w["g_mix_post"], w["g_mlp_pre"]],
                      [(D_MODEL, F32), (D_MODEL, BF16)], [], "res_norm_mix")

    def relu2(acc):
        r = jnp.maximum(acc, 0.0)
        return acc, r * r

    up, u = _matmul(h2, w["w_up"], "nn", [BF16, BF16], "mlp_up", epilogue=relu2)
    zd = _matmul(u, w["w_down"], "nn", [F32], "mlp_down")
    if g_next is None:
        (x2,) = _rowwise(f_res_norm, [x1, zd], [w["g_mlp_post"]], [(D_MODEL, F32)], [], "res_norm_mlp")
        h_next = None
    else:
        x2, h_next = _rowwise(f_res_norm_norm, [x1, zd], [w["g_mlp_post"], g_next],
                              [(D_MODEL, F32), (D_MODEL, BF16)], [], "res_norm_mlp_next")

    sv.update(h=h, p1=p1, p2=p2, gates=gates, cq_n=cq_n, ckv_n=ckv_n, q_swa=q_swa, k_swa=k_swa, q_mla=q_mla,
              k_mla=k_mla, v_mla=v_mla, o_mla=o_mla, lse_mla=lse_mla, o_swa=o_swa, lse_swa=lse_swa, o_sb=o_sb,
              oa=oa, ob=ob, oc=oc, mixed=mixed, y=y, x1=x1, h2=h2, up=up, u=u, zd=zd)
    return x2, h_next, sv


def _layer_bwd(dx2, w, sv, tabs):
    mla_tab, swa_tab = tabs
    gr = {}

    def f_norm_bwd(rows, consts):
        dx, dg = _rms_bwd(rows[0], consts[0], rows[1])
        return [dx], [dg]

    def f_norm_bwd_res(rows, consts):
        dx, dg = _rms_bwd(rows[0], consts[0], rows[1])
        return [rows[2] + dx], [dg]

    dzd, gr["g_mlp_post"] = _rowwise(f_norm_bwd, [sv["zd"], dx2], [w["g_mlp_post"]], [(D_MODEL, BF16)], [D_MODEL], "b_norm_mlp_post")
    gr["w_down"] = _matmul(sv["u"], dzd, "tn", [BF16], "b_w_down")
    dup = _matmul(dzd, w["w_down"], "nt", [BF16], "b_mlp_down",
                  epilogue=lambda acc, up: (acc * 2.0 * jnp.maximum(up.astype(F32), 0.0),), extras=[sv["up"]])
    gr["w_up"] = _matmul(sv["h2"], dup, "tn", [BF16], "b_w_up")
    dh2 = _matmul(dup, w["w_up"], "nt", [F32], "b_mlp_up")
    dx1, gr["g_mlp_pre"] = _rowwise(f_norm_bwd_res, [sv["x1"], dh2, dx2], [w["g_mlp_pre"]], [(D_MODEL, F32)], [D_MODEL], "b_norm_mlp_pre")

    dy, gr["g_mix_post"] = _rowwise(f_norm_bwd, [sv["y"], dx1], [w["g_mix_post"]], [(D_MODEL, BF16)], [D_MODEL], "b_norm_mix_post")
    gr["w_out"] = _matmul(sv["mixed"], dy, "tn", [BF16], "b_w_out")
    dmixed = _matmul(dy, w["w_out"], "nt", [F32], "b_out_proj")

    def f_mix_bwd(rows, consts):
        dm, a, b, c, g = rows
        g = g.astype(F32)
        outs, dls = [], []
        for j, o in enumerate((a, b, c)):
            gj = g[:, j * D_MODEL:(j + 1) * D_MODEL]
            outs.append(dm * gj)
            dls.append(dm * o * gj * (1.0 - gj))
        dl = jnp.concatenate(dls, axis=1)
        return outs + [dl], [dl]

    doa, dob, doc, dlogit, gr["b_gate"] = _rowwise(
        f_mix_bwd, [dmixed, sv["oa"], sv["ob"], sv["oc"], sv["gates"]], [],
        [(D_MODEL, BF16)] * 3 + [(P3_W, BF16)], [P3_W], "b_gate_mix")

    gr["w_o_mla"] = _matmul(sv["o_mla"], doa, "tn", [BF16], "b_w_o_mla")
    gr["w_o_swa"] = _matmul(sv["o_swa"], dob, "tn", [BF16], "b_w_o_swa")
    gr["w_o_sb"] = _matmul(sv["o_sb"], doc, "tn", [BF16], "b_w_o_sb")
    do_mla = _matmul(doa, w["w_o_mla"], "nt", [BF16], "b_o_proj_mla")
    do_swa = _matmul(dob, w["w_o_swa"], "nt", [BF16], "b_o_proj_swa")
    do_sb = _matmul(doc, w["w_o_sb"], "nt", [BF16], "b_o_proj_sb")

    dq_sb, dk_sb, dv_sb = _sb_attn_bwd(sv["p2"], sv["o_sb"], do_sb, SB_HEADS, "sb_bwd", 2, 10, 18)
    dq_swa, dk_swa, dv_swa, dsink = _swa_bwd(sv["q_swa"], sv["k_swa"], sv["p2"], 0, w["sink_b"], sv["o_swa"],
                                             sv["lse_swa"], do_swa, "swa_bwd")
    gr["swa_sinks"] = dsink.reshape(SWA_HEADS, LANES)[:, 0]
    dq_mla, dk_mla, dv_mla = _softmax_attn_bwd(sv["q_mla"], sv["k_mla"], sv["v_mla"], sv["o_mla"], sv["lse_mla"], do_mla,
                                               MLA_HEADS, (MLA_NOPE + MLA_ROPE) ** -0.5, "mla_bwd")

    def f_mla_post(rows, consts):
        dq, dk, qc, qu, qd, kc, ku, kd = rows
        dqs = [_rope(dq[:, j * LANES:(j + 1) * LANES], qc, qu, qd, MLA_ROPE // 2) for j in range(8)]
        dkr = dk[:, 0:LANES]
        for j in range(1, 8):
            dkr = dkr + dk[:, j * LANES:(j + 1) * LANES]
        return [jnp.concatenate(dqs, axis=1), _rope(dkr, kc, ku, kd, MLA_ROPE // 2)], []

    dq_lat, dkr = _rowwise(f_mla_post, [dq_mla, dk_mla, *mla_tab["q_inv"], *mla_tab["k_inv"]], [],
                           [(1024, BF16), (LANES, F32)], [], "b_mla_post")
    gr["w_uq"] = _matmul(sv["cq_n"], dq_lat, "tn", [BF16], "b_w_uq")
    gr["w_ukv_k"] = _matmul(sv["ckv_n"], dk_mla, "tn", [BF16], "b_w_ukv_k")
    gr["w_ukv_v"] = _matmul(sv["ckv_n"], dv_mla, "tn", [BF16], "b_w_ukv_v")
    dcq_n = _matmul(dq_lat, w["w_uq"], "nt", [F32], "b_mla_q_up")
    dckv_a = _matmul(dk_mla, w["w_ukv_k"], "nt", [F32], "b_mla_k_up")
    dckv_b = _matmul(dv_mla, w["w_ukv_v"], "nt", [F32], "b_mla_v_up")

    def f_prep_bwd(rows, consts):
        t, dcq, dca, dcb, dkr_, dqs, dks, sc, su, sd = rows
        gq, gkv = consts
        dc_q, dgq = _rms_bwd(t[:, 0:256], gq, dcq)
        dc_kv, dgkv = _rms_bwd(t[:, 256:384], gkv, dca + dcb)
        q_parts = [_rope(dqs[:, j * LANES:(j + 1) * LANES], sc, su, sd, SWA_HEAD_DIM // 2) for j in range(8)]
        k_parts = [_rope(dks[:, j * LANES:(j + 1) * LANES], sc, su, sd, SWA_HEAD_DIM // 2) for j in range(2)]
        return [jnp.concatenate([dc_q, dc_kv, dkr_] + q_parts + k_parts, axis=1)], [dgq, dgkv]

    dp1, gr["g_q_lat"], gr["g_kv_lat"] = _rowwise(
        f_prep_bwd, [sv["p1"], dcq_n, dckv_a, dckv_b, dkr, dq_swa, dk_swa, *swa_tab["inv"]], [w["g_q_lat"], w["g_kv_lat"]],
        [(P1_W, BF16)], [256, 128], "b_lat_prep")

    gr["w_in1"] = _matmul(sv["h"], dp1, "tn", [BF16], "b_w_in_lat")
    dh = _matmul(dp1, w["w_in1"], "nt", [F32], "b_proj_lat")
    gr["w_in2"] = []
    add_prev = lambda acc, prev: (acc + prev,)
    for piece, wp, tag in zip((dv_swa, dq_sb, dk_sb, dv_sb), w["w_in2_parts"], ("vswa", "qsb", "ksb", "vsb")):
        gr["w_in2"].append(_matmul(sv["h"], piece, "tn", [BF16], "b_w_in_" + tag))
        dh = _matmul(piece, wp, "nt", [F32], "b_proj_" + tag, epilogue=add_prev, extras=[dh])
    gr["w_in3"] = _matmul(sv["h"], dlogit, "tn", [BF16], "b_w_in_gate")
    dh = _matmul(dlogit, w["w_in3"], "nt", [F32], "b_proj_gate", epilogue=add_prev, extras=[dh])
    dx, gr["g_mix_pre"] = _rowwise(f_norm_bwd_res, [sv["x"], dh, dx1], [w["g_mix_pre"]], [(D_MODEL, F32)], [D_MODEL], "b_norm_mix_pre")
    return dx, gr


def _local_step(x, positions, loss_target, full):
    mc, mu, md = _rope_tables(positions, MLA_NOPE, MLA_ROPE, True)
    kc, ku, kd = _rope_tables(positions, MLA_NOPE, MLA_ROPE, False)
    sc, su, sd = _rope_tables(positions, 0, SWA_HEAD_DIM, False)
    mla_tab = {"q": (mc, mu, md), "k": (kc, ku, kd), "q_inv": (mc, -mu, -md), "k_inv": (kc, -ku, -kd)}
    swa_tab = {"f": (sc, su, sd), "inv": (sc, -su, -sd)}
    tabs = (mla_tab, swa_tab)

    big = {n: full[n].astype(BF16) for n in SHARDED}
    w1, w2, w3 = _w_in_internal(big["w_in"])
    uk, uv = _w_ukv_internal(big["w_ukv"])
    stacks = {
        "w_in1": w1, "w_in2": jnp.concatenate(w2, axis=-1), "w_in3": w3,
        "w_uq": _pad_cols(big["w_uq"], MLA_HEADS, MLA_NOPE + MLA_ROPE), "w_ukv_k": uk, "w_ukv_v": uv,
        "w_o_mla": _pad_rows(big["w_o_mla"], 8, 64), "w_o_swa": _pad_rows(big["w_o_swa"], 8, 64),
        "w_o_sb": _pad_rows(big["w_o_sb"], 8, 64), "w_out": big["w_out"], "w_up": big["w_up"], "w_down": big["w_down"],
    }
    layers = []
    for l in range(DEPTH):
        layers.append({
            **{n: (t, l) for n, t in stacks.items()}, "w_in2_parts": [(t, l) for t in w2],
            "g_mix_pre": full["g_mix_pre"][l][None], "b_gate": full["b_gate"][l][None],
            "g_q_lat": full["g_q_lat"][l][None], "g_kv_lat": full["g_kv_lat"][l][None],
            "g_mix_post": full["g_mix_post"][l][None], "g_mlp_pre": full["g_mlp_pre"][l][None],
            "g_mlp_post": full["g_mlp_post"][l][None],
            "sink_b": jnp.repeat(full["swa_sinks"][l], LANES)[None],
        })

    saved = []
    h, pre = x, None
    for l in range(DEPTH):
        g_next = layers[l + 1]["g_mix_pre"] if l + 1 < DEPTH else None
        h, pre, sv = _layer_fwd(h, pre, layers[l], tabs, g_next)
        saved.append(sv)

    def f_loss(rows, consts):
        err = rows[0] - rows[1]
        return [err * (1.0 / D_MODEL)], [jnp.sum(err * err, axis=1, keepdims=True)]

    dy, sq = _rowwise(f_loss, [h, loss_target], [], [(D_MODEL, F32)], [1], "loss_head")
    loss_part = sq * (0.5 / D_MODEL)

    grs = [None] * DEPTH
    d = dy
    for l in reversed(range(DEPTH)):
        d, grs[l] = _layer_bwd(d, layers[l], saved[l], tabs)
    st = lambda pick: jnp.stack([pick(grs[l]) for l in range(DEPTH)])
    vec = lambda n: st(lambda gr: gr[n][0] if gr[n].ndim == 2 else gr[n])
    stacked = {n: vec(n) for n in SMALL}
    stacked.update({n: st(lambda gr: gr[n]) for n in ("w_out", "w_up", "w_down")})
    stacked["w_in"] = _w_in_reference(st(lambda gr: gr["w_in1"]), [st(lambda gr: gr["w_in2"][p]) for p in range(4)],
                                      st(lambda gr: gr["w_in3"]))
    stacked["w_uq"] = _unpad_cols(st(lambda gr: gr["w_uq"]), MLA_HEADS, MLA_NOPE + MLA_ROPE)
    stacked["w_ukv"] = _w_ukv_reference(st(lambda gr: gr["w_ukv_k"]), st(lambda gr: gr["w_ukv_v"]))
    for n in ("w_o_mla", "w_o_swa", "w_o_sb"):
        stacked[n] = _unpad_rows(st(lambda gr: gr[n]), 8, 64)
    return loss_part, d, stacked


def _lane_padded(width):
    return max(width, LANES)


def _rows_of(a, dtype):
    extra = _lane_padded(a.shape[-1]) - a.shape[-1]
    if extra:
        a = jnp.pad(a, [(0, 0)] * (a.ndim - 1) + [(0, extra)])
    return a.astype(dtype).reshape(-1, LANES)


def _pack(shards, small, dtype):
    parts = [_rows_of(shards[n], dtype) for n in SHARDED]
    if small is not None:
        parts += [_rows_of(small[n], dtype) for n in SMALL]
    slab = jnp.concatenate(parts, axis=0)
    pad = (-slab.shape[0]) % SLAB_ROW_ALIGN
    return jnp.pad(slab, ((0, pad), (0, 0)))


def _unpack(slab, shard_shapes, small_shapes):
    out, r = {}, 0
    shapes = [(n, shard_shapes[n]) for n in SHARDED]
    if small_shapes is not None:
        shapes += [(n, small_shapes[n]) for n in SMALL]
    for n, shape in shapes:
        wide = shape[:-1] + (_lane_padded(shape[-1]),)
        rows = int(np.prod(wide)) // LANES
        out[n] = slab[r:r + rows].reshape(wide)[..., :shape[-1]]
        r += rows
    return out


def _chip_exchange(src, name):
    rows = src.shape[-2]

    def body(src_ref, out_ref, send_sems, recv_sems):
        x, y, c = lax.axis_index("x"), lax.axis_index("y"), lax.axis_index("c")
        me = 2 * x + y
        chips = [(1 - x, y), (x, 1 - y), (1 - x, 1 - y)]
        sends = []
        for k, (cx, cy) in enumerate(chips):
            cp = pltpu.make_async_remote_copy(
                src_ref=src_ref.at[2 * cx + cy], dst_ref=out_ref.at[me], send_sem=send_sems.at[k],
                recv_sem=recv_sems.at[k], device_id=(cx, cy, c), device_id_type=pl.DeviceIdType.MESH)
            cp.start()
            sends.append(cp)
        for k, (cx, cy) in enumerate(chips):
            pltpu.make_async_remote_copy(
                src_ref=src_ref.at[me], dst_ref=out_ref.at[2 * cx + cy], send_sem=send_sems.at[k],
                recv_sem=recv_sems.at[k], device_id=(cx, cy, c), device_id_type=pl.DeviceIdType.MESH).wait_recv()
        for cp in sends:
            cp.wait_send()

    return pl.pallas_call(
        body,
        name=name,
        in_specs=[pl.BlockSpec(memory_space=pl.ANY)],
        out_specs=pl.BlockSpec(memory_space=pl.ANY),
        out_shape=jax.ShapeDtypeStruct((N_CHIPS, rows, LANES), src.dtype),
        scratch_shapes=[pltpu.SemaphoreType.DMA((3,)), pltpu.SemaphoreType.DMA((3,))],
    )(src)


def _half_rows(c, half):
    return pl.ds(pl.multiple_of(c * half, SLAB_ROW_ALIGN // 2), half)


def _gather_weights(src, name):
    rows = src.shape[0]
    half = rows // 2

    def body(src_ref, out_ref, send_sems, recv_sems):
        x, y, c = lax.axis_index("x"), lax.axis_index("y"), lax.axis_index("c")
        me = 2 * x + y
        chips = [(1 - x, y), (x, 1 - y), (1 - x, 1 - y)]

        def copy(k, src_view, slab, part, to):
            return pltpu.make_async_remote_copy(
                src_ref=src_view, dst_ref=out_ref.at[slab, _half_rows(part, half), :], send_sem=send_sems.at[k],
                recv_sem=recv_sems.at[k], device_id=to, device_id_type=pl.DeviceIdType.MESH)

        sends = [copy(k, src_ref.at[_half_rows(c, half), :], me, c, (cx, cy, c)) for k, (cx, cy) in enumerate(chips)]
        for cp in sends:
            cp.start()
        for k, (cx, cy) in enumerate(chips):
            j = 2 * cx + cy
            landed = out_ref.at[j, _half_rows(c, half), :]
            copy(k, landed, j, c, (cx, cy, c)).wait_recv()
            fwd = copy(3 + k, landed, j, c, (x, y, 1 - c))
            fwd.start()
            sends.append(fwd)
        for k, (cx, cy) in enumerate(chips):
            j = 2 * cx + cy
            copy(3 + k, out_ref.at[j, _half_rows(1 - c, half), :], j, 1 - c, (x, y, 1 - c)).wait_recv()
        for cp in sends:
            cp.wait_send()

    return pl.pallas_call(
        body,
        name=name,
        in_specs=[pl.BlockSpec(memory_space=pl.ANY)],
        out_specs=pl.BlockSpec(memory_space=pl.ANY),
        out_shape=jax.ShapeDtypeStruct((N_CHIPS, rows, LANES), src.dtype),
        scratch_shapes=[pltpu.SemaphoreType.DMA((6,)), pltpu.SemaphoreType.DMA((6,))],
    )(src)


def _sibling_halves(src, name):
    n, rows, _ = src.shape
    half = rows // 2

    def body(src_ref, out_ref, send_sem, recv_sem):
        x, y, c = lax.axis_index("x"), lax.axis_index("y"), lax.axis_index("c")
        cp = pltpu.make_async_remote_copy(
            src_ref=src_ref.at[:, _half_rows(1 - c, half), :], dst_ref=out_ref, send_sem=send_sem, recv_sem=recv_sem,
            device_id=(x, y, 1 - c), device_id_type=pl.DeviceIdType.MESH)
        cp.start()
        cp.wait()

    return pl.pallas_call(
        body,
        name=name,
        in_specs=[pl.BlockSpec(memory_space=pl.ANY)],
        out_specs=pl.BlockSpec(memory_space=pl.ANY),
        out_shape=jax.ShapeDtypeStruct((n, half, LANES), src.dtype),
        scratch_shapes=[pltpu.SemaphoreType.DMA, pltpu.SemaphoreType.DMA],
    )(src)


def _sibling_join(src, name):
    half = src.shape[0]

    def body(src_ref, out_ref, send_sem, recv_sem):
        x, y, c = lax.axis_index("x"), lax.axis_index("y"), lax.axis_index("c")
        cp = pltpu.make_async_remote_copy(
            src_ref=src_ref, dst_ref=out_ref.at[_half_rows(c, half), :], send_sem=send_sem, recv_sem=recv_sem,
            device_id=(x, y, 1 - c), device_id_type=pl.DeviceIdType.MESH)
        cp.start()
        pltpu.make_async_remote_copy(
            src_ref=src_ref, dst_ref=out_ref.at[_half_rows(1 - c, half), :], send_sem=send_sem, recv_sem=recv_sem,
            device_id=(x, y, 1 - c), device_id_type=pl.DeviceIdType.MESH).wait_recv()
        cp.wait_send()

    return pl.pallas_call(
        body,
        name=name,
        in_specs=[pl.BlockSpec(memory_space=pl.ANY)],
        out_specs=pl.BlockSpec(memory_space=pl.ANY),
        out_shape=jax.ShapeDtypeStruct((2 * half, LANES), src.dtype),
        scratch_shapes=[pltpu.SemaphoreType.DMA, pltpu.SemaphoreType.DMA],
    )(src)


SUM_ROWS = 1024


def _pair_sum(mine, theirs, c, name):
    n, half, _ = theirs.shape
    blocks = half // SUM_ROWS

    def body(c_ref, a_ref, b_ref, o_ref):
        o_ref[...] = (a_ref[...].astype(F32) + b_ref[...].astype(F32)).astype(o_ref.dtype)

    return pl.pallas_call(
        body,
        name=name,
        grid_spec=pltpu.PrefetchScalarGridSpec(
            num_scalar_prefetch=1,
            grid=(blocks,),
            in_specs=[pl.BlockSpec((n, SUM_ROWS, LANES), lambda i, c_ref: (0, c_ref[0] * blocks + i, 0)),
                      pl.BlockSpec((n, SUM_ROWS, LANES), lambda i, c_ref: (0, i, 0))],
            out_specs=pl.BlockSpec((n, SUM_ROWS, LANES), lambda i, c_ref: (0, i, 0)),
        ),
        out_shape=jax.ShapeDtypeStruct((n, half, LANES), BF16),
        compiler_params=_params(("arbitrary",)),
    )(jnp.reshape(c, (1,)).astype(jnp.int32), mine, theirs)


def _sum_chips(own, landed, me, name):
    rows = landed.shape[1]

    def body(me_ref, a_ref, b_ref, o_ref):
        t = [jnp.where(me_ref[0] == j, a_ref[j], b_ref[j]).astype(F32) for j in range(N_CHIPS)]
        o_ref[...] = ((t[0] + t[1]) + t[2]) + t[3]

    slabs = pl.BlockSpec((N_CHIPS, SUM_ROWS, LANES), lambda i, me_ref: (0, i, 0))
    return pl.pallas_call(
        body,
        name=name,
        grid_spec=pltpu.PrefetchScalarGridSpec(
            num_scalar_prefetch=1,
            grid=(rows // SUM_ROWS,),
            in_specs=[slabs, slabs],
            out_specs=pl.BlockSpec((SUM_ROWS, LANES), lambda i, me_ref: (i, 0)),
        ),
        out_shape=jax.ShapeDtypeStruct((rows, LANES), F32),
        compiler_params=_params(("arbitrary",)),
    )(jnp.reshape(me, (1,)).astype(jnp.int32), own, landed)


def _adamw(w, m, v, g, name):
    shape = w.shape
    flat = lambda a: a.reshape(-1, shape[-1])

    def fn(rows, consts):
        w_, m_, v_, g_ = rows
        m_new = ADAM_B1 * m_ + (1.0 - ADAM_B1) * g_
        v_new = ADAM_B2 * v_ + (1.0 - ADAM_B2) * (g_ * g_)
        m_hat = m_new / (1.0 - ADAM_B1 ** ADAM_STEP)
        v_hat = v_new / (1.0 - ADAM_B2 ** ADAM_STEP)
        delta = -ADAM_LR * (m_hat / (jnp.sqrt(v_hat) + ADAM_EPS) + ADAM_WD * w_)
        return [delta, m_new, v_new], []

    outs = _rowwise(fn, [flat(w), flat(m), flat(v), flat(g)], [], [(shape[-1], F32)] * 3, [], name)
    return [o.reshape(shape) for o in outs]


def kernel(x, positions, g_mix_pre, w_in, b_gate, g_q_lat, g_kv_lat, w_uq, w_ukv, swa_sinks, w_o_mla, w_o_swa, w_o_sb, w_out, g_mix_post, g_mlp_pre, w_up, w_down, g_mlp_post, loss_target, m_g_mix_pre, m_w_in, m_b_gate, m_g_q_lat, m_g_kv_lat, m_w_uq, m_w_ukv, m_swa_sinks, m_w_o_mla, m_w_o_swa, m_w_o_sb, m_w_out, m_g_mix_post, m_g_mlp_pre, m_w_up, m_w_down, m_g_mlp_post, v_g_mix_pre, v_w_in, v_b_gate, v_g_q_lat, v_g_kv_lat, v_w_uq, v_w_ukv, v_swa_sinks, v_w_o_mla, v_w_o_swa, v_w_o_sb, v_w_out, v_g_mix_post, v_g_mlp_pre, v_w_up, v_w_down, v_g_mlp_post):
    given = dict(locals())
    wts = {n: given[n] for n in WEIGHTS}
    mom_m = {n: given["m_" + n] for n in WEIGHTS}
    mom_v = {n: given["v_" + n] for n in WEIGHTS}
    shard_shapes = {n: wts[n].shape for n in SHARDED}
    small_shapes = {n: wts[n].shape for n in SMALL}

    me = 2 * lax.axis_index("x") + lax.axis_index("y")
    core = lax.axis_index("c")
    gathered = _gather_weights(_pack(wts, None, BF16), "gather_weights")
    full = {n: wts[n] for n in SMALL}
    per_chip = [_unpack(gathered[j], shard_shapes, None) for j in range(N_CHIPS)]
    for n in SHARDED:
        own = wts[n].astype(BF16)
        full[n] = jnp.concatenate([jnp.where(me == j, own, per_chip[j][n]) for j in range(N_CHIPS)], axis=SHARD_AXIS[n])

    loss_part, grad_x, grads = _local_step(x[0], positions[0], loss_target[0], full)
    loss = lax.psum(loss_part[0, 0], ("x", "y", "c"))

    small_g = {n: grads[n] for n in SMALL}
    slabs = []
    for j in range(N_CHIPS):
        shard = {n: jnp.split(grads[n], N_CHIPS, axis=SHARD_AXIS[n])[j] for n in SHARDED}
        slabs.append(_pack(shard, small_g, BF16))
    per_chip_g = jnp.stack(slabs)
    theirs = _sibling_halves(per_chip_g, "pair_grads")
    pair = _pair_sum(per_chip_g, theirs, core, "sum_pair")
    landed = _chip_exchange(pair, "scatter_grads")
    my_half = _sum_chips(pair, landed, me, "sum_chips")
    g_slab = lax.dynamic_update_slice(_sibling_join(my_half, "join_grads"), my_half, (core * my_half.shape[0], 0))

    g = _unpack(g_slab, shard_shapes, small_shapes)
    stepped = {n: _adamw(wts[n], mom_m[n], mom_v[n], g[n], "adamw_" + n) for n in WEIGHTS}
    outs = [loss, grad_x[None]] + [g[n] for n in WEIGHTS]
    for part in range(3):
        outs += [stepped[n][part] for n in WEIGHTS]
    return tuple(outs)
```

```python
import numpy as np
import jax
import jax.numpy as jnp
from jax import lax
from jax.experimental import pallas as pl
from jax.experimental.pallas import tpu as pltpu

F32 = jnp.float32
BF16 = jnp.bfloat16

D_MODEL = 1024
DEPTH = 4
MLA_HEADS, MLA_Q_LORA, MLA_KV_LORA, MLA_NOPE, MLA_ROPE, MLA_V = 8, 256, 128, 64, 32, 64
SWA_HEADS, SWA_KV_HEADS, SWA_HEAD_DIM, SWA_WINDOW = 8, 2, 64, 128
SB_HEADS, SB_HEAD_DIM = 8, 64
D_FF = 4 * D_MODEL
ROPE_THETA = 10000.0
EPS = 1e-6
SPLIT_SIZES = (256, 128, 32, 512, 128, 128, 512, 512, 512, 3 * D_MODEL)
SPLIT_POINTS = [int(v) for v in np.cumsum(SPLIT_SIZES)[:-1]]

ADAM_LR, ADAM_B1, ADAM_B2, ADAM_EPS, ADAM_WD, ADAM_STEP = 0.001, 0.9, 0.999, 1e-08, 0.01, 10

LANES = 128
V7X_VMEM_BYTES = 64 * 1024 * 1024
VMEM_LIMIT = V7X_VMEM_BYTES - 8 * 1024 * 1024
MATMUL_VMEM_BUDGET = 36 * 1024 * 1024
N_CHIPS = 4
SLAB_ROW_ALIGN = 2048

P1_W = 256 + 128 + 128 + 1024 + 256
P2_W = 256 + 1024 + 1024 + 1024
P3_W = 3 * D_MODEL

SHARDED = ("w_in", "w_uq", "w_ukv", "w_o_mla", "w_o_swa", "w_o_sb", "w_out", "w_up", "w_down")
SHARD_AXIS = {"w_in": 2, "w_uq": 2, "w_ukv": 2, "w_o_mla": 2, "w_o_swa": 2, "w_o_sb": 2, "w_out": 1, "w_up": 2, "w_down": 1}
SMALL = ("g_mix_pre", "b_gate", "g_q_lat", "g_kv_lat", "swa_sinks", "g_mix_post", "g_mlp_pre", "g_mlp_post")
WEIGHTS = ("g_mix_pre", "w_in", "b_gate", "g_q_lat", "g_kv_lat", "w_uq", "w_ukv", "swa_sinks", "w_o_mla", "w_o_swa",
           "w_o_sb", "w_out", "g_mix_post", "g_mlp_pre", "w_up", "w_down", "g_mlp_post")

NN = (((1,), (0,)), ((), ()))
NT = (((1,), (1,)), ((), ()))
TN = (((0,), (0,)), ((), ()))


def _dot(a, b, dims):
    return lax.dot_general(a, b, dims, preferred_element_type=F32)


def _params(sem):
    return pltpu.CompilerParams(dimension_semantics=sem, vmem_limit_bytes=VMEM_LIMIT)


def _largest_tile(n, cap):
    if n <= cap:
        return n
    best = LANES
    for t in range(LANES, cap + 1, LANES):
        if n % t == 0:
            best = t
    return best


def _matmul_tiles(M, N, K, a_bytes, b_bytes, out_bytes, extra_bytes):
    tn = _largest_tile(N, 1792)
    tm = _largest_tile(M, 1024 if tn <= 1024 else 512)
    tk = _largest_tile(K, 2048)

    def need(tm_, tk_):
        acc = 4 * tm_ * tn if tk_ < K else 0
        return 2 * (tm_ * tk_ * a_bytes + tk_ * tn * b_bytes + tm_ * tn * (out_bytes + extra_bytes)) + acc

    while need(tm, tk) > MATMUL_VMEM_BUDGET:
        if tk >= tm and tk % 256 == 0:
            tk //= 2
        elif tm % 256 == 0:
            tm //= 2
        else:
            break
    return tm, tn, tk


def _matmul(a, b, mode, out_dtypes, name, epilogue=None, extras=(), row_extras=()):
    b_layer = None
    if isinstance(b, tuple):
        b, b_layer = b
    b_shape = b.shape[-2:]
    if mode == "nn":
        (M, K), (K2, N) = a.shape, b_shape
    elif mode == "nt":
        (M, K), (N, K2) = a.shape, b_shape
    else:
        (K, M), (K2, N) = a.shape, b_shape
    assert K == K2, (name, a.shape, b.shape)
    tm, tn, tk = _matmul_tiles(
        M, N, K, a.dtype.itemsize, b.dtype.itemsize, sum(jnp.dtype(d).itemsize for d in out_dtypes),
        sum(e.dtype.itemsize for e in extras))
    assert M % tm == 0 and N % tn == 0 and K % tk == 0, (name, M, N, K, tm, tn, tk)
    nk = K // tk
    if mode == "tn":
        a_spec = pl.BlockSpec((tk, tm), lambda i, j, k: (k, i))
    else:
        a_spec = pl.BlockSpec((tm, tk), lambda i, j, k: (i, k))
    b_block, b_index = ((tn, tk), lambda i, j, k: (j, k)) if mode == "nt" else ((tk, tn), lambda i, j, k: (k, j))
    if b_layer is None:
        b_spec = pl.BlockSpec(b_block, b_index)
    else:
        b_spec = pl.BlockSpec((None,) + b_block, lambda i, j, k: (b_layer,) + b_index(i, j, k))
    dims = {"nn": NN, "nt": NT, "tn": TN}[mode]
    n_ex, n_rex, n_out = len(extras), len(row_extras), len(out_dtypes)

    def body(*refs):
        a_ref, b_ref = refs[:2]
        ex = refs[2:2 + n_ex]
        rex = refs[2 + n_ex:2 + n_ex + n_rex]
        outs = refs[2 + n_ex + n_rex:2 + n_ex + n_rex + n_out]

        def finish(total):
            res = (total,) if epilogue is None else epilogue(total, *[e[...] for e in ex], *[e[...] for e in rex])
            for o, r in zip(outs, res):
                o[...] = r.astype(o.dtype)

        part = _dot(a_ref[...].astype(BF16), b_ref[...].astype(BF16), dims)
        if nk == 1:
            finish(part)
            return
        acc = refs[-1]
        k = pl.program_id(2)

        @pl.when(k == 0)
        def _():
            acc[...] = part

        @pl.when(k > 0)
        def _():
            acc[...] += part

        @pl.when(k == nk - 1)
        def _():
            finish(acc[...])

    in_specs = [a_spec, b_spec]
    in_specs += [pl.BlockSpec((tm, tn), lambda i, j, k: (i, j)) for _ in extras]
    in_specs += [pl.BlockSpec((1, tn), lambda i, j, k: (0, j)) for _ in row_extras]
    out = pl.pallas_call(
        body,
        name=name,
        grid=(M // tm, N // tn, nk),
        in_specs=in_specs,
        out_specs=[pl.BlockSpec((tm, tn), lambda i, j, k: (i, j)) for _ in out_dtypes],
        out_shape=[jax.ShapeDtypeStruct((M, N), dt) for dt in out_dtypes],
        scratch_shapes=[pltpu.VMEM((tm, tn), F32)] if nk > 1 else [],
        compiler_params=_params(("parallel", "parallel", "arbitrary")),
    )(a, b, *extras, *row_extras)
    return out[0] if n_out == 1 else out


ROWWISE_ROW_BYTES = 16 * 1024


def _rowwise(fn, rows, consts, out_defs, sum_widths, name):
    R = rows[0].shape[0]
    per_row = sum(r.shape[1] * r.dtype.itemsize for r in rows) + sum(w * jnp.dtype(dt).itemsize for w, dt in out_defs)
    bm = 512 if per_row <= ROWWISE_ROW_BYTES else 256
    while R % bm:
        bm //= 2
    bm = max(bm, 1)
    n_r, n_c, n_o = len(rows), len(consts), len(out_defs)
    n_s = len(sum_widths)

    def body(*refs):
        r_in = refs[:n_r]
        c_in = refs[n_r:n_r + n_c]
        o_refs = refs[n_r + n_c:n_r + n_c + n_o]
        s_refs = refs[n_r + n_c + n_o:]
        outs, sums = fn([r[...] for r in r_in], [c[...] for c in c_in])
        for o, val in zip(o_refs, outs):
            o[...] = val.astype(o.dtype)
        if n_s:
            @pl.when(pl.program_id(0) == 0)
            def _():
                for s in s_refs:
                    s[...] = jnp.zeros_like(s)

            for s, val in zip(s_refs, sums):
                s[...] += jnp.sum(val, axis=0, keepdims=True)

    in_specs = [pl.BlockSpec((bm, r.shape[1]), lambda i: (i, 0)) for r in rows]
    in_specs += [pl.BlockSpec(c.shape, lambda i: (0, 0)) for c in consts]
    out_specs = [pl.BlockSpec((bm, w), lambda i: (i, 0)) for w, _ in out_defs]
    out_specs += [pl.BlockSpec((1, w), lambda i: (0, 0)) for w in sum_widths]
    out_shape = [jax.ShapeDtypeStruct((R, w), dt) for w, dt in out_defs]
    out_shape += [jax.ShapeDtypeStruct((1, w), F32) for w in sum_widths]
    return pl.pallas_call(
        body,
        name=name,
        grid=(R // bm,),
        in_specs=in_specs,
        out_specs=out_specs,
        out_shape=out_shape,
        compiler_params=_params(("arbitrary",)),
    )(*rows, *consts)


def _rms(x, g):
    r = lax.rsqrt(jnp.mean(x * x, axis=-1, keepdims=True) + EPS)
    return x * r * g


def _rms_bwd(x, g, dy):
    r = lax.rsqrt(jnp.mean(x * x, axis=-1, keepdims=True) + EPS)
    n = x * r
    dn = dy * g
    dx = r * (dn - n * jnp.mean(dn * n, axis=-1, keepdims=True))
    return dx, dy * n


def _rope(x, c, s_up, s_dn, half):
    return x * c + pltpu.roll(x, half, 1) * s_up + pltpu.roll(x, LANES - half, 1) * s_dn


def _rope_tables(positions, lo, d, nope_pass):
    S = positions.shape[0]
    half = d // 2
    inv = 1.0 / (ROPE_THETA ** (jnp.arange(0, d, 2, dtype=F32) / d))
    ang = positions.astype(F32)[:, None] * inv
    cos, sin = jnp.cos(ang), jnp.sin(ang)
    z = lambda n: jnp.zeros((S, n), F32)
    head = jnp.ones((S, lo), F32) if nope_pass else z(lo)
    tail = LANES - lo - d
    c = jnp.concatenate([head, cos, cos, z(tail)], axis=1)
    s_up = jnp.concatenate([z(lo), z(half), sin, z(tail)], axis=1)
    s_dn = jnp.concatenate([z(lo), -sin, z(half), z(tail)], axis=1)
    return c, s_up, s_dn


MLA_FWD_CFG = (2, 1024)
MLA_BWD_CFG = (2, 512)
SB_FWD_CFG = (2, 256)
SB_BWD_CFG = (4, 256)


def _tile_mask(bk, strict):
    row = lax.broadcasted_iota(jnp.int32, (bk, bk), 0)
    col = lax.broadcasted_iota(jnp.int32, (bk, bk), 1)
    return (col < row) if strict else (col <= row)


def _att_layout(S, cfg):
    nch, bk = cfg
    bq = nch * bk
    assert S % bq == 0, (S, cfg)
    rows = [slice(r * bk, (r + 1) * bk) for r in range(nch)]
    q_spec = lambda off=0: pl.BlockSpec((bq, LANES), lambda h, i: (i, off + h))
    kv_spec = lambda off=0: pl.BlockSpec((S, LANES), lambda h, i: (0, off + h))
    return bq, rows, q_spec, kv_spec


def _total(terms):
    terms = list(terms)
    out = terms[0]
    for t in terms[1:]:
        out = out + t
    return out


def _walk(nch, i, step, carry, leftward, alive=None):
    everyone = range(nch)
    if leftward:
        for d in reversed(everyone):
            carry = step(nch * i + d, carry, range(d, nch), {d})
        if alive is None:
            return lax.fori_loop(0, nch * i, lambda t, c: step(nch * i - 1 - t, c, everyone, set()), carry)
        more = lambda tc: jnp.logical_and(tc[0] < nch * i, alive(tc[1]))
        left = lambda tc: (tc[0] + 1, step(nch * i - 1 - tc[0], tc[1], everyone, set()))
        return lax.while_loop(more, left, (jnp.int32(0), carry))[1]
    carry = lax.fori_loop(0, nch * i, lambda kb, c: step(kb, c, everyone, set()), carry)
    for d in everyone:
        carry = step(nch * i + d, carry, range(d, nch), {d})
    return carry


ONES_LANE = MLA_V


def _softmax_attn_fwd(q, k, v, heads, name, q_off=0, k_off=0, v_off=0):
    S = q.shape[0]
    nch, bk = MLA_FWD_CFG
    bq, rows, q_spec, kv_spec = _att_layout(S, MLA_FWD_CFG)

    def body(q_ref, k_ref, v_ref, o_ref, lse_ref):
        i = pl.program_id(1)
        qs = [q_ref[rw, :] for rw in rows]

        def step(kb, cs, active, masked):
            off = pl.multiple_of(kb * bk, bk)
            ks, vs = k_ref[pl.ds(off, bk), :], v_ref[pl.ds(off, bk), :]
            A = list(active)
            s = {r: _dot(qs[r], ks, NT) for r in A}
            s = {r: (jnp.where(_tile_mask(bk, False), s[r], -1e30) if r in masked else s[r]) for r in A}
            m_new = {r: jnp.maximum(cs[r][0], jnp.max(s[r], axis=1, keepdims=True)) for r in A}
            p = {r: jnp.exp(s[r] - m_new[r]) for r in A}
            alpha = {r: jnp.exp(cs[r][0] - m_new[r]) for r in A}
            new = list(cs)
            for r in A:
                new[r] = (m_new[r], alpha[r] * cs[r][1] + _dot(p[r].astype(BF16), vs, NN))
            return tuple(new)

        init = (jnp.full((bk, 1), -1e30, F32), jnp.zeros((bk, LANES), F32))
        cs = _walk(nch, i, step, tuple(init for _ in rows), False)
        for r, (m, acc) in enumerate(cs):
            l = acc[:, ONES_LANE:ONES_LANE + 1]
            o_ref[rows[r], :] = (acc / l).astype(o_ref.dtype)
            lse_ref[rows[r], :] = m + jnp.log(l)

    return pl.pallas_call(
        body,
        name=name,
        grid=(heads, S // bq),
        in_specs=[q_spec(q_off), kv_spec(k_off), kv_spec(v_off)],
        out_specs=[q_spec(), pl.BlockSpec((None, bq, 1), lambda h, i: (h, i, 0))],
        out_shape=[jax.ShapeDtypeStruct((S, heads * LANES), BF16), jax.ShapeDtypeStruct((heads, S, 1), F32)],
        compiler_params=_params(("parallel", "arbitrary")),
    )(q, k, v)


def _softmax_attn_bwd(q, k, v, o, lse, do, heads, scale, name, q_off=0, k_off=0, v_off=0):
    S = q.shape[0]
    nch, bk = MLA_BWD_CFG
    bq, rows, q_spec, kv_spec = _att_layout(S, MLA_BWD_CFG)

    def body(q_ref, k_ref, v_ref, o_ref, lse_ref, do_ref, dq_ref, dk_ref, dv_ref):
        i = pl.program_id(1)

        @pl.when(i == 0)
        def _():
            dk_ref[...] = jnp.zeros_like(dk_ref)
            dv_ref[...] = jnp.zeros_like(dv_ref)

        qs = [q_ref[rw, :] for rw in rows]
        dos = [do_ref[rw, :] for rw in rows]
        lses = [lse_ref[rw, :] for rw in rows]
        deltas = [jnp.sum(dos[r].astype(F32) * o_ref[rows[r], :].astype(F32), axis=1, keepdims=True) for r in range(nch)]

        def step(kb, dqs, active, masked):
            off = pl.multiple_of(kb * bk, bk)
            ks, vs = k_ref[pl.ds(off, bk), :], v_ref[pl.ds(off, bk), :]
            A = list(active)
            s = {r: _dot(qs[r], ks, NT) for r in A}
            s = {r: (jnp.where(_tile_mask(bk, False), s[r], -1e30) if r in masked else s[r]) for r in A}
            p = {r: jnp.exp(s[r] - lses[r]) for r in A}
            dp = {r: _dot(dos[r], vs, NT) for r in A}
            ds = {r: (p[r] * (dp[r] - deltas[r])).astype(BF16) for r in A}
            dv_c = _total(_dot(p[r].astype(BF16), dos[r], TN) for r in A)
            dk_c = _total(_dot(ds[r], qs[r], TN) for r in A)
            dk_ref[pl.ds(off, bk), :] += dk_c
            dv_ref[pl.ds(off, bk), :] += dv_c
            new = list(dqs)
            for r in A:
                new[r] = dqs[r] + _dot(ds[r], ks, NN)
            return tuple(new)

        dqs = _walk(nch, i, step, tuple(jnp.zeros((bk, LANES), F32) for _ in rows), False)
        for r in range(nch):
            dq_ref[rows[r], :] = dqs[r] * scale

    return pl.pallas_call(
        body,
        name=name,
        grid=(heads, S // bq),
        in_specs=[q_spec(q_off), kv_spec(k_off), kv_spec(v_off), q_spec(),
                  pl.BlockSpec((None, bq, 1), lambda h, i: (h, i, 0)), q_spec()],
        out_specs=[q_spec(), kv_spec(), kv_spec()],
        out_shape=[jax.ShapeDtypeStruct((S, heads * LANES), F32)] * 3,
        compiler_params=_params(("parallel", "arbitrary")),
    )(q, k, v, o, lse, do)


def _tri(n, inclusive):
    r = lax.broadcasted_iota(jnp.int32, (n, n), 0)
    c = lax.broadcasted_iota(jnp.int32, (n, n), 1)
    return jnp.where((r >= c) if inclusive else (r > c), 1.0, 0.0).astype(BF16)


def _suffix_sum(x, tri, split=True):
    hi = x.astype(BF16)
    if not split:
        return _dot(hi, tri, NN)
    lo = (x - hi.astype(F32)).astype(BF16)
    return _dot(hi, tri, NN) + _dot(lo, tri, NN)


def _sb_logs(z):
    lg = jnp.log(1.0 + jnp.exp(-jnp.abs(z)))
    l1m = -(jnp.maximum(z, 0.0) + lg)
    return l1m, l1m + z


SB_SCALE = SB_HEAD_DIM ** -0.5
assert SB_SCALE == 0.125
SB_DEAD = -110.0


def _sb_alive(cs):
    top = cs[0][0]
    for c in cs[1:]:
        top = jnp.maximum(top, c[0])
    return jnp.max(top) > SB_DEAD


def _sb_attn_fwd(qkv, heads, name, q_off, k_off, v_off):
    S = qkv.shape[0]
    nch, bk = SB_FWD_CFG
    bq, rows, q_spec, kv_spec = _att_layout(S, SB_FWD_CFG)

    def body(q_ref, k_ref, v_ref, o_ref):
        i = pl.program_id(1)
        qs = [q_ref[rw, :] * SB_SCALE for rw in rows]
        tri = _tri(bk, False)

        def step(kb, cs, active, masked):
            off = pl.multiple_of(kb * bk, bk)
            ks, vs = k_ref[pl.ds(off, bk), :], v_ref[pl.ds(off, bk), :]
            A = list(active)
            lg = {r: _sb_logs(_dot(qs[r], ks, NT)) for r in A}
            l1m = {r: (jnp.where(_tile_mask(bk, True), lg[r][0], 0.0) if r in masked else lg[r][0]) for r in A}
            suf = {r: _suffix_sum(l1m[r], tri, split=False) for r in A}
            ex = {r: lg[r][1] + cs[r][0] + suf[r] for r in A}
            ex = {r: (jnp.where(_tile_mask(bk, True), ex[r], -1e30) if r in masked else ex[r]) for r in A}
            ab = {r: jnp.exp(ex[r]).astype(BF16) for r in A}
            new = list(cs)
            for r in A:
                new[r] = (cs[r][0] + jnp.sum(l1m[r], axis=1, keepdims=True), cs[r][1] + _dot(ab[r], vs, NN))
            return tuple(new)

        init = (jnp.zeros((bk, 1), F32), jnp.zeros((bk, LANES), F32))
        cs = _walk(nch, i, step, tuple(init for _ in rows), True, _sb_alive)
        for r in range(nch):
            o_ref[rows[r], :] = cs[r][1]

    return pl.pallas_call(
        body,
        name=name,
        grid=(heads, S // bq),
        in_specs=[q_spec(q_off), kv_spec(k_off), kv_spec(v_off)],
        out_specs=q_spec(),
        out_shape=jax.ShapeDtypeStruct((S, heads * LANES), F32),
        compiler_params=_params(("parallel", "arbitrary")),
    )(qkv, qkv, qkv)


def _sb_attn_bwd(qkv, o, do, heads, name, q_off, k_off, v_off):
    S = qkv.shape[0]
    nch, bk = SB_BWD_CFG
    bq, rows, q_spec, kv_spec = _att_layout(S, SB_BWD_CFG)

    def body(q_ref, k_ref, v_ref, o_ref, do_ref, dq_ref, dk_ref, dv_ref):
        i = pl.program_id(1)

        @pl.when(i == 0)
        def _():
            dk_ref[...] = jnp.zeros_like(dk_ref)
            dv_ref[...] = jnp.zeros_like(dv_ref)

        tri = _tri(bk, False)
        qs = [q_ref[rw, :] * SB_SCALE for rw in rows]
        dos = [do_ref[rw, :] for rw in rows]
        deltas = [jnp.sum(dos[r].astype(F32) * o_ref[rows[r], :], axis=1, keepdims=True) for r in range(nch)]

        def step(kb, cs, active, masked):
            off = pl.multiple_of(kb * bk, bk)
            ks, vs = k_ref[pl.ds(off, bk), :], v_ref[pl.ds(off, bk), :]
            A = list(active)
            lg = {r: _sb_logs(_dot(qs[r], ks, NT)) for r in A}
            l1m = {r: (jnp.where(_tile_mask(bk, True), lg[r][0], 0.0) if r in masked else lg[r][0]) for r in A}
            suf = {r: _suffix_sum(l1m[r], tri, split=False) for r in A}
            ex = {r: lg[r][1] + cs[r][0] + suf[r] for r in A}
            ex = {r: (jnp.where(_tile_mask(bk, True), ex[r], -1e30) if r in masked else ex[r]) for r in A}
            ab = {r: jnp.exp(ex[r]).astype(BF16) for r in A}
            da = {r: _dot(dos[r], vs, NT) for r in A}
            g = {r: ab[r].astype(F32) * da[r] for r in A}
            gs = {r: _suffix_sum(g[r], tri) for r in A}
            beta = {r: jnp.exp(lg[r][1]) for r in A}
            dz = {r: g[r] - beta[r] * (deltas[r] - cs[r][1] - gs[r]) for r in A}
            dz = {r: (jnp.where(_tile_mask(bk, True), dz[r], 0.0) if r in masked else dz[r]) for r in A}
            dzb = {r: dz[r].astype(BF16) for r in A}
            dv_c = _total(_dot(ab[r], dos[r], TN) for r in A)
            dk_c = _total(_dot(dzb[r], qs[r], TN) for r in A)
            dk_ref[pl.ds(off, bk), :] += dk_c
            dv_ref[pl.ds(off, bk), :] += dv_c
            new = list(cs)
            for r in A:
                new[r] = (cs[r][0] + jnp.sum(l1m[r], axis=1, keepdims=True),
                          cs[r][1] + jnp.sum(g[r], axis=1, keepdims=True), cs[r][2] + _dot(dzb[r], ks, NN))
            return tuple(new)

        zcol = jnp.zeros((bk, 1), F32)
        init = (zcol, zcol, jnp.zeros((bk, LANES), F32))
        cs = _walk(nch, i, step, tuple(init for _ in rows), True, _sb_alive)
        for r in range(nch):
            dq_ref[rows[r], :] = cs[r][2] * SB_SCALE

    return pl.pallas_call(
        body,
        name=name,
        grid=(heads, S // bq),
        in_specs=[q_spec(q_off), kv_spec(k_off), kv_spec(v_off), q_spec(), q_spec()],
        out_specs=[q_spec(), kv_spec(), kv_spec()],
        out_shape=[jax.ShapeDtypeStruct((S, heads * LANES), F32)] * 3,
        compiler_params=_params(("parallel", "arbitrary")),
    )(qkv, qkv, qkv, o, do)


SWA_BLK = 128
SWA_GROUP = SWA_HEADS // SWA_KV_HEADS


SWA_NB = 4
SWA_ROWS = SWA_NB * SWA_BLK


def _swa_band_mask(first):
    row = lax.broadcasted_iota(jnp.int32, (SWA_BLK, 2 * SWA_BLK), 0)
    col = lax.broadcasted_iota(jnp.int32, (SWA_BLK, 2 * SWA_BLK), 1)
    return (col > row) & (col <= row + SWA_WINDOW) & (jnp.logical_not(first) | (col >= SWA_BLK))


def _swa_in_specs(v_off):
    gw = SWA_GROUP * LANES
    before = lambda h, n: (jnp.maximum(SWA_NB * n - 1, 0), h)
    return [
        pl.BlockSpec((SWA_ROWS, gw), lambda h, n: (n, h)),
        pl.BlockSpec((SWA_BLK, LANES), before),
        pl.BlockSpec((SWA_ROWS, LANES), lambda h, n: (n, h)),
        pl.BlockSpec((SWA_BLK, LANES), lambda h, n: (jnp.maximum(SWA_NB * n - 1, 0), v_off + h)),
        pl.BlockSpec((SWA_ROWS, LANES), lambda h, n: (n, v_off + h)),
        pl.BlockSpec((1, gw), lambda h, n: (0, h)),
    ]


def _swa_bands(n, kp_ref, kc_ref, vp_ref, vc_ref):
    k_all = jnp.concatenate([kp_ref[...], kc_ref[...]], axis=0)
    v_all = jnp.concatenate([vp_ref[...], vc_ref[...]], axis=0)
    bands = []
    for j in range(SWA_NB):
        rows = slice(j * SWA_BLK, (j + 2) * SWA_BLK)
        bands.append((k_all[rows], v_all[rows], _swa_band_mask((n == 0) if j == 0 else False)))
    return bands


def _swa_fwd(q, k, v, v_off, sink_b, name):
    S = q.shape[0]
    assert S % SWA_ROWS == 0
    scale = SWA_HEAD_DIM ** -0.5
    gw = SWA_GROUP * LANES

    def body(q_ref, kp_ref, kc_ref, vp_ref, vc_ref, sink_ref, o_ref, lse_ref):
        n = pl.program_id(1)
        bands = _swa_bands(n, kp_ref, kc_ref, vp_ref, vc_ref)
        P = [(j, g) for j in range(SWA_NB) for g in range(SWA_GROUP)]
        rows = lambda j: slice(j * SWA_BLK, (j + 1) * SWA_BLK)
        lanes = lambda g: slice(g * LANES, (g + 1) * LANES)
        sk = {g: sink_ref[:, g * LANES:g * LANES + 1] for g in range(SWA_GROUP)}
        s = {(j, g): jnp.where(bands[j][2], _dot(q_ref[rows(j), lanes(g)], bands[j][0], NT) * scale, -1e30) for j, g in P}
        m = {(j, g): jnp.maximum(jnp.max(s[j, g], axis=1, keepdims=True), sk[g]) for j, g in P}
        p = {(j, g): jnp.exp(s[j, g] - m[j, g]) for j, g in P}
        den = {(j, g): jnp.sum(p[j, g], axis=1, keepdims=True) + jnp.exp(sk[g] - m[j, g]) for j, g in P}
        for j, g in P:
            o_ref[rows(j), lanes(g)] = _dot((p[j, g] / den[j, g]).astype(BF16), bands[j][1], NN).astype(o_ref.dtype)
            lse_ref[g, rows(j), :] = m[j, g] + jnp.log(den[j, g])

    return pl.pallas_call(
        body,
        name=name,
        grid=(SWA_KV_HEADS, S // SWA_ROWS),
        in_specs=_swa_in_specs(v_off),
        out_specs=[
            pl.BlockSpec((SWA_ROWS, gw), lambda h, n: (n, h)),
            pl.BlockSpec((SWA_GROUP, SWA_ROWS, 1), lambda h, n: (h, n, 0)),
        ],
        out_shape=[jax.ShapeDtypeStruct((S, SWA_HEADS * LANES), BF16), jax.ShapeDtypeStruct((SWA_HEADS, S, 1), F32)],
        compiler_params=_params(("parallel", "arbitrary")),
    )(q, k, k, v, v, sink_b)


def _swa_bwd(q, k, v, v_off, sink_b, o, lse, do, name):
    S = q.shape[0]
    assert S % SWA_ROWS == 0
    scale = SWA_HEAD_DIM ** -0.5
    gw = SWA_GROUP * LANES

    def body(q_ref, kp_ref, kc_ref, vp_ref, vc_ref, sink_ref, o_ref, lse_ref, do_ref, dq_ref, dk_ref, dv_ref, dsink_ref):
        n = pl.program_id(1)

        @pl.when(n == 0)
        def _():
            dk_ref[...] = jnp.zeros_like(dk_ref)
            dv_ref[...] = jnp.zeros_like(dv_ref)
            dsink_ref[...] = jnp.zeros_like(dsink_ref)

        bands = _swa_bands(n, kp_ref, kc_ref, vp_ref, vc_ref)
        P = [(j, g) for j in range(SWA_NB) for g in range(SWA_GROUP)]
        rows = lambda j: slice(j * SWA_BLK, (j + 1) * SWA_BLK)
        lanes = lambda g: slice(g * LANES, (g + 1) * LANES)
        qs = {(j, g): q_ref[rows(j), lanes(g)] for j, g in P}
        dos = {(j, g): do_ref[rows(j), lanes(g)] for j, g in P}
        lses = {(j, g): lse_ref[g, rows(j), :] for j, g in P}
        delta = {(j, g): jnp.sum(dos[j, g].astype(F32) * o_ref[rows(j), lanes(g)].astype(F32), axis=1, keepdims=True)
                 for j, g in P}
        s = {(j, g): jnp.where(bands[j][2], _dot(qs[j, g], bands[j][0], NT) * scale, -1e30) for j, g in P}
        p = {(j, g): jnp.exp(s[j, g] - lses[j, g]) for j, g in P}
        dp = {(j, g): _dot(dos[j, g], bands[j][1], NT) for j, g in P}
        ds = {(j, g): (p[j, g] * (dp[j, g] - delta[j, g]) * scale).astype(BF16) for j, g in P}
        for j, g in P:
            dq_ref[rows(j), lanes(g)] = _dot(ds[j, g], bands[j][0], NN)
        for g in range(SWA_GROUP):
            p_sink = [jnp.exp(sink_ref[:, g * LANES:g * LANES + 1] - lses[j, g]) * delta[j, g] for j in range(SWA_NB)]
            dsink_ref[:, lanes(g)] += jnp.zeros((1, LANES), F32) - jnp.sum(_total(p_sink), axis=0, keepdims=True)
        dkb = [_total(_dot(ds[j, g], qs[j, g], TN) for g in range(SWA_GROUP)) for j in range(SWA_NB)]
        dvb = [_total(_dot(p[j, g].astype(BF16), dos[j, g], TN) for g in range(SWA_GROUP)) for j in range(SWA_NB)]
        base = pl.multiple_of(n * SWA_ROWS, SWA_ROWS)
        for j in range(SWA_NB):
            own = pl.ds(base + j * SWA_BLK, SWA_BLK)
            after = j + 1 < SWA_NB
            dk_ref[own, :] += dkb[j][SWA_BLK:] + dkb[j + 1][:SWA_BLK] if after else dkb[j][SWA_BLK:]
            dv_ref[own, :] += dvb[j][SWA_BLK:] + dvb[j + 1][:SWA_BLK] if after else dvb[j][SWA_BLK:]

        @pl.when(n > 0)
        def _():
            before = pl.ds(pl.multiple_of(n * SWA_ROWS - SWA_BLK, SWA_BLK), SWA_BLK)
            dk_ref[before, :] += dkb[0][:SWA_BLK]
            dv_ref[before, :] += dvb[0][:SWA_BLK]

    return pl.pallas_call(
        body,
        name=name,
        grid=(SWA_KV_HEADS, S // SWA_ROWS),
        in_specs=_swa_in_specs(v_off) + [
            pl.BlockSpec((SWA_ROWS, gw), lambda h, n: (n, h)),
            pl.BlockSpec((SWA_GROUP, SWA_ROWS, 1), lambda h, n: (h, n, 0)),
            pl.BlockSpec((SWA_ROWS, gw), lambda h, n: (n, h)),
        ],
        out_specs=[
            pl.BlockSpec((SWA_ROWS, gw), lambda h, n: (n, h)),
            pl.BlockSpec((S, LANES), lambda h, n: (0, h)),
            pl.BlockSpec((S, LANES), lambda h, n: (0, h)),
            pl.BlockSpec((1, gw), lambda h, n: (0, h)),
        ],
        out_shape=[
            jax.ShapeDtypeStruct((S, SWA_HEADS * LANES), F32),
            jax.ShapeDtypeStruct((S, SWA_KV_HEADS * LANES), F32),
            jax.ShapeDtypeStruct((S, SWA_KV_HEADS * LANES), F32),
            jax.ShapeDtypeStruct((1, SWA_HEADS * LANES), F32),
        ],
        compiler_params=_params(("parallel", "arbitrary")),
    )(q, k, k, v, v, sink_b, o, lse, do)


def _pad_last(t, width):
    return jnp.pad(t, [(0, 0)] * (t.ndim - 1) + [(0, width - t.shape[-1])])


def _pad_cols(w, heads, real):
    lead = w.shape[:-1]
    return _pad_last(w.reshape(*lead, heads, real), LANES).reshape(*lead, heads * LANES)


def _unpad_cols(g, heads, real):
    lead = g.shape[:-1]
    return g.reshape(*lead, heads, LANES)[..., :real].reshape(*lead, heads * real)


def _pad_rows(w, heads, real):
    lead, n = w.shape[:-2], w.shape[-1]
    w = w.reshape(*lead, heads, real, n)
    return jnp.pad(w, [(0, 0)] * (w.ndim - 2) + [(0, LANES - real), (0, 0)]).reshape(*lead, heads * LANES, n)


def _unpad_rows(g, heads, real):
    lead, n = g.shape[:-2], g.shape[-1]
    return g.reshape(*lead, heads, LANES, n)[..., :real, :].reshape(*lead, heads * real, n)


def _w_in_internal(w_in):
    c_q, c_kv, k_r, q_swa, k_swa, v_swa, q_sb, k_sb, v_sb, gate = jnp.split(w_in, SPLIT_POINTS, axis=-1)
    k_r = jnp.pad(k_r, [(0, 0)] * (k_r.ndim - 1) + [(MLA_NOPE, LANES - MLA_NOPE - MLA_ROPE)])
    w1 = jnp.concatenate([c_q, c_kv, k_r, _pad_cols(q_swa, 8, 64), _pad_cols(k_swa, 2, 64)], axis=-1)
    w2 = [_pad_cols(v_swa, 2, 64), _pad_cols(q_sb, 8, 64), _pad_cols(k_sb, 8, 64), _pad_cols(v_sb, 8, 64)]
    return w1, w2, gate


def _w_in_reference(g1, g2, g3):
    c_q, c_kv, k_r, q_swa, k_swa = jnp.split(g1, [256, 384, 512, 1536], axis=-1)
    v_swa, q_sb, k_sb, v_sb = g2
    return jnp.concatenate([
        c_q, c_kv, k_r[..., MLA_NOPE:MLA_NOPE + MLA_ROPE], _unpad_cols(q_swa, 8, 64), _unpad_cols(k_swa, 2, 64),
        _unpad_cols(v_swa, 2, 64), _unpad_cols(q_sb, 8, 64), _unpad_cols(k_sb, 8, 64), _unpad_cols(v_sb, 8, 64),
        g3], axis=-1)


def _w_ukv_internal(w):
    lead = w.shape[:-1]
    w3 = w.reshape(*lead, MLA_HEADS, MLA_NOPE + MLA_V)
    pad = lambda t: _pad_last(t, LANES).reshape(*lead, MLA_HEADS * LANES)
    return pad(w3[..., :MLA_NOPE]), pad(w3[..., MLA_NOPE:])


def _w_ukv_reference(gk, gv):
    lead = gk.shape[:-1]
    gk = gk.reshape(*lead, MLA_HEADS, LANES)[..., :MLA_NOPE]
    gv = gv.reshape(*lead, MLA_HEADS, LANES)[..., :MLA_V]
    return jnp.concatenate([gk, gv], axis=-1).reshape(*lead, MLA_HEADS * (MLA_NOPE + MLA_V))


def _layer_fwd(x, h, w, tabs, g_next):
    mla_tab, swa_tab = tabs
    sv = {"x": x}

    def f_norm(rows, consts):
        return [_rms(rows[0], consts[0])], []

    def f_res_norm_norm(rows, consts):
        out = rows[0] + _rms(rows[1], consts[0])
        return [out, _rms(out, consts[1])], []

    if h is None:
        (h,) = _rowwise(f_norm, [x], [w["g_mix_pre"]], [(D_MODEL, BF16)], [], "norm_mix_pre")
    p1 = _matmul(h, w["w_in1"], "nn", [F32], "proj_lat")
    p2 = _matmul(h, w["w_in2"], "nn", [BF16], "proj_qkv")
    gates = _matmul(h, w["w_in3"], "nn", [BF16], "proj_gate",
                    epilogue=lambda acc, b: (1.0 / (1.0 + jnp.exp(-(acc + b))),), row_extras=[w["b_gate"]])

    def f_prep(rows, consts):
        t = rows[0]
        gq, gkv = consts[0], consts[1]
        mc, mu, md = rows[1], rows[2], rows[3]
        sc, su, sd = rows[4], rows[5], rows[6]
        cq_n = _rms(t[:, 0:256], gq)
        ckv_n = _rms(t[:, 256:384], gkv)
        kr = _rope(t[:, 384:512], mc, mu, md, MLA_ROPE // 2)
        qs = [_rope(t[:, 512 + j * LANES:512 + (j + 1) * LANES], sc, su, sd, SWA_HEAD_DIM // 2) for j in range(8)]
        ks = [_rope(t[:, 1536 + j * LANES:1536 + (j + 1) * LANES], sc, su, sd, SWA_HEAD_DIM // 2) for j in range(2)]
        return [cq_n, ckv_n, kr, jnp.concatenate(qs, axis=1), jnp.concatenate(ks, axis=1)], []

    cq_n, ckv_n, kr, q_swa, k_swa = _rowwise(
        f_prep, [p1, *mla_tab["k"], *swa_tab["f"]], [w["g_q_lat"], w["g_kv_lat"]],
        [(256, BF16), (128, BF16), (LANES, F32), (1024, BF16), (256, BF16)], [], "lat_prep")

    q_lat = _matmul(cq_n, w["w_uq"], "nn", [F32], "mla_q_up")
    k_lat = _matmul(ckv_n, w["w_ukv_k"], "nn", [F32], "mla_k_up")
    def ones_lane(acc):
        lane = lax.broadcasted_iota(jnp.int32, acc.shape, 1) % LANES
        return (jnp.where(lane == ONES_LANE, 1.0, acc),)

    v_mla = _matmul(ckv_n, w["w_ukv_v"], "nn", [BF16], "mla_v_up", epilogue=ones_lane)
    mla_scale = (MLA_NOPE + MLA_ROPE) ** -0.5

    def f_mla_prep(rows, consts):
        ql, kl, krr, mc, mu, md = rows
        qs = [_rope(ql[:, j * LANES:(j + 1) * LANES], mc, mu, md, MLA_ROPE // 2) * mla_scale for j in range(8)]
        ks = [kl[:, j * LANES:(j + 1) * LANES] + krr for j in range(8)]
        return [jnp.concatenate(qs, axis=1), jnp.concatenate(ks, axis=1)], []

    q_mla, k_mla = _rowwise(f_mla_prep, [q_lat, k_lat, kr, *mla_tab["q"]], [], [(1024, BF16), (1024, BF16)], [], "mla_prep")

    o_mla, lse_mla = _softmax_attn_fwd(q_mla, k_mla, v_mla, MLA_HEADS, "mla_fwd")
    o_swa, lse_swa = _swa_fwd(q_swa, k_swa, p2, 0, w["sink_b"], "swa_fwd")
    o_sb = _sb_attn_fwd(p2, SB_HEADS, "sb_fwd", 2, 10, 18)

    oa = _matmul(o_mla, w["w_o_mla"], "nn", [BF16], "o_proj_mla")
    ob = _matmul(o_swa, w["w_o_swa"], "nn", [BF16], "o_proj_swa")
    oc = _matmul(o_sb, w["w_o_sb"], "nn", [BF16], "o_proj_sb")

    def f_mix(rows, consts):
        a, b, c, g = rows
        g = g.astype(F32)
        return [g[:, 0:1024] * a + g[:, 1024:2048] * b + g[:, 2048:3072] * c], []

    (mixed,) = _rowwise(f_mix, [oa, ob, oc, gates], [], [(D_MODEL, BF16)], [], "gate_mix")
    y = _matmul(mixed, w["w_out"], "nn", [F32], "out_proj")

    def f_res_norm(rows, consts):
        return [rows[0] + _rms(rows[1], consts[0])], []

    x1, h2 = _rowwise(f_res_norm_norm, [x, y], [w["g_mix_post"], w["g_mlp_pre"]],
                      [(D_MODEL, F32), (D_MODEL, BF16)], [], "res_norm_mix")

    def relu2(acc):
        r = jnp.maximum(acc, 0.0)
        return acc, r * r

    up, u = _matmul(h2, w["w_up"], "nn", [BF16, BF16], "mlp_up", epilogue=relu2)
    zd = _matmul(u, w["w_down"], "nn", [F32], "mlp_down")
    if g_next is None:
        (x2,) = _rowwise(f_res_norm, [x1, zd], [w["g_mlp_post"]], [(D_MODEL, F32)], [], "res_norm_mlp")
        h_next = None
    else:
        x2, h_next = _rowwise(f_res_norm_norm, [x1, zd], [w["g_mlp_post"], g_next],
                              [(D_MODEL, F32), (D_MODEL, BF16)], [], "res_norm_mlp_next")

    sv.update(h=h, p1=p1, p2=p2, gates=gates, cq_n=cq_n, ckv_n=ckv_n, q_swa=q_swa, k_swa=k_swa, q_mla=q_mla,
              k_mla=k_mla, v_mla=v_mla, o_mla=o_mla, lse_mla=lse_mla, o_swa=o_swa, lse_swa=lse_swa, o_sb=o_sb,
              oa=oa, ob=ob, oc=oc, mixed=mixed, y=y, x1=x1, h2=h2, up=up, u=u, zd=zd)
    return x2, h_next, sv


def _layer_bwd(dx2, w, sv, tabs):
    mla_tab, swa_tab = tabs
    gr = {}

    def f_norm_bwd(rows, consts):
        dx, dg = _rms_bwd(rows[0], consts[0], rows[1])
        return [dx], [dg]

    def f_norm_bwd_res(rows, consts):
        dx, dg = _rms_bwd(rows[0], consts[0], rows[1])
        return [rows[2] + dx], [dg]

    dzd, gr["g_mlp_post"] = _rowwise(f_norm_bwd, [sv["zd"], dx2], [w["g_mlp_post"]], [(D_MODEL, BF16)], [D_MODEL], "b_norm_mlp_post")
    gr["w_down"] = _matmul(sv["u"], dzd, "tn", [BF16], "b_w_down")
    dup = _matmul(dzd, w["w_down"], "nt", [BF16], "b_mlp_down",
                  epilogue=lambda acc, up: (acc * 2.0 * jnp.maximum(up.astype(F32), 0.0),), extras=[sv["up"]])
    gr["w_up"] = _matmul(sv["h2"], dup, "tn", [BF16], "b_w_up")
    dh2 = _matmul(dup, w["w_up"], "nt", [F32], "b_mlp_up")
    def f_norm_bwd_pair(rows, consts):
        x1_, dh2_, dx2_, y_ = rows
        d1, dg1 = _rms_bwd(x1_, consts[0], dh2_)
        dx1_ = dx2_ + d1
        dy_, dg2 = _rms_bwd(y_, consts[1], dx1_)
        return [dx1_, dy_], [dg1, dg2]

    dx1, dy, gr["g_mlp_pre"], gr["g_mix_post"] = _rowwise(
        f_norm_bwd_pair, [sv["x1"], dh2, dx2, sv["y"]], [w["g_mlp_pre"], w["g_mix_post"]],
        [(D_MODEL, F32), (D_MODEL, BF16)], [D_MODEL, D_MODEL], "b_norm_mlp_pre_mix_post")
    gr["w_out"] = _matmul(sv["mixed"], dy, "tn", [BF16], "b_w_out")
    dmixed = _matmul(dy, w["w_out"], "nt", [F32], "b_out_proj")

    def f_mix_bwd(rows, consts):
        dm, a, b, c, g = rows
        g = g.astype(F32)
        outs, dls = [], []
        for j, o in enumerate((a, b, c)):
            gj = g[:, j * D_MODEL:(j + 1) * D_MODEL]
            outs.append(dm * gj)
            dls.append(dm * o * gj * (1.0 - gj))
        dl = jnp.concatenate(dls, axis=1)
        return outs + [dl], [dl]

    doa, dob, doc, dlogit, gr["b_gate"] = _rowwise(
        f_mix_bwd, [dmixed, sv["oa"], sv["ob"], sv["oc"], sv["gates"]], [],
        [(D_MODEL, BF16)] * 3 + [(P3_W, BF16)], [P3_W], "b_gate_mix")

    gr["w_o_mla"] = _matmul(sv["o_mla"], doa, "tn", [BF16], "b_w_o_mla")
    gr["w_o_swa"] = _matmul(sv["o_swa"], dob, "tn", [BF16], "b_w_o_swa")
    gr["w_o_sb"] = _matmul(sv["o_sb"], doc, "tn", [BF16], "b_w_o_sb")
    do_mla = _matmul(doa, w["w_o_mla"], "nt", [BF16], "b_o_proj_mla")
    do_swa = _matmul(dob, w["w_o_swa"], "nt", [BF16], "b_o_proj_swa")
    do_sb = _matmul(doc, w["w_o_sb"], "nt", [BF16], "b_o_proj_sb")

    dq_sb, dk_sb, dv_sb = _sb_attn_bwd(sv["p2"], sv["o_sb"], do_sb, SB_HEADS, "sb_bwd", 2, 10, 18)
    dq_swa, dk_swa, dv_swa, dsink = _swa_bwd(sv["q_swa"], sv["k_swa"], sv["p2"], 0, w["sink_b"], sv["o_swa"],
                                             sv["lse_swa"], do_swa, "swa_bwd")
    gr["swa_sinks"] = dsink.reshape(SWA_HEADS, LANES)[:, 0]
    dq_mla, dk_mla, dv_mla = _softmax_attn_bwd(sv["q_mla"], sv["k_mla"], sv["v_mla"], sv["o_mla"], sv["lse_mla"], do_mla,
                                               MLA_HEADS, (MLA_NOPE + MLA_ROPE) ** -0.5, "mla_bwd")

    def f_mla_post(rows, consts):
        dq, dk, qc, qu, qd, kc, ku, kd = rows
        dqs = [_rope(dq[:, j * LANES:(j + 1) * LANES], qc, qu, qd, MLA_ROPE // 2) for j in range(8)]
        dkr = dk[:, 0:LANES]
        for j in range(1, 8):
            dkr = dkr + dk[:, j * LANES:(j + 1) * LANES]
        return [jnp.concatenate(dqs, axis=1), _rope(dkr, kc, ku, kd, MLA_ROPE // 2)], []

    dq_lat, dkr = _rowwise(f_mla_post, [dq_mla, dk_mla, *mla_tab["q_inv"], *mla_tab["k_inv"]], [],
                           [(1024, BF16), (LANES, F32)], [], "b_mla_post")
    gr["w_uq"] = _matmul(sv["cq_n"], dq_lat, "tn", [BF16], "b_w_uq")
    gr["w_ukv_k"] = _matmul(sv["ckv_n"], dk_mla, "tn", [BF16], "b_w_ukv_k")
    gr["w_ukv_v"] = _matmul(sv["ckv_n"], dv_mla, "tn", [BF16], "b_w_ukv_v")
    dcq_n = _matmul(dq_lat, w["w_uq"], "nt", [F32], "b_mla_q_up")
    dckv_a = _matmul(dk_mla, w["w_ukv_k"], "nt", [F32], "b_mla_k_up")
    dckv_b = _matmul(dv_mla, w["w_ukv_v"], "nt", [F32], "b_mla_v_up")

    def f_prep_bwd(rows, consts):
        t, dcq, dca, dcb, dkr_, dqs, dks, sc, su, sd = rows
        gq, gkv = consts
        dc_q, dgq = _rms_bwd(t[:, 0:256], gq, dcq)
        dc_kv, dgkv = _rms_bwd(t[:, 256:384], gkv, dca + dcb)
        q_parts = [_rope(dqs[:, j * LANES:(j + 1) * LANES], sc, su, sd, SWA_HEAD_DIM // 2) for j in range(8)]
        k_parts = [_rope(dks[:, j * LANES:(j + 1) * LANES], sc, su, sd, SWA_HEAD_DIM // 2) for j in range(2)]
        return [jnp.concatenate([dc_q, dc_kv, dkr_] + q_parts + k_parts, axis=1)], [dgq, dgkv]

    dp1, gr["g_q_lat"], gr["g_kv_lat"] = _rowwise(
        f_prep_bwd, [sv["p1"], dcq_n, dckv_a, dckv_b, dkr, dq_swa, dk_swa, *swa_tab["inv"]], [w["g_q_lat"], w["g_kv_lat"]],
        [(P1_W, BF16)], [256, 128], "b_lat_prep")

    gr["w_in1"] = _matmul(sv["h"], dp1, "tn", [BF16], "b_w_in_lat")
    dh = _matmul(dp1, w["w_in1"], "nt", [F32], "b_proj_lat")
    gr["w_in2"] = []
    add_prev = lambda acc, prev: (acc + prev,)
    for piece, wp, tag in zip((dv_swa, dq_sb, dk_sb, dv_sb), w["w_in2_parts"], ("vswa", "qsb", "ksb", "vsb")):
        gr["w_in2"].append(_matmul(sv["h"], piece, "tn", [BF16], "b_w_in_" + tag))
        dh = _matmul(piece, wp, "nt", [F32], "b_proj_" + tag, epilogue=add_prev, extras=[dh])
    gr["w_in3"] = _matmul(sv["h"], dlogit, "tn", [BF16], "b_w_in_gate")
    dh = _matmul(dlogit, w["w_in3"], "nt", [F32], "b_proj_gate", epilogue=add_prev, extras=[dh])
    dx, gr["g_mix_pre"] = _rowwise(f_norm_bwd_res, [sv["x"], dh, dx1], [w["g_mix_pre"]], [(D_MODEL, F32)], [D_MODEL], "b_norm_mix_pre")
    return dx, gr


def _local_step(x, positions, loss_target, full):
    mc, mu, md = _rope_tables(positions, MLA_NOPE, MLA_ROPE, True)
    kc, ku, kd = _rope_tables(positions, MLA_NOPE, MLA_ROPE, False)
    sc, su, sd = _rope_tables(positions, 0, SWA_HEAD_DIM, False)
    mla_tab = {"q": (mc, mu, md), "k": (kc, ku, kd), "q_inv": (mc, -mu, -md), "k_inv": (kc, -ku, -kd)}
    swa_tab = {"f": (sc, su, sd), "inv": (sc, -su, -sd)}
    tabs = (mla_tab, swa_tab)

    big = {n: full[n].astype(BF16) for n in SHARDED}
    w1, w2, w3 = _w_in_internal(big["w_in"])
    uk, uv = _w_ukv_internal(big["w_ukv"])
    stacks = {
        "w_in1": w1, "w_in2": jnp.concatenate(w2, axis=-1), "w_in3": w3,
        "w_uq": _pad_cols(big["w_uq"], MLA_HEADS, MLA_NOPE + MLA_ROPE), "w_ukv_k": uk, "w_ukv_v": uv,
        "w_o_mla": _pad_rows(big["w_o_mla"], 8, 64), "w_o_swa": _pad_rows(big["w_o_swa"], 8, 64),
        "w_o_sb": _pad_rows(big["w_o_sb"], 8, 64), "w_out": big["w_out"], "w_up": big["w_up"], "w_down": big["w_down"],
    }
    layers = []
    for l in range(DEPTH):
        layers.append({
            **{n: (t, l) for n, t in stacks.items()}, "w_in2_parts": [(t, l) for t in w2],
            "g_mix_pre": full["g_mix_pre"][l][None], "b_gate": full["b_gate"][l][None],
            "g_q_lat": full["g_q_lat"][l][None], "g_kv_lat": full["g_kv_lat"][l][None],
            "g_mix_post": full["g_mix_post"][l][None], "g_mlp_pre": full["g_mlp_pre"][l][None],
            "g_mlp_post": full["g_mlp_post"][l][None],
            "sink_b": jnp.repeat(full["swa_sinks"][l], LANES)[None],
        })

    saved = []
    h, pre = x, None
    for l in range(DEPTH):
        g_next = layers[l + 1]["g_mix_pre"] if l + 1 < DEPTH else None
        h, pre, sv = _layer_fwd(h, pre, layers[l], tabs, g_next)
        saved.append(sv)

    def f_loss(rows, consts):
        err = rows[0] - rows[1]
        return [err * (1.0 / D_MODEL)], [jnp.sum(err * err, axis=1, keepdims=True)]

    dy, sq = _rowwise(f_loss, [h, loss_target], [], [(D_MODEL, F32)], [1], "loss_head")
    loss_part = sq * (0.5 / D_MODEL)

    grs = [None] * DEPTH
    d = dy
    for l in reversed(range(DEPTH)):
        d, grs[l] = _layer_bwd(d, layers[l], saved[l], tabs)
    st = lambda pick: jnp.stack([pick(grs[l]) for l in range(DEPTH)])
    vec = lambda n: st(lambda gr: gr[n][0] if gr[n].ndim == 2 else gr[n])
    stacked = {n: vec(n) for n in SMALL}
    stacked.update({n: st(lambda gr: gr[n]) for n in ("w_out", "w_up", "w_down")})
    stacked["w_in"] = _w_in_reference(st(lambda gr: gr["w_in1"]), [st(lambda gr: gr["w_in2"][p]) for p in range(4)],
                                      st(lambda gr: gr["w_in3"]))
    stacked["w_uq"] = _unpad_cols(st(lambda gr: gr["w_uq"]), MLA_HEADS, MLA_NOPE + MLA_ROPE)
    stacked["w_ukv"] = _w_ukv_reference(st(lambda gr: gr["w_ukv_k"]), st(lambda gr: gr["w_ukv_v"]))
    for n in ("w_o_mla", "w_o_swa", "w_o_sb"):
        stacked[n] = _unpad_rows(st(lambda gr: gr[n]), 8, 64)
    return loss_part, d, stacked


def _lane_padded(width):
    return max(width, LANES)


def _rows_of(a, dtype):
    extra = _lane_padded(a.shape[-1]) - a.shape[-1]
    if extra:
        a = jnp.pad(a, [(0, 0)] * (a.ndim - 1) + [(0, extra)])
    return a.astype(dtype).reshape(-1, LANES)


def _pack(shards, small, dtype):
    parts = [_rows_of(shards[n], dtype) for n in SHARDED]
    if small is not None:
        parts += [_rows_of(small[n], dtype) for n in SMALL]
    slab = jnp.concatenate(parts, axis=0)
    pad = (-slab.shape[0]) % SLAB_ROW_ALIGN
    return jnp.pad(slab, ((0, pad), (0, 0)))


def _unpack(slab, shard_shapes, small_shapes):
    out, r = {}, 0
    shapes = [(n, shard_shapes[n]) for n in SHARDED]
    if small_shapes is not None:
        shapes += [(n, small_shapes[n]) for n in SMALL]
    for n, shape in shapes:
        wide = shape[:-1] + (_lane_padded(shape[-1]),)
        rows = int(np.prod(wide)) // LANES
        out[n] = slab[r:r + rows].reshape(wide)[..., :shape[-1]]
        r += rows
    return out


def _chip_exchange(src, name):
    rows = src.shape[-2]

    def body(src_ref, out_ref, send_sems, recv_sems):
        x, y, c = lax.axis_index("x"), lax.axis_index("y"), lax.axis_index("c")
        me = 2 * x + y
        chips = [(1 - x, y), (x, 1 - y), (1 - x, 1 - y)]
        sends = []
        for k, (cx, cy) in enumerate(chips):
            cp = pltpu.make_async_remote_copy(
                src_ref=src_ref.at[2 * cx + cy], dst_ref=out_ref.at[me], send_sem=send_sems.at[k],
                recv_sem=recv_sems.at[k], device_id=(cx, cy, c), device_id_type=pl.DeviceIdType.MESH)
            cp.start()
            sends.append(cp)
        for k, (cx, cy) in enumerate(chips):
            pltpu.make_async_remote_copy(
                src_ref=src_ref.at[me], dst_ref=out_ref.at[2 * cx + cy], send_sem=send_sems.at[k],
                recv_sem=recv_sems.at[k], device_id=(cx, cy, c), device_id_type=pl.DeviceIdType.MESH).wait_recv()
        for cp in sends:
            cp.wait_send()

    return pl.pallas_call(
        body,
        name=name,
        in_specs=[pl.BlockSpec(memory_space=pl.ANY)],
        out_specs=pl.BlockSpec(memory_space=pl.ANY),
        out_shape=jax.ShapeDtypeStruct((N_CHIPS, rows, LANES), src.dtype),
        scratch_shapes=[pltpu.SemaphoreType.DMA((3,)), pltpu.SemaphoreType.DMA((3,))],
    )(src)


def _half_rows(c, half):
    return pl.ds(pl.multiple_of(c * half, SLAB_ROW_ALIGN // 2), half)


def _gather_weights(src, name):
    rows = src.shape[0]
    half = rows // 2

    def body(src_ref, out_ref, send_sems, recv_sems):
        x, y, c = lax.axis_index("x"), lax.axis_index("y"), lax.axis_index("c")
        me = 2 * x + y
        chips = [(1 - x, y), (x, 1 - y), (1 - x, 1 - y)]

        def copy(k, src_view, slab, part, to):
            return pltpu.make_async_remote_copy(
                src_ref=src_view, dst_ref=out_ref.at[slab, _half_rows(part, half), :], send_sem=send_sems.at[k],
                recv_sem=recv_sems.at[k], device_id=to, device_id_type=pl.DeviceIdType.MESH)

        sends = [copy(k, src_ref.at[_half_rows(c, half), :], me, c, (cx, cy, c)) for k, (cx, cy) in enumerate(chips)]
        for cp in sends:
            cp.start()
        for k, (cx, cy) in enumerate(chips):
            j = 2 * cx + cy
            landed = out_ref.at[j, _half_rows(c, half), :]
            copy(k, landed, j, c, (cx, cy, c)).wait_recv()
            fwd = copy(3 + k, landed, j, c, (x, y, 1 - c))
            fwd.start()
            sends.append(fwd)
        for k, (cx, cy) in enumerate(chips):
            j = 2 * cx + cy
            copy(3 + k, out_ref.at[j, _half_rows(1 - c, half), :], j, 1 - c, (x, y, 1 - c)).wait_recv()
        for cp in sends:
            cp.wait_send()

    return pl.pallas_call(
        body,
        name=name,
        in_specs=[pl.BlockSpec(memory_space=pl.ANY)],
        out_specs=pl.BlockSpec(memory_space=pl.ANY),
        out_shape=jax.ShapeDtypeStruct((N_CHIPS, rows, LANES), src.dtype),
        scratch_shapes=[pltpu.SemaphoreType.DMA((6,)), pltpu.SemaphoreType.DMA((6,))],
    )(src)


def _sibling_halves(src, name):
    n, rows, _ = src.shape
    half = rows // 2

    def body(src_ref, out_ref, send_sem, recv_sem):
        x, y, c = lax.axis_index("x"), lax.axis_index("y"), lax.axis_index("c")
        cp = pltpu.make_async_remote_copy(
            src_ref=src_ref.at[:, _half_rows(1 - c, half), :], dst_ref=out_ref, send_sem=send_sem, recv_sem=recv_sem,
            device_id=(x, y, 1 - c), device_id_type=pl.DeviceIdType.MESH)
        cp.start()
        cp.wait()

    return pl.pallas_call(
        body,
        name=name,
        in_specs=[pl.BlockSpec(memory_space=pl.ANY)],
        out_specs=pl.BlockSpec(memory_space=pl.ANY),
        out_shape=jax.ShapeDtypeStruct((n, half, LANES), src.dtype),
        scratch_shapes=[pltpu.SemaphoreType.DMA, pltpu.SemaphoreType.DMA],
    )(src)


def _sibling_join(src, name):
    half = src.shape[0]

    def body(src_ref, out_ref, send_sem, recv_sem):
        x, y, c = lax.axis_index("x"), lax.axis_index("y"), lax.axis_index("c")
        cp = pltpu.make_async_remote_copy(
            src_ref=src_ref, dst_ref=out_ref.at[_half_rows(c, half), :], send_sem=send_sem, recv_sem=recv_sem,
            device_id=(x, y, 1 - c), device_id_type=pl.DeviceIdType.MESH)
        cp.start()
        pltpu.make_async_remote_copy(
            src_ref=src_ref, dst_ref=out_ref.at[_half_rows(1 - c, half), :], send_sem=send_sem, recv_sem=recv_sem,
            device_id=(x, y, 1 - c), device_id_type=pl.DeviceIdType.MESH).wait_recv()
        cp.wait_send()

    return pl.pallas_call(
        body,
        name=name,
        in_specs=[pl.BlockSpec(memory_space=pl.ANY)],
        out_specs=pl.BlockSpec(memory_space=pl.ANY),
        out_shape=jax.ShapeDtypeStruct((2 * half, LANES), src.dtype),
        scratch_shapes=[pltpu.SemaphoreType.DMA, pltpu.SemaphoreType.DMA],
    )(src)


SUM_ROWS = 1024


def _pair_sum(mine, theirs, c, name):
    n, half, _ = theirs.shape
    blocks = half // SUM_ROWS

    def body(c_ref, a_ref, b_ref, o_ref):
        o_ref[...] = (a_ref[...].astype(F32) + b_ref[...].astype(F32)).astype(o_ref.dtype)

    return pl.pallas_call(
        body,
        name=name,
        grid_spec=pltpu.PrefetchScalarGridSpec(
            num_scalar_prefetch=1,
            grid=(blocks,),
            in_specs=[pl.BlockSpec((n, SUM_ROWS, LANES), lambda i, c_ref: (0, c_ref[0] * blocks + i, 0)),
                      pl.BlockSpec((n, SUM_ROWS, LANES), lambda i, c_ref: (0, i, 0))],
            out_specs=pl.BlockSpec((n, SUM_ROWS, LANES), lambda i, c_ref: (0, i, 0)),
        ),
        out_shape=jax.ShapeDtypeStruct((n, half, LANES), BF16),
        compiler_params=_params(("arbitrary",)),
    )(jnp.reshape(c, (1,)).astype(jnp.int32), mine, theirs)


def _sum_chips(own, landed, me, name):
    rows = landed.shape[1]

    def body(me_ref, a_ref, b_ref, o_ref):
        t = [jnp.where(me_ref[0] == j, a_ref[j], b_ref[j]).astype(F32) for j in range(N_CHIPS)]
        o_ref[...] = ((t[0] + t[1]) + t[2]) + t[3]

    slabs = pl.BlockSpec((N_CHIPS, SUM_ROWS, LANES), lambda i, me_ref: (0, i, 0))
    return pl.pallas_call(
        body,
        name=name,
        grid_spec=pltpu.PrefetchScalarGridSpec(
            num_scalar_prefetch=1,
            grid=(rows // SUM_ROWS,),
            in_specs=[slabs, slabs],
            out_specs=pl.BlockSpec((SUM_ROWS, LANES), lambda i, me_ref: (i, 0)),
        ),
        out_shape=jax.ShapeDtypeStruct((rows, LANES), F32),
        compiler_params=_params(("arbitrary",)),
    )(jnp.reshape(me, (1,)).astype(jnp.int32), own, landed)


def _adamw(w, m, v, g, name):
    shape = w.shape
    flat = lambda a: a.reshape(-1, shape[-1])

    def fn(rows, consts):
        w_, m_, v_, g_ = rows
        m_new = ADAM_B1 * m_ + (1.0 - ADAM_B1) * g_
        v_new = ADAM_B2 * v_ + (1.0 - ADAM_B2) * (g_ * g_)
        m_hat = m_new / (1.0 - ADAM_B1 ** ADAM_STEP)
        v_hat = v_new / (1.0 - ADAM_B2 ** ADAM_STEP)
        delta = -ADAM_LR * (m_hat / (jnp.sqrt(v_hat) + ADAM_EPS) + ADAM_WD * w_)
        return [delta, m_new, v_new], []

    outs = _rowwise(fn, [flat(w), flat(m), flat(v), flat(g)], [], [(shape[-1], F32)] * 3, [], name)
    return [o.reshape(shape) for o in outs]


def kernel(x, positions, g_mix_pre, w_in, b_gate, g_q_lat, g_kv_lat, w_uq, w_ukv, swa_sinks, w_o_mla, w_o_swa, w_o_sb, w_out, g_mix_post, g_mlp_pre, w_up, w_down, g_mlp_post, loss_target, m_g_mix_pre, m_w_in, m_b_gate, m_g_q_lat, m_g_kv_lat, m_w_uq, m_w_ukv, m_swa_sinks, m_w_o_mla, m_w_o_swa, m_w_o_sb, m_w_out, m_g_mix_post, m_g_mlp_pre, m_w_up, m_w_down, m_g_mlp_post, v_g_mix_pre, v_w_in, v_b_gate, v_g_q_lat, v_g_kv_lat, v_w_uq, v_w_ukv, v_swa_sinks, v_w_o_mla, v_w_o_swa, v_w_o_sb, v_w_out, v_g_mix_post, v_g_mlp_pre, v_w_up, v_w_down, v_g_mlp_post):
    given = dict(locals())
    wts = {n: given[n] for n in WEIGHTS}
    mom_m = {n: given["m_" + n] for n in WEIGHTS}
    mom_v = {n: given["v_" + n] for n in WEIGHTS}
    shard_shapes = {n: wts[n].shape for n in SHARDED}
    small_shapes = {n: wts[n].shape for n in SMALL}

    me = 2 * lax.axis_index("x") + lax.axis_index("y")
    core = lax.axis_index("c")
    gathered = _gather_weights(_pack(wts, None, BF16), "gather_weights")
    full = {n: wts[n] for n in SMALL}
    per_chip = [_unpack(gathered[j], shard_shapes, None) for j in range(N_CHIPS)]
    for n in SHARDED:
        own = wts[n].astype(BF16)
        full[n] = jnp.concatenate([jnp.where(me == j, own, per_chip[j][n]) for j in range(N_CHIPS)], axis=SHARD_AXIS[n])

    loss_part, grad_x, grads = _local_step(x[0], positions[0], loss_target[0], full)
    loss = lax.psum(loss_part[0, 0], ("x", "y", "c"))

    small_g = {n: grads[n] for n in SMALL}
    slabs = []
    for j in range(N_CHIPS):
        shard = {n: jnp.split(grads[n], N_CHIPS, axis=SHARD_AXIS[n])[j] for n in SHARDED}
        slabs.append(_pack(shard, small_g, BF16))
    per_chip_g = jnp.stack(slabs)
    theirs = _sibling_halves(per_chip_g, "pair_grads")
    pair = _pair_sum(per_chip_g, theirs, core, "sum_pair")
    landed = _chip_exchange(pair, "scatter_grads")
    my_half = _sum_chips(pair, landed, me, "sum_chips")
    g_slab = lax.dynamic_update_slice(_sibling_join(my_half, "join_grads"), my_half, (core * my_half.shape[0], 0))

    g = _unpack(g_slab, shard_shapes, small_shapes)
    stepped = {n: _adamw(wts[n], mom_m[n], mom_v[n], g[n], "adamw_" + n) for n in WEIGHTS}
    outs = [loss, grad_x[None]] + [g[n] for n in WEIGHTS]
    for part in range(3):
        outs += [stepped[n][part] for n in WEIGHTS]
    return tuple(outs)
```
